```python
import math
import jax, jax.numpy as jnp
from jax import lax
import numpy as np

D_MODEL = 1024
BATCH = 16
SEQ = 4096
DEPTH = 1
DEC_BATCH = 8
DEC_SEQ = 32
PAST_LEN = 2048

CHUNK = 64
SSM_WIDTH = D_MODEL // 2
SSM_GROUP_CH = 16
SSM_GROUPS = SSM_WIDTH // SSM_GROUP_CH
SSM_STATE = 64
DT_MIN = 0.001
DT_MAX = 0.1
N_HEADS = 8
HEAD_DIM = 64
N_KV_HEADS = 2
KV_REP = N_HEADS // N_KV_HEADS
ATTN_WIDTH = N_HEADS * HEAD_DIM
IDX_HEADS = 8
IDX_DIM = 64
TOPK_MAX = 256
Q_BLOCK = 128
REL_BUCKETS = 32
REL_MAX_DIST = 1024
N_EXPERTS = 32
TOP_K = 4
D_FF = D_MODEL
SWIGLU_LIMIT = 7.0
SWIGLU_ALPHA = 1.702
MOE_BLOCK = 128
EPS = 1e-6
N_IN = SSM_WIDTH + ATTN_WIDTH + 2 * N_KV_HEADS * HEAD_DIM + IDX_HEADS * IDX_DIM + IDX_DIM + IDX_HEADS + 2 * D_MODEL

kernel_name = 'hybrid_s5_dsa_moe_stream_step'


def rmsnorm(x, g):
    xf = x.astype(jnp.float32)
    return (xf * lax.rsqrt(jnp.mean(xf * xf, axis=-1, keepdims=True) + EPS) * g.astype(jnp.float32)).astype(x.dtype)


def layernorm(x, g, b):
    xf = x.astype(jnp.float32)
    mu = jnp.mean(xf, axis=-1, keepdims=True)
    xc = xf - mu
    var = jnp.mean(xc * xc, axis=-1, keepdims=True)
    return (xc * lax.rsqrt(var + EPS) * g.astype(jnp.float32) + b.astype(jnp.float32)).astype(x.dtype)


def rel_bucket(rel):
    half = REL_BUCKETS // 2
    max_exact = half // 2
    n = jnp.abs(rel)
    large = max_exact + (jnp.log(jnp.maximum(n, 1).astype(jnp.float32) / max_exact)
                         / math.log(REL_MAX_DIST / max_exact) * (half - max_exact)).astype(jnp.int32)
    large = jnp.minimum(large, half - 1)
    return jnp.where(rel > 0, half, 0) + jnp.where(n < max_exact, n, large)


def _split_sizes():
    kv = N_KV_HEADS * HEAD_DIM
    return [SSM_WIDTH, ATTN_WIDTH, kv, kv, IDX_HEADS * IDX_DIM, IDX_DIM, IDX_HEADS, D_MODEL, D_MODEL]


def _s5_combine(left, right):
    a_l, b_l = left
    a_r, b_r = right
    return a_r * a_l, a_r * b_l + b_r


def s5_branch(u, h0_re, h0_im, p):
    bsz, seq, _ = u.shape
    uf = u.astype(jnp.float32).reshape(bsz, seq, SSM_GROUPS, SSM_GROUP_CH)
    lam = lax.complex(p['ssm_lambda_re'].astype(jnp.float32), p['ssm_lambda_im'].astype(jnp.float32))
    dt = jnp.exp(p['ssm_log_dt'].astype(jnp.float32))[:, None]
    a_bar = jnp.exp(lam * dt)
    b_cplx = lax.complex(p['ssm_b_re'].astype(jnp.float32), p['ssm_b_im'].astype(jnp.float32))
    b_bar = ((a_bar - 1.0) / lam)[:, :, None] * b_cplx
    bu = lax.complex(jnp.einsum('blgc,gpc->blgp', uf, jnp.real(b_bar)),
                     jnp.einsum('blgc,gpc->blgp', uf, jnp.imag(b_bar)))
    if h0_re is not None:
        h0 = lax.complex(h0_re.astype(jnp.float32), h0_im.astype(jnp.float32))
        bu = bu.at[:, 0].add(a_bar * h0)
    a_seq = jnp.broadcast_to(a_bar, (1, seq) + a_bar.shape)
    _, states = lax.associative_scan(_s5_combine, (a_seq, bu), axis=1)
    y = (jnp.einsum('blgp,gcp->blgc', jnp.real(states), p['ssm_c_re'].astype(jnp.float32))
         - jnp.einsum('blgp,gcp->blgc', jnp.imag(states), p['ssm_c_im'].astype(jnp.float32)))
    y = y.reshape(bsz, seq, SSM_WIDTH) + p['ssm_d'].astype(jnp.float32) * u.astype(jnp.float32)
    g = jax.nn.gelu(y).astype(u.dtype)
    y_a = (g @ p['ssm_w_glu_a']) * jax.nn.sigmoid(g @ p['ssm_w_glu_b'])
    last = states[:, -1]
    return y_a, jnp.real(last).astype(u.dtype), jnp.imag(last).astype(u.dtype)


def dsa_branch(q, k, v, qi, ki, wi, past_k, past_v, past_ik, p, rel_bias):
    bsz, seq = q.shape[0], q.shape[1]
    q = rmsnorm(q, p['q_norm_g'])
    k = rmsnorm(k, p['k_norm_g'])
    ki = layernorm(ki, p['idx_k_norm_g'], p['idx_k_norm_b'])
    if past_k is None:
        k_all, v_all, ki_all = k, v, ki
    else:
        k_all = jnp.concatenate([past_k, k], axis=1)
        v_all = jnp.concatenate([past_v, v], axis=1)
        ki_all = jnp.concatenate([past_ik, ki], axis=1)
    n_keys = k_all.shape[1]
    past = n_keys - seq
    topk = min(TOPK_MAX, n_keys // 4)
    q_pos = past + jnp.arange(seq, dtype=jnp.int32)
    key_chunk = jnp.arange(n_keys, dtype=jnp.int32) // CHUNK
    ki_f = ki_all.astype(jnp.float32)
    w_eff = wi.astype(jnp.float32) * (IDX_HEADS ** -0.5 * IDX_DIM ** -0.5)
    qg = q.reshape(bsz, seq, N_KV_HEADS, KV_REP, HEAD_DIM)
    bias_tab = rel_bias.astype(jnp.float32)
    scale = HEAD_DIM ** -0.5

    def attend(args):
        qb, qib, wb, qp = args
        nq = qp.shape[0]
        s = jnp.einsum('bqhd,bkd->bqhk', qib.astype(jnp.float32), ki_f)
        score = jnp.einsum('bqh,bqhk->bqk', wb, jax.nn.relu(s))
        admissible = key_chunk[None, :] <= (qp // CHUNK)[:, None]
        score = jnp.where(admissible[None], score, -jnp.inf)
        _, sel = lax.top_k(score, topk)
        valid = (sel // CHUNK) <= (qp // CHUNK)[None, :, None]
        kg = jax.vmap(lambda kk, ii: kk[ii])(k_all, sel)
        vg = jax.vmap(lambda vv, ii: vv[ii])(v_all, sel)
        logits = jnp.einsum('bqgrd,bqkgd->bqgrk', qb, kg).astype(jnp.float32) * scale
        bias = bias_tab[rel_bucket(sel - qp[None, :, None])]
        bias = bias.reshape(bsz, nq, topk, N_KV_HEADS, KV_REP).transpose(0, 1, 3, 4, 2)
        logits = jnp.where(valid[:, :, None, None, :], logits + bias, -jnp.inf)
        probs = jax.nn.softmax(logits, axis=-1).astype(vg.dtype)
        return jnp.einsum('bqgrk,bqkgd->bqgrd', probs, vg)

    if seq > Q_BLOCK and seq % Q_BLOCK == 0:
        nb = seq // Q_BLOCK

        def to_blocks(t):
            return t.reshape((bsz, nb, Q_BLOCK) + t.shape[2:]).swapaxes(0, 1)

        out = lax.map(attend, (to_blocks(qg), to_blocks(qi), to_blocks(w_eff), q_pos.reshape(nb, Q_BLOCK)))
        out = out.swapaxes(0, 1).reshape(bsz, seq, ATTN_WIDTH)
    else:
        out = attend((qg, qi, w_eff, q_pos)).reshape(bsz, seq, ATTN_WIDTH)
    return out @ p['w_attn_up'], k, v, ki


def moe_ffn(h, p):
    bsz, seq, dm = h.shape
    xt = h.reshape(bsz * seq, dm)
    n_tok = xt.shape[0]
    n_assign = n_tok * TOP_K
    logits = (xt @ p['moe_w_router'] + p['moe_b_router']).astype(jnp.float32)
    top_logit, top_e = lax.top_k(logits, TOP_K)
    gate = jax.nn.softmax(top_logit, axis=-1)
    e_flat = top_e.reshape(-1)
    tok_flat = jnp.repeat(jnp.arange(n_tok, dtype=jnp.int32), TOP_K)
    order = jnp.argsort(e_flat)
    e_sorted = e_flat[order]
    counts = jnp.zeros((N_EXPERTS,), jnp.int32).at[e_flat].add(1)
    padded = (counts + MOE_BLOCK - 1) // MOE_BLOCK * MOE_BLOCK
    pad_end = jnp.cumsum(padded)
    pad_start = pad_end - padded
    start = jnp.cumsum(counts) - counts
    dest = pad_start[e_sorted] + jnp.arange(n_assign, dtype=jnp.int32) - start[e_sorted]
    n_blocks = -(-(n_assign + N_EXPERTS * (MOE_BLOCK - 1)) // MOE_BLOCK)
    n_rows = n_blocks * MOE_BLOCK
    row_tok = jnp.full((n_rows,), n_tok, jnp.int32).at[dest].set(tok_flat[order])
    row_gate = jnp.zeros((n_rows,), jnp.float32).at[dest].set(gate.reshape(-1)[order]).astype(h.dtype)
    block_expert = jnp.minimum(
        jnp.searchsorted(pad_end, jnp.arange(n_blocks, dtype=jnp.int32) * MOE_BLOCK, side='right'), N_EXPERTS - 1)
    x_pad = jnp.concatenate([xt, jnp.zeros((1, dm), xt.dtype)], axis=0)

    def expert_block(args):
        tok_b, gate_b, e = args
        xb = x_pad[tok_b]
        a = jnp.minimum(xb @ p['moe_w_gate'][e] + p['moe_b_gate'][e], SWIGLU_LIMIT)
        b = jnp.clip(xb @ p['moe_w_up'][e] + p['moe_b_up'][e], -SWIGLU_LIMIT, SWIGLU_LIMIT)
        hid = a * jax.nn.sigmoid(SWIGLU_ALPHA * a) * (b + 1.0)
        return (hid @ p['moe_w_down'][e] + p['moe_b_down'][e]) * gate_b[:, None]

    out = lax.map(expert_block, (row_tok.reshape(n_blocks, MOE_BLOCK), row_gate.reshape(n_blocks, MOE_BLOCK), block_expert))
    y = jax.ops.segment_sum(out.reshape(n_rows, dm), row_tok, num_segments=n_tok + 1)[:n_tok]
    return y.reshape(bsz, seq, dm)


def trunk_layer(x, past_k, past_v, past_ik, h0_re, h0_im, p, rel_bias):
    bsz, seq, _ = x.shape
    h = rmsnorm(x, p['norm1_g'])
    z = h @ p['w_in']
    points = np.cumsum(_split_sizes())[:-1].tolist()
    u, q, k, v, qi, ki, wi, ga, gb = jnp.split(z, points, axis=-1)
    y_a, s_re, s_im = s5_branch(u, h0_re, h0_im, p)
    y_b, k_new, v_new, ik_new = dsa_branch(
        q.reshape(bsz, seq, N_HEADS, HEAD_DIM), k.reshape(bsz, seq, N_KV_HEADS, HEAD_DIM),
        v.reshape(bsz, seq, N_KV_HEADS, HEAD_DIM), qi.reshape(bsz, seq, IDX_HEADS, IDX_DIM), ki, wi,
        past_k, past_v, past_ik, p, rel_bias)
    merged = jax.nn.sigmoid(ga) * y_a + jax.nn.sigmoid(gb) * y_b
    x = x + merged @ p['w_out']
    x = x + moe_ffn(rmsnorm(x, p['norm2_g']), p)
    return x, k_new, v_new, ik_new, s_re, s_im


def setup_inputs(seed: int = 0) -> dict:
    key = jax.random.key(seed)
    ks = jax.random.split(key, 35)
    f32 = jnp.float32

    def nrm(i, shape, scale):
        return jax.random.normal(ks[i], shape, f32) * scale

    L_ = DEPTH
    G, P, C = SSM_GROUPS, SSM_STATE, SSM_GROUP_CH
    lam_im = math.pi * jnp.arange(P, dtype=f32)[None, None, :] + nrm(11, (L_, G, P), 0.01)
    return {
        'x_prompt': nrm(0, (BATCH, SEQ, D_MODEL), 1.0),
        'x_sample': nrm(1, (DEC_BATCH, DEC_SEQ, D_MODEL), 1.0),
        'cache_k': nrm(2, (L_, DEC_BATCH, PAST_LEN, N_KV_HEADS, HEAD_DIM), 1.0),
        'cache_v': nrm(3, (L_, DEC_BATCH, PAST_LEN, N_KV_HEADS, HEAD_DIM), 1.0),
        'cache_idx_k': nrm(4, (L_, DEC_BATCH, PAST_LEN, IDX_DIM), 1.0),
        'state_ssm_re': nrm(5, (L_, DEC_BATCH, G, P), 0.5),
        'state_ssm_im': nrm(6, (L_, DEC_BATCH, G, P), 0.5),
        'rel_bias': nrm(7, (REL_BUCKETS, N_HEADS), 0.5),
        'norm1_g': 1.0 + nrm(8, (L_, D_MODEL), 0.02),
        'w_in': nrm(9, (L_, D_MODEL, N_IN), D_MODEL ** -0.5),
        'ssm_lambda_re': -0.5 + nrm(10, (L_, G, P), 0.01),
        'ssm_lambda_im': lam_im,
        'ssm_log_dt': jax.random.uniform(ks[12], (L_, G), f32, math.log(DT_MIN), math.log(DT_MAX)),
        'ssm_b_re': nrm(13, (L_, G, P, C), (2 * C) ** -0.5),
        'ssm_b_im': nrm(14, (L_, G, P, C), (2 * C) ** -0.5),
        'ssm_c_re': nrm(15, (L_, G, C, P), (2 * P) ** -0.5),
        'ssm_c_im': nrm(16, (L_, G, C, P), (2 * P) ** -0.5),
        'ssm_d': nrm(17, (L_, SSM_WIDTH), 1.0),
        'ssm_w_glu_a': nrm(18, (L_, SSM_WIDTH, D_MODEL), SSM_WIDTH ** -0.5),
        'ssm_w_glu_b': nrm(19, (L_, SSM_WIDTH, D_MODEL), SSM_WIDTH ** -0.5),
        'q_norm_g': 1.0 + nrm(20, (L_, HEAD_DIM), 0.02),
        'k_norm_g': 1.0 + nrm(21, (L_, HEAD_DIM), 0.02),
        'idx_k_norm_g': 1.0 + nrm(22, (L_, IDX_DIM), 0.02),
        'idx_k_norm_b': nrm(23, (L_, IDX_DIM), 0.02),
        'w_attn_up': nrm(24, (L_, ATTN_WIDTH, D_MODEL), ATTN_WIDTH ** -0.5),
        'w_out': nrm(25, (L_, D_MODEL, D_MODEL), D_MODEL ** -0.5),
        'norm2_g': 1.0 + nrm(26, (L_, D_MODEL), 0.02),
        'moe_w_router': nrm(27, (L_, D_MODEL, N_EXPERTS), D_MODEL ** -0.5),
        'moe_b_router': nrm(28, (L_, N_EXPERTS), 0.01),
        'moe_w_gate': nrm(29, (L_, N_EXPERTS, D_MODEL, D_FF), D_MODEL ** -0.5),
        'moe_b_gate': nrm(30, (L_, N_EXPERTS, D_FF), 0.01),
        'moe_w_up': nrm(31, (L_, N_EXPERTS, D_MODEL, D_FF), D_MODEL ** -0.5),
        'moe_b_up': nrm(32, (L_, N_EXPERTS, D_FF), 0.01),
        'moe_w_down': nrm(33, (L_, N_EXPERTS, D_FF, D_MODEL), D_FF ** -0.5),
        'moe_b_down': nrm(34, (L_, N_EXPERTS, D_MODEL), 0.01),
    }


def reference(x_prompt, x_sample, cache_k, cache_v, cache_idx_k, state_ssm_re, state_ssm_im, rel_bias,
              norm1_g, w_in, ssm_lambda_re, ssm_lambda_im, ssm_log_dt, ssm_b_re, ssm_b_im, ssm_c_re, ssm_c_im,
              ssm_d, ssm_w_glu_a, ssm_w_glu_b, q_norm_g, k_norm_g, idx_k_norm_g, idx_k_norm_b, w_attn_up, w_out,
              norm2_g, moe_w_router, moe_b_router, moe_w_gate, moe_b_gate, moe_w_up, moe_b_up, moe_w_down, moe_b_down):
    layer_weights = dict(
        norm1_g=norm1_g, w_in=w_in, ssm_lambda_re=ssm_lambda_re, ssm_lambda_im=ssm_lambda_im, ssm_log_dt=ssm_log_dt,
        ssm_b_re=ssm_b_re, ssm_b_im=ssm_b_im, ssm_c_re=ssm_c_re, ssm_c_im=ssm_c_im, ssm_d=ssm_d,
        ssm_w_glu_a=ssm_w_glu_a, ssm_w_glu_b=ssm_w_glu_b, q_norm_g=q_norm_g, k_norm_g=k_norm_g,
        idx_k_norm_g=idx_k_norm_g, idx_k_norm_b=idx_k_norm_b, w_attn_up=w_attn_up, w_out=w_out, norm2_g=norm2_g,
        moe_w_router=moe_w_router, moe_b_router=moe_b_router, moe_w_gate=moe_w_gate, moe_b_gate=moe_b_gate,
        moe_w_up=moe_w_up, moe_b_up=moe_b_up, moe_w_down=moe_w_down, moe_b_down=moe_b_down)
    xp, xs = x_prompt, x_sample
    st_p, st_s = [], []
    for l in range(DEPTH):
        pl = {name: w[l] for name, w in layer_weights.items()}
        xp, kp, vp, ikp, srp, sip = trunk_layer(xp, None, None, None, None, None, pl, rel_bias)
        xs, ks_, vs_, iks, srs, sis = trunk_layer(xs, cache_k[l], cache_v[l], cache_idx_k[l],
                                                  state_ssm_re[l], state_ssm_im[l], pl, rel_bias)
        st_p.append((kp, vp, ikp, srp, sip))
        st_s.append((ks_, vs_, iks, srs, sis))
    k_prompt = jnp.stack([s[0] for s in st_p])
    v_prompt = jnp.stack([s[1] for s in st_p])
    idx_k_prompt = jnp.stack([s[2] for s in st_p])
    ssm_re_prompt = jnp.stack([s[3] for s in st_p])
    ssm_im_prompt = jnp.stack([s[4] for s in st_p])
    k_sample = jnp.stack([s[0] for s in st_s])
    v_sample = jnp.stack([s[1] for s in st_s])
    idx_k_sample = jnp.stack([s[2] for s in st_s])
    ssm_re_sample = jnp.stack([s[3] for s in st_s])
    ssm_im_sample = jnp.stack([s[4] for s in st_s])
    return (xp, xs, k_prompt, v_prompt, idx_k_prompt, ssm_re_prompt, ssm_im_prompt,
            k_sample, v_sample, idx_k_sample, ssm_re_sample, ssm_im_sample)
```

```python
import functools
import math

import numpy as np
import jax
import jax.numpy as jnp
from jax import lax
from jax.experimental import pallas as pl
from jax.experimental.pallas import tpu as pltpu

F32 = jnp.float32
BF16 = jnp.bfloat16
I32 = jnp.int32

LANES = 128
VMEM_LIMIT = 56 * 1024 * 1024

CHUNK = 64
SSM_GROUP_CH = 16
SSM_STATE = 64
N_HEADS = 8
HEAD_DIM = 64
N_KV_HEADS = 2
KV_REP = N_HEADS // N_KV_HEADS
IDX_HEADS = 8
IDX_DIM = 64
TOPK_MAX = 256
REL_BUCKETS = 32
REL_MAX_DIST = 1024
N_EXPERTS = 32
TOP_K = 4
SWIGLU_LIMIT = 7.0
SWIGLU_ALPHA = 1.702
EPS = 1e-6

KEY_TILE = 128
NEG_BIG = -1e30
MOE_ROWS = 128


def _cp(sem):
    return pltpu.CompilerParams(dimension_semantics=sem, vmem_limit_bytes=VMEM_LIMIT)


def _dot(a, b):
    return jnp.dot(a, b, preferred_element_type=F32)


def _dot_nt(a, b):
    return lax.dot_general(a, b, (((1,), (1,)), ((), ())), preferred_element_type=F32)


def _dot_tn(a, b):
    return lax.dot_general(a, b, (((0,), (0,)), ((), ())), preferred_element_type=F32)


def _split(a):
    hi = a.astype(BF16)
    lo = (a - hi.astype(F32)).astype(BF16)
    return hi, lo


def _dot_split(a, g):
    hi, lo = _split(a)
    return _dot(hi, g) + _dot(lo, g)


def _proj_kernel(x_ref, g1_ref, wu_ref, wq_ref, wk_ref, wv_ref, wqi_ref, wki_ref, wwi_ref, wga_ref, wgb_ref,
                 wkc_ref, wvc_ref, gq_ref, gk_ref, gkc_ref, gi_ref, bi_ref, ones_h_ref, ones_c_ref,
                 u_ref, q_ref, kp_ref, vp_ref, qi_ref, kip_ref, w_ref, sga_ref, sgb_ref,
                 kc_ref, vc_ref, kic_ref):
    x = x_ref[...]
    ms = jnp.mean(x * x, axis=-1, keepdims=True)
    hn = (x * lax.rsqrt(ms + EPS) * g1_ref[...]).astype(BF16)
    ones_h = ones_h_ref[...]
    lane = lax.broadcasted_iota(I32, (x.shape[0], LANES), 1)

    u_ref[...] = _dot(hn, wu_ref[...]).astype(BF16)

    q = _dot(hn, wq_ref[...])
    scale = HEAD_DIM ** -0.5
    for h in range(N_HEADS):
        qh = q[:, h * LANES:(h + 1) * LANES]
        msq = _dot_split(qh * qh, ones_h)
        q_ref[h] = (qh * lax.rsqrt(msq + EPS) * (gq_ref[...] * scale)).astype(BF16)

    k = _dot(hn, wk_ref[...])
    for g in range(N_KV_HEADS):
        kg = k[:, g * LANES:(g + 1) * LANES]
        msk = _dot_split(kg * kg, ones_h)
        kp_ref[g] = (kg * lax.rsqrt(msk + EPS) * gk_ref[...]).astype(BF16)

    v = _dot(hn, wv_ref[...])
    for g in range(N_KV_HEADS):
        vg = v[:, g * LANES:(g + 1) * LANES]
        vp_ref[g] = jnp.where(lane == HEAD_DIM, 1.0, vg).astype(BF16)

    qi = _dot(hn, wqi_ref[...])
    for h in range(IDX_HEADS):
        qi_ref[h] = qi[:, h * LANES:(h + 1) * LANES].astype(BF16)

    ki = _dot(hn, wki_ref[...])
    mu = _dot_split(ki, ones_h)
    xc = jnp.where(lane < IDX_DIM, ki - mu, 0.0)
    var = _dot_split(xc * xc, ones_h)
    kin = xc * lax.rsqrt(var + EPS) * gi_ref[...] + bi_ref[...]
    kip_ref[...] = kin.astype(BF16)
    kic_ref[...] = kin[:, :IDX_DIM]

    w_ref[...] = _dot(hn, wwi_ref[...]) * (IDX_HEADS ** -0.5 * IDX_DIM ** -0.5)

    sga_ref[...] = jax.nn.sigmoid(_dot(hn, wga_ref[...])).astype(BF16)
    sgb_ref[...] = jax.nn.sigmoid(_dot(hn, wgb_ref[...])).astype(BF16)

    kc = _dot(hn, wkc_ref[...])
    mskc = _dot_split(kc * kc, ones_c_ref[...])
    kc_ref[...] = kc * lax.rsqrt(mskc + EPS) * gkc_ref[...]
    vc_ref[...] = _dot(hn, wvc_ref[...])


def _proj(x, pw, bsz, seq, tm):
    d = x.shape[-1]
    nt = seq // tm
    t = bsz * seq
    x2 = x.reshape(t, d)

    def tok(b, i):
        return (b * nt + i, 0)

    def cst(b, i):
        return (0, 0)

    def wspec(a):
        return pl.BlockSpec(a.shape, cst)

    weights = [pw['g1'], pw['wu'], pw['wq'], pw['wk'], pw['wv'], pw['wqi'], pw['wki'], pw['wwi'], pw['wga'],
               pw['wgb'], pw['wkc'], pw['wvc'], pw['gq'], pw['gk'], pw['gkc'], pw['gi'], pw['bi'],
               pw['ones_h'], pw['ones_c']]
    ssm_w = pw['wu'].shape[1]
    out_shape = (
        jax.ShapeDtypeStruct((seq, bsz * ssm_w), BF16),
        jax.ShapeDtypeStruct((bsz, N_HEADS, seq, LANES), BF16),
        jax.ShapeDtypeStruct((bsz, N_KV_HEADS, seq, LANES), BF16),
        jax.ShapeDtypeStruct((bsz, N_KV_HEADS, seq, LANES), BF16),
        jax.ShapeDtypeStruct((bsz, IDX_HEADS, seq, LANES), BF16),
        jax.ShapeDtypeStruct((t, LANES), BF16),
        jax.ShapeDtypeStruct((t, LANES), F32),
        jax.ShapeDtypeStruct((t, d), BF16),
        jax.ShapeDtypeStruct((t, d), BF16),
        jax.ShapeDtypeStruct((t, N_KV_HEADS * HEAD_DIM), F32),
        jax.ShapeDtypeStruct((t, N_KV_HEADS * HEAD_DIM), F32),
        jax.ShapeDtypeStruct((t, IDX_DIM), F32),
    )

    def hm(nh):
        return pl.BlockSpec((None, nh, tm, LANES), lambda b, i: (b, 0, i, 0))

    out_specs = (
        pl.BlockSpec((tm, ssm_w), lambda b, i: (i, b)),
        hm(N_HEADS), hm(N_KV_HEADS), hm(N_KV_HEADS), hm(IDX_HEADS),
        pl.BlockSpec((tm, LANES), tok), pl.BlockSpec((tm, LANES), tok),
        pl.BlockSpec((tm, d), tok), pl.BlockSpec((tm, d), tok),
        pl.BlockSpec((tm, N_KV_HEADS * HEAD_DIM), tok), pl.BlockSpec((tm, N_KV_HEADS * HEAD_DIM), tok),
        pl.BlockSpec((tm, IDX_DIM), tok),
    )
    return pl.pallas_call(
        _proj_kernel,
        grid=(bsz, nt),
        in_specs=[pl.BlockSpec((tm, d), tok)] + [wspec(a) for a in weights],
        out_specs=out_specs,
        out_shape=out_shape,
        compiler_params=_cp(("arbitrary", "arbitrary")),
        name="proj",
    )(x2, *weights)


def _gelu_tanh(x):
    return 0.5 * x * (1.0 + jnp.tanh(math.sqrt(2.0 / math.pi) * (x + 0.044715 * (x * x * x))))


def _s5_kernel(u_ref, h0_ref, bmat_ref, are_ref, aim_ref, cmat_ref, dvec_ref, wa_ref, wb_ref,
               ya_ref, hout_ref, state_ref, bu_ref, *, bsz, tc, strip):
    s = pl.program_id(0)
    half = are_ref.shape[1]

    @pl.when(s == 0)
    def _():
        state_ref[...] = h0_ref[...]

    u = u_ref[...]
    bu_ref[...] = _dot(u, bmat_ref[...])

    for c0 in range(0, half, strip):
        ar = jnp.broadcast_to(are_ref[:, c0:c0 + strip], (bsz, strip))
        ai = jnp.broadcast_to(aim_ref[:, c0:c0 + strip], (bsz, strip))
        hr0 = state_ref[:, c0:c0 + strip]
        hi0 = state_ref[:, half + c0:half + c0 + strip]

        def step(t, carry):
            hr, hi = carry
            r0 = pl.multiple_of(t * bsz, bsz)
            br = bu_ref[pl.ds(r0, bsz), c0:c0 + strip]
            bi = bu_ref[pl.ds(r0, bsz), half + c0:half + c0 + strip]
            nr = ar * hr - ai * hi + br
            ni = ar * hi + ai * hr + bi
            bu_ref[pl.ds(r0, bsz), c0:c0 + strip] = nr
            bu_ref[pl.ds(r0, bsz), half + c0:half + c0 + strip] = ni
            return nr, ni

        hr, hi = lax.fori_loop(0, tc, step, (hr0, hi0))
        state_ref[:, c0:c0 + strip] = hr
        state_ref[:, half + c0:half + c0 + strip] = hi

    y = _dot(bu_ref[...].astype(BF16), cmat_ref[...]) + dvec_ref[...] * u.astype(F32)
    g = _gelu_tanh(y).astype(BF16)
    ya = _dot(g, wa_ref[...]) * jax.nn.sigmoid(_dot(g, wb_ref[...]))
    ya_ref[...] = ya.astype(BF16)

    @pl.when(s == pl.num_programs(0) - 1)
    def _():
        hout_ref[...] = state_ref[...]


def _s5(u_tb, h0, sw, bsz, seq, tc):
    rows = tc * bsz
    ssm_w = sw['bmat'].shape[0]
    two_half = sw['bmat'].shape[1]
    half = two_half // 2
    d = sw['wa'].shape[1]
    u2 = u_tb.reshape(seq * bsz, ssm_w)
    strip = min(512, half)

    def cst(s):
        return (0, 0)

    consts = [h0, sw['bmat'], sw['a_re'], sw['a_im'], sw['cmat'], sw['d'], sw['wa'], sw['wb']]
    ya, hout = pl.pallas_call(
        functools.partial(_s5_kernel, bsz=bsz, tc=tc, strip=strip),
        grid=(seq // tc,),
        in_specs=[pl.BlockSpec((rows, ssm_w), lambda s: (s, 0))] + [pl.BlockSpec(a.shape, cst) for a in consts],
        out_specs=(pl.BlockSpec((rows, d), lambda s: (s, 0)), pl.BlockSpec((bsz, two_half), cst)),
        out_shape=(jax.ShapeDtypeStruct((seq * bsz, d), BF16), jax.ShapeDtypeStruct((bsz, two_half), F32)),
        scratch_shapes=[pltpu.VMEM((bsz, two_half), F32), pltpu.VMEM((rows, two_half), F32)],
        compiler_params=_cp(("arbitrary",)),
        name="s5",
    )(u2, *consts)
    return ya.reshape(seq, bsz * d), hout


def _f2key(x):
    b = lax.bitcast_convert_type(x, I32)
    return b ^ ((b >> 31) & 0x7FFFFFFF)


def _key2f(k):
    return lax.bitcast_convert_type(k ^ ((k >> 31) & 0x7FFFFFFF), F32)


_KEY_LO = int(np.array(-np.finfo(np.float32).max, np.float32).view(np.int32)) ^ 0x7FFFFFFF
_KEY_HI = int(np.array(np.inf, np.float32).view(np.int32))
if _KEY_LO >= 2 ** 31:
    _KEY_LO -= 2 ** 32


def _dsa_kernel(q_ref, qi_ref, w_ref, k_ref, v_ref, ki_ref, bias_ref, o_ref,
                s_ref, wb_ref, thr_ref, lo_ref, hi_ref, acc_ref, m_ref,
                *, tq, past, n_keys, topk, nkt, nd, idx_bits):
    i = pl.program_id(1)
    tk = KEY_TILE
    q0 = past + i * tq
    last_chunk = (q0 + tq - 1) // CHUNK
    n_kt = jnp.minimum(nkt, ((last_chunk + 1) * CHUNK + tk - 1) // tk)
    d0 = q0 // tk

    row = lax.broadcasted_iota(I32, (tq, tk), 0)
    col = lax.broadcasted_iota(I32, (tq, tk), 1)
    q_chunk = (q0 + row) // CHUNK
    n_adm = jnp.minimum((q_chunk + 1) * CHUNK, n_keys)
    need = jnp.minimum(topk, n_adm)

    w = w_ref[...]
    for h in range(IDX_HEADS):
        wb_ref[h * tq:(h + 1) * tq, :] = jnp.broadcast_to(w[:, h:h + 1], (tq, tk))
    qi = qi_ref[...].reshape(IDX_HEADS * tq, LANES)

    def score_tile(kt, c):
        k0 = pl.multiple_of(kt * tk, tk)
        s = _dot_nt(qi, ki_ref[pl.ds(k0, tk), :])
        sc = jnp.zeros((tq, tk), F32)
        for h in range(IDX_HEADS):
            sc = sc + wb_ref[h * tq:(h + 1) * tq, :] * jnp.maximum(s[h * tq:(h + 1) * tq, :], 0.0)
        kpos = k0 + col
        adm = ((kpos // CHUNK) <= q_chunk) & (kpos < n_keys)
        s_ref[kt] = jnp.where(adm, sc, -jnp.inf)
        return c

    lax.fori_loop(0, n_kt, score_tile, 0)

    needf = need.astype(F32)

    def count(pred):
        def body(kt, c):
            return c + jnp.where(pred(s_ref[kt], kt), 1.0, 0.0)
        c = lax.fori_loop(0, n_kt, body, jnp.zeros((tq, tk), F32))
        return jnp.broadcast_to(jnp.sum(c, axis=1, keepdims=True), (tq, tk))

    lo_ref[...] = jnp.full((tq, tk), _KEY_LO, I32)
    hi_ref[...] = jnp.full((tq, tk), _KEY_HI, I32)

    def bisect(it, c):
        lo = lo_ref[...]
        hi = hi_ref[...]
        mid = (lo >> 1) + (hi >> 1) + (lo & hi & 1)
        thr = _key2f(mid)
        ge = count(lambda s, kt: s >= thr) >= needf
        lo_ref[...] = jnp.where(ge, mid, lo)
        hi_ref[...] = jnp.where(ge, hi, mid)
        return c

    lax.fori_loop(0, 32, bisect, 0)
    thr = _key2f(lo_ref[...])
    thr_ref[...] = thr

    c_ge = count(lambda s, kt: s >= thr)
    n_tied_rows = jnp.max(jnp.where(c_ge > needf, 1.0, 0.0))

    @pl.when(n_tied_rows > 0.0)
    def _():
        rem = needf - count(lambda s, kt: s > thr)
        lo_ref[...] = jnp.zeros((tq, tk), I32)
        hi_ref[...] = jnp.full((tq, tk), nkt * tk, I32)

        def bisect_idx(it, c):
            lo = lo_ref[...]
            hi = hi_ref[...]
            mid = (lo + hi) >> 1
            ok = count(lambda s, kt: (s == thr) & (kt * tk + col < mid)) >= rem
            hi_ref[...] = jnp.where(ok, mid, hi)
            lo_ref[...] = jnp.where(ok, lo, mid)
            return c

        lax.fori_loop(0, idx_bits, bisect_idx, 0)
        cut = hi_ref[...]

        def drop(kt, c):
            s = s_ref[kt]
            s_ref[kt] = jnp.where((s == thr) & (kt * tk + col >= cut), -jnp.inf, s)
            return c

        lax.fori_loop(0, n_kt, drop, 0)

    rows_g = KV_REP * tq
    m_ref[...] = jnp.full(m_ref.shape, NEG_BIG, F32)
    acc_ref[...] = jnp.zeros(acc_ref.shape, F32)

    def attend(kt, c):
        k0 = pl.multiple_of(kt * tk, tk)
        thr_t = thr_ref[...]
        maskadd = jnp.where(s_ref[kt] >= thr_t, 0.0, NEG_BIG)
        dd = jnp.minimum(d0 - kt, nd - 1)
        for g in range(N_KV_HEADS):
            qg = q_ref[g * KV_REP:(g + 1) * KV_REP].reshape(rows_g, LANES)
            s = _dot_nt(qg, k_ref[g, pl.ds(k0, tk), :])
            parts = []
            for r in range(KV_REP):
                b = bias_ref[dd, g * KV_REP + r, 0:tq, :]
                parts.append(s[r * tq:(r + 1) * tq, :] + (b + maskadd))
            lg = jnp.concatenate(parts, axis=0)
            m_old = m_ref[g]
            m_new = jnp.maximum(m_old, jnp.broadcast_to(jnp.max(lg, axis=1, keepdims=True), (rows_g, tk)))
            p = jnp.exp(lg - m_new)
            alpha = jnp.exp(m_old - m_new)
            acc_ref[g] = alpha * acc_ref[g] + _dot(p.astype(BF16), v_ref[g, pl.ds(k0, tk), :])
            m_ref[g] = m_new
        return c

    lax.fori_loop(0, n_kt, attend, 0)

    for g in range(N_KV_HEADS):
        acc = acc_ref[g]
        den = jnp.broadcast_to(acc[:, HEAD_DIM:HEAD_DIM + 1], (rows_g, LANES))
        og = acc / den
        for r in range(KV_REP):
            h = g * KV_REP + r
            o_ref[:, h * LANES:(h + 1) * LANES] = og[r * tq:(r + 1) * tq, :].astype(BF16)


def _dsa(q, qi, w, k_all, v_all, ki_all, bias_tiles, bsz, seq, past, n_keys, tq):
    lk = k_all.shape[2]
    nkt = lk // KEY_TILE
    topk = min(TOPK_MAX, n_keys // 4)
    nd = bias_tiles.shape[0]
    nq = seq // tq
    w3 = w.reshape(bsz, seq, LANES)
    idx_bits = int(math.ceil(math.log2(lk))) + 1
    kern = functools.partial(_dsa_kernel, tq=tq, past=past, n_keys=n_keys, topk=topk, nkt=nkt, nd=nd,
                             idx_bits=idx_bits)
    return pl.pallas_call(
        kern,
        grid=(bsz, nq),
        in_specs=[
            pl.BlockSpec((None, N_HEADS, tq, LANES), lambda b, i: (b, 0, i, 0)),
            pl.BlockSpec((None, IDX_HEADS, tq, LANES), lambda b, i: (b, 0, i, 0)),
            pl.BlockSpec((None, tq, LANES), lambda b, i: (b, i, 0)),
            pl.BlockSpec((None, N_KV_HEADS, lk, LANES), lambda b, i: (b, 0, 0, 0)),
            pl.BlockSpec((None, N_KV_HEADS, lk, LANES), lambda b, i: (b, 0, 0, 0)),
            pl.BlockSpec((None, lk, LANES), lambda b, i: (b, 0, 0)),
            pl.BlockSpec(bias_tiles.shape, lambda b, i: (0, 0, 0, 0)),
        ],
        out_specs=pl.BlockSpec((None, tq, N_HEADS * LANES), lambda b, i: (b, i, 0)),
        out_shape=jax.ShapeDtypeStruct((bsz, seq, N_HEADS * LANES), BF16),
        scratch_shapes=[
            pltpu.VMEM((nkt, tq, KEY_TILE), F32),
            pltpu.VMEM((IDX_HEADS * tq, KEY_TILE), F32),
            pltpu.VMEM((tq, KEY_TILE), F32),
            pltpu.VMEM((tq, KEY_TILE), I32),
            pltpu.VMEM((tq, KEY_TILE), I32),
            pltpu.VMEM((N_KV_HEADS, KV_REP * tq, LANES), F32),
            pltpu.VMEM((N_KV_HEADS, KV_REP * tq, KEY_TILE), F32),
        ],
        compiler_params=_cp(("arbitrary", "arbitrary")),
        name="dsa",
    )(q, qi, w3, k_all, v_all, ki_all, bias_tiles)


def _merge_kernel(x_ref, ya_ref, at_ref, sga_ref, sgb_ref, wup_ref, wout_ref, g2_ref, wr_hi_ref, wr_lo_ref, br_ref,
                  x1_ref, h2_ref, gt_ref, rt_ref, cnt_ref, run_ref, *, tm, sub):
    step = pl.program_id(0)

    @pl.when(step % sub == 0)
    def _():
        run_ref[...] = jnp.zeros(run_ref.shape, F32)

    yb = _dot(at_ref[...], wup_ref[...])
    merged = sga_ref[...].astype(F32) * ya_ref[...].astype(F32) + sgb_ref[...].astype(F32) * yb
    x1 = x_ref[...] + _dot(merged.astype(BF16), wout_ref[...])
    x1_ref[...] = x1
    ms = jnp.mean(x1 * x1, axis=-1, keepdims=True)
    h2 = x1 * lax.rsqrt(ms + EPS) * g2_ref[...]
    h2_hi, h2_lo = _split(h2)
    h2_ref[...] = h2_hi

    wr_hi = wr_hi_ref[...]
    logit = (_dot_nt(wr_hi, h2_hi) + _dot_nt(wr_hi, h2_lo) + _dot_nt(wr_lo_ref[...], h2_hi)) + br_ref[:, 0:1]
    ne = logit.shape[0]
    eid = lax.broadcasted_iota(I32, (ne, tm), 0).astype(F32)
    selb = jnp.zeros((ne, tm), F32)
    tops = []
    picks = []
    for _ in range(TOP_K):
        mx = jnp.max(logit, axis=0, keepdims=True)
        pick = jnp.min(jnp.where(logit == mx, eid, float(ne)), axis=0, keepdims=True)
        hit = eid == pick
        selb = jnp.where(hit, 1.0, selb)
        logit = jnp.where(hit, -jnp.inf, logit)
        tops.append(mx)
        picks.append(hit)
    ex = [jnp.exp(t - tops[0]) for t in tops]
    den = ex[0] + ex[1] + ex[2] + ex[3]
    gate = jnp.zeros((ne, tm), F32)
    for hit, e in zip(picks, ex):
        gate = jnp.where(hit, e / den, gate)
    gt_ref[...] = gate

    sel = selb > 0.5
    selb = selb.astype(BF16)
    r_i = lax.broadcasted_iota(I32, (tm, tm), 0)
    c_i = lax.broadcasted_iota(I32, (tm, tm), 1)
    tri = jnp.where(r_i < c_i, 1.0, 0.0).astype(BF16)
    run = run_ref[...]
    rank = _dot(selb, tri) + jnp.broadcast_to(run[:, 0:1], (ne, tm))
    rt_ref[...] = jnp.where(sel, rank, -1.0)
    run = run + _dot(selb, jnp.ones((tm, LANES), BF16))
    run_ref[...] = run
    cnt_ref[...] = run


def _merge(x2, ya, attn, sga, sgb, mw, tm, moe_tile):
    t, d = x2.shape
    sub = moe_tile // tm
    ne = mw['wr_hi'].shape[0]

    def tok(i):
        return (i, 0)

    def cst(i):
        return (0, 0)

    consts = [mw['wup'], mw['wout'], mw['g2'], mw['wr_hi'], mw['wr_lo'], mw['br']]
    return pl.pallas_call(
        functools.partial(_merge_kernel, tm=tm, sub=sub),
        grid=(t // tm,),
        in_specs=[
            pl.BlockSpec((tm, d), tok),
            pl.BlockSpec((tm, d), tok),
            pl.BlockSpec((tm, attn.shape[-1]), tok),
            pl.BlockSpec((tm, d), tok),
            pl.BlockSpec((tm, d), tok),
        ] + [pl.BlockSpec(a.shape, cst) for a in consts],
        out_specs=(
            pl.BlockSpec((tm, d), tok),
            pl.BlockSpec((tm, d), tok),
            pl.BlockSpec((ne, tm), lambda i: (0, i)),
            pl.BlockSpec((ne, tm), lambda i: (0, i)),
            pl.BlockSpec((None, ne, LANES), lambda i: (i // sub, 0, 0)),
        ),
        out_shape=(
            jax.ShapeDtypeStruct((t, d), F32),
            jax.ShapeDtypeStruct((t, d), BF16),
            jax.ShapeDtypeStruct((ne, t), F32),
            jax.ShapeDtypeStruct((ne, t), F32),
            jax.ShapeDtypeStruct((t // moe_tile, ne, LANES), F32),
        ),
        scratch_shapes=[pltpu.VMEM((ne, LANES), F32)],
        compiler_params=_cp(("arbitrary",)),
        name="merge",
    )(x2, ya, attn, sga, sgb, *consts)


def _moe_kernel(cnt_ref, h2_ref, x1_ref, gt_ref, rt_ref, wg_ref, wu_ref, wd_ref, bg_ref, bu_ref, bd_ref, y_ref,
                *, tt):
    j = pl.program_id(0)
    e = pl.program_id(1)
    ne = pl.num_programs(1)
    rb = MOE_ROWS

    @pl.when(e == 0)
    def _():
        y_ref[...] = x1_ref[...]

    n_rows = cnt_ref[j * ne + e]
    n_blk = (n_rows + rb - 1) // rb
    g_row = gt_ref[0]
    r_row = rt_ref[0]
    rid = lax.broadcasted_iota(I32, (rb, tt), 0).astype(F32)

    def block(blk, c):
        hit = jnp.broadcast_to(r_row, (rb, tt)) == (rid + (blk * rb).astype(F32))
        p = jnp.where(hit, 1.0, 0.0).astype(BF16)
        xg = _dot(p, h2_ref[...]).astype(BF16)
        a = jnp.minimum(_dot(xg, wg_ref[0]) + bg_ref[0], SWIGLU_LIMIT)
        b = jnp.clip(_dot(xg, wu_ref[0]) + bu_ref[0], -SWIGLU_LIMIT, SWIGLU_LIMIT)
        hid = a * jax.nn.sigmoid(SWIGLU_ALPHA * a) * (b + 1.0)
        o = _dot(hid.astype(BF16), wd_ref[0]) + bd_ref[0]
        g_col = jnp.sum(jnp.where(hit, jnp.broadcast_to(g_row, (rb, tt)), 0.0), axis=1, keepdims=True)
        og = (o * g_col).astype(BF16)
        y_ref[...] += _dot_tn(p, og)
        return c

    lax.fori_loop(0, n_blk, block, 0)


def _moe(h2, x1, gt, rt, cnt, ew, tt):
    t, d = h2.shape
    ne = gt.shape[0]
    nt = t // tt
    f = ew['wg'].shape[-1]
    grid_spec = pltpu.PrefetchScalarGridSpec(
        num_scalar_prefetch=1,
        grid=(nt, ne),
        in_specs=[
            pl.BlockSpec((tt, d), lambda j, e, c: (j, 0)),
            pl.BlockSpec((tt, d), lambda j, e, c: (j, 0)),
            pl.BlockSpec((1, 1, tt), lambda j, e, c: (e, 0, j)),
            pl.BlockSpec((1, 1, tt), lambda j, e, c: (e, 0, j)),
            pl.BlockSpec((1, d, f), lambda j, e, c: (e, 0, 0)),
            pl.BlockSpec((1, d, f), lambda j, e, c: (e, 0, 0)),
            pl.BlockSpec((1, f, d), lambda j, e, c: (e, 0, 0)),
            pl.BlockSpec((1, 1, f), lambda j, e, c: (e, 0, 0)),
            pl.BlockSpec((1, 1, f), lambda j, e, c: (e, 0, 0)),
            pl.BlockSpec((1, 1, d), lambda j, e, c: (e, 0, 0)),
        ],
        out_specs=pl.BlockSpec((tt, d), lambda j, e, c: (j, 0)),
    )
    return pl.pallas_call(
        functools.partial(_moe_kernel, tt=tt),
        grid_spec=grid_spec,
        out_shape=jax.ShapeDtypeStruct((t, d), F32),
        compiler_params=_cp(("arbitrary", "arbitrary")),
        name="moe",
    )(cnt, h2, x1, gt, rt, ew['wg'], ew['wu'], ew['wd'], ew['bg'], ew['bu'], ew['bd'])


def _pad_heads(wmat, n_heads, width):
    d = wmat.shape[0]
    w3 = wmat.reshape(d, n_heads, width)
    return jnp.pad(w3, ((0, 0), (0, 0), (0, LANES - width))).reshape(d, n_heads * LANES)


def _pad_lanes(v, width=LANES):
    v = v.reshape(1, -1)
    return jnp.pad(v, ((0, 0), (0, width - v.shape[1])))


def _rel_bucket(rel):
    half = REL_BUCKETS // 2
    max_exact = half // 2
    n = jnp.abs(rel)
    large = max_exact + (jnp.log(jnp.maximum(n, 1).astype(jnp.float32) / max_exact)
                         / math.log(REL_MAX_DIST / max_exact) * (half - max_exact)).astype(jnp.int32)
    large = jnp.minimum(large, half - 1)
    return jnp.where(rel > 0, half, 0) + jnp.where(n < max_exact, n, large)


def _bias_tiles(rel_bias):
    tk = KEY_TILE
    half = REL_BUCKETS // 2
    max_exact = half // 2
    n_sat = int(math.ceil(max_exact * (REL_MAX_DIST / max_exact) ** ((half - 1 - max_exact) / (half - max_exact)))) + 2
    nd = (n_sat + 2 * tk - 2) // tk + 1
    dd = jnp.arange(nd, dtype=I32)[:, None, None]
    r = jnp.arange(tk, dtype=I32)[None, :, None]
    c = jnp.arange(tk, dtype=I32)[None, None, :]
    bucket = _rel_bucket(c - r - dd * tk)
    tiles = rel_bias.astype(F32)[bucket]
    return tiles.transpose(0, 3, 1, 2)


def _prep_proj(norm1_g, w_in, q_norm_g, k_norm_g, idx_k_norm_g, idx_k_norm_b, d_model):
    ssm_w = d_model // 2
    attn_w = N_HEADS * HEAD_DIM
    kv = N_KV_HEADS * HEAD_DIM
    sizes = [ssm_w, attn_w, kv, kv, IDX_HEADS * IDX_DIM, IDX_DIM, IDX_HEADS, d_model, d_model]
    pts = np.cumsum(sizes)[:-1].tolist()
    wu, wq, wk, wv, wqi, wki, wwi, wga, wgb = jnp.split(w_in, pts, axis=1)
    bf = lambda a: a.astype(BF16)
    blk = np.kron(np.eye(N_KV_HEADS), np.ones((HEAD_DIM, HEAD_DIM))) / HEAD_DIM
    return dict(
        g1=norm1_g.reshape(1, -1).astype(F32),
        wu=bf(wu), wq=bf(_pad_heads(wq, N_HEADS, HEAD_DIM)), wk=bf(_pad_heads(wk, N_KV_HEADS, HEAD_DIM)),
        wv=bf(_pad_heads(wv, N_KV_HEADS, HEAD_DIM)), wqi=bf(_pad_heads(wqi, IDX_HEADS, IDX_DIM)),
        wki=bf(_pad_heads(wki, 1, IDX_DIM)), wwi=bf(_pad_heads(wwi, 1, IDX_HEADS)),
        wga=bf(wga), wgb=bf(wgb), wkc=bf(wk), wvc=bf(wv),
        gq=_pad_lanes(q_norm_g.astype(F32)), gk=_pad_lanes(k_norm_g.astype(F32)),
        gkc=jnp.tile(k_norm_g.astype(F32), N_KV_HEADS).reshape(1, -1),
        gi=_pad_lanes(idx_k_norm_g.astype(F32)), bi=_pad_lanes(idx_k_norm_b.astype(F32)),
        ones_h=jnp.full((LANES, LANES), 1.0 / HEAD_DIM, BF16),
        ones_c=jnp.asarray(blk, BF16),
    )


def _prep_s5(lre, lim, log_dt, b_re, b_im, c_re, c_im, dvec, wa, wb):
    g, p = lre.shape
    ch = b_re.shape[-1]
    lam = lax.complex(lre.astype(F32), lim.astype(F32))
    dt = jnp.exp(log_dt.astype(F32))[:, None]
    a_bar = jnp.exp(lam * dt)
    b_bar = ((a_bar - 1.0) / lam)[:, :, None] * lax.complex(b_re.astype(F32), b_im.astype(F32))
    eye = jnp.eye(g, dtype=F32)
    b_r = jnp.einsum('gpc,gh->gchp', jnp.real(b_bar), eye).reshape(g * ch, g * p)
    b_i = jnp.einsum('gpc,gh->gchp', jnp.imag(b_bar), eye).reshape(g * ch, g * p)
    c_r = jnp.einsum('gcp,gh->gphc', c_re.astype(F32), eye).reshape(g * p, g * ch)
    c_i = jnp.einsum('gcp,gh->gphc', c_im.astype(F32), eye).reshape(g * p, g * ch)
    return dict(
        bmat=jnp.concatenate([b_r, b_i], axis=1).astype(BF16),
        cmat=jnp.concatenate([c_r, -c_i], axis=0).astype(BF16),
        a_re=jnp.real(a_bar).reshape(1, g * p), a_im=jnp.imag(a_bar).reshape(1, g * p),
        d=dvec.reshape(1, -1).astype(F32), wa=wa.astype(BF16), wb=wb.astype(BF16),
    )


def _prep_merge(w_attn_up, w_out, norm2_g, w_router, b_router):
    d = w_attn_up.shape[1]
    wup = jnp.pad(w_attn_up.reshape(N_HEADS, HEAD_DIM, d), ((0, 0), (0, LANES - HEAD_DIM), (0, 0)))
    wr_t = w_router.astype(F32).T
    wr_hi = wr_t.astype(BF16)
    wr_lo = (wr_t - wr_hi.astype(F32)).astype(BF16)
    return dict(
        wup=wup.reshape(N_HEADS * LANES, d).astype(BF16), wout=w_out.astype(BF16),
        g2=norm2_g.reshape(1, -1).astype(F32), wr_hi=wr_hi, wr_lo=wr_lo,
        br=jnp.broadcast_to(b_router.astype(F32)[:, None], (b_router.shape[0], LANES)),
    )


def _prep_moe(wg, bg, wu, bu, wd, bd):
    return dict(wg=wg.astype(BF16), wu=wu.astype(BF16), wd=wd.astype(BF16),
                bg=bg.astype(F32)[:, None, :], bu=bu.astype(F32)[:, None, :], bd=bd.astype(F32)[:, None, :])


def _pick_tile(n, pref):
    t = min(n, pref)
    while n % t:
        t //= 2
    return t


def _pad_keys(a, axis, lk):
    pad = [(0, 0)] * a.ndim
    pad[axis] = (0, lk - a.shape[axis])
    return jnp.pad(a, pad)


def _trunk_layer(x, past_k, past_v, past_ik, h0_re, h0_im, pw, sw, mw, ew, bias_tiles):
    bsz, seq, d = x.shape
    t = bsz * seq
    tm = _pick_tile(seq, 512)
    u_tb, q, kp, vp, qi, kip, w, sga, sgb, kc, vc, kic = _proj(x, pw, bsz, seq, tm)

    half = sw['a_re'].shape[1]
    if h0_re is None:
        h0 = jnp.zeros((bsz, 2 * half), F32)
    else:
        h0 = jnp.concatenate([h0_re.reshape(bsz, half), h0_im.reshape(bsz, half)], axis=1).astype(F32)
    tc = _pick_tile(seq, max(1, 512 // bsz))
    ya_tb, hout = _s5(u_tb, h0, sw, bsz, seq, tc)
    groups = half // SSM_STATE
    s_re = hout[:, :half].reshape(bsz, groups, SSM_STATE)
    s_im = hout[:, half:].reshape(bsz, groups, SSM_STATE)

    past = 0 if past_k is None else past_k.shape[1]
    n_keys = past + seq
    lk = -(-n_keys // KEY_TILE) * KEY_TILE
    kip3 = kip.reshape(bsz, seq, LANES)
    if past:
        lane = jnp.arange(LANES)
        pk = jnp.pad(past_k.astype(F32), ((0, 0), (0, 0), (0, 0), (0, LANES - HEAD_DIM))).astype(BF16)
        pv = jnp.pad(past_v.astype(F32), ((0, 0), (0, 0), (0, 0), (0, LANES - HEAD_DIM)))
        pv = jnp.where(lane == HEAD_DIM, 1.0, pv).astype(BF16)
        pik = jnp.pad(past_ik.astype(F32), ((0, 0), (0, 0), (0, LANES - IDX_DIM))).astype(BF16)
        k_all = jnp.concatenate([pk.transpose(0, 2, 1, 3), kp], axis=2)
        v_all = jnp.concatenate([pv.transpose(0, 2, 1, 3), vp], axis=2)
        ki_all = jnp.concatenate([pik, kip3], axis=1)
    else:
        k_all, v_all, ki_all = kp, vp, kip3
    k_all = _pad_keys(k_all, 2, lk)
    v_all = _pad_keys(v_all, 2, lk)
    ki_all = _pad_keys(ki_all, 1, lk)
    tq = _pick_tile(seq, KEY_TILE)
    attn = _dsa(q, qi, w, k_all, v_all, ki_all, bias_tiles, bsz, seq, past, n_keys, tq)

    moe_tile = _pick_tile(t, 1024)
    tm2 = _pick_tile(moe_tile, 512)
    ya = ya_tb.reshape(seq, bsz, d).transpose(1, 0, 2).reshape(t, d)
    x1, h2, gt, rt, cnt = _merge(x.reshape(t, d), ya, attn.reshape(t, attn.shape[-1]), sga, sgb, mw, tm2, moe_tile)
    cnt_i = cnt[:, :, 0].astype(I32).reshape(-1)
    ne = gt.shape[0]
    y = _moe(h2, x1, gt.reshape(ne, 1, t), rt.reshape(ne, 1, t), cnt_i, ew, moe_tile)

    k_new = kc.reshape(bsz, seq, N_KV_HEADS, HEAD_DIM)
    v_new = vc.reshape(bsz, seq, N_KV_HEADS, HEAD_DIM)
    ik_new = kic.reshape(bsz, seq, IDX_DIM)
    return y.reshape(bsz, seq, d), k_new, v_new, ik_new, s_re, s_im


def kernel(x_prompt, x_sample, cache_k, cache_v, cache_idx_k, state_ssm_re, state_ssm_im, rel_bias, norm1_g, w_in, ssm_lambda_re, ssm_lambda_im, ssm_log_dt, ssm_b_re, ssm_b_im, ssm_c_re, ssm_c_im, ssm_d, ssm_w_glu_a, ssm_w_glu_b, q_norm_g, k_norm_g, idx_k_norm_g, idx_k_norm_b, w_attn_up, w_out, norm2_g, moe_w_router, moe_b_router, moe_w_gate, moe_b_gate, moe_w_up, moe_b_up, moe_w_down, moe_b_down):
    depth = w_in.shape[0]
    d_model = x_prompt.shape[-1]
    bias_tiles = _bias_tiles(rel_bias)
    xp, xs = x_prompt, x_sample
    st_p, st_s = [], []
    for l in range(depth):
        pw = _prep_proj(norm1_g[l], w_in[l], q_norm_g[l], k_norm_g[l], idx_k_norm_g[l], idx_k_norm_b[l], d_model)
        sw = _prep_s5(ssm_lambda_re[l], ssm_lambda_im[l], ssm_log_dt[l], ssm_b_re[l], ssm_b_im[l], ssm_c_re[l],
                      ssm_c_im[l], ssm_d[l], ssm_w_glu_a[l], ssm_w_glu_b[l])
        mw = _prep_merge(w_attn_up[l], w_out[l], norm2_g[l], moe_w_router[l], moe_b_router[l])
        ew = _prep_moe(moe_w_gate[l], moe_b_gate[l], moe_w_up[l], moe_b_up[l], moe_w_down[l], moe_b_down[l])
        xp, *sp = _trunk_layer(xp, None, None, None, None, None, pw, sw, mw, ew, bias_tiles)
        xs, *ss = _trunk_layer(xs, cache_k[l], cache_v[l], cache_idx_k[l], state_ssm_re[l], state_ssm_im[l],
                               pw, sw, mw, ew, bias_tiles)
        st_p.append(sp)
        st_s.append(ss)
    outs_p = [jnp.stack([s[i] for s in st_p]) for i in range(5)]
    outs_s = [jnp.stack([s[i] for s in st_s]) for i in range(5)]
    return (xp, xs, *outs_p, *outs_s)
```

```python
import functools
import math

import numpy as np
import jax
import jax.numpy as jnp
from jax import lax
from jax.experimental import pallas as pl
from jax.experimental.pallas import tpu as pltpu

F32 = jnp.float32
BF16 = jnp.bfloat16
I32 = jnp.int32

LANES = 128
VMEM_LIMIT = 56 * 1024 * 1024

CHUNK = 64
SSM_GROUP_CH = 16
SSM_STATE = 64
N_HEADS = 8
HEAD_DIM = 64
N_KV_HEADS = 2
KV_REP = N_HEADS // N_KV_HEADS
IDX_HEADS = 8
IDX_DIM = 64
TOPK_MAX = 256
REL_BUCKETS = 32
REL_MAX_DIST = 1024
N_EXPERTS = 32
TOP_K = 4
SWIGLU_LIMIT = 7.0
SWIGLU_ALPHA = 1.702
EPS = 1e-6

KEY_TILE = 128
KEY_BLOCK = 256
LOG2E = math.log2(math.e)
NEG_BIG = -1e30
MOE_ROWS = 128


def _cp(sem):
    return pltpu.CompilerParams(dimension_semantics=sem, vmem_limit_bytes=VMEM_LIMIT)


def _dot(a, b):
    return jnp.dot(a, b, preferred_element_type=F32)


def _dot_nt(a, b):
    return lax.dot_general(a, b, (((1,), (1,)), ((), ())), preferred_element_type=F32)


def _dot_tn(a, b):
    return lax.dot_general(a, b, (((0,), (0,)), ((), ())), preferred_element_type=F32)


def _split(a):
    hi = a.astype(BF16)
    lo = (a - hi.astype(F32)).astype(BF16)
    return hi, lo


def _dot_split(a, g):
    hi, lo = _split(a)
    return _dot(hi, g) + _dot(lo, g)


def _proj_kernel(x_ref, g1_ref, wu_ref, wq_ref, wk_ref, wv_ref, wqi_ref, wki_ref, wwi_ref, wga_ref, wgb_ref,
                 wkc_ref, wvc_ref, gq_ref, gk_ref, gkc_ref, gi_ref, bi_ref, ones_h_ref, ones_c_ref,
                 u_ref, q_ref, kp_ref, vp_ref, qi_ref, kip_ref, w_ref, sga_ref, sgb_ref,
                 kc_ref, vc_ref, kic_ref):
    x = x_ref[...]
    ms = jnp.mean(x * x, axis=-1, keepdims=True)
    hn = (x * lax.rsqrt(ms + EPS) * g1_ref[...]).astype(BF16)
    ones_h = ones_h_ref[...]
    lane = lax.broadcasted_iota(I32, (x.shape[0], LANES), 1)

    u_ref[...] = _dot(hn, wu_ref[...]).astype(BF16)

    q = _dot(hn, wq_ref[...])
    scale = HEAD_DIM ** -0.5 * LOG2E
    for h in range(N_HEADS):
        qh = q[:, h * LANES:(h + 1) * LANES]
        msq = _dot_split(qh * qh, ones_h)
        q_ref[h] = (qh * lax.rsqrt(msq + EPS) * (gq_ref[...] * scale)).astype(BF16)

    k = _dot(hn, wk_ref[...])
    for g in range(N_KV_HEADS):
        kg = k[:, g * LANES:(g + 1) * LANES]
        msk = _dot_split(kg * kg, ones_h)
        kn = kg * lax.rsqrt(msk + EPS) * gk_ref[...]
        kp_ref[g] = jnp.where(lane == HEAD_DIM, 1.0, kn).astype(BF16)

    v = _dot(hn, wv_ref[...])
    for g in range(N_KV_HEADS):
        vg = v[:, g * LANES:(g + 1) * LANES]
        vp_ref[g] = jnp.where(lane == HEAD_DIM, 1.0, vg).astype(BF16)

    qi = _dot(hn, wqi_ref[...])
    for h in range(IDX_HEADS):
        qi_ref[h] = qi[:, h * LANES:(h + 1) * LANES].astype(BF16)

    ki = _dot(hn, wki_ref[...])
    mu = _dot_split(ki, ones_h)
    xc = jnp.where(lane < IDX_DIM, ki - mu, 0.0)
    var = _dot_split(xc * xc, ones_h)
    kin = xc * lax.rsqrt(var + EPS) * gi_ref[...] + bi_ref[...]
    kip_ref[...] = kin.astype(BF16)
    kic_ref[...] = kin[:, :IDX_DIM]

    w_ref[...] = _dot(hn, wwi_ref[...]) * (IDX_HEADS ** -0.5 * IDX_DIM ** -0.5)

    sga_ref[...] = jax.nn.sigmoid(_dot(hn, wga_ref[...])).astype(BF16)
    sgb_ref[...] = jax.nn.sigmoid(_dot(hn, wgb_ref[...])).astype(BF16)

    kc = _dot(hn, wkc_ref[...])
    mskc = _dot_split(kc * kc, ones_c_ref[...])
    kc_ref[...] = kc * lax.rsqrt(mskc + EPS) * gkc_ref[...]
    vc_ref[...] = _dot(hn, wvc_ref[...])


def _proj(x, pw, bsz, seq, tm):
    d = x.shape[-1]
    nt = seq // tm
    t = bsz * seq
    x2 = x.reshape(t, d)

    def tok(b, i):
        return (b * nt + i, 0)

    def cst(b, i):
        return (0, 0)

    def wspec(a):
        return pl.BlockSpec(a.shape, cst)

    weights = [pw['g1'], pw['wu'], pw['wq'], pw['wk'], pw['wv'], pw['wqi'], pw['wki'], pw['wwi'], pw['wga'],
               pw['wgb'], pw['wkc'], pw['wvc'], pw['gq'], pw['gk'], pw['gkc'], pw['gi'], pw['bi'],
               pw['ones_h'], pw['ones_c']]
    ssm_w = pw['wu'].shape[1]
    out_shape = (
        jax.ShapeDtypeStruct((seq, bsz * ssm_w), BF16),
        jax.ShapeDtypeStruct((bsz, N_HEADS, seq, LANES), BF16),
        jax.ShapeDtypeStruct((bsz, N_KV_HEADS, seq, LANES), BF16),
        jax.ShapeDtypeStruct((bsz, N_KV_HEADS, seq, LANES), BF16),
        jax.ShapeDtypeStruct((bsz, IDX_HEADS, seq, LANES), BF16),
        jax.ShapeDtypeStruct((t, LANES), BF16),
        jax.ShapeDtypeStruct((t, LANES), F32),
        jax.ShapeDtypeStruct((t, d), BF16),
        jax.ShapeDtypeStruct((t, d), BF16),
        jax.ShapeDtypeStruct((t, N_KV_HEADS * HEAD_DIM), F32),
        jax.ShapeDtypeStruct((t, N_KV_HEADS * HEAD_DIM), F32),
        jax.ShapeDtypeStruct((t, IDX_DIM), F32),
    )

    def hm(nh):
        return pl.BlockSpec((None, nh, tm, LANES), lambda b, i: (b, 0, i, 0))

    out_specs = (
        pl.BlockSpec((tm, ssm_w), lambda b, i: (i, b)),
        hm(N_HEADS), hm(N_KV_HEADS), hm(N_KV_HEADS), hm(IDX_HEADS),
        pl.BlockSpec((tm, LANES), tok), pl.BlockSpec((tm, LANES), tok),
        pl.BlockSpec((tm, d), tok), pl.BlockSpec((tm, d), tok),
        pl.BlockSpec((tm, N_KV_HEADS * HEAD_DIM), tok), pl.BlockSpec((tm, N_KV_HEADS * HEAD_DIM), tok),
        pl.BlockSpec((tm, IDX_DIM), tok),
    )
    return pl.pallas_call(
        _proj_kernel,
        grid=(bsz, nt),
        in_specs=[pl.BlockSpec((tm, d), tok)] + [wspec(a) for a in weights],
        out_specs=out_specs,
        out_shape=out_shape,
        compiler_params=_cp(("arbitrary", "arbitrary")),
        name="proj",
    )(x2, *weights)


def _gelu_tanh(x):
    return 0.5 * x * (1.0 + jnp.tanh(math.sqrt(2.0 / math.pi) * (x + 0.044715 * (x * x * x))))


def _s5_kernel(u_ref, h0_ref, bmat_ref, are_ref, aim_ref, cmat_ref, dvec_ref, wa_ref, wb_ref,
               ya_ref, hout_ref, state_ref, bu_ref, *, bsz, tc, strip):
    s = pl.program_id(0)
    half = are_ref.shape[1]

    @pl.when(s == 0)
    def _():
        state_ref[...] = h0_ref[...]

    u = u_ref[...]
    bu_ref[...] = _dot(u, bmat_ref[...])

    for c0 in range(0, half, strip):
        ar = jnp.broadcast_to(are_ref[:, c0:c0 + strip], (bsz, strip))
        ai = jnp.broadcast_to(aim_ref[:, c0:c0 + strip], (bsz, strip))
        hr0 = state_ref[:, c0:c0 + strip]
        hi0 = state_ref[:, half + c0:half + c0 + strip]

        def step(t, carry):
            hr, hi = carry
            r0 = pl.multiple_of(t * bsz, bsz)
            br = bu_ref[pl.ds(r0, bsz), c0:c0 + strip]
            bi = bu_ref[pl.ds(r0, bsz), half + c0:half + c0 + strip]
            nr = ar * hr - ai * hi + br
            ni = ar * hi + ai * hr + bi
            bu_ref[pl.ds(r0, bsz), c0:c0 + strip] = nr
            bu_ref[pl.ds(r0, bsz), half + c0:half + c0 + strip] = ni
            return nr, ni

        hr, hi = lax.fori_loop(0, tc, step, (hr0, hi0))
        state_ref[:, c0:c0 + strip] = hr
        state_ref[:, half + c0:half + c0 + strip] = hi

    y = _dot(bu_ref[...].astype(BF16), cmat_ref[...]) + dvec_ref[...] * u.astype(F32)
    g = _gelu_tanh(y).astype(BF16)
    ya = _dot(g, wa_ref[...]) * jax.nn.sigmoid(_dot(g, wb_ref[...]))
    ya_ref[...] = ya.astype(BF16)

    @pl.when(s == pl.num_programs(0) - 1)
    def _():
        hout_ref[...] = state_ref[...]


def _s5(u_tb, h0, sw, bsz, seq, tc):
    rows = tc * bsz
    ssm_w = sw['bmat'].shape[0]
    two_half = sw['bmat'].shape[1]
    half = two_half // 2
    d = sw['wa'].shape[1]
    u2 = u_tb.reshape(seq * bsz, ssm_w)
    strip = min(512, half)

    def cst(s):
        return (0, 0)

    consts = [h0, sw['bmat'], sw['a_re'], sw['a_im'], sw['cmat'], sw['d'], sw['wa'], sw['wb']]
    ya, hout = pl.pallas_call(
        functools.partial(_s5_kernel, bsz=bsz, tc=tc, strip=strip),
        grid=(seq // tc,),
        in_specs=[pl.BlockSpec((rows, ssm_w), lambda s: (s, 0))] + [pl.BlockSpec(a.shape, cst) for a in consts],
        out_specs=(pl.BlockSpec((rows, d), lambda s: (s, 0)), pl.BlockSpec((bsz, two_half), cst)),
        out_shape=(jax.ShapeDtypeStruct((seq * bsz, d), BF16), jax.ShapeDtypeStruct((bsz, two_half), F32)),
        scratch_shapes=[pltpu.VMEM((bsz, two_half), F32), pltpu.VMEM((rows, two_half), F32)],
        compiler_params=_cp(("arbitrary",)),
        name="s5",
    )(u2, *consts)
    return ya.reshape(seq, bsz * d), hout


def _f2key(x):
    b = lax.bitcast_convert_type(x, I32)
    return b ^ ((b >> 31) & 0x7FFFFFFF)


def _key2f(k):
    return lax.bitcast_convert_type(k ^ ((k >> 31) & 0x7FFFFFFF), F32)


_KEY_LO = int(np.array(-np.finfo(np.float32).max, np.float32).view(np.int32)) ^ 0x7FFFFFFF
_KEY_HI = int(np.array(np.inf, np.float32).view(np.int32))
if _KEY_LO >= 2 ** 31:
    _KEY_LO -= 2 ** 32


def _dsa_kernel_v1(q_ref, qi_ref, w_ref, k_ref, v_ref, ki_ref, bias_ref, o_ref,
                   s_ref, wb_ref, thr_ref, lo_ref, hi_ref, acc_ref, m_ref,
                   *, tq, past, n_keys, topk, nkt, nd, idx_bits):
    i = pl.program_id(1)
    tk = KEY_TILE
    q0 = past + i * tq
    last_chunk = (q0 + tq - 1) // CHUNK
    n_kt = jnp.minimum(nkt, ((last_chunk + 1) * CHUNK + tk - 1) // tk)
    d0 = q0 // tk

    row = lax.broadcasted_iota(I32, (tq, tk), 0)
    col = lax.broadcasted_iota(I32, (tq, tk), 1)
    q_chunk = (q0 + row) // CHUNK
    n_adm = jnp.minimum((q_chunk + 1) * CHUNK, n_keys)
    need = jnp.minimum(topk, n_adm)

    w = w_ref[...]
    for h in range(IDX_HEADS):
        wb_ref[h * tq:(h + 1) * tq, :] = jnp.broadcast_to(w[:, h:h + 1], (tq, tk))
    qi = qi_ref[...].reshape(IDX_HEADS * tq, LANES)

    def score_tile(kt, c):
        k0 = pl.multiple_of(kt * tk, tk)
        s = _dot_nt(qi, ki_ref[pl.ds(k0, tk), :])
        sc = jnp.zeros((tq, tk), F32)
        for h in range(IDX_HEADS):
            sc = sc + wb_ref[h * tq:(h + 1) * tq, :] * jnp.maximum(s[h * tq:(h + 1) * tq, :], 0.0)
        kpos = k0 + col
        adm = ((kpos // CHUNK) <= q_chunk) & (kpos < n_keys)
        s_ref[kt] = jnp.where(adm, sc, -jnp.inf)
        return c

    lax.fori_loop(0, n_kt, score_tile, 0)

    needf = need.astype(F32)

    def count(pred):
        def body(kt, c):
            return c + jnp.where(pred(s_ref[kt], kt), 1.0, 0.0)
        c = lax.fori_loop(0, n_kt, body, jnp.zeros((tq, tk), F32))
        return jnp.broadcast_to(jnp.sum(c, axis=1, keepdims=True), (tq, tk))

    lo_ref[...] = jnp.full((tq, tk), _KEY_LO, I32)
    hi_ref[...] = jnp.full((tq, tk), _KEY_HI, I32)

    def bisect(it, c):
        lo = lo_ref[...]
        hi = hi_ref[...]
        mid = (lo >> 1) + (hi >> 1) + (lo & hi & 1)
        thr = _key2f(mid)
        ge = count(lambda s, kt: s >= thr) >= needf
        lo_ref[...] = jnp.where(ge, mid, lo)
        hi_ref[...] = jnp.where(ge, hi, mid)
        return c

    lax.fori_loop(0, 32, bisect, 0)
    thr = _key2f(lo_ref[...])
    thr_ref[...] = thr

    c_ge = count(lambda s, kt: s >= thr)
    n_tied_rows = jnp.max(jnp.where(c_ge > needf, 1.0, 0.0))

    @pl.when(n_tied_rows > 0.0)
    def _():
        rem = needf - count(lambda s, kt: s > thr)
        lo_ref[...] = jnp.zeros((tq, tk), I32)
        hi_ref[...] = jnp.full((tq, tk), nkt * tk, I32)

        def bisect_idx(it, c):
            lo = lo_ref[...]
            hi = hi_ref[...]
            mid = (lo + hi) >> 1
            ok = count(lambda s, kt: (s == thr) & (kt * tk + col < mid)) >= rem
            hi_ref[...] = jnp.where(ok, mid, hi)
            lo_ref[...] = jnp.where(ok, lo, mid)
            return c

        lax.fori_loop(0, idx_bits, bisect_idx, 0)
        cut = hi_ref[...]

        def drop(kt, c):
            s = s_ref[kt]
            s_ref[kt] = jnp.where((s == thr) & (kt * tk + col >= cut), -jnp.inf, s)
            return c

        lax.fori_loop(0, n_kt, drop, 0)

    rows_g = KV_REP * tq
    m_ref[...] = jnp.full(m_ref.shape, NEG_BIG, F32)
    acc_ref[...] = jnp.zeros(acc_ref.shape, F32)

    def attend(kt, c):
        k0 = pl.multiple_of(kt * tk, tk)
        thr_t = thr_ref[...]
        maskadd = jnp.where(s_ref[kt] >= thr_t, 0.0, NEG_BIG)
        dd = jnp.minimum(d0 - kt, nd - 1)
        for g in range(N_KV_HEADS):
            qg = q_ref[g * KV_REP:(g + 1) * KV_REP].reshape(rows_g, LANES)
            s = _dot_nt(qg, k_ref[g, pl.ds(k0, tk), :])
            parts = []
            for r in range(KV_REP):
                b = bias_ref[dd, g * KV_REP + r, 0:tq, :]
                parts.append(s[r * tq:(r + 1) * tq, :] + (b + maskadd))
            lg = jnp.concatenate(parts, axis=0)
            m_old = m_ref[g]
            m_new = jnp.maximum(m_old, jnp.broadcast_to(jnp.max(lg, axis=1, keepdims=True), (rows_g, tk)))
            p = jnp.exp(lg - m_new)
            alpha = jnp.exp(m_old - m_new)
            acc_ref[g] = alpha * acc_ref[g] + _dot(p.astype(BF16), v_ref[g, pl.ds(k0, tk), :])
            m_ref[g] = m_new
        return c

    lax.fori_loop(0, n_kt, attend, 0)

    for g in range(N_KV_HEADS):
        acc = acc_ref[g]
        den = jnp.broadcast_to(acc[:, HEAD_DIM:HEAD_DIM + 1], (rows_g, LANES))
        og = acc / den
        for r in range(KV_REP):
            h = g * KV_REP + r
            o_ref[:, h * LANES:(h + 1) * LANES] = og[r * tq:(r + 1) * tq, :].astype(BF16)


def _dsa_kernel(st_ref, q_ref, qi_ref, w_ref, k_ref, v_ref, ki_ref, bias_ref, o_ref,
                s_ref, wb_ref, lo_ref, hi_ref, clo_ref, chi_ref, thr_ref, m_ref, q2_ref, acc_ref,
                *, bsz, tq, past, n_keys, topk, nkt, nd, idx_bits):
    b_id = pl.program_id(0)
    i = pl.program_id(1)
    kb = KEY_BLOCK
    q0 = past + i * tq
    last_chunk = (q0 + tq - 1) // CHUNK
    n_kt = jnp.minimum(nkt, ((last_chunk + 1) * CHUNK + kb - 1) // kb)
    d0 = q0 // KEY_TILE

    row = lax.broadcasted_iota(I32, (tq, kb), 0)
    col = lax.broadcasted_iota(I32, (tq, kb), 1)
    q_chunk = (q0 + row) // CHUNK
    row1 = lax.broadcasted_iota(I32, (tq, LANES), 0)
    n_adm = jnp.minimum(((q0 + row1) // CHUNK + 1) * CHUNK, n_keys)
    n_admf = n_adm.astype(F32)
    needf = jnp.minimum(topk, n_adm).astype(F32)

    def wide(x):
        return jnp.concatenate([x, x], axis=1)

    w = w_ref[...]
    for h in range(IDX_HEADS):
        wb_ref[h * tq:(h + 1) * tq, :] = jnp.broadcast_to(w[:, h:h + 1], (tq, kb))
    qi = qi_ref[...].reshape(IDX_HEADS * tq, LANES)

    def score_tile(kt, c):
        k0 = pl.multiple_of(kt * kb, kb)
        s = _dot_nt(qi, ki_ref[pl.ds(k0, kb), :])
        sc = jnp.zeros((tq, kb), F32)
        for h in range(IDX_HEADS):
            sc = sc + wb_ref[h * tq:(h + 1) * tq, :] * jnp.maximum(s[h * tq:(h + 1) * tq, :], 0.0)
        kpos = k0 + col
        adm = ((kpos // CHUNK) <= q_chunk) & (kpos < n_keys)
        s_ref[kt] = jnp.where(adm, sc, -jnp.inf)
        return c

    lax.fori_loop(0, n_kt, score_tile, 0)

    def row_reduce(x, op):
        x = op(x[:, :LANES], x[:, LANES:])
        red = jnp.max if op is jnp.maximum else (jnp.min if op is jnp.minimum else jnp.sum)
        return jnp.broadcast_to(red(x, axis=1, keepdims=True), (tq, LANES))

    def count(pred):
        def body(kt, c):
            return c + jnp.where(pred(s_ref[kt], kt), 1.0, 0.0)
        return row_reduce(lax.fori_loop(0, n_kt, body, jnp.zeros((tq, kb), F32)), jnp.add)

    def minmax(kt, c):
        mx, mn = c
        s = s_ref[kt]
        return jnp.maximum(mx, s), jnp.minimum(mn, jnp.where(s == -jnp.inf, jnp.inf, s))

    mx, mn = lax.fori_loop(0, n_kt, minmax,
                           (jnp.full((tq, kb), -jnp.inf, F32), jnp.full((tq, kb), jnp.inf, F32)))
    lo_ref[...] = row_reduce(mn, jnp.minimum)
    hi_ref[...] = _key2f(_f2key(row_reduce(mx, jnp.maximum)) + 1)
    clo_ref[...] = n_admf
    chi_ref[...] = jnp.zeros((tq, LANES), F32)

    def searching(lo, hi, clo):
        return (_f2key(hi) > _f2key(lo) + 1) & (clo > needf)

    def refine(c):
        it, _ = c
        lo, hi, clo, chi = lo_ref[...], hi_ref[...], clo_ref[...], chi_ref[...]
        klo, khi = _f2key(lo), _f2key(hi)
        t_interp = lo + (hi - lo) * ((clo - needf - 0.5) / (clo - chi))
        k_bisect = (klo >> 1) + (khi >> 1) + (klo & khi & 1)
        k_t = jnp.where(it % 2 == 0, _f2key(t_interp), k_bisect)
        k_t = jnp.minimum(jnp.maximum(k_t, klo + 1), khi - 1)
        t = _key2f(k_t)
        t2 = wide(t)
        cnt = count(lambda s, kt: s >= t2)
        open_ = searching(lo, hi, clo)
        up = open_ & (cnt >= needf)
        dn = open_ & (cnt < needf)
        lo = jnp.where(up, t, lo)
        clo = jnp.where(up, cnt, clo)
        hi = jnp.where(dn, t, hi)
        chi = jnp.where(dn, cnt, chi)
        lo_ref[...], hi_ref[...], clo_ref[...], chi_ref[...] = lo, hi, clo, chi
        return it + 1, jnp.max(jnp.where(searching(lo, hi, clo), 1.0, 0.0))

    active0 = jnp.max(jnp.where(searching(lo_ref[...], hi_ref[...], clo_ref[...]), 1.0, 0.0))
    lax.while_loop(lambda c: c[1] > 0.0, refine, (jnp.int32(0), active0))
    thr = lo_ref[...]
    thr_ref[...] = thr
    thr2 = wide(thr)

    n_tied_rows = jnp.max(jnp.where(clo_ref[...] > needf, 1.0, 0.0))

    @pl.when(n_tied_rows > 0.0)
    def _():
        rem = needf - count(lambda s, kt: s > thr2)
        lo_ref[...] = jnp.zeros((tq, LANES), F32)
        hi_ref[...] = jnp.full((tq, LANES), float(nkt * kb), F32)
        colf = col.astype(F32)

        def bisect_idx(it, c):
            lo = lo_ref[...]
            hi = hi_ref[...]
            mid = jnp.floor((lo + hi) * 0.5)
            mid2 = wide(mid)
            ok = count(lambda s, kt: (s == thr2) & ((kt * kb).astype(F32) + colf < mid2)) >= rem
            hi_ref[...] = jnp.where(ok, mid, hi)
            lo_ref[...] = jnp.where(ok, lo, mid)
            return c

        lax.fori_loop(0, idx_bits, bisect_idx, 0)
        cut2 = wide(hi_ref[...])

        def drop(kt, c):
            s = s_ref[kt]
            s_ref[kt] = jnp.where((s == thr2) & ((kt * kb).astype(F32) + colf >= cut2), -jnp.inf, s)
            return c

        lax.fori_loop(0, n_kt, drop, 0)

    rows_g = KV_REP * tq
    qf = q_ref[...].reshape(N_HEADS * tq, LANES).astype(F32)
    qn = jnp.sqrt(jnp.sum(qf * qf, axis=1, keepdims=True))
    for h in range(N_HEADS):
        kmax = st_ref[b_id * N_KV_HEADS + h // KV_REP]
        bmax = st_ref[bsz * N_KV_HEADS + h]
        bound = qn[h * tq:(h + 1) * tq, :] * (kmax * 1.01) + (bmax + 0.1)
        m_ref[h * tq:(h + 1) * tq, :] = jnp.broadcast_to(bound, (tq, LANES))

    def bias_mask(kt):
        maskadd = jnp.where(s_ref[kt] >= thr2, 0.0, NEG_BIG)
        da = jnp.clip(d0 - 2 * kt, 0, nd - 1)
        db = jnp.clip(d0 - 2 * kt - 1, 0, nd - 1)

        def bm(h):
            bias = jnp.concatenate([bias_ref[da, h, 0:tq, :], bias_ref[db, h, 0:tq, :]], axis=1)
            return bias + maskadd
        return bm

    @pl.when(jnp.max(m_ref[...]) > 30.0)
    def _():
        m_ref[...] = jnp.full(m_ref.shape, NEG_BIG, F32)

        def row_max(kt, c):
            k0 = pl.multiple_of(kt * kb, kb)
            bm = bias_mask(kt)
            for g in range(N_KV_HEADS):
                qg = q_ref[g * KV_REP:(g + 1) * KV_REP].reshape(rows_g, LANES)
                s = _dot_nt(qg, k_ref[g, pl.ds(k0, kb), :])
                for r in range(KV_REP):
                    h = g * KV_REP + r
                    lg = s[r * tq:(r + 1) * tq, :] + bm(h)
                    m_ref[h * tq:(h + 1) * tq, :] = jnp.maximum(m_ref[h * tq:(h + 1) * tq, :],
                                                                jnp.maximum(lg[:, :LANES], lg[:, LANES:]))
            return c

        lax.fori_loop(0, n_kt, row_max, 0)
        m_ref[...] = jnp.broadcast_to(jnp.max(m_ref[...], axis=1, keepdims=True), m_ref.shape)

    lane8 = lax.broadcasted_iota(I32, (N_HEADS * tq, LANES), 1)
    q2_ref[...] = jnp.where(lane8 == HEAD_DIM, -m_ref[...], qf).astype(BF16)
    acc_ref[...] = jnp.zeros(acc_ref.shape, F32)

    def attend(kt, c):
        k0 = pl.multiple_of(kt * kb, kb)
        bm = bias_mask(kt)
        for g in range(N_KV_HEADS):
            s = _dot_nt(q2_ref[g * rows_g:(g + 1) * rows_g, :], k_ref[g, pl.ds(k0, kb), :])
            lg = jnp.concatenate([s[r * tq:(r + 1) * tq, :] + bm(g * KV_REP + r) for r in range(KV_REP)], axis=0)
            acc_ref[g] += _dot(jnp.exp2(lg).astype(BF16), v_ref[g, pl.ds(k0, kb), :])
        return c

    lax.fori_loop(0, n_kt, attend, 0)

    for g in range(N_KV_HEADS):
        acc = acc_ref[g]
        den = jnp.broadcast_to(acc[:, HEAD_DIM:HEAD_DIM + 1], (rows_g, LANES))
        og = acc / den
        for r in range(KV_REP):
            h = g * KV_REP + r
            o_ref[:, h * LANES:(h + 1) * LANES] = og[r * tq:(r + 1) * tq, :].astype(BF16)


def _dsa(stats, q, qi, w, k_all, v_all, ki_all, bias_tiles, bsz, seq, past, n_keys, tq):
    lk = k_all.shape[2]
    nkt = lk // KEY_BLOCK
    topk = min(TOPK_MAX, n_keys // 4)
    nd = bias_tiles.shape[0]
    nq = seq // tq
    assert past % KEY_TILE == 0 and (tq == KEY_TILE or nq == 1)
    w3 = w.reshape(bsz, seq, LANES)
    idx_bits = int(math.ceil(math.log2(lk))) + 1
    kern = functools.partial(_dsa_kernel, bsz=bsz, tq=tq, past=past, n_keys=n_keys, topk=topk, nkt=nkt, nd=nd,
                             idx_bits=idx_bits)
    grid_spec = pltpu.PrefetchScalarGridSpec(
        num_scalar_prefetch=1,
        grid=(bsz, nq),
        in_specs=[
            pl.BlockSpec((None, N_HEADS, tq, LANES), lambda b, i, s: (b, 0, i, 0)),
            pl.BlockSpec((None, IDX_HEADS, tq, LANES), lambda b, i, s: (b, 0, i, 0)),
            pl.BlockSpec((None, tq, LANES), lambda b, i, s: (b, i, 0)),
            pl.BlockSpec((None, N_KV_HEADS, lk, LANES), lambda b, i, s: (b, 0, 0, 0)),
            pl.BlockSpec((None, N_KV_HEADS, lk, LANES), lambda b, i, s: (b, 0, 0, 0)),
            pl.BlockSpec((None, lk, LANES), lambda b, i, s: (b, 0, 0)),
            pl.BlockSpec(bias_tiles.shape, lambda b, i, s: (0, 0, 0, 0)),
        ],
        out_specs=pl.BlockSpec((None, tq, N_HEADS * LANES), lambda b, i, s: (b, i, 0)),
        scratch_shapes=[
            pltpu.VMEM((nkt, tq, KEY_BLOCK), F32),
            pltpu.VMEM((IDX_HEADS * tq, KEY_BLOCK), F32),
            pltpu.VMEM((tq, LANES), F32),
            pltpu.VMEM((tq, LANES), F32),
            pltpu.VMEM((tq, LANES), F32),
            pltpu.VMEM((tq, LANES), F32),
            pltpu.VMEM((tq, LANES), F32),
            pltpu.VMEM((N_HEADS * tq, LANES), F32),
            pltpu.VMEM((N_HEADS * tq, LANES), BF16),
            pltpu.VMEM((N_KV_HEADS, KV_REP * tq, LANES), F32),
        ],
    )
    return pl.pallas_call(
        kern,
        grid_spec=grid_spec,
        out_shape=jax.ShapeDtypeStruct((bsz, seq, N_HEADS * LANES), BF16),
        compiler_params=_cp(("arbitrary", "arbitrary")),
        name="dsa",
    )(stats, q, qi, w3, k_all, v_all, ki_all, bias_tiles)


def _dsa_v1(q, qi, w, k_all, v_all, ki_all, bias_tiles, bsz, seq, past, n_keys, tq):
    lk = k_all.shape[2]
    nkt = lk // KEY_TILE
    topk = min(TOPK_MAX, n_keys // 4)
    nd = bias_tiles.shape[0]
    nq = seq // tq
    w3 = w.reshape(bsz, seq, LANES)
    idx_bits = int(math.ceil(math.log2(lk))) + 1
    kern = functools.partial(_dsa_kernel_v1, tq=tq, past=past, n_keys=n_keys, topk=topk, nkt=nkt, nd=nd,
                             idx_bits=idx_bits)
    return pl.pallas_call(
        kern,
        grid=(bsz, nq),
        in_specs=[
            pl.BlockSpec((None, N_HEADS, tq, LANES), lambda b, i: (b, 0, i, 0)),
            pl.BlockSpec((None, IDX_HEADS, tq, LANES), lambda b, i: (b, 0, i, 0)),
            pl.BlockSpec((None, tq, LANES), lambda b, i: (b, i, 0)),
            pl.BlockSpec((None, N_KV_HEADS, lk, LANES), lambda b, i: (b, 0, 0, 0)),
            pl.BlockSpec((None, N_KV_HEADS, lk, LANES), lambda b, i: (b, 0, 0, 0)),
            pl.BlockSpec((None, lk, LANES), lambda b, i: (b, 0, 0)),
            pl.BlockSpec(bias_tiles.shape, lambda b, i: (0, 0, 0, 0)),
        ],
        out_specs=pl.BlockSpec((None, tq, N_HEADS * LANES), lambda b, i: (b, i, 0)),
        out_shape=jax.ShapeDtypeStruct((bsz, seq, N_HEADS * LANES), BF16),
        scratch_shapes=[
            pltpu.VMEM((nkt, tq, KEY_TILE), F32),
            pltpu.VMEM((IDX_HEADS * tq, KEY_TILE), F32),
            pltpu.VMEM((tq, KEY_TILE), F32),
            pltpu.VMEM((tq, KEY_TILE), I32),
            pltpu.VMEM((tq, KEY_TILE), I32),
            pltpu.VMEM((N_KV_HEADS, KV_REP * tq, LANES), F32),
            pltpu.VMEM((N_KV_HEADS, KV_REP * tq, KEY_TILE), F32),
        ],
        compiler_params=_cp(("arbitrary", "arbitrary")),
        name="dsa",
    )(q, qi, w3, k_all, v_all, ki_all, bias_tiles)


def _merge_kernel(x_ref, ya_ref, at_ref, sga_ref, sgb_ref, wup_ref, wout_ref, g2_ref, wr_hi_ref, wr_lo_ref, br_ref,
                  x1_ref, h2_ref, gt_ref, rt_ref, cnt_ref, run_ref, *, tm, sub):
    step = pl.program_id(0)

    @pl.when(step % sub == 0)
    def _():
        run_ref[...] = jnp.zeros(run_ref.shape, F32)

    yb = _dot(at_ref[...], wup_ref[...])
    merged = sga_ref[...].astype(F32) * ya_ref[...].astype(F32) + sgb_ref[...].astype(F32) * yb
    x1 = x_ref[...] + _dot(merged.astype(BF16), wout_ref[...])
    x1_ref[...] = x1
    ms = jnp.mean(x1 * x1, axis=-1, keepdims=True)
    h2 = x1 * lax.rsqrt(ms + EPS) * g2_ref[...]
    h2_hi, h2_lo = _split(h2)
    h2_ref[...] = h2_hi

    wr_hi = wr_hi_ref[...]
    logit = (_dot_nt(wr_hi, h2_hi) + _dot_nt(wr_hi, h2_lo) + _dot_nt(wr_lo_ref[...], h2_hi)) + br_ref[:, 0:1]
    ne = logit.shape[0]
    eid = lax.broadcasted_iota(I32, (ne, tm), 0).astype(F32)
    selb = jnp.zeros((ne, tm), F32)
    tops = []
    picks = []
    for _ in range(TOP_K):
        mx = jnp.max(logit, axis=0, keepdims=True)
        pick = jnp.min(jnp.where(logit == mx, eid, float(ne)), axis=0, keepdims=True)
        hit = eid == pick
        selb = jnp.where(hit, 1.0, selb)
        logit = jnp.where(hit, -jnp.inf, logit)
        tops.append(mx)
        picks.append(hit)
    ex = [jnp.exp(t - tops[0]) for t in tops]
    den = ex[0] + ex[1] + ex[2] + ex[3]
    gate = jnp.zeros((ne, tm), F32)
    for hit, e in zip(picks, ex):
        gate = jnp.where(hit, e / den, gate)
    gt_ref[...] = gate

    sel = selb > 0.5
    selb = selb.astype(BF16)
    r_i = lax.broadcasted_iota(I32, (tm, tm), 0)
    c_i = lax.broadcasted_iota(I32, (tm, tm), 1)
    tri = jnp.where(r_i < c_i, 1.0, 0.0).astype(BF16)
    run = run_ref[...]
    rank = _dot(selb, tri) + jnp.broadcast_to(run[:, 0:1], (ne, tm))
    rt_ref[...] = jnp.where(sel, rank, -1.0)
    run = run + _dot(selb, jnp.ones((tm, LANES), BF16))
    run_ref[...] = run
    cnt_ref[...] = run


def _merge(x2, ya, attn, sga, sgb, mw, tm, moe_tile):
    t, d = x2.shape
    sub = moe_tile // tm
    ne = mw['wr_hi'].shape[0]

    def tok(i):
        return (i, 0)

    def cst(i):
        return (0, 0)

    consts = [mw['wup'], mw['wout'], mw['g2'], mw['wr_hi'], mw['wr_lo'], mw['br']]
    return pl.pallas_call(
        functools.partial(_merge_kernel, tm=tm, sub=sub),
        grid=(t // tm,),
        in_specs=[
            pl.BlockSpec((tm, d), tok),
            pl.BlockSpec((tm, d), tok),
            pl.BlockSpec((tm, attn.shape[-1]), tok),
            pl.BlockSpec((tm, d), tok),
            pl.BlockSpec((tm, d), tok),
        ] + [pl.BlockSpec(a.shape, cst) for a in consts],
        out_specs=(
            pl.BlockSpec((tm, d), tok),
            pl.BlockSpec((tm, d), tok),
            pl.BlockSpec((ne, tm), lambda i: (0, i)),
            pl.BlockSpec((ne, tm), lambda i: (0, i)),
            pl.BlockSpec((None, ne, LANES), lambda i: (i // sub, 0, 0)),
        ),
        out_shape=(
            jax.ShapeDtypeStruct((t, d), F32),
            jax.ShapeDtypeStruct((t, d), BF16),
            jax.ShapeDtypeStruct((ne, t), F32),
            jax.ShapeDtypeStruct((ne, t), F32),
            jax.ShapeDtypeStruct((t // moe_tile, ne, LANES), F32),
        ),
        scratch_shapes=[pltpu.VMEM((ne, LANES), F32)],
        compiler_params=_cp(("arbitrary",)),
        name="merge",
    )(x2, ya, attn, sga, sgb, *consts)


def _moe_kernel(cnt_ref, h2_ref, x1_ref, gt_ref, rt_ref, wg_ref, wu_ref, wd_ref, bg_ref, bu_ref, bd_ref, y_ref,
                *, tt):
    j = pl.program_id(0)
    e = pl.program_id(1)
    ne = pl.num_programs(1)
    rb = MOE_ROWS

    @pl.when(e == 0)
    def _():
        y_ref[...] = x1_ref[...]

    n_rows = cnt_ref[j * ne + e]
    n_blk = (n_rows + rb - 1) // rb
    g_row = gt_ref[0]
    r_row = rt_ref[0]
    rid = lax.broadcasted_iota(I32, (rb, tt), 0).astype(F32)

    def block(blk, c):
        hit = jnp.broadcast_to(r_row, (rb, tt)) == (rid + (blk * rb).astype(F32))
        p = jnp.where(hit, 1.0, 0.0).astype(BF16)
        xg = _dot(p, h2_ref[...]).astype(BF16)
        a = jnp.minimum(_dot(xg, wg_ref[0]) + bg_ref[0], SWIGLU_LIMIT)
        b = jnp.clip(_dot(xg, wu_ref[0]) + bu_ref[0], -SWIGLU_LIMIT, SWIGLU_LIMIT)
        hid = a * jax.nn.sigmoid(SWIGLU_ALPHA * a) * (b + 1.0)
        o = _dot(hid.astype(BF16), wd_ref[0]) + bd_ref[0]
        g_col = jnp.sum(jnp.where(hit, jnp.broadcast_to(g_row, (rb, tt)), 0.0), axis=1, keepdims=True)
        og = (o * g_col).astype(BF16)
        y_ref[...] += _dot_tn(p, og)
        return c

    lax.fori_loop(0, n_blk, block, 0)


def _moe(h2, x1, gt, rt, cnt, ew, tt):
    t, d = h2.shape
    ne = gt.shape[0]
    nt = t // tt
    f = ew['wg'].shape[-1]
    grid_spec = pltpu.PrefetchScalarGridSpec(
        num_scalar_prefetch=1,
        grid=(nt, ne),
        in_specs=[
            pl.BlockSpec((tt, d), lambda j, e, c: (j, 0)),
            pl.BlockSpec((tt, d), lambda j, e, c: (j, 0)),
            pl.BlockSpec((1, 1, tt), lambda j, e, c: (e, 0, j)),
            pl.BlockSpec((1, 1, tt), lambda j, e, c: (e, 0, j)),
            pl.BlockSpec((1, d, f), lambda j, e, c: (e, 0, 0)),
            pl.BlockSpec((1, d, f), lambda j, e, c: (e, 0, 0)),
            pl.BlockSpec((1, f, d), lambda j, e, c: (e, 0, 0)),
            pl.BlockSpec((1, 1, f), lambda j, e, c: (e, 0, 0)),
            pl.BlockSpec((1, 1, f), lambda j, e, c: (e, 0, 0)),
            pl.BlockSpec((1, 1, d), lambda j, e, c: (e, 0, 0)),
        ],
        out_specs=pl.BlockSpec((tt, d), lambda j, e, c: (j, 0)),
    )
    return pl.pallas_call(
        functools.partial(_moe_kernel, tt=tt),
        grid_spec=grid_spec,
        out_shape=jax.ShapeDtypeStruct((t, d), F32),
        compiler_params=_cp(("arbitrary", "arbitrary")),
        name="moe",
    )(cnt, h2, x1, gt, rt, ew['wg'], ew['wu'], ew['wd'], ew['bg'], ew['bu'], ew['bd'])


def _pad_heads(wmat, n_heads, width):
    d = wmat.shape[0]
    w3 = wmat.reshape(d, n_heads, width)
    return jnp.pad(w3, ((0, 0), (0, 0), (0, LANES - width))).reshape(d, n_heads * LANES)


def _pad_lanes(v, width=LANES):
    v = v.reshape(1, -1)
    return jnp.pad(v, ((0, 0), (0, width - v.shape[1])))


def _rel_bucket(rel):
    half = REL_BUCKETS // 2
    max_exact = half // 2
    n = jnp.abs(rel)
    large = max_exact + (jnp.log(jnp.maximum(n, 1).astype(jnp.float32) / max_exact)
                         / math.log(REL_MAX_DIST / max_exact) * (half - max_exact)).astype(jnp.int32)
    large = jnp.minimum(large, half - 1)
    return jnp.where(rel > 0, half, 0) + jnp.where(n < max_exact, n, large)


def _bias_tiles(rel_bias):
    tk = KEY_TILE
    half = REL_BUCKETS // 2
    max_exact = half // 2
    n_sat = int(math.ceil(max_exact * (REL_MAX_DIST / max_exact) ** ((half - 1 - max_exact) / (half - max_exact)))) + 2
    nd = (n_sat + 2 * tk - 2) // tk + 1
    dd = jnp.arange(nd, dtype=I32)[:, None, None]
    r = jnp.arange(tk, dtype=I32)[None, :, None]
    c = jnp.arange(tk, dtype=I32)[None, None, :]
    bucket = _rel_bucket(c - r - dd * tk)
    tiles = (rel_bias.astype(F32) * LOG2E)[bucket]
    return tiles.transpose(0, 3, 1, 2)


def _prep_proj(norm1_g, w_in, q_norm_g, k_norm_g, idx_k_norm_g, idx_k_norm_b, d_model):
    ssm_w = d_model // 2
    attn_w = N_HEADS * HEAD_DIM
    kv = N_KV_HEADS * HEAD_DIM
    sizes = [ssm_w, attn_w, kv, kv, IDX_HEADS * IDX_DIM, IDX_DIM, IDX_HEADS, d_model, d_model]
    pts = np.cumsum(sizes)[:-1].tolist()
    wu, wq, wk, wv, wqi, wki, wwi, wga, wgb = jnp.split(w_in, pts, axis=1)
    bf = lambda a: a.astype(BF16)
    blk = np.kron(np.eye(N_KV_HEADS), np.ones((HEAD_DIM, HEAD_DIM))) / HEAD_DIM
    return dict(
        g1=norm1_g.reshape(1, -1).astype(F32),
        wu=bf(wu), wq=bf(_pad_heads(wq, N_HEADS, HEAD_DIM)), wk=bf(_pad_heads(wk, N_KV_HEADS, HEAD_DIM)),
        wv=bf(_pad_heads(wv, N_KV_HEADS, HEAD_DIM)), wqi=bf(_pad_heads(wqi, IDX_HEADS, IDX_DIM)),
        wki=bf(_pad_heads(wki, 1, IDX_DIM)), wwi=bf(_pad_heads(wwi, 1, IDX_HEADS)),
        wga=bf(wga), wgb=bf(wgb), wkc=bf(wk), wvc=bf(wv),
        gq=_pad_lanes(q_norm_g.astype(F32)), gk=_pad_lanes(k_norm_g.astype(F32)),
        gkc=jnp.tile(k_norm_g.astype(F32), N_KV_HEADS).reshape(1, -1),
        gi=_pad_lanes(idx_k_norm_g.astype(F32)), bi=_pad_lanes(idx_k_norm_b.astype(F32)),
        ones_h=jnp.full((LANES, LANES), 1.0 / HEAD_DIM, BF16),
        ones_c=jnp.asarray(blk, BF16),
    )


def _prep_s5(lre, lim, log_dt, b_re, b_im, c_re, c_im, dvec, wa, wb):
    g, p = lre.shape
    ch = b_re.shape[-1]
    lam = lax.complex(lre.astype(F32), lim.astype(F32))
    dt = jnp.exp(log_dt.astype(F32))[:, None]
    a_bar = jnp.exp(lam * dt)
    b_bar = ((a_bar - 1.0) / lam)[:, :, None] * lax.complex(b_re.astype(F32), b_im.astype(F32))
    eye = jnp.eye(g, dtype=F32)
    b_r = jnp.einsum('gpc,gh->gchp', jnp.real(b_bar), eye).reshape(g * ch, g * p)
    b_i = jnp.einsum('gpc,gh->gchp', jnp.imag(b_bar), eye).reshape(g * ch, g * p)
    c_r = jnp.einsum('gcp,gh->gphc', c_re.astype(F32), eye).reshape(g * p, g * ch)
    c_i = jnp.einsum('gcp,gh->gphc', c_im.astype(F32), eye).reshape(g * p, g * ch)
    return dict(
        bmat=jnp.concatenate([b_r, b_i], axis=1).astype(BF16),
        cmat=jnp.concatenate([c_r, -c_i], axis=0).astype(BF16),
        a_re=jnp.real(a_bar).reshape(1, g * p), a_im=jnp.imag(a_bar).reshape(1, g * p),
        d=dvec.reshape(1, -1).astype(F32), wa=wa.astype(BF16), wb=wb.astype(BF16),
    )


def _prep_merge(w_attn_up, w_out, norm2_g, w_router, b_router):
    d = w_attn_up.shape[1]
    wup = jnp.pad(w_attn_up.reshape(N_HEADS, HEAD_DIM, d), ((0, 0), (0, LANES - HEAD_DIM), (0, 0)))
    wr_t = w_router.astype(F32).T
    wr_hi = wr_t.astype(BF16)
    wr_lo = (wr_t - wr_hi.astype(F32)).astype(BF16)
    return dict(
        wup=wup.reshape(N_HEADS * LANES, d).astype(BF16), wout=w_out.astype(BF16),
        g2=norm2_g.reshape(1, -1).astype(F32), wr_hi=wr_hi, wr_lo=wr_lo,
        br=jnp.broadcast_to(b_router.astype(F32)[:, None], (b_router.shape[0], LANES)),
    )


def _prep_moe(wg, bg, wu, bu, wd, bd):
    return dict(wg=wg.astype(BF16), wu=wu.astype(BF16), wd=wd.astype(BF16),
                bg=bg.astype(F32)[:, None, :], bu=bu.astype(F32)[:, None, :], bd=bd.astype(F32)[:, None, :])


def _pick_tile(n, pref):
    t = min(n, pref)
    while n % t:
        t //= 2
    return t


def _pad_keys(a, axis, lk):
    pad = [(0, 0)] * a.ndim
    pad[axis] = (0, lk - a.shape[axis])
    return jnp.pad(a, pad)


def _trunk_layer(x, past_k, past_v, past_ik, h0_re, h0_im, pw, sw, mw, ew, bias_tiles):
    bsz, seq, d = x.shape
    t = bsz * seq
    tm = _pick_tile(seq, 512)
    u_tb, q, kp, vp, qi, kip, w, sga, sgb, kc, vc, kic = _proj(x, pw, bsz, seq, tm)

    half = sw['a_re'].shape[1]
    if h0_re is None:
        h0 = jnp.zeros((bsz, 2 * half), F32)
    else:
        h0 = jnp.concatenate([h0_re.reshape(bsz, half), h0_im.reshape(bsz, half)], axis=1).astype(F32)
    tc = _pick_tile(seq, max(1, 512 // bsz))
    ya_tb, hout = _s5(u_tb, h0, sw, bsz, seq, tc)
    groups = half // SSM_STATE
    s_re = hout[:, :half].reshape(bsz, groups, SSM_STATE)
    s_im = hout[:, half:].reshape(bsz, groups, SSM_STATE)

    past = 0 if past_k is None else past_k.shape[1]
    n_keys = past + seq
    lk = -(-n_keys // KEY_BLOCK) * KEY_BLOCK
    kip3 = kip.reshape(bsz, seq, LANES)
    if past:
        lane = jnp.arange(LANES)
        pk = jnp.pad(past_k.astype(F32), ((0, 0), (0, 0), (0, 0), (0, LANES - HEAD_DIM)))
        pk = jnp.where(lane == HEAD_DIM, 1.0, pk).astype(BF16)
        pv = jnp.pad(past_v.astype(F32), ((0, 0), (0, 0), (0, 0), (0, LANES - HEAD_DIM)))
        pv = jnp.where(lane == HEAD_DIM, 1.0, pv).astype(BF16)
        pik = jnp.pad(past_ik.astype(F32), ((0, 0), (0, 0), (0, LANES - IDX_DIM))).astype(BF16)
        k_all = jnp.concatenate([pk.transpose(0, 2, 1, 3), kp], axis=2)
        v_all = jnp.concatenate([pv.transpose(0, 2, 1, 3), vp], axis=2)
        ki_all = jnp.concatenate([pik, kip3], axis=1)
    else:
        k_all, v_all, ki_all = kp, vp, kip3
    k_all = _pad_keys(k_all, 2, lk)
    v_all = _pad_keys(v_all, 2, lk)
    ki_all = _pad_keys(ki_all, 1, lk)
    tq = _pick_tile(seq, KEY_TILE)
    kf = k_all[..., :HEAD_DIM].astype(F32)
    kmax = jnp.sqrt(jnp.max(jnp.sum(kf * kf, axis=-1), axis=-1)).reshape(-1)
    bmax = jnp.max(jnp.abs(bias_tiles), axis=(0, 2, 3))
    stats = jnp.concatenate([kmax, bmax]).astype(F32)
    attn = _dsa(stats, q, qi, w, k_all, v_all, ki_all, bias_tiles, bsz, seq, past, n_keys, tq)

    moe_tile = _pick_tile(t, 1024)
    tm2 = _pick_tile(moe_tile, 512)
    ya = ya_tb.reshape(seq, bsz, d).transpose(1, 0, 2).reshape(t, d)
    x1, h2, gt, rt, cnt = _merge(x.reshape(t, d), ya, attn.reshape(t, attn.shape[-1]), sga, sgb, mw, tm2, moe_tile)
    cnt_i = cnt[:, :, 0].astype(I32).reshape(-1)
    ne = gt.shape[0]
    y = _moe(h2, x1, gt.reshape(ne, 1, t), rt.reshape(ne, 1, t), cnt_i, ew, moe_tile)

    k_new = kc.reshape(bsz, seq, N_KV_HEADS, HEAD_DIM)
    v_new = vc.reshape(bsz, seq, N_KV_HEADS, HEAD_DIM)
    ik_new = kic.reshape(bsz, seq, IDX_DIM)
    return y.reshape(bsz, seq, d), k_new, v_new, ik_new, s_re, s_im


def kernel(x_prompt, x_sample, cache_k, cache_v, cache_idx_k, state_ssm_re, state_ssm_im, rel_bias, norm1_g, w_in, ssm_lambda_re, ssm_lambda_im, ssm_log_dt, ssm_b_re, ssm_b_im, ssm_c_re, ssm_c_im, ssm_d, ssm_w_glu_a, ssm_w_glu_b, q_norm_g, k_norm_g, idx_k_norm_g, idx_k_norm_b, w_attn_up, w_out, norm2_g, moe_w_router, moe_b_router, moe_w_gate, moe_b_gate, moe_w_up, moe_b_up, moe_w_down, moe_b_down):
    depth = w_in.shape[0]
    d_model = x_prompt.shape[-1]
    bias_tiles = _bias_tiles(rel_bias)
    xp, xs = x_prompt, x_sample
    st_p, st_s = [], []
    for l in range(depth):
        pw = _prep_proj(norm1_g[l], w_in[l], q_norm_g[l], k_norm_g[l], idx_k_norm_g[l], idx_k_norm_b[l], d_model)
        sw = _prep_s5(ssm_lambda_re[l], ssm_lambda_im[l], ssm_log_dt[l], ssm_b_re[l], ssm_b_im[l], ssm_c_re[l],
                      ssm_c_im[l], ssm_d[l], ssm_w_glu_a[l], ssm_w_glu_b[l])
        mw = _prep_merge(w_attn_up[l], w_out[l], norm2_g[l], moe_w_router[l], moe_b_router[l])
        ew = _prep_moe(moe_w_gate[l], moe_b_gate[l], moe_w_up[l], moe_b_up[l], moe_w_down[l], moe_b_down[l])
        xp, *sp = _trunk_layer(xp, None, None, None, None, None, pw, sw, mw, ew, bias_tiles)
        xs, *ss = _trunk_layer(xs, cache_k[l], cache_v[l], cache_idx_k[l], state_ssm_re[l], state_ssm_im[l],
                               pw, sw, mw, ew, bias_tiles)
        st_p.append(sp)
        st_s.append(ss)
    outs_p = [jnp.stack([s[i] for s in st_p]) for i in range(5)]
    outs_s = [jnp.stack([s[i] for s in st_s]) for i in range(5)]
    return (xp, xs, *outs_p, *outs_s)
```

```python
import functools
import math

import numpy as np
import jax
import jax.numpy as jnp
from jax import lax
from jax.experimental import pallas as pl
from jax.experimental.pallas import tpu as pltpu

F32 = jnp.float32
BF16 = jnp.bfloat16
I32 = jnp.int32

LANES = 128
SUBLANES = 8
VMEM_LIMIT = 56 * 1024 * 1024

CHUNK = 64
SSM_GROUP_CH = 16
SSM_STATE = 64
N_HEADS = 8
HEAD_DIM = 64
N_KV_HEADS = 2
KV_REP = N_HEADS // N_KV_HEADS
IDX_HEADS = 8
IDX_DIM = 64
TOPK_MAX = 256
REL_BUCKETS = 32
REL_MAX_DIST = 1024
N_EXPERTS = 32
TOP_K = 4
SWIGLU_LIMIT = 7.0
SWIGLU_ALPHA = 1.702
EPS = 1e-6

KEY_TILE = 128
KEY_BLOCK = 256
LOG2E = math.log2(math.e)
NEG_BIG = -1e30
SHIFT_LIMIT = 30.0
SEARCH_GROUP = 5
MOE_ROWS = 128


def _cp(sem):
    return pltpu.CompilerParams(dimension_semantics=sem, vmem_limit_bytes=VMEM_LIMIT)


def _dot(a, b):
    return jnp.dot(a, b, preferred_element_type=F32)


def _dot_nt(a, b):
    return lax.dot_general(a, b, (((1,), (1,)), ((), ())), preferred_element_type=F32)


def _dot_tn(a, b):
    return lax.dot_general(a, b, (((0,), (0,)), ((), ())), preferred_element_type=F32)


def _split(a):
    hi = a.astype(BF16)
    lo = (a - hi.astype(F32)).astype(BF16)
    return hi, lo


def _dot_split(a, g):
    hi, lo = _split(a)
    return _dot(hi, g) + _dot(lo, g)


def _proj_kernel(x_ref, g1_ref, wu_ref, wq_ref, wk_ref, wv_ref, wqi_ref, wki_ref, wwit_ref, wga_ref, wgb_ref,
                 wkc_ref, wvc_ref, gq_ref, gk_ref, gkc_ref, gi_ref, bi_ref, ones_h_ref, ones_c_ref,
                 u_ref, q_ref, kp_ref, vp_ref, qi_ref, kip_ref, wt_ref, sga_ref, sgb_ref,
                 kc_ref, vc_ref, kic_ref):
    x = x_ref[...]
    ms = jnp.mean(x * x, axis=-1, keepdims=True)
    hn = (x * lax.rsqrt(ms + EPS) * g1_ref[...]).astype(BF16)
    ones_h = ones_h_ref[...]
    lane = lax.broadcasted_iota(I32, (x.shape[0], LANES), 1)

    u_ref[...] = _dot(hn, wu_ref[...]).astype(BF16)

    q = _dot(hn, wq_ref[...])
    scale = HEAD_DIM ** -0.5 * LOG2E
    for h in range(N_HEADS):
        qh = q[:, h * LANES:(h + 1) * LANES]
        msq = _dot_split(qh * qh, ones_h)
        q_ref[h] = (qh * lax.rsqrt(msq + EPS) * (gq_ref[...] * scale)).astype(BF16)

    k = _dot(hn, wk_ref[...])
    for g in range(N_KV_HEADS):
        kg = k[:, g * LANES:(g + 1) * LANES]
        msk = _dot_split(kg * kg, ones_h)
        kn = kg * lax.rsqrt(msk + EPS) * gk_ref[...]
        kp_ref[g] = jnp.where(lane == HEAD_DIM, 1.0, kn).astype(BF16)

    v = _dot(hn, wv_ref[...])
    for g in range(N_KV_HEADS):
        vg = v[:, g * LANES:(g + 1) * LANES]
        vp_ref[g] = jnp.where(lane == HEAD_DIM, 1.0, vg).astype(BF16)

    qi = _dot(hn, wqi_ref[...])
    for h in range(IDX_HEADS):
        qi_ref[h] = qi[:, h * LANES:(h + 1) * LANES].astype(BF16)

    ki = _dot(hn, wki_ref[...])
    mu = _dot_split(ki, ones_h)
    xc = jnp.where(lane < IDX_DIM, ki - mu, 0.0)
    var = _dot_split(xc * xc, ones_h)
    kin = xc * lax.rsqrt(var + EPS) * gi_ref[...] + bi_ref[...]
    kip_ref[...] = kin.astype(BF16)
    kic_ref[...] = kin[:, :IDX_DIM]

    wt = _dot_nt(wwit_ref[...], hn)
    wt_ref[...] = wt[0:IDX_HEADS, :] * (IDX_HEADS ** -0.5 * IDX_DIM ** -0.5)

    sga_ref[...] = jax.nn.sigmoid(_dot(hn, wga_ref[...])).astype(BF16)
    sgb_ref[...] = jax.nn.sigmoid(_dot(hn, wgb_ref[...])).astype(BF16)

    kc = _dot(hn, wkc_ref[...])
    mskc = _dot_split(kc * kc, ones_c_ref[...])
    kc_ref[...] = kc * lax.rsqrt(mskc + EPS) * gkc_ref[...]
    vc_ref[...] = _dot(hn, wvc_ref[...])


def _proj(x, pw, bsz, seq, tm):
    d = x.shape[-1]
    nt = seq // tm
    t = bsz * seq
    x2 = x.reshape(t, d)

    def tok(b, i):
        return (b * nt + i, 0)

    def cst(b, i):
        return (0, 0)

    def wspec(a):
        return pl.BlockSpec(a.shape, cst)

    weights = [pw['g1'], pw['wu'], pw['wq'], pw['wk'], pw['wv'], pw['wqi'], pw['wki'], pw['wwit'], pw['wga'],
               pw['wgb'], pw['wkc'], pw['wvc'], pw['gq'], pw['gk'], pw['gkc'], pw['gi'], pw['bi'],
               pw['ones_h'], pw['ones_c']]
    ssm_w = pw['wu'].shape[1]
    out_shape = (
        jax.ShapeDtypeStruct((seq, bsz * ssm_w), BF16),
        jax.ShapeDtypeStruct((bsz, N_HEADS, seq, LANES), BF16),
        jax.ShapeDtypeStruct((bsz, N_KV_HEADS, seq, LANES), BF16),
        jax.ShapeDtypeStruct((bsz, N_KV_HEADS, seq, LANES), BF16),
        jax.ShapeDtypeStruct((bsz, IDX_HEADS, seq, LANES), BF16),
        jax.ShapeDtypeStruct((t, LANES), BF16),
        jax.ShapeDtypeStruct((bsz, IDX_HEADS, seq), F32),
        jax.ShapeDtypeStruct((t, d), BF16),
        jax.ShapeDtypeStruct((t, d), BF16),
        jax.ShapeDtypeStruct((t, N_KV_HEADS * HEAD_DIM), F32),
        jax.ShapeDtypeStruct((t, N_KV_HEADS * HEAD_DIM), F32),
        jax.ShapeDtypeStruct((t, IDX_DIM), F32),
    )

    def hm(nh):
        return pl.BlockSpec((None, nh, tm, LANES), lambda b, i: (b, 0, i, 0))

    out_specs = (
        pl.BlockSpec((tm, ssm_w), lambda b, i: (i, b)),
        hm(N_HEADS), hm(N_KV_HEADS), hm(N_KV_HEADS), hm(IDX_HEADS),
        pl.BlockSpec((tm, LANES), tok),
        pl.BlockSpec((None, IDX_HEADS, tm), lambda b, i: (b, 0, i)),
        pl.BlockSpec((tm, d), tok), pl.BlockSpec((tm, d), tok),
        pl.BlockSpec((tm, N_KV_HEADS * HEAD_DIM), tok), pl.BlockSpec((tm, N_KV_HEADS * HEAD_DIM), tok),
        pl.BlockSpec((tm, IDX_DIM), tok),
    )
    return pl.pallas_call(
        _proj_kernel,
        grid=(bsz, nt),
        in_specs=[pl.BlockSpec((tm, d), tok)] + [wspec(a) for a in weights],
        out_specs=out_specs,
        out_shape=out_shape,
        compiler_params=_cp(("arbitrary", "arbitrary")),
        name="proj",
    )(x2, *weights)


def _gelu_tanh(x):
    return 0.5 * x * (1.0 + jnp.tanh(math.sqrt(2.0 / math.pi) * (x + 0.044715 * (x * x * x))))


def _s5_kernel(u_ref, h0_ref, bmat_ref, are_ref, aim_ref, cmat_ref, dvec_ref, wa_ref, wb_ref,
               ya_ref, hout_ref, state_ref, bu_ref, *, bsz, tc, strip):
    s = pl.program_id(0)
    half = are_ref.shape[1]

    @pl.when(s == 0)
    def _():
        state_ref[...] = h0_ref[...]

    u = u_ref[...]
    bu_ref[...] = _dot(u, bmat_ref[...])

    for c0 in range(0, half, strip):
        ar = jnp.broadcast_to(are_ref[:, c0:c0 + strip], (bsz, strip))
        ai = jnp.broadcast_to(aim_ref[:, c0:c0 + strip], (bsz, strip))
        hr0 = state_ref[:, c0:c0 + strip]
        hi0 = state_ref[:, half + c0:half + c0 + strip]

        def step(t, carry):
            hr, hi = carry
            r0 = pl.multiple_of(t * bsz, bsz)
            br = bu_ref[pl.ds(r0, bsz), c0:c0 + strip]
            bi = bu_ref[pl.ds(r0, bsz), half + c0:half + c0 + strip]
            nr = ar * hr - ai * hi + br
            ni = ar * hi + ai * hr + bi
            bu_ref[pl.ds(r0, bsz), c0:c0 + strip] = nr
            bu_ref[pl.ds(r0, bsz), half + c0:half + c0 + strip] = ni
            return nr, ni

        hr, hi = lax.fori_loop(0, tc, step, (hr0, hi0))
        state_ref[:, c0:c0 + strip] = hr
        state_ref[:, half + c0:half + c0 + strip] = hi

    y = _dot(bu_ref[...].astype(BF16), cmat_ref[...]) + dvec_ref[...] * u.astype(F32)
    g = _gelu_tanh(y).astype(BF16)
    ya = _dot(g, wa_ref[...]) * jax.nn.sigmoid(_dot(g, wb_ref[...]))
    ya_ref[...] = ya.astype(BF16)

    @pl.when(s == pl.num_programs(0) - 1)
    def _():
        hout_ref[...] = state_ref[...]


def _s5(u_tb, h0, sw, bsz, seq, tc):
    rows = tc * bsz
    ssm_w = sw['bmat'].shape[0]
    two_half = sw['bmat'].shape[1]
    half = two_half // 2
    d = sw['wa'].shape[1]
    u2 = u_tb.reshape(seq * bsz, ssm_w)
    strip = min(512, half)

    def cst(s):
        return (0, 0)

    consts = [h0, sw['bmat'], sw['a_re'], sw['a_im'], sw['cmat'], sw['d'], sw['wa'], sw['wb']]
    ya, hout = pl.pallas_call(
        functools.partial(_s5_kernel, bsz=bsz, tc=tc, strip=strip),
        grid=(seq // tc,),
        in_specs=[pl.BlockSpec((rows, ssm_w), lambda s: (s, 0))] + [pl.BlockSpec(a.shape, cst) for a in consts],
        out_specs=(pl.BlockSpec((rows, d), lambda s: (s, 0)), pl.BlockSpec((bsz, two_half), cst)),
        out_shape=(jax.ShapeDtypeStruct((seq * bsz, d), BF16), jax.ShapeDtypeStruct((bsz, two_half), F32)),
        scratch_shapes=[pltpu.VMEM((bsz, two_half), F32), pltpu.VMEM((rows, two_half), F32)],
        compiler_params=_cp(("arbitrary",)),
        name="s5",
    )(u2, *consts)
    return ya.reshape(seq, bsz * d), hout


def _f2key(x):
    b = lax.bitcast_convert_type(x, I32)
    return b ^ ((b >> 31) & 0x7FFFFFFF)


def _key2f(k):
    return lax.bitcast_convert_type(k ^ ((k >> 31) & 0x7FFFFFFF), F32)


def _dsa_kernel(st_ref, q_ref, qi_ref, wt_ref, k_ref, vt_ref, ki_ref, bias_ref, o_ref,
                s_ref, lo_ref, hi_ref, clo_ref, glo_ref, ghi_ref, side_ref, q2_ref, mrow_ref, acc_ref,
                *, bsz, tq, past, n_keys, topk, nkt, nd, idx_bits):
    b_id = pl.program_id(0)
    i = pl.program_id(1)
    kb = KEY_BLOCK
    sl = SUBLANES
    q0 = past + i * tq
    last_chunk = (q0 + tq - 1) // CHUNK
    n_kt = jnp.minimum(nkt, ((last_chunk + 1) * CHUNK + kb - 1) // kb)
    d0 = q0 // KEY_TILE

    krow = lax.broadcasted_iota(I32, (kb, tq), 0)
    q_chunk = (q0 + lax.broadcasted_iota(I32, (kb, tq), 1)) // CHUNK
    qc8 = (q0 + lax.broadcasted_iota(I32, (sl, tq), 1)) // CHUNK
    n_adm = jnp.minimum((qc8 + 1) * CHUNK, n_keys)
    n_admf = n_adm.astype(F32)
    needf = jnp.minimum(topk, n_adm).astype(F32)

    def bcast(x):
        return jnp.broadcast_to(x[0:1, :], (kb, tq))

    def rep(x):
        return jnp.broadcast_to(x, (sl, tq))

    qi = qi_ref[...].reshape(IDX_HEADS * tq, LANES)

    def score_block(kt):
        k0 = pl.multiple_of(kt * kb, kb)
        s = _dot_nt(ki_ref[pl.ds(k0, kb), :], qi)
        sc = jnp.zeros((kb, tq), F32)
        for h in range(IDX_HEADS):
            sc = sc + wt_ref[h:h + 1, :] * jnp.maximum(s[:, h * tq:(h + 1) * tq], 0.0)
        kpos = k0 + krow
        adm = ((kpos // CHUNK) <= q_chunk) & (kpos < n_keys)
        s_ref[kt] = jnp.where(adm, sc, -jnp.inf)

    def score_pair(j, c):
        score_block(2 * j)
        score_block(2 * j + 1)
        return c

    lax.fori_loop(0, n_kt // 2, score_pair, 0)

    @pl.when(n_kt % 2 == 1)
    def _():
        score_block(n_kt - 1)

    part = 4 * sl

    def fold(x, op):
        x = x.reshape(kb // part, part, tq)
        acc = x[0]
        for j in range(1, kb // part):
            acc = op(acc, x[j])
        return acc

    def count(pred):
        def body(kt, c):
            return c + fold(jnp.where(pred(s_ref[kt], kt), 1.0, 0.0), jnp.add)
        c = lax.fori_loop(0, n_kt, body, jnp.zeros((part, tq), F32))
        return rep(jnp.sum(c, axis=0, keepdims=True))

    def minmax(kt, c):
        mx, mn = c
        s = s_ref[kt]
        return (jnp.maximum(mx, fold(s, jnp.maximum)),
                jnp.minimum(mn, fold(jnp.where(s == -jnp.inf, jnp.inf, s), jnp.minimum)))

    mx, mn = lax.fori_loop(0, n_kt, minmax,
                           (jnp.full((part, tq), -jnp.inf, F32), jnp.full((part, tq), jnp.inf, F32)))
    lo_ref[...] = rep(jnp.min(mn, axis=0, keepdims=True))
    hi_ref[...] = _key2f(_f2key(rep(jnp.max(mx, axis=0, keepdims=True))) + 1)
    clo_ref[...] = n_admf
    glo_ref[...] = n_admf - needf + 0.5
    ghi_ref[...] = 0.0 - needf + 0.5
    side_ref[...] = jnp.zeros((sl, tq), F32)

    def searching(lo, hi, clo):
        return (_f2key(hi) > _f2key(lo) + 1) & (clo > needf)

    def refine(it, c):
        lo, hi, clo = lo_ref[...], hi_ref[...], clo_ref[...]
        glo, ghi, side = glo_ref[...], ghi_ref[...], side_ref[...]
        k_t = _f2key(lo + (hi - lo) * (glo / (glo - ghi)))
        t = _key2f(jnp.minimum(jnp.maximum(k_t, _f2key(lo) + 1), _f2key(hi) - 1))
        tb = bcast(t)
        cnt = count(lambda s, kt: s >= tb)
        g = cnt - needf + 0.5
        open_ = searching(lo, hi, clo)
        up = open_ & (g > 0.0)
        dn = open_ & (g < 0.0)
        lo_ref[...] = jnp.where(up, t, lo)
        clo_ref[...] = jnp.where(up, cnt, clo)
        hi_ref[...] = jnp.where(dn, t, hi)
        glo_ref[...] = jnp.where(up, g, jnp.where(dn & (side < 0.0), glo * 0.5, glo))
        ghi_ref[...] = jnp.where(dn, g, jnp.where(up & (side > 0.0), ghi * 0.5, ghi))
        side_ref[...] = jnp.where(up, 1.0, jnp.where(dn, -1.0, side))
        return c

    def snap():
        lo, hi, clo = lo_ref[...], hi_ref[...], clo_ref[...]
        lo_b, hi_b = bcast(lo), bcast(hi)

        def body(kt, c):
            a, b = c
            s = s_ref[kt]
            return (jnp.minimum(a, fold(jnp.where(s >= lo_b, s, jnp.inf), jnp.minimum)),
                    jnp.maximum(b, fold(jnp.where(s < hi_b, s, -jnp.inf), jnp.maximum)))

        a, b = lax.fori_loop(0, n_kt, body,
                             (jnp.full((part, tq), jnp.inf, F32), jnp.full((part, tq), -jnp.inf, F32)))
        open_ = searching(lo, hi, clo)
        lo_ref[...] = jnp.where(open_, rep(jnp.min(a, axis=0, keepdims=True)), lo)
        hi_ref[...] = jnp.where(open_, _key2f(_f2key(rep(jnp.max(b, axis=0, keepdims=True))) + 1), hi)

    def n_searching():
        return jnp.max(jnp.where(searching(lo_ref[...], hi_ref[...], clo_ref[...]), 1.0, 0.0))

    def group(c):
        grp, _ = c
        lax.fori_loop(0, SEARCH_GROUP, refine, 0)

        @pl.when(grp >= 1)
        def _():
            snap()
        return grp + 1, n_searching()

    lax.while_loop(lambda c: c[1] > 0.0, group, (jnp.int32(0), n_searching()))
    thr = lo_ref[...]
    thr_b = bcast(thr)

    n_tied = jnp.max(jnp.where(clo_ref[...] > needf, 1.0, 0.0))

    @pl.when(n_tied > 0.0)
    def _():
        rem = needf - count(lambda s, kt: s > thr_b)
        lo_ref[...] = jnp.zeros((sl, tq), F32)
        hi_ref[...] = jnp.full((sl, tq), float(nkt * kb), F32)
        krowf = krow.astype(F32)

        def bisect_idx(it, c):
            lo = lo_ref[...]
            hi = hi_ref[...]
            mid = jnp.floor((lo + hi) * 0.5)
            mid_b = bcast(mid)
            ok = count(lambda s, kt: (s == thr_b) & ((kt * kb).astype(F32) + krowf < mid_b)) >= rem
            hi_ref[...] = jnp.where(ok, mid, hi)
            lo_ref[...] = jnp.where(ok, lo, mid)
            return c

        lax.fori_loop(0, idx_bits, bisect_idx, 0)
        cut_b = bcast(hi_ref[...])

        def drop(kt, c):
            s = s_ref[kt]
            s_ref[kt] = jnp.where((s == thr_b) & ((kt * kb).astype(F32) + krowf >= cut_b), -jnp.inf, s)
            return c

        lax.fori_loop(0, n_kt, drop, 0)

    rows_g = KV_REP * tq
    qf = q_ref[...].reshape(N_HEADS * tq, LANES).astype(F32)
    qn = jnp.sqrt(jnp.sum(qf * qf, axis=1, keepdims=True))
    lane = lax.broadcasted_iota(I32, (tq, LANES), 1)
    worst = jnp.float32(0.0)
    for h in range(N_HEADS):
        kmax = st_ref[b_id * N_KV_HEADS + h // KV_REP]
        bmax = st_ref[bsz * N_KV_HEADS + h]
        bound = qn[h * tq:(h + 1) * tq, :] * (kmax * 1.01) + (bmax + 0.1)
        worst = jnp.maximum(worst, jnp.max(bound))
        q2_ref[h * tq:(h + 1) * tq, :] = jnp.where(lane == HEAD_DIM, -bound, qf[h * tq:(h + 1) * tq, :]).astype(BF16)
    mrow_ref[...] = jnp.zeros(mrow_ref.shape, F32)

    def logits(kt, g):
        k0 = pl.multiple_of(kt * kb, kb)
        maskadd = jnp.where(s_ref[kt] >= thr_b, 0.0, NEG_BIG)
        da = jnp.clip(d0 - 2 * kt, 0, nd - 1)
        db = jnp.clip(d0 - 2 * kt - 1, 0, nd - 1)
        s = _dot_nt(k_ref[g, pl.ds(k0, kb), :], q2_ref[g * rows_g:(g + 1) * rows_g, :])
        parts = []
        for r in range(KV_REP):
            h = g * KV_REP + r
            bias = jnp.concatenate([bias_ref[da, h], bias_ref[db, h]], axis=0)
            shift = maskadd - mrow_ref[0:1, h * tq:(h + 1) * tq]
            parts.append(s[:, r * tq:(r + 1) * tq] + (bias + shift))
        return jnp.concatenate(parts, axis=1)

    @pl.when(worst > SHIFT_LIMIT)
    def _():
        def col_max(kt, c):
            return tuple(jnp.maximum(c[g], logits(kt, g)) for g in range(N_KV_HEADS))
        init = tuple(jnp.full((kb, rows_g), NEG_BIG, F32) for _ in range(N_KV_HEADS))
        mxs = lax.fori_loop(0, n_kt, col_max, init)
        for g in range(N_KV_HEADS):
            mrow_ref[:, g * rows_g:(g + 1) * rows_g] = jnp.broadcast_to(
                jnp.max(mxs[g], axis=0, keepdims=True), (sl, rows_g))

    acc_ref[...] = jnp.zeros(acc_ref.shape, F32)

    def attend(kts):
        for g in range(N_KV_HEADS):
            pv = None
            for kt in kts:
                p = jnp.exp2(logits(kt, g)).astype(BF16)
                d = _dot(vt_ref[g, kt], p)
                pv = d if pv is None else pv + d
            acc_ref[g] += pv

    def attend_pair(j, c):
        attend((2 * j, 2 * j + 1))
        return c

    lax.fori_loop(0, n_kt // 2, attend_pair, 0)

    @pl.when(n_kt % 2 == 1)
    def _():
        attend((n_kt - 1,))

    for g in range(N_KV_HEADS):
        acc = acc_ref[g]
        og = acc / acc[HEAD_DIM:HEAD_DIM + 1, :]
        for r in range(KV_REP):
            h = g * KV_REP + r
            o_ref[:, h * LANES:(h + 1) * LANES] = og[:, r * tq:(r + 1) * tq].T.astype(BF16)


def _dsa(stats, q, qi, wt, k_all, vt_all, ki_all, bias_tiles, bsz, seq, past, n_keys, tq):
    lk = k_all.shape[2]
    nkt = lk // KEY_BLOCK
    topk = min(TOPK_MAX, n_keys // 4)
    nd = bias_tiles.shape[0]
    nq = seq // tq
    assert past % KEY_TILE == 0 and tq == KEY_TILE
    idx_bits = int(math.ceil(math.log2(lk))) + 1
    kern = functools.partial(_dsa_kernel, bsz=bsz, tq=tq, past=past, n_keys=n_keys, topk=topk, nkt=nkt, nd=nd,
                             idx_bits=idx_bits)
    row_state = pltpu.VMEM((SUBLANES, tq), F32)
    grid_spec = pltpu.PrefetchScalarGridSpec(
        num_scalar_prefetch=1,
        grid=(bsz, nq),
        in_specs=[
            pl.BlockSpec((None, N_HEADS, tq, LANES), lambda b, i, s: (b, 0, i, 0)),
            pl.BlockSpec((None, IDX_HEADS, tq, LANES), lambda b, i, s: (b, 0, i, 0)),
            pl.BlockSpec((None, IDX_HEADS, tq), lambda b, i, s: (b, 0, i)),
            pl.BlockSpec((None, N_KV_HEADS, lk, LANES), lambda b, i, s: (b, 0, 0, 0)),
            pl.BlockSpec((None, N_KV_HEADS, nkt, LANES, KEY_BLOCK), lambda b, i, s: (b, 0, 0, 0, 0)),
            pl.BlockSpec((None, lk, LANES), lambda b, i, s: (b, 0, 0)),
            pl.BlockSpec(bias_tiles.shape, lambda b, i, s: (0, 0, 0, 0)),
        ],
        out_specs=pl.BlockSpec((None, tq, N_HEADS * LANES), lambda b, i, s: (b, i, 0)),
        scratch_shapes=[
            pltpu.VMEM((nkt, KEY_BLOCK, tq), F32),
            row_state, row_state, row_state, row_state, row_state, row_state,
            pltpu.VMEM((N_HEADS * tq, LANES), BF16),
            pltpu.VMEM((SUBLANES, N_HEADS * tq), F32),
            pltpu.VMEM((N_KV_HEADS, LANES, KV_REP * tq), F32),
        ],
    )
    return pl.pallas_call(
        kern,
        grid_spec=grid_spec,
        out_shape=jax.ShapeDtypeStruct((bsz, seq, N_HEADS * LANES), BF16),
        compiler_params=_cp(("arbitrary", "arbitrary")),
        name="dsa",
    )(stats, q, qi, wt, k_all, vt_all, ki_all, bias_tiles)


def _merge_kernel(x_ref, ya_ref, at_ref, sga_ref, sgb_ref, wup_ref, wout_ref, g2_ref, wr_hi_ref, wr_lo_ref, br_ref,
                  x1_ref, h2_ref, gt_ref, rt_ref, cnt_ref, run_ref, *, tm, sub):
    step = pl.program_id(0)

    @pl.when(step % sub == 0)
    def _():
        run_ref[...] = jnp.zeros(run_ref.shape, F32)

    yb = _dot(at_ref[...], wup_ref[...])
    merged = sga_ref[...].astype(F32) * ya_ref[...].astype(F32) + sgb_ref[...].astype(F32) * yb
    x1 = x_ref[...] + _dot(merged.astype(BF16), wout_ref[...])
    x1_ref[...] = x1
    ms = jnp.mean(x1 * x1, axis=-1, keepdims=True)
    h2 = x1 * lax.rsqrt(ms + EPS) * g2_ref[...]
    h2_hi, h2_lo = _split(h2)
    h2_ref[...] = h2_hi

    wr_hi = wr_hi_ref[...]
    logit = (_dot_nt(wr_hi, h2_hi) + _dot_nt(wr_hi, h2_lo) + _dot_nt(wr_lo_ref[...], h2_hi)) + br_ref[:, 0:1]
    ne = logit.shape[0]
    eid = lax.broadcasted_iota(I32, (ne, tm), 0).astype(F32)
    selb = jnp.zeros((ne, tm), F32)
    tops = []
    picks = []
    for _ in range(TOP_K):
        mx = jnp.max(logit, axis=0, keepdims=True)
        pick = jnp.min(jnp.where(logit == mx, eid, float(ne)), axis=0, keepdims=True)
        hit = eid == pick
        selb = jnp.where(hit, 1.0, selb)
        logit = jnp.where(hit, -jnp.inf, logit)
        tops.append(mx)
        picks.append(hit)
    ex = [jnp.exp(t - tops[0]) for t in tops]
    den = ex[0] + ex[1] + ex[2] + ex[3]
    gate = jnp.zeros((ne, tm), F32)
    for hit, e in zip(picks, ex):
        gate = jnp.where(hit, e / den, gate)
    gt_ref[...] = gate

    sel = selb > 0.5
    selb = selb.astype(BF16)
    r_i = lax.broadcasted_iota(I32, (tm, tm), 0)
    c_i = lax.broadcasted_iota(I32, (tm, tm), 1)
    tri = jnp.where(r_i < c_i, 1.0, 0.0).astype(BF16)
    run = run_ref[...]
    rank = _dot(selb, tri) + jnp.broadcast_to(run[:, 0:1], (ne, tm))
    rt_ref[...] = jnp.where(sel, rank, -1.0)
    run = run + _dot(selb, jnp.ones((tm, LANES), BF16))
    run_ref[...] = run
    cnt_ref[...] = run


def _merge(x2, ya, attn, sga, sgb, mw, tm, moe_tile):
    t, d = x2.shape
    sub = moe_tile // tm
    ne = mw['wr_hi'].shape[0]

    def tok(i):
        return (i, 0)

    def cst(i):
        return (0, 0)

    consts = [mw['wup'], mw['wout'], mw['g2'], mw['wr_hi'], mw['wr_lo'], mw['br']]
    return pl.pallas_call(
        functools.partial(_merge_kernel, tm=tm, sub=sub),
        grid=(t // tm,),
        in_specs=[
            pl.BlockSpec((tm, d), tok),
            pl.BlockSpec((tm, d), tok),
            pl.BlockSpec((tm, attn.shape[-1]), tok),
            pl.BlockSpec((tm, d), tok),
            pl.BlockSpec((tm, d), tok),
        ] + [pl.BlockSpec(a.shape, cst) for a in consts],
        out_specs=(
            pl.BlockSpec((tm, d), tok),
            pl.BlockSpec((tm, d), tok),
            pl.BlockSpec((ne, tm), lambda i: (0, i)),
            pl.BlockSpec((ne, tm), lambda i: (0, i)),
            pl.BlockSpec((None, ne, LANES), lambda i: (i // sub, 0, 0)),
        ),
        out_shape=(
            jax.ShapeDtypeStruct((t, d), F32),
            jax.ShapeDtypeStruct((t, d), BF16),
            jax.ShapeDtypeStruct((ne, t), F32),
            jax.ShapeDtypeStruct((ne, t), F32),
            jax.ShapeDtypeStruct((t // moe_tile, ne, LANES), F32),
        ),
        scratch_shapes=[pltpu.VMEM((ne, LANES), F32)],
        compiler_params=_cp(("arbitrary",)),
        name="merge",
    )(x2, ya, attn, sga, sgb, *consts)


def _moe_kernel(cnt_ref, h2_ref, x1_ref, gt_ref, rt_ref, wg_ref, wu_ref, wd_ref, bg_ref, bu_ref, bd_ref, y_ref,
                *, tt):
    j = pl.program_id(0)
    e = pl.program_id(1)
    ne = pl.num_programs(1)
    rb = MOE_ROWS

    @pl.when(e == 0)
    def _():
        y_ref[...] = x1_ref[...]

    n_rows = cnt_ref[j * ne + e]
    n_blk = (n_rows + rb - 1) // rb
    g_row = gt_ref[0]
    r_row = rt_ref[0]
    rid = lax.broadcasted_iota(I32, (rb, tt), 0).astype(F32)

    def block(blk, c):
        hit = jnp.broadcast_to(r_row, (rb, tt)) == (rid + (blk * rb).astype(F32))
        p = jnp.where(hit, 1.0, 0.0).astype(BF16)
        xg = _dot(p, h2_ref[...]).astype(BF16)
        a = jnp.minimum(_dot(xg, wg_ref[0]) + bg_ref[0], SWIGLU_LIMIT)
        b = jnp.clip(_dot(xg, wu_ref[0]) + bu_ref[0], -SWIGLU_LIMIT, SWIGLU_LIMIT)
        hid = a * jax.nn.sigmoid(SWIGLU_ALPHA * a) * (b + 1.0)
        o = _dot(hid.astype(BF16), wd_ref[0]) + bd_ref[0]
        g_col = jnp.sum(jnp.where(hit, jnp.broadcast_to(g_row, (rb, tt)), 0.0), axis=1, keepdims=True)
        og = (o * g_col).astype(BF16)
        y_ref[...] += _dot_tn(p, og)
        return c

    lax.fori_loop(0, n_blk, block, 0)


def _moe(h2, x1, gt, rt, cnt, ew, tt):
    t, d = h2.shape
    ne = gt.shape[0]
    nt = t // tt
    f = ew['wg'].shape[-1]
    grid_spec = pltpu.PrefetchScalarGridSpec(
        num_scalar_prefetch=1,
        grid=(nt, ne),
        in_specs=[
            pl.BlockSpec((tt, d), lambda j, e, c: (j, 0)),
            pl.BlockSpec((tt, d), lambda j, e, c: (j, 0)),
            pl.BlockSpec((1, 1, tt), lambda j, e, c: (e, 0, j)),
            pl.BlockSpec((1, 1, tt), lambda j, e, c: (e, 0, j)),
            pl.BlockSpec((1, d, f), lambda j, e, c: (e, 0, 0)),
            pl.BlockSpec((1, d, f), lambda j, e, c: (e, 0, 0)),
            pl.BlockSpec((1, f, d), lambda j, e, c: (e, 0, 0)),
            pl.BlockSpec((1, 1, f), lambda j, e, c: (e, 0, 0)),
            pl.BlockSpec((1, 1, f), lambda j, e, c: (e, 0, 0)),
            pl.BlockSpec((1, 1, d), lambda j, e, c: (e, 0, 0)),
        ],
        out_specs=pl.BlockSpec((tt, d), lambda j, e, c: (j, 0)),
    )
    return pl.pallas_call(
        functools.partial(_moe_kernel, tt=tt),
        grid_spec=grid_spec,
        out_shape=jax.ShapeDtypeStruct((t, d), F32),
        compiler_params=_cp(("arbitrary", "arbitrary")),
        name="moe",
    )(cnt, h2, x1, gt, rt, ew['wg'], ew['wu'], ew['wd'], ew['bg'], ew['bu'], ew['bd'])


def _pad_heads(wmat, n_heads, width):
    d = wmat.shape[0]
    w3 = wmat.reshape(d, n_heads, width)
    return jnp.pad(w3, ((0, 0), (0, 0), (0, LANES - width))).reshape(d, n_heads * LANES)


def _pad_lanes(v, width=LANES):
    v = v.reshape(1, -1)
    return jnp.pad(v, ((0, 0), (0, width - v.shape[1])))


def _rel_bucket(rel):
    half = REL_BUCKETS // 2
    max_exact = half // 2
    n = jnp.abs(rel)
    large = max_exact + (jnp.log(jnp.maximum(n, 1).astype(jnp.float32) / max_exact)
                         / math.log(REL_MAX_DIST / max_exact) * (half - max_exact)).astype(jnp.int32)
    large = jnp.minimum(large, half - 1)
    return jnp.where(rel > 0, half, 0) + jnp.where(n < max_exact, n, large)


def _bias_tiles(rel_bias):
    tk = KEY_TILE
    half = REL_BUCKETS // 2
    max_exact = half // 2
    n_sat = int(math.ceil(max_exact * (REL_MAX_DIST / max_exact) ** ((half - 1 - max_exact) / (half - max_exact)))) + 2
    nd = (n_sat + 2 * tk - 2) // tk + 1
    dd = jnp.arange(nd, dtype=I32)[:, None, None]
    c = jnp.arange(tk, dtype=I32)[None, :, None]
    r = jnp.arange(tk, dtype=I32)[None, None, :]
    bucket = _rel_bucket(c - r - dd * tk)
    onehot = (bucket[..., None] == jnp.arange(REL_BUCKETS, dtype=I32)).astype(F32)
    tiles = jnp.einsum('dcrb,bh->dhcr', onehot, rel_bias.astype(F32) * LOG2E,
                       precision=lax.Precision.HIGHEST)
    return tiles


def _prep_proj(norm1_g, w_in, q_norm_g, k_norm_g, idx_k_norm_g, idx_k_norm_b, d_model):
    ssm_w = d_model // 2
    attn_w = N_HEADS * HEAD_DIM
    kv = N_KV_HEADS * HEAD_DIM
    sizes = [ssm_w, attn_w, kv, kv, IDX_HEADS * IDX_DIM, IDX_DIM, IDX_HEADS, d_model, d_model]
    pts = np.cumsum(sizes)[:-1].tolist()
    wu, wq, wk, wv, wqi, wki, wwi, wga, wgb = jnp.split(w_in, pts, axis=1)
    bf = lambda a: a.astype(BF16)
    blk = np.kron(np.eye(N_KV_HEADS), np.ones((HEAD_DIM, HEAD_DIM))) / HEAD_DIM
    wwit = jnp.pad(wwi.T, ((0, 2 * SUBLANES - IDX_HEADS), (0, 0)))
    return dict(
        g1=norm1_g.reshape(1, -1).astype(F32),
        wu=bf(wu), wq=bf(_pad_heads(wq, N_HEADS, HEAD_DIM)), wk=bf(_pad_heads(wk, N_KV_HEADS, HEAD_DIM)),
        wv=bf(_pad_heads(wv, N_KV_HEADS, HEAD_DIM)), wqi=bf(_pad_heads(wqi, IDX_HEADS, IDX_DIM)),
        wki=bf(_pad_heads(wki, 1, IDX_DIM)), wwit=bf(wwit),
        wga=bf(wga), wgb=bf(wgb), wkc=bf(wk), wvc=bf(wv),
        gq=_pad_lanes(q_norm_g.astype(F32)), gk=_pad_lanes(k_norm_g.astype(F32)),
        gkc=jnp.tile(k_norm_g.astype(F32), N_KV_HEADS).reshape(1, -1),
        gi=_pad_lanes(idx_k_norm_g.astype(F32)), bi=_pad_lanes(idx_k_norm_b.astype(F32)),
        ones_h=jnp.full((LANES, LANES), 1.0 / HEAD_DIM, BF16),
        ones_c=jnp.asarray(blk, BF16),
    )


def _prep_s5(lre, lim, log_dt, b_re, b_im, c_re, c_im, dvec, wa, wb):
    g, p = lre.shape
    ch = b_re.shape[-1]
    lam = lax.complex(lre.astype(F32), lim.astype(F32))
    dt = jnp.exp(log_dt.astype(F32))[:, None]
    a_bar = jnp.exp(lam * dt)
    b_bar = ((a_bar - 1.0) / lam)[:, :, None] * lax.complex(b_re.astype(F32), b_im.astype(F32))
    eye = jnp.eye(g, dtype=F32)
    b_r = jnp.einsum('gpc,gh->gchp', jnp.real(b_bar), eye).reshape(g * ch, g * p)
    b_i = jnp.einsum('gpc,gh->gchp', jnp.imag(b_bar), eye).reshape(g * ch, g * p)
    c_r = jnp.einsum('gcp,gh->gphc', c_re.astype(F32), eye).reshape(g * p, g * ch)
    c_i = jnp.einsum('gcp,gh->gphc', c_im.astype(F32), eye).reshape(g * p, g * ch)
    return dict(
        bmat=jnp.concatenate([b_r, b_i], axis=1).astype(BF16),
        cmat=jnp.concatenate([c_r, -c_i], axis=0).astype(BF16),
        a_re=jnp.real(a_bar).reshape(1, g * p), a_im=jnp.imag(a_bar).reshape(1, g * p),
        d=dvec.reshape(1, -1).astype(F32), wa=wa.astype(BF16), wb=wb.astype(BF16),
    )


def _prep_merge(w_attn_up, w_out, norm2_g, w_router, b_router):
    d = w_attn_up.shape[1]
    wup = jnp.pad(w_attn_up.reshape(N_HEADS, HEAD_DIM, d), ((0, 0), (0, LANES - HEAD_DIM), (0, 0)))
    wr_t = w_router.astype(F32).T
    wr_hi = wr_t.astype(BF16)
    wr_lo = (wr_t - wr_hi.astype(F32)).astype(BF16)
    return dict(
        wup=wup.reshape(N_HEADS * LANES, d).astype(BF16), wout=w_out.astype(BF16),
        g2=norm2_g.reshape(1, -1).astype(F32), wr_hi=wr_hi, wr_lo=wr_lo,
        br=jnp.broadcast_to(b_router.astype(F32)[:, None], (b_router.shape[0], LANES)),
    )


def _prep_moe(wg, bg, wu, bu, wd, bd):
    return dict(wg=wg.astype(BF16), wu=wu.astype(BF16), wd=wd.astype(BF16),
                bg=bg.astype(F32)[:, None, :], bu=bu.astype(F32)[:, None, :], bd=bd.astype(F32)[:, None, :])


def _pick_tile(n, pref):
    t = min(n, pref)
    while n % t:
        t //= 2
    return t


def _pad_axis(a, axis, size):
    pad = [(0, 0)] * a.ndim
    pad[axis] = (0, size - a.shape[axis])
    return jnp.pad(a, pad)


def _trunk_layer(x, past_k, past_v, past_ik, h0_re, h0_im, pw, sw, mw, ew, bias_tiles):
    bsz, seq, d = x.shape
    t = bsz * seq
    tm = _pick_tile(seq, 512)
    u_tb, q, kp, vp, qi, kip, wt, sga, sgb, kc, vc, kic = _proj(x, pw, bsz, seq, tm)

    half = sw['a_re'].shape[1]
    if h0_re is None:
        h0 = jnp.zeros((bsz, 2 * half), F32)
    else:
        h0 = jnp.concatenate([h0_re.reshape(bsz, half), h0_im.reshape(bsz, half)], axis=1).astype(F32)
    tc = _pick_tile(seq, max(1, 512 // bsz))
    ya_tb, hout = _s5(u_tb, h0, sw, bsz, seq, tc)
    groups = half // SSM_STATE
    s_re = hout[:, :half].reshape(bsz, groups, SSM_STATE)
    s_im = hout[:, half:].reshape(bsz, groups, SSM_STATE)

    past = 0 if past_k is None else past_k.shape[1]
    n_keys = past + seq
    lk = -(-n_keys // KEY_BLOCK) * KEY_BLOCK
    kip3 = kip.reshape(bsz, seq, LANES)
    if past:
        lane = jnp.arange(LANES)
        pk = jnp.pad(past_k.astype(F32), ((0, 0), (0, 0), (0, 0), (0, LANES - HEAD_DIM)))
        pk = jnp.where(lane == HEAD_DIM, 1.0, pk).astype(BF16)
        pv = jnp.pad(past_v.astype(F32), ((0, 0), (0, 0), (0, 0), (0, LANES - HEAD_DIM)))
        pv = jnp.where(lane == HEAD_DIM, 1.0, pv).astype(BF16)
        pik = jnp.pad(past_ik.astype(F32), ((0, 0), (0, 0), (0, LANES - IDX_DIM))).astype(BF16)
        k_all = jnp.concatenate([pk.transpose(0, 2, 1, 3), kp], axis=2)
        v_all = jnp.concatenate([pv.transpose(0, 2, 1, 3), vp], axis=2)
        ki_all = jnp.concatenate([pik, kip3], axis=1)
    else:
        k_all, v_all, ki_all = kp, vp, kip3
    k_all = _pad_axis(k_all, 2, lk)
    v_all = _pad_axis(v_all, 2, lk)
    ki_all = _pad_axis(ki_all, 1, lk)
    vt_all = v_all.reshape(bsz, N_KV_HEADS, lk // KEY_BLOCK, KEY_BLOCK, LANES).transpose(0, 1, 2, 4, 3)
    tq = KEY_TILE
    seq_q = -(-seq // tq) * tq
    q_p, qi_p, wt_p = _pad_axis(q, 2, seq_q), _pad_axis(qi, 2, seq_q), _pad_axis(wt, 2, seq_q)
    kf = k_all[..., :HEAD_DIM].astype(F32)
    kmax = jnp.sqrt(jnp.max(jnp.sum(kf * kf, axis=-1), axis=-1)).reshape(-1)
    bmax = jnp.max(jnp.abs(bias_tiles), axis=(0, 2, 3))
    stats = jnp.concatenate([kmax, bmax]).astype(F32)
    attn = _dsa(stats, q_p, qi_p, wt_p, k_all, vt_all, ki_all, bias_tiles, bsz, seq_q, past, n_keys, tq)
    attn = attn[:, :seq]

    moe_tile = _pick_tile(t, 1024)
    tm2 = _pick_tile(moe_tile, 512)
    ya = ya_tb.reshape(seq, bsz, d).transpose(1, 0, 2).reshape(t, d)
    x1, h2, gt, rt, cnt = _merge(x.reshape(t, d), ya, attn.reshape(t, attn.shape[-1]), sga, sgb, mw, tm2, moe_tile)
    cnt_i = cnt[:, :, 0].astype(I32).reshape(-1)
    ne = gt.shape[0]
    y = _moe(h2, x1, gt.reshape(ne, 1, t), rt.reshape(ne, 1, t), cnt_i, ew, moe_tile)

    k_new = kc.reshape(bsz, seq, N_KV_HEADS, HEAD_DIM)
    v_new = vc.reshape(bsz, seq, N_KV_HEADS, HEAD_DIM)
    ik_new = kic.reshape(bsz, seq, IDX_DIM)
    return y.reshape(bsz, seq, d), k_new, v_new, ik_new, s_re, s_im


def kernel(x_prompt, x_sample, cache_k, cache_v, cache_idx_k, state_ssm_re, state_ssm_im, rel_bias, norm1_g, w_in, ssm_lambda_re, ssm_lambda_im, ssm_log_dt, ssm_b_re, ssm_b_im, ssm_c_re, ssm_c_im, ssm_d, ssm_w_glu_a, ssm_w_glu_b, q_norm_g, k_norm_g, idx_k_norm_g, idx_k_norm_b, w_attn_up, w_out, norm2_g, moe_w_router, moe_b_router, moe_w_gate, moe_b_gate, moe_w_up, moe_b_up, moe_w_down, moe_b_down):
    depth = w_in.shape[0]
    d_model = x_prompt.shape[-1]
    bias_tiles = _bias_tiles(rel_bias)
    xp, xs = x_prompt, x_sample
    st_p, st_s = [], []
    for l in range(depth):
        pw = _prep_proj(norm1_g[l], w_in[l], q_norm_g[l], k_norm_g[l], idx_k_norm_g[l], idx_k_norm_b[l], d_model)
        sw = _prep_s5(ssm_lambda_re[l], ssm_lambda_im[l], ssm_log_dt[l], ssm_b_re[l], ssm_b_im[l], ssm_c_re[l],
                      ssm_c_im[l], ssm_d[l], ssm_w_glu_a[l], ssm_w_glu_b[l])
        mw = _prep_merge(w_attn_up[l], w_out[l], norm2_g[l], moe_w_router[l], moe_b_router[l])
        ew = _prep_moe(moe_w_gate[l], moe_b_gate[l], moe_w_up[l], moe_b_up[l], moe_w_down[l], moe_b_down[l])
        xp, *sp = _trunk_layer(xp, None, None, None, None, None, pw, sw, mw, ew, bias_tiles)
        xs, *ss = _trunk_layer(xs, cache_k[l], cache_v[l], cache_idx_k[l], state_ssm_re[l], state_ssm_im[l],
                               pw, sw, mw, ew, bias_tiles)
        st_p.append(sp)
        st_s.append(ss)
    outs_p = [jnp.stack([s[i] for s in st_p]) for i in range(5)]
    outs_s = [jnp.stack([s[i] for s in st_s]) for i in range(5)]
    return (xp, xs, *outs_p, *outs_s)
```

```python
import functools
import math

import numpy as np
import jax
import jax.numpy as jnp
from jax import lax
from jax.experimental import pallas as pl
from jax.experimental.pallas import tpu as pltpu

F32 = jnp.float32
BF16 = jnp.bfloat16
I32 = jnp.int32

LANES = 128
SUBLANES = 8
VMEM_LIMIT = 56 * 1024 * 1024

CHUNK = 64
SSM_GROUP_CH = 16
SSM_STATE = 64
N_HEADS = 8
HEAD_DIM = 64
N_KV_HEADS = 2
KV_REP = N_HEADS // N_KV_HEADS
IDX_HEADS = 8
IDX_DIM = 64
TOPK_MAX = 256
REL_BUCKETS = 32
REL_MAX_DIST = 1024
N_EXPERTS = 32
TOP_K = 4
SWIGLU_LIMIT = 7.0
SWIGLU_ALPHA = 1.702
EPS = 1e-6

KEY_TILE = 128
KEY_BLOCK = 256
LOG2E = math.log2(math.e)
NEG_BIG = -1e30
SHIFT_LIMIT = 30.0
SEARCH_GROUP = 5
MOE_TILE = 1024
MOE_ROWS = 160


def _cp(sem):
    return pltpu.CompilerParams(dimension_semantics=sem, vmem_limit_bytes=VMEM_LIMIT)


def _dot(a, b):
    return jnp.dot(a, b, preferred_element_type=F32)


def _dot_nt(a, b):
    return lax.dot_general(a, b, (((1,), (1,)), ((), ())), preferred_element_type=F32)


def _dot_tn(a, b):
    return lax.dot_general(a, b, (((0,), (0,)), ((), ())), preferred_element_type=F32)


def _split(a):
    hi = a.astype(BF16)
    lo = (a - hi.astype(F32)).astype(BF16)
    return hi, lo


def _dot_split(a, g):
    hi, lo = _split(a)
    return _dot(hi, g) + _dot(lo, g)


def _proj_kernel(x_ref, g1_ref, wu_ref, wq_ref, wk_ref, wv_ref, wqi_ref, wki_ref, wwit_ref, wga_ref, wgb_ref,
                 wkc_ref, wvc_ref, gq_ref, gk_ref, gkc_ref, gi_ref, bi_ref, ones_h_ref, ones_c_ref,
                 u_ref, q_ref, kp_ref, vp_ref, qi_ref, kip_ref, wt_ref, sga_ref, sgb_ref,
                 kc_ref, vc_ref, kic_ref):
    x = x_ref[...]
    ms = jnp.mean(x * x, axis=-1, keepdims=True)
    hn = (x * lax.rsqrt(ms + EPS) * g1_ref[...]).astype(BF16)
    ones_h = ones_h_ref[...]
    lane = lax.broadcasted_iota(I32, (x.shape[0], LANES), 1)

    u_ref[...] = _dot(hn, wu_ref[...]).astype(BF16)

    q = _dot(hn, wq_ref[...])
    scale = HEAD_DIM ** -0.5 * LOG2E
    for h in range(N_HEADS):
        qh = q[:, h * LANES:(h + 1) * LANES]
        msq = _dot_split(qh * qh, ones_h)
        q_ref[h] = (qh * lax.rsqrt(msq + EPS) * (gq_ref[...] * scale)).astype(BF16)

    k = _dot(hn, wk_ref[...])
    for g in range(N_KV_HEADS):
        kg = k[:, g * LANES:(g + 1) * LANES]
        msk = _dot_split(kg * kg, ones_h)
        kn = kg * lax.rsqrt(msk + EPS) * gk_ref[...]
        kp_ref[g] = jnp.where(lane == HEAD_DIM, 1.0, kn).astype(BF16)

    v = _dot(hn, wv_ref[...])
    for g in range(N_KV_HEADS):
        vg = v[:, g * LANES:(g + 1) * LANES]
        vp_ref[g] = jnp.where(lane == HEAD_DIM, 1.0, vg).astype(BF16)

    qi = _dot(hn, wqi_ref[...])
    for h in range(IDX_HEADS):
        qi_ref[h] = qi[:, h * LANES:(h + 1) * LANES].astype(BF16)

    ki = _dot(hn, wki_ref[...])
    mu = _dot_split(ki, ones_h)
    xc = jnp.where(lane < IDX_DIM, ki - mu, 0.0)
    var = _dot_split(xc * xc, ones_h)
    kin = xc * lax.rsqrt(var + EPS) * gi_ref[...] + bi_ref[...]
    kip_ref[...] = kin.astype(BF16)
    kic_ref[...] = kin[:, :IDX_DIM]

    wt = _dot_nt(wwit_ref[...], hn)
    wt_ref[...] = wt[0:IDX_HEADS, :] * (IDX_HEADS ** -0.5 * IDX_DIM ** -0.5)

    sga_ref[...] = jax.nn.sigmoid(_dot(hn, wga_ref[...])).astype(BF16)
    sgb_ref[...] = jax.nn.sigmoid(_dot(hn, wgb_ref[...])).astype(BF16)

    kc = _dot(hn, wkc_ref[...])
    mskc = _dot_split(kc * kc, ones_c_ref[...])
    kc_ref[...] = kc * lax.rsqrt(mskc + EPS) * gkc_ref[...]
    vc_ref[...] = _dot(hn, wvc_ref[...])


def _proj(x, pw, bsz, seq, tm):
    d = x.shape[-1]
    nt = seq // tm
    t = bsz * seq
    x2 = x.reshape(t, d)

    def tok(b, i):
        return (b * nt + i, 0)

    def cst(b, i):
        return (0, 0)

    def wspec(a):
        return pl.BlockSpec(a.shape, cst)

    weights = [pw['g1'], pw['wu'], pw['wq'], pw['wk'], pw['wv'], pw['wqi'], pw['wki'], pw['wwit'], pw['wga'],
               pw['wgb'], pw['wkc'], pw['wvc'], pw['gq'], pw['gk'], pw['gkc'], pw['gi'], pw['bi'],
               pw['ones_h'], pw['ones_c']]
    ssm_w = pw['wu'].shape[1]
    out_shape = (
        jax.ShapeDtypeStruct((seq, bsz * ssm_w), BF16),
        jax.ShapeDtypeStruct((bsz, N_HEADS, seq, LANES), BF16),
        jax.ShapeDtypeStruct((bsz, N_KV_HEADS, seq, LANES), BF16),
        jax.ShapeDtypeStruct((bsz, N_KV_HEADS, seq, LANES), BF16),
        jax.ShapeDtypeStruct((bsz, IDX_HEADS, seq, LANES), BF16),
        jax.ShapeDtypeStruct((t, LANES), BF16),
        jax.ShapeDtypeStruct((bsz, IDX_HEADS, seq), F32),
        jax.ShapeDtypeStruct((t, d), BF16),
        jax.ShapeDtypeStruct((t, d), BF16),
        jax.ShapeDtypeStruct((t, N_KV_HEADS * HEAD_DIM), F32),
        jax.ShapeDtypeStruct((t, N_KV_HEADS * HEAD_DIM), F32),
        jax.ShapeDtypeStruct((t, IDX_DIM), F32),
    )

    def hm(nh):
        return pl.BlockSpec((None, nh, tm, LANES), lambda b, i: (b, 0, i, 0))

    out_specs = (
        pl.BlockSpec((tm, ssm_w), lambda b, i: (i, b)),
        hm(N_HEADS), hm(N_KV_HEADS), hm(N_KV_HEADS), hm(IDX_HEADS),
        pl.BlockSpec((tm, LANES), tok),
        pl.BlockSpec((None, IDX_HEADS, tm), lambda b, i: (b, 0, i)),
        pl.BlockSpec((tm, d), tok), pl.BlockSpec((tm, d), tok),
        pl.BlockSpec((tm, N_KV_HEADS * HEAD_DIM), tok), pl.BlockSpec((tm, N_KV_HEADS * HEAD_DIM), tok),
        pl.BlockSpec((tm, IDX_DIM), tok),
    )
    return pl.pallas_call(
        _proj_kernel,
        grid=(bsz, nt),
        in_specs=[pl.BlockSpec((tm, d), tok)] + [wspec(a) for a in weights],
        out_specs=out_specs,
        out_shape=out_shape,
        compiler_params=_cp(("arbitrary", "arbitrary")),
        name="proj",
    )(x2, *weights)


def _gelu_tanh(x):
    return 0.5 * x * (1.0 + jnp.tanh(math.sqrt(2.0 / math.pi) * (x + 0.044715 * (x * x * x))))


def _s5_kernel(u_ref, h0_ref, bmat_ref, are_ref, aim_ref, cmat_ref, dvec_ref, wa_ref, wb_ref,
               ya_ref, hout_ref, state_ref, bu_ref, *, bsz, tc, strip):
    s = pl.program_id(0)
    half = are_ref.shape[1]

    @pl.when(s == 0)
    def _():
        state_ref[...] = h0_ref[...]

    u = u_ref[...]
    bu_ref[...] = _dot(u, bmat_ref[...])

    for c0 in range(0, half, strip):
        ar = jnp.broadcast_to(are_ref[:, c0:c0 + strip], (bsz, strip))
        ai = jnp.broadcast_to(aim_ref[:, c0:c0 + strip], (bsz, strip))
        hr0 = state_ref[:, c0:c0 + strip]
        hi0 = state_ref[:, half + c0:half + c0 + strip]

        def step(t, carry):
            hr, hi = carry
            r0 = pl.multiple_of(t * bsz, bsz)
            br = bu_ref[pl.ds(r0, bsz), c0:c0 + strip]
            bi = bu_ref[pl.ds(r0, bsz), half + c0:half + c0 + strip]
            nr = ar * hr - ai * hi + br
            ni = ar * hi + ai * hr + bi
            bu_ref[pl.ds(r0, bsz), c0:c0 + strip] = nr
            bu_ref[pl.ds(r0, bsz), half + c0:half + c0 + strip] = ni
            return nr, ni

        hr, hi = lax.fori_loop(0, tc, step, (hr0, hi0))
        state_ref[:, c0:c0 + strip] = hr
        state_ref[:, half + c0:half + c0 + strip] = hi

    y = _dot(bu_ref[...].astype(BF16), cmat_ref[...]) + dvec_ref[...] * u.astype(F32)
    g = _gelu_tanh(y).astype(BF16)
    ya = _dot(g, wa_ref[...]) * jax.nn.sigmoid(_dot(g, wb_ref[...]))
    ya_ref[...] = ya.astype(BF16)

    @pl.when(s == pl.num_programs(0) - 1)
    def _():
        hout_ref[...] = state_ref[...]


def _s5(u_tb, h0, sw, bsz, seq, tc):
    rows = tc * bsz
    ssm_w = sw['bmat'].shape[0]
    two_half = sw['bmat'].shape[1]
    half = two_half // 2
    d = sw['wa'].shape[1]
    u2 = u_tb.reshape(seq * bsz, ssm_w)
    strip = min(512, half)

    def cst(s):
        return (0, 0)

    consts = [h0, sw['bmat'], sw['a_re'], sw['a_im'], sw['cmat'], sw['d'], sw['wa'], sw['wb']]
    ya, hout = pl.pallas_call(
        functools.partial(_s5_kernel, bsz=bsz, tc=tc, strip=strip),
        grid=(seq // tc,),
        in_specs=[pl.BlockSpec((rows, ssm_w), lambda s: (s, 0))] + [pl.BlockSpec(a.shape, cst) for a in consts],
        out_specs=(pl.BlockSpec((rows, d), lambda s: (s, 0)), pl.BlockSpec((bsz, two_half), cst)),
        out_shape=(jax.ShapeDtypeStruct((seq * bsz, d), BF16), jax.ShapeDtypeStruct((bsz, two_half), F32)),
        scratch_shapes=[pltpu.VMEM((bsz, two_half), F32), pltpu.VMEM((rows, two_half), F32)],
        compiler_params=_cp(("arbitrary",)),
        name="s5",
    )(u2, *consts)
    return ya.reshape(seq, bsz * d), hout


def _f2key(x):
    b = lax.bitcast_convert_type(x, I32)
    return b ^ ((b >> 31) & 0x7FFFFFFF)


def _key2f(k):
    return lax.bitcast_convert_type(k ^ ((k >> 31) & 0x7FFFFFFF), F32)


def _dsa_kernel(st_ref, q_ref, qi_ref, wt_ref, k_ref, vt_ref, ki_ref, bias_ref, o_ref,
                s_ref, lo_ref, hi_ref, clo_ref, glo_ref, ghi_ref, side_ref, q2_ref, mrow_ref, acc_ref,
                *, bsz, tq, past, n_keys, topk, nkt, nd, idx_bits):
    b_id = pl.program_id(0)
    i = pl.program_id(1)
    kb = KEY_BLOCK
    sl = SUBLANES
    q0 = past + i * tq
    last_chunk = (q0 + tq - 1) // CHUNK
    n_kt = jnp.minimum(nkt, ((last_chunk + 1) * CHUNK + kb - 1) // kb)
    d0 = q0 // KEY_TILE

    krow = lax.broadcasted_iota(I32, (kb, tq), 0)
    q_chunk = (q0 + lax.broadcasted_iota(I32, (kb, tq), 1)) // CHUNK
    qc8 = (q0 + lax.broadcasted_iota(I32, (sl, tq), 1)) // CHUNK
    n_adm = jnp.minimum((qc8 + 1) * CHUNK, n_keys)
    n_admf = n_adm.astype(F32)
    needf = jnp.minimum(topk, n_adm).astype(F32)

    def bcast(x):
        return jnp.broadcast_to(x[0:1, :], (kb, tq))

    def rep(x):
        return jnp.broadcast_to(x, (sl, tq))

    qi = qi_ref[...].reshape(IDX_HEADS * tq, LANES)

    def score_block(kt, masked):
        k0 = pl.multiple_of(kt * kb, kb)
        s = _dot_nt(ki_ref[pl.ds(k0, kb), :], qi)
        sc = jnp.zeros((kb, tq), F32)
        for h in range(IDX_HEADS):
            sc = sc + wt_ref[h:h + 1, :] * jnp.maximum(s[:, h * tq:(h + 1) * tq], 0.0)
        if masked:
            kpos = k0 + krow
            adm = ((kpos // CHUNK) <= q_chunk) & (kpos < n_keys)
            sc = jnp.where(adm, sc, -jnp.inf)
        s_ref[kt] = sc

    def score_pair(j, c):
        score_block(2 * j, False)
        score_block(2 * j + 1, False)
        return c

    lax.fori_loop(0, (n_kt - 1) // 2, score_pair, 0)

    @pl.when((n_kt - 1) % 2 == 1)
    def _():
        score_block(n_kt - 2, False)

    score_block(n_kt - 1, True)

    part = 4 * sl

    def fold(x, op):
        x = x.reshape(kb // part, part, tq)
        acc = x[0]
        for j in range(1, kb // part):
            acc = op(acc, x[j])
        return acc

    def count(pred):
        def body(kt, c):
            return c + fold(jnp.where(pred(s_ref[kt], kt), 1.0, 0.0), jnp.add)
        c = lax.fori_loop(0, n_kt, body, jnp.zeros((part, tq), F32))
        return rep(jnp.sum(c, axis=0, keepdims=True))

    def minmax(kt, c):
        mx, mn = c
        s = s_ref[kt]
        return (jnp.maximum(mx, fold(s, jnp.maximum)),
                jnp.minimum(mn, fold(jnp.where(s == -jnp.inf, jnp.inf, s), jnp.minimum)))

    mx, mn = lax.fori_loop(0, n_kt, minmax,
                           (jnp.full((part, tq), -jnp.inf, F32), jnp.full((part, tq), jnp.inf, F32)))
    lo_ref[...] = rep(jnp.min(mn, axis=0, keepdims=True))
    hi_ref[...] = _key2f(_f2key(rep(jnp.max(mx, axis=0, keepdims=True))) + 1)
    clo_ref[...] = n_admf
    glo_ref[...] = n_admf - needf + 0.5
    ghi_ref[...] = 0.0 - needf + 0.5
    side_ref[...] = jnp.zeros((sl, tq), F32)

    def searching(lo, hi, clo):
        return (_f2key(hi) > _f2key(lo) + 1) & (clo > needf)

    def refine(it, c):
        lo, hi, clo = lo_ref[...], hi_ref[...], clo_ref[...]
        glo, ghi, side = glo_ref[...], ghi_ref[...], side_ref[...]
        k_t = _f2key(lo + (hi - lo) * (glo / (glo - ghi)))
        t = _key2f(jnp.minimum(jnp.maximum(k_t, _f2key(lo) + 1), _f2key(hi) - 1))
        tb = bcast(t)
        cnt = count(lambda s, kt: s >= tb)
        g = cnt - needf + 0.5
        open_ = searching(lo, hi, clo)
        up = open_ & (g > 0.0)
        dn = open_ & (g < 0.0)
        lo_ref[...] = jnp.where(up, t, lo)
        clo_ref[...] = jnp.where(up, cnt, clo)
        hi_ref[...] = jnp.where(dn, t, hi)
        glo_ref[...] = jnp.where(up, g, jnp.where(dn & (side < 0.0), glo * 0.5, glo))
        ghi_ref[...] = jnp.where(dn, g, jnp.where(up & (side > 0.0), ghi * 0.5, ghi))
        side_ref[...] = jnp.where(up, 1.0, jnp.where(dn, -1.0, side))
        return c

    def snap():
        lo, hi, clo = lo_ref[...], hi_ref[...], clo_ref[...]
        lo_b, hi_b = bcast(lo), bcast(hi)

        def body(kt, c):
            a, b = c
            s = s_ref[kt]
            return (jnp.minimum(a, fold(jnp.where(s >= lo_b, s, jnp.inf), jnp.minimum)),
                    jnp.maximum(b, fold(jnp.where(s < hi_b, s, -jnp.inf), jnp.maximum)))

        a, b = lax.fori_loop(0, n_kt, body,
                             (jnp.full((part, tq), jnp.inf, F32), jnp.full((part, tq), -jnp.inf, F32)))
        open_ = searching(lo, hi, clo)
        lo_ref[...] = jnp.where(open_, rep(jnp.min(a, axis=0, keepdims=True)), lo)
        hi_ref[...] = jnp.where(open_, _key2f(_f2key(rep(jnp.max(b, axis=0, keepdims=True))) + 1), hi)

    def n_searching():
        return jnp.max(jnp.where(searching(lo_ref[...], hi_ref[...], clo_ref[...]), 1.0, 0.0))

    def group(c):
        grp, _ = c
        lax.fori_loop(0, SEARCH_GROUP, refine, 0)

        @pl.when(grp >= 1)
        def _():
            snap()
        return grp + 1, n_searching()

    lax.while_loop(lambda c: c[1] > 0.0, group, (jnp.int32(0), n_searching()))
    thr = lo_ref[...]
    thr_b = bcast(thr)

    n_tied = jnp.max(jnp.where(clo_ref[...] > needf, 1.0, 0.0))

    @pl.when(n_tied > 0.0)
    def _():
        rem = needf - count(lambda s, kt: s > thr_b)
        lo_ref[...] = jnp.zeros((sl, tq), F32)
        hi_ref[...] = jnp.full((sl, tq), float(nkt * kb), F32)
        krowf = krow.astype(F32)

        def bisect_idx(it, c):
            lo = lo_ref[...]
            hi = hi_ref[...]
            mid = jnp.floor((lo + hi) * 0.5)
            mid_b = bcast(mid)
            ok = count(lambda s, kt: (s == thr_b) & ((kt * kb).astype(F32) + krowf < mid_b)) >= rem
            hi_ref[...] = jnp.where(ok, mid, hi)
            lo_ref[...] = jnp.where(ok, lo, mid)
            return c

        lax.fori_loop(0, idx_bits, bisect_idx, 0)
        cut_b = bcast(hi_ref[...])

        def drop(kt, c):
            s = s_ref[kt]
            s_ref[kt] = jnp.where((s == thr_b) & ((kt * kb).astype(F32) + krowf >= cut_b), -jnp.inf, s)
            return c

        lax.fori_loop(0, n_kt, drop, 0)

    rows_g = KV_REP * tq
    qf = q_ref[...].reshape(N_HEADS * tq, LANES).astype(F32)
    qn = jnp.sqrt(jnp.sum(qf * qf, axis=1, keepdims=True))
    lane = lax.broadcasted_iota(I32, (tq, LANES), 1)
    worst = jnp.float32(0.0)
    for h in range(N_HEADS):
        kmax = st_ref[b_id * N_KV_HEADS + h // KV_REP]
        bmax = st_ref[bsz * N_KV_HEADS + h]
        bfar = st_ref[bsz * N_KV_HEADS + N_HEADS + h]
        bound = qn[h * tq:(h + 1) * tq, :] * (kmax * 1.01) + (bmax + 0.1)
        worst = jnp.maximum(worst, jnp.max(bound))
        q2_ref[h * tq:(h + 1) * tq, :] = jnp.where(lane == HEAD_DIM, bfar - bound,
                                                   qf[h * tq:(h + 1) * tq, :]).astype(BF16)
    n_far = jnp.clip((d0 - nd) // 2 + 1, 0, n_kt)

    def logits(kt, g, near, exact):
        k0 = pl.multiple_of(kt * kb, kb)
        maskadd = jnp.where(s_ref[kt] >= thr_b, 0.0, NEG_BIG)
        s = _dot_nt(k_ref[g, pl.ds(k0, kb), :], q2_ref[g * rows_g:(g + 1) * rows_g, :])
        if near:
            da = jnp.clip(d0 - 2 * kt, 0, nd - 1)
            db = jnp.clip(d0 - 2 * kt - 1, 0, nd - 1)
        parts = []
        for r in range(KV_REP):
            h = g * KV_REP + r
            add = maskadd - mrow_ref[0:1, h * tq:(h + 1) * tq] if exact else maskadd
            if near:
                add = jnp.concatenate([bias_ref[da, h], bias_ref[db, h]], axis=0) + add
            parts.append(s[:, r * tq:(r + 1) * tq] + add)
        return jnp.concatenate(parts, axis=1)

    def over_blocks(fn):
        def far_pair(j, c):
            fn((2 * j, 2 * j + 1), False)
            return c
        lax.fori_loop(0, n_far // 2, far_pair, 0)

        @pl.when(n_far % 2 == 1)
        def _():
            fn((n_far - 1,), False)

        n_near = n_kt - n_far

        def near_pair(j, c):
            fn((n_far + 2 * j, n_far + 2 * j + 1), True)
            return c
        lax.fori_loop(0, n_near // 2, near_pair, 0)

        @pl.when(n_near % 2 == 1)
        def _():
            fn((n_kt - 1,), True)

    def attend(exact):
        acc_ref[...] = jnp.zeros(acc_ref.shape, F32)

        def blocks(kts, near):
            for g in range(N_KV_HEADS):
                pv = None
                for kt in kts:
                    p = jnp.exp2(logits(kt, g, near, exact)).astype(BF16)
                    d = _dot(vt_ref[g, kt], p)
                    pv = d if pv is None else pv + d
                acc_ref[g] += pv
        over_blocks(blocks)

    @pl.when(worst <= SHIFT_LIMIT)
    def _():
        attend(False)

    @pl.when(worst > SHIFT_LIMIT)
    def _():
        mrow_ref[...] = jnp.full(mrow_ref.shape, NEG_BIG, F32)

        def blocks(kts, near):
            for g in range(N_KV_HEADS):
                for kt in kts:
                    mx = jnp.max(logits(kt, g, near, False), axis=0, keepdims=True)
                    cur = mrow_ref[:, g * rows_g:(g + 1) * rows_g]
                    mrow_ref[:, g * rows_g:(g + 1) * rows_g] = jnp.maximum(cur, jnp.broadcast_to(mx, (sl, rows_g)))
        over_blocks(blocks)
        attend(True)

    for g in range(N_KV_HEADS):
        acc = acc_ref[g]
        og = acc / acc[HEAD_DIM:HEAD_DIM + 1, :]
        for r in range(KV_REP):
            h = g * KV_REP + r
            o_ref[:, h * LANES:(h + 1) * LANES] = og[:, r * tq:(r + 1) * tq].T.astype(BF16)


def _dsa(stats, q, qi, wt, k_all, vt_all, ki_all, bias_tiles, bsz, seq, past, n_keys, tq):
    lk = k_all.shape[2]
    nkt = lk // KEY_BLOCK
    topk = min(TOPK_MAX, n_keys // 4)
    nd = bias_tiles.shape[0]
    nq = seq // tq
    assert past % KEY_TILE == 0 and tq == KEY_TILE
    idx_bits = int(math.ceil(math.log2(lk))) + 1
    kern = functools.partial(_dsa_kernel, bsz=bsz, tq=tq, past=past, n_keys=n_keys, topk=topk, nkt=nkt, nd=nd,
                             idx_bits=idx_bits)
    row_state = pltpu.VMEM((SUBLANES, tq), F32)
    grid_spec = pltpu.PrefetchScalarGridSpec(
        num_scalar_prefetch=1,
        grid=(bsz, nq),
        in_specs=[
            pl.BlockSpec((None, N_HEADS, tq, LANES), lambda b, i, s: (b, 0, i, 0)),
            pl.BlockSpec((None, IDX_HEADS, tq, LANES), lambda b, i, s: (b, 0, i, 0)),
            pl.BlockSpec((None, IDX_HEADS, tq), lambda b, i, s: (b, 0, i)),
            pl.BlockSpec((None, N_KV_HEADS, lk, LANES), lambda b, i, s: (b, 0, 0, 0)),
            pl.BlockSpec((None, N_KV_HEADS, nkt, LANES, KEY_BLOCK), lambda b, i, s: (b, 0, 0, 0, 0)),
            pl.BlockSpec((None, lk, LANES), lambda b, i, s: (b, 0, 0)),
            pl.BlockSpec(bias_tiles.shape, lambda b, i, s: (0, 0, 0, 0)),
        ],
        out_specs=pl.BlockSpec((None, tq, N_HEADS * LANES), lambda b, i, s: (b, i, 0)),
        scratch_shapes=[
            pltpu.VMEM((nkt, KEY_BLOCK, tq), F32),
            row_state, row_state, row_state, row_state, row_state, row_state,
            pltpu.VMEM((N_HEADS * tq, LANES), BF16),
            pltpu.VMEM((SUBLANES, N_HEADS * tq), F32),
            pltpu.VMEM((N_KV_HEADS, LANES, KV_REP * tq), F32),
        ],
    )
    return pl.pallas_call(
        kern,
        grid_spec=grid_spec,
        out_shape=jax.ShapeDtypeStruct((bsz, seq, N_HEADS * LANES), BF16),
        compiler_params=_cp(("arbitrary", "arbitrary")),
        name="dsa",
    )(stats, q, qi, wt, k_all, vt_all, ki_all, bias_tiles)


def _merge_kernel(x_ref, ya_ref, at_ref, sga_ref, sgb_ref, wup_ref, wout_ref, g2_ref, wr_hi_ref, wr_lo_ref, br_ref,
                  x1_ref, h2_ref, gt_ref, rt_ref, cnt_ref, run_ref, *, tm, sub):
    step = pl.program_id(0)

    @pl.when(step % sub == 0)
    def _():
        run_ref[...] = jnp.zeros(run_ref.shape, F32)

    yb = _dot(at_ref[...], wup_ref[...])
    merged = sga_ref[...].astype(F32) * ya_ref[...].astype(F32) + sgb_ref[...].astype(F32) * yb
    x1 = x_ref[...] + _dot(merged.astype(BF16), wout_ref[...])
    x1_ref[...] = x1
    ms = jnp.mean(x1 * x1, axis=-1, keepdims=True)
    h2 = x1 * lax.rsqrt(ms + EPS) * g2_ref[...]
    h2_hi, h2_lo = _split(h2)
    h2_ref[...] = h2_hi

    wr_hi = wr_hi_ref[...]
    logit = (_dot_nt(wr_hi, h2_hi) + _dot_nt(wr_hi, h2_lo) + _dot_nt(wr_lo_ref[...], h2_hi)) + br_ref[:, 0:1]
    ne = logit.shape[0]
    eid = lax.broadcasted_iota(I32, (ne, tm), 0).astype(F32)
    selb = jnp.zeros((ne, tm), F32)
    tops = []
    picks = []
    for _ in range(TOP_K):
        mx = jnp.max(logit, axis=0, keepdims=True)
        pick = jnp.min(jnp.where(logit == mx, eid, float(ne)), axis=0, keepdims=True)
        hit = eid == pick
        selb = jnp.where(hit, 1.0, selb)
        logit = jnp.where(hit, -jnp.inf, logit)
        tops.append(mx)
        picks.append(hit)
    ex = [jnp.exp(t - tops[0]) for t in tops]
    den = ex[0] + ex[1] + ex[2] + ex[3]
    gate = jnp.zeros((ne, tm), F32)
    for hit, e in zip(picks, ex):
        gate = jnp.where(hit, e / den, gate)
    gt_ref[...] = gate

    sel = selb > 0.5
    selb = selb.astype(BF16)
    r_i = lax.broadcasted_iota(I32, (tm, tm), 0)
    c_i = lax.broadcasted_iota(I32, (tm, tm), 1)
    tri = jnp.where(r_i < c_i, 1.0, 0.0).astype(BF16)
    run = run_ref[...]
    rank = _dot(selb, tri) + jnp.broadcast_to(run[:, 0:1], (ne, tm))
    rt_ref[...] = jnp.where(sel, rank, -1.0)
    run = run + _dot(selb, jnp.ones((tm, LANES), BF16))
    run_ref[...] = run
    cnt_ref[...] = run


def _merge(x2, ya_tb, attn, sga, sgb, mw, bsz, seq, tm, moe_tile):
    t, d = x2.shape
    if seq % tm == 0:
        nt = seq // tm
        ya, ya_spec = ya_tb, pl.BlockSpec((tm, d), lambda i: (i % nt, i // nt))
    else:
        ya = ya_tb.reshape(seq, bsz, d).transpose(1, 0, 2).reshape(t, d)
        ya_spec = pl.BlockSpec((tm, d), lambda i: (i, 0))
    sub = moe_tile // tm
    ne = mw['wr_hi'].shape[0]

    def tok(i):
        return (i, 0)

    def cst(i):
        return (0, 0)

    consts = [mw['wup'], mw['wout'], mw['g2'], mw['wr_hi'], mw['wr_lo'], mw['br']]
    return pl.pallas_call(
        functools.partial(_merge_kernel, tm=tm, sub=sub),
        grid=(t // tm,),
        in_specs=[
            pl.BlockSpec((tm, d), tok),
            ya_spec,
            pl.BlockSpec((tm, attn.shape[-1]), tok),
            pl.BlockSpec((tm, d), tok),
            pl.BlockSpec((tm, d), tok),
        ] + [pl.BlockSpec(a.shape, cst) for a in consts],
        out_specs=(
            pl.BlockSpec((tm, d), tok),
            pl.BlockSpec((tm, d), tok),
            pl.BlockSpec((ne, tm), lambda i: (0, i)),
            pl.BlockSpec((ne, tm), lambda i: (0, i)),
            pl.BlockSpec((None, ne, LANES), lambda i: (i // sub, 0, 0)),
        ),
        out_shape=(
            jax.ShapeDtypeStruct((t, d), F32),
            jax.ShapeDtypeStruct((t, d), BF16),
            jax.ShapeDtypeStruct((ne, t), F32),
            jax.ShapeDtypeStruct((ne, t), F32),
            jax.ShapeDtypeStruct((t // moe_tile, ne, LANES), F32),
        ),
        scratch_shapes=[pltpu.VMEM((ne, LANES), F32)],
        compiler_params=_cp(("arbitrary",)),
        name="merge",
    )(x2, ya, attn, sga, sgb, *consts)


def _moe_kernel(cnt_ref, h2_ref, x1_ref, gt_ref, rt_ref, wg_ref, wu_ref, wd_ref, bg_ref, bu_ref, bd_ref, y_ref,
                *, tt):
    j = pl.program_id(0)
    e = pl.program_id(1)
    ne = pl.num_programs(1)
    rb = MOE_ROWS

    @pl.when(e == 0)
    def _():
        y_ref[...] = x1_ref[...]

    n_rows = cnt_ref[j * ne + e]
    n_blk = (n_rows + rb - 1) // rb
    mine = lax.broadcasted_iota(I32, (SUBLANES, tt), 0) == e % SUBLANES
    g_row = jnp.sum(jnp.where(mine, gt_ref[...], 0.0), axis=0, keepdims=True)
    r_row = jnp.sum(jnp.where(mine, rt_ref[...], 0.0), axis=0, keepdims=True)
    rid = lax.broadcasted_iota(I32, (rb, tt), 0).astype(F32)

    def block(blk, c):
        hit = jnp.broadcast_to(r_row, (rb, tt)) == (rid + (blk * rb).astype(F32))
        p = jnp.where(hit, 1.0, 0.0).astype(BF16)
        xg = _dot(p, h2_ref[...]).astype(BF16)
        a = jnp.minimum(_dot(xg, wg_ref[0]) + bg_ref[0], SWIGLU_LIMIT)
        b = jnp.clip(_dot(xg, wu_ref[0]) + bu_ref[0], -SWIGLU_LIMIT, SWIGLU_LIMIT)
        hid = a * jax.nn.sigmoid(SWIGLU_ALPHA * a) * (b + 1.0)
        o = _dot(hid.astype(BF16), wd_ref[0]) + bd_ref[0]
        g_col = jnp.sum(jnp.where(hit, jnp.broadcast_to(g_row, (rb, tt)), 0.0), axis=1, keepdims=True)
        og = (o * g_col).astype(BF16)
        y_ref[...] += _dot_tn(p, og)
        return c

    lax.fori_loop(0, n_blk, block, 0)


def _moe(h2, x1, gt, rt, cnt, ew, tt):
    t, d = h2.shape
    ne = gt.shape[0]
    nt = t // tt
    f = ew['wg'].shape[-1]
    grid_spec = pltpu.PrefetchScalarGridSpec(
        num_scalar_prefetch=1,
        grid=(nt, ne),
        in_specs=[
            pl.BlockSpec((tt, d), lambda j, e, c: (j, 0)),
            pl.BlockSpec((tt, d), lambda j, e, c: (j, 0)),
            pl.BlockSpec((SUBLANES, tt), lambda j, e, c: (e // SUBLANES, j)),
            pl.BlockSpec((SUBLANES, tt), lambda j, e, c: (e // SUBLANES, j)),
            pl.BlockSpec((1, d, f), lambda j, e, c: (e, 0, 0)),
            pl.BlockSpec((1, d, f), lambda j, e, c: (e, 0, 0)),
            pl.BlockSpec((1, f, d), lambda j, e, c: (e, 0, 0)),
            pl.BlockSpec((1, 1, f), lambda j, e, c: (e, 0, 0)),
            pl.BlockSpec((1, 1, f), lambda j, e, c: (e, 0, 0)),
            pl.BlockSpec((1, 1, d), lambda j, e, c: (e, 0, 0)),
        ],
        out_specs=pl.BlockSpec((tt, d), lambda j, e, c: (j, 0)),
    )
    return pl.pallas_call(
        functools.partial(_moe_kernel, tt=tt),
        grid_spec=grid_spec,
        out_shape=jax.ShapeDtypeStruct((t, d), F32),
        compiler_params=_cp(("arbitrary", "arbitrary")),
        name="moe",
    )(cnt, h2, x1, gt, rt, ew['wg'], ew['wu'], ew['wd'], ew['bg'], ew['bu'], ew['bd'])


def _pad_heads(wmat, n_heads, width):
    d = wmat.shape[0]
    w3 = wmat.reshape(d, n_heads, width)
    return jnp.pad(w3, ((0, 0), (0, 0), (0, LANES - width))).reshape(d, n_heads * LANES)


def _pad_lanes(v, width=LANES):
    v = v.reshape(1, -1)
    return jnp.pad(v, ((0, 0), (0, width - v.shape[1])))


def _rel_bucket(rel):
    half = REL_BUCKETS // 2
    max_exact = half // 2
    n = jnp.abs(rel)
    large = max_exact + (jnp.log(jnp.maximum(n, 1).astype(jnp.float32) / max_exact)
                         / math.log(REL_MAX_DIST / max_exact) * (half - max_exact)).astype(jnp.int32)
    large = jnp.minimum(large, half - 1)
    return jnp.where(rel > 0, half, 0) + jnp.where(n < max_exact, n, large)


def _bias_tiles(rel_bias):
    tk = KEY_TILE
    half = REL_BUCKETS // 2
    max_exact = half // 2
    n_sat = int(math.ceil(max_exact * (REL_MAX_DIST / max_exact) ** ((half - 1 - max_exact) / (half - max_exact)))) + 2
    nd = (n_sat + 2 * tk - 2) // tk + 1
    dd = jnp.arange(nd, dtype=I32)[:, None, None]
    c = jnp.arange(tk, dtype=I32)[None, :, None]
    r = jnp.arange(tk, dtype=I32)[None, None, :]
    bucket = _rel_bucket(c - r - dd * tk)
    onehot = (bucket[..., None] == jnp.arange(REL_BUCKETS, dtype=I32)).astype(F32)
    tiles = jnp.einsum('dcrb,bh->dhcr', onehot, rel_bias.astype(F32) * LOG2E,
                       precision=lax.Precision.HIGHEST)
    return tiles


def _prep_proj(norm1_g, w_in, q_norm_g, k_norm_g, idx_k_norm_g, idx_k_norm_b, d_model):
    ssm_w = d_model // 2
    attn_w = N_HEADS * HEAD_DIM
    kv = N_KV_HEADS * HEAD_DIM
    sizes = [ssm_w, attn_w, kv, kv, IDX_HEADS * IDX_DIM, IDX_DIM, IDX_HEADS, d_model, d_model]
    pts = np.cumsum(sizes)[:-1].tolist()
    wu, wq, wk, wv, wqi, wki, wwi, wga, wgb = jnp.split(w_in, pts, axis=1)
    bf = lambda a: a.astype(BF16)
    blk = np.kron(np.eye(N_KV_HEADS), np.ones((HEAD_DIM, HEAD_DIM))) / HEAD_DIM
    wwit = jnp.pad(wwi.T, ((0, 2 * SUBLANES - IDX_HEADS), (0, 0)))
    return dict(
        g1=norm1_g.reshape(1, -1).astype(F32),
        wu=bf(wu), wq=bf(_pad_heads(wq, N_HEADS, HEAD_DIM)), wk=bf(_pad_heads(wk, N_KV_HEADS, HEAD_DIM)),
        wv=bf(_pad_heads(wv, N_KV_HEADS, HEAD_DIM)), wqi=bf(_pad_heads(wqi, IDX_HEADS, IDX_DIM)),
        wki=bf(_pad_heads(wki, 1, IDX_DIM)), wwit=bf(wwit),
        wga=bf(wga), wgb=bf(wgb), wkc=bf(wk), wvc=bf(wv),
        gq=_pad_lanes(q_norm_g.astype(F32)), gk=_pad_lanes(k_norm_g.astype(F32)),
        gkc=jnp.tile(k_norm_g.astype(F32), N_KV_HEADS).reshape(1, -1),
        gi=_pad_lanes(idx_k_norm_g.astype(F32)), bi=_pad_lanes(idx_k_norm_b.astype(F32)),
        ones_h=jnp.full((LANES, LANES), 1.0 / HEAD_DIM, BF16),
        ones_c=jnp.asarray(blk, BF16),
    )


def _prep_s5(lre, lim, log_dt, b_re, b_im, c_re, c_im, dvec, wa, wb):
    g, p = lre.shape
    ch = b_re.shape[-1]
    lam = lax.complex(lre.astype(F32), lim.astype(F32))
    dt = jnp.exp(log_dt.astype(F32))[:, None]
    a_bar = jnp.exp(lam * dt)
    b_bar = ((a_bar - 1.0) / lam)[:, :, None] * lax.complex(b_re.astype(F32), b_im.astype(F32))
    eye = jnp.eye(g, dtype=F32)
    b_r = jnp.einsum('gpc,gh->gchp', jnp.real(b_bar), eye).reshape(g * ch, g * p)
    b_i = jnp.einsum('gpc,gh->gchp', jnp.imag(b_bar), eye).reshape(g * ch, g * p)
    c_r = jnp.einsum('gcp,gh->gphc', c_re.astype(F32), eye).reshape(g * p, g * ch)
    c_i = jnp.einsum('gcp,gh->gphc', c_im.astype(F32), eye).reshape(g * p, g * ch)
    return dict(
        bmat=jnp.concatenate([b_r, b_i], axis=1).astype(BF16),
        cmat=jnp.concatenate([c_r, -c_i], axis=0).astype(BF16),
        a_re=jnp.real(a_bar).reshape(1, g * p), a_im=jnp.imag(a_bar).reshape(1, g * p),
        d=dvec.reshape(1, -1).astype(F32), wa=wa.astype(BF16), wb=wb.astype(BF16),
    )


def _prep_merge(w_attn_up, w_out, norm2_g, w_router, b_router):
    d = w_attn_up.shape[1]
    wup = jnp.pad(w_attn_up.reshape(N_HEADS, HEAD_DIM, d), ((0, 0), (0, LANES - HEAD_DIM), (0, 0)))
    wr_t = w_router.astype(F32).T
    wr_hi = wr_t.astype(BF16)
    wr_lo = (wr_t - wr_hi.astype(F32)).astype(BF16)
    return dict(
        wup=wup.reshape(N_HEADS * LANES, d).astype(BF16), wout=w_out.astype(BF16),
        g2=norm2_g.reshape(1, -1).astype(F32), wr_hi=wr_hi, wr_lo=wr_lo,
        br=jnp.broadcast_to(b_router.astype(F32)[:, None], (b_router.shape[0], LANES)),
    )


def _prep_moe(wg, bg, wu, bu, wd, bd):
    return dict(wg=wg.astype(BF16), wu=wu.astype(BF16), wd=wd.astype(BF16),
                bg=bg.astype(F32)[:, None, :], bu=bu.astype(F32)[:, None, :], bd=bd.astype(F32)[:, None, :])


def _pick_tile(n, pref):
    t = min(n, pref)
    while n % t:
        t //= 2
    return t


def _pad_axis(a, axis, size):
    pad = [(0, 0)] * a.ndim
    pad[axis] = (0, size - a.shape[axis])
    return jnp.pad(a, pad)


def _trunk_layer(x, past_k, past_v, past_ik, h0_re, h0_im, pw, sw, mw, ew, bias_tiles):
    bsz, seq, d = x.shape
    t = bsz * seq
    tm = _pick_tile(seq, 512)
    u_tb, q, kp, vp, qi, kip, wt, sga, sgb, kc, vc, kic = _proj(x, pw, bsz, seq, tm)

    half = sw['a_re'].shape[1]
    if h0_re is None:
        h0 = jnp.zeros((bsz, 2 * half), F32)
    else:
        h0 = jnp.concatenate([h0_re.reshape(bsz, half), h0_im.reshape(bsz, half)], axis=1).astype(F32)
    tc = _pick_tile(seq, max(1, 512 // bsz))
    ya_tb, hout = _s5(u_tb, h0, sw, bsz, seq, tc)
    groups = half // SSM_STATE
    s_re = hout[:, :half].reshape(bsz, groups, SSM_STATE)
    s_im = hout[:, half:].reshape(bsz, groups, SSM_STATE)

    past = 0 if past_k is None else past_k.shape[1]
    n_keys = past + seq
    lk = -(-n_keys // KEY_BLOCK) * KEY_BLOCK
    kip3 = kip.reshape(bsz, seq, LANES)
    if past:
        lane = jnp.arange(LANES)
        pk = jnp.pad(past_k.astype(F32), ((0, 0), (0, 0), (0, 0), (0, LANES - HEAD_DIM)))
        pk = jnp.where(lane == HEAD_DIM, 1.0, pk).astype(BF16)
        pv = jnp.pad(past_v.astype(F32), ((0, 0), (0, 0), (0, 0), (0, LANES - HEAD_DIM)))
        pv = jnp.where(lane == HEAD_DIM, 1.0, pv).astype(BF16)
        pik = jnp.pad(past_ik.astype(F32), ((0, 0), (0, 0), (0, LANES - IDX_DIM))).astype(BF16)
        k_all = jnp.concatenate([pk.transpose(0, 2, 1, 3), kp], axis=2)
        v_all = jnp.concatenate([pv.transpose(0, 2, 1, 3), vp], axis=2)
        ki_all = jnp.concatenate([pik, kip3], axis=1)
    else:
        k_all, v_all, ki_all = kp, vp, kip3
    k_all = _pad_axis(k_all, 2, lk)
    v_all = _pad_axis(v_all, 2, lk)
    ki_all = _pad_axis(ki_all, 1, lk)
    vt_all = v_all.reshape(bsz, N_KV_HEADS, lk // KEY_BLOCK, KEY_BLOCK, LANES).transpose(0, 1, 2, 4, 3)
    tq = KEY_TILE
    seq_q = -(-seq // tq) * tq
    q_p, qi_p, wt_p = _pad_axis(q, 2, seq_q), _pad_axis(qi, 2, seq_q), _pad_axis(wt, 2, seq_q)
    kf = k_all[..., :HEAD_DIM].astype(F32)
    kmax = jnp.sqrt(jnp.max(jnp.sum(kf * kf, axis=-1), axis=-1)).reshape(-1)
    bfar = bias_tiles[-1, :, 0, 0]
    bmax = jnp.max(jnp.abs(bias_tiles), axis=(0, 2, 3))
    stats = jnp.concatenate([kmax, bmax, bfar]).astype(F32)
    bias_tiles = bias_tiles - bfar[None, :, None, None]
    attn = _dsa(stats, q_p, qi_p, wt_p, k_all, vt_all, ki_all, bias_tiles, bsz, seq_q, past, n_keys, tq)
    attn = attn[:, :seq]

    moe_tile = _pick_tile(t, MOE_TILE)
    tm2 = _pick_tile(moe_tile, 512)
    x1, h2, gt, rt, cnt = _merge(x.reshape(t, d), ya_tb, attn.reshape(t, attn.shape[-1]), sga, sgb, mw,
                                 bsz, seq, tm2, moe_tile)
    cnt_i = cnt[:, :, 0].astype(I32).reshape(-1)
    y = _moe(h2, x1, gt, rt, cnt_i, ew, moe_tile)

    k_new = kc.reshape(bsz, seq, N_KV_HEADS, HEAD_DIM)
    v_new = vc.reshape(bsz, seq, N_KV_HEADS, HEAD_DIM)
    ik_new = kic.reshape(bsz, seq, IDX_DIM)
    return y.reshape(bsz, seq, d), k_new, v_new, ik_new, s_re, s_im


def kernel(x_prompt, x_sample, cache_k, cache_v, cache_idx_k, state_ssm_re, state_ssm_im, rel_bias, norm1_g, w_in, ssm_lambda_re, ssm_lambda_im, ssm_log_dt, ssm_b_re, ssm_b_im, ssm_c_re, ssm_c_im, ssm_d, ssm_w_glu_a, ssm_w_glu_b, q_norm_g, k_norm_g, idx_k_norm_g, idx_k_norm_b, w_attn_up, w_out, norm2_g, moe_w_router, moe_b_router, moe_w_gate, moe_b_gate, moe_w_up, moe_b_up, moe_w_down, moe_b_down):
    depth = w_in.shape[0]
    d_model = x_prompt.shape[-1]
    bias_tiles = _bias_tiles(rel_bias)
    xp, xs = x_prompt, x_sample
    st_p, st_s = [], []
    for l in range(depth):
        pw = _prep_proj(norm1_g[l], w_in[l], q_norm_g[l], k_norm_g[l], idx_k_norm_g[l], idx_k_norm_b[l], d_model)
        sw = _prep_s5(ssm_lambda_re[l], ssm_lambda_im[l], ssm_log_dt[l], ssm_b_re[l], ssm_b_im[l], ssm_c_re[l],
                      ssm_c_im[l], ssm_d[l], ssm_w_glu_a[l], ssm_w_glu_b[l])
        mw = _prep_merge(w_attn_up[l], w_out[l], norm2_g[l], moe_w_router[l], moe_b_router[l])
        ew = _prep_moe(moe_w_gate[l], moe_b_gate[l], moe_w_up[l], moe_b_up[l], moe_w_down[l], moe_b_down[l])
        xp, *sp = _trunk_layer(xp, None, None, None, None, None, pw, sw, mw, ew, bias_tiles)
        xs, *ss = _trunk_layer(xs, cache_k[l], cache_v[l], cache_idx_k[l], state_ssm_re[l], state_ssm_im[l],
                               pw, sw, mw, ew, bias_tiles)
        st_p.append(sp)
        st_s.append(ss)
    outs_p = [jnp.stack([s[i] for s in st_p]) for i in range(5)]
    outs_s = [jnp.stack([s[i] for s in st_s]) for i in range(5)]
    return (xp, xs, *outs_p, *outs_s)
```

```python
import functools
import math

import numpy as np
import jax
import jax.numpy as jnp
from jax import lax
from jax.experimental import pallas as pl
from jax.experimental.pallas import tpu as pltpu

F32 = jnp.float32
BF16 = jnp.bfloat16
I32 = jnp.int32

LANES = 128
SUBLANES = 8
VMEM_LIMIT = 56 * 1024 * 1024

CHUNK = 64
SSM_GROUP_CH = 16
SSM_STATE = 64
N_HEADS = 8
HEAD_DIM = 64
N_KV_HEADS = 2
KV_REP = N_HEADS // N_KV_HEADS
IDX_HEADS = 8
IDX_DIM = 64
TOPK_MAX = 256
REL_BUCKETS = 32
REL_MAX_DIST = 1024
N_EXPERTS = 32
TOP_K = 4
SWIGLU_LIMIT = 7.0
SWIGLU_ALPHA = 1.702
EPS = 1e-6

KEY_TILE = 128
KEY_BLOCK = 256
LOG2E = math.log2(math.e)
NEG_BIG = -1e30
SHIFT_LIMIT = 30.0
SEARCH_FIRST = 8
SEARCH_GROUP = 4
MOE_TILE = 1024
MOE_ROWS = 160
MOE_GROUP = 8


def _cp(sem):
    return pltpu.CompilerParams(dimension_semantics=sem, vmem_limit_bytes=VMEM_LIMIT)


def _dot(a, b):
    return jnp.dot(a, b, preferred_element_type=F32)


def _dot_nt(a, b):
    return lax.dot_general(a, b, (((1,), (1,)), ((), ())), preferred_element_type=F32)


def _dot_tn(a, b):
    return lax.dot_general(a, b, (((0,), (0,)), ((), ())), preferred_element_type=F32)


def _split(a):
    hi = a.astype(BF16)
    lo = (a - hi.astype(F32)).astype(BF16)
    return hi, lo


def _dot_split(a, g):
    hi, lo = _split(a)
    return _dot(hi, g) + _dot(lo, g)


def _proj_kernel(x_ref, g1_ref, wu_ref, wq_ref, wk_ref, wv_ref, wqi_ref, wki_ref, wwit_ref, wga_ref, wgb_ref,
                 wkc_ref, wvc_ref, gq_ref, gk_ref, gkc_ref, gi_ref, bi_ref, ones_h_ref, ones_c_ref,
                 u_ref, q_ref, kp_ref, vp_ref, qi_ref, kip_ref, wt_ref, sga_ref, sgb_ref,
                 kc_ref, vc_ref, kic_ref):
    x = x_ref[...]
    ms = jnp.mean(x * x, axis=-1, keepdims=True)
    hn = (x * lax.rsqrt(ms + EPS) * g1_ref[...]).astype(BF16)
    ones_h = ones_h_ref[...]
    lane = lax.broadcasted_iota(I32, (x.shape[0], LANES), 1)

    u_ref[...] = _dot(hn, wu_ref[...]).astype(BF16)

    q = _dot(hn, wq_ref[...])
    scale = HEAD_DIM ** -0.5 * LOG2E
    for h in range(N_HEADS):
        qh = q[:, h * LANES:(h + 1) * LANES]
        msq = _dot_split(qh * qh, ones_h)
        q_ref[h] = (qh * lax.rsqrt(msq + EPS) * (gq_ref[...] * scale)).astype(BF16)

    k = _dot(hn, wk_ref[...])
    for g in range(N_KV_HEADS):
        kg = k[:, g * LANES:(g + 1) * LANES]
        msk = _dot_split(kg * kg, ones_h)
        kn = kg * lax.rsqrt(msk + EPS) * gk_ref[...]
        kp_ref[g] = jnp.where(lane == HEAD_DIM, 1.0, kn).astype(BF16)

    v = _dot(hn, wv_ref[...])
    for g in range(N_KV_HEADS):
        vg = v[:, g * LANES:(g + 1) * LANES]
        vp_ref[g] = jnp.where(lane == HEAD_DIM, 1.0, vg).astype(BF16)

    qi = _dot(hn, wqi_ref[...])
    for h in range(IDX_HEADS):
        qi_ref[h] = qi[:, h * LANES:(h + 1) * LANES].astype(BF16)

    ki = _dot(hn, wki_ref[...])
    mu = _dot_split(ki, ones_h)
    xc = jnp.where(lane < IDX_DIM, ki - mu, 0.0)
    var = _dot_split(xc * xc, ones_h)
    kin = xc * lax.rsqrt(var + EPS) * gi_ref[...] + bi_ref[...]
    kip_ref[...] = kin.astype(BF16)
    kic_ref[...] = kin[:, :IDX_DIM]

    wt = _dot_nt(wwit_ref[...], hn)
    wt_ref[...] = wt[0:IDX_HEADS, :] * (IDX_HEADS ** -0.5 * IDX_DIM ** -0.5)

    sga_ref[...] = jax.nn.sigmoid(_dot(hn, wga_ref[...])).astype(BF16)
    sgb_ref[...] = jax.nn.sigmoid(_dot(hn, wgb_ref[...])).astype(BF16)

    kc = _dot(hn, wkc_ref[...])
    mskc = _dot_split(kc * kc, ones_c_ref[...])
    kc_ref[...] = kc * lax.rsqrt(mskc + EPS) * gkc_ref[...]
    vc_ref[...] = _dot(hn, wvc_ref[...])


def _proj(x, pw, bsz, seq, tm):
    d = x.shape[-1]
    nt = seq // tm
    t = bsz * seq
    x2 = x.reshape(t, d)

    def tok(b, i):
        return (b * nt + i, 0)

    def cst(b, i):
        return (0, 0)

    def wspec(a):
        return pl.BlockSpec(a.shape, cst)

    weights = [pw['g1'], pw['wu'], pw['wq'], pw['wk'], pw['wv'], pw['wqi'], pw['wki'], pw['wwit'], pw['wga'],
               pw['wgb'], pw['wkc'], pw['wvc'], pw['gq'], pw['gk'], pw['gkc'], pw['gi'], pw['bi'],
               pw['ones_h'], pw['ones_c']]
    ssm_w = pw['wu'].shape[1]
    out_shape = (
        jax.ShapeDtypeStruct((seq, bsz * ssm_w), BF16),
        jax.ShapeDtypeStruct((bsz, N_HEADS, seq, LANES), BF16),
        jax.ShapeDtypeStruct((bsz, N_KV_HEADS, seq, LANES), BF16),
        jax.ShapeDtypeStruct((bsz, N_KV_HEADS, seq, LANES), BF16),
        jax.ShapeDtypeStruct((bsz, IDX_HEADS, seq, LANES), BF16),
        jax.ShapeDtypeStruct((t, LANES), BF16),
        jax.ShapeDtypeStruct((bsz, IDX_HEADS, seq), F32),
        jax.ShapeDtypeStruct((t, d), BF16),
        jax.ShapeDtypeStruct((t, d), BF16),
        jax.ShapeDtypeStruct((t, N_KV_HEADS * HEAD_DIM), F32),
        jax.ShapeDtypeStruct((t, N_KV_HEADS * HEAD_DIM), F32),
        jax.ShapeDtypeStruct((t, IDX_DIM), F32),
    )

    def hm(nh):
        return pl.BlockSpec((None, nh, tm, LANES), lambda b, i: (b, 0, i, 0))

    out_specs = (
        pl.BlockSpec((tm, ssm_w), lambda b, i: (i, b)),
        hm(N_HEADS), hm(N_KV_HEADS), hm(N_KV_HEADS), hm(IDX_HEADS),
        pl.BlockSpec((tm, LANES), tok),
        pl.BlockSpec((None, IDX_HEADS, tm), lambda b, i: (b, 0, i)),
        pl.BlockSpec((tm, d), tok), pl.BlockSpec((tm, d), tok),
        pl.BlockSpec((tm, N_KV_HEADS * HEAD_DIM), tok), pl.BlockSpec((tm, N_KV_HEADS * HEAD_DIM), tok),
        pl.BlockSpec((tm, IDX_DIM), tok),
    )
    return pl.pallas_call(
        _proj_kernel,
        grid=(bsz, nt),
        in_specs=[pl.BlockSpec((tm, d), tok)] + [wspec(a) for a in weights],
        out_specs=out_specs,
        out_shape=out_shape,
        compiler_params=_cp(("arbitrary", "arbitrary")),
        name="proj",
    )(x2, *weights)


def _gelu_tanh(x):
    return 0.5 * x * (1.0 + jnp.tanh(math.sqrt(2.0 / math.pi) * (x + 0.044715 * (x * x * x))))


def _s5_kernel(u_ref, h0_ref, bmat_ref, are_ref, aim_ref, cmat_ref, dvec_ref, wa_ref, wb_ref,
               ya_ref, hout_ref, state_ref, bu_ref, *, bsz, tc, strip):
    s = pl.program_id(0)
    half = are_ref.shape[1]

    @pl.when(s == 0)
    def _():
        state_ref[...] = h0_ref[...]

    u = u_ref[...]
    bu_ref[...] = _dot(u, bmat_ref[...])

    for c0 in range(0, half, strip):
        ar = jnp.broadcast_to(are_ref[:, c0:c0 + strip], (bsz, strip))
        ai = jnp.broadcast_to(aim_ref[:, c0:c0 + strip], (bsz, strip))
        hr0 = state_ref[:, c0:c0 + strip]
        hi0 = state_ref[:, half + c0:half + c0 + strip]

        def step(t, carry):
            hr, hi = carry
            r0 = pl.multiple_of(t * bsz, bsz)
            br = bu_ref[pl.ds(r0, bsz), c0:c0 + strip]
            bi = bu_ref[pl.ds(r0, bsz), half + c0:half + c0 + strip]
            nr = ar * hr - ai * hi + br
            ni = ar * hi + ai * hr + bi
            bu_ref[pl.ds(r0, bsz), c0:c0 + strip] = nr
            bu_ref[pl.ds(r0, bsz), half + c0:half + c0 + strip] = ni
            return nr, ni

        hr, hi = lax.fori_loop(0, tc, step, (hr0, hi0))
        state_ref[:, c0:c0 + strip] = hr
        state_ref[:, half + c0:half + c0 + strip] = hi

    y = _dot(bu_ref[...].astype(BF16), cmat_ref[...]) + dvec_ref[...] * u.astype(F32)
    g = _gelu_tanh(y).astype(BF16)
    ya = _dot(g, wa_ref[...]) * jax.nn.sigmoid(_dot(g, wb_ref[...]))
    ya_ref[...] = ya.astype(BF16)

    @pl.when(s == pl.num_programs(0) - 1)
    def _():
        hout_ref[...] = state_ref[...]


def _s5(u_tb, h0, sw, bsz, seq, tc):
    rows = tc * bsz
    ssm_w = sw['bmat'].shape[0]
    two_half = sw['bmat'].shape[1]
    half = two_half // 2
    d = sw['wa'].shape[1]
    u2 = u_tb.reshape(seq * bsz, ssm_w)
    strip = min(512, half)

    def cst(s):
        return (0, 0)

    consts = [h0, sw['bmat'], sw['a_re'], sw['a_im'], sw['cmat'], sw['d'], sw['wa'], sw['wb']]
    ya, hout = pl.pallas_call(
        functools.partial(_s5_kernel, bsz=bsz, tc=tc, strip=strip),
        grid=(seq // tc,),
        in_specs=[pl.BlockSpec((rows, ssm_w), lambda s: (s, 0))] + [pl.BlockSpec(a.shape, cst) for a in consts],
        out_specs=(pl.BlockSpec((rows, d), lambda s: (s, 0)), pl.BlockSpec((bsz, two_half), cst)),
        out_shape=(jax.ShapeDtypeStruct((seq * bsz, d), BF16), jax.ShapeDtypeStruct((bsz, two_half), F32)),
        scratch_shapes=[pltpu.VMEM((bsz, two_half), F32), pltpu.VMEM((rows, two_half), F32)],
        compiler_params=_cp(("arbitrary",)),
        name="s5",
    )(u2, *consts)
    return ya.reshape(seq, bsz * d), hout


def _f2key(x):
    b = lax.bitcast_convert_type(x, I32)
    return b ^ ((b >> 31) & 0x7FFFFFFF)


def _key2f(k):
    return lax.bitcast_convert_type(k ^ ((k >> 31) & 0x7FFFFFFF), F32)


def _dsa_kernel(st_ref, q_ref, qi_ref, wt_ref, k_ref, vt_ref, ki_ref, bias_ref, o_ref,
                s_ref, lo_ref, hi_ref, clo_ref, glo_ref, ghi_ref, side_ref, q2_ref, mrow_ref, acc_ref,
                *, bsz, tq, past, n_keys, topk, nkt, nd, idx_bits):
    b_id = pl.program_id(0)
    i = pl.program_id(1)
    kb = KEY_BLOCK
    sl = SUBLANES
    q0 = past + i * tq
    last_chunk = (q0 + tq - 1) // CHUNK
    n_kt = jnp.minimum(nkt, ((last_chunk + 1) * CHUNK + kb - 1) // kb)
    d0 = q0 // KEY_TILE

    krow = lax.broadcasted_iota(I32, (kb, tq), 0)
    q_chunk = (q0 + lax.broadcasted_iota(I32, (kb, tq), 1)) // CHUNK
    qc8 = (q0 + lax.broadcasted_iota(I32, (sl, tq), 1)) // CHUNK
    n_adm = jnp.minimum((qc8 + 1) * CHUNK, n_keys)
    n_admf = n_adm.astype(F32)
    needf = jnp.minimum(topk, n_adm).astype(F32)

    def bcast(x):
        return jnp.broadcast_to(x[0:1, :], (kb, tq))

    def rep(x):
        return jnp.broadcast_to(x, (sl, tq))

    qi = qi_ref[...].reshape(IDX_HEADS * tq, LANES)

    def score_block(kt, masked):
        k0 = pl.multiple_of(kt * kb, kb)
        s = _dot_nt(ki_ref[pl.ds(k0, kb), :], qi)
        sc = jnp.zeros((kb, tq), F32)
        for h in range(IDX_HEADS):
            sc = sc + wt_ref[h:h + 1, :] * jnp.maximum(s[:, h * tq:(h + 1) * tq], 0.0)
        if masked:
            kpos = k0 + krow
            adm = ((kpos // CHUNK) <= q_chunk) & (kpos < n_keys)
            sc = jnp.where(adm, sc, -jnp.inf)
        s_ref[kt] = sc

    def score_pair(j, c):
        score_block(2 * j, False)
        score_block(2 * j + 1, False)
        return c

    lax.fori_loop(0, (n_kt - 1) // 2, score_pair, 0)

    @pl.when((n_kt - 1) % 2 == 1)
    def _():
        score_block(n_kt - 2, False)

    score_block(n_kt - 1, True)

    part = 4 * sl

    def fold(x, op):
        x = x.reshape(kb // part, part, tq)
        acc = x[0]
        for j in range(1, kb // part):
            acc = op(acc, x[j])
        return acc

    def count(pred):
        def body(kt, c):
            return c + fold(jnp.where(pred(s_ref[kt], kt), 1.0, 0.0), jnp.add)
        c = lax.fori_loop(0, n_kt, body, jnp.zeros((part, tq), F32))
        return rep(jnp.sum(c, axis=0, keepdims=True))

    def minmax(kt, c):
        mx, mn = c
        s = s_ref[kt]
        return (jnp.maximum(mx, fold(s, jnp.maximum)),
                jnp.minimum(mn, fold(jnp.where(s == -jnp.inf, jnp.inf, s), jnp.minimum)))

    mx, mn = lax.fori_loop(0, n_kt, minmax,
                           (jnp.full((part, tq), -jnp.inf, F32), jnp.full((part, tq), jnp.inf, F32)))
    lo_ref[...] = rep(jnp.min(mn, axis=0, keepdims=True))
    hi_ref[...] = _key2f(_f2key(rep(jnp.max(mx, axis=0, keepdims=True))) + 1)
    def odds(cnt):
        c = jnp.clip(cnt, 0.5, n_admf - 0.5)
        return jnp.log((n_admf - c) / c)

    target = odds(needf - 0.5)
    clo_ref[...] = n_admf
    glo_ref[...] = target - odds(n_admf)
    ghi_ref[...] = target - odds(jnp.zeros((sl, tq), F32))
    side_ref[...] = jnp.zeros((sl, tq), F32)

    def searching(lo, hi, clo):
        return (_f2key(hi) > _f2key(lo) + 1) & (clo > needf)

    def refine(it, c):
        lo, hi, clo = lo_ref[...], hi_ref[...], clo_ref[...]
        glo, ghi, side = glo_ref[...], ghi_ref[...], side_ref[...]
        k_t = _f2key(lo + (hi - lo) * (glo / (glo - ghi)))
        t = _key2f(jnp.minimum(jnp.maximum(k_t, _f2key(lo) + 1), _f2key(hi) - 1))
        tb = bcast(t)
        cnt = count(lambda s, kt: s >= tb)
        g = target - odds(cnt)
        open_ = searching(lo, hi, clo)
        up = open_ & (cnt >= needf)
        dn = open_ & (cnt < needf)
        lo_ref[...] = jnp.where(up, t, lo)
        clo_ref[...] = jnp.where(up, cnt, clo)
        hi_ref[...] = jnp.where(dn, t, hi)
        glo_ref[...] = jnp.where(up, g, jnp.where(dn & (side < 0.0), glo * 0.5, glo))
        ghi_ref[...] = jnp.where(dn, g, jnp.where(up & (side > 0.0), ghi * 0.5, ghi))
        side_ref[...] = jnp.where(up, 1.0, jnp.where(dn, -1.0, side))
        return c

    def snap():
        lo, hi, clo = lo_ref[...], hi_ref[...], clo_ref[...]
        lo_b, hi_b = bcast(lo), bcast(hi)

        def body(kt, c):
            a, b = c
            s = s_ref[kt]
            return (jnp.minimum(a, fold(jnp.where(s >= lo_b, s, jnp.inf), jnp.minimum)),
                    jnp.maximum(b, fold(jnp.where(s < hi_b, s, -jnp.inf), jnp.maximum)))

        a, b = lax.fori_loop(0, n_kt, body,
                             (jnp.full((part, tq), jnp.inf, F32), jnp.full((part, tq), -jnp.inf, F32)))
        open_ = searching(lo, hi, clo)
        lo_ref[...] = jnp.where(open_, rep(jnp.min(a, axis=0, keepdims=True)), lo)
        hi_ref[...] = jnp.where(open_, _key2f(_f2key(rep(jnp.max(b, axis=0, keepdims=True))) + 1), hi)

    def n_searching():
        return jnp.max(jnp.where(searching(lo_ref[...], hi_ref[...], clo_ref[...]), 1.0, 0.0))

    def group(c):
        grp, _ = c
        lax.fori_loop(0, jnp.where(grp == 0, SEARCH_FIRST, SEARCH_GROUP), refine, 0)
        snap()
        return grp + 1, n_searching()

    lax.while_loop(lambda c: c[1] > 0.0, group, (jnp.int32(0), n_searching()))
    thr = lo_ref[...]
    thr_b = bcast(thr)

    n_tied = jnp.max(jnp.where(clo_ref[...] > needf, 1.0, 0.0))

    @pl.when(n_tied > 0.0)
    def _():
        rem = needf - count(lambda s, kt: s > thr_b)
        lo_ref[...] = jnp.zeros((sl, tq), F32)
        hi_ref[...] = jnp.full((sl, tq), float(nkt * kb), F32)
        krowf = krow.astype(F32)

        def bisect_idx(it, c):
            lo = lo_ref[...]
            hi = hi_ref[...]
            mid = jnp.floor((lo + hi) * 0.5)
            mid_b = bcast(mid)
            ok = count(lambda s, kt: (s == thr_b) & ((kt * kb).astype(F32) + krowf < mid_b)) >= rem
            hi_ref[...] = jnp.where(ok, mid, hi)
            lo_ref[...] = jnp.where(ok, lo, mid)
            return c

        lax.fori_loop(0, idx_bits, bisect_idx, 0)
        cut_b = bcast(hi_ref[...])

        def drop(kt, c):
            s = s_ref[kt]
            s_ref[kt] = jnp.where((s == thr_b) & ((kt * kb).astype(F32) + krowf >= cut_b), -jnp.inf, s)
            return c

        lax.fori_loop(0, n_kt, drop, 0)

    rows_g = KV_REP * tq
    qf = q_ref[...].reshape(N_HEADS * tq, LANES).astype(F32)
    qn = jnp.sqrt(jnp.sum(qf * qf, axis=1, keepdims=True))
    lane = lax.broadcasted_iota(I32, (tq, LANES), 1)
    worst = jnp.float32(0.0)
    for h in range(N_HEADS):
        kmax = st_ref[b_id * N_KV_HEADS + h // KV_REP]
        bmax = st_ref[bsz * N_KV_HEADS + h]
        bfar = st_ref[bsz * N_KV_HEADS + N_HEADS + h]
        bound = qn[h * tq:(h + 1) * tq, :] * (kmax * 1.01) + (bmax + 0.1)
        worst = jnp.maximum(worst, jnp.max(bound))
        q2_ref[h * tq:(h + 1) * tq, :] = jnp.where(lane == HEAD_DIM, bfar - bound,
                                                   qf[h * tq:(h + 1) * tq, :]).astype(BF16)
    n_far = jnp.clip((d0 - nd) // 2 + 1, 0, n_kt)

    def logits(kt, g, near, exact):
        k0 = pl.multiple_of(kt * kb, kb)
        maskadd = jnp.where(s_ref[kt] >= thr_b, 0.0, NEG_BIG)
        s = _dot_nt(k_ref[g, pl.ds(k0, kb), :], q2_ref[g * rows_g:(g + 1) * rows_g, :])
        if near:
            da = jnp.clip(d0 - 2 * kt, 0, nd - 1)
            db = jnp.clip(d0 - 2 * kt - 1, 0, nd - 1)
        parts = []
        for r in range(KV_REP):
            h = g * KV_REP + r
            add = maskadd - mrow_ref[0:1, h * tq:(h + 1) * tq] if exact else maskadd
            if near:
                add = jnp.concatenate([bias_ref[da, h], bias_ref[db, h]], axis=0) + add
            parts.append(s[:, r * tq:(r + 1) * tq] + add)
        return jnp.concatenate(parts, axis=1)

    def over_blocks(fn):
        def far_pair(j, c):
            fn((2 * j, 2 * j + 1), False)
            return c
        lax.fori_loop(0, n_far // 2, far_pair, 0)

        @pl.when(n_far % 2 == 1)
        def _():
            fn((n_far - 1,), False)

        n_near = n_kt - n_far

        def near_pair(j, c):
            fn((n_far + 2 * j, n_far + 2 * j + 1), True)
            return c
        lax.fori_loop(0, n_near // 2, near_pair, 0)

        @pl.when(n_near % 2 == 1)
        def _():
            fn((n_kt - 1,), True)

    def attend(exact):
        acc_ref[...] = jnp.zeros(acc_ref.shape, F32)

        def blocks(kts, near):
            for g in range(N_KV_HEADS):
                pv = None
                for kt in kts:
                    p = jnp.exp2(logits(kt, g, near, exact)).astype(BF16)
                    d = _dot(vt_ref[g, kt], p)
                    pv = d if pv is None else pv + d
                acc_ref[g] += pv
        over_blocks(blocks)

    @pl.when(worst <= SHIFT_LIMIT)
    def _():
        attend(False)

    @pl.when(worst > SHIFT_LIMIT)
    def _():
        mrow_ref[...] = jnp.full(mrow_ref.shape, NEG_BIG, F32)

        def blocks(kts, near):
            for g in range(N_KV_HEADS):
                for kt in kts:
                    mx = jnp.max(logits(kt, g, near, False), axis=0, keepdims=True)
                    cur = mrow_ref[:, g * rows_g:(g + 1) * rows_g]
                    mrow_ref[:, g * rows_g:(g + 1) * rows_g] = jnp.maximum(cur, jnp.broadcast_to(mx, (sl, rows_g)))
        over_blocks(blocks)
        attend(True)

    for g in range(N_KV_HEADS):
        acc = acc_ref[g]
        og = acc / acc[HEAD_DIM:HEAD_DIM + 1, :]
        for r in range(KV_REP):
            h = g * KV_REP + r
            o_ref[:, h * LANES:(h + 1) * LANES] = og[:, r * tq:(r + 1) * tq].T.astype(BF16)


def _dsa(stats, q, qi, wt, k_all, vt_all, ki_all, bias_tiles, bsz, seq, past, n_keys, tq):
    lk = k_all.shape[2]
    nkt = lk // KEY_BLOCK
    topk = min(TOPK_MAX, n_keys // 4)
    nd = bias_tiles.shape[0]
    nq = seq // tq
    assert past % KEY_TILE == 0 and tq == KEY_TILE
    idx_bits = int(math.ceil(math.log2(lk))) + 1
    kern = functools.partial(_dsa_kernel, bsz=bsz, tq=tq, past=past, n_keys=n_keys, topk=topk, nkt=nkt, nd=nd,
                             idx_bits=idx_bits)
    row_state = pltpu.VMEM((SUBLANES, tq), F32)
    grid_spec = pltpu.PrefetchScalarGridSpec(
        num_scalar_prefetch=1,
        grid=(bsz, nq),
        in_specs=[
            pl.BlockSpec((None, N_HEADS, tq, LANES), lambda b, i, s: (b, 0, i, 0)),
            pl.BlockSpec((None, IDX_HEADS, tq, LANES), lambda b, i, s: (b, 0, i, 0)),
            pl.BlockSpec((None, IDX_HEADS, tq), lambda b, i, s: (b, 0, i)),
            pl.BlockSpec((None, N_KV_HEADS, lk, LANES), lambda b, i, s: (b, 0, 0, 0)),
            pl.BlockSpec((None, N_KV_HEADS, nkt, LANES, KEY_BLOCK), lambda b, i, s: (b, 0, 0, 0, 0)),
            pl.BlockSpec((None, lk, LANES), lambda b, i, s: (b, 0, 0)),
            pl.BlockSpec(bias_tiles.shape, lambda b, i, s: (0, 0, 0, 0)),
        ],
        out_specs=pl.BlockSpec((None, tq, N_HEADS * LANES), lambda b, i, s: (b, i, 0)),
        scratch_shapes=[
            pltpu.VMEM((nkt, KEY_BLOCK, tq), F32),
            row_state, row_state, row_state, row_state, row_state, row_state,
            pltpu.VMEM((N_HEADS * tq, LANES), BF16),
            pltpu.VMEM((SUBLANES, N_HEADS * tq), F32),
            pltpu.VMEM((N_KV_HEADS, LANES, KV_REP * tq), F32),
        ],
    )
    return pl.pallas_call(
        kern,
        grid_spec=grid_spec,
        out_shape=jax.ShapeDtypeStruct((bsz, seq, N_HEADS * LANES), BF16),
        compiler_params=_cp(("arbitrary", "arbitrary")),
        name="dsa",
    )(stats, q, qi, wt, k_all, vt_all, ki_all, bias_tiles)


def _merge_kernel(x_ref, ya_ref, at_ref, sga_ref, sgb_ref, wup_ref, wout_ref, g2_ref, wr_hi_ref, wr_lo_ref, br_ref,
                  x1_ref, h2_ref, gt_ref, rt_ref, cnt_ref, run_ref, *, tm, sub):
    step = pl.program_id(0)

    @pl.when(step % sub == 0)
    def _():
        run_ref[...] = jnp.zeros(run_ref.shape, F32)

    yb = _dot(at_ref[...], wup_ref[...])
    merged = sga_ref[...].astype(F32) * ya_ref[...].astype(F32) + sgb_ref[...].astype(F32) * yb
    x1 = x_ref[...] + _dot(merged.astype(BF16), wout_ref[...])
    x1_ref[...] = x1
    ms = jnp.mean(x1 * x1, axis=-1, keepdims=True)
    h2 = x1 * lax.rsqrt(ms + EPS) * g2_ref[...]
    h2_hi, h2_lo = _split(h2)
    h2_ref[...] = h2_hi

    wr_hi = wr_hi_ref[...]
    logit = (_dot_nt(wr_hi, h2_hi) + _dot_nt(wr_hi, h2_lo) + _dot_nt(wr_lo_ref[...], h2_hi)) + br_ref[:, 0:1]
    ne = logit.shape[0]
    eid = lax.broadcasted_iota(I32, (ne, tm), 0).astype(F32)
    selb = jnp.zeros((ne, tm), F32)
    tops = []
    picks = []
    for _ in range(TOP_K):
        mx = jnp.max(logit, axis=0, keepdims=True)
        pick = jnp.min(jnp.where(logit == mx, eid, float(ne)), axis=0, keepdims=True)
        hit = eid == pick
        selb = jnp.where(hit, 1.0, selb)
        logit = jnp.where(hit, -jnp.inf, logit)
        tops.append(mx)
        picks.append(hit)
    ex = [jnp.exp(t - tops[0]) for t in tops]
    den = ex[0] + ex[1] + ex[2] + ex[3]
    gate = jnp.zeros((ne, tm), F32)
    for hit, e in zip(picks, ex):
        gate = jnp.where(hit, e / den, gate)
    gt_ref[...] = gate

    sel = selb > 0.5
    selb = selb.astype(BF16)
    r_i = lax.broadcasted_iota(I32, (tm, tm), 0)
    c_i = lax.broadcasted_iota(I32, (tm, tm), 1)
    tri = jnp.where(r_i < c_i, 1.0, 0.0).astype(BF16)
    run = run_ref[...]
    rank = _dot(selb, tri) + jnp.broadcast_to(run[:, 0:1], (ne, tm))
    rt_ref[...] = jnp.where(sel, rank, -1.0)
    run = run + _dot(selb, jnp.ones((tm, LANES), BF16))
    run_ref[...] = run
    cnt_ref[...] = run


def _merge(x2, ya_tb, attn, sga, sgb, mw, bsz, seq, tm, moe_tile):
    t, d = x2.shape
    if seq % tm == 0:
        nt = seq // tm
        ya, ya_spec = ya_tb, pl.BlockSpec((tm, d), lambda i: (i % nt, i // nt))
    else:
        ya = ya_tb.reshape(seq, bsz, d).transpose(1, 0, 2).reshape(t, d)
        ya_spec = pl.BlockSpec((tm, d), lambda i: (i, 0))
    sub = moe_tile // tm
    ne = mw['wr_hi'].shape[0]

    def tok(i):
        return (i, 0)

    def cst(i):
        return (0, 0)

    consts = [mw['wup'], mw['wout'], mw['g2'], mw['wr_hi'], mw['wr_lo'], mw['br']]
    return pl.pallas_call(
        functools.partial(_merge_kernel, tm=tm, sub=sub),
        grid=(t // tm,),
        in_specs=[
            pl.BlockSpec((tm, d), tok),
            ya_spec,
            pl.BlockSpec((tm, attn.shape[-1]), tok),
            pl.BlockSpec((tm, d), tok),
            pl.BlockSpec((tm, d), tok),
        ] + [pl.BlockSpec(a.shape, cst) for a in consts],
        out_specs=(
            pl.BlockSpec((tm, d), tok),
            pl.BlockSpec((tm, d), tok),
            pl.BlockSpec((ne, tm), lambda i: (0, i)),
            pl.BlockSpec((ne, tm), lambda i: (0, i)),
            pl.BlockSpec((None, ne, LANES), lambda i: (i // sub, 0, 0)),
        ),
        out_shape=(
            jax.ShapeDtypeStruct((t, d), F32),
            jax.ShapeDtypeStruct((t, d), BF16),
            jax.ShapeDtypeStruct((ne, t), F32),
            jax.ShapeDtypeStruct((ne, t), F32),
            jax.ShapeDtypeStruct((t // moe_tile, ne, LANES), F32),
        ),
        scratch_shapes=[pltpu.VMEM((ne, LANES), F32)],
        compiler_params=_cp(("arbitrary",)),
        name="merge",
    )(x2, ya, attn, sga, sgb, *consts)


def _moe_kernel(cnt_ref, h2_ref, x1_ref, gt_ref, rt_ref, wg_ref, wu_ref, wd_ref, bg_ref, bu_ref, bd_ref, y_ref,
                pg_ref, og_ref, *, tt):
    j = pl.program_id(0)
    e = pl.program_id(1)
    ne = pl.num_programs(1)
    rb = MOE_ROWS
    slot = e % MOE_GROUP

    @pl.when(e == 0)
    def _():
        y_ref[...] = x1_ref[...]

    n_rows = cnt_ref[j * ne + e]
    n_blk = (n_rows + rb - 1) // rb
    mine = lax.broadcasted_iota(I32, (SUBLANES, tt), 0) == e % SUBLANES
    g_row = jnp.sum(jnp.where(mine, gt_ref[...], 0.0), axis=0, keepdims=True)
    r_row = jnp.sum(jnp.where(mine, rt_ref[...], 0.0), axis=0, keepdims=True)
    rid = lax.broadcasted_iota(I32, (rb, tt), 0).astype(F32)

    def expert_rows(blk):
        hit = jnp.broadcast_to(r_row, (rb, tt)) == (rid + (blk * rb).astype(F32))
        p = jnp.where(hit, 1.0, 0.0).astype(BF16)
        xg = _dot(p, h2_ref[...]).astype(BF16)
        a = jnp.minimum(_dot(xg, wg_ref[0]) + bg_ref[0], SWIGLU_LIMIT)
        b = jnp.clip(_dot(xg, wu_ref[0]) + bu_ref[0], -SWIGLU_LIMIT, SWIGLU_LIMIT)
        hid = a * jax.nn.sigmoid(SWIGLU_ALPHA * a) * (b + 1.0)
        o = _dot(hid.astype(BF16), wd_ref[0]) + bd_ref[0]
        g_col = jnp.sum(jnp.where(hit, jnp.broadcast_to(g_row, (rb, tt)), 0.0), axis=1, keepdims=True)
        return p, (o * g_col).astype(BF16)

    p, og = expert_rows(jnp.int32(0))
    r0 = pl.multiple_of(slot * rb, rb)
    pg_ref[pl.ds(r0, rb), :] = p
    og_ref[pl.ds(r0, rb), :] = og

    @pl.when(slot == MOE_GROUP - 1)
    def _():
        y_ref[...] += _dot_tn(pg_ref[...], og_ref[...])

    def overflow(blk, c):
        p, og = expert_rows(blk)
        y_ref[...] += _dot_tn(p, og)
        return c

    lax.fori_loop(1, n_blk, overflow, 0)


def _moe(h2, x1, gt, rt, cnt, ew, tt):
    t, d = h2.shape
    ne = gt.shape[0]
    nt = t // tt
    f = ew['wg'].shape[-1]
    grid_spec = pltpu.PrefetchScalarGridSpec(
        num_scalar_prefetch=1,
        grid=(nt, ne),
        in_specs=[
            pl.BlockSpec((tt, d), lambda j, e, c: (j, 0)),
            pl.BlockSpec((tt, d), lambda j, e, c: (j, 0)),
            pl.BlockSpec((SUBLANES, tt), lambda j, e, c: (e // SUBLANES, j)),
            pl.BlockSpec((SUBLANES, tt), lambda j, e, c: (e // SUBLANES, j)),
            pl.BlockSpec((1, d, f), lambda j, e, c: (e, 0, 0)),
            pl.BlockSpec((1, d, f), lambda j, e, c: (e, 0, 0)),
            pl.BlockSpec((1, f, d), lambda j, e, c: (e, 0, 0)),
            pl.BlockSpec((1, 1, f), lambda j, e, c: (e, 0, 0)),
            pl.BlockSpec((1, 1, f), lambda j, e, c: (e, 0, 0)),
            pl.BlockSpec((1, 1, d), lambda j, e, c: (e, 0, 0)),
        ],
        out_specs=pl.BlockSpec((tt, d), lambda j, e, c: (j, 0)),
        scratch_shapes=[pltpu.VMEM((MOE_GROUP * MOE_ROWS, tt), BF16),
                        pltpu.VMEM((MOE_GROUP * MOE_ROWS, d), BF16)],
    )
    assert ne % MOE_GROUP == 0
    return pl.pallas_call(
        functools.partial(_moe_kernel, tt=tt),
        grid_spec=grid_spec,
        out_shape=jax.ShapeDtypeStruct((t, d), F32),
        compiler_params=_cp(("arbitrary", "arbitrary")),
        name="moe",
    )(cnt, h2, x1, gt, rt, ew['wg'], ew['wu'], ew['wd'], ew['bg'], ew['bu'], ew['bd'])


def _pad_heads(wmat, n_heads, width):
    d = wmat.shape[0]
    w3 = wmat.reshape(d, n_heads, width)
    return jnp.pad(w3, ((0, 0), (0, 0), (0, LANES - width))).reshape(d, n_heads * LANES)


def _pad_lanes(v, width=LANES):
    v = v.reshape(1, -1)
    return jnp.pad(v, ((0, 0), (0, width - v.shape[1])))


def _rel_bucket(rel):
    half = REL_BUCKETS // 2
    max_exact = half // 2
    n = jnp.abs(rel)
    large = max_exact + (jnp.log(jnp.maximum(n, 1).astype(jnp.float32) / max_exact)
                         / math.log(REL_MAX_DIST / max_exact) * (half - max_exact)).astype(jnp.int32)
    large = jnp.minimum(large, half - 1)
    return jnp.where(rel > 0, half, 0) + jnp.where(n < max_exact, n, large)


def _bias_tiles(rel_bias):
    tk = KEY_TILE
    half = REL_BUCKETS // 2
    max_exact = half // 2
    n_sat = int(math.ceil(max_exact * (REL_MAX_DIST / max_exact) ** ((half - 1 - max_exact) / (half - max_exact)))) + 2
    nd = (n_sat + 2 * tk - 2) // tk + 1
    dd = jnp.arange(nd, dtype=I32)[:, None, None]
    c = jnp.arange(tk, dtype=I32)[None, :, None]
    r = jnp.arange(tk, dtype=I32)[None, None, :]
    bucket = _rel_bucket(c - r - dd * tk)
    onehot = (bucket[..., None] == jnp.arange(REL_BUCKETS, dtype=I32)).astype(F32)
    tiles = jnp.einsum('dcrb,bh->dhcr', onehot, rel_bias.astype(F32) * LOG2E,
                       precision=lax.Precision.HIGHEST)
    return tiles


def _prep_proj(norm1_g, w_in, q_norm_g, k_norm_g, idx_k_norm_g, idx_k_norm_b, d_model):
    ssm_w = d_model // 2
    attn_w = N_HEADS * HEAD_DIM
    kv = N_KV_HEADS * HEAD_DIM
    sizes = [ssm_w, attn_w, kv, kv, IDX_HEADS * IDX_DIM, IDX_DIM, IDX_HEADS, d_model, d_model]
    pts = np.cumsum(sizes)[:-1].tolist()
    wu, wq, wk, wv, wqi, wki, wwi, wga, wgb = jnp.split(w_in, pts, axis=1)
    bf = lambda a: a.astype(BF16)
    blk = np.kron(np.eye(N_KV_HEADS), np.ones((HEAD_DIM, HEAD_DIM))) / HEAD_DIM
    wwit = jnp.pad(wwi.T, ((0, 2 * SUBLANES - IDX_HEADS), (0, 0)))
    return dict(
        g1=norm1_g.reshape(1, -1).astype(F32),
        wu=bf(wu), wq=bf(_pad_heads(wq, N_HEADS, HEAD_DIM)), wk=bf(_pad_heads(wk, N_KV_HEADS, HEAD_DIM)),
        wv=bf(_pad_heads(wv, N_KV_HEADS, HEAD_DIM)), wqi=bf(_pad_heads(wqi, IDX_HEADS, IDX_DIM)),
        wki=bf(_pad_heads(wki, 1, IDX_DIM)), wwit=bf(wwit),
        wga=bf(wga), wgb=bf(wgb), wkc=bf(wk), wvc=bf(wv),
        gq=_pad_lanes(q_norm_g.astype(F32)), gk=_pad_lanes(k_norm_g.astype(F32)),
        gkc=jnp.tile(k_norm_g.astype(F32), N_KV_HEADS).reshape(1, -1),
        gi=_pad_lanes(idx_k_norm_g.astype(F32)), bi=_pad_lanes(idx_k_norm_b.astype(F32)),
        ones_h=jnp.full((LANES, LANES), 1.0 / HEAD_DIM, BF16),
        ones_c=jnp.asarray(blk, BF16),
    )


def _prep_s5(lre, lim, log_dt, b_re, b_im, c_re, c_im, dvec, wa, wb):
    g, p = lre.shape
    ch = b_re.shape[-1]
    lam = lax.complex(lre.astype(F32), lim.astype(F32))
    dt = jnp.exp(log_dt.astype(F32))[:, None]
    a_bar = jnp.exp(lam * dt)
    b_bar = ((a_bar - 1.0) / lam)[:, :, None] * lax.complex(b_re.astype(F32), b_im.astype(F32))
    eye = jnp.eye(g, dtype=F32)
    b_r = jnp.einsum('gpc,gh->gchp', jnp.real(b_bar), eye).reshape(g * ch, g * p)
    b_i = jnp.einsum('gpc,gh->gchp', jnp.imag(b_bar), eye).reshape(g * ch, g * p)
    c_r = jnp.einsum('gcp,gh->gphc', c_re.astype(F32), eye).reshape(g * p, g * ch)
    c_i = jnp.einsum('gcp,gh->gphc', c_im.astype(F32), eye).reshape(g * p, g * ch)
    return dict(
        bmat=jnp.concatenate([b_r, b_i], axis=1).astype(BF16),
        cmat=jnp.concatenate([c_r, -c_i], axis=0).astype(BF16),
        a_re=jnp.real(a_bar).reshape(1, g * p), a_im=jnp.imag(a_bar).reshape(1, g * p),
        d=dvec.reshape(1, -1).astype(F32), wa=wa.astype(BF16), wb=wb.astype(BF16),
    )


def _prep_merge(w_attn_up, w_out, norm2_g, w_router, b_router):
    d = w_attn_up.shape[1]
    wup = jnp.pad(w_attn_up.reshape(N_HEADS, HEAD_DIM, d), ((0, 0), (0, LANES - HEAD_DIM), (0, 0)))
    wr_t = w_router.astype(F32).T
    wr_hi = wr_t.astype(BF16)
    wr_lo = (wr_t - wr_hi.astype(F32)).astype(BF16)
    return dict(
        wup=wup.reshape(N_HEADS * LANES, d).astype(BF16), wout=w_out.astype(BF16),
        g2=norm2_g.reshape(1, -1).astype(F32), wr_hi=wr_hi, wr_lo=wr_lo,
        br=jnp.broadcast_to(b_router.astype(F32)[:, None], (b_router.shape[0], LANES)),
    )


def _prep_moe(wg, bg, wu, bu, wd, bd):
    return dict(wg=wg.astype(BF16), wu=wu.astype(BF16), wd=wd.astype(BF16),
                bg=bg.astype(F32)[:, None, :], bu=bu.astype(F32)[:, None, :], bd=bd.astype(F32)[:, None, :])


def _pick_tile(n, pref):
    t = min(n, pref)
    while n % t:
        t //= 2
    return t


def _pad_axis(a, axis, size):
    pad = [(0, 0)] * a.ndim
    pad[axis] = (0, size - a.shape[axis])
    return jnp.pad(a, pad)


def _trunk_layer(x, past_k, past_v, past_ik, h0_re, h0_im, pw, sw, mw, ew, bias_tiles):
    bsz, seq, d = x.shape
    t = bsz * seq
    tm = _pick_tile(seq, 512)
    u_tb, q, kp, vp, qi, kip, wt, sga, sgb, kc, vc, kic = _proj(x, pw, bsz, seq, tm)

    half = sw['a_re'].shape[1]
    if h0_re is None:
        h0 = jnp.zeros((bsz, 2 * half), F32)
    else:
        h0 = jnp.concatenate([h0_re.reshape(bsz, half), h0_im.reshape(bsz, half)], axis=1).astype(F32)
    tc = _pick_tile(seq, max(1, 512 // bsz))
    ya_tb, hout = _s5(u_tb, h0, sw, bsz, seq, tc)
    groups = half // SSM_STATE
    s_re = hout[:, :half].reshape(bsz, groups, SSM_STATE)
    s_im = hout[:, half:].reshape(bsz, groups, SSM_STATE)

    past = 0 if past_k is None else past_k.shape[1]
    n_keys = past + seq
    lk = -(-n_keys // KEY_BLOCK) * KEY_BLOCK
    kip3 = kip.reshape(bsz, seq, LANES)
    if past:
        lane = jnp.arange(LANES)
        pk = jnp.pad(past_k.astype(F32), ((0, 0), (0, 0), (0, 0), (0, LANES - HEAD_DIM)))
        pk = jnp.where(lane == HEAD_DIM, 1.0, pk).astype(BF16)
        pv = jnp.pad(past_v.astype(F32), ((0, 0), (0, 0), (0, 0), (0, LANES - HEAD_DIM)))
        pv = jnp.where(lane == HEAD_DIM, 1.0, pv).astype(BF16)
        pik = jnp.pad(past_ik.astype(F32), ((0, 0), (0, 0), (0, LANES - IDX_DIM))).astype(BF16)
        k_all = jnp.concatenate([pk.transpose(0, 2, 1, 3), kp], axis=2)
        v_all = jnp.concatenate([pv.transpose(0, 2, 1, 3), vp], axis=2)
        ki_all = jnp.concatenate([pik, kip3], axis=1)
    else:
        k_all, v_all, ki_all = kp, vp, kip3
    k_all = _pad_axis(k_all, 2, lk)
    v_all = _pad_axis(v_all, 2, lk)
    ki_all = _pad_axis(ki_all, 1, lk)
    vt_all = v_all.reshape(bsz, N_KV_HEADS, lk // KEY_BLOCK, KEY_BLOCK, LANES).transpose(0, 1, 2, 4, 3)
    tq = KEY_TILE
    seq_q = -(-seq // tq) * tq
    q_p, qi_p, wt_p = _pad_axis(q, 2, seq_q), _pad_axis(qi, 2, seq_q), _pad_axis(wt, 2, seq_q)
    kf = k_all[..., :HEAD_DIM].astype(F32)
    kmax = jnp.sqrt(jnp.max(jnp.sum(kf * kf, axis=-1), axis=-1)).reshape(-1)
    bfar = bias_tiles[-1, :, 0, 0]
    bmax = jnp.max(jnp.abs(bias_tiles), axis=(0, 2, 3))
    stats = jnp.concatenate([kmax, bmax, bfar]).astype(F32)
    bias_tiles = bias_tiles - bfar[None, :, None, None]
    attn = _dsa(stats, q_p, qi_p, wt_p, k_all, vt_all, ki_all, bias_tiles, bsz, seq_q, past, n_keys, tq)
    attn = attn[:, :seq]

    moe_tile = _pick_tile(t, MOE_TILE)
    tm2 = _pick_tile(moe_tile, 512)
    x1, h2, gt, rt, cnt = _merge(x.reshape(t, d), ya_tb, attn.reshape(t, attn.shape[-1]), sga, sgb, mw,
                                 bsz, seq, tm2, moe_tile)
    cnt_i = cnt[:, :, 0].astype(I32).reshape(-1)
    y = _moe(h2, x1, gt, rt, cnt_i, ew, moe_tile)

    k_new = kc.reshape(bsz, seq, N_KV_HEADS, HEAD_DIM)
    v_new = vc.reshape(bsz, seq, N_KV_HEADS, HEAD_DIM)
    ik_new = kic.reshape(bsz, seq, IDX_DIM)
    return y.reshape(bsz, seq, d), k_new, v_new, ik_new, s_re, s_im


def kernel(x_prompt, x_sample, cache_k, cache_v, cache_idx_k, state_ssm_re, state_ssm_im, rel_bias, norm1_g, w_in, ssm_lambda_re, ssm_lambda_im, ssm_log_dt, ssm_b_re, ssm_b_im, ssm_c_re, ssm_c_im, ssm_d, ssm_w_glu_a, ssm_w_glu_b, q_norm_g, k_norm_g, idx_k_norm_g, idx_k_norm_b, w_attn_up, w_out, norm2_g, moe_w_router, moe_b_router, moe_w_gate, moe_b_gate, moe_w_up, moe_b_up, moe_w_down, moe_b_down):
    depth = w_in.shape[0]
    d_model = x_prompt.shape[-1]
    bias_tiles = _bias_tiles(rel_bias)
    xp, xs = x_prompt, x_sample
    st_p, st_s = [], []
    for l in range(depth):
        pw = _prep_proj(norm1_g[l], w_in[l], q_norm_g[l], k_norm_g[l], idx_k_norm_g[l], idx_k_norm_b[l], d_model)
        sw = _prep_s5(ssm_lambda_re[l], ssm_lambda_im[l], ssm_log_dt[l], ssm_b_re[l], ssm_b_im[l], ssm_c_re[l],
                      ssm_c_im[l], ssm_d[l], ssm_w_glu_a[l], ssm_w_glu_b[l])
        mw = _prep_merge(w_attn_up[l], w_out[l], norm2_g[l], moe_w_router[l], moe_b_router[l])
        ew = _prep_moe(moe_w_gate[l], moe_b_gate[l], moe_w_up[l], moe_b_up[l], moe_w_down[l], moe_b_down[l])
        xp, *sp = _trunk_layer(xp, None, None, None, None, None, pw, sw, mw, ew, bias_tiles)
        xs, *ss = _trunk_layer(xs, cache_k[l], cache_v[l], cache_idx_k[l], state_ssm_re[l], state_ssm_im[l],
                               pw, sw, mw, ew, bias_tiles)
        st_p.append(sp)
        st_s.append(ss)
    outs_p = [jnp.stack([s[i] for s in st_p]) for i in range(5)]
    outs_s = [jnp.stack([s[i] for s in st_s]) for i in range(5)]
    return (xp, xs, *outs_p, *outs_s)
```

```python
import functools
import math

import numpy as np
import jax
import jax.numpy as jnp
from jax import lax
from jax.experimental import pallas as pl
from jax.experimental.pallas import tpu as pltpu

F32 = jnp.float32
BF16 = jnp.bfloat16
I32 = jnp.int32

LANES = 128
SUBLANES = 8
VMEM_LIMIT = 56 * 1024 * 1024

CHUNK = 64
SSM_GROUP_CH = 16
SSM_STATE = 64
N_HEADS = 8
HEAD_DIM = 64
N_KV_HEADS = 2
KV_REP = N_HEADS // N_KV_HEADS
IDX_HEADS = 8
IDX_DIM = 64
TOPK_MAX = 256
REL_BUCKETS = 32
REL_MAX_DIST = 1024
N_EXPERTS = 32
TOP_K = 4
SWIGLU_LIMIT = 7.0
SWIGLU_ALPHA = 1.702
EPS = 1e-6

KEY_TILE = 128
KEY_BLOCK = 256
LOG2E = math.log2(math.e)
NEG_BIG = -1e30
SHIFT_LIMIT = 30.0
SEARCH_FIRST = 8
SEARCH_GROUP = 4
MOE_TILE = 1024
MOE_ROWS = 160
MOE_GROUP = 8
MOE_PAIR = 2


def _cp(sem):
    return pltpu.CompilerParams(dimension_semantics=sem, vmem_limit_bytes=VMEM_LIMIT)


def _dot(a, b):
    return jnp.dot(a, b, preferred_element_type=F32)


def _dot_nt(a, b):
    return lax.dot_general(a, b, (((1,), (1,)), ((), ())), preferred_element_type=F32)


def _dot_tn(a, b):
    return lax.dot_general(a, b, (((0,), (0,)), ((), ())), preferred_element_type=F32)


def _split(a):
    hi = a.astype(BF16)
    lo = (a - hi.astype(F32)).astype(BF16)
    return hi, lo


def _dot_split(a, g):
    hi, lo = _split(a)
    return _dot(hi, g) + _dot(lo, g)


def _proj_kernel(x_ref, g1_ref, wu_ref, wq_ref, wk_ref, wv_ref, wqi_ref, wki_ref, wwit_ref, wga_ref, wgb_ref,
                 wkc_ref, wvc_ref, gq_ref, gk_ref, gkc_ref, gi_ref, bi_ref, ones_h_ref, ones_c_ref,
                 u_ref, q_ref, kp_ref, vp_ref, qi_ref, kip_ref, wt_ref, sga_ref, sgb_ref,
                 kc_ref, vc_ref, kic_ref):
    x = x_ref[...]
    ms = jnp.mean(x * x, axis=-1, keepdims=True)
    hn = (x * lax.rsqrt(ms + EPS) * g1_ref[...]).astype(BF16)
    ones_h = ones_h_ref[...]
    lane = lax.broadcasted_iota(I32, (x.shape[0], LANES), 1)

    u_ref[...] = _dot(hn, wu_ref[...]).astype(BF16)

    q = _dot(hn, wq_ref[...])
    scale = HEAD_DIM ** -0.5 * LOG2E
    for h in range(N_HEADS):
        qh = q[:, h * LANES:(h + 1) * LANES]
        msq = _dot_split(qh * qh, ones_h)
        q_ref[h] = (qh * lax.rsqrt(msq + EPS) * (gq_ref[...] * scale)).astype(BF16)

    k = _dot(hn, wk_ref[...])
    for g in range(N_KV_HEADS):
        kg = k[:, g * LANES:(g + 1) * LANES]
        msk = _dot_split(kg * kg, ones_h)
        kn = kg * lax.rsqrt(msk + EPS) * gk_ref[...]
        kp_ref[g] = jnp.where(lane == HEAD_DIM, 1.0, kn).astype(BF16)

    v = _dot(hn, wv_ref[...])
    for g in range(N_KV_HEADS):
        vg = v[:, g * LANES:(g + 1) * LANES]
        vp_ref[g] = jnp.where(lane == HEAD_DIM, 1.0, vg).astype(BF16)

    qi = _dot(hn, wqi_ref[...])
    for h in range(IDX_HEADS):
        qi_ref[h] = qi[:, h * LANES:(h + 1) * LANES].astype(BF16)

    ki = _dot(hn, wki_ref[...])
    mu = _dot_split(ki, ones_h)
    xc = jnp.where(lane < IDX_DIM, ki - mu, 0.0)
    var = _dot_split(xc * xc, ones_h)
    kin = xc * lax.rsqrt(var + EPS) * gi_ref[...] + bi_ref[...]
    kip_ref[...] = kin.astype(BF16)
    kic_ref[...] = kin[:, :IDX_DIM]

    wt = _dot_nt(wwit_ref[...], hn)
    wt_ref[...] = wt[0:IDX_HEADS, :] * (IDX_HEADS ** -0.5 * IDX_DIM ** -0.5)

    sga_ref[...] = jax.nn.sigmoid(_dot(hn, wga_ref[...])).astype(BF16)
    sgb_ref[...] = jax.nn.sigmoid(_dot(hn, wgb_ref[...])).astype(BF16)

    kc = _dot(hn, wkc_ref[...])
    mskc = _dot_split(kc * kc, ones_c_ref[...])
    kc_ref[...] = kc * lax.rsqrt(mskc + EPS) * gkc_ref[...]
    vc_ref[...] = _dot(hn, wvc_ref[...])


def _proj(x, pw, bsz, seq, tm):
    d = x.shape[-1]
    nt = seq // tm
    t = bsz * seq
    x2 = x.reshape(t, d)

    def tok(b, i):
        return (b * nt + i, 0)

    def cst(b, i):
        return (0, 0)

    def wspec(a):
        return pl.BlockSpec(a.shape, cst)

    weights = [pw['g1'], pw['wu'], pw['wq'], pw['wk'], pw['wv'], pw['wqi'], pw['wki'], pw['wwit'], pw['wga'],
               pw['wgb'], pw['wkc'], pw['wvc'], pw['gq'], pw['gk'], pw['gkc'], pw['gi'], pw['bi'],
               pw['ones_h'], pw['ones_c']]
    ssm_w = pw['wu'].shape[1]
    out_shape = (
        jax.ShapeDtypeStruct((seq, bsz * ssm_w), BF16),
        jax.ShapeDtypeStruct((bsz, N_HEADS, seq, LANES), BF16),
        jax.ShapeDtypeStruct((bsz, N_KV_HEADS, seq, LANES), BF16),
        jax.ShapeDtypeStruct((bsz, N_KV_HEADS, seq, LANES), BF16),
        jax.ShapeDtypeStruct((bsz, IDX_HEADS, seq, LANES), BF16),
        jax.ShapeDtypeStruct((t, LANES), BF16),
        jax.ShapeDtypeStruct((bsz, IDX_HEADS, seq), F32),
        jax.ShapeDtypeStruct((t, d), BF16),
        jax.ShapeDtypeStruct((t, d), BF16),
        jax.ShapeDtypeStruct((t, N_KV_HEADS * HEAD_DIM), F32),
        jax.ShapeDtypeStruct((t, N_KV_HEADS * HEAD_DIM), F32),
        jax.ShapeDtypeStruct((t, IDX_DIM), F32),
    )

    def hm(nh):
        return pl.BlockSpec((None, nh, tm, LANES), lambda b, i: (b, 0, i, 0))

    out_specs = (
        pl.BlockSpec((tm, ssm_w), lambda b, i: (i, b)),
        hm(N_HEADS), hm(N_KV_HEADS), hm(N_KV_HEADS), hm(IDX_HEADS),
        pl.BlockSpec((tm, LANES), tok),
        pl.BlockSpec((None, IDX_HEADS, tm), lambda b, i: (b, 0, i)),
        pl.BlockSpec((tm, d), tok), pl.BlockSpec((tm, d), tok),
        pl.BlockSpec((tm, N_KV_HEADS * HEAD_DIM), tok), pl.BlockSpec((tm, N_KV_HEADS * HEAD_DIM), tok),
        pl.BlockSpec((tm, IDX_DIM), tok),
    )
    return pl.pallas_call(
        _proj_kernel,
        grid=(bsz, nt),
        in_specs=[pl.BlockSpec((tm, d), tok)] + [wspec(a) for a in weights],
        out_specs=out_specs,
        out_shape=out_shape,
        compiler_params=_cp(("arbitrary", "arbitrary")),
        name="proj",
    )(x2, *weights)


def _gelu_tanh(x):
    return 0.5 * x * (1.0 + jnp.tanh(math.sqrt(2.0 / math.pi) * (x + 0.044715 * (x * x * x))))


def _s5_kernel(u_ref, h0_ref, bmat_ref, are_ref, aim_ref, cmat_ref, dvec_ref, wa_ref, wb_ref,
               ya_ref, hout_ref, state_ref, bu_ref, *, bsz, tc, strip):
    s = pl.program_id(0)
    half = are_ref.shape[1]

    @pl.when(s == 0)
    def _():
        state_ref[...] = h0_ref[...]

    u = u_ref[...]
    bu_ref[...] = _dot(u, bmat_ref[...])

    for c0 in range(0, half, strip):
        ar = jnp.broadcast_to(are_ref[:, c0:c0 + strip], (bsz, strip))
        ai = jnp.broadcast_to(aim_ref[:, c0:c0 + strip], (bsz, strip))
        hr0 = state_ref[:, c0:c0 + strip]
        hi0 = state_ref[:, half + c0:half + c0 + strip]

        def step(t, carry):
            hr, hi = carry
            r0 = pl.multiple_of(t * bsz, bsz)
            br = bu_ref[pl.ds(r0, bsz), c0:c0 + strip]
            bi = bu_ref[pl.ds(r0, bsz), half + c0:half + c0 + strip]
            nr = ar * hr - ai * hi + br
            ni = ar * hi + ai * hr + bi
            bu_ref[pl.ds(r0, bsz), c0:c0 + strip] = nr
            bu_ref[pl.ds(r0, bsz), half + c0:half + c0 + strip] = ni
            return nr, ni

        hr, hi = lax.fori_loop(0, tc, step, (hr0, hi0))
        state_ref[:, c0:c0 + strip] = hr
        state_ref[:, half + c0:half + c0 + strip] = hi

    y = _dot(bu_ref[...].astype(BF16), cmat_ref[...]) + dvec_ref[...] * u.astype(F32)
    g = _gelu_tanh(y).astype(BF16)
    ya = _dot(g, wa_ref[...]) * jax.nn.sigmoid(_dot(g, wb_ref[...]))
    ya_ref[...] = ya.astype(BF16)

    @pl.when(s == pl.num_programs(0) - 1)
    def _():
        hout_ref[...] = state_ref[...]


def _s5(u_tb, h0, sw, bsz, seq, tc):
    rows = tc * bsz
    ssm_w = sw['bmat'].shape[0]
    two_half = sw['bmat'].shape[1]
    half = two_half // 2
    d = sw['wa'].shape[1]
    u2 = u_tb.reshape(seq * bsz, ssm_w)
    strip = min(512, half)

    def cst(s):
        return (0, 0)

    consts = [h0, sw['bmat'], sw['a_re'], sw['a_im'], sw['cmat'], sw['d'], sw['wa'], sw['wb']]
    ya, hout = pl.pallas_call(
        functools.partial(_s5_kernel, bsz=bsz, tc=tc, strip=strip),
        grid=(seq // tc,),
        in_specs=[pl.BlockSpec((rows, ssm_w), lambda s: (s, 0))] + [pl.BlockSpec(a.shape, cst) for a in consts],
        out_specs=(pl.BlockSpec((rows, d), lambda s: (s, 0)), pl.BlockSpec((bsz, two_half), cst)),
        out_shape=(jax.ShapeDtypeStruct((seq * bsz, d), BF16), jax.ShapeDtypeStruct((bsz, two_half), F32)),
        scratch_shapes=[pltpu.VMEM((bsz, two_half), F32), pltpu.VMEM((rows, two_half), F32)],
        compiler_params=_cp(("arbitrary",)),
        name="s5",
    )(u2, *consts)
    return ya.reshape(seq, bsz * d), hout


def _f2key(x):
    b = lax.bitcast_convert_type(x, I32)
    return b ^ ((b >> 31) & 0x7FFFFFFF)


def _key2f(k):
    return lax.bitcast_convert_type(k ^ ((k >> 31) & 0x7FFFFFFF), F32)


def _dsa_kernel(st_ref, q_ref, qi_ref, wt_ref, k_ref, vt_ref, ki_ref, bias_ref, o_ref,
                s_ref, lo_ref, hi_ref, clo_ref, glo_ref, ghi_ref, side_ref, q2_ref, mrow_ref, acc_ref,
                *, bsz, tq, past, n_keys, topk, nkt, nd, idx_bits):
    b_id = pl.program_id(0)
    i = pl.program_id(1)
    kb = KEY_BLOCK
    sl = SUBLANES
    q0 = past + i * tq
    last_chunk = (q0 + tq - 1) // CHUNK
    n_kt = jnp.minimum(nkt, ((last_chunk + 1) * CHUNK + kb - 1) // kb)
    d0 = q0 // KEY_TILE

    krow = lax.broadcasted_iota(I32, (kb, tq), 0)
    q_chunk = (q0 + lax.broadcasted_iota(I32, (kb, tq), 1)) // CHUNK
    qc8 = (q0 + lax.broadcasted_iota(I32, (sl, tq), 1)) // CHUNK
    n_adm = jnp.minimum((qc8 + 1) * CHUNK, n_keys)
    n_admf = n_adm.astype(F32)
    needf = jnp.minimum(topk, n_adm).astype(F32)

    def bcast(x):
        return jnp.broadcast_to(x[0:1, :], (kb, tq))

    def rep(x):
        return jnp.broadcast_to(x, (sl, tq))

    qi = qi_ref[...].reshape(IDX_HEADS * tq, LANES)

    def score_block(kt, masked):
        k0 = pl.multiple_of(kt * kb, kb)
        s = _dot_nt(ki_ref[pl.ds(k0, kb), :], qi)
        sc = jnp.zeros((kb, tq), F32)
        for h in range(IDX_HEADS):
            sc = sc + wt_ref[h:h + 1, :] * jnp.maximum(s[:, h * tq:(h + 1) * tq], 0.0)
        if masked:
            kpos = k0 + krow
            adm = ((kpos // CHUNK) <= q_chunk) & (kpos < n_keys)
            sc = jnp.where(adm, sc, -jnp.inf)
        s_ref[kt] = sc

    def score_pair(j, c):
        score_block(2 * j, False)
        score_block(2 * j + 1, False)
        return c

    lax.fori_loop(0, (n_kt - 1) // 2, score_pair, 0)

    @pl.when((n_kt - 1) % 2 == 1)
    def _():
        score_block(n_kt - 2, False)

    score_block(n_kt - 1, True)

    part = 4 * sl

    def fold(x, op):
        x = x.reshape(kb // part, part, tq)
        acc = x[0]
        for j in range(1, kb // part):
            acc = op(acc, x[j])
        return acc

    def count(pred):
        def body(kt, c):
            return c + fold(jnp.where(pred(s_ref[kt], kt), 1.0, 0.0), jnp.add)
        c = lax.fori_loop(0, n_kt, body, jnp.zeros((part, tq), F32))
        return rep(jnp.sum(c, axis=0, keepdims=True))

    def minmax(kt, c):
        mx, mn = c
        s = s_ref[kt]
        return (jnp.maximum(mx, fold(s, jnp.maximum)),
                jnp.minimum(mn, fold(jnp.where(s == -jnp.inf, jnp.inf, s), jnp.minimum)))

    mx, mn = lax.fori_loop(0, n_kt, minmax,
                           (jnp.full((part, tq), -jnp.inf, F32), jnp.full((part, tq), jnp.inf, F32)))
    lo_ref[...] = rep(jnp.min(mn, axis=0, keepdims=True))
    hi_ref[...] = _key2f(_f2key(rep(jnp.max(mx, axis=0, keepdims=True))) + 1)
    def odds(cnt):
        c = jnp.clip(cnt, 0.5, n_admf - 0.5)
        return jnp.log((n_admf - c) / c)

    target = odds(needf - 0.5)
    clo_ref[...] = n_admf
    glo_ref[...] = target - odds(n_admf)
    ghi_ref[...] = target - odds(jnp.zeros((sl, tq), F32))
    side_ref[...] = jnp.zeros((sl, tq), F32)

    def searching(lo, hi, clo):
        return (_f2key(hi) > _f2key(lo) + 1) & (clo > needf)

    def refine(it, c):
        lo, hi, clo = lo_ref[...], hi_ref[...], clo_ref[...]
        glo, ghi, side = glo_ref[...], ghi_ref[...], side_ref[...]
        k_t = _f2key(lo + (hi - lo) * (glo / (glo - ghi)))
        t = _key2f(jnp.minimum(jnp.maximum(k_t, _f2key(lo) + 1), _f2key(hi) - 1))
        tb = bcast(t)
        cnt = count(lambda s, kt: s >= tb)
        g = target - odds(cnt)
        open_ = searching(lo, hi, clo)
        up = open_ & (cnt >= needf)
        dn = open_ & (cnt < needf)
        lo_ref[...] = jnp.where(up, t, lo)
        clo_ref[...] = jnp.where(up, cnt, clo)
        hi_ref[...] = jnp.where(dn, t, hi)
        glo_ref[...] = jnp.where(up, g, jnp.where(dn & (side < 0.0), glo * 0.5, glo))
        ghi_ref[...] = jnp.where(dn, g, jnp.where(up & (side > 0.0), ghi * 0.5, ghi))
        side_ref[...] = jnp.where(up, 1.0, jnp.where(dn, -1.0, side))
        return c

    def snap():
        lo, hi, clo = lo_ref[...], hi_ref[...], clo_ref[...]
        lo_b, hi_b = bcast(lo), bcast(hi)

        def body(kt, c):
            a, b = c
            s = s_ref[kt]
            return (jnp.minimum(a, fold(jnp.where(s >= lo_b, s, jnp.inf), jnp.minimum)),
                    jnp.maximum(b, fold(jnp.where(s < hi_b, s, -jnp.inf), jnp.maximum)))

        a, b = lax.fori_loop(0, n_kt, body,
                             (jnp.full((part, tq), jnp.inf, F32), jnp.full((part, tq), -jnp.inf, F32)))
        open_ = searching(lo, hi, clo)
        lo_ref[...] = jnp.where(open_, rep(jnp.min(a, axis=0, keepdims=True)), lo)
        hi_ref[...] = jnp.where(open_, _key2f(_f2key(rep(jnp.max(b, axis=0, keepdims=True))) + 1), hi)

    def n_searching():
        return jnp.max(jnp.where(searching(lo_ref[...], hi_ref[...], clo_ref[...]), 1.0, 0.0))

    def group(c):
        grp, _ = c
        lax.fori_loop(0, jnp.where(grp == 0, SEARCH_FIRST, SEARCH_GROUP), refine, 0)
        snap()
        return grp + 1, n_searching()

    lax.while_loop(lambda c: c[1] > 0.0, group, (jnp.int32(0), n_searching()))
    thr = lo_ref[...]
    thr_b = bcast(thr)

    n_tied = jnp.max(jnp.where(clo_ref[...] > needf, 1.0, 0.0))

    @pl.when(n_tied > 0.0)
    def _():
        rem = needf - count(lambda s, kt: s > thr_b)
        lo_ref[...] = jnp.zeros((sl, tq), F32)
        hi_ref[...] = jnp.full((sl, tq), float(nkt * kb), F32)
        krowf = krow.astype(F32)

        def bisect_idx(it, c):
            lo = lo_ref[...]
            hi = hi_ref[...]
            mid = jnp.floor((lo + hi) * 0.5)
            mid_b = bcast(mid)
            ok = count(lambda s, kt: (s == thr_b) & ((kt * kb).astype(F32) + krowf < mid_b)) >= rem
            hi_ref[...] = jnp.where(ok, mid, hi)
            lo_ref[...] = jnp.where(ok, lo, mid)
            return c

        lax.fori_loop(0, idx_bits, bisect_idx, 0)
        cut_b = bcast(hi_ref[...])

        def drop(kt, c):
            s = s_ref[kt]
            s_ref[kt] = jnp.where((s == thr_b) & ((kt * kb).astype(F32) + krowf >= cut_b), -jnp.inf, s)
            return c

        lax.fori_loop(0, n_kt, drop, 0)

    rows_g = KV_REP * tq
    qf = q_ref[...].reshape(N_HEADS * tq, LANES).astype(F32)
    qn = jnp.sqrt(jnp.sum(qf * qf, axis=1, keepdims=True))
    lane = lax.broadcasted_iota(I32, (tq, LANES), 1)
    worst = jnp.float32(0.0)
    for h in range(N_HEADS):
        kmax = st_ref[b_id * N_KV_HEADS + h // KV_REP]
        bmax = st_ref[bsz * N_KV_HEADS + h]
        bfar = st_ref[bsz * N_KV_HEADS + N_HEADS + h]
        bound = qn[h * tq:(h + 1) * tq, :] * (kmax * 1.01) + (bmax + 0.1)
        worst = jnp.maximum(worst, jnp.max(bound))
        q2_ref[h * tq:(h + 1) * tq, :] = jnp.where(lane == HEAD_DIM, bfar - bound,
                                                   qf[h * tq:(h + 1) * tq, :]).astype(BF16)
    n_far = jnp.clip((d0 - nd) // 2 + 1, 0, n_kt)

    def logits(kt, g, near, exact):
        k0 = pl.multiple_of(kt * kb, kb)
        maskadd = jnp.where(s_ref[kt] >= thr_b, 0.0, NEG_BIG)
        s = _dot_nt(k_ref[g, pl.ds(k0, kb), :], q2_ref[g * rows_g:(g + 1) * rows_g, :])
        if near:
            da = jnp.clip(d0 - 2 * kt, 0, nd - 1)
            db = jnp.clip(d0 - 2 * kt - 1, 0, nd - 1)
        parts = []
        for r in range(KV_REP):
            h = g * KV_REP + r
            add = maskadd - mrow_ref[0:1, h * tq:(h + 1) * tq] if exact else maskadd
            if near:
                add = jnp.concatenate([bias_ref[da, h], bias_ref[db, h]], axis=0) + add
            parts.append(s[:, r * tq:(r + 1) * tq] + add)
        return jnp.concatenate(parts, axis=1)

    def over_blocks(fn):
        def far_pair(j, c):
            fn((2 * j, 2 * j + 1), False)
            return c
        lax.fori_loop(0, n_far // 2, far_pair, 0)

        @pl.when(n_far % 2 == 1)
        def _():
            fn((n_far - 1,), False)

        n_near = n_kt - n_far

        def near_pair(j, c):
            fn((n_far + 2 * j, n_far + 2 * j + 1), True)
            return c
        lax.fori_loop(0, n_near // 2, near_pair, 0)

        @pl.when(n_near % 2 == 1)
        def _():
            fn((n_kt - 1,), True)

    def attend(exact):
        acc_ref[...] = jnp.zeros(acc_ref.shape, F32)

        def blocks(kts, near):
            for g in range(N_KV_HEADS):
                pv = None
                for kt in kts:
                    p = jnp.exp2(logits(kt, g, near, exact)).astype(BF16)
                    d = _dot(vt_ref[g, kt], p)
                    pv = d if pv is None else pv + d
                acc_ref[g] += pv
        over_blocks(blocks)

    @pl.when(worst <= SHIFT_LIMIT)
    def _():
        attend(False)

    @pl.when(worst > SHIFT_LIMIT)
    def _():
        mrow_ref[...] = jnp.full(mrow_ref.shape, NEG_BIG, F32)

        def blocks(kts, near):
            for g in range(N_KV_HEADS):
                for kt in kts:
                    mx = jnp.max(logits(kt, g, near, False), axis=0, keepdims=True)
                    cur = mrow_ref[:, g * rows_g:(g + 1) * rows_g]
                    mrow_ref[:, g * rows_g:(g + 1) * rows_g] = jnp.maximum(cur, jnp.broadcast_to(mx, (sl, rows_g)))
        over_blocks(blocks)
        attend(True)

    for g in range(N_KV_HEADS):
        acc = acc_ref[g]
        og = acc / acc[HEAD_DIM:HEAD_DIM + 1, :]
        for r in range(KV_REP):
            h = g * KV_REP + r
            o_ref[:, h * LANES:(h + 1) * LANES] = og[:, r * tq:(r + 1) * tq].T.astype(BF16)


def _dsa(stats, q, qi, wt, k_all, vt_all, ki_all, bias_tiles, bsz, seq, past, n_keys, tq):
    lk = k_all.shape[2]
    nkt = lk // KEY_BLOCK
    topk = min(TOPK_MAX, n_keys // 4)
    nd = bias_tiles.shape[0]
    nq = seq // tq
    assert past % KEY_TILE == 0 and tq == KEY_TILE
    idx_bits = int(math.ceil(math.log2(lk))) + 1
    kern = functools.partial(_dsa_kernel, bsz=bsz, tq=tq, past=past, n_keys=n_keys, topk=topk, nkt=nkt, nd=nd,
                             idx_bits=idx_bits)
    row_state = pltpu.VMEM((SUBLANES, tq), F32)
    grid_spec = pltpu.PrefetchScalarGridSpec(
        num_scalar_prefetch=1,
        grid=(bsz, nq),
        in_specs=[
            pl.BlockSpec((None, N_HEADS, tq, LANES), lambda b, i, s: (b, 0, i, 0)),
            pl.BlockSpec((None, IDX_HEADS, tq, LANES), lambda b, i, s: (b, 0, i, 0)),
            pl.BlockSpec((None, IDX_HEADS, tq), lambda b, i, s: (b, 0, i)),
            pl.BlockSpec((None, N_KV_HEADS, lk, LANES), lambda b, i, s: (b, 0, 0, 0)),
            pl.BlockSpec((None, N_KV_HEADS, nkt, LANES, KEY_BLOCK), lambda b, i, s: (b, 0, 0, 0, 0)),
            pl.BlockSpec((None, lk, LANES), lambda b, i, s: (b, 0, 0)),
            pl.BlockSpec(bias_tiles.shape, lambda b, i, s: (0, 0, 0, 0)),
        ],
        out_specs=pl.BlockSpec((None, tq, N_HEADS * LANES), lambda b, i, s: (b, i, 0)),
        scratch_shapes=[
            pltpu.VMEM((nkt, KEY_BLOCK, tq), F32),
            row_state, row_state, row_state, row_state, row_state, row_state,
            pltpu.VMEM((N_HEADS * tq, LANES), BF16),
            pltpu.VMEM((SUBLANES, N_HEADS * tq), F32),
            pltpu.VMEM((N_KV_HEADS, LANES, KV_REP * tq), F32),
        ],
    )
    return pl.pallas_call(
        kern,
        grid_spec=grid_spec,
        out_shape=jax.ShapeDtypeStruct((bsz, seq, N_HEADS * LANES), BF16),
        compiler_params=_cp(("arbitrary", "arbitrary")),
        name="dsa",
    )(stats, q, qi, wt, k_all, vt_all, ki_all, bias_tiles)


def _merge_kernel(x_ref, ya_ref, at_ref, sga_ref, sgb_ref, wup_ref, wout_ref, g2_ref, wr_hi_ref, wr_lo_ref, br_ref,
                  x1_ref, h2_ref, gt_ref, rt_ref, cnt_ref, run_ref, *, tm, sub):
    step = pl.program_id(0)

    @pl.when(step % sub == 0)
    def _():
        run_ref[...] = jnp.zeros(run_ref.shape, F32)

    yb = _dot(at_ref[...], wup_ref[...])
    merged = sga_ref[...].astype(F32) * ya_ref[...].astype(F32) + sgb_ref[...].astype(F32) * yb
    x1 = x_ref[...] + _dot(merged.astype(BF16), wout_ref[...])
    x1_ref[...] = x1
    ms = jnp.mean(x1 * x1, axis=-1, keepdims=True)
    h2 = x1 * lax.rsqrt(ms + EPS) * g2_ref[...]
    h2_hi, h2_lo = _split(h2)
    h2_ref[...] = h2_hi

    wr_hi = wr_hi_ref[...]
    logit = (_dot_nt(wr_hi, h2_hi) + _dot_nt(wr_hi, h2_lo) + _dot_nt(wr_lo_ref[...], h2_hi)) + br_ref[:, 0:1]
    ne = logit.shape[0]
    eid = lax.broadcasted_iota(I32, (ne, tm), 0).astype(F32)
    selb = jnp.zeros((ne, tm), F32)
    tops = []
    picks = []
    for _ in range(TOP_K):
        mx = jnp.max(logit, axis=0, keepdims=True)
        pick = jnp.min(jnp.where(logit == mx, eid, float(ne)), axis=0, keepdims=True)
        hit = eid == pick
        selb = jnp.where(hit, 1.0, selb)
        logit = jnp.where(hit, -jnp.inf, logit)
        tops.append(mx)
        picks.append(hit)
    ex = [jnp.exp(t - tops[0]) for t in tops]
    den = ex[0] + ex[1] + ex[2] + ex[3]
    gate = jnp.zeros((ne, tm), F32)
    for hit, e in zip(picks, ex):
        gate = jnp.where(hit, e / den, gate)
    gt_ref[...] = gate

    sel = selb > 0.5
    selb = selb.astype(BF16)
    r_i = lax.broadcasted_iota(I32, (tm, tm), 0)
    c_i = lax.broadcasted_iota(I32, (tm, tm), 1)
    tri = jnp.where(r_i < c_i, 1.0, 0.0).astype(BF16)
    run = run_ref[...]
    rank = _dot(selb, tri) + jnp.broadcast_to(run[:, 0:1], (ne, tm))
    rt_ref[...] = jnp.where(sel, rank, -1.0)
    run = run + _dot(selb, jnp.ones((tm, LANES), BF16))
    run_ref[...] = run
    cnt_ref[...] = run


def _merge(x2, ya_tb, attn, sga, sgb, mw, bsz, seq, tm, moe_tile):
    t, d = x2.shape
    if seq % tm == 0:
        nt = seq // tm
        ya, ya_spec = ya_tb, pl.BlockSpec((tm, d), lambda i: (i % nt, i // nt))
    else:
        ya = ya_tb.reshape(seq, bsz, d).transpose(1, 0, 2).reshape(t, d)
        ya_spec = pl.BlockSpec((tm, d), lambda i: (i, 0))
    sub = moe_tile // tm
    ne = mw['wr_hi'].shape[0]

    def tok(i):
        return (i, 0)

    def cst(i):
        return (0, 0)

    consts = [mw['wup'], mw['wout'], mw['g2'], mw['wr_hi'], mw['wr_lo'], mw['br']]
    return pl.pallas_call(
        functools.partial(_merge_kernel, tm=tm, sub=sub),
        grid=(t // tm,),
        in_specs=[
            pl.BlockSpec((tm, d), tok),
            ya_spec,
            pl.BlockSpec((tm, attn.shape[-1]), tok),
            pl.BlockSpec((tm, d), tok),
            pl.BlockSpec((tm, d), tok),
        ] + [pl.BlockSpec(a.shape, cst) for a in consts],
        out_specs=(
            pl.BlockSpec((tm, d), tok),
            pl.BlockSpec((tm, d), tok),
            pl.BlockSpec((ne, tm), lambda i: (0, i)),
            pl.BlockSpec((ne, tm), lambda i: (0, i)),
            pl.BlockSpec((None, ne, LANES), lambda i: (i // sub, 0, 0)),
        ),
        out_shape=(
            jax.ShapeDtypeStruct((t, d), F32),
            jax.ShapeDtypeStruct((t, d), BF16),
            jax.ShapeDtypeStruct((ne, t), F32),
            jax.ShapeDtypeStruct((ne, t), F32),
            jax.ShapeDtypeStruct((t // moe_tile, ne, LANES), F32),
        ),
        scratch_shapes=[pltpu.VMEM((ne, LANES), F32)],
        compiler_params=_cp(("arbitrary",)),
        name="merge",
    )(x2, ya, attn, sga, sgb, *consts)


def _moe_kernel(cnt_ref, h2_ref, x1_hbm, gt_ref, rt_ref, wg_ref, wu_ref, wd_ref, bg_ref, bu_ref, bd_ref, y_ref,
                pg_ref, og_ref, sem, *, tt, pair):
    j = pl.program_id(0)
    e = pl.program_id(1)
    s = pl.program_id(2)
    ne = pl.num_programs(1)
    rb = MOE_ROWS
    slot = e % MOE_GROUP
    row0 = pl.multiple_of(s * tt, tt)

    @pl.when((e == 0) & (s == 0))
    def _():
        cp = pltpu.make_async_copy(x1_hbm.at[pl.ds(pl.multiple_of(j * pair * tt, tt), pair * tt), :], y_ref, sem)
        cp.start()
        cp.wait()

    n_rows = cnt_ref[(j * pair + s) * ne + e]
    n_blk = (n_rows + rb - 1) // rb
    mine = lax.broadcasted_iota(I32, (SUBLANES, tt), 0) == e % SUBLANES
    g_row = jnp.sum(jnp.where(mine, gt_ref[...], 0.0), axis=0, keepdims=True)
    r_row = jnp.sum(jnp.where(mine, rt_ref[...], 0.0), axis=0, keepdims=True)
    rid = lax.broadcasted_iota(I32, (rb, tt), 0).astype(F32)

    def expert_rows(blk):
        hit = jnp.broadcast_to(r_row, (rb, tt)) == (rid + (blk * rb).astype(F32))
        p = jnp.where(hit, 1.0, 0.0).astype(BF16)
        xg = _dot(p, h2_ref[pl.ds(row0, tt), :]).astype(BF16)
        a = jnp.minimum(_dot(xg, wg_ref[0]) + bg_ref[0], SWIGLU_LIMIT)
        b = jnp.clip(_dot(xg, wu_ref[0]) + bu_ref[0], -SWIGLU_LIMIT, SWIGLU_LIMIT)
        hid = a * jax.nn.sigmoid(SWIGLU_ALPHA * a) * (b + 1.0)
        o = _dot(hid.astype(BF16), wd_ref[0]) + bd_ref[0]
        g_col = jnp.sum(jnp.where(hit, jnp.broadcast_to(g_row, (rb, tt)), 0.0), axis=1, keepdims=True)
        return p, (o * g_col).astype(BF16)

    p, og = expert_rows(jnp.int32(0))
    r0 = pl.multiple_of(slot * rb, rb)
    pg_ref[s, pl.ds(r0, rb), :] = p
    og_ref[s, pl.ds(r0, rb), :] = og

    @pl.when(slot == MOE_GROUP - 1)
    def _():
        y_ref[pl.ds(row0, tt), :] += _dot_tn(pg_ref[s], og_ref[s])

    def overflow(blk, c):
        p, og = expert_rows(blk)
        y_ref[pl.ds(row0, tt), :] += _dot_tn(p, og)
        return c

    lax.fori_loop(1, n_blk, overflow, 0)


def _moe(h2, x1, gt, rt, cnt, ew, tt):
    t, d = h2.shape
    ne = gt.shape[0]
    nt = t // tt
    f = ew['wg'].shape[-1]
    pair = MOE_PAIR if nt % MOE_PAIR == 0 else 1
    grid_spec = pltpu.PrefetchScalarGridSpec(
        num_scalar_prefetch=1,
        grid=(nt // pair, ne, pair),
        in_specs=[
            pl.BlockSpec((pair * tt, d), lambda j, e, s, c: (j, 0)),
            pl.BlockSpec(memory_space=pl.ANY),
            pl.BlockSpec((SUBLANES, tt), lambda j, e, s, c: (e // SUBLANES, j * pair + s)),
            pl.BlockSpec((SUBLANES, tt), lambda j, e, s, c: (e // SUBLANES, j * pair + s)),
            pl.BlockSpec((1, d, f), lambda j, e, s, c: (e, 0, 0)),
            pl.BlockSpec((1, d, f), lambda j, e, s, c: (e, 0, 0)),
            pl.BlockSpec((1, f, d), lambda j, e, s, c: (e, 0, 0)),
            pl.BlockSpec((1, 1, f), lambda j, e, s, c: (e, 0, 0)),
            pl.BlockSpec((1, 1, f), lambda j, e, s, c: (e, 0, 0)),
            pl.BlockSpec((1, 1, d), lambda j, e, s, c: (e, 0, 0)),
        ],
        out_specs=pl.BlockSpec((pair * tt, d), lambda j, e, s, c: (j, 0)),
        scratch_shapes=[pltpu.VMEM((pair, MOE_GROUP * MOE_ROWS, tt), BF16),
                        pltpu.VMEM((pair, MOE_GROUP * MOE_ROWS, d), BF16),
                        pltpu.SemaphoreType.DMA(())],
    )
    assert ne % MOE_GROUP == 0
    return pl.pallas_call(
        functools.partial(_moe_kernel, tt=tt, pair=pair),
        grid_spec=grid_spec,
        out_shape=jax.ShapeDtypeStruct((t, d), F32),
        compiler_params=_cp(("arbitrary", "arbitrary", "arbitrary")),
        name="moe",
    )(cnt, h2, x1, gt, rt, ew['wg'], ew['wu'], ew['wd'], ew['bg'], ew['bu'], ew['bd'])


def _pad_heads(wmat, n_heads, width):
    d = wmat.shape[0]
    w3 = wmat.reshape(d, n_heads, width)
    return jnp.pad(w3, ((0, 0), (0, 0), (0, LANES - width))).reshape(d, n_heads * LANES)


def _pad_lanes(v, width=LANES):
    v = v.reshape(1, -1)
    return jnp.pad(v, ((0, 0), (0, width - v.shape[1])))


def _rel_bucket(rel):
    half = REL_BUCKETS // 2
    max_exact = half // 2
    n = jnp.abs(rel)
    large = max_exact + (jnp.log(jnp.maximum(n, 1).astype(jnp.float32) / max_exact)
                         / math.log(REL_MAX_DIST / max_exact) * (half - max_exact)).astype(jnp.int32)
    large = jnp.minimum(large, half - 1)
    return jnp.where(rel > 0, half, 0) + jnp.where(n < max_exact, n, large)


def _bias_tiles(rel_bias):
    tk = KEY_TILE
    half = REL_BUCKETS // 2
    max_exact = half // 2
    n_sat = int(math.ceil(max_exact * (REL_MAX_DIST / max_exact) ** ((half - 1 - max_exact) / (half - max_exact)))) + 2
    nd = (n_sat + 2 * tk - 2) // tk + 1
    dd = jnp.arange(nd, dtype=I32)[:, None, None]
    c = jnp.arange(tk, dtype=I32)[None, :, None]
    r = jnp.arange(tk, dtype=I32)[None, None, :]
    bucket = _rel_bucket(c - r - dd * tk)
    onehot = (bucket[..., None] == jnp.arange(REL_BUCKETS, dtype=I32)).astype(F32)
    tiles = jnp.einsum('dcrb,bh->dhcr', onehot, rel_bias.astype(F32) * LOG2E,
                       precision=lax.Precision.HIGHEST)
    return tiles


def _prep_proj(norm1_g, w_in, q_norm_g, k_norm_g, idx_k_norm_g, idx_k_norm_b, d_model):
    ssm_w = d_model // 2
    attn_w = N_HEADS * HEAD_DIM
    kv = N_KV_HEADS * HEAD_DIM
    sizes = [ssm_w, attn_w, kv, kv, IDX_HEADS * IDX_DIM, IDX_DIM, IDX_HEADS, d_model, d_model]
    pts = np.cumsum(sizes)[:-1].tolist()
    wu, wq, wk, wv, wqi, wki, wwi, wga, wgb = jnp.split(w_in, pts, axis=1)
    bf = lambda a: a.astype(BF16)
    blk = np.kron(np.eye(N_KV_HEADS), np.ones((HEAD_DIM, HEAD_DIM))) / HEAD_DIM
    wwit = jnp.pad(wwi.T, ((0, 2 * SUBLANES - IDX_HEADS), (0, 0)))
    return dict(
        g1=norm1_g.reshape(1, -1).astype(F32),
        wu=bf(wu), wq=bf(_pad_heads(wq, N_HEADS, HEAD_DIM)), wk=bf(_pad_heads(wk, N_KV_HEADS, HEAD_DIM)),
        wv=bf(_pad_heads(wv, N_KV_HEADS, HEAD_DIM)), wqi=bf(_pad_heads(wqi, IDX_HEADS, IDX_DIM)),
        wki=bf(_pad_heads(wki, 1, IDX_DIM)), wwit=bf(wwit),
        wga=bf(wga), wgb=bf(wgb), wkc=bf(wk), wvc=bf(wv),
        gq=_pad_lanes(q_norm_g.astype(F32)), gk=_pad_lanes(k_norm_g.astype(F32)),
        gkc=jnp.tile(k_norm_g.astype(F32), N_KV_HEADS).reshape(1, -1),
        gi=_pad_lanes(idx_k_norm_g.astype(F32)), bi=_pad_lanes(idx_k_norm_b.astype(F32)),
        ones_h=jnp.full((LANES, LANES), 1.0 / HEAD_DIM, BF16),
        ones_c=jnp.asarray(blk, BF16),
    )


def _prep_s5(lre, lim, log_dt, b_re, b_im, c_re, c_im, dvec, wa, wb):
    g, p = lre.shape
    ch = b_re.shape[-1]
    lam = lax.complex(lre.astype(F32), lim.astype(F32))
    dt = jnp.exp(log_dt.astype(F32))[:, None]
    a_bar = jnp.exp(lam * dt)
    b_bar = ((a_bar - 1.0) / lam)[:, :, None] * lax.complex(b_re.astype(F32), b_im.astype(F32))
    eye = jnp.eye(g, dtype=F32)
    b_r = jnp.einsum('gpc,gh->gchp', jnp.real(b_bar), eye).reshape(g * ch, g * p)
    b_i = jnp.einsum('gpc,gh->gchp', jnp.imag(b_bar), eye).reshape(g * ch, g * p)
    c_r = jnp.einsum('gcp,gh->gphc', c_re.astype(F32), eye).reshape(g * p, g * ch)
    c_i = jnp.einsum('gcp,gh->gphc', c_im.astype(F32), eye).reshape(g * p, g * ch)
    return dict(
        bmat=jnp.concatenate([b_r, b_i], axis=1).astype(BF16),
        cmat=jnp.concatenate([c_r, -c_i], axis=0).astype(BF16),
        a_re=jnp.real(a_bar).reshape(1, g * p), a_im=jnp.imag(a_bar).reshape(1, g * p),
        d=dvec.reshape(1, -1).astype(F32), wa=wa.astype(BF16), wb=wb.astype(BF16),
    )


def _prep_merge(w_attn_up, w_out, norm2_g, w_router, b_router):
    d = w_attn_up.shape[1]
    wup = jnp.pad(w_attn_up.reshape(N_HEADS, HEAD_DIM, d), ((0, 0), (0, LANES - HEAD_DIM), (0, 0)))
    wr_t = w_router.astype(F32).T
    wr_hi = wr_t.astype(BF16)
    wr_lo = (wr_t - wr_hi.astype(F32)).astype(BF16)
    return dict(
        wup=wup.reshape(N_HEADS * LANES, d).astype(BF16), wout=w_out.astype(BF16),
        g2=norm2_g.reshape(1, -1).astype(F32), wr_hi=wr_hi, wr_lo=wr_lo,
        br=jnp.broadcast_to(b_router.astype(F32)[:, None], (b_router.shape[0], LANES)),
    )


def _prep_moe(wg, bg, wu, bu, wd, bd):
    return dict(wg=wg.astype(BF16), wu=wu.astype(BF16), wd=wd.astype(BF16),
                bg=bg.astype(F32)[:, None, :], bu=bu.astype(F32)[:, None, :], bd=bd.astype(F32)[:, None, :])


def _pick_tile(n, pref):
    t = min(n, pref)
    while n % t:
        t //= 2
    return t


def _pad_axis(a, axis, size):
    pad = [(0, 0)] * a.ndim
    pad[axis] = (0, size - a.shape[axis])
    return jnp.pad(a, pad)


def _trunk_layer(x, past_k, past_v, past_ik, h0_re, h0_im, pw, sw, mw, ew, bias_tiles):
    bsz, seq, d = x.shape
    t = bsz * seq
    tm = _pick_tile(seq, 512)
    u_tb, q, kp, vp, qi, kip, wt, sga, sgb, kc, vc, kic = _proj(x, pw, bsz, seq, tm)

    half = sw['a_re'].shape[1]
    if h0_re is None:
        h0 = jnp.zeros((bsz, 2 * half), F32)
    else:
        h0 = jnp.concatenate([h0_re.reshape(bsz, half), h0_im.reshape(bsz, half)], axis=1).astype(F32)
    tc = _pick_tile(seq, max(1, 512 // bsz))
    ya_tb, hout = _s5(u_tb, h0, sw, bsz, seq, tc)
    groups = half // SSM_STATE
    s_re = hout[:, :half].reshape(bsz, groups, SSM_STATE)
    s_im = hout[:, half:].reshape(bsz, groups, SSM_STATE)

    past = 0 if past_k is None else past_k.shape[1]
    n_keys = past + seq
    lk = -(-n_keys // KEY_BLOCK) * KEY_BLOCK
    kip3 = kip.reshape(bsz, seq, LANES)
    if past:
        lane = jnp.arange(LANES)
        pk = jnp.pad(past_k.astype(F32), ((0, 0), (0, 0), (0, 0), (0, LANES - HEAD_DIM)))
        pk = jnp.where(lane == HEAD_DIM, 1.0, pk).astype(BF16)
        pv = jnp.pad(past_v.astype(F32), ((0, 0), (0, 0), (0, 0), (0, LANES - HEAD_DIM)))
        pv = jnp.where(lane == HEAD_DIM, 1.0, pv).astype(BF16)
        pik = jnp.pad(past_ik.astype(F32), ((0, 0), (0, 0), (0, LANES - IDX_DIM))).astype(BF16)
        k_all = jnp.concatenate([pk.transpose(0, 2, 1, 3), kp], axis=2)
        v_all = jnp.concatenate([pv.transpose(0, 2, 1, 3), vp], axis=2)
        ki_all = jnp.concatenate([pik, kip3], axis=1)
    else:
        k_all, v_all, ki_all = kp, vp, kip3
    k_all = _pad_axis(k_all, 2, lk)
    v_all = _pad_axis(v_all, 2, lk)
    ki_all = _pad_axis(ki_all, 1, lk)
    vt_all = v_all.reshape(bsz, N_KV_HEADS, lk // KEY_BLOCK, KEY_BLOCK, LANES).transpose(0, 1, 2, 4, 3)
    tq = KEY_TILE
    seq_q = -(-seq // tq) * tq
    q_p, qi_p, wt_p = _pad_axis(q, 2, seq_q), _pad_axis(qi, 2, seq_q), _pad_axis(wt, 2, seq_q)
    kf = k_all[..., :HEAD_DIM].astype(F32)
    kmax = jnp.sqrt(jnp.max(jnp.sum(kf * kf, axis=-1), axis=-1)).reshape(-1)
    bfar = bias_tiles[-1, :, 0, 0]
    bmax = jnp.max(jnp.abs(bias_tiles), axis=(0, 2, 3))
    stats = jnp.concatenate([kmax, bmax, bfar]).astype(F32)
    bias_tiles = bias_tiles - bfar[None, :, None, None]
    attn = _dsa(stats, q_p, qi_p, wt_p, k_all, vt_all, ki_all, bias_tiles, bsz, seq_q, past, n_keys, tq)
    attn = attn[:, :seq]

    moe_tile = _pick_tile(t, MOE_TILE)
    tm2 = _pick_tile(moe_tile, 512)
    x1, h2, gt, rt, cnt = _merge(x.reshape(t, d), ya_tb, attn.reshape(t, attn.shape[-1]), sga, sgb, mw,
                                 bsz, seq, tm2, moe_tile)
    cnt_i = cnt[:, :, 0].astype(I32).reshape(-1)
    y = _moe(h2, x1, gt, rt, cnt_i, ew, moe_tile)

    k_new = kc.reshape(bsz, seq, N_KV_HEADS, HEAD_DIM)
    v_new = vc.reshape(bsz, seq, N_KV_HEADS, HEAD_DIM)
    ik_new = kic.reshape(bsz, seq, IDX_DIM)
    return y.reshape(bsz, seq, d), k_new, v_new, ik_new, s_re, s_im


def kernel(x_prompt, x_sample, cache_k, cache_v, cache_idx_k, state_ssm_re, state_ssm_im, rel_bias, norm1_g, w_in, ssm_lambda_re, ssm_lambda_im, ssm_log_dt, ssm_b_re, ssm_b_im, ssm_c_re, ssm_c_im, ssm_d, ssm_w_glu_a, ssm_w_glu_b, q_norm_g, k_norm_g, idx_k_norm_g, idx_k_norm_b, w_attn_up, w_out, norm2_g, moe_w_router, moe_b_router, moe_w_gate, moe_b_gate, moe_w_up, moe_b_up, moe_w_down, moe_b_down):
    depth = w_in.shape[0]
    d_model = x_prompt.shape[-1]
    bias_tiles = _bias_tiles(rel_bias)
    xp, xs = x_prompt, x_sample
    st_p, st_s = [], []
    for l in range(depth):
        pw = _prep_proj(norm1_g[l], w_in[l], q_norm_g[l], k_norm_g[l], idx_k_norm_g[l], idx_k_norm_b[l], d_model)
        sw = _prep_s5(ssm_lambda_re[l], ssm_lambda_im[l], ssm_log_dt[l], ssm_b_re[l], ssm_b_im[l], ssm_c_re[l],
                      ssm_c_im[l], ssm_d[l], ssm_w_glu_a[l], ssm_w_glu_b[l])
        mw = _prep_merge(w_attn_up[l], w_out[l], norm2_g[l], moe_w_router[l], moe_b_router[l])
        ew = _prep_moe(moe_w_gate[l], moe_b_gate[l], moe_w_up[l], moe_b_up[l], moe_w_down[l], moe_b_down[l])
        xp, *sp = _trunk_layer(xp, None, None, None, None, None, pw, sw, mw, ew, bias_tiles)
        xs, *ss = _trunk_layer(xs, cache_k[l], cache_v[l], cache_idx_k[l], state_ssm_re[l], state_ssm_im[l],
                               pw, sw, mw, ew, bias_tiles)
        st_p.append(sp)
        st_s.append(ss)
    outs_p = [jnp.stack([s[i] for s in st_p]) for i in range(5)]
    outs_s = [jnp.stack([s[i] for s in st_s]) for i in range(5)]
    return (xp, xs, *outs_p, *outs_s)
```

```python
import functools
import math

import numpy as np
import jax
import jax.numpy as jnp
from jax import lax
from jax.experimental import pallas as pl
from jax.experimental.pallas import tpu as pltpu

F32 = jnp.float32
BF16 = jnp.bfloat16
I32 = jnp.int32

LANES = 128
SUBLANES = 8
VMEM_LIMIT = 56 * 1024 * 1024

CHUNK = 64
SSM_GROUP_CH = 16
SSM_STATE = 64
N_HEADS = 8
HEAD_DIM = 64
N_KV_HEADS = 2
KV_REP = N_HEADS // N_KV_HEADS
IDX_HEADS = 8
IDX_DIM = 64
TOPK_MAX = 256
REL_BUCKETS = 32
REL_MAX_DIST = 1024
N_EXPERTS = 32
TOP_K = 4
SWIGLU_LIMIT = 7.0
SWIGLU_ALPHA = 1.702
EPS = 1e-6

KEY_TILE = 128
KEY_BLOCK = 256
LOG2E = math.log2(math.e)
NEG_BIG = -1e30
SHIFT_LIMIT = 30.0
SEARCH_FIRST = 8
SEARCH_GROUP = 4
MOE_TILE = 1024
MOE_ROWS = 160
MOE_GROUP = 8
MOE_PAIR = 2


def _cp(sem):
    return pltpu.CompilerParams(dimension_semantics=sem, vmem_limit_bytes=VMEM_LIMIT)


def _dot(a, b):
    return jnp.dot(a, b, preferred_element_type=F32)


def _dot_nt(a, b):
    return lax.dot_general(a, b, (((1,), (1,)), ((), ())), preferred_element_type=F32)


def _dot_tn(a, b):
    return lax.dot_general(a, b, (((0,), (0,)), ((), ())), preferred_element_type=F32)


def _split(a):
    hi = a.astype(BF16)
    lo = (a - hi.astype(F32)).astype(BF16)
    return hi, lo


def _dot_split(a, g):
    hi, lo = _split(a)
    return _dot(hi, g) + _dot(lo, g)


def _proj_kernel(x_ref, g1_ref, wu_ref, wq_ref, wk_ref, wv_ref, wqi_ref, wki_ref, wwit_ref, wga_ref, wgb_ref,
                 wkc_ref, wvc_ref, gq_ref, gk_ref, gkc_ref, gi_ref, bi_ref, ones_h_ref, ones_c_ref,
                 u_ref, q_ref, kp_ref, vp_ref, qi_ref, kip_ref, wt_ref, sga_ref, sgb_ref,
                 kc_ref, vc_ref, kic_ref):
    x = x_ref[...]
    ms = jnp.mean(x * x, axis=-1, keepdims=True)
    hn = (x * lax.rsqrt(ms + EPS) * g1_ref[...]).astype(BF16)
    ones_h = ones_h_ref[...]
    lane = lax.broadcasted_iota(I32, (x.shape[0], LANES), 1)

    u_ref[...] = _dot(hn, wu_ref[...]).astype(BF16)

    q = _dot(hn, wq_ref[...])
    scale = HEAD_DIM ** -0.5 * LOG2E
    for h in range(N_HEADS):
        qh = q[:, h * LANES:(h + 1) * LANES]
        msq = _dot_split(qh * qh, ones_h)
        q_ref[h] = (qh * lax.rsqrt(msq + EPS) * (gq_ref[...] * scale)).astype(BF16)

    k = _dot(hn, wk_ref[...])
    for g in range(N_KV_HEADS):
        kg = k[:, g * LANES:(g + 1) * LANES]
        msk = _dot_split(kg * kg, ones_h)
        kn = kg * lax.rsqrt(msk + EPS) * gk_ref[...]
        kp_ref[g] = jnp.where(lane == HEAD_DIM, 1.0, kn).astype(BF16)

    v = _dot(hn, wv_ref[...])
    for g in range(N_KV_HEADS):
        vg = v[:, g * LANES:(g + 1) * LANES]
        vp_ref[g] = jnp.where(lane == HEAD_DIM, 1.0, vg).astype(BF16)

    qi = _dot(hn, wqi_ref[...])
    for h in range(IDX_HEADS):
        qi_ref[h] = qi[:, h * LANES:(h + 1) * LANES].astype(BF16)

    ki = _dot(hn, wki_ref[...])
    mu = _dot_split(ki, ones_h)
    xc = jnp.where(lane < IDX_DIM, ki - mu, 0.0)
    var = _dot_split(xc * xc, ones_h)
    kin = xc * lax.rsqrt(var + EPS) * gi_ref[...] + bi_ref[...]
    kip_ref[...] = kin.astype(BF16)
    kic_ref[...] = kin[:, :IDX_DIM]

    wt = _dot_nt(wwit_ref[...], hn)
    wt_ref[...] = wt[0:IDX_HEADS, :] * (IDX_HEADS ** -0.5 * IDX_DIM ** -0.5)

    sga_ref[...] = jax.nn.sigmoid(_dot(hn, wga_ref[...])).astype(BF16)
    sgb_ref[...] = jax.nn.sigmoid(_dot(hn, wgb_ref[...])).astype(BF16)

    kc = _dot(hn, wkc_ref[...])
    mskc = _dot_split(kc * kc, ones_c_ref[...])
    kc_ref[...] = kc * lax.rsqrt(mskc + EPS) * gkc_ref[...]
    vc_ref[...] = _dot(hn, wvc_ref[...])


def _proj(x, pw, bsz, seq, tm):
    d = x.shape[-1]
    nt = seq // tm
    t = bsz * seq
    x2 = x.reshape(t, d)

    def tok(b, i):
        return (b * nt + i, 0)

    def cst(b, i):
        return (0, 0)

    def wspec(a):
        return pl.BlockSpec(a.shape, cst)

    weights = [pw['g1'], pw['wu'], pw['wq'], pw['wk'], pw['wv'], pw['wqi'], pw['wki'], pw['wwit'], pw['wga'],
               pw['wgb'], pw['wkc'], pw['wvc'], pw['gq'], pw['gk'], pw['gkc'], pw['gi'], pw['bi'],
               pw['ones_h'], pw['ones_c']]
    ssm_w = pw['wu'].shape[1]
    out_shape = (
        jax.ShapeDtypeStruct((seq, bsz * ssm_w), BF16),
        jax.ShapeDtypeStruct((bsz, N_HEADS, seq, LANES), BF16),
        jax.ShapeDtypeStruct((bsz, N_KV_HEADS, seq, LANES), BF16),
        jax.ShapeDtypeStruct((bsz, N_KV_HEADS, seq, LANES), BF16),
        jax.ShapeDtypeStruct((bsz, IDX_HEADS, seq, LANES), BF16),
        jax.ShapeDtypeStruct((t, LANES), BF16),
        jax.ShapeDtypeStruct((bsz, IDX_HEADS, seq), F32),
        jax.ShapeDtypeStruct((t, d), BF16),
        jax.ShapeDtypeStruct((t, d), BF16),
        jax.ShapeDtypeStruct((t, N_KV_HEADS * HEAD_DIM), F32),
        jax.ShapeDtypeStruct((t, N_KV_HEADS * HEAD_DIM), F32),
        jax.ShapeDtypeStruct((t, IDX_DIM), F32),
    )

    def hm(nh):
        return pl.BlockSpec((None, nh, tm, LANES), lambda b, i: (b, 0, i, 0))

    out_specs = (
        pl.BlockSpec((tm, ssm_w), lambda b, i: (i, b)),
        hm(N_HEADS), hm(N_KV_HEADS), hm(N_KV_HEADS), hm(IDX_HEADS),
        pl.BlockSpec((tm, LANES), tok),
        pl.BlockSpec((None, IDX_HEADS, tm), lambda b, i: (b, 0, i)),
        pl.BlockSpec((tm, d), tok), pl.BlockSpec((tm, d), tok),
        pl.BlockSpec((tm, N_KV_HEADS * HEAD_DIM), tok), pl.BlockSpec((tm, N_KV_HEADS * HEAD_DIM), tok),
        pl.BlockSpec((tm, IDX_DIM), tok),
    )
    return pl.pallas_call(
        _proj_kernel,
        grid=(bsz, nt),
        in_specs=[pl.BlockSpec((tm, d), tok)] + [wspec(a) for a in weights],
        out_specs=out_specs,
        out_shape=out_shape,
        compiler_params=_cp(("arbitrary", "arbitrary")),
        name="proj",
    )(x2, *weights)


def _gelu_tanh(x):
    return 0.5 * x * (1.0 + jnp.tanh(math.sqrt(2.0 / math.pi) * (x + 0.044715 * (x * x * x))))


def _s5_kernel(u_ref, h0_ref, bmat_ref, are_ref, aim_ref, cmat_ref, dvec_ref, wa_ref, wb_ref,
               ya_ref, hout_ref, state_ref, bu_ref, *, bsz, tc, strip):
    s = pl.program_id(0)
    half = are_ref.shape[1]

    @pl.when(s == 0)
    def _():
        state_ref[...] = h0_ref[...]

    u = u_ref[...]
    bu_ref[...] = _dot(u, bmat_ref[...])

    for c0 in range(0, half, strip):
        ar = jnp.broadcast_to(are_ref[:, c0:c0 + strip], (bsz, strip))
        ai = jnp.broadcast_to(aim_ref[:, c0:c0 + strip], (bsz, strip))
        hr0 = state_ref[:, c0:c0 + strip]
        hi0 = state_ref[:, half + c0:half + c0 + strip]

        def step(t, carry):
            hr, hi = carry
            r0 = pl.multiple_of(t * bsz, bsz)
            br = bu_ref[pl.ds(r0, bsz), c0:c0 + strip]
            bi = bu_ref[pl.ds(r0, bsz), half + c0:half + c0 + strip]
            nr = ar * hr - ai * hi + br
            ni = ar * hi + ai * hr + bi
            bu_ref[pl.ds(r0, bsz), c0:c0 + strip] = nr
            bu_ref[pl.ds(r0, bsz), half + c0:half + c0 + strip] = ni
            return nr, ni

        hr, hi = lax.fori_loop(0, tc, step, (hr0, hi0))
        state_ref[:, c0:c0 + strip] = hr
        state_ref[:, half + c0:half + c0 + strip] = hi

    y = _dot(bu_ref[...].astype(BF16), cmat_ref[...]) + dvec_ref[...] * u.astype(F32)
    g = _gelu_tanh(y).astype(BF16)
    ya = _dot(g, wa_ref[...]) * jax.nn.sigmoid(_dot(g, wb_ref[...]))
    ya_ref[...] = ya.astype(BF16)

    @pl.when(s == pl.num_programs(0) - 1)
    def _():
        hout_ref[...] = state_ref[...]


def _s5(u_tb, h0, sw, bsz, seq, tc):
    rows = tc * bsz
    ssm_w = sw['bmat'].shape[0]
    two_half = sw['bmat'].shape[1]
    half = two_half // 2
    d = sw['wa'].shape[1]
    u2 = u_tb.reshape(seq * bsz, ssm_w)
    strip = min(512, half)

    def cst(s):
        return (0, 0)

    consts = [h0, sw['bmat'], sw['a_re'], sw['a_im'], sw['cmat'], sw['d'], sw['wa'], sw['wb']]
    ya, hout = pl.pallas_call(
        functools.partial(_s5_kernel, bsz=bsz, tc=tc, strip=strip),
        grid=(seq // tc,),
        in_specs=[pl.BlockSpec((rows, ssm_w), lambda s: (s, 0))] + [pl.BlockSpec(a.shape, cst) for a in consts],
        out_specs=(pl.BlockSpec((rows, d), lambda s: (s, 0)), pl.BlockSpec((bsz, two_half), cst)),
        out_shape=(jax.ShapeDtypeStruct((seq * bsz, d), BF16), jax.ShapeDtypeStruct((bsz, two_half), F32)),
        scratch_shapes=[pltpu.VMEM((bsz, two_half), F32), pltpu.VMEM((rows, two_half), F32)],
        compiler_params=_cp(("arbitrary",)),
        name="s5",
    )(u2, *consts)
    return ya.reshape(seq, bsz * d), hout


def _f2key(x):
    b = lax.bitcast_convert_type(x, I32)
    return b ^ ((b >> 31) & 0x7FFFFFFF)


def _key2f(k):
    return lax.bitcast_convert_type(k ^ ((k >> 31) & 0x7FFFFFFF), F32)


def _dsa_kernel(st_ref, q_ref, qi_ref, wt_ref, k_ref, vt_ref, ki_ref, bias_ref, o_ref,
                s_ref, lo_ref, hi_ref, clo_ref, glo_ref, ghi_ref, side_ref, q2_ref, mrow_ref, acc_ref,
                *, bsz, tq, past, n_keys, topk, nkt, nd, idx_bits):
    b_id = pl.program_id(0)
    i = pl.program_id(1)
    kb = KEY_BLOCK
    sl = SUBLANES
    q0 = past + i * tq
    last_chunk = (q0 + tq - 1) // CHUNK
    n_kt = jnp.minimum(nkt, ((last_chunk + 1) * CHUNK + kb - 1) // kb)
    d0 = q0 // KEY_TILE

    krow = lax.broadcasted_iota(I32, (kb, tq), 0)
    q_chunk = (q0 + lax.broadcasted_iota(I32, (kb, tq), 1)) // CHUNK
    qc8 = (q0 + lax.broadcasted_iota(I32, (sl, tq), 1)) // CHUNK
    n_adm = jnp.minimum((qc8 + 1) * CHUNK, n_keys)
    n_admf = n_adm.astype(F32)
    needf = jnp.minimum(topk, n_adm).astype(F32)

    def bcast(x):
        return jnp.broadcast_to(x[0:1, :], (kb, tq))

    def rep(x):
        return jnp.broadcast_to(x, (sl, tq))

    qi = qi_ref[...].reshape(IDX_HEADS * tq, LANES)

    def score_block(kt, masked):
        k0 = pl.multiple_of(kt * kb, kb)
        s = _dot_nt(ki_ref[pl.ds(k0, kb), :], qi)
        sc = jnp.zeros((kb, tq), F32)
        for h in range(IDX_HEADS):
            sc = sc + wt_ref[h:h + 1, :] * jnp.maximum(s[:, h * tq:(h + 1) * tq], 0.0)
        if masked:
            kpos = k0 + krow
            adm = ((kpos // CHUNK) <= q_chunk) & (kpos < n_keys)
            sc = jnp.where(adm, sc, -jnp.inf)
        s_ref[kt] = sc

    def score_pair(j, c):
        score_block(2 * j, False)
        score_block(2 * j + 1, False)
        return c

    lax.fori_loop(0, (n_kt - 1) // 2, score_pair, 0)

    @pl.when((n_kt - 1) % 2 == 1)
    def _():
        score_block(n_kt - 2, False)

    score_block(n_kt - 1, True)

    part = 4 * sl

    def fold(x, op):
        x = x.reshape(kb // part, part, tq)
        acc = x[0]
        for j in range(1, kb // part):
            acc = op(acc, x[j])
        return acc

    def count(pred):
        def body(kt, c):
            return c + fold(jnp.where(pred(s_ref[kt], kt), 1.0, 0.0), jnp.add)
        c = lax.fori_loop(0, n_kt, body, jnp.zeros((part, tq), F32))
        return rep(jnp.sum(c, axis=0, keepdims=True))

    def minmax(kt, c):
        mx, mn = c
        s = s_ref[kt]
        return (jnp.maximum(mx, fold(s, jnp.maximum)),
                jnp.minimum(mn, fold(jnp.where(s == -jnp.inf, jnp.inf, s), jnp.minimum)))

    mx, mn = lax.fori_loop(0, n_kt, minmax,
                           (jnp.full((part, tq), -jnp.inf, F32), jnp.full((part, tq), jnp.inf, F32)))
    lo_ref[...] = rep(jnp.min(mn, axis=0, keepdims=True))
    hi_ref[...] = _key2f(_f2key(rep(jnp.max(mx, axis=0, keepdims=True))) + 1)
    def odds(cnt):
        c = jnp.clip(cnt, 0.5, n_admf - 0.5)
        return jnp.log((n_admf - c) / c)

    target = odds(needf - 0.5)
    clo_ref[...] = n_admf
    glo_ref[...] = target - odds(n_admf)
    ghi_ref[...] = target - odds(jnp.zeros((sl, tq), F32))
    side_ref[...] = jnp.zeros((sl, tq), F32)

    def searching(lo, hi, clo):
        return (_f2key(hi) > _f2key(lo) + 1) & (clo > needf)

    def refine(it, c):
        lo, hi, clo = lo_ref[...], hi_ref[...], clo_ref[...]
        glo, ghi, side = glo_ref[...], ghi_ref[...], side_ref[...]
        k_t = _f2key(lo + (hi - lo) * (glo / (glo - ghi)))
        t = _key2f(jnp.minimum(jnp.maximum(k_t, _f2key(lo) + 1), _f2key(hi) - 1))
        tb = bcast(t)
        cnt = count(lambda s, kt: s >= tb)
        g = target - odds(cnt)
        open_ = searching(lo, hi, clo)
        up = open_ & (cnt >= needf)
        dn = open_ & (cnt < needf)
        lo_ref[...] = jnp.where(up, t, lo)
        clo_ref[...] = jnp.where(up, cnt, clo)
        hi_ref[...] = jnp.where(dn, t, hi)
        glo_ref[...] = jnp.where(up, g, jnp.where(dn & (side < 0.0), glo * 0.5, glo))
        ghi_ref[...] = jnp.where(dn, g, jnp.where(up & (side > 0.0), ghi * 0.5, ghi))
        side_ref[...] = jnp.where(up, 1.0, jnp.where(dn, -1.0, side))
        return c

    def snap():
        lo, hi, clo = lo_ref[...], hi_ref[...], clo_ref[...]
        lo_b, hi_b = bcast(lo), bcast(hi)

        def body(kt, c):
            a, b = c
            s = s_ref[kt]
            return (jnp.minimum(a, fold(jnp.where(s >= lo_b, s, jnp.inf), jnp.minimum)),
                    jnp.maximum(b, fold(jnp.where(s < hi_b, s, -jnp.inf), jnp.maximum)))

        a, b = lax.fori_loop(0, n_kt, body,
                             (jnp.full((part, tq), jnp.inf, F32), jnp.full((part, tq), -jnp.inf, F32)))
        open_ = searching(lo, hi, clo)
        lo_ref[...] = jnp.where(open_, rep(jnp.min(a, axis=0, keepdims=True)), lo)
        hi_ref[...] = jnp.where(open_, _key2f(_f2key(rep(jnp.max(b, axis=0, keepdims=True))) + 1), hi)

    def n_searching():
        return jnp.max(jnp.where(searching(lo_ref[...], hi_ref[...], clo_ref[...]), 1.0, 0.0))

    def group(c):
        grp, _ = c
        lax.fori_loop(0, jnp.where(grp == 0, SEARCH_FIRST, SEARCH_GROUP), refine, 0)
        snap()
        return grp + 1, n_searching()

    lax.while_loop(lambda c: c[1] > 0.0, group, (jnp.int32(0), n_searching()))
    thr = lo_ref[...]
    thr_b = bcast(thr)

    n_tied = jnp.max(jnp.where(clo_ref[...] > needf, 1.0, 0.0))

    @pl.when(n_tied > 0.0)
    def _():
        rem = needf - count(lambda s, kt: s > thr_b)
        lo_ref[...] = jnp.zeros((sl, tq), F32)
        hi_ref[...] = jnp.full((sl, tq), float(nkt * kb), F32)
        krowf = krow.astype(F32)

        def bisect_idx(it, c):
            lo = lo_ref[...]
            hi = hi_ref[...]
            mid = jnp.floor((lo + hi) * 0.5)
            mid_b = bcast(mid)
            ok = count(lambda s, kt: (s == thr_b) & ((kt * kb).astype(F32) + krowf < mid_b)) >= rem
            hi_ref[...] = jnp.where(ok, mid, hi)
            lo_ref[...] = jnp.where(ok, lo, mid)
            return c

        lax.fori_loop(0, idx_bits, bisect_idx, 0)
        cut_b = bcast(hi_ref[...])

        def drop(kt, c):
            s = s_ref[kt]
            s_ref[kt] = jnp.where((s == thr_b) & ((kt * kb).astype(F32) + krowf >= cut_b), -jnp.inf, s)
            return c

        lax.fori_loop(0, n_kt, drop, 0)

    rows_g = KV_REP * tq
    qf = q_ref[...].reshape(N_HEADS * tq, LANES).astype(F32)
    qn = jnp.sqrt(jnp.sum(qf * qf, axis=1, keepdims=True))
    lane = lax.broadcasted_iota(I32, (tq, LANES), 1)
    worst = jnp.float32(0.0)
    for h in range(N_HEADS):
        kmax = st_ref[b_id * N_KV_HEADS + h // KV_REP]
        bmax = st_ref[bsz * N_KV_HEADS + h]
        bfar = st_ref[bsz * N_KV_HEADS + N_HEADS + h]
        bound = qn[h * tq:(h + 1) * tq, :] * (kmax * 1.01) + (bmax + 0.1)
        worst = jnp.maximum(worst, jnp.max(bound))
        q2_ref[h * tq:(h + 1) * tq, :] = jnp.where(lane == HEAD_DIM, bfar - bound,
                                                   qf[h * tq:(h + 1) * tq, :]).astype(BF16)
    n_far = jnp.clip((d0 - nd) // 2 + 1, 0, n_kt)

    def logits(kt, g, near, exact):
        k0 = pl.multiple_of(kt * kb, kb)
        maskadd = jnp.where(s_ref[kt] >= thr_b, 0.0, NEG_BIG)
        s = _dot_nt(k_ref[g, pl.ds(k0, kb), :], q2_ref[g * rows_g:(g + 1) * rows_g, :])
        if near:
            da = jnp.clip(d0 - 2 * kt, 0, nd - 1)
            db = jnp.clip(d0 - 2 * kt - 1, 0, nd - 1)
        parts = []
        for r in range(KV_REP):
            h = g * KV_REP + r
            add = maskadd - mrow_ref[0:1, h * tq:(h + 1) * tq] if exact else maskadd
            if near:
                add = jnp.concatenate([bias_ref[da, h], bias_ref[db, h]], axis=0) + add
            parts.append(s[:, r * tq:(r + 1) * tq] + add)
        return jnp.concatenate(parts, axis=1)

    def over_blocks(fn):
        def far_pair(j, c):
            fn((2 * j, 2 * j + 1), False)
            return c
        lax.fori_loop(0, n_far // 2, far_pair, 0)

        @pl.when(n_far % 2 == 1)
        def _():
            fn((n_far - 1,), False)

        n_near = n_kt - n_far

        def near_pair(j, c):
            fn((n_far + 2 * j, n_far + 2 * j + 1), True)
            return c
        lax.fori_loop(0, n_near // 2, near_pair, 0)

        @pl.when(n_near % 2 == 1)
        def _():
            fn((n_kt - 1,), True)

    def attend(exact):
        acc_ref[...] = jnp.zeros(acc_ref.shape, F32)

        def blocks(kts, near):
            for g in range(N_KV_HEADS):
                pv = None
                for kt in kts:
                    p = jnp.exp2(logits(kt, g, near, exact)).astype(BF16)
                    d = _dot(vt_ref[g, kt], p)
                    pv = d if pv is None else pv + d
                acc_ref[g] += pv
        over_blocks(blocks)

    @pl.when(worst <= SHIFT_LIMIT)
    def _():
        attend(False)

    @pl.when(worst > SHIFT_LIMIT)
    def _():
        mrow_ref[...] = jnp.full(mrow_ref.shape, NEG_BIG, F32)

        def blocks(kts, near):
            for g in range(N_KV_HEADS):
                for kt in kts:
                    mx = jnp.max(logits(kt, g, near, False), axis=0, keepdims=True)
                    cur = mrow_ref[:, g * rows_g:(g + 1) * rows_g]
                    mrow_ref[:, g * rows_g:(g + 1) * rows_g] = jnp.maximum(cur, jnp.broadcast_to(mx, (sl, rows_g)))
        over_blocks(blocks)
        attend(True)

    for g in range(N_KV_HEADS):
        acc = acc_ref[g]
        og = acc / acc[HEAD_DIM:HEAD_DIM + 1, :]
        for r in range(KV_REP):
            h = g * KV_REP + r
            o_ref[:, h * LANES:(h + 1) * LANES] = og[:, r * tq:(r + 1) * tq].T.astype(BF16)


def _dsa(stats, q, qi, wt, k_all, vt_all, ki_all, bias_tiles, bsz, seq, past, n_keys, tq):
    lk = k_all.shape[2]
    nkt = lk // KEY_BLOCK
    topk = min(TOPK_MAX, n_keys // 4)
    nd = bias_tiles.shape[0]
    nq = seq // tq
    assert past % KEY_TILE == 0 and tq == KEY_TILE
    idx_bits = int(math.ceil(math.log2(lk))) + 1
    kern = functools.partial(_dsa_kernel, bsz=bsz, tq=tq, past=past, n_keys=n_keys, topk=topk, nkt=nkt, nd=nd,
                             idx_bits=idx_bits)
    row_state = pltpu.VMEM((SUBLANES, tq), F32)
    grid_spec = pltpu.PrefetchScalarGridSpec(
        num_scalar_prefetch=1,
        grid=(bsz, nq),
        in_specs=[
            pl.BlockSpec((None, N_HEADS, tq, LANES), lambda b, i, s: (b, 0, i, 0)),
            pl.BlockSpec((None, IDX_HEADS, tq, LANES), lambda b, i, s: (b, 0, i, 0)),
            pl.BlockSpec((None, IDX_HEADS, tq), lambda b, i, s: (b, 0, i)),
            pl.BlockSpec((None, N_KV_HEADS, lk, LANES), lambda b, i, s: (b, 0, 0, 0)),
            pl.BlockSpec((None, N_KV_HEADS, nkt, LANES, KEY_BLOCK), lambda b, i, s: (b, 0, 0, 0, 0)),
            pl.BlockSpec((None, lk, LANES), lambda b, i, s: (b, 0, 0)),
            pl.BlockSpec(bias_tiles.shape, lambda b, i, s: (0, 0, 0, 0)),
        ],
        out_specs=pl.BlockSpec((None, tq, N_HEADS * LANES), lambda b, i, s: (b, i, 0)),
        scratch_shapes=[
            pltpu.VMEM((nkt, KEY_BLOCK, tq), F32),
            row_state, row_state, row_state, row_state, row_state, row_state,
            pltpu.VMEM((N_HEADS * tq, LANES), BF16),
            pltpu.VMEM((SUBLANES, N_HEADS * tq), F32),
            pltpu.VMEM((N_KV_HEADS, LANES, KV_REP * tq), F32),
        ],
    )
    return pl.pallas_call(
        kern,
        grid_spec=grid_spec,
        out_shape=jax.ShapeDtypeStruct((bsz, seq, N_HEADS * LANES), BF16),
        compiler_params=_cp(("arbitrary", "arbitrary")),
        name="dsa",
    )(stats, q, qi, wt, k_all, vt_all, ki_all, bias_tiles)


def _merge_kernel(x_ref, ya_ref, at_ref, sga_ref, sgb_ref, wup_ref, wout_ref, g2_ref, wr_hi_ref, wr_lo_ref, br_ref,
                  x1_ref, h2_ref, gt_ref, rt_ref, cnt_ref, run_ref, *, tm, sub):
    step = pl.program_id(0)

    @pl.when(step % sub == 0)
    def _():
        run_ref[...] = jnp.zeros(run_ref.shape, F32)

    yb = _dot(at_ref[...], wup_ref[...])
    merged = sga_ref[...].astype(F32) * ya_ref[...].astype(F32) + sgb_ref[...].astype(F32) * yb
    x1 = x_ref[...] + _dot(merged.astype(BF16), wout_ref[...])
    x1_ref[...] = x1
    ms = jnp.mean(x1 * x1, axis=-1, keepdims=True)
    h2 = x1 * lax.rsqrt(ms + EPS) * g2_ref[...]
    h2_hi, h2_lo = _split(h2)
    h2_ref[...] = h2_hi

    wr_hi = wr_hi_ref[...]
    logit = (_dot_nt(wr_hi, h2_hi) + _dot_nt(wr_hi, h2_lo) + _dot_nt(wr_lo_ref[...], h2_hi)) + br_ref[:, 0:1]
    ne = logit.shape[0]
    eid = lax.broadcasted_iota(I32, (ne, tm), 0).astype(F32)
    selb = jnp.zeros((ne, tm), F32)
    tops = []
    picks = []
    for _ in range(TOP_K):
        mx = jnp.max(logit, axis=0, keepdims=True)
        pick = jnp.min(jnp.where(logit == mx, eid, float(ne)), axis=0, keepdims=True)
        hit = eid == pick
        selb = jnp.where(hit, 1.0, selb)
        logit = jnp.where(hit, -jnp.inf, logit)
        tops.append(mx)
        picks.append(hit)
    ex = [jnp.exp(t - tops[0]) for t in tops]
    den = ex[0] + ex[1] + ex[2] + ex[3]
    gate = jnp.zeros((ne, tm), F32)
    for hit, e in zip(picks, ex):
        gate = jnp.where(hit, e / den, gate)
    gt_ref[...] = gate

    sel = selb > 0.5
    selb = selb.astype(BF16)
    r_i = lax.broadcasted_iota(I32, (tm, tm), 0)
    c_i = lax.broadcasted_iota(I32, (tm, tm), 1)
    tri = jnp.where(r_i < c_i, 1.0, 0.0).astype(BF16)
    run = run_ref[...]
    rank = _dot(selb, tri) + jnp.broadcast_to(run[:, 0:1], (ne, tm))
    rt_ref[...] = jnp.where(sel, rank, -1.0)
    run = run + _dot(selb, jnp.ones((tm, LANES), BF16))
    run_ref[...] = run
    cnt_ref[...] = run


def _merge(x2, ya_tb, attn, sga, sgb, mw, bsz, seq, tm, moe_tile):
    t, d = x2.shape
    if seq % tm == 0:
        nt = seq // tm
        ya, ya_spec = ya_tb, pl.BlockSpec((tm, d), lambda i: (i % nt, i // nt))
    else:
        ya = ya_tb.reshape(seq, bsz, d).transpose(1, 0, 2).reshape(t, d)
        ya_spec = pl.BlockSpec((tm, d), lambda i: (i, 0))
    sub = moe_tile // tm
    ne = mw['wr_hi'].shape[0]

    def tok(i):
        return (i, 0)

    def cst(i):
        return (0, 0)

    consts = [mw['wup'], mw['wout'], mw['g2'], mw['wr_hi'], mw['wr_lo'], mw['br']]
    return pl.pallas_call(
        functools.partial(_merge_kernel, tm=tm, sub=sub),
        grid=(t // tm,),
        in_specs=[
            pl.BlockSpec((tm, d), tok),
            ya_spec,
            pl.BlockSpec((tm, attn.shape[-1]), tok),
            pl.BlockSpec((tm, d), tok),
            pl.BlockSpec((tm, d), tok),
        ] + [pl.BlockSpec(a.shape, cst) for a in consts],
        out_specs=(
            pl.BlockSpec((tm, d), tok),
            pl.BlockSpec((tm, d), tok),
            pl.BlockSpec((ne, tm), lambda i: (0, i)),
            pl.BlockSpec((ne, tm), lambda i: (0, i)),
            pl.BlockSpec((None, ne, LANES), lambda i: (i // sub, 0, 0)),
        ),
        out_shape=(
            jax.ShapeDtypeStruct((t, d), F32),
            jax.ShapeDtypeStruct((t, d), BF16),
            jax.ShapeDtypeStruct((ne, t), F32),
            jax.ShapeDtypeStruct((ne, t), F32),
            jax.ShapeDtypeStruct((t // moe_tile, ne, LANES), F32),
        ),
        scratch_shapes=[pltpu.VMEM((ne, LANES), F32)],
        compiler_params=_cp(("arbitrary",)),
        name="merge",
    )(x2, ya, attn, sga, sgb, *consts)


def _moe_kernel(cnt_ref, h2_ref, x1_hbm, gt_ref, rt_ref, wg_ref, wu_ref, wd_ref, bg_ref, bu_ref, bd_ref, y_ref,
                pg_ref, og_ref, sem, *, tt, pair):
    j = pl.program_id(0)
    e = pl.program_id(1)
    ne = pl.num_programs(1)
    rb = MOE_ROWS
    slot = e % MOE_GROUP

    @pl.when(e == 0)
    def _():
        cp = pltpu.make_async_copy(x1_hbm.at[pl.ds(pl.multiple_of(j * pair * tt, tt), pair * tt), :], y_ref, sem)
        cp.start()
        cp.wait()

    mine = lax.broadcasted_iota(I32, (SUBLANES, pair * tt), 0) == e % SUBLANES
    g_all = jnp.sum(jnp.where(mine, gt_ref[...], 0.0), axis=0, keepdims=True)
    r_all = jnp.sum(jnp.where(mine, rt_ref[...], 0.0), axis=0, keepdims=True)
    rid = lax.broadcasted_iota(I32, (rb, tt), 0).astype(F32)

    def one_hot(s, blk):
        return jnp.broadcast_to(r_all[:, s * tt:(s + 1) * tt], (rb, tt)) == (rid + (blk * rb).astype(F32))

    def gather(s, hit):
        p = jnp.where(hit, 1.0, 0.0).astype(BF16)
        return p, _dot(p, h2_ref[s * tt:(s + 1) * tt, :]).astype(BF16)

    def expert(xg):
        a = jnp.minimum(_dot(xg, wg_ref[0]) + bg_ref[0], SWIGLU_LIMIT)
        b = jnp.clip(_dot(xg, wu_ref[0]) + bu_ref[0], -SWIGLU_LIMIT, SWIGLU_LIMIT)
        hid = a * jax.nn.sigmoid(SWIGLU_ALPHA * a) * (b + 1.0)
        return _dot(hid.astype(BF16), wd_ref[0]) + bd_ref[0]

    def gated(s, hit, o):
        g_row = jnp.broadcast_to(g_all[:, s * tt:(s + 1) * tt], (rb, tt))
        return (o * jnp.sum(jnp.where(hit, g_row, 0.0), axis=1, keepdims=True)).astype(BF16)

    hits = [one_hot(s, jnp.int32(0)) for s in range(pair)]
    gathered = [gather(s, hits[s]) for s in range(pair)]
    o = expert(jnp.concatenate([xg for _, xg in gathered], axis=0))
    r0 = pl.multiple_of(slot * rb, rb)
    for s in range(pair):
        pg_ref[s, pl.ds(r0, rb), :] = gathered[s][0]
        og_ref[s, pl.ds(r0, rb), :] = gated(s, hits[s], o[s * rb:(s + 1) * rb])

    @pl.when(slot == MOE_GROUP - 1)
    def _():
        for s in range(pair):
            y_ref[s * tt:(s + 1) * tt, :] += _dot_tn(pg_ref[s], og_ref[s])

    for s in range(pair):
        def overflow(blk, c, s=s):
            hit = one_hot(s, blk)
            p, xg = gather(s, hit)
            y_ref[s * tt:(s + 1) * tt, :] += _dot_tn(p, gated(s, hit, expert(xg)))
            return c

        n_rows = cnt_ref[(j * pair + s) * ne + e]
        lax.fori_loop(1, (n_rows + rb - 1) // rb, overflow, 0)


def _moe(h2, x1, gt, rt, cnt, ew, tt):
    t, d = h2.shape
    ne = gt.shape[0]
    nt = t // tt
    f = ew['wg'].shape[-1]
    pair = MOE_PAIR if nt % MOE_PAIR == 0 else 1
    grid_spec = pltpu.PrefetchScalarGridSpec(
        num_scalar_prefetch=1,
        grid=(nt // pair, ne),
        in_specs=[
            pl.BlockSpec((pair * tt, d), lambda j, e, c: (j, 0)),
            pl.BlockSpec(memory_space=pl.ANY),
            pl.BlockSpec((SUBLANES, pair * tt), lambda j, e, c: (e // SUBLANES, j)),
            pl.BlockSpec((SUBLANES, pair * tt), lambda j, e, c: (e // SUBLANES, j)),
            pl.BlockSpec((1, d, f), lambda j, e, c: (e, 0, 0)),
            pl.BlockSpec((1, d, f), lambda j, e, c: (e, 0, 0)),
            pl.BlockSpec((1, f, d), lambda j, e, c: (e, 0, 0)),
            pl.BlockSpec((1, 1, f), lambda j, e, c: (e, 0, 0)),
            pl.BlockSpec((1, 1, f), lambda j, e, c: (e, 0, 0)),
            pl.BlockSpec((1, 1, d), lambda j, e, c: (e, 0, 0)),
        ],
        out_specs=pl.BlockSpec((pair * tt, d), lambda j, e, c: (j, 0)),
        scratch_shapes=[pltpu.VMEM((pair, MOE_GROUP * MOE_ROWS, tt), BF16),
                        pltpu.VMEM((pair, MOE_GROUP * MOE_ROWS, d), BF16),
                        pltpu.SemaphoreType.DMA(())],
    )
    assert ne % MOE_GROUP == 0
    return pl.pallas_call(
        functools.partial(_moe_kernel, tt=tt, pair=pair),
        grid_spec=grid_spec,
        out_shape=jax.ShapeDtypeStruct((t, d), F32),
        compiler_params=_cp(("arbitrary", "arbitrary")),
        name="moe",
    )(cnt, h2, x1, gt, rt, ew['wg'], ew['wu'], ew['wd'], ew['bg'], ew['bu'], ew['bd'])


def _pad_heads(wmat, n_heads, width):
    d = wmat.shape[0]
    w3 = wmat.reshape(d, n_heads, width)
    return jnp.pad(w3, ((0, 0), (0, 0), (0, LANES - width))).reshape(d, n_heads * LANES)


def _pad_lanes(v, width=LANES):
    v = v.reshape(1, -1)
    return jnp.pad(v, ((0, 0), (0, width - v.shape[1])))


def _rel_bucket(rel):
    half = REL_BUCKETS // 2
    max_exact = half // 2
    n = jnp.abs(rel)
    large = max_exact + (jnp.log(jnp.maximum(n, 1).astype(jnp.float32) / max_exact)
                         / math.log(REL_MAX_DIST / max_exact) * (half - max_exact)).astype(jnp.int32)
    large = jnp.minimum(large, half - 1)
    return jnp.where(rel > 0, half, 0) + jnp.where(n < max_exact, n, large)


def _bias_tiles(rel_bias):
    tk = KEY_TILE
    half = REL_BUCKETS // 2
    max_exact = half // 2
    n_sat = int(math.ceil(max_exact * (REL_MAX_DIST / max_exact) ** ((half - 1 - max_exact) / (half - max_exact)))) + 2
    nd = (n_sat + 2 * tk - 2) // tk + 1
    dd = jnp.arange(nd, dtype=I32)[:, None, None]
    c = jnp.arange(tk, dtype=I32)[None, :, None]
    r = jnp.arange(tk, dtype=I32)[None, None, :]
    bucket = _rel_bucket(c - r - dd * tk)
    onehot = (bucket[..., None] == jnp.arange(REL_BUCKETS, dtype=I32)).astype(F32)
    tiles = jnp.einsum('dcrb,bh->dhcr', onehot, rel_bias.astype(F32) * LOG2E,
                       precision=lax.Precision.HIGHEST)
    return tiles


def _prep_proj(norm1_g, w_in, q_norm_g, k_norm_g, idx_k_norm_g, idx_k_norm_b, d_model):
    ssm_w = d_model // 2
    attn_w = N_HEADS * HEAD_DIM
    kv = N_KV_HEADS * HEAD_DIM
    sizes = [ssm_w, attn_w, kv, kv, IDX_HEADS * IDX_DIM, IDX_DIM, IDX_HEADS, d_model, d_model]
    pts = np.cumsum(sizes)[:-1].tolist()
    wu, wq, wk, wv, wqi, wki, wwi, wga, wgb = jnp.split(w_in, pts, axis=1)
    bf = lambda a: a.astype(BF16)
    blk = np.kron(np.eye(N_KV_HEADS), np.ones((HEAD_DIM, HEAD_DIM))) / HEAD_DIM
    wwit = jnp.pad(wwi.T, ((0, 2 * SUBLANES - IDX_HEADS), (0, 0)))
    return dict(
        g1=norm1_g.reshape(1, -1).astype(F32),
        wu=bf(wu), wq=bf(_pad_heads(wq, N_HEADS, HEAD_DIM)), wk=bf(_pad_heads(wk, N_KV_HEADS, HEAD_DIM)),
        wv=bf(_pad_heads(wv, N_KV_HEADS, HEAD_DIM)), wqi=bf(_pad_heads(wqi, IDX_HEADS, IDX_DIM)),
        wki=bf(_pad_heads(wki, 1, IDX_DIM)), wwit=bf(wwit),
        wga=bf(wga), wgb=bf(wgb), wkc=bf(wk), wvc=bf(wv),
        gq=_pad_lanes(q_norm_g.astype(F32)), gk=_pad_lanes(k_norm_g.astype(F32)),
        gkc=jnp.tile(k_norm_g.astype(F32), N_KV_HEADS).reshape(1, -1),
        gi=_pad_lanes(idx_k_norm_g.astype(F32)), bi=_pad_lanes(idx_k_norm_b.astype(F32)),
        ones_h=jnp.full((LANES, LANES), 1.0 / HEAD_DIM, BF16),
        ones_c=jnp.asarray(blk, BF16),
    )


def _prep_s5(lre, lim, log_dt, b_re, b_im, c_re, c_im, dvec, wa, wb):
    g, p = lre.shape
    ch = b_re.shape[-1]
    lam = lax.complex(lre.astype(F32), lim.astype(F32))
    dt = jnp.exp(log_dt.astype(F32))[:, None]
    a_bar = jnp.exp(lam * dt)
    b_bar = ((a_bar - 1.0) / lam)[:, :, None] * lax.complex(b_re.astype(F32), b_im.astype(F32))
    eye = jnp.eye(g, dtype=F32)
    b_r = jnp.einsum('gpc,gh->gchp', jnp.real(b_bar), eye).reshape(g * ch, g * p)
    b_i = jnp.einsum('gpc,gh->gchp', jnp.imag(b_bar), eye).reshape(g * ch, g * p)
    c_r = jnp.einsum('gcp,gh->gphc', c_re.astype(F32), eye).reshape(g * p, g * ch)
    c_i = jnp.einsum('gcp,gh->gphc', c_im.astype(F32), eye).reshape(g * p, g * ch)
    return dict(
        bmat=jnp.concatenate([b_r, b_i], axis=1).astype(BF16),
        cmat=jnp.concatenate([c_r, -c_i], axis=0).astype(BF16),
        a_re=jnp.real(a_bar).reshape(1, g * p), a_im=jnp.imag(a_bar).reshape(1, g * p),
        d=dvec.reshape(1, -1).astype(F32), wa=wa.astype(BF16), wb=wb.astype(BF16),
    )


def _prep_merge(w_attn_up, w_out, norm2_g, w_router, b_router):
    d = w_attn_up.shape[1]
    wup = jnp.pad(w_attn_up.reshape(N_HEADS, HEAD_DIM, d), ((0, 0), (0, LANES - HEAD_DIM), (0, 0)))
    wr_t = w_router.astype(F32).T
    wr_hi = wr_t.astype(BF16)
    wr_lo = (wr_t - wr_hi.astype(F32)).astype(BF16)
    return dict(
        wup=wup.reshape(N_HEADS * LANES, d).astype(BF16), wout=w_out.astype(BF16),
        g2=norm2_g.reshape(1, -1).astype(F32), wr_hi=wr_hi, wr_lo=wr_lo,
        br=jnp.broadcast_to(b_router.astype(F32)[:, None], (b_router.shape[0], LANES)),
    )


def _prep_moe(wg, bg, wu, bu, wd, bd):
    return dict(wg=wg.astype(BF16), wu=wu.astype(BF16), wd=wd.astype(BF16),
                bg=bg.astype(F32)[:, None, :], bu=bu.astype(F32)[:, None, :], bd=bd.astype(F32)[:, None, :])


def _pick_tile(n, pref):
    t = min(n, pref)
    while n % t:
        t //= 2
    return t


def _pad_axis(a, axis, size):
    pad = [(0, 0)] * a.ndim
    pad[axis] = (0, size - a.shape[axis])
    return jnp.pad(a, pad)


def _trunk_layer(x, past_k, past_v, past_ik, h0_re, h0_im, pw, sw, mw, ew, bias_tiles):
    bsz, seq, d = x.shape
    t = bsz * seq
    tm = _pick_tile(seq, 512)
    u_tb, q, kp, vp, qi, kip, wt, sga, sgb, kc, vc, kic = _proj(x, pw, bsz, seq, tm)

    half = sw['a_re'].shape[1]
    if h0_re is None:
        h0 = jnp.zeros((bsz, 2 * half), F32)
    else:
        h0 = jnp.concatenate([h0_re.reshape(bsz, half), h0_im.reshape(bsz, half)], axis=1).astype(F32)
    tc = _pick_tile(seq, max(1, 512 // bsz))
    ya_tb, hout = _s5(u_tb, h0, sw, bsz, seq, tc)
    groups = half // SSM_STATE
    s_re = hout[:, :half].reshape(bsz, groups, SSM_STATE)
    s_im = hout[:, half:].reshape(bsz, groups, SSM_STATE)

    past = 0 if past_k is None else past_k.shape[1]
    n_keys = past + seq
    lk = -(-n_keys // KEY_BLOCK) * KEY_BLOCK
    kip3 = kip.reshape(bsz, seq, LANES)
    if past:
        lane = jnp.arange(LANES)
        pk = jnp.pad(past_k.astype(F32), ((0, 0), (0, 0), (0, 0), (0, LANES - HEAD_DIM)))
        pk = jnp.where(lane == HEAD_DIM, 1.0, pk).astype(BF16)
        pv = jnp.pad(past_v.astype(F32), ((0, 0), (0, 0), (0, 0), (0, LANES - HEAD_DIM)))
        pv = jnp.where(lane == HEAD_DIM, 1.0, pv).astype(BF16)
        pik = jnp.pad(past_ik.astype(F32), ((0, 0), (0, 0), (0, LANES - IDX_DIM))).astype(BF16)
        k_all = jnp.concatenate([pk.transpose(0, 2, 1, 3), kp], axis=2)
        v_all = jnp.concatenate([pv.transpose(0, 2, 1, 3), vp], axis=2)
        ki_all = jnp.concatenate([pik, kip3], axis=1)
    else:
        k_all, v_all, ki_all = kp, vp, kip3
    k_all = _pad_axis(k_all, 2, lk)
    v_all = _pad_axis(v_all, 2, lk)
    ki_all = _pad_axis(ki_all, 1, lk)
    vt_all = v_all.reshape(bsz, N_KV_HEADS, lk // KEY_BLOCK, KEY_BLOCK, LANES).transpose(0, 1, 2, 4, 3)
    tq = KEY_TILE
    seq_q = -(-seq // tq) * tq
    q_p, qi_p, wt_p = _pad_axis(q, 2, seq_q), _pad_axis(qi, 2, seq_q), _pad_axis(wt, 2, seq_q)
    kf = k_all[..., :HEAD_DIM].astype(F32)
    kmax = jnp.sqrt(jnp.max(jnp.sum(kf * kf, axis=-1), axis=-1)).reshape(-1)
    bfar = bias_tiles[-1, :, 0, 0]
    bmax = jnp.max(jnp.abs(bias_tiles), axis=(0, 2, 3))
    stats = jnp.concatenate([kmax, bmax, bfar]).astype(F32)
    bias_tiles = bias_tiles - bfar[None, :, None, None]
    attn = _dsa(stats, q_p, qi_p, wt_p, k_all, vt_all, ki_all, bias_tiles, bsz, seq_q, past, n_keys, tq)
    attn = attn[:, :seq]

    moe_tile = _pick_tile(t, MOE_TILE)
    tm2 = _pick_tile(moe_tile, 512)
    x1, h2, gt, rt, cnt = _merge(x.reshape(t, d), ya_tb, attn.reshape(t, attn.shape[-1]), sga, sgb, mw,
                                 bsz, seq, tm2, moe_tile)
    cnt_i = cnt[:, :, 0].astype(I32).reshape(-1)
    y = _moe(h2, x1, gt, rt, cnt_i, ew, moe_tile)

    k_new = kc.reshape(bsz, seq, N_KV_HEADS, HEAD_DIM)
    v_new = vc.reshape(bsz, seq, N_KV_HEADS, HEAD_DIM)
    ik_new = kic.reshape(bsz, seq, IDX_DIM)
    return y.reshape(bsz, seq, d), k_new, v_new, ik_new, s_re, s_im


def kernel(x_prompt, x_sample, cache_k, cache_v, cache_idx_k, state_ssm_re, state_ssm_im, rel_bias, norm1_g, w_in, ssm_lambda_re, ssm_lambda_im, ssm_log_dt, ssm_b_re, ssm_b_im, ssm_c_re, ssm_c_im, ssm_d, ssm_w_glu_a, ssm_w_glu_b, q_norm_g, k_norm_g, idx_k_norm_g, idx_k_norm_b, w_attn_up, w_out, norm2_g, moe_w_router, moe_b_router, moe_w_gate, moe_b_gate, moe_w_up, moe_b_up, moe_w_down, moe_b_down):
    depth = w_in.shape[0]
    d_model = x_prompt.shape[-1]
    bias_tiles = _bias_tiles(rel_bias)
    xp, xs = x_prompt, x_sample
    st_p, st_s = [], []
    for l in range(depth):
        pw = _prep_proj(norm1_g[l], w_in[l], q_norm_g[l], k_norm_g[l], idx_k_norm_g[l], idx_k_norm_b[l], d_model)
        sw = _prep_s5(ssm_lambda_re[l], ssm_lambda_im[l], ssm_log_dt[l], ssm_b_re[l], ssm_b_im[l], ssm_c_re[l],
                      ssm_c_im[l], ssm_d[l], ssm_w_glu_a[l], ssm_w_glu_b[l])
        mw = _prep_merge(w_attn_up[l], w_out[l], norm2_g[l], moe_w_router[l], moe_b_router[l])
        ew = _prep_moe(moe_w_gate[l], moe_b_gate[l], moe_w_up[l], moe_b_up[l], moe_w_down[l], moe_b_down[l])
        xp, *sp = _trunk_layer(xp, None, None, None, None, None, pw, sw, mw, ew, bias_tiles)
        xs, *ss = _trunk_layer(xs, cache_k[l], cache_v[l], cache_idx_k[l], state_ssm_re[l], state_ssm_im[l],
                               pw, sw, mw, ew, bias_tiles)
        st_p.append(sp)
        st_s.append(ss)
    outs_p = [jnp.stack([s[i] for s in st_p]) for i in range(5)]
    outs_s = [jnp.stack([s[i] for s in st_s]) for i in range(5)]
    return (xp, xs, *outs_p, *outs_s)
```

```python
import functools
import math

import numpy as np
import jax
import jax.numpy as jnp
from jax import lax
from jax.experimental import pallas as pl
from jax.experimental.pallas import tpu as pltpu

F32 = jnp.float32
BF16 = jnp.bfloat16
I32 = jnp.int32

LANES = 128
SUBLANES = 8
VMEM_LIMIT = 56 * 1024 * 1024

CHUNK = 64
SSM_GROUP_CH = 16
SSM_STATE = 64
N_HEADS = 8
HEAD_DIM = 64
N_KV_HEADS = 2
KV_REP = N_HEADS // N_KV_HEADS
IDX_HEADS = 8
IDX_DIM = 64
TOPK_MAX = 256
REL_BUCKETS = 32
REL_MAX_DIST = 1024
N_EXPERTS = 32
TOP_K = 4
SWIGLU_LIMIT = 7.0
SWIGLU_ALPHA = 1.702
EPS = 1e-6

KEY_TILE = 128
KEY_BLOCK = 256
LOG2E = math.log2(math.e)
NEG_BIG = -1e30
SHIFT_LIMIT = 30.0
S5_DIAG = 2
SEARCH_FIRST = 8
SEARCH_GROUP = 4
MOE_TILE = 1024
MOE_ROWS = 160
MOE_GROUP = 8
MOE_PAIR = 2


def _cp(sem):
    return pltpu.CompilerParams(dimension_semantics=sem, vmem_limit_bytes=VMEM_LIMIT)


def _dot(a, b):
    return jnp.dot(a, b, preferred_element_type=F32)


def _dot_nt(a, b):
    return lax.dot_general(a, b, (((1,), (1,)), ((), ())), preferred_element_type=F32)


def _dot_tn(a, b):
    return lax.dot_general(a, b, (((0,), (0,)), ((), ())), preferred_element_type=F32)


def _split(a):
    hi = a.astype(BF16)
    lo = (a - hi.astype(F32)).astype(BF16)
    return hi, lo


def _dot_split(a, g):
    hi, lo = _split(a)
    return _dot(hi, g) + _dot(lo, g)


def _proj_kernel(x_ref, g1_ref, wu_ref, wq_ref, wk_ref, wv_ref, wqi_ref, wki_ref, wwit_ref, wga_ref, wgb_ref,
                 wkc_ref, wvc_ref, gq_ref, gk_ref, gkc_ref, gi_ref, bi_ref, ones_h_ref, ones_c_ref,
                 u_ref, q_ref, kp_ref, vp_ref, qi_ref, kip_ref, wt_ref, sga_ref, sgb_ref,
                 kc_ref, vc_ref, kic_ref):
    x = x_ref[...]
    ms = jnp.mean(x * x, axis=-1, keepdims=True)
    hn = (x * lax.rsqrt(ms + EPS) * g1_ref[...]).astype(BF16)
    ones_h = ones_h_ref[...]
    lane = lax.broadcasted_iota(I32, (x.shape[0], LANES), 1)

    u_ref[...] = _dot(hn, wu_ref[...]).astype(BF16)

    q = _dot(hn, wq_ref[...])
    scale = HEAD_DIM ** -0.5 * LOG2E
    for h in range(N_HEADS):
        qh = q[:, h * LANES:(h + 1) * LANES]
        msq = _dot_split(qh * qh, ones_h)
        q_ref[h] = (qh * lax.rsqrt(msq + EPS) * (gq_ref[...] * scale)).astype(BF16)

    k = _dot(hn, wk_ref[...])
    for g in range(N_KV_HEADS):
        kg = k[:, g * LANES:(g + 1) * LANES]
        msk = _dot_split(kg * kg, ones_h)
        kn = kg * lax.rsqrt(msk + EPS) * gk_ref[...]
        kp_ref[g] = jnp.where(lane == HEAD_DIM, 1.0, kn).astype(BF16)

    v = _dot(hn, wv_ref[...])
    for g in range(N_KV_HEADS):
        vg = v[:, g * LANES:(g + 1) * LANES]
        vp_ref[g] = jnp.where(lane == HEAD_DIM, 1.0, vg).astype(BF16)

    qi = _dot(hn, wqi_ref[...])
    for h in range(IDX_HEADS):
        qi_ref[h] = qi[:, h * LANES:(h + 1) * LANES].astype(BF16)

    ki = _dot(hn, wki_ref[...])
    mu = _dot_split(ki, ones_h)
    xc = jnp.where(lane < IDX_DIM, ki - mu, 0.0)
    var = _dot_split(xc * xc, ones_h)
    kin = xc * lax.rsqrt(var + EPS) * gi_ref[...] + bi_ref[...]
    kip_ref[...] = kin.astype(BF16)
    kic_ref[...] = kin[:, :IDX_DIM]

    wt = _dot_nt(wwit_ref[...], hn)
    wt_ref[...] = wt[0:IDX_HEADS, :] * (IDX_HEADS ** -0.5 * IDX_DIM ** -0.5)

    sga_ref[...] = jax.nn.sigmoid(_dot(hn, wga_ref[...])).astype(BF16)
    sgb_ref[...] = jax.nn.sigmoid(_dot(hn, wgb_ref[...])).astype(BF16)

    kc = _dot(hn, wkc_ref[...])
    mskc = _dot_split(kc * kc, ones_c_ref[...])
    kc_ref[...] = kc * lax.rsqrt(mskc + EPS) * gkc_ref[...]
    vc_ref[...] = _dot(hn, wvc_ref[...])


def _proj(x, pw, bsz, seq, tm):
    d = x.shape[-1]
    nt = seq // tm
    t = bsz * seq
    x2 = x.reshape(t, d)

    def tok(b, i):
        return (b * nt + i, 0)

    def cst(b, i):
        return (0, 0)

    def wspec(a):
        return pl.BlockSpec(a.shape, cst)

    weights = [pw['g1'], pw['wu'], pw['wq'], pw['wk'], pw['wv'], pw['wqi'], pw['wki'], pw['wwit'], pw['wga'],
               pw['wgb'], pw['wkc'], pw['wvc'], pw['gq'], pw['gk'], pw['gkc'], pw['gi'], pw['bi'],
               pw['ones_h'], pw['ones_c']]
    ssm_w = pw['wu'].shape[1]
    out_shape = (
        jax.ShapeDtypeStruct((seq, bsz * ssm_w), BF16),
        jax.ShapeDtypeStruct((bsz, N_HEADS, seq, LANES), BF16),
        jax.ShapeDtypeStruct((bsz, N_KV_HEADS, seq, LANES), BF16),
        jax.ShapeDtypeStruct((bsz, N_KV_HEADS, seq, LANES), BF16),
        jax.ShapeDtypeStruct((bsz, IDX_HEADS, seq, LANES), BF16),
        jax.ShapeDtypeStruct((t, LANES), BF16),
        jax.ShapeDtypeStruct((bsz, IDX_HEADS, seq), F32),
        jax.ShapeDtypeStruct((t, d), BF16),
        jax.ShapeDtypeStruct((t, d), BF16),
        jax.ShapeDtypeStruct((t, N_KV_HEADS * HEAD_DIM), F32),
        jax.ShapeDtypeStruct((t, N_KV_HEADS * HEAD_DIM), F32),
        jax.ShapeDtypeStruct((t, IDX_DIM), F32),
    )

    def hm(nh):
        return pl.BlockSpec((None, nh, tm, LANES), lambda b, i: (b, 0, i, 0))

    out_specs = (
        pl.BlockSpec((tm, ssm_w), lambda b, i: (i, b)),
        hm(N_HEADS), hm(N_KV_HEADS), hm(N_KV_HEADS), hm(IDX_HEADS),
        pl.BlockSpec((tm, LANES), tok),
        pl.BlockSpec((None, IDX_HEADS, tm), lambda b, i: (b, 0, i)),
        pl.BlockSpec((tm, d), tok), pl.BlockSpec((tm, d), tok),
        pl.BlockSpec((tm, N_KV_HEADS * HEAD_DIM), tok), pl.BlockSpec((tm, N_KV_HEADS * HEAD_DIM), tok),
        pl.BlockSpec((tm, IDX_DIM), tok),
    )
    return pl.pallas_call(
        _proj_kernel,
        grid=(bsz, nt),
        in_specs=[pl.BlockSpec((tm, d), tok)] + [wspec(a) for a in weights],
        out_specs=out_specs,
        out_shape=out_shape,
        compiler_params=_cp(("arbitrary", "arbitrary")),
        name="proj",
    )(x2, *weights)


def _gelu_tanh(x):
    return 0.5 * x * (1.0 + jnp.tanh(math.sqrt(2.0 / math.pi) * (x + 0.044715 * (x * x * x))))


def _s5_kernel(u_ref, h0_ref, bre_ref, bim_ref, are_ref, aim_ref, cre_ref, cim_ref, dvec_ref, wa_ref, wb_ref,
               ya_ref, hout_ref, state_ref, bu_ref, yf_ref, *, bsz, tc, strip):
    s = pl.program_id(0)
    half = are_ref.shape[1]

    @pl.when(s == 0)
    def _():
        state_ref[...] = h0_ref[...]

    u = u_ref[...]
    cw = u.shape[1] // S5_DIAG
    sw = half // S5_DIAG
    for j in range(S5_DIAG):
        uj = u[:, j * cw:(j + 1) * cw]
        bu_ref[:, j * sw:(j + 1) * sw] = _dot(uj, bre_ref[j])
        bu_ref[:, half + j * sw:half + (j + 1) * sw] = _dot(uj, bim_ref[j])

    for c0 in range(0, half, strip):
        ar = jnp.broadcast_to(are_ref[:, c0:c0 + strip], (bsz, strip))
        ai = jnp.broadcast_to(aim_ref[:, c0:c0 + strip], (bsz, strip))
        hr0 = state_ref[:, c0:c0 + strip]
        hi0 = state_ref[:, half + c0:half + c0 + strip]

        def step(t, carry):
            hr, hi = carry
            r0 = pl.multiple_of(t * bsz, bsz)
            br = bu_ref[pl.ds(r0, bsz), c0:c0 + strip]
            bi = bu_ref[pl.ds(r0, bsz), half + c0:half + c0 + strip]
            nr = ar * hr - ai * hi + br
            ni = ar * hi + ai * hr + bi
            bu_ref[pl.ds(r0, bsz), c0:c0 + strip] = nr
            bu_ref[pl.ds(r0, bsz), half + c0:half + c0 + strip] = ni
            return nr, ni

        hr, hi = lax.fori_loop(0, tc, step, (hr0, hi0))
        state_ref[:, c0:c0 + strip] = hr
        state_ref[:, half + c0:half + c0 + strip] = hi

    ys = []
    for j in range(S5_DIAG):
        s_re = bu_ref[:, j * sw:(j + 1) * sw].astype(BF16)
        s_im = bu_ref[:, half + j * sw:half + (j + 1) * sw].astype(BF16)
        ys.append(_dot(s_re, cre_ref[j]) + _dot(s_im, cim_ref[j]))
    y = jnp.concatenate(ys, axis=1) + dvec_ref[...] * u.astype(F32)
    g = _gelu_tanh(y).astype(BF16)
    ya = _dot(g, wa_ref[...]) * jax.nn.sigmoid(_dot(g, wb_ref[...]))
    n_chunk = ya.shape[1] // LANES
    for c in range(n_chunk):
        yf_ref[c] = ya[:, c * LANES:(c + 1) * LANES]
    for b in range(bsz):
        ya_ref[b] = jnp.concatenate([yf_ref[c, pl.ds(b, tc, stride=bsz), :] for c in range(n_chunk)],
                                    axis=1).astype(BF16)

    @pl.when(s == pl.num_programs(0) - 1)
    def _():
        hout_ref[...] = state_ref[...]


def _s5(u_tb, h0, sw, bsz, seq, tc):
    rows = tc * bsz
    ssm_w = sw['d'].shape[1]
    half = sw['a_re'].shape[1]
    two_half = 2 * half
    d = sw['wa'].shape[1]
    u2 = u_tb.reshape(seq * bsz, ssm_w)
    strip = min(512, half)

    consts = [h0, sw['b_re'], sw['b_im'], sw['a_re'], sw['a_im'], sw['c_re'], sw['c_im'], sw['d'], sw['wa'], sw['wb']]

    def cst(s):
        return (0, 0)

    def cspec(a):
        return pl.BlockSpec(a.shape, lambda s: (0,) * a.ndim)

    ya, hout = pl.pallas_call(
        functools.partial(_s5_kernel, bsz=bsz, tc=tc, strip=strip),
        grid=(seq // tc,),
        in_specs=[pl.BlockSpec((rows, ssm_w), lambda s: (s, 0))] + [cspec(a) for a in consts],
        out_specs=(pl.BlockSpec((bsz, tc, d), lambda s: (0, s, 0)), pl.BlockSpec((bsz, two_half), cst)),
        out_shape=(jax.ShapeDtypeStruct((bsz, seq, d), BF16), jax.ShapeDtypeStruct((bsz, two_half), F32)),
        scratch_shapes=[pltpu.VMEM((bsz, two_half), F32), pltpu.VMEM((rows, two_half), F32),
                        pltpu.VMEM((d // LANES, rows, LANES), F32)],
        compiler_params=_cp(("arbitrary",)),
        name="s5",
    )(u2, *consts)
    return ya.reshape(bsz * seq, d), hout


def _f2key(x):
    b = lax.bitcast_convert_type(x, I32)
    return b ^ ((b >> 31) & 0x7FFFFFFF)


def _key2f(k):
    return lax.bitcast_convert_type(k ^ ((k >> 31) & 0x7FFFFFFF), F32)


def _dsa_kernel(st_ref, q_ref, qi_ref, wt_ref, k_ref, vt_ref, ki_ref, bias_ref, o_ref,
                s_ref, lo_ref, hi_ref, clo_ref, glo_ref, ghi_ref, side_ref, q2_ref, mrow_ref, acc_ref,
                *, bsz, tq, past, n_keys, topk, nkt, nd, idx_bits):
    b_id = pl.program_id(0)
    i = pl.program_id(1)
    kb = KEY_BLOCK
    sl = SUBLANES
    q0 = past + i * tq
    last_chunk = (q0 + tq - 1) // CHUNK
    n_kt = jnp.minimum(nkt, ((last_chunk + 1) * CHUNK + kb - 1) // kb)
    d0 = q0 // KEY_TILE

    krow = lax.broadcasted_iota(I32, (kb, tq), 0)
    q_chunk = (q0 + lax.broadcasted_iota(I32, (kb, tq), 1)) // CHUNK
    qc8 = (q0 + lax.broadcasted_iota(I32, (sl, tq), 1)) // CHUNK
    n_adm = jnp.minimum((qc8 + 1) * CHUNK, n_keys)
    n_admf = n_adm.astype(F32)
    needf = jnp.minimum(topk, n_adm).astype(F32)

    def bcast(x):
        return jnp.broadcast_to(x[0:1, :], (kb, tq))

    def rep(x):
        return jnp.broadcast_to(x, (sl, tq))

    qi = qi_ref[...].reshape(IDX_HEADS * tq, LANES)

    def score_block(kt, masked):
        k0 = pl.multiple_of(kt * kb, kb)
        s = _dot_nt(ki_ref[pl.ds(k0, kb), :], qi)
        sc = jnp.zeros((kb, tq), F32)
        for h in range(IDX_HEADS):
            sc = sc + wt_ref[h:h + 1, :] * jnp.maximum(s[:, h * tq:(h + 1) * tq], 0.0)
        if masked:
            kpos = k0 + krow
            adm = ((kpos // CHUNK) <= q_chunk) & (kpos < n_keys)
            sc = jnp.where(adm, sc, -jnp.inf)
        s_ref[kt] = sc

    def score_pair(j, c):
        score_block(2 * j, False)
        score_block(2 * j + 1, False)
        return c

    lax.fori_loop(0, (n_kt - 1) // 2, score_pair, 0)

    @pl.when((n_kt - 1) % 2 == 1)
    def _():
        score_block(n_kt - 2, False)

    score_block(n_kt - 1, True)

    part = 4 * sl

    def fold(x, op):
        x = x.reshape(kb // part, part, tq)
        acc = x[0]
        for j in range(1, kb // part):
            acc = op(acc, x[j])
        return acc

    def count(pred):
        def body(kt, c):
            return c + fold(jnp.where(pred(s_ref[kt], kt), 1.0, 0.0), jnp.add)
        c = lax.fori_loop(0, n_kt, body, jnp.zeros((part, tq), F32))
        return rep(jnp.sum(c, axis=0, keepdims=True))

    def minmax(kt, c):
        mx, mn = c
        s = s_ref[kt]
        return (jnp.maximum(mx, fold(s, jnp.maximum)),
                jnp.minimum(mn, fold(jnp.where(s == -jnp.inf, jnp.inf, s), jnp.minimum)))

    mx, mn = lax.fori_loop(0, n_kt, minmax,
                           (jnp.full((part, tq), -jnp.inf, F32), jnp.full((part, tq), jnp.inf, F32)))
    lo_ref[...] = rep(jnp.min(mn, axis=0, keepdims=True))
    hi_ref[...] = _key2f(_f2key(rep(jnp.max(mx, axis=0, keepdims=True))) + 1)
    def odds(cnt):
        c = jnp.clip(cnt, 0.5, n_admf - 0.5)
        return jnp.log((n_admf - c) / c)

    target = odds(needf - 0.5)
    clo_ref[...] = n_admf
    glo_ref[...] = target - odds(n_admf)
    ghi_ref[...] = target - odds(jnp.zeros((sl, tq), F32))
    side_ref[...] = jnp.zeros((sl, tq), F32)

    def searching(lo, hi, clo):
        return (_f2key(hi) > _f2key(lo) + 1) & (clo > needf)

    def refine(it, c):
        lo, hi, clo = lo_ref[...], hi_ref[...], clo_ref[...]
        glo, ghi, side = glo_ref[...], ghi_ref[...], side_ref[...]
        k_t = _f2key(lo + (hi - lo) * (glo / (glo - ghi)))
        t = _key2f(jnp.minimum(jnp.maximum(k_t, _f2key(lo) + 1), _f2key(hi) - 1))
        tb = bcast(t)
        cnt = count(lambda s, kt: s >= tb)
        g = target - odds(cnt)
        open_ = searching(lo, hi, clo)
        up = open_ & (cnt >= needf)
        dn = open_ & (cnt < needf)
        lo_ref[...] = jnp.where(up, t, lo)
        clo_ref[...] = jnp.where(up, cnt, clo)
        hi_ref[...] = jnp.where(dn, t, hi)
        glo_ref[...] = jnp.where(up, g, jnp.where(dn & (side < 0.0), glo * 0.5, glo))
        ghi_ref[...] = jnp.where(dn, g, jnp.where(up & (side > 0.0), ghi * 0.5, ghi))
        side_ref[...] = jnp.where(up, 1.0, jnp.where(dn, -1.0, side))
        return c

    def snap():
        lo, hi, clo = lo_ref[...], hi_ref[...], clo_ref[...]
        lo_b, hi_b = bcast(lo), bcast(hi)

        def body(kt, c):
            a, b = c
            s = s_ref[kt]
            return (jnp.minimum(a, fold(jnp.where(s >= lo_b, s, jnp.inf), jnp.minimum)),
                    jnp.maximum(b, fold(jnp.where(s < hi_b, s, -jnp.inf), jnp.maximum)))

        a, b = lax.fori_loop(0, n_kt, body,
                             (jnp.full((part, tq), jnp.inf, F32), jnp.full((part, tq), -jnp.inf, F32)))
        open_ = searching(lo, hi, clo)
        lo_ref[...] = jnp.where(open_, rep(jnp.min(a, axis=0, keepdims=True)), lo)
        hi_ref[...] = jnp.where(open_, _key2f(_f2key(rep(jnp.max(b, axis=0, keepdims=True))) + 1), hi)

    def n_searching():
        return jnp.max(jnp.where(searching(lo_ref[...], hi_ref[...], clo_ref[...]), 1.0, 0.0))

    def group(c):
        grp, _ = c
        lax.fori_loop(0, jnp.where(grp == 0, SEARCH_FIRST, SEARCH_GROUP), refine, 0)
        snap()
        return grp + 1, n_searching()

    lax.while_loop(lambda c: c[1] > 0.0, group, (jnp.int32(0), n_searching()))
    thr = lo_ref[...]
    thr_b = bcast(thr)

    n_tied = jnp.max(jnp.where(clo_ref[...] > needf, 1.0, 0.0))

    @pl.when(n_tied > 0.0)
    def _():
        rem = needf - count(lambda s, kt: s > thr_b)
        lo_ref[...] = jnp.zeros((sl, tq), F32)
        hi_ref[...] = jnp.full((sl, tq), float(nkt * kb), F32)
        krowf = krow.astype(F32)

        def bisect_idx(it, c):
            lo = lo_ref[...]
            hi = hi_ref[...]
            mid = jnp.floor((lo + hi) * 0.5)
            mid_b = bcast(mid)
            ok = count(lambda s, kt: (s == thr_b) & ((kt * kb).astype(F32) + krowf < mid_b)) >= rem
            hi_ref[...] = jnp.where(ok, mid, hi)
            lo_ref[...] = jnp.where(ok, lo, mid)
            return c

        lax.fori_loop(0, idx_bits, bisect_idx, 0)
        cut_b = bcast(hi_ref[...])

        def drop(kt, c):
            s = s_ref[kt]
            s_ref[kt] = jnp.where((s == thr_b) & ((kt * kb).astype(F32) + krowf >= cut_b), -jnp.inf, s)
            return c

        lax.fori_loop(0, n_kt, drop, 0)

    rows_g = KV_REP * tq
    qf = q_ref[...].reshape(N_HEADS * tq, LANES).astype(F32)
    qn = jnp.sqrt(jnp.sum(qf * qf, axis=1, keepdims=True))
    lane = lax.broadcasted_iota(I32, (tq, LANES), 1)
    worst = jnp.float32(0.0)
    for h in range(N_HEADS):
        kmax = st_ref[b_id * N_KV_HEADS + h // KV_REP]
        bmax = st_ref[bsz * N_KV_HEADS + h]
        bfar = st_ref[bsz * N_KV_HEADS + N_HEADS + h]
        bound = qn[h * tq:(h + 1) * tq, :] * (kmax * 1.01) + (bmax + 0.1)
        worst = jnp.maximum(worst, jnp.max(bound))
        q2_ref[h * tq:(h + 1) * tq, :] = jnp.where(lane == HEAD_DIM, bfar - bound,
                                                   qf[h * tq:(h + 1) * tq, :]).astype(BF16)
    n_far = jnp.clip((d0 - nd) // 2 + 1, 0, n_kt)

    def logits(kt, g, near, exact):
        k0 = pl.multiple_of(kt * kb, kb)
        maskadd = jnp.where(s_ref[kt] >= thr_b, 0.0, NEG_BIG)
        s = _dot_nt(k_ref[g, pl.ds(k0, kb), :], q2_ref[g * rows_g:(g + 1) * rows_g, :])
        if near:
            da = jnp.clip(d0 - 2 * kt, 0, nd - 1)
            db = jnp.clip(d0 - 2 * kt - 1, 0, nd - 1)
        parts = []
        for r in range(KV_REP):
            h = g * KV_REP + r
            add = maskadd - mrow_ref[0:1, h * tq:(h + 1) * tq] if exact else maskadd
            if near:
                add = jnp.concatenate([bias_ref[da, h], bias_ref[db, h]], axis=0) + add
            parts.append(s[:, r * tq:(r + 1) * tq] + add)
        return jnp.concatenate(parts, axis=1)

    def over_blocks(fn):
        def far_pair(j, c):
            fn((2 * j, 2 * j + 1), False)
            return c
        lax.fori_loop(0, n_far // 2, far_pair, 0)

        @pl.when(n_far % 2 == 1)
        def _():
            fn((n_far - 1,), False)

        n_near = n_kt - n_far

        def near_pair(j, c):
            fn((n_far + 2 * j, n_far + 2 * j + 1), True)
            return c
        lax.fori_loop(0, n_near // 2, near_pair, 0)

        @pl.when(n_near % 2 == 1)
        def _():
            fn((n_kt - 1,), True)

    def attend(exact):
        acc_ref[...] = jnp.zeros(acc_ref.shape, F32)

        def blocks(kts, near):
            for g in range(N_KV_HEADS):
                pv = None
                for kt in kts:
                    p = jnp.exp2(logits(kt, g, near, exact)).astype(BF16)
                    d = _dot(vt_ref[g, kt], p)
                    pv = d if pv is None else pv + d
                acc_ref[g] += pv
        over_blocks(blocks)

    @pl.when(worst <= SHIFT_LIMIT)
    def _():
        attend(False)

    @pl.when(worst > SHIFT_LIMIT)
    def _():
        mrow_ref[...] = jnp.full(mrow_ref.shape, NEG_BIG, F32)

        def blocks(kts, near):
            for g in range(N_KV_HEADS):
                for kt in kts:
                    mx = jnp.max(logits(kt, g, near, False), axis=0, keepdims=True)
                    cur = mrow_ref[:, g * rows_g:(g + 1) * rows_g]
                    mrow_ref[:, g * rows_g:(g + 1) * rows_g] = jnp.maximum(cur, jnp.broadcast_to(mx, (sl, rows_g)))
        over_blocks(blocks)
        attend(True)

    for g in range(N_KV_HEADS):
        acc = acc_ref[g]
        og = acc / acc[HEAD_DIM:HEAD_DIM + 1, :]
        for r in range(KV_REP):
            h = g * KV_REP + r
            o_ref[:, h * LANES:(h + 1) * LANES] = og[:, r * tq:(r + 1) * tq].T.astype(BF16)


def _dsa(stats, q, qi, wt, k_all, vt_all, ki_all, bias_tiles, bsz, seq, past, n_keys, tq):
    lk = k_all.shape[2]
    nkt = lk // KEY_BLOCK
    topk = min(TOPK_MAX, n_keys // 4)
    nd = bias_tiles.shape[0]
    nq = seq // tq
    assert past % KEY_TILE == 0 and tq == KEY_TILE
    idx_bits = int(math.ceil(math.log2(lk))) + 1
    kern = functools.partial(_dsa_kernel, bsz=bsz, tq=tq, past=past, n_keys=n_keys, topk=topk, nkt=nkt, nd=nd,
                             idx_bits=idx_bits)
    row_state = pltpu.VMEM((SUBLANES, tq), F32)
    grid_spec = pltpu.PrefetchScalarGridSpec(
        num_scalar_prefetch=1,
        grid=(bsz, nq),
        in_specs=[
            pl.BlockSpec((None, N_HEADS, tq, LANES), lambda b, i, s: (b, 0, i, 0)),
            pl.BlockSpec((None, IDX_HEADS, tq, LANES), lambda b, i, s: (b, 0, i, 0)),
            pl.BlockSpec((None, IDX_HEADS, tq), lambda b, i, s: (b, 0, i)),
            pl.BlockSpec((None, N_KV_HEADS, lk, LANES), lambda b, i, s: (b, 0, 0, 0)),
            pl.BlockSpec((None, N_KV_HEADS, nkt, LANES, KEY_BLOCK), lambda b, i, s: (b, 0, 0, 0, 0)),
            pl.BlockSpec((None, lk, LANES), lambda b, i, s: (b, 0, 0)),
            pl.BlockSpec(bias_tiles.shape, lambda b, i, s: (0, 0, 0, 0)),
        ],
        out_specs=pl.BlockSpec((None, tq, N_HEADS * LANES), lambda b, i, s: (b, i, 0)),
        scratch_shapes=[
            pltpu.VMEM((nkt, KEY_BLOCK, tq), F32),
            row_state, row_state, row_state, row_state, row_state, row_state,
            pltpu.VMEM((N_HEADS * tq, LANES), BF16),
            pltpu.VMEM((SUBLANES, N_HEADS * tq), F32),
            pltpu.VMEM((N_KV_HEADS, LANES, KV_REP * tq), F32),
        ],
    )
    return pl.pallas_call(
        kern,
        grid_spec=grid_spec,
        out_shape=jax.ShapeDtypeStruct((bsz, seq, N_HEADS * LANES), BF16),
        compiler_params=_cp(("arbitrary", "arbitrary")),
        name="dsa",
    )(stats, q, qi, wt, k_all, vt_all, ki_all, bias_tiles)


def _merge_kernel(x_ref, ya_ref, at_ref, sga_ref, sgb_ref, wup_ref, wout_ref, g2_ref, wr_hi_ref, wr_lo_ref, br_ref,
                  x1_ref, h2_ref, gt_ref, rt_ref, cnt_ref, run_ref, *, tm, sub):
    step = pl.program_id(0)

    @pl.when(step % sub == 0)
    def _():
        run_ref[...] = jnp.zeros(run_ref.shape, F32)

    yb = _dot(at_ref[...], wup_ref[...])
    merged = sga_ref[...].astype(F32) * ya_ref[...].astype(F32) + sgb_ref[...].astype(F32) * yb
    x1 = x_ref[...] + _dot(merged.astype(BF16), wout_ref[...])
    x1_ref[...] = x1
    ms = jnp.mean(x1 * x1, axis=-1, keepdims=True)
    h2 = x1 * lax.rsqrt(ms + EPS) * g2_ref[...]
    h2_hi, h2_lo = _split(h2)
    h2_ref[...] = h2_hi

    wr_hi = wr_hi_ref[...]
    logit = (_dot_nt(wr_hi, h2_hi) + _dot_nt(wr_hi, h2_lo) + _dot_nt(wr_lo_ref[...], h2_hi)) + br_ref[:, 0:1]
    ne = logit.shape[0]
    eid = lax.broadcasted_iota(I32, (ne, tm), 0).astype(F32)
    selb = jnp.zeros((ne, tm), F32)
    tops = []
    picks = []
    for _ in range(TOP_K):
        mx = jnp.max(logit, axis=0, keepdims=True)
        pick = jnp.min(jnp.where(logit == mx, eid, float(ne)), axis=0, keepdims=True)
        hit = eid == pick
        selb = jnp.where(hit, 1.0, selb)
        logit = jnp.where(hit, -jnp.inf, logit)
        tops.append(mx)
        picks.append(hit)
    ex = [jnp.exp(t - tops[0]) for t in tops]
    den = ex[0] + ex[1] + ex[2] + ex[3]
    gate = jnp.zeros((ne, tm), F32)
    for hit, e in zip(picks, ex):
        gate = jnp.where(hit, e / den, gate)
    gt_ref[...] = gate

    sel = selb > 0.5
    selb = selb.astype(BF16)
    r_i = lax.broadcasted_iota(I32, (tm, tm), 0)
    c_i = lax.broadcasted_iota(I32, (tm, tm), 1)
    tri = jnp.where(r_i < c_i, 1.0, 0.0).astype(BF16)
    run = run_ref[...]
    rank = _dot(selb, tri) + jnp.broadcast_to(run[:, 0:1], (ne, tm))
    rt_ref[...] = jnp.where(sel, rank, -1.0)
    run = run + _dot(selb, jnp.ones((tm, LANES), BF16))
    run_ref[...] = run
    cnt_ref[...] = run


def _merge(x2, ya, attn, sga, sgb, mw, tm, moe_tile):
    t, d = x2.shape
    sub = moe_tile // tm
    ne = mw['wr_hi'].shape[0]

    def tok(i):
        return (i, 0)

    def cst(i):
        return (0, 0)

    consts = [mw['wup'], mw['wout'], mw['g2'], mw['wr_hi'], mw['wr_lo'], mw['br']]
    return pl.pallas_call(
        functools.partial(_merge_kernel, tm=tm, sub=sub),
        grid=(t // tm,),
        in_specs=[
            pl.BlockSpec((tm, d), tok),
            pl.BlockSpec((tm, d), tok),
            pl.BlockSpec((tm, attn.shape[-1]), tok),
            pl.BlockSpec((tm, d), tok),
            pl.BlockSpec((tm, d), tok),
        ] + [pl.BlockSpec(a.shape, cst) for a in consts],
        out_specs=(
            pl.BlockSpec((tm, d), tok),
            pl.BlockSpec((tm, d), tok),
            pl.BlockSpec((ne, tm), lambda i: (0, i)),
            pl.BlockSpec((ne, tm), lambda i: (0, i)),
            pl.BlockSpec((None, ne, LANES), lambda i: (i // sub, 0, 0)),
        ),
        out_shape=(
            jax.ShapeDtypeStruct((t, d), F32),
            jax.ShapeDtypeStruct((t, d), BF16),
            jax.ShapeDtypeStruct((ne, t), F32),
            jax.ShapeDtypeStruct((ne, t), F32),
            jax.ShapeDtypeStruct((t // moe_tile, ne, LANES), F32),
        ),
        scratch_shapes=[pltpu.VMEM((ne, LANES), F32)],
        compiler_params=_cp(("arbitrary",)),
        name="merge",
    )(x2, ya, attn, sga, sgb, *consts)


def _moe_kernel(cnt_ref, h2_ref, x1_hbm, gt_ref, rt_ref, wg_ref, wu_ref, wd_ref, bg_ref, bu_ref, bd_ref, y_ref,
                pg_ref, og_ref, sem, *, tt, pair):
    j = pl.program_id(0)
    e = pl.program_id(1)
    ne = pl.num_programs(1)
    rb = MOE_ROWS
    slot = e % MOE_GROUP

    @pl.when(e == 0)
    def _():
        cp = pltpu.make_async_copy(x1_hbm.at[pl.ds(pl.multiple_of(j * pair * tt, tt), pair * tt), :], y_ref, sem)
        cp.start()
        cp.wait()

    mine = lax.broadcasted_iota(I32, (SUBLANES, pair * tt), 0) == e % SUBLANES
    g_all = jnp.sum(jnp.where(mine, gt_ref[...], 0.0), axis=0, keepdims=True)
    r_all = jnp.sum(jnp.where(mine, rt_ref[...], 0.0), axis=0, keepdims=True)
    rid = lax.broadcasted_iota(I32, (rb, tt), 0).astype(F32)

    def one_hot(s, blk):
        return jnp.broadcast_to(r_all[:, s * tt:(s + 1) * tt], (rb, tt)) == (rid + (blk * rb).astype(F32))

    def gather(s, hit):
        p = jnp.where(hit, 1.0, 0.0).astype(BF16)
        return p, _dot(p, h2_ref[s * tt:(s + 1) * tt, :]).astype(BF16)

    def expert(xg):
        a = jnp.minimum(_dot(xg, wg_ref[0]) + bg_ref[0], SWIGLU_LIMIT)
        b = jnp.clip(_dot(xg, wu_ref[0]) + bu_ref[0], -SWIGLU_LIMIT, SWIGLU_LIMIT)
        hid = a * jax.nn.sigmoid(SWIGLU_ALPHA * a) * (b + 1.0)
        return _dot(hid.astype(BF16), wd_ref[0]) + bd_ref[0]

    def gated(s, hit, o):
        g_row = jnp.broadcast_to(g_all[:, s * tt:(s + 1) * tt], (rb, tt))
        return (o * jnp.sum(jnp.where(hit, g_row, 0.0), axis=1, keepdims=True)).astype(BF16)

    hits = [one_hot(s, jnp.int32(0)) for s in range(pair)]
    gathered = [gather(s, hits[s]) for s in range(pair)]
    o = expert(jnp.concatenate([xg for _, xg in gathered], axis=0))
    r0 = pl.multiple_of(slot * rb, rb)
    for s in range(pair):
        pg_ref[s, pl.ds(r0, rb), :] = gathered[s][0]
        og_ref[s, pl.ds(r0, rb), :] = gated(s, hits[s], o[s * rb:(s + 1) * rb])

    @pl.when(slot == MOE_GROUP - 1)
    def _():
        for s in range(pair):
            y_ref[s * tt:(s + 1) * tt, :] += _dot_tn(pg_ref[s], og_ref[s])

    for s in range(pair):
        def overflow(blk, c, s=s):
            hit = one_hot(s, blk)
            p, xg = gather(s, hit)
            y_ref[s * tt:(s + 1) * tt, :] += _dot_tn(p, gated(s, hit, expert(xg)))
            return c

        n_rows = cnt_ref[(j * pair + s) * ne + e]
        lax.fori_loop(1, (n_rows + rb - 1) // rb, overflow, 0)


def _moe(h2, x1, gt, rt, cnt, ew, tt):
    t, d = h2.shape
    ne = gt.shape[0]
    nt = t // tt
    f = ew['wg'].shape[-1]
    pair = MOE_PAIR if nt % MOE_PAIR == 0 else 1
    grid_spec = pltpu.PrefetchScalarGridSpec(
        num_scalar_prefetch=1,
        grid=(nt // pair, ne),
        in_specs=[
            pl.BlockSpec((pair * tt, d), lambda j, e, c: (j, 0)),
            pl.BlockSpec(memory_space=pl.ANY),
            pl.BlockSpec((SUBLANES, pair * tt), lambda j, e, c: (e // SUBLANES, j)),
            pl.BlockSpec((SUBLANES, pair * tt), lambda j, e, c: (e // SUBLANES, j)),
            pl.BlockSpec((1, d, f), lambda j, e, c: (e, 0, 0)),
            pl.BlockSpec((1, d, f), lambda j, e, c: (e, 0, 0)),
            pl.BlockSpec((1, f, d), lambda j, e, c: (e, 0, 0)),
            pl.BlockSpec((1, 1, f), lambda j, e, c: (e, 0, 0)),
            pl.BlockSpec((1, 1, f), lambda j, e, c: (e, 0, 0)),
            pl.BlockSpec((1, 1, d), lambda j, e, c: (e, 0, 0)),
        ],
        out_specs=pl.BlockSpec((pair * tt, d), lambda j, e, c: (j, 0)),
        scratch_shapes=[pltpu.VMEM((pair, MOE_GROUP * MOE_ROWS, tt), BF16),
                        pltpu.VMEM((pair, MOE_GROUP * MOE_ROWS, d), BF16),
                        pltpu.SemaphoreType.DMA(())],
    )
    assert ne % MOE_GROUP == 0
    return pl.pallas_call(
        functools.partial(_moe_kernel, tt=tt, pair=pair),
        grid_spec=grid_spec,
        out_shape=jax.ShapeDtypeStruct((t, d), F32),
        compiler_params=_cp(("arbitrary", "arbitrary")),
        name="moe",
    )(cnt, h2, x1, gt, rt, ew['wg'], ew['wu'], ew['wd'], ew['bg'], ew['bu'], ew['bd'])


def _pad_heads(wmat, n_heads, width):
    d = wmat.shape[0]
    w3 = wmat.reshape(d, n_heads, width)
    return jnp.pad(w3, ((0, 0), (0, 0), (0, LANES - width))).reshape(d, n_heads * LANES)


def _pad_lanes(v, width=LANES):
    v = v.reshape(1, -1)
    return jnp.pad(v, ((0, 0), (0, width - v.shape[1])))


def _rel_bucket(rel):
    half = REL_BUCKETS // 2
    max_exact = half // 2
    n = jnp.abs(rel)
    large = max_exact + (jnp.log(jnp.maximum(n, 1).astype(jnp.float32) / max_exact)
                         / math.log(REL_MAX_DIST / max_exact) * (half - max_exact)).astype(jnp.int32)
    large = jnp.minimum(large, half - 1)
    return jnp.where(rel > 0, half, 0) + jnp.where(n < max_exact, n, large)


def _bias_tiles(rel_bias):
    tk = KEY_TILE
    half = REL_BUCKETS // 2
    max_exact = half // 2
    n_sat = int(math.ceil(max_exact * (REL_MAX_DIST / max_exact) ** ((half - 1 - max_exact) / (half - max_exact)))) + 2
    nd = (n_sat + 2 * tk - 2) // tk + 1
    dd = jnp.arange(nd, dtype=I32)[:, None, None]
    c = jnp.arange(tk, dtype=I32)[None, :, None]
    r = jnp.arange(tk, dtype=I32)[None, None, :]
    bucket = _rel_bucket(c - r - dd * tk)
    onehot = (bucket[..., None] == jnp.arange(REL_BUCKETS, dtype=I32)).astype(F32)
    tiles = jnp.einsum('dcrb,bh->dhcr', onehot, rel_bias.astype(F32) * LOG2E,
                       precision=lax.Precision.HIGHEST)
    return tiles


def _prep_proj(norm1_g, w_in, q_norm_g, k_norm_g, idx_k_norm_g, idx_k_norm_b, d_model):
    ssm_w = d_model // 2
    attn_w = N_HEADS * HEAD_DIM
    kv = N_KV_HEADS * HEAD_DIM
    sizes = [ssm_w, attn_w, kv, kv, IDX_HEADS * IDX_DIM, IDX_DIM, IDX_HEADS, d_model, d_model]
    pts = np.cumsum(sizes)[:-1].tolist()
    wu, wq, wk, wv, wqi, wki, wwi, wga, wgb = jnp.split(w_in, pts, axis=1)
    bf = lambda a: a.astype(BF16)
    blk = np.kron(np.eye(N_KV_HEADS), np.ones((HEAD_DIM, HEAD_DIM))) / HEAD_DIM
    wwit = jnp.pad(wwi.T, ((0, 2 * SUBLANES - IDX_HEADS), (0, 0)))
    return dict(
        g1=norm1_g.reshape(1, -1).astype(F32),
        wu=bf(wu), wq=bf(_pad_heads(wq, N_HEADS, HEAD_DIM)), wk=bf(_pad_heads(wk, N_KV_HEADS, HEAD_DIM)),
        wv=bf(_pad_heads(wv, N_KV_HEADS, HEAD_DIM)), wqi=bf(_pad_heads(wqi, IDX_HEADS, IDX_DIM)),
        wki=bf(_pad_heads(wki, 1, IDX_DIM)), wwit=bf(wwit),
        wga=bf(wga), wgb=bf(wgb), wkc=bf(wk), wvc=bf(wv),
        gq=_pad_lanes(q_norm_g.astype(F32)), gk=_pad_lanes(k_norm_g.astype(F32)),
        gkc=jnp.tile(k_norm_g.astype(F32), N_KV_HEADS).reshape(1, -1),
        gi=_pad_lanes(idx_k_norm_g.astype(F32)), bi=_pad_lanes(idx_k_norm_b.astype(F32)),
        ones_h=jnp.full((LANES, LANES), 1.0 / HEAD_DIM, BF16),
        ones_c=jnp.asarray(blk, BF16),
    )


def _prep_s5(lre, lim, log_dt, b_re, b_im, c_re, c_im, dvec, wa, wb):
    g, p = lre.shape
    ch = b_re.shape[-1]
    lam = lax.complex(lre.astype(F32), lim.astype(F32))
    dt = jnp.exp(log_dt.astype(F32))[:, None]
    a_bar = jnp.exp(lam * dt)
    b_bar = ((a_bar - 1.0) / lam)[:, :, None] * lax.complex(b_re.astype(F32), b_im.astype(F32))
    gs = g // S5_DIAG
    eye = jnp.eye(gs, dtype=F32)

    def blocks_in(m):
        return jnp.einsum('jgpc,gh->jgchp', m.reshape(S5_DIAG, gs, p, ch), eye).reshape(S5_DIAG, gs * ch, gs * p)

    def blocks_out(m):
        return jnp.einsum('jgcp,gh->jgphc', m.reshape(S5_DIAG, gs, ch, p), eye).reshape(S5_DIAG, gs * p, gs * ch)

    return dict(
        b_re=blocks_in(jnp.real(b_bar)).astype(BF16), b_im=blocks_in(jnp.imag(b_bar)).astype(BF16),
        c_re=blocks_out(c_re.astype(F32)).astype(BF16), c_im=blocks_out(-c_im.astype(F32)).astype(BF16),
        a_re=jnp.real(a_bar).reshape(1, g * p), a_im=jnp.imag(a_bar).reshape(1, g * p),
        d=dvec.reshape(1, -1).astype(F32), wa=wa.astype(BF16), wb=wb.astype(BF16),
    )


def _prep_merge(w_attn_up, w_out, norm2_g, w_router, b_router):
    d = w_attn_up.shape[1]
    wup = jnp.pad(w_attn_up.reshape(N_HEADS, HEAD_DIM, d), ((0, 0), (0, LANES - HEAD_DIM), (0, 0)))
    wr_t = w_router.astype(F32).T
    wr_hi = wr_t.astype(BF16)
    wr_lo = (wr_t - wr_hi.astype(F32)).astype(BF16)
    return dict(
        wup=wup.reshape(N_HEADS * LANES, d).astype(BF16), wout=w_out.astype(BF16),
        g2=norm2_g.reshape(1, -1).astype(F32), wr_hi=wr_hi, wr_lo=wr_lo,
        br=jnp.broadcast_to(b_router.astype(F32)[:, None], (b_router.shape[0], LANES)),
    )


def _prep_moe(wg, bg, wu, bu, wd, bd):
    return dict(wg=wg.astype(BF16), wu=wu.astype(BF16), wd=wd.astype(BF16),
                bg=bg.astype(F32)[:, None, :], bu=bu.astype(F32)[:, None, :], bd=bd.astype(F32)[:, None, :])


def _pick_tile(n, pref):
    t = min(n, pref)
    while n % t:
        t //= 2
    return t


def _pad_axis(a, axis, size):
    pad = [(0, 0)] * a.ndim
    pad[axis] = (0, size - a.shape[axis])
    return jnp.pad(a, pad)


def _trunk_layer(x, past_k, past_v, past_ik, h0_re, h0_im, pw, sw, mw, ew, bias_tiles):
    bsz, seq, d = x.shape
    t = bsz * seq
    tm = _pick_tile(seq, 512)
    u_tb, q, kp, vp, qi, kip, wt, sga, sgb, kc, vc, kic = _proj(x, pw, bsz, seq, tm)

    half = sw['a_re'].shape[1]
    if h0_re is None:
        h0 = jnp.zeros((bsz, 2 * half), F32)
    else:
        h0 = jnp.concatenate([h0_re.reshape(bsz, half), h0_im.reshape(bsz, half)], axis=1).astype(F32)
    tc = _pick_tile(seq, max(1, 512 // bsz))
    ya, hout = _s5(u_tb, h0, sw, bsz, seq, tc)
    groups = half // SSM_STATE
    s_re = hout[:, :half].reshape(bsz, groups, SSM_STATE)
    s_im = hout[:, half:].reshape(bsz, groups, SSM_STATE)

    past = 0 if past_k is None else past_k.shape[1]
    n_keys = past + seq
    lk = -(-n_keys // KEY_BLOCK) * KEY_BLOCK
    kip3 = kip.reshape(bsz, seq, LANES)
    if past:
        lane = jnp.arange(LANES)
        pk = jnp.pad(past_k.astype(F32), ((0, 0), (0, 0), (0, 0), (0, LANES - HEAD_DIM)))
        pk = jnp.where(lane == HEAD_DIM, 1.0, pk).astype(BF16)
        pv = jnp.pad(past_v.astype(F32), ((0, 0), (0, 0), (0, 0), (0, LANES - HEAD_DIM)))
        pv = jnp.where(lane == HEAD_DIM, 1.0, pv).astype(BF16)
        pik = jnp.pad(past_ik.astype(F32), ((0, 0), (0, 0), (0, LANES - IDX_DIM))).astype(BF16)
        k_all = jnp.concatenate([pk.transpose(0, 2, 1, 3), kp], axis=2)
        v_all = jnp.concatenate([pv.transpose(0, 2, 1, 3), vp], axis=2)
        ki_all = jnp.concatenate([pik, kip3], axis=1)
    else:
        k_all, v_all, ki_all = kp, vp, kip3
    k_all = _pad_axis(k_all, 2, lk)
    v_all = _pad_axis(v_all, 2, lk)
    ki_all = _pad_axis(ki_all, 1, lk)
    vt_all = v_all.reshape(bsz, N_KV_HEADS, lk // KEY_BLOCK, KEY_BLOCK, LANES).transpose(0, 1, 2, 4, 3)
    tq = KEY_TILE
    seq_q = -(-seq // tq) * tq
    q_p, qi_p, wt_p = _pad_axis(q, 2, seq_q), _pad_axis(qi, 2, seq_q), _pad_axis(wt, 2, seq_q)
    kf = k_all[..., :HEAD_DIM].astype(F32)
    kmax = jnp.sqrt(jnp.max(jnp.sum(kf * kf, axis=-1), axis=-1)).reshape(-1)
    bfar = bias_tiles[-1, :, 0, 0]
    bmax = jnp.max(jnp.abs(bias_tiles), axis=(0, 2, 3))
    stats = jnp.concatenate([kmax, bmax, bfar]).astype(F32)
    bias_tiles = bias_tiles - bfar[None, :, None, None]
    attn = _dsa(stats, q_p, qi_p, wt_p, k_all, vt_all, ki_all, bias_tiles, bsz, seq_q, past, n_keys, tq)
    attn = attn[:, :seq]

    moe_tile = _pick_tile(t, MOE_TILE)
    tm2 = _pick_tile(moe_tile, 512)
    x1, h2, gt, rt, cnt = _merge(x.reshape(t, d), ya, attn.reshape(t, attn.shape[-1]), sga, sgb, mw, tm2, moe_tile)
    cnt_i = cnt[:, :, 0].astype(I32).reshape(-1)
    y = _moe(h2, x1, gt, rt, cnt_i, ew, moe_tile)

    k_new = kc.reshape(bsz, seq, N_KV_HEADS, HEAD_DIM)
    v_new = vc.reshape(bsz, seq, N_KV_HEADS, HEAD_DIM)
    ik_new = kic.reshape(bsz, seq, IDX_DIM)
    return y.reshape(bsz, seq, d), k_new, v_new, ik_new, s_re, s_im


def kernel(x_prompt, x_sample, cache_k, cache_v, cache_idx_k, state_ssm_re, state_ssm_im, rel_bias, norm1_g, w_in, ssm_lambda_re, ssm_lambda_im, ssm_log_dt, ssm_b_re, ssm_b_im, ssm_c_re, ssm_c_im, ssm_d, ssm_w_glu_a, ssm_w_glu_b, q_norm_g, k_norm_g, idx_k_norm_g, idx_k_norm_b, w_attn_up, w_out, norm2_g, moe_w_router, moe_b_router, moe_w_gate, moe_b_gate, moe_w_up, moe_b_up, moe_w_down, moe_b_down):
    depth = w_in.shape[0]
    d_model = x_prompt.shape[-1]
    bias_tiles = _bias_tiles(rel_bias)
    xp, xs = x_prompt, x_sample
    st_p, st_s = [], []
    for l in range(depth):
        pw = _prep_proj(norm1_g[l], w_in[l], q_norm_g[l], k_norm_g[l], idx_k_norm_g[l], idx_k_norm_b[l], d_model)
        sw = _prep_s5(ssm_lambda_re[l], ssm_lambda_im[l], ssm_log_dt[l], ssm_b_re[l], ssm_b_im[l], ssm_c_re[l],
                      ssm_c_im[l], ssm_d[l], ssm_w_glu_a[l], ssm_w_glu_b[l])
        mw = _prep_merge(w_attn_up[l], w_out[l], norm2_g[l], moe_w_router[l], moe_b_router[l])
        ew = _prep_moe(moe_w_gate[l], moe_b_gate[l], moe_w_up[l], moe_b_up[l], moe_w_down[l], moe_b_down[l])
        xp, *sp = _trunk_layer(xp, None, None, None, None, None, pw, sw, mw, ew, bias_tiles)
        xs, *ss = _trunk_layer(xs, cache_k[l], cache_v[l], cache_idx_k[l], state_ssm_re[l], state_ssm_im[l],
                               pw, sw, mw, ew, bias_tiles)
        st_p.append(sp)
        st_s.append(ss)
    outs_p = [jnp.stack([s[i] for s in st_p]) for i in range(5)]
    outs_s = [jnp.stack([s[i] for s in st_s]) for i in range(5)]
    return (xp, xs, *outs_p, *outs_s)
```

```python
import functools
import math

import numpy as np
import jax
import jax.numpy as jnp
from jax import lax
from jax.experimental import pallas as pl
from jax.experimental.pallas import tpu as pltpu

F32 = jnp.float32
BF16 = jnp.bfloat16
I32 = jnp.int32

LANES = 128
SUBLANES = 8
VMEM_LIMIT = 56 * 1024 * 1024

CHUNK = 64
SSM_GROUP_CH = 16
SSM_STATE = 64
N_HEADS = 8
HEAD_DIM = 64
N_KV_HEADS = 2
KV_REP = N_HEADS // N_KV_HEADS
IDX_HEADS = 8
IDX_DIM = 64
TOPK_MAX = 256
REL_BUCKETS = 32
REL_MAX_DIST = 1024
N_EXPERTS = 32
TOP_K = 4
SWIGLU_LIMIT = 7.0
SWIGLU_ALPHA = 1.702
EPS = 1e-6

KEY_TILE = 128
KEY_BLOCK = 256
SCORE_CHUNK = 4
LOG2E = math.log2(math.e)
NEG_BIG = -1e30
SHIFT_LIMIT = 30.0
S5_DIAG = 2
SEARCH_FIRST = 8
SEARCH_GROUP = 4
MOE_TILE = 1024
MOE_ROWS = 160
MOE_GROUP = 8
MOE_PAIR = 2


def _cp(sem):
    return pltpu.CompilerParams(dimension_semantics=sem, vmem_limit_bytes=VMEM_LIMIT)


def _dot(a, b):
    return jnp.dot(a, b, preferred_element_type=F32)


def _dot_nt(a, b):
    return lax.dot_general(a, b, (((1,), (1,)), ((), ())), preferred_element_type=F32)


def _dot_tn(a, b):
    return lax.dot_general(a, b, (((0,), (0,)), ((), ())), preferred_element_type=F32)


def _split(a):
    hi = a.astype(BF16)
    lo = (a - hi.astype(F32)).astype(BF16)
    return hi, lo


def _dot_split(a, g):
    hi, lo = _split(a)
    return _dot(hi, g) + _dot(lo, g)


def _proj_kernel(x_ref, g1_ref, wu_ref, wq_ref, wk_ref, wv_ref, wqi_ref, wki_ref, wwit_ref, wga_ref, wgb_ref,
                 wkc_ref, wvc_ref, gq_ref, gk_ref, gkc_ref, gi_ref, bi_ref, ones_h_ref, ones_c_ref,
                 u_ref, q_ref, kp_ref, vp_ref, qi_ref, kip_ref, wt_ref, sga_ref, sgb_ref,
                 kc_ref, vc_ref, kic_ref):
    x = x_ref[...]
    ms = jnp.mean(x * x, axis=-1, keepdims=True)
    hn = (x * lax.rsqrt(ms + EPS) * g1_ref[...]).astype(BF16)
    ones_h = ones_h_ref[...]
    lane = lax.broadcasted_iota(I32, (x.shape[0], LANES), 1)

    u_ref[...] = _dot(hn, wu_ref[...]).astype(BF16)

    q = _dot(hn, wq_ref[...])
    scale = HEAD_DIM ** -0.5 * LOG2E
    for h in range(N_HEADS):
        qh = q[:, h * LANES:(h + 1) * LANES]
        msq = _dot_split(qh * qh, ones_h)
        q_ref[h] = (qh * lax.rsqrt(msq + EPS) * (gq_ref[...] * scale)).astype(BF16)

    k = _dot(hn, wk_ref[...])
    for g in range(N_KV_HEADS):
        kg = k[:, g * LANES:(g + 1) * LANES]
        msk = _dot_split(kg * kg, ones_h)
        kn = kg * lax.rsqrt(msk + EPS) * gk_ref[...]
        kp_ref[g] = jnp.where(lane == HEAD_DIM, 1.0, kn).astype(BF16)

    v = _dot(hn, wv_ref[...])
    for g in range(N_KV_HEADS):
        vg = v[:, g * LANES:(g + 1) * LANES]
        vp_ref[g] = jnp.where(lane == HEAD_DIM, 1.0, vg).astype(BF16)

    qi = _dot(hn, wqi_ref[...])
    for h in range(IDX_HEADS):
        qi_ref[h] = qi[:, h * LANES:(h + 1) * LANES].astype(BF16)

    ki = _dot(hn, wki_ref[...])
    mu = _dot_split(ki, ones_h)
    xc = jnp.where(lane < IDX_DIM, ki - mu, 0.0)
    var = _dot_split(xc * xc, ones_h)
    kin = xc * lax.rsqrt(var + EPS) * gi_ref[...] + bi_ref[...]
    kip_ref[...] = kin.astype(BF16)
    kic_ref[...] = kin[:, :IDX_DIM]

    wt = _dot_nt(wwit_ref[...], hn)
    wt_ref[...] = wt[0:IDX_HEADS, :] * (IDX_HEADS ** -0.5 * IDX_DIM ** -0.5)

    sga_ref[...] = jax.nn.sigmoid(_dot(hn, wga_ref[...])).astype(BF16)
    sgb_ref[...] = jax.nn.sigmoid(_dot(hn, wgb_ref[...])).astype(BF16)

    kc = _dot(hn, wkc_ref[...])
    mskc = _dot_split(kc * kc, ones_c_ref[...])
    kc_ref[...] = kc * lax.rsqrt(mskc + EPS) * gkc_ref[...]
    vc_ref[...] = _dot(hn, wvc_ref[...])


def _proj(x, pw, bsz, seq, tm):
    d = x.shape[-1]
    nt = seq // tm
    t = bsz * seq
    x2 = x.reshape(t, d)

    def tok(b, i):
        return (b * nt + i, 0)

    def cst(b, i):
        return (0, 0)

    def wspec(a):
        return pl.BlockSpec(a.shape, cst)

    weights = [pw['g1'], pw['wu'], pw['wq'], pw['wk'], pw['wv'], pw['wqi'], pw['wki'], pw['wwit'], pw['wga'],
               pw['wgb'], pw['wkc'], pw['wvc'], pw['gq'], pw['gk'], pw['gkc'], pw['gi'], pw['bi'],
               pw['ones_h'], pw['ones_c']]
    ssm_w = pw['wu'].shape[1]
    out_shape = (
        jax.ShapeDtypeStruct((seq, bsz * ssm_w), BF16),
        jax.ShapeDtypeStruct((bsz, N_HEADS, seq, LANES), BF16),
        jax.ShapeDtypeStruct((bsz, N_KV_HEADS, seq, LANES), BF16),
        jax.ShapeDtypeStruct((bsz, N_KV_HEADS, seq, LANES), BF16),
        jax.ShapeDtypeStruct((bsz, IDX_HEADS, seq, LANES), BF16),
        jax.ShapeDtypeStruct((t, LANES), BF16),
        jax.ShapeDtypeStruct((bsz, IDX_HEADS, seq), F32),
        jax.ShapeDtypeStruct((t, d), BF16),
        jax.ShapeDtypeStruct((t, d), BF16),
        jax.ShapeDtypeStruct((t, N_KV_HEADS * HEAD_DIM), F32),
        jax.ShapeDtypeStruct((t, N_KV_HEADS * HEAD_DIM), F32),
        jax.ShapeDtypeStruct((t, IDX_DIM), F32),
    )

    def hm(nh):
        return pl.BlockSpec((None, nh, tm, LANES), lambda b, i: (b, 0, i, 0))

    out_specs = (
        pl.BlockSpec((tm, ssm_w), lambda b, i: (i, b)),
        hm(N_HEADS), hm(N_KV_HEADS), hm(N_KV_HEADS), hm(IDX_HEADS),
        pl.BlockSpec((tm, LANES), tok),
        pl.BlockSpec((None, IDX_HEADS, tm), lambda b, i: (b, 0, i)),
        pl.BlockSpec((tm, d), tok), pl.BlockSpec((tm, d), tok),
        pl.BlockSpec((tm, N_KV_HEADS * HEAD_DIM), tok), pl.BlockSpec((tm, N_KV_HEADS * HEAD_DIM), tok),
        pl.BlockSpec((tm, IDX_DIM), tok),
    )
    return pl.pallas_call(
        _proj_kernel,
        grid=(bsz, nt),
        in_specs=[pl.BlockSpec((tm, d), tok)] + [wspec(a) for a in weights],
        out_specs=out_specs,
        out_shape=out_shape,
        compiler_params=_cp(("arbitrary", "arbitrary")),
        name="proj",
    )(x2, *weights)


def _gelu_tanh(x):
    return 0.5 * x * (1.0 + jnp.tanh(math.sqrt(2.0 / math.pi) * (x + 0.044715 * (x * x * x))))


def _s5_kernel(u_ref, h0_ref, bre_ref, bim_ref, are_ref, aim_ref, cre_ref, cim_ref, dvec_ref, wa_ref, wb_ref,
               ya_ref, hout_ref, state_ref, bu_ref, yf_ref, *, bsz, tc, strip):
    s = pl.program_id(0)
    half = are_ref.shape[1]

    @pl.when(s == 0)
    def _():
        state_ref[...] = h0_ref[...]

    u = u_ref[...]
    cw = u.shape[1] // S5_DIAG
    sw = half // S5_DIAG
    for j in range(S5_DIAG):
        uj = u[:, j * cw:(j + 1) * cw]
        bu_ref[:, j * sw:(j + 1) * sw] = _dot(uj, bre_ref[j])
        bu_ref[:, half + j * sw:half + (j + 1) * sw] = _dot(uj, bim_ref[j])

    for c0 in range(0, half, strip):
        ar = jnp.broadcast_to(are_ref[:, c0:c0 + strip], (bsz, strip))
        ai = jnp.broadcast_to(aim_ref[:, c0:c0 + strip], (bsz, strip))
        hr0 = state_ref[:, c0:c0 + strip]
        hi0 = state_ref[:, half + c0:half + c0 + strip]

        def step(t, carry):
            hr, hi = carry
            r0 = pl.multiple_of(t * bsz, bsz)
            br = bu_ref[pl.ds(r0, bsz), c0:c0 + strip]
            bi = bu_ref[pl.ds(r0, bsz), half + c0:half + c0 + strip]
            nr = ar * hr - ai * hi + br
            ni = ar * hi + ai * hr + bi
            bu_ref[pl.ds(r0, bsz), c0:c0 + strip] = nr
            bu_ref[pl.ds(r0, bsz), half + c0:half + c0 + strip] = ni
            return nr, ni

        hr, hi = lax.fori_loop(0, tc, step, (hr0, hi0))
        state_ref[:, c0:c0 + strip] = hr
        state_ref[:, half + c0:half + c0 + strip] = hi

    ys = []
    for j in range(S5_DIAG):
        s_re = bu_ref[:, j * sw:(j + 1) * sw].astype(BF16)
        s_im = bu_ref[:, half + j * sw:half + (j + 1) * sw].astype(BF16)
        ys.append(_dot(s_re, cre_ref[j]) + _dot(s_im, cim_ref[j]))
    y = jnp.concatenate(ys, axis=1) + dvec_ref[...] * u.astype(F32)
    g = _gelu_tanh(y).astype(BF16)
    ya = _dot(g, wa_ref[...]) * jax.nn.sigmoid(_dot(g, wb_ref[...]))
    n_chunk = ya.shape[1] // LANES
    for c in range(n_chunk):
        yf_ref[c] = ya[:, c * LANES:(c + 1) * LANES]
    for b in range(bsz):
        ya_ref[b] = jnp.concatenate([yf_ref[c, pl.ds(b, tc, stride=bsz), :] for c in range(n_chunk)],
                                    axis=1).astype(BF16)

    @pl.when(s == pl.num_programs(0) - 1)
    def _():
        hout_ref[...] = state_ref[...]


def _s5(u_tb, h0, sw, bsz, seq, tc):
    rows = tc * bsz
    ssm_w = sw['d'].shape[1]
    half = sw['a_re'].shape[1]
    two_half = 2 * half
    d = sw['wa'].shape[1]
    u2 = u_tb.reshape(seq * bsz, ssm_w)
    strip = min(512, half)

    consts = [h0, sw['b_re'], sw['b_im'], sw['a_re'], sw['a_im'], sw['c_re'], sw['c_im'], sw['d'], sw['wa'], sw['wb']]

    def cst(s):
        return (0, 0)

    def cspec(a):
        return pl.BlockSpec(a.shape, lambda s: (0,) * a.ndim)

    ya, hout = pl.pallas_call(
        functools.partial(_s5_kernel, bsz=bsz, tc=tc, strip=strip),
        grid=(seq // tc,),
        in_specs=[pl.BlockSpec((rows, ssm_w), lambda s: (s, 0))] + [cspec(a) for a in consts],
        out_specs=(pl.BlockSpec((bsz, tc, d), lambda s: (0, s, 0)), pl.BlockSpec((bsz, two_half), cst)),
        out_shape=(jax.ShapeDtypeStruct((bsz, seq, d), BF16), jax.ShapeDtypeStruct((bsz, two_half), F32)),
        scratch_shapes=[pltpu.VMEM((bsz, two_half), F32), pltpu.VMEM((rows, two_half), F32),
                        pltpu.VMEM((d // LANES, rows, LANES), F32)],
        compiler_params=_cp(("arbitrary",)),
        name="s5",
    )(u2, *consts)
    return ya.reshape(bsz * seq, d), hout


def _f2key(x):
    b = lax.bitcast_convert_type(x, I32)
    return b ^ ((b >> 31) & 0x7FFFFFFF)


def _key2f(k):
    return lax.bitcast_convert_type(k ^ ((k >> 31) & 0x7FFFFFFF), F32)


def _dsa_kernel(st_ref, q_ref, qi_ref, wt_ref, k_ref, vt_ref, ki_ref, bias_ref, o_ref,
                s_ref, lo_ref, hi_ref, clo_ref, glo_ref, ghi_ref, side_ref, q2_ref, mrow_ref, acc_ref,
                *, bsz, tq, past, n_keys, topk, nkt, nd, idx_bits):
    b_id = pl.program_id(0)
    i = pl.program_id(1)
    kb = KEY_BLOCK
    sl = SUBLANES
    q0 = past + i * tq
    last_chunk = (q0 + tq - 1) // CHUNK
    n_kt = jnp.minimum(nkt, ((last_chunk + 1) * CHUNK + kb - 1) // kb)
    d0 = q0 // KEY_TILE

    krow = lax.broadcasted_iota(I32, (kb, tq), 0)
    q_chunk = (q0 + lax.broadcasted_iota(I32, (kb, tq), 1)) // CHUNK
    qc8 = (q0 + lax.broadcasted_iota(I32, (sl, tq), 1)) // CHUNK
    n_adm = jnp.minimum((qc8 + 1) * CHUNK, n_keys)
    n_admf = n_adm.astype(F32)
    is_pad = q0 + lax.broadcasted_iota(I32, (sl, tq), 1) >= n_keys
    needf = jnp.where(is_pad, n_adm, jnp.minimum(topk, n_adm)).astype(F32)

    def bcast(x):
        return jnp.broadcast_to(x[0:1, :], (kb, tq))

    def rep(x):
        return jnp.broadcast_to(x, (sl, tq))

    qi = qi_ref[...].reshape(IDX_HEADS * tq, LANES)

    def score_blocks(kt, nb, masked):
        k0 = pl.multiple_of(kt * kb, kb)
        s = _dot_nt(ki_ref[pl.ds(k0, nb * kb), :], qi)
        for j in range(nb):
            sc = jnp.zeros((kb, tq), F32)
            for h in range(IDX_HEADS):
                sc = sc + wt_ref[h:h + 1, :] * jnp.maximum(s[j * kb:(j + 1) * kb, h * tq:(h + 1) * tq], 0.0)
            if masked:
                kpos = k0 + j * kb + krow
                adm = ((kpos // CHUNK) <= q_chunk) & (kpos < n_keys)
                sc = jnp.where(adm, sc, -jnp.inf)
            s_ref[kt + j] = sc

    n_open = n_kt - 1

    def score_chunk(j, c):
        score_blocks(SCORE_CHUNK * j, SCORE_CHUNK, False)
        return c

    lax.fori_loop(0, n_open // SCORE_CHUNK, score_chunk, 0)

    def score_single(j, c):
        score_blocks(j, 1, False)
        return c

    lax.fori_loop(n_open - n_open % SCORE_CHUNK, n_open, score_single, 0)
    score_blocks(n_kt - 1, 1, True)

    part = 4 * sl

    def fold(x, op):
        x = x.reshape(kb // part, part, tq)
        acc = x[0]
        for j in range(1, kb // part):
            acc = op(acc, x[j])
        return acc

    def count(pred):
        def one(kt):
            return fold(jnp.where(pred(s_ref[kt], kt), 1.0, 0.0), jnp.add)
        c = lax.fori_loop(0, n_kt // 2, lambda j, c: c + (one(2 * j) + one(2 * j + 1)), jnp.zeros((part, tq), F32))
        c = c + lax.cond(n_kt % 2 == 1, lambda: one(n_kt - 1), lambda: jnp.zeros((part, tq), F32))
        return rep(jnp.sum(c, axis=0, keepdims=True))

    def minmax(kt, c):
        mx, mn = c
        s = s_ref[kt]
        return (jnp.maximum(mx, fold(s, jnp.maximum)),
                jnp.minimum(mn, fold(jnp.where(s == -jnp.inf, jnp.inf, s), jnp.minimum)))

    mx, mn = lax.fori_loop(0, n_kt, minmax,
                           (jnp.full((part, tq), -jnp.inf, F32), jnp.full((part, tq), jnp.inf, F32)))
    lo_ref[...] = rep(jnp.min(mn, axis=0, keepdims=True))
    hi_ref[...] = _key2f(_f2key(rep(jnp.max(mx, axis=0, keepdims=True))) + 1)
    def odds(cnt):
        c = jnp.clip(cnt, 0.5, n_admf - 0.5)
        return jnp.log((n_admf - c) / c)

    target = odds(needf - 0.5)
    clo_ref[...] = n_admf
    glo_ref[...] = target - odds(n_admf)
    ghi_ref[...] = target - odds(jnp.zeros((sl, tq), F32))
    side_ref[...] = jnp.zeros((sl, tq), F32)

    def searching(lo, hi, clo):
        return (_f2key(hi) > _f2key(lo) + 1) & (clo > needf)

    def refine(it, c):
        lo, hi, clo = lo_ref[...], hi_ref[...], clo_ref[...]
        glo, ghi, side = glo_ref[...], ghi_ref[...], side_ref[...]
        k_t = _f2key(lo + (hi - lo) * (glo / (glo - ghi)))
        t = _key2f(jnp.minimum(jnp.maximum(k_t, _f2key(lo) + 1), _f2key(hi) - 1))
        tb = bcast(t)
        cnt = count(lambda s, kt: s >= tb)
        g = target - odds(cnt)
        open_ = searching(lo, hi, clo)
        up = open_ & (cnt >= needf)
        dn = open_ & (cnt < needf)
        lo_ref[...] = jnp.where(up, t, lo)
        clo_ref[...] = jnp.where(up, cnt, clo)
        hi_ref[...] = jnp.where(dn, t, hi)
        glo_ref[...] = jnp.where(up, g, jnp.where(dn & (side < 0.0), glo * 0.5, glo))
        ghi_ref[...] = jnp.where(dn, g, jnp.where(up & (side > 0.0), ghi * 0.5, ghi))
        side_ref[...] = jnp.where(up, 1.0, jnp.where(dn, -1.0, side))
        return c

    def snap():
        lo, hi, clo = lo_ref[...], hi_ref[...], clo_ref[...]
        lo_b, hi_b = bcast(lo), bcast(hi)

        def body(kt, c):
            a, b = c
            s = s_ref[kt]
            return (jnp.minimum(a, fold(jnp.where(s >= lo_b, s, jnp.inf), jnp.minimum)),
                    jnp.maximum(b, fold(jnp.where(s < hi_b, s, -jnp.inf), jnp.maximum)))

        a, b = lax.fori_loop(0, n_kt, body,
                             (jnp.full((part, tq), jnp.inf, F32), jnp.full((part, tq), -jnp.inf, F32)))
        open_ = searching(lo, hi, clo)
        lo_ref[...] = jnp.where(open_, rep(jnp.min(a, axis=0, keepdims=True)), lo)
        hi_ref[...] = jnp.where(open_, _key2f(_f2key(rep(jnp.max(b, axis=0, keepdims=True))) + 1), hi)

    def n_searching():
        return jnp.max(jnp.where(searching(lo_ref[...], hi_ref[...], clo_ref[...]), 1.0, 0.0))

    def group(c):
        grp, _ = c
        lax.fori_loop(0, jnp.where(grp == 0, SEARCH_FIRST, SEARCH_GROUP), refine, 0)
        snap()
        return grp + 1, n_searching()

    lax.while_loop(lambda c: c[1] > 0.0, group, (jnp.int32(0), n_searching()))
    thr = lo_ref[...]
    thr_b = bcast(thr)

    n_tied = jnp.max(jnp.where(clo_ref[...] > needf, 1.0, 0.0))

    @pl.when(n_tied > 0.0)
    def _():
        rem = needf - count(lambda s, kt: s > thr_b)
        lo_ref[...] = jnp.zeros((sl, tq), F32)
        hi_ref[...] = jnp.full((sl, tq), float(nkt * kb), F32)
        krowf = krow.astype(F32)

        def bisect_idx(it, c):
            lo = lo_ref[...]
            hi = hi_ref[...]
            mid = jnp.floor((lo + hi) * 0.5)
            mid_b = bcast(mid)
            ok = count(lambda s, kt: (s == thr_b) & ((kt * kb).astype(F32) + krowf < mid_b)) >= rem
            hi_ref[...] = jnp.where(ok, mid, hi)
            lo_ref[...] = jnp.where(ok, lo, mid)
            return c

        lax.fori_loop(0, idx_bits, bisect_idx, 0)
        cut_b = bcast(hi_ref[...])

        def drop(kt, c):
            s = s_ref[kt]
            s_ref[kt] = jnp.where((s == thr_b) & ((kt * kb).astype(F32) + krowf >= cut_b), -jnp.inf, s)
            return c

        lax.fori_loop(0, n_kt, drop, 0)

    rows_g = KV_REP * tq
    qf = q_ref[...].reshape(N_HEADS * tq, LANES).astype(F32)
    qn = jnp.sqrt(jnp.sum(qf * qf, axis=1, keepdims=True))
    lane = lax.broadcasted_iota(I32, (tq, LANES), 1)
    worst = jnp.float32(0.0)
    for h in range(N_HEADS):
        kmax = st_ref[b_id * N_KV_HEADS + h // KV_REP]
        bmax = st_ref[bsz * N_KV_HEADS + h]
        bfar = st_ref[bsz * N_KV_HEADS + N_HEADS + h]
        bound = qn[h * tq:(h + 1) * tq, :] * (kmax * 1.01) + (bmax + 0.1)
        worst = jnp.maximum(worst, jnp.max(bound))
        q2_ref[h * tq:(h + 1) * tq, :] = jnp.where(lane == HEAD_DIM, bfar - bound,
                                                   qf[h * tq:(h + 1) * tq, :]).astype(BF16)
    n_far = jnp.clip((d0 - nd) // 2 + 1, 0, n_kt)

    def logits(kt, g, near, exact):
        k0 = pl.multiple_of(kt * kb, kb)
        maskadd = jnp.where(s_ref[kt] >= thr_b, 0.0, NEG_BIG)
        s = _dot_nt(k_ref[g, pl.ds(k0, kb), :], q2_ref[g * rows_g:(g + 1) * rows_g, :])
        if near:
            da = jnp.clip(d0 - 2 * kt, 0, nd - 1)
            db = jnp.clip(d0 - 2 * kt - 1, 0, nd - 1)
        parts = []
        for r in range(KV_REP):
            h = g * KV_REP + r
            add = maskadd - mrow_ref[0:1, h * tq:(h + 1) * tq] if exact else maskadd
            if near:
                add = jnp.concatenate([bias_ref[da, h], bias_ref[db, h]], axis=0) + add
            parts.append(s[:, r * tq:(r + 1) * tq] + add)
        return jnp.concatenate(parts, axis=1)

    def over_blocks(fn):
        def far_pair(j, c):
            fn((2 * j, 2 * j + 1), False)
            return c
        lax.fori_loop(0, n_far // 2, far_pair, 0)

        @pl.when(n_far % 2 == 1)
        def _():
            fn((n_far - 1,), False)

        n_near = n_kt - n_far

        def near_pair(j, c):
            fn((n_far + 2 * j, n_far + 2 * j + 1), True)
            return c
        lax.fori_loop(0, n_near // 2, near_pair, 0)

        @pl.when(n_near % 2 == 1)
        def _():
            fn((n_kt - 1,), True)

    def attend(exact):
        acc_ref[...] = jnp.zeros(acc_ref.shape, F32)

        def blocks(kts, near):
            for g in range(N_KV_HEADS):
                pv = None
                for kt in kts:
                    p = jnp.exp2(logits(kt, g, near, exact)).astype(BF16)
                    d = _dot(vt_ref[g, kt], p)
                    pv = d if pv is None else pv + d
                acc_ref[g] += pv
        over_blocks(blocks)

    @pl.when(worst <= SHIFT_LIMIT)
    def _():
        attend(False)

    @pl.when(worst > SHIFT_LIMIT)
    def _():
        mrow_ref[...] = jnp.full(mrow_ref.shape, NEG_BIG, F32)

        def blocks(kts, near):
            for g in range(N_KV_HEADS):
                for kt in kts:
                    mx = jnp.max(logits(kt, g, near, False), axis=0, keepdims=True)
                    cur = mrow_ref[:, g * rows_g:(g + 1) * rows_g]
                    mrow_ref[:, g * rows_g:(g + 1) * rows_g] = jnp.maximum(cur, jnp.broadcast_to(mx, (sl, rows_g)))
        over_blocks(blocks)
        attend(True)

    for g in range(N_KV_HEADS):
        acc = acc_ref[g]
        og = acc / acc[HEAD_DIM:HEAD_DIM + 1, :]
        for r in range(KV_REP):
            h = g * KV_REP + r
            o_ref[:, h * LANES:(h + 1) * LANES] = og[:, r * tq:(r + 1) * tq].T.astype(BF16)


def _dsa(stats, q, qi, wt, k_all, vt_all, ki_all, bias_tiles, bsz, seq, past, n_keys, tq):
    lk = k_all.shape[2]
    nkt = lk // KEY_BLOCK
    topk = min(TOPK_MAX, n_keys // 4)
    nd = bias_tiles.shape[0]
    nq = seq // tq
    assert past % KEY_TILE == 0 and tq == KEY_TILE
    idx_bits = int(math.ceil(math.log2(lk))) + 1
    kern = functools.partial(_dsa_kernel, bsz=bsz, tq=tq, past=past, n_keys=n_keys, topk=topk, nkt=nkt, nd=nd,
                             idx_bits=idx_bits)
    row_state = pltpu.VMEM((SUBLANES, tq), F32)
    grid_spec = pltpu.PrefetchScalarGridSpec(
        num_scalar_prefetch=1,
        grid=(bsz, nq),
        in_specs=[
            pl.BlockSpec((None, N_HEADS, tq, LANES), lambda b, i, s: (b, 0, i, 0)),
            pl.BlockSpec((None, IDX_HEADS, tq, LANES), lambda b, i, s: (b, 0, i, 0)),
            pl.BlockSpec((None, IDX_HEADS, tq), lambda b, i, s: (b, 0, i)),
            pl.BlockSpec((None, N_KV_HEADS, lk, LANES), lambda b, i, s: (b, 0, 0, 0)),
            pl.BlockSpec((None, N_KV_HEADS, nkt, LANES, KEY_BLOCK), lambda b, i, s: (b, 0, 0, 0, 0)),
            pl.BlockSpec((None, lk, LANES), lambda b, i, s: (b, 0, 0)),
            pl.BlockSpec(bias_tiles.shape, lambda b, i, s: (0, 0, 0, 0)),
        ],
        out_specs=pl.BlockSpec((None, tq, N_HEADS * LANES), lambda b, i, s: (b, i, 0)),
        scratch_shapes=[
            pltpu.VMEM((nkt, KEY_BLOCK, tq), F32),
            row_state, row_state, row_state, row_state, row_state, row_state,
            pltpu.VMEM((N_HEADS * tq, LANES), BF16),
            pltpu.VMEM((SUBLANES, N_HEADS * tq), F32),
            pltpu.VMEM((N_KV_HEADS, LANES, KV_REP * tq), F32),
        ],
    )
    return pl.pallas_call(
        kern,
        grid_spec=grid_spec,
        out_shape=jax.ShapeDtypeStruct((bsz, seq, N_HEADS * LANES), BF16),
        compiler_params=_cp(("arbitrary", "arbitrary")),
        name="dsa",
    )(stats, q, qi, wt, k_all, vt_all, ki_all, bias_tiles)


def _merge_kernel(x_ref, ya_ref, at_ref, sga_ref, sgb_ref, wup_ref, wout_ref, g2_ref, wr_hi_ref, wr_lo_ref, br_ref,
                  x1_ref, h2_ref, gt_ref, rt_ref, cnt_ref, run_ref, *, tm, sub):
    step = pl.program_id(0)

    @pl.when(step % sub == 0)
    def _():
        run_ref[...] = jnp.zeros(run_ref.shape, F32)

    yb = _dot(at_ref[...], wup_ref[...])
    merged = sga_ref[...].astype(F32) * ya_ref[...].astype(F32) + sgb_ref[...].astype(F32) * yb
    x1 = x_ref[...] + _dot(merged.astype(BF16), wout_ref[...])
    x1_ref[...] = x1
    ms = jnp.mean(x1 * x1, axis=-1, keepdims=True)
    h2 = x1 * lax.rsqrt(ms + EPS) * g2_ref[...]
    h2_hi, h2_lo = _split(h2)
    h2_ref[...] = h2_hi

    wr_hi = wr_hi_ref[...]
    logit = (_dot_nt(wr_hi, h2_hi) + _dot_nt(wr_hi, h2_lo) + _dot_nt(wr_lo_ref[...], h2_hi)) + br_ref[:, 0:1]
    ne = logit.shape[0]
    eid = lax.broadcasted_iota(I32, (ne, tm), 0).astype(F32)
    selb = jnp.zeros((ne, tm), F32)
    tops = []
    picks = []
    for _ in range(TOP_K):
        mx = jnp.max(logit, axis=0, keepdims=True)
        pick = jnp.min(jnp.where(logit == mx, eid, float(ne)), axis=0, keepdims=True)
        hit = eid == pick
        selb = jnp.where(hit, 1.0, selb)
        logit = jnp.where(hit, -jnp.inf, logit)
        tops.append(mx)
        picks.append(hit)
    ex = [jnp.exp(t - tops[0]) for t in tops]
    den = ex[0] + ex[1] + ex[2] + ex[3]
    gate = jnp.zeros((ne, tm), F32)
    for hit, e in zip(picks, ex):
        gate = jnp.where(hit, e / den, gate)
    gt_ref[...] = gate

    sel = selb > 0.5
    selb = selb.astype(BF16)
    r_i = lax.broadcasted_iota(I32, (tm, tm), 0)
    c_i = lax.broadcasted_iota(I32, (tm, tm), 1)
    tri = jnp.where(r_i < c_i, 1.0, 0.0).astype(BF16)
    run = run_ref[...]
    rank = _dot(selb, tri) + jnp.broadcast_to(run[:, 0:1], (ne, tm))
    rt_ref[...] = jnp.where(sel, rank, -1.0)
    run = run + _dot(selb, jnp.ones((tm, LANES), BF16))
    run_ref[...] = run
    cnt_ref[...] = run


def _merge(x2, ya, attn, sga, sgb, mw, tm, moe_tile):
    t, d = x2.shape
    sub = moe_tile // tm
    ne = mw['wr_hi'].shape[0]

    def tok(i):
        return (i, 0)

    def cst(i):
        return (0, 0)

    consts = [mw['wup'], mw['wout'], mw['g2'], mw['wr_hi'], mw['wr_lo'], mw['br']]
    return pl.pallas_call(
        functools.partial(_merge_kernel, tm=tm, sub=sub),
        grid=(t // tm,),
        in_specs=[
            pl.BlockSpec((tm, d), tok),
            pl.BlockSpec((tm, d), tok),
            pl.BlockSpec((tm, attn.shape[-1]), tok),
            pl.BlockSpec((tm, d), tok),
            pl.BlockSpec((tm, d), tok),
        ] + [pl.BlockSpec(a.shape, cst) for a in consts],
        out_specs=(
            pl.BlockSpec((tm, d), tok),
            pl.BlockSpec((tm, d), tok),
            pl.BlockSpec((ne, tm), lambda i: (0, i)),
            pl.BlockSpec((ne, tm), lambda i: (0, i)),
            pl.BlockSpec((None, ne, LANES), lambda i: (i // sub, 0, 0)),
        ),
        out_shape=(
            jax.ShapeDtypeStruct((t, d), F32),
            jax.ShapeDtypeStruct((t, d), BF16),
            jax.ShapeDtypeStruct((ne, t), F32),
            jax.ShapeDtypeStruct((ne, t), F32),
            jax.ShapeDtypeStruct((t // moe_tile, ne, LANES), F32),
        ),
        scratch_shapes=[pltpu.VMEM((ne, LANES), F32)],
        compiler_params=_cp(("arbitrary",)),
        name="merge",
    )(x2, ya, attn, sga, sgb, *consts)


def _moe_kernel(cnt_ref, h2_ref, x1_hbm, gt_ref, rt_ref, wg_ref, wu_ref, wd_ref, bg_ref, bu_ref, bd_ref, y_ref,
                pg_ref, og_ref, sem, *, tt, pair):
    j = pl.program_id(0)
    e = pl.program_id(1)
    ne = pl.num_programs(1)
    rb = MOE_ROWS
    slot = e % MOE_GROUP

    @pl.when(e == 0)
    def _():
        cp = pltpu.make_async_copy(x1_hbm.at[pl.ds(pl.multiple_of(j * pair * tt, tt), pair * tt), :], y_ref, sem)
        cp.start()
        cp.wait()

    mine = lax.broadcasted_iota(I32, (SUBLANES, pair * tt), 0) == e % SUBLANES
    g_all = jnp.sum(jnp.where(mine, gt_ref[...], 0.0), axis=0, keepdims=True)
    r_all = jnp.sum(jnp.where(mine, rt_ref[...], 0.0), axis=0, keepdims=True)
    rid = lax.broadcasted_iota(I32, (rb, tt), 0).astype(F32)

    def one_hot(s, blk):
        return jnp.broadcast_to(r_all[:, s * tt:(s + 1) * tt], (rb, tt)) == (rid + (blk * rb).astype(F32))

    def gather(s, hit):
        p = jnp.where(hit, 1.0, 0.0).astype(BF16)
        return p, _dot(p, h2_ref[s * tt:(s + 1) * tt, :]).astype(BF16)

    def expert(xg):
        a = jnp.minimum(_dot(xg, wg_ref[0]) + bg_ref[0], SWIGLU_LIMIT)
        b = jnp.clip(_dot(xg, wu_ref[0]) + bu_ref[0], -SWIGLU_LIMIT, SWIGLU_LIMIT)
        hid = a * jax.nn.sigmoid(SWIGLU_ALPHA * a) * (b + 1.0)
        return _dot(hid.astype(BF16), wd_ref[0]) + bd_ref[0]

    def gated(s, hit, o):
        g_row = jnp.broadcast_to(g_all[:, s * tt:(s + 1) * tt], (rb, tt))
        return (o * jnp.sum(jnp.where(hit, g_row, 0.0), axis=1, keepdims=True)).astype(BF16)

    hits = [one_hot(s, jnp.int32(0)) for s in range(pair)]
    gathered = [gather(s, hits[s]) for s in range(pair)]
    o = expert(jnp.concatenate([xg for _, xg in gathered], axis=0))
    r0 = pl.multiple_of(slot * rb, rb)
    for s in range(pair):
        pg_ref[s, pl.ds(r0, rb), :] = gathered[s][0]
        og_ref[s, pl.ds(r0, rb), :] = gated(s, hits[s], o[s * rb:(s + 1) * rb])

    @pl.when(slot == MOE_GROUP - 1)
    def _():
        for s in range(pair):
            y_ref[s * tt:(s + 1) * tt, :] += _dot_tn(pg_ref[s], og_ref[s])

    for s in range(pair):
        def overflow(blk, c, s=s):
            hit = one_hot(s, blk)
            p, xg = gather(s, hit)
            y_ref[s * tt:(s + 1) * tt, :] += _dot_tn(p, gated(s, hit, expert(xg)))
            return c

        n_rows = cnt_ref[(j * pair + s) * ne + e]
        lax.fori_loop(1, (n_rows + rb - 1) // rb, overflow, 0)


def _moe(h2, x1, gt, rt, cnt, ew, tt):
    t, d = h2.shape
    ne = gt.shape[0]
    nt = t // tt
    f = ew['wg'].shape[-1]
    pair = MOE_PAIR if nt % MOE_PAIR == 0 else 1
    grid_spec = pltpu.PrefetchScalarGridSpec(
        num_scalar_prefetch=1,
        grid=(nt // pair, ne),
        in_specs=[
            pl.BlockSpec((pair * tt, d), lambda j, e, c: (j, 0)),
            pl.BlockSpec(memory_space=pl.ANY),
            pl.BlockSpec((SUBLANES, pair * tt), lambda j, e, c: (e // SUBLANES, j)),
            pl.BlockSpec((SUBLANES, pair * tt), lambda j, e, c: (e // SUBLANES, j)),
            pl.BlockSpec((1, d, f), lambda j, e, c: (e, 0, 0)),
            pl.BlockSpec((1, d, f), lambda j, e, c: (e, 0, 0)),
            pl.BlockSpec((1, f, d), lambda j, e, c: (e, 0, 0)),
            pl.BlockSpec((1, 1, f), lambda j, e, c: (e, 0, 0)),
            pl.BlockSpec((1, 1, f), lambda j, e, c: (e, 0, 0)),
            pl.BlockSpec((1, 1, d), lambda j, e, c: (e, 0, 0)),
        ],
        out_specs=pl.BlockSpec((pair * tt, d), lambda j, e, c: (j, 0)),
        scratch_shapes=[pltpu.VMEM((pair, MOE_GROUP * MOE_ROWS, tt), BF16),
                        pltpu.VMEM((pair, MOE_GROUP * MOE_ROWS, d), BF16),
                        pltpu.SemaphoreType.DMA(())],
    )
    assert ne % MOE_GROUP == 0
    return pl.pallas_call(
        functools.partial(_moe_kernel, tt=tt, pair=pair),
        grid_spec=grid_spec,
        out_shape=jax.ShapeDtypeStruct((t, d), F32),
        compiler_params=_cp(("arbitrary", "arbitrary")),
        name="moe",
    )(cnt, h2, x1, gt, rt, ew['wg'], ew['wu'], ew['wd'], ew['bg'], ew['bu'], ew['bd'])


def _pad_heads(wmat, n_heads, width):
    d = wmat.shape[0]
    w3 = wmat.reshape(d, n_heads, width)
    return jnp.pad(w3, ((0, 0), (0, 0), (0, LANES - width))).reshape(d, n_heads * LANES)


def _pad_lanes(v, width=LANES):
    v = v.reshape(1, -1)
    return jnp.pad(v, ((0, 0), (0, width - v.shape[1])))


def _rel_bucket(rel):
    half = REL_BUCKETS // 2
    max_exact = half // 2
    n = jnp.abs(rel)
    large = max_exact + (jnp.log(jnp.maximum(n, 1).astype(jnp.float32) / max_exact)
                         / math.log(REL_MAX_DIST / max_exact) * (half - max_exact)).astype(jnp.int32)
    large = jnp.minimum(large, half - 1)
    return jnp.where(rel > 0, half, 0) + jnp.where(n < max_exact, n, large)


def _bias_tiles(rel_bias):
    tk = KEY_TILE
    half = REL_BUCKETS // 2
    max_exact = half // 2
    n_sat = int(math.ceil(max_exact * (REL_MAX_DIST / max_exact) ** ((half - 1 - max_exact) / (half - max_exact)))) + 2
    nd = (n_sat + 2 * tk - 2) // tk + 1
    dd = jnp.arange(nd, dtype=I32)[:, None, None]
    c = jnp.arange(tk, dtype=I32)[None, :, None]
    r = jnp.arange(tk, dtype=I32)[None, None, :]
    bucket = _rel_bucket(c - r - dd * tk)
    onehot = (bucket[..., None] == jnp.arange(REL_BUCKETS, dtype=I32)).astype(F32)
    tiles = jnp.einsum('dcrb,bh->dhcr', onehot, rel_bias.astype(F32) * LOG2E,
                       precision=lax.Precision.HIGHEST)
    return tiles


def _prep_proj(norm1_g, w_in, q_norm_g, k_norm_g, idx_k_norm_g, idx_k_norm_b, d_model):
    ssm_w = d_model // 2
    attn_w = N_HEADS * HEAD_DIM
    kv = N_KV_HEADS * HEAD_DIM
    sizes = [ssm_w, attn_w, kv, kv, IDX_HEADS * IDX_DIM, IDX_DIM, IDX_HEADS, d_model, d_model]
    pts = np.cumsum(sizes)[:-1].tolist()
    wu, wq, wk, wv, wqi, wki, wwi, wga, wgb = jnp.split(w_in, pts, axis=1)
    bf = lambda a: a.astype(BF16)
    blk = np.kron(np.eye(N_KV_HEADS), np.ones((HEAD_DIM, HEAD_DIM))) / HEAD_DIM
    wwit = jnp.pad(wwi.T, ((0, 2 * SUBLANES - IDX_HEADS), (0, 0)))
    return dict(
        g1=norm1_g.reshape(1, -1).astype(F32),
        wu=bf(wu), wq=bf(_pad_heads(wq, N_HEADS, HEAD_DIM)), wk=bf(_pad_heads(wk, N_KV_HEADS, HEAD_DIM)),
        wv=bf(_pad_heads(wv, N_KV_HEADS, HEAD_DIM)), wqi=bf(_pad_heads(wqi, IDX_HEADS, IDX_DIM)),
        wki=bf(_pad_heads(wki, 1, IDX_DIM)), wwit=bf(wwit),
        wga=bf(wga), wgb=bf(wgb), wkc=bf(wk), wvc=bf(wv),
        gq=_pad_lanes(q_norm_g.astype(F32)), gk=_pad_lanes(k_norm_g.astype(F32)),
        gkc=jnp.tile(k_norm_g.astype(F32), N_KV_HEADS).reshape(1, -1),
        gi=_pad_lanes(idx_k_norm_g.astype(F32)), bi=_pad_lanes(idx_k_norm_b.astype(F32)),
        ones_h=jnp.full((LANES, LANES), 1.0 / HEAD_DIM, BF16),
        ones_c=jnp.asarray(blk, BF16),
    )


def _prep_s5(lre, lim, log_dt, b_re, b_im, c_re, c_im, dvec, wa, wb):
    g, p = lre.shape
    ch = b_re.shape[-1]
    lam = lax.complex(lre.astype(F32), lim.astype(F32))
    dt = jnp.exp(log_dt.astype(F32))[:, None]
    a_bar = jnp.exp(lam * dt)
    b_bar = ((a_bar - 1.0) / lam)[:, :, None] * lax.complex(b_re.astype(F32), b_im.astype(F32))
    gs = g // S5_DIAG
    eye = jnp.eye(gs, dtype=F32)

    def blocks_in(m):
        return jnp.einsum('jgpc,gh->jgchp', m.reshape(S5_DIAG, gs, p, ch), eye).reshape(S5_DIAG, gs * ch, gs * p)

    def blocks_out(m):
        return jnp.einsum('jgcp,gh->jgphc', m.reshape(S5_DIAG, gs, ch, p), eye).reshape(S5_DIAG, gs * p, gs * ch)

    return dict(
        b_re=blocks_in(jnp.real(b_bar)).astype(BF16), b_im=blocks_in(jnp.imag(b_bar)).astype(BF16),
        c_re=blocks_out(c_re.astype(F32)).astype(BF16), c_im=blocks_out(-c_im.astype(F32)).astype(BF16),
        a_re=jnp.real(a_bar).reshape(1, g * p), a_im=jnp.imag(a_bar).reshape(1, g * p),
        d=dvec.reshape(1, -1).astype(F32), wa=wa.astype(BF16), wb=wb.astype(BF16),
    )


def _prep_merge(w_attn_up, w_out, norm2_g, w_router, b_router):
    d = w_attn_up.shape[1]
    wup = jnp.pad(w_attn_up.reshape(N_HEADS, HEAD_DIM, d), ((0, 0), (0, LANES - HEAD_DIM), (0, 0)))
    wr_t = w_router.astype(F32).T
    wr_hi = wr_t.astype(BF16)
    wr_lo = (wr_t - wr_hi.astype(F32)).astype(BF16)
    return dict(
        wup=wup.reshape(N_HEADS * LANES, d).astype(BF16), wout=w_out.astype(BF16),
        g2=norm2_g.reshape(1, -1).astype(F32), wr_hi=wr_hi, wr_lo=wr_lo,
        br=jnp.broadcast_to(b_router.astype(F32)[:, None], (b_router.shape[0], LANES)),
    )


def _prep_moe(wg, bg, wu, bu, wd, bd):
    return dict(wg=wg.astype(BF16), wu=wu.astype(BF16), wd=wd.astype(BF16),
                bg=bg.astype(F32)[:, None, :], bu=bu.astype(F32)[:, None, :], bd=bd.astype(F32)[:, None, :])


def _pick_tile(n, pref):
    t = min(n, pref)
    while n % t:
        t //= 2
    return t


def _pad_axis(a, axis, size):
    pad = [(0, 0)] * a.ndim
    pad[axis] = (0, size - a.shape[axis])
    return jnp.pad(a, pad)


def _trunk_layer(x, past_k, past_v, past_ik, h0_re, h0_im, pw, sw, mw, ew, bias_tiles):
    bsz, seq, d = x.shape
    t = bsz * seq
    tm = _pick_tile(seq, 512)
    u_tb, q, kp, vp, qi, kip, wt, sga, sgb, kc, vc, kic = _proj(x, pw, bsz, seq, tm)

    half = sw['a_re'].shape[1]
    if h0_re is None:
        h0 = jnp.zeros((bsz, 2 * half), F32)
    else:
        h0 = jnp.concatenate([h0_re.reshape(bsz, half), h0_im.reshape(bsz, half)], axis=1).astype(F32)
    tc = _pick_tile(seq, max(1, 512 // bsz))
    ya, hout = _s5(u_tb, h0, sw, bsz, seq, tc)
    groups = half // SSM_STATE
    s_re = hout[:, :half].reshape(bsz, groups, SSM_STATE)
    s_im = hout[:, half:].reshape(bsz, groups, SSM_STATE)

    past = 0 if past_k is None else past_k.shape[1]
    n_keys = past + seq
    lk = -(-n_keys // KEY_BLOCK) * KEY_BLOCK
    kip3 = kip.reshape(bsz, seq, LANES)
    if past:
        lane = jnp.arange(LANES)
        pk = jnp.pad(past_k.astype(F32), ((0, 0), (0, 0), (0, 0), (0, LANES - HEAD_DIM)))
        pk = jnp.where(lane == HEAD_DIM, 1.0, pk).astype(BF16)
        pv = jnp.pad(past_v.astype(F32), ((0, 0), (0, 0), (0, 0), (0, LANES - HEAD_DIM)))
        pv = jnp.where(lane == HEAD_DIM, 1.0, pv).astype(BF16)
        pik = jnp.pad(past_ik.astype(F32), ((0, 0), (0, 0), (0, LANES - IDX_DIM))).astype(BF16)
        k_all = jnp.concatenate([pk.transpose(0, 2, 1, 3), kp], axis=2)
        v_all = jnp.concatenate([pv.transpose(0, 2, 1, 3), vp], axis=2)
        ki_all = jnp.concatenate([pik, kip3], axis=1)
    else:
        k_all, v_all, ki_all = kp, vp, kip3
    k_all = _pad_axis(k_all, 2, lk)
    v_all = _pad_axis(v_all, 2, lk)
    ki_all = _pad_axis(ki_all, 1, lk)
    vt_all = v_all.reshape(bsz, N_KV_HEADS, lk // KEY_BLOCK, KEY_BLOCK, LANES).transpose(0, 1, 2, 4, 3)
    tq = KEY_TILE
    seq_q = -(-seq // tq) * tq
    q_p, qi_p, wt_p = _pad_axis(q, 2, seq_q), _pad_axis(qi, 2, seq_q), _pad_axis(wt, 2, seq_q)
    kf = k_all[..., :HEAD_DIM].astype(F32)
    kmax = jnp.sqrt(jnp.max(jnp.sum(kf * kf, axis=-1), axis=-1)).reshape(-1)
    bfar = bias_tiles[-1, :, 0, 0]
    bmax = jnp.max(jnp.abs(bias_tiles), axis=(0, 2, 3))
    stats = jnp.concatenate([kmax, bmax, bfar]).astype(F32)
    bias_tiles = bias_tiles - bfar[None, :, None, None]
    attn = _dsa(stats, q_p, qi_p, wt_p, k_all, vt_all, ki_all, bias_tiles, bsz, seq_q, past, n_keys, tq)
    attn = attn[:, :seq]

    moe_tile = _pick_tile(t, MOE_TILE)
    tm2 = _pick_tile(moe_tile, 512)
    x1, h2, gt, rt, cnt = _merge(x.reshape(t, d), ya, attn.reshape(t, attn.shape[-1]), sga, sgb, mw, tm2, moe_tile)
    cnt_i = cnt[:, :, 0].astype(I32).reshape(-1)
    y = _moe(h2, x1, gt, rt, cnt_i, ew, moe_tile)

    k_new = kc.reshape(bsz, seq, N_KV_HEADS, HEAD_DIM)
    v_new = vc.reshape(bsz, seq, N_KV_HEADS, HEAD_DIM)
    ik_new = kic.reshape(bsz, seq, IDX_DIM)
    return y.reshape(bsz, seq, d), k_new, v_new, ik_new, s_re, s_im


def kernel(x_prompt, x_sample, cache_k, cache_v, cache_idx_k, state_ssm_re, state_ssm_im, rel_bias, norm1_g, w_in, ssm_lambda_re, ssm_lambda_im, ssm_log_dt, ssm_b_re, ssm_b_im, ssm_c_re, ssm_c_im, ssm_d, ssm_w_glu_a, ssm_w_glu_b, q_norm_g, k_norm_g, idx_k_norm_g, idx_k_norm_b, w_attn_up, w_out, norm2_g, moe_w_router, moe_b_router, moe_w_gate, moe_b_gate, moe_w_up, moe_b_up, moe_w_down, moe_b_down):
    depth = w_in.shape[0]
    d_model = x_prompt.shape[-1]
    bias_tiles = _bias_tiles(rel_bias)
    xp, xs = x_prompt, x_sample
    st_p, st_s = [], []
    for l in range(depth):
        pw = _prep_proj(norm1_g[l], w_in[l], q_norm_g[l], k_norm_g[l], idx_k_norm_g[l], idx_k_norm_b[l], d_model)
        sw = _prep_s5(ssm_lambda_re[l], ssm_lambda_im[l], ssm_log_dt[l], ssm_b_re[l], ssm_b_im[l], ssm_c_re[l],
                      ssm_c_im[l], ssm_d[l], ssm_w_glu_a[l], ssm_w_glu_b[l])
        mw = _prep_merge(w_attn_up[l], w_out[l], norm2_g[l], moe_w_router[l], moe_b_router[l])
        ew = _prep_moe(moe_w_gate[l], moe_b_gate[l], moe_w_up[l], moe_b_up[l], moe_w_down[l], moe_b_down[l])
        xp, *sp = _trunk_layer(xp, None, None, None, None, None, pw, sw, mw, ew, bias_tiles)
        xs, *ss = _trunk_layer(xs, cache_k[l], cache_v[l], cache_idx_k[l], state_ssm_re[l], state_ssm_im[l],
                               pw, sw, mw, ew, bias_tiles)
        st_p.append(sp)
        st_s.append(ss)
    outs_p = [jnp.stack([s[i] for s in st_p]) for i in range(5)]
    outs_s = [jnp.stack([s[i] for s in st_s]) for i in range(5)]
    return (xp, xs, *outs_p, *outs_s)
```

```python
import functools
import math

import numpy as np
import jax
import jax.numpy as jnp
from jax import lax
from jax.experimental import pallas as pl
from jax.experimental.pallas import tpu as pltpu

F32 = jnp.float32
BF16 = jnp.bfloat16
I32 = jnp.int32

LANES = 128
SUBLANES = 8
VMEM_LIMIT = 56 * 1024 * 1024

CHUNK = 64
SSM_GROUP_CH = 16
SSM_STATE = 64
N_HEADS = 8
HEAD_DIM = 64
N_KV_HEADS = 2
KV_REP = N_HEADS // N_KV_HEADS
IDX_HEADS = 8
IDX_DIM = 64
TOPK_MAX = 256
REL_BUCKETS = 32
REL_MAX_DIST = 1024
N_EXPERTS = 32
TOP_K = 4
SWIGLU_LIMIT = 7.0
SWIGLU_ALPHA = 1.702
EPS = 1e-6

KEY_TILE = 128
KEY_BLOCK = 256
SCORE_CHUNK = 4
ATTEND_CHUNK = 4
LOG2E = math.log2(math.e)
NEG_BIG = -1e30
SHIFT_LIMIT = 30.0
S5_DIAG = 2
SEARCH_FIRST = 8
SEARCH_GROUP = 4
MOE_TILE = 1024
MOE_ROWS = 160
MOE_GROUP = 8
MOE_PAIR = 2


def _cp(sem):
    return pltpu.CompilerParams(dimension_semantics=sem, vmem_limit_bytes=VMEM_LIMIT)


def _dot(a, b):
    return jnp.dot(a, b, preferred_element_type=F32)


def _dot_nt(a, b):
    return lax.dot_general(a, b, (((1,), (1,)), ((), ())), preferred_element_type=F32)


def _dot_tn(a, b):
    return lax.dot_general(a, b, (((0,), (0,)), ((), ())), preferred_element_type=F32)


def _split(a):
    hi = a.astype(BF16)
    lo = (a - hi.astype(F32)).astype(BF16)
    return hi, lo


def _dot_split(a, g):
    hi, lo = _split(a)
    return _dot(hi, g) + _dot(lo, g)


def _proj_kernel(x_ref, g1_ref, wu_ref, wq_ref, wk_ref, wv_ref, wqi_ref, wki_ref, wwit_ref, wga_ref, wgb_ref,
                 wkc_ref, wvc_ref, gq_ref, gk_ref, gkc_ref, gi_ref, bi_ref, ones_h_ref, ones_c_ref,
                 u_ref, q_ref, kp_ref, vp_ref, qi_ref, kip_ref, wt_ref, sga_ref, sgb_ref,
                 kc_ref, vc_ref, kic_ref):
    x = x_ref[...]
    ms = jnp.mean(x * x, axis=-1, keepdims=True)
    hn = (x * lax.rsqrt(ms + EPS) * g1_ref[...]).astype(BF16)
    ones_h = ones_h_ref[...]
    lane = lax.broadcasted_iota(I32, (x.shape[0], LANES), 1)

    u_ref[...] = _dot(hn, wu_ref[...]).astype(BF16)

    q = _dot(hn, wq_ref[...])
    scale = HEAD_DIM ** -0.5 * LOG2E
    for h in range(N_HEADS):
        qh = q[:, h * LANES:(h + 1) * LANES]
        msq = _dot_split(qh * qh, ones_h)
        q_ref[h] = (qh * lax.rsqrt(msq + EPS) * (gq_ref[...] * scale)).astype(BF16)

    k = _dot(hn, wk_ref[...])
    for g in range(N_KV_HEADS):
        kg = k[:, g * LANES:(g + 1) * LANES]
        msk = _dot_split(kg * kg, ones_h)
        kn = kg * lax.rsqrt(msk + EPS) * gk_ref[...]
        kp_ref[g] = jnp.where(lane == HEAD_DIM, 1.0, kn).astype(BF16)

    v = _dot(hn, wv_ref[...])
    for g in range(N_KV_HEADS):
        vg = v[:, g * LANES:(g + 1) * LANES]
        vp_ref[g] = jnp.where(lane == HEAD_DIM, 1.0, vg).astype(BF16)

    qi = _dot(hn, wqi_ref[...])
    for h in range(IDX_HEADS):
        qi_ref[h] = qi[:, h * LANES:(h + 1) * LANES].astype(BF16)

    ki = _dot(hn, wki_ref[...])
    mu = _dot_split(ki, ones_h)
    xc = jnp.where(lane < IDX_DIM, ki - mu, 0.0)
    var = _dot_split(xc * xc, ones_h)
    kin = xc * lax.rsqrt(var + EPS) * gi_ref[...] + bi_ref[...]
    kip_ref[...] = kin.astype(BF16)
    kic_ref[...] = kin[:, :IDX_DIM]

    wt = _dot_nt(wwit_ref[...], hn)
    wt_ref[...] = wt[0:IDX_HEADS, :] * (IDX_HEADS ** -0.5 * IDX_DIM ** -0.5)

    sga_ref[...] = jax.nn.sigmoid(_dot(hn, wga_ref[...])).astype(BF16)
    sgb_ref[...] = jax.nn.sigmoid(_dot(hn, wgb_ref[...])).astype(BF16)

    kc = _dot(hn, wkc_ref[...])
    mskc = _dot_split(kc * kc, ones_c_ref[...])
    kc_ref[...] = kc * lax.rsqrt(mskc + EPS) * gkc_ref[...]
    vc_ref[...] = _dot(hn, wvc_ref[...])


def _proj(x, pw, bsz, seq, tm):
    d = x.shape[-1]
    nt = seq // tm
    t = bsz * seq
    x2 = x.reshape(t, d)

    def tok(b, i):
        return (b * nt + i, 0)

    def cst(b, i):
        return (0, 0)

    def wspec(a):
        return pl.BlockSpec(a.shape, cst)

    weights = [pw['g1'], pw['wu'], pw['wq'], pw['wk'], pw['wv'], pw['wqi'], pw['wki'], pw['wwit'], pw['wga'],
               pw['wgb'], pw['wkc'], pw['wvc'], pw['gq'], pw['gk'], pw['gkc'], pw['gi'], pw['bi'],
               pw['ones_h'], pw['ones_c']]
    ssm_w = pw['wu'].shape[1]
    out_shape = (
        jax.ShapeDtypeStruct((seq, bsz * ssm_w), BF16),
        jax.ShapeDtypeStruct((bsz, N_HEADS, seq, LANES), BF16),
        jax.ShapeDtypeStruct((bsz, N_KV_HEADS, seq, LANES), BF16),
        jax.ShapeDtypeStruct((bsz, N_KV_HEADS, seq, LANES), BF16),
        jax.ShapeDtypeStruct((bsz, IDX_HEADS, seq, LANES), BF16),
        jax.ShapeDtypeStruct((t, LANES), BF16),
        jax.ShapeDtypeStruct((bsz, IDX_HEADS, seq), F32),
        jax.ShapeDtypeStruct((t, d), BF16),
        jax.ShapeDtypeStruct((t, d), BF16),
        jax.ShapeDtypeStruct((t, N_KV_HEADS * HEAD_DIM), F32),
        jax.ShapeDtypeStruct((t, N_KV_HEADS * HEAD_DIM), F32),
        jax.ShapeDtypeStruct((t, IDX_DIM), F32),
    )

    def hm(nh):
        return pl.BlockSpec((None, nh, tm, LANES), lambda b, i: (b, 0, i, 0))

    out_specs = (
        pl.BlockSpec((tm, ssm_w), lambda b, i: (i, b)),
        hm(N_HEADS), hm(N_KV_HEADS), hm(N_KV_HEADS), hm(IDX_HEADS),
        pl.BlockSpec((tm, LANES), tok),
        pl.BlockSpec((None, IDX_HEADS, tm), lambda b, i: (b, 0, i)),
        pl.BlockSpec((tm, d), tok), pl.BlockSpec((tm, d), tok),
        pl.BlockSpec((tm, N_KV_HEADS * HEAD_DIM), tok), pl.BlockSpec((tm, N_KV_HEADS * HEAD_DIM), tok),
        pl.BlockSpec((tm, IDX_DIM), tok),
    )
    return pl.pallas_call(
        _proj_kernel,
        grid=(bsz, nt),
        in_specs=[pl.BlockSpec((tm, d), tok)] + [wspec(a) for a in weights],
        out_specs=out_specs,
        out_shape=out_shape,
        compiler_params=_cp(("arbitrary", "arbitrary")),
        name="proj",
    )(x2, *weights)


def _gelu_tanh(x):
    return 0.5 * x * (1.0 + jnp.tanh(math.sqrt(2.0 / math.pi) * (x + 0.044715 * (x * x * x))))


def _s5_kernel(u_ref, h0_ref, bre_ref, bim_ref, are_ref, aim_ref, cre_ref, cim_ref, dvec_ref, wa_ref, wb_ref,
               ya_ref, hout_ref, state_ref, bu_ref, yf_ref, *, bsz, tc, strip):
    s = pl.program_id(0)
    half = are_ref.shape[1]

    @pl.when(s == 0)
    def _():
        state_ref[...] = h0_ref[...]

    u = u_ref[...]
    cw = u.shape[1] // S5_DIAG
    sw = half // S5_DIAG
    for j in range(S5_DIAG):
        uj = u[:, j * cw:(j + 1) * cw]
        bu_ref[:, j * sw:(j + 1) * sw] = _dot(uj, bre_ref[j])
        bu_ref[:, half + j * sw:half + (j + 1) * sw] = _dot(uj, bim_ref[j])

    for c0 in range(0, half, strip):
        ar = jnp.broadcast_to(are_ref[:, c0:c0 + strip], (bsz, strip))
        ai = jnp.broadcast_to(aim_ref[:, c0:c0 + strip], (bsz, strip))
        hr0 = state_ref[:, c0:c0 + strip]
        hi0 = state_ref[:, half + c0:half + c0 + strip]

        def step(t, carry):
            hr, hi = carry
            r0 = pl.multiple_of(t * bsz, bsz)
            br = bu_ref[pl.ds(r0, bsz), c0:c0 + strip]
            bi = bu_ref[pl.ds(r0, bsz), half + c0:half + c0 + strip]
            nr = ar * hr - ai * hi + br
            ni = ar * hi + ai * hr + bi
            bu_ref[pl.ds(r0, bsz), c0:c0 + strip] = nr
            bu_ref[pl.ds(r0, bsz), half + c0:half + c0 + strip] = ni
            return nr, ni

        hr, hi = lax.fori_loop(0, tc, step, (hr0, hi0))
        state_ref[:, c0:c0 + strip] = hr
        state_ref[:, half + c0:half + c0 + strip] = hi

    ys = []
    for j in range(S5_DIAG):
        s_re = bu_ref[:, j * sw:(j + 1) * sw].astype(BF16)
        s_im = bu_ref[:, half + j * sw:half + (j + 1) * sw].astype(BF16)
        ys.append(_dot(s_re, cre_ref[j]) + _dot(s_im, cim_ref[j]))
    y = jnp.concatenate(ys, axis=1) + dvec_ref[...] * u.astype(F32)
    g = _gelu_tanh(y).astype(BF16)
    ya = _dot(g, wa_ref[...]) * jax.nn.sigmoid(_dot(g, wb_ref[...]))
    n_chunk = ya.shape[1] // LANES
    for c in range(n_chunk):
        yf_ref[c] = ya[:, c * LANES:(c + 1) * LANES]
    for b in range(bsz):
        ya_ref[b] = jnp.concatenate([yf_ref[c, pl.ds(b, tc, stride=bsz), :] for c in range(n_chunk)],
                                    axis=1).astype(BF16)

    @pl.when(s == pl.num_programs(0) - 1)
    def _():
        hout_ref[...] = state_ref[...]


def _s5(u_tb, h0, sw, bsz, seq, tc):
    rows = tc * bsz
    ssm_w = sw['d'].shape[1]
    half = sw['a_re'].shape[1]
    two_half = 2 * half
    d = sw['wa'].shape[1]
    u2 = u_tb.reshape(seq * bsz, ssm_w)
    strip = min(512, half)

    consts = [h0, sw['b_re'], sw['b_im'], sw['a_re'], sw['a_im'], sw['c_re'], sw['c_im'], sw['d'], sw['wa'], sw['wb']]

    def cst(s):
        return (0, 0)

    def cspec(a):
        return pl.BlockSpec(a.shape, lambda s: (0,) * a.ndim)

    ya, hout = pl.pallas_call(
        functools.partial(_s5_kernel, bsz=bsz, tc=tc, strip=strip),
        grid=(seq // tc,),
        in_specs=[pl.BlockSpec((rows, ssm_w), lambda s: (s, 0))] + [cspec(a) for a in consts],
        out_specs=(pl.BlockSpec((bsz, tc, d), lambda s: (0, s, 0)), pl.BlockSpec((bsz, two_half), cst)),
        out_shape=(jax.ShapeDtypeStruct((bsz, seq, d), BF16), jax.ShapeDtypeStruct((bsz, two_half), F32)),
        scratch_shapes=[pltpu.VMEM((bsz, two_half), F32), pltpu.VMEM((rows, two_half), F32),
                        pltpu.VMEM((d // LANES, rows, LANES), F32)],
        compiler_params=_cp(("arbitrary",)),
        name="s5",
    )(u2, *consts)
    return ya.reshape(bsz * seq, d), hout


def _f2key(x):
    b = lax.bitcast_convert_type(x, I32)
    return b ^ ((b >> 31) & 0x7FFFFFFF)


def _key2f(k):
    return lax.bitcast_convert_type(k ^ ((k >> 31) & 0x7FFFFFFF), F32)


def _dsa_kernel(st_ref, q_ref, qi_ref, wt_ref, k_ref, vt_ref, ki_ref, bias_ref, o_ref,
                s_ref, lo_ref, hi_ref, clo_ref, glo_ref, ghi_ref, side_ref, q2_ref, mrow_ref, acc_ref,
                *, bsz, tq, past, n_keys, topk, nkt, nd, idx_bits):
    b_id = pl.program_id(0)
    i = pl.program_id(1)
    kb = KEY_BLOCK
    sl = SUBLANES
    q0 = past + i * tq
    last_chunk = (q0 + tq - 1) // CHUNK
    n_kt = jnp.minimum(nkt, ((last_chunk + 1) * CHUNK + kb - 1) // kb)
    d0 = q0 // KEY_TILE

    krow = lax.broadcasted_iota(I32, (kb, tq), 0)
    q_chunk = (q0 + lax.broadcasted_iota(I32, (kb, tq), 1)) // CHUNK
    qc8 = (q0 + lax.broadcasted_iota(I32, (sl, tq), 1)) // CHUNK
    n_adm = jnp.minimum((qc8 + 1) * CHUNK, n_keys)
    n_admf = n_adm.astype(F32)
    is_pad = q0 + lax.broadcasted_iota(I32, (sl, tq), 1) >= n_keys
    needf = jnp.where(is_pad, n_adm, jnp.minimum(topk, n_adm)).astype(F32)

    def bcast(x):
        return jnp.broadcast_to(x[0:1, :], (kb, tq))

    def rep(x):
        return jnp.broadcast_to(x, (sl, tq))

    qi = qi_ref[...].reshape(IDX_HEADS * tq, LANES)

    def score_blocks(kt, nb, masked):
        k0 = pl.multiple_of(kt * kb, kb)
        s = _dot_nt(ki_ref[pl.ds(k0, nb * kb), :], qi)
        for j in range(nb):
            sc = jnp.zeros((kb, tq), F32)
            for h in range(IDX_HEADS):
                sc = sc + wt_ref[h:h + 1, :] * jnp.maximum(s[j * kb:(j + 1) * kb, h * tq:(h + 1) * tq], 0.0)
            if masked:
                kpos = k0 + j * kb + krow
                adm = ((kpos // CHUNK) <= q_chunk) & (kpos < n_keys)
                sc = jnp.where(adm, sc, -jnp.inf)
            s_ref[kt + j] = sc

    n_open = n_kt - 1

    def score_chunk(j, c):
        score_blocks(SCORE_CHUNK * j, SCORE_CHUNK, False)
        return c

    lax.fori_loop(0, n_open // SCORE_CHUNK, score_chunk, 0)

    def score_single(j, c):
        score_blocks(j, 1, False)
        return c

    lax.fori_loop(n_open - n_open % SCORE_CHUNK, n_open, score_single, 0)
    score_blocks(n_kt - 1, 1, True)

    part = 4 * sl

    def fold(x, op):
        x = x.reshape(kb // part, part, tq)
        acc = x[0]
        for j in range(1, kb // part):
            acc = op(acc, x[j])
        return acc

    def count(pred):
        def one(kt):
            return fold(jnp.where(pred(s_ref[kt], kt), 1.0, 0.0), jnp.add)
        c = lax.fori_loop(0, n_kt // 2, lambda j, c: c + (one(2 * j) + one(2 * j + 1)), jnp.zeros((part, tq), F32))
        c = c + lax.cond(n_kt % 2 == 1, lambda: one(n_kt - 1), lambda: jnp.zeros((part, tq), F32))
        return rep(jnp.sum(c, axis=0, keepdims=True))

    def minmax(kt, c):
        mx, mn = c
        s = s_ref[kt]
        return (jnp.maximum(mx, fold(s, jnp.maximum)),
                jnp.minimum(mn, fold(jnp.where(s == -jnp.inf, jnp.inf, s), jnp.minimum)))

    mx, mn = lax.fori_loop(0, n_kt, minmax,
                           (jnp.full((part, tq), -jnp.inf, F32), jnp.full((part, tq), jnp.inf, F32)))
    lo_ref[...] = rep(jnp.min(mn, axis=0, keepdims=True))
    hi_ref[...] = _key2f(_f2key(rep(jnp.max(mx, axis=0, keepdims=True))) + 1)
    def odds(cnt):
        c = jnp.clip(cnt, 0.5, n_admf - 0.5)
        return jnp.log((n_admf - c) / c)

    target = odds(needf - 0.5)
    clo_ref[...] = n_admf
    glo_ref[...] = target - odds(n_admf)
    ghi_ref[...] = target - odds(jnp.zeros((sl, tq), F32))
    side_ref[...] = jnp.zeros((sl, tq), F32)

    def searching(lo, hi, clo):
        return (_f2key(hi) > _f2key(lo) + 1) & (clo > needf)

    def refine(it, c):
        lo, hi, clo = lo_ref[...], hi_ref[...], clo_ref[...]
        glo, ghi, side = glo_ref[...], ghi_ref[...], side_ref[...]
        k_t = _f2key(lo + (hi - lo) * (glo / (glo - ghi)))
        t = _key2f(jnp.minimum(jnp.maximum(k_t, _f2key(lo) + 1), _f2key(hi) - 1))
        tb = bcast(t)
        cnt = count(lambda s, kt: s >= tb)
        g = target - odds(cnt)
        open_ = searching(lo, hi, clo)
        up = open_ & (cnt >= needf)
        dn = open_ & (cnt < needf)
        lo_ref[...] = jnp.where(up, t, lo)
        clo_ref[...] = jnp.where(up, cnt, clo)
        hi_ref[...] = jnp.where(dn, t, hi)
        glo_ref[...] = jnp.where(up, g, jnp.where(dn & (side < 0.0), glo * 0.5, glo))
        ghi_ref[...] = jnp.where(dn, g, jnp.where(up & (side > 0.0), ghi * 0.5, ghi))
        side_ref[...] = jnp.where(up, 1.0, jnp.where(dn, -1.0, side))
        return c

    def snap():
        lo, hi, clo = lo_ref[...], hi_ref[...], clo_ref[...]
        lo_b, hi_b = bcast(lo), bcast(hi)

        def body(kt, c):
            a, b = c
            s = s_ref[kt]
            return (jnp.minimum(a, fold(jnp.where(s >= lo_b, s, jnp.inf), jnp.minimum)),
                    jnp.maximum(b, fold(jnp.where(s < hi_b, s, -jnp.inf), jnp.maximum)))

        a, b = lax.fori_loop(0, n_kt, body,
                             (jnp.full((part, tq), jnp.inf, F32), jnp.full((part, tq), -jnp.inf, F32)))
        open_ = searching(lo, hi, clo)
        lo_ref[...] = jnp.where(open_, rep(jnp.min(a, axis=0, keepdims=True)), lo)
        hi_ref[...] = jnp.where(open_, _key2f(_f2key(rep(jnp.max(b, axis=0, keepdims=True))) + 1), hi)

    def n_searching():
        return jnp.max(jnp.where(searching(lo_ref[...], hi_ref[...], clo_ref[...]), 1.0, 0.0))

    def group(c):
        grp, _ = c
        lax.fori_loop(0, jnp.where(grp == 0, SEARCH_FIRST, SEARCH_GROUP), refine, 0)
        snap()
        return grp + 1, n_searching()

    lax.while_loop(lambda c: c[1] > 0.0, group, (jnp.int32(0), n_searching()))
    thr = lo_ref[...]
    thr_b = bcast(thr)

    n_tied = jnp.max(jnp.where(clo_ref[...] > needf, 1.0, 0.0))

    @pl.when(n_tied > 0.0)
    def _():
        rem = needf - count(lambda s, kt: s > thr_b)
        lo_ref[...] = jnp.zeros((sl, tq), F32)
        hi_ref[...] = jnp.full((sl, tq), float(nkt * kb), F32)
        krowf = krow.astype(F32)

        def bisect_idx(it, c):
            lo = lo_ref[...]
            hi = hi_ref[...]
            mid = jnp.floor((lo + hi) * 0.5)
            mid_b = bcast(mid)
            ok = count(lambda s, kt: (s == thr_b) & ((kt * kb).astype(F32) + krowf < mid_b)) >= rem
            hi_ref[...] = jnp.where(ok, mid, hi)
            lo_ref[...] = jnp.where(ok, lo, mid)
            return c

        lax.fori_loop(0, idx_bits, bisect_idx, 0)
        cut_b = bcast(hi_ref[...])

        def drop(kt, c):
            s = s_ref[kt]
            s_ref[kt] = jnp.where((s == thr_b) & ((kt * kb).astype(F32) + krowf >= cut_b), -jnp.inf, s)
            return c

        lax.fori_loop(0, n_kt, drop, 0)

    rows_g = KV_REP * tq
    qf = q_ref[...].reshape(N_HEADS * tq, LANES).astype(F32)
    qn = jnp.sqrt(jnp.sum(qf * qf, axis=1, keepdims=True))
    lane = lax.broadcasted_iota(I32, (tq, LANES), 1)
    worst = jnp.float32(0.0)
    for h in range(N_HEADS):
        kmax = st_ref[b_id * N_KV_HEADS + h // KV_REP]
        bmax = st_ref[bsz * N_KV_HEADS + h]
        bfar = st_ref[bsz * N_KV_HEADS + N_HEADS + h]
        bound = qn[h * tq:(h + 1) * tq, :] * (kmax * 1.01) + (bmax + 0.1)
        worst = jnp.maximum(worst, jnp.max(bound))
        q2_ref[h * tq:(h + 1) * tq, :] = jnp.where(lane == HEAD_DIM, bfar - bound,
                                                   qf[h * tq:(h + 1) * tq, :]).astype(BF16)
    n_far = jnp.clip((d0 - nd) // 2 + 1, 0, n_kt)

    def logits(kt0, nb, g, near, exact):
        k0 = pl.multiple_of(kt0 * kb, kb)
        s = _dot_nt(k_ref[g, pl.ds(k0, nb * kb), :], q2_ref[g * rows_g:(g + 1) * rows_g, :])
        out = []
        for j in range(nb):
            kt = kt0 + j
            maskadd = jnp.where(s_ref[kt] >= thr_b, 0.0, NEG_BIG)
            if near:
                da = jnp.clip(d0 - 2 * kt, 0, nd - 1)
                db = jnp.clip(d0 - 2 * kt - 1, 0, nd - 1)
            parts = []
            for r in range(KV_REP):
                h = g * KV_REP + r
                add = maskadd - mrow_ref[0:1, h * tq:(h + 1) * tq] if exact else maskadd
                if near:
                    add = jnp.concatenate([bias_ref[da, h], bias_ref[db, h]], axis=0) + add
                parts.append(s[j * kb:(j + 1) * kb, r * tq:(r + 1) * tq] + add)
            out.append(jnp.concatenate(parts, axis=1))
        return out

    def over_blocks(fn):
        def far_chunk(j, c):
            fn(ATTEND_CHUNK * j, ATTEND_CHUNK, False)
            return c
        lax.fori_loop(0, n_far // ATTEND_CHUNK, far_chunk, 0)

        def far_single(kt, c):
            fn(kt, 1, False)
            return c
        lax.fori_loop(n_far - n_far % ATTEND_CHUNK, n_far, far_single, 0)

        n_near = n_kt - n_far

        def near_pair(j, c):
            fn(n_far + 2 * j, 2, True)
            return c
        lax.fori_loop(0, n_near // 2, near_pair, 0)

        @pl.when(n_near % 2 == 1)
        def _():
            fn(n_kt - 1, 1, True)

    def attend(exact):
        acc_ref[...] = jnp.zeros(acc_ref.shape, F32)

        def blocks(kt0, nb, near):
            for g in range(N_KV_HEADS):
                p = jnp.concatenate([jnp.exp2(lg).astype(BF16) for lg in logits(kt0, nb, g, near, exact)], axis=0)
                vt = jnp.concatenate([vt_ref[g, kt0 + j] for j in range(nb)], axis=1)
                acc_ref[g] += _dot(vt, p)
        over_blocks(blocks)

    @pl.when(worst <= SHIFT_LIMIT)
    def _():
        attend(False)

    @pl.when(worst > SHIFT_LIMIT)
    def _():
        mrow_ref[...] = jnp.full(mrow_ref.shape, NEG_BIG, F32)

        def blocks(kt0, nb, near):
            for g in range(N_KV_HEADS):
                for lg in logits(kt0, nb, g, near, False):
                    mx = jnp.max(lg, axis=0, keepdims=True)
                    cur = mrow_ref[:, g * rows_g:(g + 1) * rows_g]
                    mrow_ref[:, g * rows_g:(g + 1) * rows_g] = jnp.maximum(cur, jnp.broadcast_to(mx, (sl, rows_g)))
        over_blocks(blocks)
        attend(True)

    for g in range(N_KV_HEADS):
        acc = acc_ref[g]
        og = acc / acc[HEAD_DIM:HEAD_DIM + 1, :]
        for r in range(KV_REP):
            h = g * KV_REP + r
            o_ref[:, h * LANES:(h + 1) * LANES] = og[:, r * tq:(r + 1) * tq].T.astype(BF16)


def _dsa(stats, q, qi, wt, k_all, vt_all, ki_all, bias_tiles, bsz, seq, past, n_keys, tq):
    lk = k_all.shape[2]
    nkt = lk // KEY_BLOCK
    topk = min(TOPK_MAX, n_keys // 4)
    nd = bias_tiles.shape[0]
    nq = seq // tq
    assert past % KEY_TILE == 0 and tq == KEY_TILE
    idx_bits = int(math.ceil(math.log2(lk))) + 1
    kern = functools.partial(_dsa_kernel, bsz=bsz, tq=tq, past=past, n_keys=n_keys, topk=topk, nkt=nkt, nd=nd,
                             idx_bits=idx_bits)
    row_state = pltpu.VMEM((SUBLANES, tq), F32)
    grid_spec = pltpu.PrefetchScalarGridSpec(
        num_scalar_prefetch=1,
        grid=(bsz, nq),
        in_specs=[
            pl.BlockSpec((None, N_HEADS, tq, LANES), lambda b, i, s: (b, 0, i, 0)),
            pl.BlockSpec((None, IDX_HEADS, tq, LANES), lambda b, i, s: (b, 0, i, 0)),
            pl.BlockSpec((None, IDX_HEADS, tq), lambda b, i, s: (b, 0, i)),
            pl.BlockSpec((None, N_KV_HEADS, lk, LANES), lambda b, i, s: (b, 0, 0, 0)),
            pl.BlockSpec((None, N_KV_HEADS, nkt, LANES, KEY_BLOCK), lambda b, i, s: (b, 0, 0, 0, 0)),
            pl.BlockSpec((None, lk, LANES), lambda b, i, s: (b, 0, 0)),
            pl.BlockSpec(bias_tiles.shape, lambda b, i, s: (0, 0, 0, 0)),
        ],
        out_specs=pl.BlockSpec((None, tq, N_HEADS * LANES), lambda b, i, s: (b, i, 0)),
        scratch_shapes=[
            pltpu.VMEM((nkt, KEY_BLOCK, tq), F32),
            row_state, row_state, row_state, row_state, row_state, row_state,
            pltpu.VMEM((N_HEADS * tq, LANES), BF16),
            pltpu.VMEM((SUBLANES, N_HEADS * tq), F32),
            pltpu.VMEM((N_KV_HEADS, LANES, KV_REP * tq), F32),
        ],
    )
    return pl.pallas_call(
        kern,
        grid_spec=grid_spec,
        out_shape=jax.ShapeDtypeStruct((bsz, seq, N_HEADS * LANES), BF16),
        compiler_params=_cp(("arbitrary", "arbitrary")),
        name="dsa",
    )(stats, q, qi, wt, k_all, vt_all, ki_all, bias_tiles)


def _merge_kernel(x_ref, ya_ref, at_ref, sga_ref, sgb_ref, wup_ref, wout_ref, g2_ref, wr_hi_ref, wr_lo_ref, br_ref,
                  x1_ref, h2_ref, gt_ref, rt_ref, cnt_ref, run_ref, *, tm, sub):
    step = pl.program_id(0)

    @pl.when(step % sub == 0)
    def _():
        run_ref[...] = jnp.zeros(run_ref.shape, F32)

    yb = _dot(at_ref[...], wup_ref[...])
    merged = sga_ref[...].astype(F32) * ya_ref[...].astype(F32) + sgb_ref[...].astype(F32) * yb
    x1 = x_ref[...] + _dot(merged.astype(BF16), wout_ref[...])
    x1_ref[...] = x1
    ms = jnp.mean(x1 * x1, axis=-1, keepdims=True)
    h2 = x1 * lax.rsqrt(ms + EPS) * g2_ref[...]
    h2_hi, h2_lo = _split(h2)
    h2_ref[...] = h2_hi

    wr_hi = wr_hi_ref[...]
    logit = (_dot_nt(wr_hi, h2_hi) + _dot_nt(wr_hi, h2_lo) + _dot_nt(wr_lo_ref[...], h2_hi)) + br_ref[:, 0:1]
    ne = logit.shape[0]
    eid = lax.broadcasted_iota(I32, (ne, tm), 0).astype(F32)
    selb = jnp.zeros((ne, tm), F32)
    tops = []
    picks = []
    for _ in range(TOP_K):
        mx = jnp.max(logit, axis=0, keepdims=True)
        pick = jnp.min(jnp.where(logit == mx, eid, float(ne)), axis=0, keepdims=True)
        hit = eid == pick
        selb = jnp.where(hit, 1.0, selb)
        logit = jnp.where(hit, -jnp.inf, logit)
        tops.append(mx)
        picks.append(hit)
    ex = [jnp.exp(t - tops[0]) for t in tops]
    den = ex[0] + ex[1] + ex[2] + ex[3]
    gate = jnp.zeros((ne, tm), F32)
    for hit, e in zip(picks, ex):
        gate = jnp.where(hit, e / den, gate)
    gt_ref[...] = gate

    sel = selb > 0.5
    selb = selb.astype(BF16)
    r_i = lax.broadcasted_iota(I32, (tm, tm), 0)
    c_i = lax.broadcasted_iota(I32, (tm, tm), 1)
    tri = jnp.where(r_i < c_i, 1.0, 0.0).astype(BF16)
    run = run_ref[...]
    rank = _dot(selb, tri) + jnp.broadcast_to(run[:, 0:1], (ne, tm))
    rt_ref[...] = jnp.where(sel, rank, -1.0)
    run = run + _dot(selb, jnp.ones((tm, LANES), BF16))
    run_ref[...] = run
    cnt_ref[...] = run


def _merge(x2, ya, attn, sga, sgb, mw, tm, moe_tile):
    t, d = x2.shape
    sub = moe_tile // tm
    ne = mw['wr_hi'].shape[0]

    def tok(i):
        return (i, 0)

    def cst(i):
        return (0, 0)

    consts = [mw['wup'], mw['wout'], mw['g2'], mw['wr_hi'], mw['wr_lo'], mw['br']]
    return pl.pallas_call(
        functools.partial(_merge_kernel, tm=tm, sub=sub),
        grid=(t // tm,),
        in_specs=[
            pl.BlockSpec((tm, d), tok),
            pl.BlockSpec((tm, d), tok),
            pl.BlockSpec((tm, attn.shape[-1]), tok),
            pl.BlockSpec((tm, d), tok),
            pl.BlockSpec((tm, d), tok),
        ] + [pl.BlockSpec(a.shape, cst) for a in consts],
        out_specs=(
            pl.BlockSpec((tm, d), tok),
            pl.BlockSpec((tm, d), tok),
            pl.BlockSpec((ne, tm), lambda i: (0, i)),
            pl.BlockSpec((ne, tm), lambda i: (0, i)),
            pl.BlockSpec((None, ne, LANES), lambda i: (i // sub, 0, 0)),
        ),
        out_shape=(
            jax.ShapeDtypeStruct((t, d), F32),
            jax.ShapeDtypeStruct((t, d), BF16),
            jax.ShapeDtypeStruct((ne, t), F32),
            jax.ShapeDtypeStruct((ne, t), F32),
            jax.ShapeDtypeStruct((t // moe_tile, ne, LANES), F32),
        ),
        scratch_shapes=[pltpu.VMEM((ne, LANES), F32)],
        compiler_params=_cp(("arbitrary",)),
        name="merge",
    )(x2, ya, attn, sga, sgb, *consts)


def _moe_kernel(cnt_ref, h2_ref, x1_hbm, gt_ref, rt_ref, wg_ref, wu_ref, wd_ref, bg_ref, bu_ref, bd_ref, y_ref,
                pg_ref, og_ref, sem, *, tt, pair):
    j = pl.program_id(0)
    e = pl.program_id(1)
    ne = pl.num_programs(1)
    rb = MOE_ROWS
    slot = e % MOE_GROUP

    @pl.when(e == 0)
    def _():
        cp = pltpu.make_async_copy(x1_hbm.at[pl.ds(pl.multiple_of(j * pair * tt, tt), pair * tt), :], y_ref, sem)
        cp.start()
        cp.wait()

    mine = lax.broadcasted_iota(I32, (SUBLANES, pair * tt), 0) == e % SUBLANES
    g_all = jnp.sum(jnp.where(mine, gt_ref[...], 0.0), axis=0, keepdims=True)
    r_all = jnp.sum(jnp.where(mine, rt_ref[...], 0.0), axis=0, keepdims=True)
    rid = lax.broadcasted_iota(I32, (rb, tt), 0).astype(F32)

    def one_hot(s, blk):
        return jnp.broadcast_to(r_all[:, s * tt:(s + 1) * tt], (rb, tt)) == (rid + (blk * rb).astype(F32))

    def gather(s, hit):
        p = jnp.where(hit, 1.0, 0.0).astype(BF16)
        return p, _dot(p, h2_ref[s * tt:(s + 1) * tt, :]).astype(BF16)

    def expert(xg):
        a = jnp.minimum(_dot(xg, wg_ref[0]) + bg_ref[0], SWIGLU_LIMIT)
        b = jnp.clip(_dot(xg, wu_ref[0]) + bu_ref[0], -SWIGLU_LIMIT, SWIGLU_LIMIT)
        hid = a * jax.nn.sigmoid(SWIGLU_ALPHA * a) * (b + 1.0)
        return _dot(hid.astype(BF16), wd_ref[0]) + bd_ref[0]

    def gated(s, hit, o):
        g_row = jnp.broadcast_to(g_all[:, s * tt:(s + 1) * tt], (rb, tt))
        return (o * jnp.sum(jnp.where(hit, g_row, 0.0), axis=1, keepdims=True)).astype(BF16)

    hits = [one_hot(s, jnp.int32(0)) for s in range(pair)]
    gathered = [gather(s, hits[s]) for s in range(pair)]
    o = expert(jnp.concatenate([xg for _, xg in gathered], axis=0))
    r0 = pl.multiple_of(slot * rb, rb)
    for s in range(pair):
        pg_ref[s, pl.ds(r0, rb), :] = gathered[s][0]
        og_ref[s, pl.ds(r0, rb), :] = gated(s, hits[s], o[s * rb:(s + 1) * rb])

    @pl.when(slot == MOE_GROUP - 1)
    def _():
        for s in range(pair):
            y_ref[s * tt:(s + 1) * tt, :] += _dot_tn(pg_ref[s], og_ref[s])

    for s in range(pair):
        def overflow(blk, c, s=s):
            hit = one_hot(s, blk)
            p, xg = gather(s, hit)
            y_ref[s * tt:(s + 1) * tt, :] += _dot_tn(p, gated(s, hit, expert(xg)))
            return c

        n_rows = cnt_ref[(j * pair + s) * ne + e]
        lax.fori_loop(1, (n_rows + rb - 1) // rb, overflow, 0)


def _moe(h2, x1, gt, rt, cnt, ew, tt):
    t, d = h2.shape
    ne = gt.shape[0]
    nt = t // tt
    f = ew['wg'].shape[-1]
    pair = MOE_PAIR if nt % MOE_PAIR == 0 else 1
    grid_spec = pltpu.PrefetchScalarGridSpec(
        num_scalar_prefetch=1,
        grid=(nt // pair, ne),
        in_specs=[
            pl.BlockSpec((pair * tt, d), lambda j, e, c: (j, 0)),
            pl.BlockSpec(memory_space=pl.ANY),
            pl.BlockSpec((SUBLANES, pair * tt), lambda j, e, c: (e // SUBLANES, j)),
            pl.BlockSpec((SUBLANES, pair * tt), lambda j, e, c: (e // SUBLANES, j)),
            pl.BlockSpec((1, d, f), lambda j, e, c: (e, 0, 0)),
            pl.BlockSpec((1, d, f), lambda j, e, c: (e, 0, 0)),
            pl.BlockSpec((1, f, d), lambda j, e, c: (e, 0, 0)),
            pl.BlockSpec((1, 1, f), lambda j, e, c: (e, 0, 0)),
            pl.BlockSpec((1, 1, f), lambda j, e, c: (e, 0, 0)),
            pl.BlockSpec((1, 1, d), lambda j, e, c: (e, 0, 0)),
        ],
        out_specs=pl.BlockSpec((pair * tt, d), lambda j, e, c: (j, 0)),
        scratch_shapes=[pltpu.VMEM((pair, MOE_GROUP * MOE_ROWS, tt), BF16),
                        pltpu.VMEM((pair, MOE_GROUP * MOE_ROWS, d), BF16),
                        pltpu.SemaphoreType.DMA(())],
    )
    assert ne % MOE_GROUP == 0
    return pl.pallas_call(
        functools.partial(_moe_kernel, tt=tt, pair=pair),
        grid_spec=grid_spec,
        out_shape=jax.ShapeDtypeStruct((t, d), F32),
        compiler_params=_cp(("arbitrary", "arbitrary")),
        name="moe",
    )(cnt, h2, x1, gt, rt, ew['wg'], ew['wu'], ew['wd'], ew['bg'], ew['bu'], ew['bd'])


def _pad_heads(wmat, n_heads, width):
    d = wmat.shape[0]
    w3 = wmat.reshape(d, n_heads, width)
    return jnp.pad(w3, ((0, 0), (0, 0), (0, LANES - width))).reshape(d, n_heads * LANES)


def _pad_lanes(v, width=LANES):
    v = v.reshape(1, -1)
    return jnp.pad(v, ((0, 0), (0, width - v.shape[1])))


def _rel_bucket(rel):
    half = REL_BUCKETS // 2
    max_exact = half // 2
    n = jnp.abs(rel)
    large = max_exact + (jnp.log(jnp.maximum(n, 1).astype(jnp.float32) / max_exact)
                         / math.log(REL_MAX_DIST / max_exact) * (half - max_exact)).astype(jnp.int32)
    large = jnp.minimum(large, half - 1)
    return jnp.where(rel > 0, half, 0) + jnp.where(n < max_exact, n, large)


def _bias_tiles(rel_bias):
    tk = KEY_TILE
    half = REL_BUCKETS // 2
    max_exact = half // 2
    n_sat = int(math.ceil(max_exact * (REL_MAX_DIST / max_exact) ** ((half - 1 - max_exact) / (half - max_exact)))) + 2
    nd = (n_sat + 2 * tk - 2) // tk + 1
    dd = jnp.arange(nd, dtype=I32)[:, None, None]
    c = jnp.arange(tk, dtype=I32)[None, :, None]
    r = jnp.arange(tk, dtype=I32)[None, None, :]
    bucket = _rel_bucket(c - r - dd * tk)
    onehot = (bucket[..., None] == jnp.arange(REL_BUCKETS, dtype=I32)).astype(F32)
    tiles = jnp.einsum('dcrb,bh->dhcr', onehot, rel_bias.astype(F32) * LOG2E,
                       precision=lax.Precision.HIGHEST)
    return tiles


def _prep_proj(norm1_g, w_in, q_norm_g, k_norm_g, idx_k_norm_g, idx_k_norm_b, d_model):
    ssm_w = d_model // 2
    attn_w = N_HEADS * HEAD_DIM
    kv = N_KV_HEADS * HEAD_DIM
    sizes = [ssm_w, attn_w, kv, kv, IDX_HEADS * IDX_DIM, IDX_DIM, IDX_HEADS, d_model, d_model]
    pts = np.cumsum(sizes)[:-1].tolist()
    wu, wq, wk, wv, wqi, wki, wwi, wga, wgb = jnp.split(w_in, pts, axis=1)
    bf = lambda a: a.astype(BF16)
    blk = np.kron(np.eye(N_KV_HEADS), np.ones((HEAD_DIM, HEAD_DIM))) / HEAD_DIM
    wwit = jnp.pad(wwi.T, ((0, 2 * SUBLANES - IDX_HEADS), (0, 0)))
    return dict(
        g1=norm1_g.reshape(1, -1).astype(F32),
        wu=bf(wu), wq=bf(_pad_heads(wq, N_HEADS, HEAD_DIM)), wk=bf(_pad_heads(wk, N_KV_HEADS, HEAD_DIM)),
        wv=bf(_pad_heads(wv, N_KV_HEADS, HEAD_DIM)), wqi=bf(_pad_heads(wqi, IDX_HEADS, IDX_DIM)),
        wki=bf(_pad_heads(wki, 1, IDX_DIM)), wwit=bf(wwit),
        wga=bf(wga), wgb=bf(wgb), wkc=bf(wk), wvc=bf(wv),
        gq=_pad_lanes(q_norm_g.astype(F32)), gk=_pad_lanes(k_norm_g.astype(F32)),
        gkc=jnp.tile(k_norm_g.astype(F32), N_KV_HEADS).reshape(1, -1),
        gi=_pad_lanes(idx_k_norm_g.astype(F32)), bi=_pad_lanes(idx_k_norm_b.astype(F32)),
        ones_h=jnp.full((LANES, LANES), 1.0 / HEAD_DIM, BF16),
        ones_c=jnp.asarray(blk, BF16),
    )


def _prep_s5(lre, lim, log_dt, b_re, b_im, c_re, c_im, dvec, wa, wb):
    g, p = lre.shape
    ch = b_re.shape[-1]
    lam = lax.complex(lre.astype(F32), lim.astype(F32))
    dt = jnp.exp(log_dt.astype(F32))[:, None]
    a_bar = jnp.exp(lam * dt)
    b_bar = ((a_bar - 1.0) / lam)[:, :, None] * lax.complex(b_re.astype(F32), b_im.astype(F32))
    gs = g // S5_DIAG
    eye = jnp.eye(gs, dtype=F32)

    def blocks_in(m):
        return jnp.einsum('jgpc,gh->jgchp', m.reshape(S5_DIAG, gs, p, ch), eye).reshape(S5_DIAG, gs * ch, gs * p)

    def blocks_out(m):
        return jnp.einsum('jgcp,gh->jgphc', m.reshape(S5_DIAG, gs, ch, p), eye).reshape(S5_DIAG, gs * p, gs * ch)

    return dict(
        b_re=blocks_in(jnp.real(b_bar)).astype(BF16), b_im=blocks_in(jnp.imag(b_bar)).astype(BF16),
        c_re=blocks_out(c_re.astype(F32)).astype(BF16), c_im=blocks_out(-c_im.astype(F32)).astype(BF16),
        a_re=jnp.real(a_bar).reshape(1, g * p), a_im=jnp.imag(a_bar).reshape(1, g * p),
        d=dvec.reshape(1, -1).astype(F32), wa=wa.astype(BF16), wb=wb.astype(BF16),
    )


def _prep_merge(w_attn_up, w_out, norm2_g, w_router, b_router):
    d = w_attn_up.shape[1]
    wup = jnp.pad(w_attn_up.reshape(N_HEADS, HEAD_DIM, d), ((0, 0), (0, LANES - HEAD_DIM), (0, 0)))
    wr_t = w_router.astype(F32).T
    wr_hi = wr_t.astype(BF16)
    wr_lo = (wr_t - wr_hi.astype(F32)).astype(BF16)
    return dict(
        wup=wup.reshape(N_HEADS * LANES, d).astype(BF16), wout=w_out.astype(BF16),
        g2=norm2_g.reshape(1, -1).astype(F32), wr_hi=wr_hi, wr_lo=wr_lo,
        br=jnp.broadcast_to(b_router.astype(F32)[:, None], (b_router.shape[0], LANES)),
    )


def _prep_moe(wg, bg, wu, bu, wd, bd):
    return dict(wg=wg.astype(BF16), wu=wu.astype(BF16), wd=wd.astype(BF16),
                bg=bg.astype(F32)[:, None, :], bu=bu.astype(F32)[:, None, :], bd=bd.astype(F32)[:, None, :])


def _pick_tile(n, pref):
    t = min(n, pref)
    while n % t:
        t //= 2
    return t


def _pad_axis(a, axis, size):
    pad = [(0, 0)] * a.ndim
    pad[axis] = (0, size - a.shape[axis])
    return jnp.pad(a, pad)


def _trunk_layer(x, past_k, past_v, past_ik, h0_re, h0_im, pw, sw, mw, ew, bias_tiles):
    bsz, seq, d = x.shape
    t = bsz * seq
    tm = _pick_tile(seq, 512)
    u_tb, q, kp, vp, qi, kip, wt, sga, sgb, kc, vc, kic = _proj(x, pw, bsz, seq, tm)

    half = sw['a_re'].shape[1]
    if h0_re is None:
        h0 = jnp.zeros((bsz, 2 * half), F32)
    else:
        h0 = jnp.concatenate([h0_re.reshape(bsz, half), h0_im.reshape(bsz, half)], axis=1).astype(F32)
    tc = _pick_tile(seq, max(1, 512 // bsz))
    ya, hout = _s5(u_tb, h0, sw, bsz, seq, tc)
    groups = half // SSM_STATE
    s_re = hout[:, :half].reshape(bsz, groups, SSM_STATE)
    s_im = hout[:, half:].reshape(bsz, groups, SSM_STATE)

    past = 0 if past_k is None else past_k.shape[1]
    n_keys = past + seq
    lk = -(-n_keys // KEY_BLOCK) * KEY_BLOCK
    kip3 = kip.reshape(bsz, seq, LANES)
    if past:
        lane = jnp.arange(LANES)
        pk = jnp.pad(past_k.astype(F32), ((0, 0), (0, 0), (0, 0), (0, LANES - HEAD_DIM)))
        pk = jnp.where(lane == HEAD_DIM, 1.0, pk).astype(BF16)
        pv = jnp.pad(past_v.astype(F32), ((0, 0), (0, 0), (0, 0), (0, LANES - HEAD_DIM)))
        pv = jnp.where(lane == HEAD_DIM, 1.0, pv).astype(BF16)
        pik = jnp.pad(past_ik.astype(F32), ((0, 0), (0, 0), (0, LANES - IDX_DIM))).astype(BF16)
        k_all = jnp.concatenate([pk.transpose(0, 2, 1, 3), kp], axis=2)
        v_all = jnp.concatenate([pv.transpose(0, 2, 1, 3), vp], axis=2)
        ki_all = jnp.concatenate([pik, kip3], axis=1)
    else:
        k_all, v_all, ki_all = kp, vp, kip3
    k_all = _pad_axis(k_all, 2, lk)
    v_all = _pad_axis(v_all, 2, lk)
    ki_all = _pad_axis(ki_all, 1, lk)
    vt_all = v_all.reshape(bsz, N_KV_HEADS, lk // KEY_BLOCK, KEY_BLOCK, LANES).transpose(0, 1, 2, 4, 3)
    tq = KEY_TILE
    seq_q = -(-seq // tq) * tq
    q_p, qi_p, wt_p = _pad_axis(q, 2, seq_q), _pad_axis(qi, 2, seq_q), _pad_axis(wt, 2, seq_q)
    kf = k_all[..., :HEAD_DIM].astype(F32)
    kmax = jnp.sqrt(jnp.max(jnp.sum(kf * kf, axis=-1), axis=-1)).reshape(-1)
    bfar = bias_tiles[-1, :, 0, 0]
    bmax = jnp.max(jnp.abs(bias_tiles), axis=(0, 2, 3))
    stats = jnp.concatenate([kmax, bmax, bfar]).astype(F32)
    bias_tiles = bias_tiles - bfar[None, :, None, None]
    attn = _dsa(stats, q_p, qi_p, wt_p, k_all, vt_all, ki_all, bias_tiles, bsz, seq_q, past, n_keys, tq)
    attn = attn[:, :seq]

    moe_tile = _pick_tile(t, MOE_TILE)
    tm2 = _pick_tile(moe_tile, 512)
    x1, h2, gt, rt, cnt = _merge(x.reshape(t, d), ya, attn.reshape(t, attn.shape[-1]), sga, sgb, mw, tm2, moe_tile)
    cnt_i = cnt[:, :, 0].astype(I32).reshape(-1)
    y = _moe(h2, x1, gt, rt, cnt_i, ew, moe_tile)

    k_new = kc.reshape(bsz, seq, N_KV_HEADS, HEAD_DIM)
    v_new = vc.reshape(bsz, seq, N_KV_HEADS, HEAD_DIM)
    ik_new = kic.reshape(bsz, seq, IDX_DIM)
    return y.reshape(bsz, seq, d), k_new, v_new, ik_new, s_re, s_im


def kernel(x_prompt, x_sample, cache_k, cache_v, cache_idx_k, state_ssm_re, state_ssm_im, rel_bias, norm1_g, w_in, ssm_lambda_re, ssm_lambda_im, ssm_log_dt, ssm_b_re, ssm_b_im, ssm_c_re, ssm_c_im, ssm_d, ssm_w_glu_a, ssm_w_glu_b, q_norm_g, k_norm_g, idx_k_norm_g, idx_k_norm_b, w_attn_up, w_out, norm2_g, moe_w_router, moe_b_router, moe_w_gate, moe_b_gate, moe_w_up, moe_b_up, moe_w_down, moe_b_down):
    depth = w_in.shape[0]
    d_model = x_prompt.shape[-1]
    bias_tiles = _bias_tiles(rel_bias)
    xp, xs = x_prompt, x_sample
    st_p, st_s = [], []
    for l in range(depth):
        pw = _prep_proj(norm1_g[l], w_in[l], q_norm_g[l], k_norm_g[l], idx_k_norm_g[l], idx_k_norm_b[l], d_model)
        sw = _prep_s5(ssm_lambda_re[l], ssm_lambda_im[l], ssm_log_dt[l], ssm_b_re[l], ssm_b_im[l], ssm_c_re[l],
                      ssm_c_im[l], ssm_d[l], ssm_w_glu_a[l], ssm_w_glu_b[l])
        mw = _prep_merge(w_attn_up[l], w_out[l], norm2_g[l], moe_w_router[l], moe_b_router[l])
        ew = _prep_moe(moe_w_gate[l], moe_b_gate[l], moe_w_up[l], moe_b_up[l], moe_w_down[l], moe_b_down[l])
        xp, *sp = _trunk_layer(xp, None, None, None, None, None, pw, sw, mw, ew, bias_tiles)
        xs, *ss = _trunk_layer(xs, cache_k[l], cache_v[l], cache_idx_k[l], state_ssm_re[l], state_ssm_im[l],
                               pw, sw, mw, ew, bias_tiles)
        st_p.append(sp)
        st_s.append(ss)
    outs_p = [jnp.stack([s[i] for s in st_p]) for i in range(5)]
    outs_s = [jnp.stack([s[i] for s in st_s]) for i in range(5)]
    return (xp, xs, *outs_p, *outs_s)
```

```python
import functools
import math

import numpy as np
import jax
import jax.numpy as jnp
from jax import lax
from jax.experimental import pallas as pl
from jax.experimental.pallas import tpu as pltpu

F32 = jnp.float32
BF16 = jnp.bfloat16
I32 = jnp.int32

LANES = 128
SUBLANES = 8
VMEM_LIMIT = 56 * 1024 * 1024

CHUNK = 64
SSM_GROUP_CH = 16
SSM_STATE = 64
N_HEADS = 8
HEAD_DIM = 64
N_KV_HEADS = 2
KV_REP = N_HEADS // N_KV_HEADS
IDX_HEADS = 8
IDX_DIM = 64
TOPK_MAX = 256
REL_BUCKETS = 32
REL_MAX_DIST = 1024
N_EXPERTS = 32
TOP_K = 4
SWIGLU_LIMIT = 7.0
SWIGLU_ALPHA = 1.702
EPS = 1e-6

KEY_TILE = 128
KEY_BLOCK = 256
SCORE_CHUNK = 4
ATTEND_CHUNK = 4
LOG2E = math.log2(math.e)
NEG_BIG = -1e30
SHIFT_LIMIT = 30.0
S5_DIAG = 2
SEARCH_FIRST = 8
SEARCH_GROUP = 4
MOE_TILE = 1024
MOE_ROWS = 160
MOE_GROUP = 8
MOE_PAIR = 2


def _cp(sem):
    return pltpu.CompilerParams(dimension_semantics=sem, vmem_limit_bytes=VMEM_LIMIT)


def _dot(a, b):
    return jnp.dot(a, b, preferred_element_type=F32)


def _dot_nt(a, b):
    return lax.dot_general(a, b, (((1,), (1,)), ((), ())), preferred_element_type=F32)


def _dot_tn(a, b):
    return lax.dot_general(a, b, (((0,), (0,)), ((), ())), preferred_element_type=F32)


def _split(a):
    hi = a.astype(BF16)
    lo = (a - hi.astype(F32)).astype(BF16)
    return hi, lo


def _dot_split(a, g):
    hi, lo = _split(a)
    return _dot(hi, g) + _dot(lo, g)


def _proj_kernel(x_ref, g1_ref, wu_ref, wq_ref, wk_ref, wv_ref, wqi_ref, wki_ref, wwit_ref, wga_ref, wgb_ref,
                 wkc_ref, wvc_ref, gq_ref, gk_ref, gkc_ref, gi_ref, bi_ref, ones_h_ref, ones_c_ref,
                 u_ref, q_ref, kp_ref, vp_ref, qi_ref, kip_ref, wt_ref, sga_ref, sgb_ref,
                 kc_ref, vc_ref, kic_ref):
    x = x_ref[...]
    ms = jnp.mean(x * x, axis=-1, keepdims=True)
    hn = (x * lax.rsqrt(ms + EPS) * g1_ref[...]).astype(BF16)
    ones_h = ones_h_ref[...]
    lane = lax.broadcasted_iota(I32, (x.shape[0], LANES), 1)

    u_ref[...] = _dot(hn, wu_ref[...]).astype(BF16)

    q = _dot(hn, wq_ref[...])
    scale = HEAD_DIM ** -0.5 * LOG2E
    for h in range(N_HEADS):
        qh = q[:, h * LANES:(h + 1) * LANES]
        msq = _dot_split(qh * qh, ones_h)
        q_ref[h] = (qh * lax.rsqrt(msq + EPS) * (gq_ref[...] * scale)).astype(BF16)

    k = _dot(hn, wk_ref[...])
    for g in range(N_KV_HEADS):
        kg = k[:, g * LANES:(g + 1) * LANES]
        msk = _dot_split(kg * kg, ones_h)
        kn = kg * lax.rsqrt(msk + EPS) * gk_ref[...]
        kp_ref[g] = jnp.where(lane == HEAD_DIM, 1.0, kn).astype(BF16)

    v = _dot(hn, wv_ref[...])
    for g in range(N_KV_HEADS):
        vg = v[:, g * LANES:(g + 1) * LANES]
        vp_ref[g] = jnp.where(lane == HEAD_DIM, 1.0, vg).astype(BF16)

    qi = _dot(hn, wqi_ref[...])
    for h in range(IDX_HEADS):
        qi_ref[h] = qi[:, h * LANES:(h + 1) * LANES].astype(BF16)

    ki = _dot(hn, wki_ref[...])
    mu = _dot_split(ki, ones_h)
    xc = jnp.where(lane < IDX_DIM, ki - mu, 0.0)
    var = _dot_split(xc * xc, ones_h)
    kin = xc * lax.rsqrt(var + EPS) * gi_ref[...] + bi_ref[...]
    kip_ref[...] = kin.astype(BF16)
    kic_ref[...] = kin[:, :IDX_DIM]

    wt = _dot_nt(wwit_ref[...], hn)
    wt_ref[...] = wt[0:IDX_HEADS, :] * (IDX_HEADS ** -0.5 * IDX_DIM ** -0.5)

    sga_ref[...] = jax.nn.sigmoid(_dot(hn, wga_ref[...])).astype(BF16)
    sgb_ref[...] = jax.nn.sigmoid(_dot(hn, wgb_ref[...])).astype(BF16)

    kc = _dot(hn, wkc_ref[...])
    mskc = _dot_split(kc * kc, ones_c_ref[...])
    kc_ref[...] = kc * lax.rsqrt(mskc + EPS) * gkc_ref[...]
    vc_ref[...] = _dot(hn, wvc_ref[...])


def _proj(x, pw, bsz, seq, tm):
    d = x.shape[-1]
    nt = seq // tm
    t = bsz * seq
    x2 = x.reshape(t, d)

    def tok(b, i):
        return (b * nt + i, 0)

    def cst(b, i):
        return (0, 0)

    def wspec(a):
        return pl.BlockSpec(a.shape, cst)

    weights = [pw['g1'], pw['wu'], pw['wq'], pw['wk'], pw['wv'], pw['wqi'], pw['wki'], pw['wwit'], pw['wga'],
               pw['wgb'], pw['wkc'], pw['wvc'], pw['gq'], pw['gk'], pw['gkc'], pw['gi'], pw['bi'],
               pw['ones_h'], pw['ones_c']]
    ssm_w = pw['wu'].shape[1]
    out_shape = (
        jax.ShapeDtypeStruct((t, ssm_w), BF16),
        jax.ShapeDtypeStruct((bsz, N_HEADS, seq, LANES), BF16),
        jax.ShapeDtypeStruct((bsz, N_KV_HEADS, seq, LANES), BF16),
        jax.ShapeDtypeStruct((bsz, N_KV_HEADS, seq, LANES), BF16),
        jax.ShapeDtypeStruct((bsz, IDX_HEADS, seq, LANES), BF16),
        jax.ShapeDtypeStruct((t, LANES), BF16),
        jax.ShapeDtypeStruct((bsz, IDX_HEADS, seq), F32),
        jax.ShapeDtypeStruct((t, d), BF16),
        jax.ShapeDtypeStruct((t, d), BF16),
        jax.ShapeDtypeStruct((t, N_KV_HEADS * HEAD_DIM), F32),
        jax.ShapeDtypeStruct((t, N_KV_HEADS * HEAD_DIM), F32),
        jax.ShapeDtypeStruct((t, IDX_DIM), F32),
    )

    def hm(nh):
        return pl.BlockSpec((None, nh, tm, LANES), lambda b, i: (b, 0, i, 0))

    out_specs = (
        pl.BlockSpec((tm, ssm_w), tok),
        hm(N_HEADS), hm(N_KV_HEADS), hm(N_KV_HEADS), hm(IDX_HEADS),
        pl.BlockSpec((tm, LANES), tok),
        pl.BlockSpec((None, IDX_HEADS, tm), lambda b, i: (b, 0, i)),
        pl.BlockSpec((tm, d), tok), pl.BlockSpec((tm, d), tok),
        pl.BlockSpec((tm, N_KV_HEADS * HEAD_DIM), tok), pl.BlockSpec((tm, N_KV_HEADS * HEAD_DIM), tok),
        pl.BlockSpec((tm, IDX_DIM), tok),
    )
    return pl.pallas_call(
        _proj_kernel,
        grid=(bsz, nt),
        in_specs=[pl.BlockSpec((tm, d), tok)] + [wspec(a) for a in weights],
        out_specs=out_specs,
        out_shape=out_shape,
        compiler_params=_cp(("arbitrary", "arbitrary")),
        name="proj",
    )(x2, *weights)


def _gelu_tanh(x):
    return 0.5 * x * (1.0 + jnp.tanh(math.sqrt(2.0 / math.pi) * (x + 0.044715 * (x * x * x))))


def _s5_kernel(u_ref, h0_ref, bre_ref, bim_ref, are_ref, aim_ref, cre_ref, cim_ref, dvec_ref, wa_ref, wb_ref,
               ya_ref, hout_ref, state_ref, bu_ref, yf_ref, ug_ref, *, bsz, tc, strip):
    s = pl.program_id(0)
    half = are_ref.shape[1]

    @pl.when(s == 0)
    def _():
        state_ref[...] = h0_ref[...]

    ssm_w = dvec_ref.shape[1]
    for b in range(bsz):
        for c in range(ssm_w // LANES):
            ug_ref[c, pl.ds(b, tc, stride=bsz), :] = u_ref[b, :, c * LANES:(c + 1) * LANES].astype(F32)
    u = jnp.concatenate([ug_ref[c] for c in range(ssm_w // LANES)], axis=1).astype(BF16)
    cw = u.shape[1] // S5_DIAG
    sw = half // S5_DIAG
    for j in range(S5_DIAG):
        uj = u[:, j * cw:(j + 1) * cw]
        bu_ref[:, j * sw:(j + 1) * sw] = _dot(uj, bre_ref[j])
        bu_ref[:, half + j * sw:half + (j + 1) * sw] = _dot(uj, bim_ref[j])

    for c0 in range(0, half, strip):
        ar = jnp.broadcast_to(are_ref[:, c0:c0 + strip], (bsz, strip))
        ai = jnp.broadcast_to(aim_ref[:, c0:c0 + strip], (bsz, strip))
        hr0 = state_ref[:, c0:c0 + strip]
        hi0 = state_ref[:, half + c0:half + c0 + strip]

        def step(t, carry):
            hr, hi = carry
            r0 = pl.multiple_of(t * bsz, bsz)
            br = bu_ref[pl.ds(r0, bsz), c0:c0 + strip]
            bi = bu_ref[pl.ds(r0, bsz), half + c0:half + c0 + strip]
            nr = ar * hr - ai * hi + br
            ni = ar * hi + ai * hr + bi
            bu_ref[pl.ds(r0, bsz), c0:c0 + strip] = nr
            bu_ref[pl.ds(r0, bsz), half + c0:half + c0 + strip] = ni
            return nr, ni

        hr, hi = lax.fori_loop(0, tc, step, (hr0, hi0))
        state_ref[:, c0:c0 + strip] = hr
        state_ref[:, half + c0:half + c0 + strip] = hi

    ys = []
    for j in range(S5_DIAG):
        s_re = bu_ref[:, j * sw:(j + 1) * sw].astype(BF16)
        s_im = bu_ref[:, half + j * sw:half + (j + 1) * sw].astype(BF16)
        ys.append(_dot(s_re, cre_ref[j]) + _dot(s_im, cim_ref[j]))
    y = jnp.concatenate(ys, axis=1) + dvec_ref[...] * u.astype(F32)
    g = _gelu_tanh(y).astype(BF16)
    ya = _dot(g, wa_ref[...]) * jax.nn.sigmoid(_dot(g, wb_ref[...]))
    n_chunk = ya.shape[1] // LANES
    for c in range(n_chunk):
        yf_ref[c] = ya[:, c * LANES:(c + 1) * LANES]
    for b in range(bsz):
        ya_ref[b] = jnp.concatenate([yf_ref[c, pl.ds(b, tc, stride=bsz), :] for c in range(n_chunk)],
                                    axis=1).astype(BF16)

    @pl.when(s == pl.num_programs(0) - 1)
    def _():
        hout_ref[...] = state_ref[...]


def _s5(u, h0, sw, bsz, seq, tc):
    rows = tc * bsz
    ssm_w = sw['d'].shape[1]
    half = sw['a_re'].shape[1]
    two_half = 2 * half
    d = sw['wa'].shape[1]
    u3 = u.reshape(bsz, seq, ssm_w)
    strip = min(512, half)

    consts = [h0, sw['b_re'], sw['b_im'], sw['a_re'], sw['a_im'], sw['c_re'], sw['c_im'], sw['d'], sw['wa'], sw['wb']]

    def cst(s):
        return (0, 0)

    def cspec(a):
        return pl.BlockSpec(a.shape, lambda s: (0,) * a.ndim)

    ya, hout = pl.pallas_call(
        functools.partial(_s5_kernel, bsz=bsz, tc=tc, strip=strip),
        grid=(seq // tc,),
        in_specs=[pl.BlockSpec((bsz, tc, ssm_w), lambda s: (0, s, 0))] + [cspec(a) for a in consts],
        out_specs=(pl.BlockSpec((bsz, tc, d), lambda s: (0, s, 0)), pl.BlockSpec((bsz, two_half), cst)),
        out_shape=(jax.ShapeDtypeStruct((bsz, seq, d), BF16), jax.ShapeDtypeStruct((bsz, two_half), F32)),
        scratch_shapes=[pltpu.VMEM((bsz, two_half), F32), pltpu.VMEM((rows, two_half), F32),
                        pltpu.VMEM((d // LANES, rows, LANES), F32), pltpu.VMEM((ssm_w // LANES, rows, LANES), F32)],
        compiler_params=_cp(("arbitrary",)),
        name="s5",
    )(u3, *consts)
    return ya.reshape(bsz * seq, d), hout


def _f2key(x):
    b = lax.bitcast_convert_type(x, I32)
    return b ^ ((b >> 31) & 0x7FFFFFFF)


def _key2f(k):
    return lax.bitcast_convert_type(k ^ ((k >> 31) & 0x7FFFFFFF), F32)


def _dsa_kernel(st_ref, q_ref, qi_ref, wt_ref, k_ref, vt_ref, ki_ref, bias_ref, o_ref,
                s_ref, lo_ref, hi_ref, clo_ref, glo_ref, ghi_ref, side_ref, q2_ref, mrow_ref, acc_ref,
                *, bsz, tq, past, n_keys, topk, nkt, nd, idx_bits, near_max):
    b_id = pl.program_id(0)
    i = pl.program_id(1)
    kb = KEY_BLOCK
    sl = SUBLANES
    q0 = past + i * tq
    last_chunk = (q0 + tq - 1) // CHUNK
    n_kt = jnp.minimum(nkt, ((last_chunk + 1) * CHUNK + kb - 1) // kb)
    d0 = q0 // KEY_TILE

    krow = lax.broadcasted_iota(I32, (kb, tq), 0)
    q_chunk = (q0 + lax.broadcasted_iota(I32, (kb, tq), 1)) // CHUNK
    qc8 = (q0 + lax.broadcasted_iota(I32, (sl, tq), 1)) // CHUNK
    n_adm = jnp.minimum((qc8 + 1) * CHUNK, n_keys)
    n_admf = n_adm.astype(F32)
    is_pad = q0 + lax.broadcasted_iota(I32, (sl, tq), 1) >= n_keys
    needf = jnp.where(is_pad, n_adm, jnp.minimum(topk, n_adm)).astype(F32)

    def bcast(x):
        return jnp.broadcast_to(x[0:1, :], (kb, tq))

    def rep(x):
        return jnp.broadcast_to(x, (sl, tq))

    qi = qi_ref[...].reshape(IDX_HEADS * tq, LANES)

    def score_blocks(kt, nb, masked):
        k0 = pl.multiple_of(kt * kb, kb)
        s = _dot_nt(ki_ref[pl.ds(k0, nb * kb), :], qi)
        for j in range(nb):
            sc = jnp.zeros((kb, tq), F32)
            for h in range(IDX_HEADS):
                sc = sc + wt_ref[h:h + 1, :] * jnp.maximum(s[j * kb:(j + 1) * kb, h * tq:(h + 1) * tq], 0.0)
            if masked:
                kpos = k0 + j * kb + krow
                adm = ((kpos // CHUNK) <= q_chunk) & (kpos < n_keys)
                sc = jnp.where(adm, sc, -jnp.inf)
            s_ref[kt + j] = sc

    n_open = n_kt - 1

    def score_chunk(j, c):
        score_blocks(SCORE_CHUNK * j, SCORE_CHUNK, False)
        return c

    lax.fori_loop(0, n_open // SCORE_CHUNK, score_chunk, 0)

    def score_single(j, c):
        score_blocks(j, 1, False)
        return c

    lax.fori_loop(n_open - n_open % SCORE_CHUNK, n_open, score_single, 0)
    score_blocks(n_kt - 1, 1, True)
    s_ref[n_kt] = jnp.full((kb, tq), -jnp.inf, F32)
    n_pair = (n_kt + 1) // 2

    part = 4 * sl

    def fold(x, op):
        x = x.reshape(kb // part, part, tq)
        acc = x[0]
        for j in range(1, kb // part):
            acc = op(acc, x[j])
        return acc

    def count(pred):
        def one(kt):
            return fold(jnp.where(pred(s_ref[kt], kt), 1.0, 0.0), jnp.add)
        c = lax.fori_loop(0, n_pair, lambda j, c: c + (one(2 * j) + one(2 * j + 1)), jnp.zeros((part, tq), F32))
        return rep(jnp.sum(c, axis=0, keepdims=True))

    def minmax(j, c):
        mx, mn = c
        for kt in (2 * j, 2 * j + 1):
            s = s_ref[kt]
            mx = jnp.maximum(mx, fold(s, jnp.maximum))
            mn = jnp.minimum(mn, fold(jnp.where(s == -jnp.inf, jnp.inf, s), jnp.minimum))
        return mx, mn

    mx, mn = lax.fori_loop(0, n_pair, minmax,
                           (jnp.full((part, tq), -jnp.inf, F32), jnp.full((part, tq), jnp.inf, F32)))
    lo_ref[...] = rep(jnp.min(mn, axis=0, keepdims=True))
    hi_ref[...] = _key2f(_f2key(rep(jnp.max(mx, axis=0, keepdims=True))) + 1)
    def odds(cnt):
        c = jnp.clip(cnt, 0.5, n_admf - 0.5)
        return jnp.log((n_admf - c) / c)

    target = odds(needf - 0.5)
    clo_ref[...] = n_admf
    glo_ref[...] = target - odds(n_admf)
    ghi_ref[...] = target - odds(jnp.zeros((sl, tq), F32))
    side_ref[...] = jnp.zeros((sl, tq), F32)

    def searching(lo, hi, clo):
        return (_f2key(hi) > _f2key(lo) + 1) & (clo > needf)

    def refine(it, c):
        lo, hi, clo = lo_ref[...], hi_ref[...], clo_ref[...]
        glo, ghi, side = glo_ref[...], ghi_ref[...], side_ref[...]
        k_t = _f2key(lo + (hi - lo) * (glo / (glo - ghi)))
        t = _key2f(jnp.minimum(jnp.maximum(k_t, _f2key(lo) + 1), _f2key(hi) - 1))
        tb = bcast(t)
        cnt = count(lambda s, kt: s >= tb)
        g = target - odds(cnt)
        open_ = searching(lo, hi, clo)
        up = open_ & (cnt >= needf)
        dn = open_ & (cnt < needf)
        lo_ref[...] = jnp.where(up, t, lo)
        clo_ref[...] = jnp.where(up, cnt, clo)
        hi_ref[...] = jnp.where(dn, t, hi)
        glo_ref[...] = jnp.where(up, g, jnp.where(dn & (side < 0.0), glo * 0.5, glo))
        ghi_ref[...] = jnp.where(dn, g, jnp.where(up & (side > 0.0), ghi * 0.5, ghi))
        side_ref[...] = jnp.where(up, 1.0, jnp.where(dn, -1.0, side))
        return c

    def snap():
        lo, hi, clo = lo_ref[...], hi_ref[...], clo_ref[...]
        lo_b, hi_b = bcast(lo), bcast(hi)

        def body(j, c):
            a, b = c
            for kt in (2 * j, 2 * j + 1):
                s = s_ref[kt]
                a = jnp.minimum(a, fold(jnp.where(s >= lo_b, s, jnp.inf), jnp.minimum))
                b = jnp.maximum(b, fold(jnp.where(s < hi_b, s, -jnp.inf), jnp.maximum))
            return a, b

        a, b = lax.fori_loop(0, n_pair, body,
                             (jnp.full((part, tq), jnp.inf, F32), jnp.full((part, tq), -jnp.inf, F32)))
        open_ = searching(lo, hi, clo)
        lo_ref[...] = jnp.where(open_, rep(jnp.min(a, axis=0, keepdims=True)), lo)
        hi_ref[...] = jnp.where(open_, _key2f(_f2key(rep(jnp.max(b, axis=0, keepdims=True))) + 1), hi)

    def n_searching():
        return jnp.max(jnp.where(searching(lo_ref[...], hi_ref[...], clo_ref[...]), 1.0, 0.0))

    def group(c):
        grp, _ = c
        lax.fori_loop(0, jnp.where(grp == 0, SEARCH_FIRST, SEARCH_GROUP), refine, 0)
        snap()
        return grp + 1, n_searching()

    lax.while_loop(lambda c: c[1] > 0.0, group, (jnp.int32(0), n_searching()))
    thr = lo_ref[...]
    thr_b = bcast(thr)

    n_tied = jnp.max(jnp.where(clo_ref[...] > needf, 1.0, 0.0))

    @pl.when(n_tied > 0.0)
    def _():
        rem = needf - count(lambda s, kt: s > thr_b)
        lo_ref[...] = jnp.zeros((sl, tq), F32)
        hi_ref[...] = jnp.full((sl, tq), float(nkt * kb), F32)
        krowf = krow.astype(F32)

        def bisect_idx(it, c):
            lo = lo_ref[...]
            hi = hi_ref[...]
            mid = jnp.floor((lo + hi) * 0.5)
            mid_b = bcast(mid)
            ok = count(lambda s, kt: (s == thr_b) & ((kt * kb).astype(F32) + krowf < mid_b)) >= rem
            hi_ref[...] = jnp.where(ok, mid, hi)
            lo_ref[...] = jnp.where(ok, lo, mid)
            return c

        lax.fori_loop(0, idx_bits, bisect_idx, 0)
        cut_b = bcast(hi_ref[...])

        def drop(kt, c):
            s = s_ref[kt]
            s_ref[kt] = jnp.where((s == thr_b) & ((kt * kb).astype(F32) + krowf >= cut_b), -jnp.inf, s)
            return c

        lax.fori_loop(0, n_kt, drop, 0)

    rows_g = KV_REP * tq
    qf = q_ref[...].reshape(N_HEADS * tq, LANES).astype(F32)
    qn = jnp.sqrt(jnp.sum(qf * qf, axis=1, keepdims=True))
    lane = lax.broadcasted_iota(I32, (tq, LANES), 1)
    worst = jnp.float32(0.0)
    for h in range(N_HEADS):
        kmax = st_ref[b_id * N_KV_HEADS + h // KV_REP]
        bmax = st_ref[bsz * N_KV_HEADS + h]
        bfar = st_ref[bsz * N_KV_HEADS + N_HEADS + h]
        bound = qn[h * tq:(h + 1) * tq, :] * (kmax * 1.01) + (bmax + 0.1)
        worst = jnp.maximum(worst, jnp.max(bound))
        q2_ref[h * tq:(h + 1) * tq, :] = jnp.where(lane == HEAD_DIM, bfar - bound,
                                                   qf[h * tq:(h + 1) * tq, :]).astype(BF16)
    n_far = jnp.clip((d0 - nd) // 2 + 1, 0, n_kt)

    def logits(kt0, nb, g, near, exact):
        k0 = pl.multiple_of(kt0 * kb, kb)
        s = _dot_nt(k_ref[g, pl.ds(k0, nb * kb), :], q2_ref[g * rows_g:(g + 1) * rows_g, :])
        out = []
        for j in range(nb):
            kt = kt0 + j
            maskadd = jnp.where(s_ref[kt] >= thr_b, 0.0, NEG_BIG)
            if near:
                da = jnp.clip(d0 - 2 * kt, 0, nd - 1)
                db = jnp.clip(d0 - 2 * kt - 1, 0, nd - 1)
            parts = []
            for r in range(KV_REP):
                h = g * KV_REP + r
                add = maskadd - mrow_ref[0:1, h * tq:(h + 1) * tq] if exact else maskadd
                if near:
                    add = jnp.concatenate([bias_ref[da, h], bias_ref[db, h]], axis=0) + add
                parts.append(s[j * kb:(j + 1) * kb, r * tq:(r + 1) * tq] + add)
            out.append(jnp.concatenate(parts, axis=1))
        return out

    def over_blocks(fn):
        def far_chunk(j, c):
            fn(ATTEND_CHUNK * j, ATTEND_CHUNK, False)
            return c
        lax.fori_loop(0, n_far // ATTEND_CHUNK, far_chunk, 0)

        def far_single(kt, c):
            fn(kt, 1, False)
            return c
        lax.fori_loop(n_far - n_far % ATTEND_CHUNK, n_far, far_single, 0)

        n_near = n_kt - n_far
        for nb in range(1, near_max + 1):
            @pl.when(n_near == nb)
            def _(nb=nb):
                fn(n_far, nb, True)

    def attend(exact):
        acc_ref[...] = jnp.zeros(acc_ref.shape, F32)

        def blocks(kt0, nb, near):
            for g in range(N_KV_HEADS):
                p = jnp.concatenate([jnp.exp2(lg).astype(BF16) for lg in logits(kt0, nb, g, near, exact)], axis=0)
                vt = jnp.concatenate([vt_ref[g, kt0 + j] for j in range(nb)], axis=1)
                acc_ref[g] += _dot(vt, p)
        over_blocks(blocks)

    @pl.when(worst <= SHIFT_LIMIT)
    def _():
        attend(False)

    @pl.when(worst > SHIFT_LIMIT)
    def _():
        mrow_ref[...] = jnp.full(mrow_ref.shape, NEG_BIG, F32)

        def blocks(kt0, nb, near):
            for g in range(N_KV_HEADS):
                for lg in logits(kt0, nb, g, near, False):
                    mx = jnp.max(lg, axis=0, keepdims=True)
                    cur = mrow_ref[:, g * rows_g:(g + 1) * rows_g]
                    mrow_ref[:, g * rows_g:(g + 1) * rows_g] = jnp.maximum(cur, jnp.broadcast_to(mx, (sl, rows_g)))
        over_blocks(blocks)
        attend(True)

    for g in range(N_KV_HEADS):
        acc = acc_ref[g]
        og = acc / acc[HEAD_DIM:HEAD_DIM + 1, :]
        for r in range(KV_REP):
            h = g * KV_REP + r
            o_ref[:, h * LANES:(h + 1) * LANES] = og[:, r * tq:(r + 1) * tq].T.astype(BF16)


def _dsa(stats, q, qi, wt, k_all, vt_all, ki_all, bias_tiles, bsz, seq, past, n_keys, tq):
    lk = k_all.shape[2]
    nkt = lk // KEY_BLOCK
    topk = min(TOPK_MAX, n_keys // 4)
    nd = bias_tiles.shape[0]
    nq = seq // tq
    assert past % KEY_TILE == 0 and tq == KEY_TILE
    idx_bits = int(math.ceil(math.log2(lk))) + 1
    near_max = 0
    for i in range(nq):
        q0 = past + i * tq
        n_kt = min(nkt, (((q0 + tq - 1) // CHUNK + 1) * CHUNK + KEY_BLOCK - 1) // KEY_BLOCK)
        n_far = min(max((q0 // KEY_TILE - nd) // 2 + 1, 0), n_kt)
        near_max = max(near_max, n_kt - n_far)
    kern = functools.partial(_dsa_kernel, bsz=bsz, tq=tq, past=past, n_keys=n_keys, topk=topk, nkt=nkt, nd=nd,
                             idx_bits=idx_bits, near_max=near_max)
    row_state = pltpu.VMEM((SUBLANES, tq), F32)
    grid_spec = pltpu.PrefetchScalarGridSpec(
        num_scalar_prefetch=1,
        grid=(bsz, nq),
        in_specs=[
            pl.BlockSpec((None, N_HEADS, tq, LANES), lambda b, i, s: (b, 0, i, 0)),
            pl.BlockSpec((None, IDX_HEADS, tq, LANES), lambda b, i, s: (b, 0, i, 0)),
            pl.BlockSpec((None, IDX_HEADS, tq), lambda b, i, s: (b, 0, i)),
            pl.BlockSpec((None, N_KV_HEADS, lk, LANES), lambda b, i, s: (b, 0, 0, 0)),
            pl.BlockSpec((None, N_KV_HEADS, nkt, LANES, KEY_BLOCK), lambda b, i, s: (b, 0, 0, 0, 0)),
            pl.BlockSpec((None, lk, LANES), lambda b, i, s: (b, 0, 0)),
            pl.BlockSpec(bias_tiles.shape, lambda b, i, s: (0, 0, 0, 0)),
        ],
        out_specs=pl.BlockSpec((None, tq, N_HEADS * LANES), lambda b, i, s: (b, i, 0)),
        scratch_shapes=[
            pltpu.VMEM((nkt + 1, KEY_BLOCK, tq), F32),
            row_state, row_state, row_state, row_state, row_state, row_state,
            pltpu.VMEM((N_HEADS * tq, LANES), BF16),
            pltpu.VMEM((SUBLANES, N_HEADS * tq), F32),
            pltpu.VMEM((N_KV_HEADS, LANES, KV_REP * tq), F32),
        ],
    )
    return pl.pallas_call(
        kern,
        grid_spec=grid_spec,
        out_shape=jax.ShapeDtypeStruct((bsz, seq, N_HEADS * LANES), BF16),
        compiler_params=_cp(("arbitrary", "arbitrary")),
        name="dsa",
    )(stats, q, qi, wt, k_all, vt_all, ki_all, bias_tiles)


def _merge_kernel(x_ref, ya_ref, at_ref, sga_ref, sgb_ref, wup_ref, wout_ref, g2_ref, wr_hi_ref, wr_lo_ref, br_ref,
                  x1_ref, h2_ref, gt_ref, rt_ref, cnt_ref, run_ref, *, tm, sub):
    step = pl.program_id(0)

    @pl.when(step % sub == 0)
    def _():
        run_ref[...] = jnp.zeros(run_ref.shape, F32)

    yb = _dot(at_ref[...], wup_ref[...])
    merged = sga_ref[...].astype(F32) * ya_ref[...].astype(F32) + sgb_ref[...].astype(F32) * yb
    x1 = x_ref[...] + _dot(merged.astype(BF16), wout_ref[...])
    x1_ref[...] = x1
    ms = jnp.mean(x1 * x1, axis=-1, keepdims=True)
    h2 = x1 * lax.rsqrt(ms + EPS) * g2_ref[...]
    h2_hi, h2_lo = _split(h2)
    h2_ref[...] = h2_hi

    wr_hi = wr_hi_ref[...]
    logit = (_dot_nt(wr_hi, h2_hi) + _dot_nt(wr_hi, h2_lo) + _dot_nt(wr_lo_ref[...], h2_hi)) + br_ref[:, 0:1]
    ne = logit.shape[0]
    eid = lax.broadcasted_iota(I32, (ne, tm), 0).astype(F32)
    selb = jnp.zeros((ne, tm), F32)
    tops = []
    picks = []
    for _ in range(TOP_K):
        mx = jnp.max(logit, axis=0, keepdims=True)
        pick = jnp.min(jnp.where(logit == mx, eid, float(ne)), axis=0, keepdims=True)
        hit = eid == pick
        selb = jnp.where(hit, 1.0, selb)
        logit = jnp.where(hit, -jnp.inf, logit)
        tops.append(mx)
        picks.append(hit)
    ex = [jnp.exp(t - tops[0]) for t in tops]
    den = ex[0] + ex[1] + ex[2] + ex[3]
    gate = jnp.zeros((ne, tm), F32)
    for hit, e in zip(picks, ex):
        gate = jnp.where(hit, e / den, gate)
    gt_ref[...] = gate

    sel = selb > 0.5
    selb = selb.astype(BF16)
    r_i = lax.broadcasted_iota(I32, (tm, tm), 0)
    c_i = lax.broadcasted_iota(I32, (tm, tm), 1)
    tri = jnp.where(r_i < c_i, 1.0, 0.0).astype(BF16)
    run = run_ref[...]
    rank = _dot(selb, tri) + jnp.broadcast_to(run[:, 0:1], (ne, tm))
    rt_ref[...] = jnp.where(sel, rank, -1.0)
    run = run + _dot(selb, jnp.ones((tm, LANES), BF16))
    run_ref[...] = run
    cnt_ref[...] = run


def _merge(x2, ya, attn, sga, sgb, mw, tm, moe_tile):
    t, d = x2.shape
    sub = moe_tile // tm
    ne = mw['wr_hi'].shape[0]

    def tok(i):
        return (i, 0)

    def cst(i):
        return (0, 0)

    consts = [mw['wup'], mw['wout'], mw['g2'], mw['wr_hi'], mw['wr_lo'], mw['br']]
    return pl.pallas_call(
        functools.partial(_merge_kernel, tm=tm, sub=sub),
        grid=(t // tm,),
        in_specs=[
            pl.BlockSpec((tm, d), tok),
            pl.BlockSpec((tm, d), tok),
            pl.BlockSpec((tm, attn.shape[-1]), tok),
            pl.BlockSpec((tm, d), tok),
            pl.BlockSpec((tm, d), tok),
        ] + [pl.BlockSpec(a.shape, cst) for a in consts],
        out_specs=(
            pl.BlockSpec((tm, d), tok),
            pl.BlockSpec((tm, d), tok),
            pl.BlockSpec((ne, tm), lambda i: (0, i)),
            pl.BlockSpec((ne, tm), lambda i: (0, i)),
            pl.BlockSpec((None, ne, LANES), lambda i: (i // sub, 0, 0)),
        ),
        out_shape=(
            jax.ShapeDtypeStruct((t, d), F32),
            jax.ShapeDtypeStruct((t, d), BF16),
            jax.ShapeDtypeStruct((ne, t), F32),
            jax.ShapeDtypeStruct((ne, t), F32),
            jax.ShapeDtypeStruct((t // moe_tile, ne, LANES), F32),
        ),
        scratch_shapes=[pltpu.VMEM((ne, LANES), F32)],
        compiler_params=_cp(("arbitrary",)),
        name="merge",
    )(x2, ya, attn, sga, sgb, *consts)


def _moe_kernel(cnt_ref, h2_ref, x1_hbm, gt_ref, rt_ref, wg_ref, wu_ref, wd_ref, bg_ref, bu_ref, bd_ref, y_ref,
                pg_ref, og_ref, sem, *, tt, pair):
    j = pl.program_id(0)
    e = pl.program_id(1)
    ne = pl.num_programs(1)
    rb = MOE_ROWS
    slot = e % MOE_GROUP

    @pl.when(e == 0)
    def _():
        cp = pltpu.make_async_copy(x1_hbm.at[pl.ds(pl.multiple_of(j * pair * tt, tt), pair * tt), :], y_ref, sem)
        cp.start()
        cp.wait()

    mine = lax.broadcasted_iota(I32, (SUBLANES, pair * tt), 0) == e % SUBLANES
    g_all = jnp.sum(jnp.where(mine, gt_ref[...], 0.0), axis=0, keepdims=True)
    r_all = jnp.sum(jnp.where(mine, rt_ref[...], 0.0), axis=0, keepdims=True)
    rid = lax.broadcasted_iota(I32, (rb, tt), 0).astype(F32)

    def one_hot(s, blk):
        return jnp.broadcast_to(r_all[:, s * tt:(s + 1) * tt], (rb, tt)) == (rid + (blk * rb).astype(F32))

    def gather(s, hit):
        p = jnp.where(hit, 1.0, 0.0).astype(BF16)
        return p, _dot(p, h2_ref[s * tt:(s + 1) * tt, :]).astype(BF16)

    def expert(xg):
        a = jnp.minimum(_dot(xg, wg_ref[0]) + bg_ref[0], SWIGLU_LIMIT)
        b = jnp.clip(_dot(xg, wu_ref[0]) + bu_ref[0], -SWIGLU_LIMIT, SWIGLU_LIMIT)
        hid = a * jax.nn.sigmoid(SWIGLU_ALPHA * a) * (b + 1.0)
        return _dot(hid.astype(BF16), wd_ref[0]) + bd_ref[0]

    def gated(s, hit, o):
        g_row = jnp.broadcast_to(g_all[:, s * tt:(s + 1) * tt], (rb, tt))
        return (o * jnp.sum(jnp.where(hit, g_row, 0.0), axis=1, keepdims=True)).astype(BF16)

    hits = [one_hot(s, jnp.int32(0)) for s in range(pair)]
    gathered = [gather(s, hits[s]) for s in range(pair)]
    o = expert(jnp.concatenate([xg for _, xg in gathered], axis=0))
    r0 = pl.multiple_of(slot * rb, rb)
    for s in range(pair):
        pg_ref[s, pl.ds(r0, rb), :] = gathered[s][0]
        og_ref[s, pl.ds(r0, rb), :] = gated(s, hits[s], o[s * rb:(s + 1) * rb])

    @pl.when(slot == MOE_GROUP - 1)
    def _():
        for s in range(pair):
            y_ref[s * tt:(s + 1) * tt, :] += _dot_tn(pg_ref[s], og_ref[s])

    for s in range(pair):
        def overflow(blk, c, s=s):
            hit = one_hot(s, blk)
            p, xg = gather(s, hit)
            y_ref[s * tt:(s + 1) * tt, :] += _dot_tn(p, gated(s, hit, expert(xg)))
            return c

        n_rows = cnt_ref[(j * pair + s) * ne + e]
        lax.fori_loop(1, (n_rows + rb - 1) // rb, overflow, 0)


def _moe(h2, x1, gt, rt, cnt, ew, tt):
    t, d = h2.shape
    ne = gt.shape[0]
    nt = t // tt
    f = ew['wg'].shape[-1]
    pair = MOE_PAIR if nt % MOE_PAIR == 0 else 1
    grid_spec = pltpu.PrefetchScalarGridSpec(
        num_scalar_prefetch=1,
        grid=(nt // pair, ne),
        in_specs=[
            pl.BlockSpec((pair * tt, d), lambda j, e, c: (j, 0)),
            pl.BlockSpec(memory_space=pl.ANY),
            pl.BlockSpec((SUBLANES, pair * tt), lambda j, e, c: (e // SUBLANES, j)),
            pl.BlockSpec((SUBLANES, pair * tt), lambda j, e, c: (e // SUBLANES, j)),
            pl.BlockSpec((1, d, f), lambda j, e, c: (e, 0, 0)),
            pl.BlockSpec((1, d, f), lambda j, e, c: (e, 0, 0)),
            pl.BlockSpec((1, f, d), lambda j, e, c: (e, 0, 0)),
            pl.BlockSpec((1, 1, f), lambda j, e, c: (e, 0, 0)),
            pl.BlockSpec((1, 1, f), lambda j, e, c: (e, 0, 0)),
            pl.BlockSpec((1, 1, d), lambda j, e, c: (e, 0, 0)),
        ],
        out_specs=pl.BlockSpec((pair * tt, d), lambda j, e, c: (j, 0)),
        scratch_shapes=[pltpu.VMEM((pair, MOE_GROUP * MOE_ROWS, tt), BF16),
                        pltpu.VMEM((pair, MOE_GROUP * MOE_ROWS, d), BF16),
                        pltpu.SemaphoreType.DMA(())],
    )
    assert ne % MOE_GROUP == 0
    return pl.pallas_call(
        functools.partial(_moe_kernel, tt=tt, pair=pair),
        grid_spec=grid_spec,
        out_shape=jax.ShapeDtypeStruct((t, d), F32),
        compiler_params=_cp(("arbitrary", "arbitrary")),
        name="moe",
    )(cnt, h2, x1, gt, rt, ew['wg'], ew['wu'], ew['wd'], ew['bg'], ew['bu'], ew['bd'])


def _pad_heads(wmat, n_heads, width):
    d = wmat.shape[0]
    w3 = wmat.reshape(d, n_heads, width)
    return jnp.pad(w3, ((0, 0), (0, 0), (0, LANES - width))).reshape(d, n_heads * LANES)


def _pad_lanes(v, width=LANES):
    v = v.reshape(1, -1)
    return jnp.pad(v, ((0, 0), (0, width - v.shape[1])))


def _rel_bucket(rel):
    half = REL_BUCKETS // 2
    max_exact = half // 2
    n = jnp.abs(rel)
    large = max_exact + (jnp.log(jnp.maximum(n, 1).astype(jnp.float32) / max_exact)
                         / math.log(REL_MAX_DIST / max_exact) * (half - max_exact)).astype(jnp.int32)
    large = jnp.minimum(large, half - 1)
    return jnp.where(rel > 0, half, 0) + jnp.where(n < max_exact, n, large)


def _bias_tiles(rel_bias):
    tk = KEY_TILE
    half = REL_BUCKETS // 2
    max_exact = half // 2
    n_sat = int(math.ceil(max_exact * (REL_MAX_DIST / max_exact) ** ((half - 1 - max_exact) / (half - max_exact)))) + 2
    nd = (n_sat + 2 * tk - 2) // tk + 1
    dd = jnp.arange(nd, dtype=I32)[:, None, None]
    c = jnp.arange(tk, dtype=I32)[None, :, None]
    r = jnp.arange(tk, dtype=I32)[None, None, :]
    bucket = _rel_bucket(c - r - dd * tk)
    onehot = (bucket[..., None] == jnp.arange(REL_BUCKETS, dtype=I32)).astype(F32)
    tiles = jnp.einsum('dcrb,bh->dhcr', onehot, rel_bias.astype(F32) * LOG2E,
                       precision=lax.Precision.HIGHEST)
    return tiles


def _prep_proj(norm1_g, w_in, q_norm_g, k_norm_g, idx_k_norm_g, idx_k_norm_b, d_model):
    ssm_w = d_model // 2
    attn_w = N_HEADS * HEAD_DIM
    kv = N_KV_HEADS * HEAD_DIM
    sizes = [ssm_w, attn_w, kv, kv, IDX_HEADS * IDX_DIM, IDX_DIM, IDX_HEADS, d_model, d_model]
    pts = np.cumsum(sizes)[:-1].tolist()
    wu, wq, wk, wv, wqi, wki, wwi, wga, wgb = jnp.split(w_in, pts, axis=1)
    bf = lambda a: a.astype(BF16)
    blk = np.kron(np.eye(N_KV_HEADS), np.ones((HEAD_DIM, HEAD_DIM))) / HEAD_DIM
    wwit = jnp.pad(wwi.T, ((0, 2 * SUBLANES - IDX_HEADS), (0, 0)))
    return dict(
        g1=norm1_g.reshape(1, -1).astype(F32),
        wu=bf(wu), wq=bf(_pad_heads(wq, N_HEADS, HEAD_DIM)), wk=bf(_pad_heads(wk, N_KV_HEADS, HEAD_DIM)),
        wv=bf(_pad_heads(wv, N_KV_HEADS, HEAD_DIM)), wqi=bf(_pad_heads(wqi, IDX_HEADS, IDX_DIM)),
        wki=bf(_pad_heads(wki, 1, IDX_DIM)), wwit=bf(wwit),
        wga=bf(wga), wgb=bf(wgb), wkc=bf(wk), wvc=bf(wv),
        gq=_pad_lanes(q_norm_g.astype(F32)), gk=_pad_lanes(k_norm_g.astype(F32)),
        gkc=jnp.tile(k_norm_g.astype(F32), N_KV_HEADS).reshape(1, -1),
        gi=_pad_lanes(idx_k_norm_g.astype(F32)), bi=_pad_lanes(idx_k_norm_b.astype(F32)),
        ones_h=jnp.full((LANES, LANES), 1.0 / HEAD_DIM, BF16),
        ones_c=jnp.asarray(blk, BF16),
    )


def _prep_s5(lre, lim, log_dt, b_re, b_im, c_re, c_im, dvec, wa, wb):
    g, p = lre.shape
    ch = b_re.shape[-1]
    lam = lax.complex(lre.astype(F32), lim.astype(F32))
    dt = jnp.exp(log_dt.astype(F32))[:, None]
    a_bar = jnp.exp(lam * dt)
    b_bar = ((a_bar - 1.0) / lam)[:, :, None] * lax.complex(b_re.astype(F32), b_im.astype(F32))
    gs = g // S5_DIAG
    eye = jnp.eye(gs, dtype=F32)

    def blocks_in(m):
        return jnp.einsum('jgpc,gh->jgchp', m.reshape(S5_DIAG, gs, p, ch), eye).reshape(S5_DIAG, gs * ch, gs * p)

    def blocks_out(m):
        return jnp.einsum('jgcp,gh->jgphc', m.reshape(S5_DIAG, gs, ch, p), eye).reshape(S5_DIAG, gs * p, gs * ch)

    return dict(
        b_re=blocks_in(jnp.real(b_bar)).astype(BF16), b_im=blocks_in(jnp.imag(b_bar)).astype(BF16),
        c_re=blocks_out(c_re.astype(F32)).astype(BF16), c_im=blocks_out(-c_im.astype(F32)).astype(BF16),
        a_re=jnp.real(a_bar).reshape(1, g * p), a_im=jnp.imag(a_bar).reshape(1, g * p),
        d=dvec.reshape(1, -1).astype(F32), wa=wa.astype(BF16), wb=wb.astype(BF16),
    )


def _prep_merge(w_attn_up, w_out, norm2_g, w_router, b_router):
    d = w_attn_up.shape[1]
    wup = jnp.pad(w_attn_up.reshape(N_HEADS, HEAD_DIM, d), ((0, 0), (0, LANES - HEAD_DIM), (0, 0)))
    wr_t = w_router.astype(F32).T
    wr_hi = wr_t.astype(BF16)
    wr_lo = (wr_t - wr_hi.astype(F32)).astype(BF16)
    return dict(
        wup=wup.reshape(N_HEADS * LANES, d).astype(BF16), wout=w_out.astype(BF16),
        g2=norm2_g.reshape(1, -1).astype(F32), wr_hi=wr_hi, wr_lo=wr_lo,
        br=jnp.broadcast_to(b_router.astype(F32)[:, None], (b_router.shape[0], LANES)),
    )


def _prep_moe(wg, bg, wu, bu, wd, bd):
    return dict(wg=wg.astype(BF16), wu=wu.astype(BF16), wd=wd.astype(BF16),
                bg=bg.astype(F32)[:, None, :], bu=bu.astype(F32)[:, None, :], bd=bd.astype(F32)[:, None, :])


def _pick_tile(n, pref):
    t = min(n, pref)
    while n % t:
        t //= 2
    return t


def _pad_axis(a, axis, size):
    pad = [(0, 0)] * a.ndim
    pad[axis] = (0, size - a.shape[axis])
    return jnp.pad(a, pad)


def _trunk_layer(x, past_k, past_v, past_ik, h0_re, h0_im, pw, sw, mw, ew, bias_tiles):
    bsz, seq, d = x.shape
    t = bsz * seq
    tm = _pick_tile(seq, 512)
    u_tb, q, kp, vp, qi, kip, wt, sga, sgb, kc, vc, kic = _proj(x, pw, bsz, seq, tm)

    half = sw['a_re'].shape[1]
    if h0_re is None:
        h0 = jnp.zeros((bsz, 2 * half), F32)
    else:
        h0 = jnp.concatenate([h0_re.reshape(bsz, half), h0_im.reshape(bsz, half)], axis=1).astype(F32)
    tc = _pick_tile(seq, max(1, 512 // bsz))
    ya, hout = _s5(u_tb, h0, sw, bsz, seq, tc)
    groups = half // SSM_STATE
    s_re = hout[:, :half].reshape(bsz, groups, SSM_STATE)
    s_im = hout[:, half:].reshape(bsz, groups, SSM_STATE)

    past = 0 if past_k is None else past_k.shape[1]
    n_keys = past + seq
    lk = -(-n_keys // KEY_BLOCK) * KEY_BLOCK
    kip3 = kip.reshape(bsz, seq, LANES)
    if past:
        lane = jnp.arange(LANES)
        pk = jnp.pad(past_k.astype(F32), ((0, 0), (0, 0), (0, 0), (0, LANES - HEAD_DIM)))
        pk = jnp.where(lane == HEAD_DIM, 1.0, pk).astype(BF16)
        pv = jnp.pad(past_v.astype(F32), ((0, 0), (0, 0), (0, 0), (0, LANES - HEAD_DIM)))
        pv = jnp.where(lane == HEAD_DIM, 1.0, pv).astype(BF16)
        pik = jnp.pad(past_ik.astype(F32), ((0, 0), (0, 0), (0, LANES - IDX_DIM))).astype(BF16)
        k_all = jnp.concatenate([pk.transpose(0, 2, 1, 3), kp], axis=2)
        v_all = jnp.concatenate([pv.transpose(0, 2, 1, 3), vp], axis=2)
        ki_all = jnp.concatenate([pik, kip3], axis=1)
    else:
        k_all, v_all, ki_all = kp, vp, kip3
    k_all = _pad_axis(k_all, 2, lk)
    v_all = _pad_axis(v_all, 2, lk)
    ki_all = _pad_axis(ki_all, 1, lk)
    vt_all = v_all.reshape(bsz, N_KV_HEADS, lk // KEY_BLOCK, KEY_BLOCK, LANES).transpose(0, 1, 2, 4, 3)
    tq = KEY_TILE
    seq_q = -(-seq // tq) * tq
    q_p, qi_p, wt_p = _pad_axis(q, 2, seq_q), _pad_axis(qi, 2, seq_q), _pad_axis(wt, 2, seq_q)
    kf = k_all[..., :HEAD_DIM].astype(F32)
    kmax = jnp.sqrt(jnp.max(jnp.sum(kf * kf, axis=-1), axis=-1)).reshape(-1)
    bfar = bias_tiles[-1, :, 0, 0]
    bmax = jnp.max(jnp.abs(bias_tiles), axis=(0, 2, 3))
    stats = jnp.concatenate([kmax, bmax, bfar]).astype(F32)
    bias_tiles = bias_tiles - bfar[None, :, None, None]
    attn = _dsa(stats, q_p, qi_p, wt_p, k_all, vt_all, ki_all, bias_tiles, bsz, seq_q, past, n_keys, tq)
    attn = attn[:, :seq]

    moe_tile = _pick_tile(t, MOE_TILE)
    tm2 = _pick_tile(moe_tile, 512)
    x1, h2, gt, rt, cnt = _merge(x.reshape(t, d), ya, attn.reshape(t, attn.shape[-1]), sga, sgb, mw, tm2, moe_tile)
    cnt_i = cnt[:, :, 0].astype(I32).reshape(-1)
    y = _moe(h2, x1, gt, rt, cnt_i, ew, moe_tile)

    k_new = kc.reshape(bsz, seq, N_KV_HEADS, HEAD_DIM)
    v_new = vc.reshape(bsz, seq, N_KV_HEADS, HEAD_DIM)
    ik_new = kic.reshape(bsz, seq, IDX_DIM)
    return y.reshape(bsz, seq, d), k_new, v_new, ik_new, s_re, s_im


def kernel(x_prompt, x_sample, cache_k, cache_v, cache_idx_k, state_ssm_re, state_ssm_im, rel_bias, norm1_g, w_in, ssm_lambda_re, ssm_lambda_im, ssm_log_dt, ssm_b_re, ssm_b_im, ssm_c_re, ssm_c_im, ssm_d, ssm_w_glu_a, ssm_w_glu_b, q_norm_g, k_norm_g, idx_k_norm_g, idx_k_norm_b, w_attn_up, w_out, norm2_g, moe_w_router, moe_b_router, moe_w_gate, moe_b_gate, moe_w_up, moe_b_up, moe_w_down, moe_b_down):
    depth = w_in.shape[0]
    d_model = x_prompt.shape[-1]
    bias_tiles = _bias_tiles(rel_bias)
    xp, xs = x_prompt, x_sample
    st_p, st_s = [], []
    for l in range(depth):
        pw = _prep_proj(norm1_g[l], w_in[l], q_norm_g[l], k_norm_g[l], idx_k_norm_g[l], idx_k_norm_b[l], d_model)
        sw = _prep_s5(ssm_lambda_re[l], ssm_lambda_im[l], ssm_log_dt[l], ssm_b_re[l], ssm_b_im[l], ssm_c_re[l],
                      ssm_c_im[l], ssm_d[l], ssm_w_glu_a[l], ssm_w_glu_b[l])
        mw = _prep_merge(w_attn_up[l], w_out[l], norm2_g[l], moe_w_router[l], moe_b_router[l])
        ew = _prep_moe(moe_w_gate[l], moe_b_gate[l], moe_w_up[l], moe_b_up[l], moe_w_down[l], moe_b_down[l])
        xp, *sp = _trunk_layer(xp, None, None, None, None, None, pw, sw, mw, ew, bias_tiles)
        xs, *ss = _trunk_layer(xs, cache_k[l], cache_v[l], cache_idx_k[l], state_ssm_re[l], state_ssm_im[l],
                               pw, sw, mw, ew, bias_tiles)
        st_p.append(sp)
        st_s.append(ss)
    outs_p = [jnp.stack([s[i] for s in st_p]) for i in range(5)]
    outs_s = [jnp.stack([s[i] for s in st_s]) for i in range(5)]
    return (xp, xs, *outs_p, *outs_s)
```

```python
import functools
import math

import numpy as np
import jax
import jax.numpy as jnp
from jax import lax
from jax.experimental import pallas as pl
from jax.experimental.pallas import tpu as pltpu

F32 = jnp.float32
BF16 = jnp.bfloat16
I32 = jnp.int32

LANES = 128
SUBLANES = 8
VMEM_LIMIT = 56 * 1024 * 1024

CHUNK = 64
SSM_GROUP_CH = 16
SSM_STATE = 64
N_HEADS = 8
HEAD_DIM = 64
N_KV_HEADS = 2
KV_REP = N_HEADS // N_KV_HEADS
IDX_HEADS = 8
IDX_DIM = 64
TOPK_MAX = 256
REL_BUCKETS = 32
REL_MAX_DIST = 1024
N_EXPERTS = 32
TOP_K = 4
SWIGLU_LIMIT = 7.0
SWIGLU_ALPHA = 1.702
EPS = 1e-6

KEY_TILE = 128
KEY_BLOCK = 256
SCORE_CHUNK = 4
ATTEND_CHUNK = 4
LOG2E = math.log2(math.e)
NEG_BIG = -1e30
SHIFT_LIMIT = 30.0
S5_DIAG = 2
SEARCH_FIRST = 8
SEARCH_GROUP = 4
MOE_TILE = 1024
MOE_ROWS = 160
MOE_GROUP = 8
MOE_PAIR = 2


def _cp(sem):
    return pltpu.CompilerParams(dimension_semantics=sem, vmem_limit_bytes=VMEM_LIMIT)


def _dot(a, b):
    return jnp.dot(a, b, preferred_element_type=F32)


def _dot_nt(a, b):
    return lax.dot_general(a, b, (((1,), (1,)), ((), ())), preferred_element_type=F32)


def _dot_tn(a, b):
    return lax.dot_general(a, b, (((0,), (0,)), ((), ())), preferred_element_type=F32)


def _split(a):
    hi = a.astype(BF16)
    lo = (a - hi.astype(F32)).astype(BF16)
    return hi, lo


def _dot_split(a, g):
    hi, lo = _split(a)
    return _dot(hi, g) + _dot(lo, g)


def _proj_kernel(x_ref, g1_ref, wu_ref, wq_ref, wk_ref, wv_ref, wqi_ref, wki_ref, wwit_ref, wga_ref, wgb_ref,
                 wkc_ref, wvc_ref, gq_ref, gk_ref, gkc_ref, gi_ref, bi_ref, ones_h_ref, ones_c_ref,
                 u_ref, q_ref, kp_ref, vp_ref, qi_ref, kip_ref, wt_ref, sga_ref, sgb_ref,
                 kc_ref, vc_ref, kic_ref):
    x = x_ref[...]
    ms = jnp.mean(x * x, axis=-1, keepdims=True)
    hn = (x * lax.rsqrt(ms + EPS) * g1_ref[...]).astype(BF16)
    ones_h = ones_h_ref[...]
    lane = lax.broadcasted_iota(I32, (x.shape[0], LANES), 1)

    u_ref[...] = _dot(hn, wu_ref[...]).astype(BF16)

    q = _dot(hn, wq_ref[...])
    scale = HEAD_DIM ** -0.5 * LOG2E
    for h in range(N_HEADS):
        qh = q[:, h * LANES:(h + 1) * LANES]
        msq = _dot_split(qh * qh, ones_h)
        q_ref[h] = (qh * lax.rsqrt(msq + EPS) * (gq_ref[...] * scale)).astype(BF16)

    k = _dot(hn, wk_ref[...])
    for g in range(N_KV_HEADS):
        kg = k[:, g * LANES:(g + 1) * LANES]
        msk = _dot_split(kg * kg, ones_h)
        kn = kg * lax.rsqrt(msk + EPS) * gk_ref[...]
        kp_ref[g] = jnp.where(lane == HEAD_DIM, 1.0, kn).astype(BF16)

    v = _dot(hn, wv_ref[...])
    for g in range(N_KV_HEADS):
        vg = v[:, g * LANES:(g + 1) * LANES]
        vp_ref[g] = jnp.where(lane == HEAD_DIM, 1.0, vg).astype(BF16)

    qi = _dot(hn, wqi_ref[...])
    for h in range(IDX_HEADS):
        qi_ref[h] = qi[:, h * LANES:(h + 1) * LANES].astype(BF16)

    ki = _dot(hn, wki_ref[...])
    mu = _dot_split(ki, ones_h)
    xc = jnp.where(lane < IDX_DIM, ki - mu, 0.0)
    var = _dot_split(xc * xc, ones_h)
    kin = xc * lax.rsqrt(var + EPS) * gi_ref[...] + bi_ref[...]
    kip_ref[...] = kin.astype(BF16)
    kic_ref[...] = kin[:, :IDX_DIM]

    wt = _dot_nt(wwit_ref[...], hn)
    wt_ref[...] = wt[0:IDX_HEADS, :] * (IDX_HEADS ** -0.5 * IDX_DIM ** -0.5)

    sga_ref[...] = jax.nn.sigmoid(_dot(hn, wga_ref[...])).astype(BF16)
    sgb_ref[...] = jax.nn.sigmoid(_dot(hn, wgb_ref[...])).astype(BF16)

    kc = _dot(hn, wkc_ref[...])
    mskc = _dot_split(kc * kc, ones_c_ref[...])
    kc_ref[...] = kc * lax.rsqrt(mskc + EPS) * gkc_ref[...]
    vc_ref[...] = _dot(hn, wvc_ref[...])


def _proj(x, pw, bsz, seq, tm):
    d = x.shape[-1]
    nt = seq // tm
    t = bsz * seq
    x2 = x.reshape(t, d)

    def tok(b, i):
        return (b * nt + i, 0)

    def cst(b, i):
        return (0, 0)

    def wspec(a):
        return pl.BlockSpec(a.shape, cst)

    weights = [pw['g1'], pw['wu'], pw['wq'], pw['wk'], pw['wv'], pw['wqi'], pw['wki'], pw['wwit'], pw['wga'],
               pw['wgb'], pw['wkc'], pw['wvc'], pw['gq'], pw['gk'], pw['gkc'], pw['gi'], pw['bi'],
               pw['ones_h'], pw['ones_c']]
    ssm_w = pw['wu'].shape[1]
    out_shape = (
        jax.ShapeDtypeStruct((t, ssm_w), BF16),
        jax.ShapeDtypeStruct((bsz, N_HEADS, seq, LANES), BF16),
        jax.ShapeDtypeStruct((bsz, N_KV_HEADS, seq, LANES), BF16),
        jax.ShapeDtypeStruct((bsz, N_KV_HEADS, seq, LANES), BF16),
        jax.ShapeDtypeStruct((bsz, IDX_HEADS, seq, LANES), BF16),
        jax.ShapeDtypeStruct((t, LANES), BF16),
        jax.ShapeDtypeStruct((bsz, IDX_HEADS, seq), F32),
        jax.ShapeDtypeStruct((t, d), BF16),
        jax.ShapeDtypeStruct((t, d), BF16),
        jax.ShapeDtypeStruct((t, N_KV_HEADS * HEAD_DIM), F32),
        jax.ShapeDtypeStruct((t, N_KV_HEADS * HEAD_DIM), F32),
        jax.ShapeDtypeStruct((t, IDX_DIM), F32),
    )

    def hm(nh):
        return pl.BlockSpec((None, nh, tm, LANES), lambda b, i: (b, 0, i, 0))

    out_specs = (
        pl.BlockSpec((tm, ssm_w), tok),
        hm(N_HEADS), hm(N_KV_HEADS), hm(N_KV_HEADS), hm(IDX_HEADS),
        pl.BlockSpec((tm, LANES), tok),
        pl.BlockSpec((None, IDX_HEADS, tm), lambda b, i: (b, 0, i)),
        pl.BlockSpec((tm, d), tok), pl.BlockSpec((tm, d), tok),
        pl.BlockSpec((tm, N_KV_HEADS * HEAD_DIM), tok), pl.BlockSpec((tm, N_KV_HEADS * HEAD_DIM), tok),
        pl.BlockSpec((tm, IDX_DIM), tok),
    )
    return pl.pallas_call(
        _proj_kernel,
        grid=(bsz, nt),
        in_specs=[pl.BlockSpec((tm, d), tok)] + [wspec(a) for a in weights],
        out_specs=out_specs,
        out_shape=out_shape,
        compiler_params=_cp(("arbitrary", "arbitrary")),
        name="proj",
    )(x2, *weights)


def _gelu_tanh(x):
    return 0.5 * x * (1.0 + jnp.tanh(math.sqrt(2.0 / math.pi) * (x + 0.044715 * (x * x * x))))


def _s5_kernel(u_ref, h0_ref, bre_ref, bim_ref, are_ref, aim_ref, cre_ref, cim_ref, dvec_ref, wa_ref, wb_ref,
               ya_ref, hout_ref, state_ref, bu_ref, yf_ref, ug_ref, *, bsz, tc, strip):
    s = pl.program_id(0)
    half = are_ref.shape[1]

    @pl.when(s == 0)
    def _():
        state_ref[...] = h0_ref[...]

    ssm_w = dvec_ref.shape[1]
    for b in range(bsz):
        for c in range(ssm_w // LANES):
            ug_ref[c, pl.ds(b, tc, stride=bsz), :] = u_ref[b, :, c * LANES:(c + 1) * LANES].astype(F32)
    u = jnp.concatenate([ug_ref[c] for c in range(ssm_w // LANES)], axis=1).astype(BF16)
    cw = u.shape[1] // S5_DIAG
    sw = half // S5_DIAG
    for j in range(S5_DIAG):
        uj = u[:, j * cw:(j + 1) * cw]
        bu_ref[:, j * sw:(j + 1) * sw] = _dot(uj, bre_ref[j])
        bu_ref[:, half + j * sw:half + (j + 1) * sw] = _dot(uj, bim_ref[j])

    for c0 in range(0, half, strip):
        ar = jnp.broadcast_to(are_ref[:, c0:c0 + strip], (bsz, strip))
        ai = jnp.broadcast_to(aim_ref[:, c0:c0 + strip], (bsz, strip))
        hr0 = state_ref[:, c0:c0 + strip]
        hi0 = state_ref[:, half + c0:half + c0 + strip]

        def step(t, carry):
            hr, hi = carry
            r0 = pl.multiple_of(t * bsz, bsz)
            br = bu_ref[pl.ds(r0, bsz), c0:c0 + strip]
            bi = bu_ref[pl.ds(r0, bsz), half + c0:half + c0 + strip]
            nr = ar * hr - ai * hi + br
            ni = ar * hi + ai * hr + bi
            bu_ref[pl.ds(r0, bsz), c0:c0 + strip] = nr
            bu_ref[pl.ds(r0, bsz), half + c0:half + c0 + strip] = ni
            return nr, ni

        hr, hi = lax.fori_loop(0, tc, step, (hr0, hi0))
        state_ref[:, c0:c0 + strip] = hr
        state_ref[:, half + c0:half + c0 + strip] = hi

    ys = []
    for j in range(S5_DIAG):
        s_re = bu_ref[:, j * sw:(j + 1) * sw].astype(BF16)
        s_im = bu_ref[:, half + j * sw:half + (j + 1) * sw].astype(BF16)
        ys.append(_dot(s_re, cre_ref[j]) + _dot(s_im, cim_ref[j]))
    y = jnp.concatenate(ys, axis=1) + dvec_ref[...] * u.astype(F32)
    g = _gelu_tanh(y).astype(BF16)
    ya = _dot(g, wa_ref[...]) * jax.nn.sigmoid(_dot(g, wb_ref[...]))
    n_chunk = ya.shape[1] // LANES
    for c in range(n_chunk):
        yf_ref[c] = ya[:, c * LANES:(c + 1) * LANES]
    for b in range(bsz):
        ya_ref[b] = jnp.concatenate([yf_ref[c, pl.ds(b, tc, stride=bsz), :] for c in range(n_chunk)],
                                    axis=1).astype(BF16)

    @pl.when(s == pl.num_programs(0) - 1)
    def _():
        hout_ref[...] = state_ref[...]


def _s5(u, h0, sw, bsz, seq, tc):
    rows = tc * bsz
    ssm_w = sw['d'].shape[1]
    half = sw['a_re'].shape[1]
    two_half = 2 * half
    d = sw['wa'].shape[1]
    u3 = u.reshape(bsz, seq, ssm_w)
    strip = min(512, half)

    consts = [h0, sw['b_re'], sw['b_im'], sw['a_re'], sw['a_im'], sw['c_re'], sw['c_im'], sw['d'], sw['wa'], sw['wb']]

    def cst(s):
        return (0, 0)

    def cspec(a):
        return pl.BlockSpec(a.shape, lambda s: (0,) * a.ndim)

    ya, hout = pl.pallas_call(
        functools.partial(_s5_kernel, bsz=bsz, tc=tc, strip=strip),
        grid=(seq // tc,),
        in_specs=[pl.BlockSpec((bsz, tc, ssm_w), lambda s: (0, s, 0))] + [cspec(a) for a in consts],
        out_specs=(pl.BlockSpec((bsz, tc, d), lambda s: (0, s, 0)), pl.BlockSpec((bsz, two_half), cst)),
        out_shape=(jax.ShapeDtypeStruct((bsz, seq, d), BF16), jax.ShapeDtypeStruct((bsz, two_half), F32)),
        scratch_shapes=[pltpu.VMEM((bsz, two_half), F32), pltpu.VMEM((rows, two_half), F32),
                        pltpu.VMEM((d // LANES, rows, LANES), F32), pltpu.VMEM((ssm_w // LANES, rows, LANES), F32)],
        compiler_params=_cp(("arbitrary",)),
        name="s5",
    )(u3, *consts)
    return ya.reshape(bsz * seq, d), hout


def _f2key(x):
    b = lax.bitcast_convert_type(x, I32)
    return b ^ ((b >> 31) & 0x7FFFFFFF)


def _key2f(k):
    return lax.bitcast_convert_type(k ^ ((k >> 31) & 0x7FFFFFFF), F32)


def _dsa_kernel(st_ref, q_ref, qi_ref, wt_ref, k_ref, vt_ref, ki_ref, bias_ref, o_ref,
                s_ref, lo_ref, hi_ref, clo_ref, glo_ref, ghi_ref, side_ref, q2_ref, mrow_ref, acc_ref,
                *, bsz, tq, past, n_keys, topk, nkt, nd, idx_bits, near_max):
    b_id = pl.program_id(0)
    i = pl.program_id(1)
    kb = KEY_BLOCK
    sl = SUBLANES
    q0 = past + i * tq
    last_chunk = (q0 + tq - 1) // CHUNK
    n_kt = jnp.minimum(nkt, ((last_chunk + 1) * CHUNK + kb - 1) // kb)
    d0 = q0 // KEY_TILE

    krow = lax.broadcasted_iota(I32, (kb, tq), 0)
    q_chunk = (q0 + lax.broadcasted_iota(I32, (kb, tq), 1)) // CHUNK
    qc8 = (q0 + lax.broadcasted_iota(I32, (sl, tq), 1)) // CHUNK
    n_adm = jnp.minimum((qc8 + 1) * CHUNK, n_keys)
    n_admf = n_adm.astype(F32)
    is_pad = q0 + lax.broadcasted_iota(I32, (sl, tq), 1) >= n_keys
    needf = jnp.where(is_pad, n_adm, jnp.minimum(topk, n_adm)).astype(F32)

    def bcast(x):
        return jnp.broadcast_to(x[0:1, :], (kb, tq))

    def rep(x):
        return jnp.broadcast_to(x, (sl, tq))

    qi = qi_ref[...].reshape(IDX_HEADS * tq, LANES)

    def score_blocks(kt, nb, masked):
        k0 = pl.multiple_of(kt * kb, kb)
        s = _dot_nt(ki_ref[pl.ds(k0, nb * kb), :], qi)
        for j in range(nb):
            sc = jnp.zeros((kb, tq), F32)
            for h in range(IDX_HEADS):
                sc = sc + wt_ref[h:h + 1, :] * jnp.maximum(s[j * kb:(j + 1) * kb, h * tq:(h + 1) * tq], 0.0)
            if masked:
                kpos = k0 + j * kb + krow
                adm = ((kpos // CHUNK) <= q_chunk) & (kpos < n_keys)
                sc = jnp.where(adm, sc, -jnp.inf)
            s_ref[kt + j] = sc

    n_open = n_kt - 1

    def score_chunk(j, c):
        score_blocks(SCORE_CHUNK * j, SCORE_CHUNK, False)
        return c

    lax.fori_loop(0, n_open // SCORE_CHUNK, score_chunk, 0)

    def score_single(j, c):
        score_blocks(j, 1, False)
        return c

    lax.fori_loop(n_open - n_open % SCORE_CHUNK, n_open, score_single, 0)
    score_blocks(n_kt - 1, 1, True)
    s_ref[n_kt] = jnp.full((kb, tq), -jnp.inf, F32)
    n_pair = (n_kt + 1) // 2

    part = 4 * sl

    def fold(x, op):
        x = x.reshape(kb // part, part, tq)
        acc = x[0]
        for j in range(1, kb // part):
            acc = op(acc, x[j])
        return acc

    def count(pred):
        def one(kt):
            return fold(jnp.where(pred(s_ref[kt], kt), 1.0, 0.0), jnp.add)
        c = lax.fori_loop(0, n_pair, lambda j, c: c + (one(2 * j) + one(2 * j + 1)), jnp.zeros((part, tq), F32))
        return rep(jnp.sum(c, axis=0, keepdims=True))

    def minmax(j, c):
        mx, mn = c
        for kt in (2 * j, 2 * j + 1):
            s = s_ref[kt]
            mx = jnp.maximum(mx, fold(s, jnp.maximum))
            mn = jnp.minimum(mn, fold(jnp.where(s == -jnp.inf, jnp.inf, s), jnp.minimum))
        return mx, mn

    mx, mn = lax.fori_loop(0, n_pair, minmax,
                           (jnp.full((part, tq), -jnp.inf, F32), jnp.full((part, tq), jnp.inf, F32)))
    lo_ref[...] = rep(jnp.min(mn, axis=0, keepdims=True))
    hi_ref[...] = _key2f(_f2key(rep(jnp.max(mx, axis=0, keepdims=True))) + 1)
    def odds(cnt):
        c = jnp.clip(cnt, 0.5, n_admf - 0.5)
        return jnp.log((n_admf - c) / c)

    target = odds(needf - 0.5)
    clo_ref[...] = n_admf
    glo_ref[...] = target - odds(n_admf)
    ghi_ref[...] = target - odds(jnp.zeros((sl, tq), F32))
    side_ref[...] = jnp.zeros((sl, tq), F32)

    def searching(lo, hi, clo):
        return (_f2key(hi) > _f2key(lo) + 1) & (clo > needf)

    def refine(it, c):
        lo, hi, clo = lo_ref[...], hi_ref[...], clo_ref[...]
        glo, ghi, side = glo_ref[...], ghi_ref[...], side_ref[...]
        k_t = _f2key(lo + (hi - lo) * (glo / (glo - ghi)))
        t = _key2f(jnp.minimum(jnp.maximum(k_t, _f2key(lo) + 1), _f2key(hi) - 1))
        tb = bcast(t)
        cnt = count(lambda s, kt: s >= tb)
        g = target - odds(cnt)
        open_ = searching(lo, hi, clo)
        up = open_ & (cnt >= needf)
        dn = open_ & (cnt < needf)
        lo_ref[...] = jnp.where(up, t, lo)
        clo_ref[...] = jnp.where(up, cnt, clo)
        hi_ref[...] = jnp.where(dn, t, hi)
        glo_ref[...] = jnp.where(up, g, jnp.where(dn & (side < 0.0), glo * 0.5, glo))
        ghi_ref[...] = jnp.where(dn, g, jnp.where(up & (side > 0.0), ghi * 0.5, ghi))
        side_ref[...] = jnp.where(up, 1.0, jnp.where(dn, -1.0, side))
        return c

    def snap():
        lo, hi, clo = lo_ref[...], hi_ref[...], clo_ref[...]
        lo_b, hi_b = bcast(lo), bcast(hi)

        def body(j, c):
            a, b = c
            for kt in (2 * j, 2 * j + 1):
                s = s_ref[kt]
                a = jnp.minimum(a, fold(jnp.where(s >= lo_b, s, jnp.inf), jnp.minimum))
                b = jnp.maximum(b, fold(jnp.where(s < hi_b, s, -jnp.inf), jnp.maximum))
            return a, b

        a, b = lax.fori_loop(0, n_pair, body,
                             (jnp.full((part, tq), jnp.inf, F32), jnp.full((part, tq), -jnp.inf, F32)))
        open_ = searching(lo, hi, clo)
        lo_ref[...] = jnp.where(open_, rep(jnp.min(a, axis=0, keepdims=True)), lo)
        hi_ref[...] = jnp.where(open_, _key2f(_f2key(rep(jnp.max(b, axis=0, keepdims=True))) + 1), hi)

    def n_searching():
        return jnp.max(jnp.where(searching(lo_ref[...], hi_ref[...], clo_ref[...]), 1.0, 0.0))

    def group(c):
        grp, _ = c
        lax.fori_loop(0, jnp.where(grp == 0, SEARCH_FIRST, SEARCH_GROUP), refine, 0)
        snap()
        return grp + 1, n_searching()

    lax.while_loop(lambda c: c[1] > 0.0, group, (jnp.int32(0), n_searching()))
    thr = lo_ref[...]
    thr_b = bcast(thr)

    n_tied = jnp.max(jnp.where(clo_ref[...] > needf, 1.0, 0.0))

    @pl.when(n_tied > 0.0)
    def _():
        rem_b = bcast(needf - count(lambda s, kt: s > thr_b))
        tri = jnp.where(lax.broadcasted_iota(I32, (kb, kb), 0) >= lax.broadcasted_iota(I32, (kb, kb), 1),
                        1.0, 0.0).astype(BF16)

        def drop(kt, seen):
            s = s_ref[kt]
            tie = s == thr_b
            rank = _dot(tri, jnp.where(tie, 1.0, 0.0).astype(BF16)) + bcast(seen)
            s_ref[kt] = jnp.where(tie & (rank > rem_b), -jnp.inf, s)
            return rep(rank[kb - 1:kb, :])

        lax.fori_loop(0, n_kt, drop, jnp.zeros((sl, tq), F32))

    rows_g = KV_REP * tq
    qf = q_ref[...].reshape(N_HEADS * tq, LANES).astype(F32)
    qn = jnp.sqrt(jnp.sum(qf * qf, axis=1, keepdims=True))
    lane = lax.broadcasted_iota(I32, (tq, LANES), 1)
    worst = jnp.float32(0.0)
    for h in range(N_HEADS):
        kmax = st_ref[b_id * N_KV_HEADS + h // KV_REP]
        bmax = st_ref[bsz * N_KV_HEADS + h]
        bfar = st_ref[bsz * N_KV_HEADS + N_HEADS + h]
        bound = qn[h * tq:(h + 1) * tq, :] * (kmax * 1.01) + (bmax + 0.1)
        worst = jnp.maximum(worst, jnp.max(bound))
        q2_ref[h * tq:(h + 1) * tq, :] = jnp.where(lane == HEAD_DIM, bfar - bound,
                                                   qf[h * tq:(h + 1) * tq, :]).astype(BF16)
    n_far = jnp.clip((d0 - nd) // 2 + 1, 0, n_kt)

    def logits(kt0, nb, g, near, exact):
        k0 = pl.multiple_of(kt0 * kb, kb)
        s = _dot_nt(k_ref[g, pl.ds(k0, nb * kb), :], q2_ref[g * rows_g:(g + 1) * rows_g, :])
        out = []
        for j in range(nb):
            kt = kt0 + j
            maskadd = jnp.where(s_ref[kt] >= thr_b, 0.0, NEG_BIG)
            if near:
                da = jnp.clip(d0 - 2 * kt, 0, nd - 1)
                db = jnp.clip(d0 - 2 * kt - 1, 0, nd - 1)
            parts = []
            for r in range(KV_REP):
                h = g * KV_REP + r
                add = maskadd - mrow_ref[0:1, h * tq:(h + 1) * tq] if exact else maskadd
                if near:
                    add = jnp.concatenate([bias_ref[da, h], bias_ref[db, h]], axis=0) + add
                parts.append(s[j * kb:(j + 1) * kb, r * tq:(r + 1) * tq] + add)
            out.append(jnp.concatenate(parts, axis=1))
        return out

    def over_blocks(fn):
        def far_chunk(j, c):
            fn(ATTEND_CHUNK * j, ATTEND_CHUNK, False)
            return c
        lax.fori_loop(0, n_far // ATTEND_CHUNK, far_chunk, 0)

        def far_single(kt, c):
            fn(kt, 1, False)
            return c
        lax.fori_loop(n_far - n_far % ATTEND_CHUNK, n_far, far_single, 0)

        n_near = n_kt - n_far
        for nb in range(1, near_max + 1):
            @pl.when(n_near == nb)
            def _(nb=nb):
                fn(n_far, nb, True)

    def attend(exact):
        acc_ref[...] = jnp.zeros(acc_ref.shape, F32)

        def blocks(kt0, nb, near):
            for g in range(N_KV_HEADS):
                p = jnp.concatenate([jnp.exp2(lg).astype(BF16) for lg in logits(kt0, nb, g, near, exact)], axis=0)
                vt = jnp.concatenate([vt_ref[g, kt0 + j] for j in range(nb)], axis=1)
                acc_ref[g] += _dot(vt, p)
        over_blocks(blocks)

    @pl.when(worst <= SHIFT_LIMIT)
    def _():
        attend(False)

    @pl.when(worst > SHIFT_LIMIT)
    def _():
        mrow_ref[...] = jnp.full(mrow_ref.shape, NEG_BIG, F32)

        def blocks(kt0, nb, near):
            for g in range(N_KV_HEADS):
                for lg in logits(kt0, nb, g, near, False):
                    mx = jnp.max(lg, axis=0, keepdims=True)
                    cur = mrow_ref[:, g * rows_g:(g + 1) * rows_g]
                    mrow_ref[:, g * rows_g:(g + 1) * rows_g] = jnp.maximum(cur, jnp.broadcast_to(mx, (sl, rows_g)))
        over_blocks(blocks)
        attend(True)

    for g in range(N_KV_HEADS):
        acc = acc_ref[g]
        og = acc / acc[HEAD_DIM:HEAD_DIM + 1, :]
        for r in range(KV_REP):
            h = g * KV_REP + r
            o_ref[:, h * LANES:(h + 1) * LANES] = og[:, r * tq:(r + 1) * tq].T.astype(BF16)


def _dsa(stats, q, qi, wt, k_all, vt_all, ki_all, bias_tiles, bsz, seq, past, n_keys, tq):
    lk = k_all.shape[2]
    nkt = lk // KEY_BLOCK
    topk = min(TOPK_MAX, n_keys // 4)
    nd = bias_tiles.shape[0]
    nq = seq // tq
    assert past % KEY_TILE == 0 and tq == KEY_TILE
    idx_bits = int(math.ceil(math.log2(lk))) + 1
    near_max = 0
    for i in range(nq):
        q0 = past + i * tq
        n_kt = min(nkt, (((q0 + tq - 1) // CHUNK + 1) * CHUNK + KEY_BLOCK - 1) // KEY_BLOCK)
        n_far = min(max((q0 // KEY_TILE - nd) // 2 + 1, 0), n_kt)
        near_max = max(near_max, n_kt - n_far)
    kern = functools.partial(_dsa_kernel, bsz=bsz, tq=tq, past=past, n_keys=n_keys, topk=topk, nkt=nkt, nd=nd,
                             idx_bits=idx_bits, near_max=near_max)
    row_state = pltpu.VMEM((SUBLANES, tq), F32)
    grid_spec = pltpu.PrefetchScalarGridSpec(
        num_scalar_prefetch=1,
        grid=(bsz, nq),
        in_specs=[
            pl.BlockSpec((None, N_HEADS, tq, LANES), lambda b, i, s: (b, 0, i, 0)),
            pl.BlockSpec((None, IDX_HEADS, tq, LANES), lambda b, i, s: (b, 0, i, 0)),
            pl.BlockSpec((None, IDX_HEADS, tq), lambda b, i, s: (b, 0, i)),
            pl.BlockSpec((None, N_KV_HEADS, lk, LANES), lambda b, i, s: (b, 0, 0, 0)),
            pl.BlockSpec((None, N_KV_HEADS, nkt, LANES, KEY_BLOCK), lambda b, i, s: (b, 0, 0, 0, 0)),
            pl.BlockSpec((None, lk, LANES), lambda b, i, s: (b, 0, 0)),
            pl.BlockSpec(bias_tiles.shape, lambda b, i, s: (0, 0, 0, 0)),
        ],
        out_specs=pl.BlockSpec((None, tq, N_HEADS * LANES), lambda b, i, s: (b, i, 0)),
        scratch_shapes=[
            pltpu.VMEM((nkt + 1, KEY_BLOCK, tq), F32),
            row_state, row_state, row_state, row_state, row_state, row_state,
            pltpu.VMEM((N_HEADS * tq, LANES), BF16),
            pltpu.VMEM((SUBLANES, N_HEADS * tq), F32),
            pltpu.VMEM((N_KV_HEADS, LANES, KV_REP * tq), F32),
        ],
    )
    return pl.pallas_call(
        kern,
        grid_spec=grid_spec,
        out_shape=jax.ShapeDtypeStruct((bsz, seq, N_HEADS * LANES), BF16),
        compiler_params=_cp(("arbitrary", "arbitrary")),
        name="dsa",
    )(stats, q, qi, wt, k_all, vt_all, ki_all, bias_tiles)


def _merge_kernel(x_ref, ya_ref, at_ref, sga_ref, sgb_ref, wup_ref, wout_ref, g2_ref, wr_hi_ref, wr_lo_ref, br_ref,
                  x1_ref, h2_ref, gt_ref, rt_ref, cnt_ref, run_ref, *, tm, sub):
    step = pl.program_id(0)

    @pl.when(step % sub == 0)
    def _():
        run_ref[...] = jnp.zeros(run_ref.shape, F32)

    yb = _dot(at_ref[...], wup_ref[...])
    merged = sga_ref[...].astype(F32) * ya_ref[...].astype(F32) + sgb_ref[...].astype(F32) * yb
    x1 = x_ref[...] + _dot(merged.astype(BF16), wout_ref[...])
    x1_ref[...] = x1
    ms = jnp.mean(x1 * x1, axis=-1, keepdims=True)
    h2 = x1 * lax.rsqrt(ms + EPS) * g2_ref[...]
    h2_hi, h2_lo = _split(h2)
    h2_ref[...] = h2_hi

    wr_hi = wr_hi_ref[...]
    logit = (_dot_nt(wr_hi, h2_hi) + _dot_nt(wr_hi, h2_lo) + _dot_nt(wr_lo_ref[...], h2_hi)) + br_ref[:, 0:1]
    ne = logit.shape[0]
    eid = lax.broadcasted_iota(I32, (ne, tm), 0).astype(F32)
    selb = jnp.zeros((ne, tm), F32)
    tops = []
    picks = []
    for _ in range(TOP_K):
        mx = jnp.max(logit, axis=0, keepdims=True)
        pick = jnp.min(jnp.where(logit == mx, eid, float(ne)), axis=0, keepdims=True)
        hit = eid == pick
        selb = jnp.where(hit, 1.0, selb)
        logit = jnp.where(hit, -jnp.inf, logit)
        tops.append(mx)
        picks.append(hit)
    ex = [jnp.exp(t - tops[0]) for t in tops]
    den = ex[0] + ex[1] + ex[2] + ex[3]
    gate = jnp.zeros((ne, tm), F32)
    for hit, e in zip(picks, ex):
        gate = jnp.where(hit, e / den, gate)
    gt_ref[...] = gate

    sel = selb > 0.5
    selb = selb.astype(BF16)
    r_i = lax.broadcasted_iota(I32, (tm, tm), 0)
    c_i = lax.broadcasted_iota(I32, (tm, tm), 1)
    tri = jnp.where(r_i < c_i, 1.0, 0.0).astype(BF16)
    run = run_ref[...]
    rank = _dot(selb, tri) + jnp.broadcast_to(run[:, 0:1], (ne, tm))
    rt_ref[...] = jnp.where(sel, rank, -1.0)
    run = run + _dot(selb, jnp.ones((tm, LANES), BF16))
    run_ref[...] = run
    cnt_ref[...] = run


def _merge(x2, ya, attn, sga, sgb, mw, tm, moe_tile):
    t, d = x2.shape
    sub = moe_tile // tm
    ne = mw['wr_hi'].shape[0]

    def tok(i):
        return (i, 0)

    def cst(i):
        return (0, 0)

    consts = [mw['wup'], mw['wout'], mw['g2'], mw['wr_hi'], mw['wr_lo'], mw['br']]
    return pl.pallas_call(
        functools.partial(_merge_kernel, tm=tm, sub=sub),
        grid=(t // tm,),
        in_specs=[
            pl.BlockSpec((tm, d), tok),
            pl.BlockSpec((tm, d), tok),
            pl.BlockSpec((tm, attn.shape[-1]), tok),
            pl.BlockSpec((tm, d), tok),
            pl.BlockSpec((tm, d), tok),
        ] + [pl.BlockSpec(a.shape, cst) for a in consts],
        out_specs=(
            pl.BlockSpec((tm, d), tok),
            pl.BlockSpec((tm, d), tok),
            pl.BlockSpec((ne, tm), lambda i: (0, i)),
            pl.BlockSpec((ne, tm), lambda i: (0, i)),
            pl.BlockSpec((None, ne, LANES), lambda i: (i // sub, 0, 0)),
        ),
        out_shape=(
            jax.ShapeDtypeStruct((t, d), F32),
            jax.ShapeDtypeStruct((t, d), BF16),
            jax.ShapeDtypeStruct((ne, t), F32),
            jax.ShapeDtypeStruct((ne, t), F32),
            jax.ShapeDtypeStruct((t // moe_tile, ne, LANES), F32),
        ),
        scratch_shapes=[pltpu.VMEM((ne, LANES), F32)],
        compiler_params=_cp(("arbitrary",)),
        name="merge",
    )(x2, ya, attn, sga, sgb, *consts)


def _moe_kernel(cnt_ref, h2_ref, x1_hbm, gt_ref, rt_ref, wg_ref, wu_ref, wd_ref, bg_ref, bu_ref, bd_ref, y_ref,
                pg_ref, og_ref, sem, *, tt, pair):
    j = pl.program_id(0)
    e = pl.program_id(1)
    ne = pl.num_programs(1)
    rb = MOE_ROWS
    slot = e % MOE_GROUP

    @pl.when(e == 0)
    def _():
        cp = pltpu.make_async_copy(x1_hbm.at[pl.ds(pl.multiple_of(j * pair * tt, tt), pair * tt), :], y_ref, sem)
        cp.start()
        cp.wait()

    mine = lax.broadcasted_iota(I32, (SUBLANES, pair * tt), 0) == e % SUBLANES
    g_all = jnp.sum(jnp.where(mine, gt_ref[...], 0.0), axis=0, keepdims=True)
    r_all = jnp.sum(jnp.where(mine, rt_ref[...], 0.0), axis=0, keepdims=True)
    rid = lax.broadcasted_iota(I32, (rb, tt), 0).astype(F32)

    def one_hot(s, blk):
        return jnp.broadcast_to(r_all[:, s * tt:(s + 1) * tt], (rb, tt)) == (rid + (blk * rb).astype(F32))

    def gather(s, hit):
        p = jnp.where(hit, 1.0, 0.0).astype(BF16)
        return p, _dot(p, h2_ref[s * tt:(s + 1) * tt, :]).astype(BF16)

    def expert(xg):
        a = jnp.minimum(_dot(xg, wg_ref[0]) + bg_ref[0], SWIGLU_LIMIT)
        b = jnp.clip(_dot(xg, wu_ref[0]) + bu_ref[0], -SWIGLU_LIMIT, SWIGLU_LIMIT)
        hid = a * jax.nn.sigmoid(SWIGLU_ALPHA * a) * (b + 1.0)
        return _dot(hid.astype(BF16), wd_ref[0]) + bd_ref[0]

    def gated(s, hit, o):
        g_row = jnp.broadcast_to(g_all[:, s * tt:(s + 1) * tt], (rb, tt))
        return (o * jnp.sum(jnp.where(hit, g_row, 0.0), axis=1, keepdims=True)).astype(BF16)

    hits = [one_hot(s, jnp.int32(0)) for s in range(pair)]
    gathered = [gather(s, hits[s]) for s in range(pair)]
    o = expert(jnp.concatenate([xg for _, xg in gathered], axis=0))
    r0 = pl.multiple_of(slot * rb, rb)
    for s in range(pair):
        pg_ref[s, pl.ds(r0, rb), :] = gathered[s][0]
        og_ref[s, pl.ds(r0, rb), :] = gated(s, hits[s], o[s * rb:(s + 1) * rb])

    @pl.when(slot == MOE_GROUP - 1)
    def _():
        for s in range(pair):
            y_ref[s * tt:(s + 1) * tt, :] += _dot_tn(pg_ref[s], og_ref[s])

    for s in range(pair):
        def overflow(blk, c, s=s):
            hit = one_hot(s, blk)
            p, xg = gather(s, hit)
            y_ref[s * tt:(s + 1) * tt, :] += _dot_tn(p, gated(s, hit, expert(xg)))
            return c

        n_rows = cnt_ref[(j * pair + s) * ne + e]
        lax.fori_loop(1, (n_rows + rb - 1) // rb, overflow, 0)


def _moe(h2, x1, gt, rt, cnt, ew, tt):
    t, d = h2.shape
    ne = gt.shape[0]
    nt = t // tt
    f = ew['wg'].shape[-1]
    pair = MOE_PAIR if nt % MOE_PAIR == 0 else 1
    grid_spec = pltpu.PrefetchScalarGridSpec(
        num_scalar_prefetch=1,
        grid=(nt // pair, ne),
        in_specs=[
            pl.BlockSpec((pair * tt, d), lambda j, e, c: (j, 0)),
            pl.BlockSpec(memory_space=pl.ANY),
            pl.BlockSpec((SUBLANES, pair * tt), lambda j, e, c: (e // SUBLANES, j)),
            pl.BlockSpec((SUBLANES, pair * tt), lambda j, e, c: (e // SUBLANES, j)),
            pl.BlockSpec((1, d, f), lambda j, e, c: (e, 0, 0)),
            pl.BlockSpec((1, d, f), lambda j, e, c: (e, 0, 0)),
            pl.BlockSpec((1, f, d), lambda j, e, c: (e, 0, 0)),
            pl.BlockSpec((1, 1, f), lambda j, e, c: (e, 0, 0)),
            pl.BlockSpec((1, 1, f), lambda j, e, c: (e, 0, 0)),
            pl.BlockSpec((1, 1, d), lambda j, e, c: (e, 0, 0)),
        ],
        out_specs=pl.BlockSpec((pair * tt, d), lambda j, e, c: (j, 0)),
        scratch_shapes=[pltpu.VMEM((pair, MOE_GROUP * MOE_ROWS, tt), BF16),
                        pltpu.VMEM((pair, MOE_GROUP * MOE_ROWS, d), BF16),
                        pltpu.SemaphoreType.DMA(())],
    )
    assert ne % MOE_GROUP == 0
    return pl.pallas_call(
        functools.partial(_moe_kernel, tt=tt, pair=pair),
        grid_spec=grid_spec,
        out_shape=jax.ShapeDtypeStruct((t, d), F32),
        compiler_params=_cp(("arbitrary", "arbitrary")),
        name="moe",
    )(cnt, h2, x1, gt, rt, ew['wg'], ew['wu'], ew['wd'], ew['bg'], ew['bu'], ew['bd'])


def _pad_heads(wmat, n_heads, width):
    d = wmat.shape[0]
    w3 = wmat.reshape(d, n_heads, width)
    return jnp.pad(w3, ((0, 0), (0, 0), (0, LANES - width))).reshape(d, n_heads * LANES)


def _pad_lanes(v, width=LANES):
    v = v.reshape(1, -1)
    return jnp.pad(v, ((0, 0), (0, width - v.shape[1])))


def _rel_bucket(rel):
    half = REL_BUCKETS // 2
    max_exact = half // 2
    n = jnp.abs(rel)
    large = max_exact + (jnp.log(jnp.maximum(n, 1).astype(jnp.float32) / max_exact)
                         / math.log(REL_MAX_DIST / max_exact) * (half - max_exact)).astype(jnp.int32)
    large = jnp.minimum(large, half - 1)
    return jnp.where(rel > 0, half, 0) + jnp.where(n < max_exact, n, large)


def _bias_tiles(rel_bias):
    tk = KEY_TILE
    half = REL_BUCKETS // 2
    max_exact = half // 2
    n_sat = int(math.ceil(max_exact * (REL_MAX_DIST / max_exact) ** ((half - 1 - max_exact) / (half - max_exact)))) + 2
    nd = (n_sat + 2 * tk - 2) // tk + 1
    dd = jnp.arange(nd, dtype=I32)[:, None, None]
    c = jnp.arange(tk, dtype=I32)[None, :, None]
    r = jnp.arange(tk, dtype=I32)[None, None, :]
    bucket = _rel_bucket(c - r - dd * tk)
    onehot = (bucket[..., None] == jnp.arange(REL_BUCKETS, dtype=I32)).astype(F32)
    tiles = jnp.einsum('dcrb,bh->dhcr', onehot, rel_bias.astype(F32) * LOG2E,
                       precision=lax.Precision.HIGHEST)
    return tiles


def _prep_proj(norm1_g, w_in, q_norm_g, k_norm_g, idx_k_norm_g, idx_k_norm_b, d_model):
    ssm_w = d_model // 2
    attn_w = N_HEADS * HEAD_DIM
    kv = N_KV_HEADS * HEAD_DIM
    sizes = [ssm_w, attn_w, kv, kv, IDX_HEADS * IDX_DIM, IDX_DIM, IDX_HEADS, d_model, d_model]
    pts = np.cumsum(sizes)[:-1].tolist()
    wu, wq, wk, wv, wqi, wki, wwi, wga, wgb = jnp.split(w_in, pts, axis=1)
    bf = lambda a: a.astype(BF16)
    blk = np.kron(np.eye(N_KV_HEADS), np.ones((HEAD_DIM, HEAD_DIM))) / HEAD_DIM
    wwit = jnp.pad(wwi.T, ((0, 2 * SUBLANES - IDX_HEADS), (0, 0)))
    return dict(
        g1=norm1_g.reshape(1, -1).astype(F32),
        wu=bf(wu), wq=bf(_pad_heads(wq, N_HEADS, HEAD_DIM)), wk=bf(_pad_heads(wk, N_KV_HEADS, HEAD_DIM)),
        wv=bf(_pad_heads(wv, N_KV_HEADS, HEAD_DIM)), wqi=bf(_pad_heads(wqi, IDX_HEADS, IDX_DIM)),
        wki=bf(_pad_heads(wki, 1, IDX_DIM)), wwit=bf(wwit),
        wga=bf(wga), wgb=bf(wgb), wkc=bf(wk), wvc=bf(wv),
        gq=_pad_lanes(q_norm_g.astype(F32)), gk=_pad_lanes(k_norm_g.astype(F32)),
        gkc=jnp.tile(k_norm_g.astype(F32), N_KV_HEADS).reshape(1, -1),
        gi=_pad_lanes(idx_k_norm_g.astype(F32)), bi=_pad_lanes(idx_k_norm_b.astype(F32)),
        ones_h=jnp.full((LANES, LANES), 1.0 / HEAD_DIM, BF16),
        ones_c=jnp.asarray(blk, BF16),
    )


def _prep_s5(lre, lim, log_dt, b_re, b_im, c_re, c_im, dvec, wa, wb):
    g, p = lre.shape
    ch = b_re.shape[-1]
    lam = lax.complex(lre.astype(F32), lim.astype(F32))
    dt = jnp.exp(log_dt.astype(F32))[:, None]
    a_bar = jnp.exp(lam * dt)
    b_bar = ((a_bar - 1.0) / lam)[:, :, None] * lax.complex(b_re.astype(F32), b_im.astype(F32))
    gs = g // S5_DIAG
    eye = jnp.eye(gs, dtype=F32)

    def blocks_in(m):
        return jnp.einsum('jgpc,gh->jgchp', m.reshape(S5_DIAG, gs, p, ch), eye).reshape(S5_DIAG, gs * ch, gs * p)

    def blocks_out(m):
        return jnp.einsum('jgcp,gh->jgphc', m.reshape(S5_DIAG, gs, ch, p), eye).reshape(S5_DIAG, gs * p, gs * ch)

    return dict(
        b_re=blocks_in(jnp.real(b_bar)).astype(BF16), b_im=blocks_in(jnp.imag(b_bar)).astype(BF16),
        c_re=blocks_out(c_re.astype(F32)).astype(BF16), c_im=blocks_out(-c_im.astype(F32)).astype(BF16),
        a_re=jnp.real(a_bar).reshape(1, g * p), a_im=jnp.imag(a_bar).reshape(1, g * p),
        d=dvec.reshape(1, -1).astype(F32), wa=wa.astype(BF16), wb=wb.astype(BF16),
    )


def _prep_merge(w_attn_up, w_out, norm2_g, w_router, b_router):
    d = w_attn_up.shape[1]
    wup = jnp.pad(w_attn_up.reshape(N_HEADS, HEAD_DIM, d), ((0, 0), (0, LANES - HEAD_DIM), (0, 0)))
    wr_t = w_router.astype(F32).T
    wr_hi = wr_t.astype(BF16)
    wr_lo = (wr_t - wr_hi.astype(F32)).astype(BF16)
    return dict(
        wup=wup.reshape(N_HEADS * LANES, d).astype(BF16), wout=w_out.astype(BF16),
        g2=norm2_g.reshape(1, -1).astype(F32), wr_hi=wr_hi, wr_lo=wr_lo,
        br=jnp.broadcast_to(b_router.astype(F32)[:, None], (b_router.shape[0], LANES)),
    )


def _prep_moe(wg, bg, wu, bu, wd, bd):
    return dict(wg=wg.astype(BF16), wu=wu.astype(BF16), wd=wd.astype(BF16),
                bg=bg.astype(F32)[:, None, :], bu=bu.astype(F32)[:, None, :], bd=bd.astype(F32)[:, None, :])


def _pick_tile(n, pref):
    t = min(n, pref)
    while n % t:
        t //= 2
    return t


def _pad_axis(a, axis, size):
    pad = [(0, 0)] * a.ndim
    pad[axis] = (0, size - a.shape[axis])
    return jnp.pad(a, pad)


def _trunk_layer(x, past_k, past_v, past_ik, h0_re, h0_im, pw, sw, mw, ew, bias_tiles):
    bsz, seq, d = x.shape
    t = bsz * seq
    tm = _pick_tile(seq, 512)
    u_tb, q, kp, vp, qi, kip, wt, sga, sgb, kc, vc, kic = _proj(x, pw, bsz, seq, tm)

    half = sw['a_re'].shape[1]
    if h0_re is None:
        h0 = jnp.zeros((bsz, 2 * half), F32)
    else:
        h0 = jnp.concatenate([h0_re.reshape(bsz, half), h0_im.reshape(bsz, half)], axis=1).astype(F32)
    tc = _pick_tile(seq, max(1, 512 // bsz))
    ya, hout = _s5(u_tb, h0, sw, bsz, seq, tc)
    groups = half // SSM_STATE
    s_re = hout[:, :half].reshape(bsz, groups, SSM_STATE)
    s_im = hout[:, half:].reshape(bsz, groups, SSM_STATE)

    past = 0 if past_k is None else past_k.shape[1]
    n_keys = past + seq
    lk = -(-n_keys // KEY_BLOCK) * KEY_BLOCK
    kip3 = kip.reshape(bsz, seq, LANES)
    if past:
        lane = jnp.arange(LANES)
        pk = jnp.pad(past_k.astype(F32), ((0, 0), (0, 0), (0, 0), (0, LANES - HEAD_DIM)))
        pk = jnp.where(lane == HEAD_DIM, 1.0, pk).astype(BF16)
        pv = jnp.pad(past_v.astype(F32), ((0, 0), (0, 0), (0, 0), (0, LANES - HEAD_DIM)))
        pv = jnp.where(lane == HEAD_DIM, 1.0, pv).astype(BF16)
        pik = jnp.pad(past_ik.astype(F32), ((0, 0), (0, 0), (0, LANES - IDX_DIM))).astype(BF16)
        k_all = jnp.concatenate([pk.transpose(0, 2, 1, 3), kp], axis=2)
        v_all = jnp.concatenate([pv.transpose(0, 2, 1, 3), vp], axis=2)
        ki_all = jnp.concatenate([pik, kip3], axis=1)
    else:
        k_all, v_all, ki_all = kp, vp, kip3
    k_all = _pad_axis(k_all, 2, lk)
    v_all = _pad_axis(v_all, 2, lk)
    ki_all = _pad_axis(ki_all, 1, lk)
    vt_all = v_all.reshape(bsz, N_KV_HEADS, lk // KEY_BLOCK, KEY_BLOCK, LANES).transpose(0, 1, 2, 4, 3)
    tq = KEY_TILE
    seq_q = -(-seq // tq) * tq
    q_p, qi_p, wt_p = _pad_axis(q, 2, seq_q), _pad_axis(qi, 2, seq_q), _pad_axis(wt, 2, seq_q)
    kf = k_all[..., :HEAD_DIM].astype(F32)
    kmax = jnp.sqrt(jnp.max(jnp.sum(kf * kf, axis=-1), axis=-1)).reshape(-1)
    bfar = bias_tiles[-1, :, 0, 0]
    bmax = jnp.max(jnp.abs(bias_tiles), axis=(0, 2, 3))
    stats = jnp.concatenate([kmax, bmax, bfar]).astype(F32)
    bias_tiles = bias_tiles - bfar[None, :, None, None]
    attn = _dsa(stats, q_p, qi_p, wt_p, k_all, vt_all, ki_all, bias_tiles, bsz, seq_q, past, n_keys, tq)
    attn = attn[:, :seq]

    moe_tile = _pick_tile(t, MOE_TILE)
    tm2 = _pick_tile(moe_tile, 512)
    x1, h2, gt, rt, cnt = _merge(x.reshape(t, d), ya, attn.reshape(t, attn.shape[-1]), sga, sgb, mw, tm2, moe_tile)
    cnt_i = cnt[:, :, 0].astype(I32).reshape(-1)
    y = _moe(h2, x1, gt, rt, cnt_i, ew, moe_tile)

    k_new = kc.reshape(bsz, seq, N_KV_HEADS, HEAD_DIM)
    v_new = vc.reshape(bsz, seq, N_KV_HEADS, HEAD_DIM)
    ik_new = kic.reshape(bsz, seq, IDX_DIM)
    return y.reshape(bsz, seq, d), k_new, v_new, ik_new, s_re, s_im


def kernel(x_prompt, x_sample, cache_k, cache_v, cache_idx_k, state_ssm_re, state_ssm_im, rel_bias, norm1_g, w_in, ssm_lambda_re, ssm_lambda_im, ssm_log_dt, ssm_b_re, ssm_b_im, ssm_c_re, ssm_c_im, ssm_d, ssm_w_glu_a, ssm_w_glu_b, q_norm_g, k_norm_g, idx_k_norm_g, idx_k_norm_b, w_attn_up, w_out, norm2_g, moe_w_router, moe_b_router, moe_w_gate, moe_b_gate, moe_w_up, moe_b_up, moe_w_down, moe_b_down):
    depth = w_in.shape[0]
    d_model = x_prompt.shape[-1]
    bias_tiles = _bias_tiles(rel_bias)
    xp, xs = x_prompt, x_sample
    st_p, st_s = [], []
    for l in range(depth):
        pw = _prep_proj(norm1_g[l], w_in[l], q_norm_g[l], k_norm_g[l], idx_k_norm_g[l], idx_k_norm_b[l], d_model)
        sw = _prep_s5(ssm_lambda_re[l], ssm_lambda_im[l], ssm_log_dt[l], ssm_b_re[l], ssm_b_im[l], ssm_c_re[l],
                      ssm_c_im[l], ssm_d[l], ssm_w_glu_a[l], ssm_w_glu_b[l])
        mw = _prep_merge(w_attn_up[l], w_out[l], norm2_g[l], moe_w_router[l], moe_b_router[l])
        ew = _prep_moe(moe_w_gate[l], moe_b_gate[l], moe_w_up[l], moe_b_up[l], moe_w_down[l], moe_b_down[l])
        xp, *sp = _trunk_layer(xp, None, None, None, None, None, pw, sw, mw, ew, bias_tiles)
        xs, *ss = _trunk_layer(xs, cache_k[l], cache_v[l], cache_idx_k[l], state_ssm_re[l], state_ssm_im[l],
                               pw, sw, mw, ew, bias_tiles)
        st_p.append(sp)
        st_s.append(ss)
    outs_p = [jnp.stack([s[i] for s in st_p]) for i in range(5)]
    outs_s = [jnp.stack([s[i] for s in st_s]) for i in range(5)]
    return (xp, xs, *outs_p, *outs_s)
```

```python
import functools
import math

import numpy as np
import jax
import jax.numpy as jnp
from jax import lax
from jax.experimental import pallas as pl
from jax.experimental.pallas import tpu as pltpu

F32 = jnp.float32
BF16 = jnp.bfloat16
I32 = jnp.int32

LANES = 128
SUBLANES = 8
VMEM_LIMIT = 56 * 1024 * 1024

CHUNK = 64
SSM_GROUP_CH = 16
SSM_STATE = 64
N_HEADS = 8
HEAD_DIM = 64
N_KV_HEADS = 2
KV_REP = N_HEADS // N_KV_HEADS
IDX_HEADS = 8
IDX_DIM = 64
TOPK_MAX = 256
REL_BUCKETS = 32
REL_MAX_DIST = 1024
N_EXPERTS = 32
TOP_K = 4
SWIGLU_LIMIT = 7.0
SWIGLU_ALPHA = 1.702
EPS = 1e-6

KEY_TILE = 128
KEY_BLOCK = 256
SCORE_CHUNK = 4
ATTEND_CHUNK = 4
LOG2E = math.log2(math.e)
NEG_BIG = -1e30
SHIFT_LIMIT = 30.0
S5_DIAG = 2
SEARCH_FIRST = 8
SEARCH_GROUP = 4
MOE_TILE = 1024
MOE_ROWS = 160
MOE_GROUP = 8
MOE_PAIR = 2


def _cp(sem):
    return pltpu.CompilerParams(dimension_semantics=sem, vmem_limit_bytes=VMEM_LIMIT)


def _dot(a, b):
    return jnp.dot(a, b, preferred_element_type=F32)


def _dot_nt(a, b):
    return lax.dot_general(a, b, (((1,), (1,)), ((), ())), preferred_element_type=F32)


def _dot_tn(a, b):
    return lax.dot_general(a, b, (((0,), (0,)), ((), ())), preferred_element_type=F32)


def _split(a):
    hi = a.astype(BF16)
    lo = (a - hi.astype(F32)).astype(BF16)
    return hi, lo


def _dot_split(a, g):
    hi, lo = _split(a)
    return _dot(hi, g) + _dot(lo, g)


def _proj_kernel(x_ref, g1_ref, wu_ref, wq_ref, wk_ref, wv_ref, wqi_ref, wki_ref, wwit_ref, wga_ref, wgb_ref,
                 wkc_ref, wvc_ref, gq_ref, gk_ref, gkc_ref, gi_ref, bi_ref, ones_h_ref, ones_c_ref,
                 u_ref, q_ref, kp_ref, vp_ref, qi_ref, kip_ref, wt_ref, sga_ref, sgb_ref,
                 kc_ref, vc_ref, kic_ref):
    x = x_ref[...]
    ms = jnp.mean(x * x, axis=-1, keepdims=True)
    hn = (x * lax.rsqrt(ms + EPS) * g1_ref[...]).astype(BF16)
    ones_h = ones_h_ref[...]
    lane = lax.broadcasted_iota(I32, (x.shape[0], LANES), 1)

    u_ref[...] = _dot(hn, wu_ref[...]).astype(BF16)

    q = _dot(hn, wq_ref[...])
    scale = HEAD_DIM ** -0.5 * LOG2E
    for h in range(N_HEADS):
        qh = q[:, h * LANES:(h + 1) * LANES]
        msq = _dot_split(qh * qh, ones_h)
        q_ref[h] = (qh * lax.rsqrt(msq + EPS) * (gq_ref[...] * scale)).astype(BF16)

    k = _dot(hn, wk_ref[...])
    for g in range(N_KV_HEADS):
        kg = k[:, g * LANES:(g + 1) * LANES]
        msk = _dot_split(kg * kg, ones_h)
        kn = kg * lax.rsqrt(msk + EPS) * gk_ref[...]
        kp_ref[g] = jnp.where(lane == HEAD_DIM, 1.0, kn).astype(BF16)

    v = _dot(hn, wv_ref[...])
    for g in range(N_KV_HEADS):
        vg = v[:, g * LANES:(g + 1) * LANES]
        vp_ref[g] = jnp.where(lane == HEAD_DIM, 1.0, vg).astype(BF16)

    qi = _dot(hn, wqi_ref[...])
    for h in range(IDX_HEADS):
        qi_ref[h] = qi[:, h * LANES:(h + 1) * LANES].astype(BF16)

    ki = _dot(hn, wki_ref[...])
    mu = _dot_split(ki, ones_h)
    xc = jnp.where(lane < IDX_DIM, ki - mu, 0.0)
    var = _dot_split(xc * xc, ones_h)
    kin = xc * lax.rsqrt(var + EPS) * gi_ref[...] + bi_ref[...]
    kip_ref[...] = kin.astype(BF16)
    kic_ref[...] = kin[:, :IDX_DIM]

    wt = _dot_nt(wwit_ref[...], hn)
    wt_ref[...] = wt[0:IDX_HEADS, :] * (IDX_HEADS ** -0.5 * IDX_DIM ** -0.5)

    sga_ref[...] = jax.nn.sigmoid(_dot(hn, wga_ref[...])).astype(BF16)
    sgb_ref[...] = jax.nn.sigmoid(_dot(hn, wgb_ref[...])).astype(BF16)

    kc = _dot(hn, wkc_ref[...])
    mskc = _dot_split(kc * kc, ones_c_ref[...])
    kc_ref[...] = kc * lax.rsqrt(mskc + EPS) * gkc_ref[...]
    vc_ref[...] = _dot(hn, wvc_ref[...])


def _proj(x, pw, bsz, seq, tm):
    d = x.shape[-1]
    nt = seq // tm
    t = bsz * seq
    x2 = x.reshape(t, d)

    def tok(b, i):
        return (b * nt + i, 0)

    def cst(b, i):
        return (0, 0)

    def wspec(a):
        return pl.BlockSpec(a.shape, cst)

    weights = [pw['g1'], pw['wu'], pw['wq'], pw['wk'], pw['wv'], pw['wqi'], pw['wki'], pw['wwit'], pw['wga'],
               pw['wgb'], pw['wkc'], pw['wvc'], pw['gq'], pw['gk'], pw['gkc'], pw['gi'], pw['bi'],
               pw['ones_h'], pw['ones_c']]
    ssm_w = pw['wu'].shape[1]
    out_shape = (
        jax.ShapeDtypeStruct((t, ssm_w), BF16),
        jax.ShapeDtypeStruct((bsz, N_HEADS, seq, LANES), BF16),
        jax.ShapeDtypeStruct((bsz, N_KV_HEADS, seq, LANES), BF16),
        jax.ShapeDtypeStruct((bsz, N_KV_HEADS, seq, LANES), BF16),
        jax.ShapeDtypeStruct((bsz, IDX_HEADS, seq, LANES), BF16),
        jax.ShapeDtypeStruct((t, LANES), BF16),
        jax.ShapeDtypeStruct((bsz, IDX_HEADS, seq), F32),
        jax.ShapeDtypeStruct((t, d), BF16),
        jax.ShapeDtypeStruct((t, d), BF16),
        jax.ShapeDtypeStruct((t, N_KV_HEADS * HEAD_DIM), F32),
        jax.ShapeDtypeStruct((t, N_KV_HEADS * HEAD_DIM), F32),
        jax.ShapeDtypeStruct((t, IDX_DIM), F32),
    )

    def hm(nh):
        return pl.BlockSpec((None, nh, tm, LANES), lambda b, i: (b, 0, i, 0))

    out_specs = (
        pl.BlockSpec((tm, ssm_w), tok),
        hm(N_HEADS), hm(N_KV_HEADS), hm(N_KV_HEADS), hm(IDX_HEADS),
        pl.BlockSpec((tm, LANES), tok),
        pl.BlockSpec((None, IDX_HEADS, tm), lambda b, i: (b, 0, i)),
        pl.BlockSpec((tm, d), tok), pl.BlockSpec((tm, d), tok),
        pl.BlockSpec((tm, N_KV_HEADS * HEAD_DIM), tok), pl.BlockSpec((tm, N_KV_HEADS * HEAD_DIM), tok),
        pl.BlockSpec((tm, IDX_DIM), tok),
    )
    return pl.pallas_call(
        _proj_kernel,
        grid=(bsz, nt),
        in_specs=[pl.BlockSpec((tm, d), tok)] + [wspec(a) for a in weights],
        out_specs=out_specs,
        out_shape=out_shape,
        compiler_params=_cp(("arbitrary", "arbitrary")),
        name="proj",
    )(x2, *weights)


def _gelu_tanh(x):
    return 0.5 * x * (1.0 + jnp.tanh(math.sqrt(2.0 / math.pi) * (x + 0.044715 * (x * x * x))))


def _s5_kernel(u_ref, h0_ref, bre_ref, bim_ref, are_ref, aim_ref, cre_ref, cim_ref, dvec_ref, wa_ref, wb_ref,
               ya_ref, hout_ref, state_ref, bu_ref, yf_ref, ug_ref, *, bsz, tc, strip):
    s = pl.program_id(0)
    half = are_ref.shape[1]

    @pl.when(s == 0)
    def _():
        state_ref[...] = h0_ref[...]

    ssm_w = dvec_ref.shape[1]
    for b in range(bsz):
        for c in range(ssm_w // LANES):
            ug_ref[c, pl.ds(b, tc, stride=bsz), :] = u_ref[b, :, c * LANES:(c + 1) * LANES].astype(F32)
    u = jnp.concatenate([ug_ref[c] for c in range(ssm_w // LANES)], axis=1).astype(BF16)
    cw = u.shape[1] // S5_DIAG
    sw = half // S5_DIAG
    for j in range(S5_DIAG):
        uj = u[:, j * cw:(j + 1) * cw]
        bu_ref[:, j * sw:(j + 1) * sw] = _dot(uj, bre_ref[j])
        bu_ref[:, half + j * sw:half + (j + 1) * sw] = _dot(uj, bim_ref[j])

    for c0 in range(0, half, strip):
        ar = jnp.broadcast_to(are_ref[:, c0:c0 + strip], (bsz, strip))
        ai = jnp.broadcast_to(aim_ref[:, c0:c0 + strip], (bsz, strip))
        hr0 = state_ref[:, c0:c0 + strip]
        hi0 = state_ref[:, half + c0:half + c0 + strip]

        def step(t, carry):
            hr, hi = carry
            r0 = pl.multiple_of(t * bsz, bsz)
            br = bu_ref[pl.ds(r0, bsz), c0:c0 + strip]
            bi = bu_ref[pl.ds(r0, bsz), half + c0:half + c0 + strip]
            nr = ar * hr - ai * hi + br
            ni = ar * hi + ai * hr + bi
            bu_ref[pl.ds(r0, bsz), c0:c0 + strip] = nr
            bu_ref[pl.ds(r0, bsz), half + c0:half + c0 + strip] = ni
            return nr, ni

        hr, hi = lax.fori_loop(0, tc, step, (hr0, hi0))
        state_ref[:, c0:c0 + strip] = hr
        state_ref[:, half + c0:half + c0 + strip] = hi

    ys = []
    for j in range(S5_DIAG):
        s_re = bu_ref[:, j * sw:(j + 1) * sw].astype(BF16)
        s_im = bu_ref[:, half + j * sw:half + (j + 1) * sw].astype(BF16)
        ys.append(_dot(s_re, cre_ref[j]) + _dot(s_im, cim_ref[j]))
    y = jnp.concatenate(ys, axis=1) + dvec_ref[...] * u.astype(F32)
    g = _gelu_tanh(y).astype(BF16)
    ya = _dot(g, wa_ref[...]) * jax.nn.sigmoid(_dot(g, wb_ref[...]))
    n_chunk = ya.shape[1] // LANES
    for c in range(n_chunk):
        yf_ref[c] = ya[:, c * LANES:(c + 1) * LANES]
    for b in range(bsz):
        ya_ref[b] = jnp.concatenate([yf_ref[c, pl.ds(b, tc, stride=bsz), :] for c in range(n_chunk)],
                                    axis=1).astype(BF16)

    @pl.when(s == pl.num_programs(0) - 1)
    def _():
        hout_ref[...] = state_ref[...]


def _s5(u, h0, sw, bsz, seq, tc):
    rows = tc * bsz
    ssm_w = sw['d'].shape[1]
    half = sw['a_re'].shape[1]
    two_half = 2 * half
    d = sw['wa'].shape[1]
    u3 = u.reshape(bsz, seq, ssm_w)
    strip = min(512, half)

    consts = [h0, sw['b_re'], sw['b_im'], sw['a_re'], sw['a_im'], sw['c_re'], sw['c_im'], sw['d'], sw['wa'], sw['wb']]

    def cst(s):
        return (0, 0)

    def cspec(a):
        return pl.BlockSpec(a.shape, lambda s: (0,) * a.ndim)

    ya, hout = pl.pallas_call(
        functools.partial(_s5_kernel, bsz=bsz, tc=tc, strip=strip),
        grid=(seq // tc,),
        in_specs=[pl.BlockSpec((bsz, tc, ssm_w), lambda s: (0, s, 0))] + [cspec(a) for a in consts],
        out_specs=(pl.BlockSpec((bsz, tc, d), lambda s: (0, s, 0)), pl.BlockSpec((bsz, two_half), cst)),
        out_shape=(jax.ShapeDtypeStruct((bsz, seq, d), BF16), jax.ShapeDtypeStruct((bsz, two_half), F32)),
        scratch_shapes=[pltpu.VMEM((bsz, two_half), F32), pltpu.VMEM((rows, two_half), F32),
                        pltpu.VMEM((d // LANES, rows, LANES), F32), pltpu.VMEM((ssm_w // LANES, rows, LANES), F32)],
        compiler_params=_cp(("arbitrary",)),
        name="s5",
    )(u3, *consts)
    return ya.reshape(bsz * seq, d), hout


def _f2key(x):
    b = lax.bitcast_convert_type(x, I32)
    return b ^ ((b >> 31) & 0x7FFFFFFF)


def _key2f(k):
    return lax.bitcast_convert_type(k ^ ((k >> 31) & 0x7FFFFFFF), F32)


def _dsa_kernel(st_ref, q_ref, qi_ref, wt_ref, k_ref, vt_ref, ki_ref, bias_ref, o_ref,
                s_ref, lo_ref, hi_ref, clo_ref, glo_ref, ghi_ref, side_ref, q2_ref, mrow_ref, acc_ref,
                *, bsz, tq, past, n_keys, topk, nkt, nd, idx_bits, near_max):
    b_id = pl.program_id(0)
    i = pl.program_id(1)
    kb = KEY_BLOCK
    sl = SUBLANES
    q0 = past + i * tq
    last_chunk = (q0 + tq - 1) // CHUNK
    n_kt = jnp.minimum(nkt, ((last_chunk + 1) * CHUNK + kb - 1) // kb)
    d0 = q0 // KEY_TILE

    krow = lax.broadcasted_iota(I32, (kb, tq), 0)
    q_chunk = (q0 + lax.broadcasted_iota(I32, (kb, tq), 1)) // CHUNK
    qc8 = (q0 + lax.broadcasted_iota(I32, (sl, tq), 1)) // CHUNK
    n_adm = jnp.minimum((qc8 + 1) * CHUNK, n_keys)
    n_admf = n_adm.astype(F32)
    is_pad = q0 + lax.broadcasted_iota(I32, (sl, tq), 1) >= n_keys
    needf = jnp.where(is_pad, n_adm, jnp.minimum(topk, n_adm)).astype(F32)

    def bcast(x):
        return jnp.broadcast_to(x[0:1, :], (kb, tq))

    def rep(x):
        return jnp.broadcast_to(x, (sl, tq))

    qi = qi_ref[...].reshape(IDX_HEADS * tq, LANES)

    def score_blocks(kt, nb, masked):
        k0 = pl.multiple_of(kt * kb, kb)
        s = _dot_nt(ki_ref[pl.ds(k0, nb * kb), :], qi)
        for j in range(nb):
            sc = jnp.zeros((kb, tq), F32)
            for h in range(IDX_HEADS):
                sc = sc + wt_ref[h:h + 1, :] * jnp.maximum(s[j * kb:(j + 1) * kb, h * tq:(h + 1) * tq], 0.0)
            if masked:
                kpos = k0 + j * kb + krow
                adm = ((kpos // CHUNK) <= q_chunk) & (kpos < n_keys)
                sc = jnp.where(adm, sc, -jnp.inf)
            s_ref[kt + j] = sc

    n_open = n_kt - 1

    def score_chunk(j, c):
        score_blocks(SCORE_CHUNK * j, SCORE_CHUNK, False)
        return c

    lax.fori_loop(0, n_open // SCORE_CHUNK, score_chunk, 0)

    def score_single(j, c):
        score_blocks(j, 1, False)
        return c

    lax.fori_loop(n_open - n_open % SCORE_CHUNK, n_open, score_single, 0)
    score_blocks(n_kt - 1, 1, True)
    s_ref[n_kt] = jnp.full((kb, tq), -jnp.inf, F32)
    n_pair = (n_kt + 1) // 2

    part = 4 * sl

    def fold(x, op):
        x = x.reshape(kb // part, part, tq)
        acc = x[0]
        for j in range(1, kb // part):
            acc = op(acc, x[j])
        return acc

    def count(pred):
        def one(kt):
            return fold(jnp.where(pred(s_ref[kt], kt), 1.0, 0.0), jnp.add)
        c = lax.fori_loop(0, n_pair, lambda j, c: c + (one(2 * j) + one(2 * j + 1)), jnp.zeros((part, tq), F32))
        return rep(jnp.sum(c, axis=0, keepdims=True))

    def minmax(j, c):
        mx, mn = c
        for kt in (2 * j, 2 * j + 1):
            s = s_ref[kt]
            mx = jnp.maximum(mx, fold(s, jnp.maximum))
            mn = jnp.minimum(mn, fold(jnp.where(s == -jnp.inf, jnp.inf, s), jnp.minimum))
        return mx, mn

    mx, mn = lax.fori_loop(0, n_pair, minmax,
                           (jnp.full((part, tq), -jnp.inf, F32), jnp.full((part, tq), jnp.inf, F32)))
    lo_ref[...] = rep(jnp.min(mn, axis=0, keepdims=True))
    hi_ref[...] = _key2f(_f2key(rep(jnp.max(mx, axis=0, keepdims=True))) + 1)
    def odds(cnt):
        c = jnp.clip(cnt, 0.5, n_admf - 0.5)
        return jnp.log((n_admf - c) / c)

    target = odds(needf - 0.5)
    clo_ref[...] = n_admf
    glo_ref[...] = target - odds(n_admf)
    ghi_ref[...] = target - odds(jnp.zeros((sl, tq), F32))
    side_ref[...] = jnp.zeros((sl, tq), F32)

    def searching(lo, hi, clo):
        return (_f2key(hi) > _f2key(lo) + 1) & (clo > needf)

    def refine(it, c):
        lo, hi, clo = lo_ref[...], hi_ref[...], clo_ref[...]
        glo, ghi, side = glo_ref[...], ghi_ref[...], side_ref[...]
        k_t = _f2key(lo + (hi - lo) * (glo / (glo - ghi)))
        t = _key2f(jnp.minimum(jnp.maximum(k_t, _f2key(lo) + 1), _f2key(hi) - 1))
        tb = bcast(t)
        cnt = count(lambda s, kt: s >= tb)
        g = target - odds(cnt)
        open_ = searching(lo, hi, clo)
        up = open_ & (cnt >= needf)
        dn = open_ & (cnt < needf)
        lo_ref[...] = jnp.where(up, t, lo)
        clo_ref[...] = jnp.where(up, cnt, clo)
        hi_ref[...] = jnp.where(dn, t, hi)
        glo_ref[...] = jnp.where(up, g, jnp.where(dn & (side < 0.0), glo * 0.5, glo))
        ghi_ref[...] = jnp.where(dn, g, jnp.where(up & (side > 0.0), ghi * 0.5, ghi))
        side_ref[...] = jnp.where(up, 1.0, jnp.where(dn, -1.0, side))
        return c

    def snap():
        lo, hi, clo = lo_ref[...], hi_ref[...], clo_ref[...]
        lo_b, hi_b = bcast(lo), bcast(hi)

        def body(j, c):
            a, b = c
            for kt in (2 * j, 2 * j + 1):
                s = s_ref[kt]
                a = jnp.minimum(a, fold(jnp.where(s >= lo_b, s, jnp.inf), jnp.minimum))
                b = jnp.maximum(b, fold(jnp.where(s < hi_b, s, -jnp.inf), jnp.maximum))
            return a, b

        a, b = lax.fori_loop(0, n_pair, body,
                             (jnp.full((part, tq), jnp.inf, F32), jnp.full((part, tq), -jnp.inf, F32)))
        open_ = searching(lo, hi, clo)
        lo_ref[...] = jnp.where(open_, rep(jnp.min(a, axis=0, keepdims=True)), lo)
        hi_ref[...] = jnp.where(open_, _key2f(_f2key(rep(jnp.max(b, axis=0, keepdims=True))) + 1), hi)

    def n_searching():
        return jnp.max(jnp.where(searching(lo_ref[...], hi_ref[...], clo_ref[...]), 1.0, 0.0))

    def group(c):
        grp, _ = c
        lax.fori_loop(0, jnp.where(grp == 0, SEARCH_FIRST, SEARCH_GROUP), refine, 0)
        snap()
        return grp + 1, n_searching()

    lax.while_loop(lambda c: c[1] > 0.0, group, (jnp.int32(0), n_searching()))
    thr = lo_ref[...]
    thr_b = bcast(thr)

    n_tied = jnp.max(jnp.where(clo_ref[...] > needf, 1.0, 0.0))

    @pl.when(n_tied > 0.0)
    def _():
        rem_b = bcast(needf - count(lambda s, kt: s > thr_b))
        tri = jnp.where(lax.broadcasted_iota(I32, (kb, kb), 0) >= lax.broadcasted_iota(I32, (kb, kb), 1),
                        1.0, 0.0).astype(BF16)

        def drop(kt, seen):
            s = s_ref[kt]
            tie = s == thr_b
            rank = _dot(tri, jnp.where(tie, 1.0, 0.0).astype(BF16)) + bcast(seen)
            s_ref[kt] = jnp.where(tie & (rank > rem_b), -jnp.inf, s)
            return rep(rank[kb - 1:kb, :])

        lax.fori_loop(0, n_kt, drop, jnp.zeros((sl, tq), F32))

    rows_g = KV_REP * tq
    qf = q_ref[...].reshape(N_HEADS * tq, LANES).astype(F32)
    qn = jnp.sqrt(jnp.sum(qf * qf, axis=1, keepdims=True))
    lane = lax.broadcasted_iota(I32, (tq, LANES), 1)
    worst = jnp.float32(0.0)
    for h in range(N_HEADS):
        kmax = st_ref[b_id * N_KV_HEADS + h // KV_REP]
        bmax = st_ref[bsz * N_KV_HEADS + h]
        bfar = st_ref[bsz * N_KV_HEADS + N_HEADS + h]
        bound = qn[h * tq:(h + 1) * tq, :] * (kmax * 1.01) + (bmax + 0.1)
        worst = jnp.maximum(worst, jnp.max(bound))
        q2_ref[h * tq:(h + 1) * tq, :] = jnp.where(lane == HEAD_DIM, bfar - bound,
                                                   qf[h * tq:(h + 1) * tq, :]).astype(BF16)
    n_far = jnp.clip((d0 - nd) // 2 + 1, 0, n_kt)

    def logits(kt0, nb, g, near, exact):
        k0 = pl.multiple_of(kt0 * kb, kb)
        s = _dot_nt(k_ref[g, pl.ds(k0, nb * kb), :], q2_ref[g * rows_g:(g + 1) * rows_g, :])
        out = []
        for j in range(nb):
            kt = kt0 + j
            maskadd = jnp.where(s_ref[kt] >= thr_b, 0.0, NEG_BIG)
            if near:
                da = jnp.clip(d0 - 2 * kt, 0, nd - 1)
                db = jnp.clip(d0 - 2 * kt - 1, 0, nd - 1)
            parts = []
            for r in range(KV_REP):
                h = g * KV_REP + r
                add = maskadd - mrow_ref[0:1, h * tq:(h + 1) * tq] if exact else maskadd
                if near:
                    add = jnp.concatenate([bias_ref[da, h], bias_ref[db, h]], axis=0) + add
                parts.append(s[j * kb:(j + 1) * kb, r * tq:(r + 1) * tq] + add)
            out.append(jnp.concatenate(parts, axis=1))
        return out

    def over_blocks(fn):
        def far_chunk(j, c):
            fn(ATTEND_CHUNK * j, ATTEND_CHUNK, False)
            return c
        lax.fori_loop(0, n_far // ATTEND_CHUNK, far_chunk, 0)

        def far_single(kt, c):
            fn(kt, 1, False)
            return c
        lax.fori_loop(n_far - n_far % ATTEND_CHUNK, n_far, far_single, 0)

        n_near = n_kt - n_far
        for nb in range(1, near_max + 1):
            @pl.when(n_near == nb)
            def _(nb=nb):
                fn(n_far, nb, True)

    def attend(exact):
        acc_ref[...] = jnp.zeros(acc_ref.shape, F32)

        def blocks(kt0, nb, near):
            for g in range(N_KV_HEADS):
                p = jnp.concatenate([jnp.exp2(lg).astype(BF16) for lg in logits(kt0, nb, g, near, exact)], axis=0)
                vt = jnp.concatenate([vt_ref[g, kt0 + j] for j in range(nb)], axis=1)
                acc_ref[g] += _dot(vt, p)
        over_blocks(blocks)

    @pl.when(worst <= SHIFT_LIMIT)
    def _():
        attend(False)

    @pl.when(worst > SHIFT_LIMIT)
    def _():
        mrow_ref[...] = jnp.full(mrow_ref.shape, NEG_BIG, F32)

        def blocks(kt0, nb, near):
            for g in range(N_KV_HEADS):
                for lg in logits(kt0, nb, g, near, False):
                    mx = jnp.max(lg, axis=0, keepdims=True)
                    cur = mrow_ref[:, g * rows_g:(g + 1) * rows_g]
                    mrow_ref[:, g * rows_g:(g + 1) * rows_g] = jnp.maximum(cur, jnp.broadcast_to(mx, (sl, rows_g)))
        over_blocks(blocks)
        attend(True)

    for g in range(N_KV_HEADS):
        acc = acc_ref[g]
        og = acc / acc[HEAD_DIM:HEAD_DIM + 1, :]
        for r in range(KV_REP):
            h = g * KV_REP + r
            o_ref[:, h * LANES:(h + 1) * LANES] = og[:, r * tq:(r + 1) * tq].T.astype(BF16)


def _dsa(stats, q, qi, wt, k_all, vt_all, ki_all, bias_tiles, bsz, seq, past, n_keys, tq):
    lk = k_all.shape[2]
    nkt = lk // KEY_BLOCK
    topk = min(TOPK_MAX, n_keys // 4)
    nd = bias_tiles.shape[0]
    nq = seq // tq
    assert past % KEY_TILE == 0 and tq == KEY_TILE
    idx_bits = int(math.ceil(math.log2(lk))) + 1
    near_max = 0
    for i in range(nq):
        q0 = past + i * tq
        n_kt = min(nkt, (((q0 + tq - 1) // CHUNK + 1) * CHUNK + KEY_BLOCK - 1) // KEY_BLOCK)
        n_far = min(max((q0 // KEY_TILE - nd) // 2 + 1, 0), n_kt)
        near_max = max(near_max, n_kt - n_far)
    kern = functools.partial(_dsa_kernel, bsz=bsz, tq=tq, past=past, n_keys=n_keys, topk=topk, nkt=nkt, nd=nd,
                             idx_bits=idx_bits, near_max=near_max)
    row_state = pltpu.VMEM((SUBLANES, tq), F32)
    grid_spec = pltpu.PrefetchScalarGridSpec(
        num_scalar_prefetch=1,
        grid=(bsz, nq),
        in_specs=[
            pl.BlockSpec((None, N_HEADS, tq, LANES), lambda b, i, s: (b, 0, i, 0)),
            pl.BlockSpec((None, IDX_HEADS, tq, LANES), lambda b, i, s: (b, 0, i, 0)),
            pl.BlockSpec((None, IDX_HEADS, tq), lambda b, i, s: (b, 0, i)),
            pl.BlockSpec((None, N_KV_HEADS, lk, LANES), lambda b, i, s: (b, 0, 0, 0)),
            pl.BlockSpec((None, N_KV_HEADS, nkt, LANES, KEY_BLOCK), lambda b, i, s: (b, 0, 0, 0, 0)),
            pl.BlockSpec((None, lk, LANES), lambda b, i, s: (b, 0, 0)),
            pl.BlockSpec(bias_tiles.shape, lambda b, i, s: (0, 0, 0, 0)),
        ],
        out_specs=pl.BlockSpec((None, tq, N_HEADS * LANES), lambda b, i, s: (b, i, 0)),
        scratch_shapes=[
            pltpu.VMEM((nkt + 1, KEY_BLOCK, tq), F32),
            row_state, row_state, row_state, row_state, row_state, row_state,
            pltpu.VMEM((N_HEADS * tq, LANES), BF16),
            pltpu.VMEM((SUBLANES, N_HEADS * tq), F32),
            pltpu.VMEM((N_KV_HEADS, LANES, KV_REP * tq), F32),
        ],
    )
    return pl.pallas_call(
        kern,
        grid_spec=grid_spec,
        out_shape=jax.ShapeDtypeStruct((bsz, seq, N_HEADS * LANES), BF16),
        compiler_params=_cp(("arbitrary", "arbitrary")),
        name="dsa",
    )(stats, q, qi, wt, k_all, vt_all, ki_all, bias_tiles)


def _merge_kernel(x_ref, ya_ref, at_ref, sga_ref, sgb_ref, wup_ref, wout_ref, g2_ref, wr_hi_ref, wr_lo_ref, br_ref,
                  x1_ref, h2_ref, gt_ref, rt_ref, cnt_ref, run_ref, *, tm, sub):
    step = pl.program_id(0)

    @pl.when(step % sub == 0)
    def _():
        run_ref[...] = jnp.zeros(run_ref.shape, F32)

    yb = _dot(at_ref[...], wup_ref[...])
    merged = sga_ref[...].astype(F32) * ya_ref[...].astype(F32) + sgb_ref[...].astype(F32) * yb
    x1 = x_ref[...] + _dot(merged.astype(BF16), wout_ref[...])
    x1_ref[...] = x1
    ms = jnp.mean(x1 * x1, axis=-1, keepdims=True)
    h2 = x1 * lax.rsqrt(ms + EPS) * g2_ref[...]
    h2_hi, h2_lo = _split(h2)
    h2_ref[...] = h2_hi

    wr_hi = wr_hi_ref[...]
    logit = (_dot_nt(wr_hi, h2_hi) + _dot_nt(wr_hi, h2_lo) + _dot_nt(wr_lo_ref[...], h2_hi)) + br_ref[:, 0:1]
    ne = logit.shape[0]
    eid = lax.broadcasted_iota(I32, (ne, tm), 0).astype(F32)
    selb = jnp.zeros((ne, tm), F32)
    tops = []
    picks = []
    for _ in range(TOP_K):
        mx = jnp.max(logit, axis=0, keepdims=True)
        pick = jnp.min(jnp.where(logit == mx, eid, float(ne)), axis=0, keepdims=True)
        hit = eid == pick
        selb = jnp.where(hit, 1.0, selb)
        logit = jnp.where(hit, -jnp.inf, logit)
        tops.append(mx)
        picks.append(hit)
    ex = [jnp.exp(t - tops[0]) for t in tops]
    den = ex[0] + ex[1] + ex[2] + ex[3]
    gate = jnp.zeros((ne, tm), F32)
    for hit, e in zip(picks, ex):
        gate = jnp.where(hit, e / den, gate)
    gt_ref[...] = gate

    sel = selb > 0.5
    selb = selb.astype(BF16)
    r_i = lax.broadcasted_iota(I32, (tm, tm), 0)
    c_i = lax.broadcasted_iota(I32, (tm, tm), 1)
    tri = jnp.where(r_i < c_i, 1.0, 0.0).astype(BF16)
    run = run_ref[...]
    rank = _dot(selb, tri) + jnp.broadcast_to(run[:, 0:1], (ne, tm))
    rt_ref[...] = jnp.where(sel, rank, -1.0)
    run = run + _dot(selb, jnp.ones((tm, LANES), BF16))
    run_ref[...] = run
    cnt_ref[...] = run


def _merge(x2, ya, attn, sga, sgb, mw, tm, moe_tile):
    t, d = x2.shape
    sub = moe_tile // tm
    ne = mw['wr_hi'].shape[0]

    def tok(i):
        return (i, 0)

    def cst(i):
        return (0, 0)

    consts = [mw['wup'], mw['wout'], mw['g2'], mw['wr_hi'], mw['wr_lo'], mw['br']]
    return pl.pallas_call(
        functools.partial(_merge_kernel, tm=tm, sub=sub),
        grid=(t // tm,),
        in_specs=[
            pl.BlockSpec((tm, d), tok),
            pl.BlockSpec((tm, d), tok),
            pl.BlockSpec((tm, attn.shape[-1]), tok),
            pl.BlockSpec((tm, d), tok),
            pl.BlockSpec((tm, d), tok),
        ] + [pl.BlockSpec(a.shape, cst) for a in consts],
        out_specs=(
            pl.BlockSpec((tm, d), tok),
            pl.BlockSpec((tm, d), tok),
            pl.BlockSpec((ne, tm), lambda i: (0, i)),
            pl.BlockSpec((ne, tm), lambda i: (0, i)),
            pl.BlockSpec((None, ne, LANES), lambda i: (i // sub, 0, 0)),
        ),
        out_shape=(
            jax.ShapeDtypeStruct((t, d), F32),
            jax.ShapeDtypeStruct((t, d), BF16),
            jax.ShapeDtypeStruct((ne, t), F32),
            jax.ShapeDtypeStruct((ne, t), F32),
            jax.ShapeDtypeStruct((t // moe_tile, ne, LANES), F32),
        ),
        scratch_shapes=[pltpu.VMEM((ne, LANES), F32)],
        compiler_params=_cp(("arbitrary",)),
        name="merge",
    )(x2, ya, attn, sga, sgb, *consts)


def _moe_kernel(cnt_ref, h2_ref, x1_hbm, gt_ref, rt_ref, wg_ref, wu_ref, wd_ref, bg_ref, bu_ref, bd_ref, y_ref,
                pg_ref, og_ref, sem, *, tt, pair):
    j = pl.program_id(0)
    e = pl.program_id(1)
    ne = pl.num_programs(1)
    rb = MOE_ROWS
    slot = e % MOE_GROUP

    @pl.when(e == 0)
    def _():
        cp = pltpu.make_async_copy(x1_hbm.at[pl.ds(pl.multiple_of(j * pair * tt, tt), pair * tt), :], y_ref, sem)
        cp.start()
        cp.wait()

    mine = lax.broadcasted_iota(I32, (SUBLANES, pair * tt), 0) == e % SUBLANES
    g_all = jnp.sum(jnp.where(mine, gt_ref[...], 0.0), axis=0, keepdims=True)
    r_all = jnp.sum(jnp.where(mine, rt_ref[...], 0.0), axis=0, keepdims=True)
    rid = lax.broadcasted_iota(I32, (rb, tt), 0).astype(F32)

    def one_hot(s, blk):
        return jnp.broadcast_to(r_all[:, s * tt:(s + 1) * tt], (rb, tt)) == (rid + (blk * rb).astype(F32))

    def gather(s, hit):
        p = jnp.where(hit, 1.0, 0.0).astype(BF16)
        return p, _dot(p, h2_ref[s * tt:(s + 1) * tt, :]).astype(BF16)

    def expert(xg):
        a = jnp.minimum(_dot(xg, wg_ref[0]) + bg_ref[0], SWIGLU_LIMIT)
        b = jnp.clip(_dot(xg, wu_ref[0]) + bu_ref[0], -SWIGLU_LIMIT, SWIGLU_LIMIT)
        hid = a * jax.nn.sigmoid(SWIGLU_ALPHA * a) * (b + 1.0)
        return _dot(hid.astype(BF16), wd_ref[0]) + bd_ref[0]

    def gated(s, hit, o):
        g_row = jnp.broadcast_to(g_all[:, s * tt:(s + 1) * tt], (rb, tt))
        return (o * jnp.sum(jnp.where(hit, g_row, 0.0), axis=1, keepdims=True)).astype(BF16)

    @pl.when(slot == 0)
    def _():
        for s in range(pair):
            for k in range(MOE_GROUP):
                r_k = jnp.broadcast_to(rt_ref[k:k + 1, s * tt:(s + 1) * tt], (rb, tt))
                pg_ref[s, k * rb:(k + 1) * rb, :] = jnp.where(r_k == rid, 1.0, 0.0).astype(BF16)
            og_ref[s] = _dot(pg_ref[s], h2_ref[s * tt:(s + 1) * tt, :]).astype(BF16)

    r0 = pl.multiple_of(slot * rb, rb)
    o = expert(jnp.concatenate([og_ref[s, pl.ds(r0, rb), :] for s in range(pair)], axis=0))
    for s in range(pair):
        og_ref[s, pl.ds(r0, rb), :] = gated(s, one_hot(s, jnp.int32(0)), o[s * rb:(s + 1) * rb])

    @pl.when(slot == MOE_GROUP - 1)
    def _():
        for s in range(pair):
            y_ref[s * tt:(s + 1) * tt, :] += _dot_tn(pg_ref[s], og_ref[s])

    for s in range(pair):
        def overflow(blk, c, s=s):
            hit = one_hot(s, blk)
            p, xg = gather(s, hit)
            y_ref[s * tt:(s + 1) * tt, :] += _dot_tn(p, gated(s, hit, expert(xg)))
            return c

        n_rows = cnt_ref[(j * pair + s) * ne + e]
        lax.fori_loop(1, (n_rows + rb - 1) // rb, overflow, 0)


def _moe(h2, x1, gt, rt, cnt, ew, tt):
    t, d = h2.shape
    ne = gt.shape[0]
    nt = t // tt
    f = ew['wg'].shape[-1]
    pair = MOE_PAIR if nt % MOE_PAIR == 0 else 1
    grid_spec = pltpu.PrefetchScalarGridSpec(
        num_scalar_prefetch=1,
        grid=(nt // pair, ne),
        in_specs=[
            pl.BlockSpec((pair * tt, d), lambda j, e, c: (j, 0)),
            pl.BlockSpec(memory_space=pl.ANY),
            pl.BlockSpec((SUBLANES, pair * tt), lambda j, e, c: (e // SUBLANES, j)),
            pl.BlockSpec((SUBLANES, pair * tt), lambda j, e, c: (e // SUBLANES, j)),
            pl.BlockSpec((1, d, f), lambda j, e, c: (e, 0, 0)),
            pl.BlockSpec((1, d, f), lambda j, e, c: (e, 0, 0)),
            pl.BlockSpec((1, f, d), lambda j, e, c: (e, 0, 0)),
            pl.BlockSpec((1, 1, f), lambda j, e, c: (e, 0, 0)),
            pl.BlockSpec((1, 1, f), lambda j, e, c: (e, 0, 0)),
            pl.BlockSpec((1, 1, d), lambda j, e, c: (e, 0, 0)),
        ],
        out_specs=pl.BlockSpec((pair * tt, d), lambda j, e, c: (j, 0)),
        scratch_shapes=[pltpu.VMEM((pair, MOE_GROUP * MOE_ROWS, tt), BF16),
                        pltpu.VMEM((pair, MOE_GROUP * MOE_ROWS, d), BF16),
                        pltpu.SemaphoreType.DMA(())],
    )
    assert ne % MOE_GROUP == 0 and MOE_GROUP == SUBLANES
    return pl.pallas_call(
        functools.partial(_moe_kernel, tt=tt, pair=pair),
        grid_spec=grid_spec,
        out_shape=jax.ShapeDtypeStruct((t, d), F32),
        compiler_params=_cp(("arbitrary", "arbitrary")),
        name="moe",
    )(cnt, h2, x1, gt, rt, ew['wg'], ew['wu'], ew['wd'], ew['bg'], ew['bu'], ew['bd'])


def _pad_heads(wmat, n_heads, width):
    d = wmat.shape[0]
    w3 = wmat.reshape(d, n_heads, width)
    return jnp.pad(w3, ((0, 0), (0, 0), (0, LANES - width))).reshape(d, n_heads * LANES)


def _pad_lanes(v, width=LANES):
    v = v.reshape(1, -1)
    return jnp.pad(v, ((0, 0), (0, width - v.shape[1])))


def _rel_bucket(rel):
    half = REL_BUCKETS // 2
    max_exact = half // 2
    n = jnp.abs(rel)
    large = max_exact + (jnp.log(jnp.maximum(n, 1).astype(jnp.float32) / max_exact)
                         / math.log(REL_MAX_DIST / max_exact) * (half - max_exact)).astype(jnp.int32)
    large = jnp.minimum(large, half - 1)
    return jnp.where(rel > 0, half, 0) + jnp.where(n < max_exact, n, large)


def _bias_tiles(rel_bias):
    tk = KEY_TILE
    half = REL_BUCKETS // 2
    max_exact = half // 2
    n_sat = int(math.ceil(max_exact * (REL_MAX_DIST / max_exact) ** ((half - 1 - max_exact) / (half - max_exact)))) + 2
    nd = (n_sat + 2 * tk - 2) // tk + 1
    dd = jnp.arange(nd, dtype=I32)[:, None, None]
    c = jnp.arange(tk, dtype=I32)[None, :, None]
    r = jnp.arange(tk, dtype=I32)[None, None, :]
    bucket = _rel_bucket(c - r - dd * tk)
    onehot = (bucket[..., None] == jnp.arange(REL_BUCKETS, dtype=I32)).astype(F32)
    tiles = jnp.einsum('dcrb,bh->dhcr', onehot, rel_bias.astype(F32) * LOG2E,
                       precision=lax.Precision.HIGHEST)
    return tiles


def _prep_proj(norm1_g, w_in, q_norm_g, k_norm_g, idx_k_norm_g, idx_k_norm_b, d_model):
    ssm_w = d_model // 2
    attn_w = N_HEADS * HEAD_DIM
    kv = N_KV_HEADS * HEAD_DIM
    sizes = [ssm_w, attn_w, kv, kv, IDX_HEADS * IDX_DIM, IDX_DIM, IDX_HEADS, d_model, d_model]
    pts = np.cumsum(sizes)[:-1].tolist()
    wu, wq, wk, wv, wqi, wki, wwi, wga, wgb = jnp.split(w_in, pts, axis=1)
    bf = lambda a: a.astype(BF16)
    blk = np.kron(np.eye(N_KV_HEADS), np.ones((HEAD_DIM, HEAD_DIM))) / HEAD_DIM
    wwit = jnp.pad(wwi.T, ((0, 2 * SUBLANES - IDX_HEADS), (0, 0)))
    return dict(
        g1=norm1_g.reshape(1, -1).astype(F32),
        wu=bf(wu), wq=bf(_pad_heads(wq, N_HEADS, HEAD_DIM)), wk=bf(_pad_heads(wk, N_KV_HEADS, HEAD_DIM)),
        wv=bf(_pad_heads(wv, N_KV_HEADS, HEAD_DIM)), wqi=bf(_pad_heads(wqi, IDX_HEADS, IDX_DIM)),
        wki=bf(_pad_heads(wki, 1, IDX_DIM)), wwit=bf(wwit),
        wga=bf(wga), wgb=bf(wgb), wkc=bf(wk), wvc=bf(wv),
        gq=_pad_lanes(q_norm_g.astype(F32)), gk=_pad_lanes(k_norm_g.astype(F32)),
        gkc=jnp.tile(k_norm_g.astype(F32), N_KV_HEADS).reshape(1, -1),
        gi=_pad_lanes(idx_k_norm_g.astype(F32)), bi=_pad_lanes(idx_k_norm_b.astype(F32)),
        ones_h=jnp.full((LANES, LANES), 1.0 / HEAD_DIM, BF16),
        ones_c=jnp.asarray(blk, BF16),
    )


def _prep_s5(lre, lim, log_dt, b_re, b_im, c_re, c_im, dvec, wa, wb):
    g, p = lre.shape
    ch = b_re.shape[-1]
    lam = lax.complex(lre.astype(F32), lim.astype(F32))
    dt = jnp.exp(log_dt.astype(F32))[:, None]
    a_bar = jnp.exp(lam * dt)
    b_bar = ((a_bar - 1.0) / lam)[:, :, None] * lax.complex(b_re.astype(F32), b_im.astype(F32))
    gs = g // S5_DIAG
    eye = jnp.eye(gs, dtype=F32)

    def blocks_in(m):
        return jnp.einsum('jgpc,gh->jgchp', m.reshape(S5_DIAG, gs, p, ch), eye).reshape(S5_DIAG, gs * ch, gs * p)

    def blocks_out(m):
        return jnp.einsum('jgcp,gh->jgphc', m.reshape(S5_DIAG, gs, ch, p), eye).reshape(S5_DIAG, gs * p, gs * ch)

    return dict(
        b_re=blocks_in(jnp.real(b_bar)).astype(BF16), b_im=blocks_in(jnp.imag(b_bar)).astype(BF16),
        c_re=blocks_out(c_re.astype(F32)).astype(BF16), c_im=blocks_out(-c_im.astype(F32)).astype(BF16),
        a_re=jnp.real(a_bar).reshape(1, g * p), a_im=jnp.imag(a_bar).reshape(1, g * p),
        d=dvec.reshape(1, -1).astype(F32), wa=wa.astype(BF16), wb=wb.astype(BF16),
    )


def _prep_merge(w_attn_up, w_out, norm2_g, w_router, b_router):
    d = w_attn_up.shape[1]
    wup = jnp.pad(w_attn_up.reshape(N_HEADS, HEAD_DIM, d), ((0, 0), (0, LANES - HEAD_DIM), (0, 0)))
    wr_t = w_router.astype(F32).T
    wr_hi = wr_t.astype(BF16)
    wr_lo = (wr_t - wr_hi.astype(F32)).astype(BF16)
    return dict(
        wup=wup.reshape(N_HEADS * LANES, d).astype(BF16), wout=w_out.astype(BF16),
        g2=norm2_g.reshape(1, -1).astype(F32), wr_hi=wr_hi, wr_lo=wr_lo,
        br=jnp.broadcast_to(b_router.astype(F32)[:, None], (b_router.shape[0], LANES)),
    )


def _prep_moe(wg, bg, wu, bu, wd, bd):
    return dict(wg=wg.astype(BF16), wu=wu.astype(BF16), wd=wd.astype(BF16),
                bg=bg.astype(F32)[:, None, :], bu=bu.astype(F32)[:, None, :], bd=bd.astype(F32)[:, None, :])


def _pick_tile(n, pref):
    t = min(n, pref)
    while n % t:
        t //= 2
    return t


def _pad_axis(a, axis, size):
    pad = [(0, 0)] * a.ndim
    pad[axis] = (0, size - a.shape[axis])
    return jnp.pad(a, pad)


def _trunk_layer(x, past_k, past_v, past_ik, h0_re, h0_im, pw, sw, mw, ew, bias_tiles):
    bsz, seq, d = x.shape
    t = bsz * seq
    tm = _pick_tile(seq, 512)
    u_tb, q, kp, vp, qi, kip, wt, sga, sgb, kc, vc, kic = _proj(x, pw, bsz, seq, tm)

    half = sw['a_re'].shape[1]
    if h0_re is None:
        h0 = jnp.zeros((bsz, 2 * half), F32)
    else:
        h0 = jnp.concatenate([h0_re.reshape(bsz, half), h0_im.reshape(bsz, half)], axis=1).astype(F32)
    tc = _pick_tile(seq, max(1, 512 // bsz))
    ya, hout = _s5(u_tb, h0, sw, bsz, seq, tc)
    groups = half // SSM_STATE
    s_re = hout[:, :half].reshape(bsz, groups, SSM_STATE)
    s_im = hout[:, half:].reshape(bsz, groups, SSM_STATE)

    past = 0 if past_k is None else past_k.shape[1]
    n_keys = past + seq
    lk = -(-n_keys // KEY_BLOCK) * KEY_BLOCK
    kip3 = kip.reshape(bsz, seq, LANES)
    if past:
        lane = jnp.arange(LANES)
        pk = jnp.pad(past_k.astype(F32), ((0, 0), (0, 0), (0, 0), (0, LANES - HEAD_DIM)))
        pk = jnp.where(lane == HEAD_DIM, 1.0, pk).astype(BF16)
        pv = jnp.pad(past_v.astype(F32), ((0, 0), (0, 0), (0, 0), (0, LANES - HEAD_DIM)))
        pv = jnp.where(lane == HEAD_DIM, 1.0, pv).astype(BF16)
        pik = jnp.pad(past_ik.astype(F32), ((0, 0), (0, 0), (0, LANES - IDX_DIM))).astype(BF16)
        k_all = jnp.concatenate([pk.transpose(0, 2, 1, 3), kp], axis=2)
        v_all = jnp.concatenate([pv.transpose(0, 2, 1, 3), vp], axis=2)
        ki_all = jnp.concatenate([pik, kip3], axis=1)
    else:
        k_all, v_all, ki_all = kp, vp, kip3
    k_all = _pad_axis(k_all, 2, lk)
    v_all = _pad_axis(v_all, 2, lk)
    ki_all = _pad_axis(ki_all, 1, lk)
    vt_all = v_all.reshape(bsz, N_KV_HEADS, lk // KEY_BLOCK, KEY_BLOCK, LANES).transpose(0, 1, 2, 4, 3)
    tq = KEY_TILE
    seq_q = -(-seq // tq) * tq
    q_p, qi_p, wt_p = _pad_axis(q, 2, seq_q), _pad_axis(qi, 2, seq_q), _pad_axis(wt, 2, seq_q)
    kf = k_all[..., :HEAD_DIM].astype(F32)
    kmax = jnp.sqrt(jnp.max(jnp.sum(kf * kf, axis=-1), axis=-1)).reshape(-1)
    bfar = bias_tiles[-1, :, 0, 0]
    bmax = jnp.max(jnp.abs(bias_tiles), axis=(0, 2, 3))
    stats = jnp.concatenate([kmax, bmax, bfar]).astype(F32)
    bias_tiles = bias_tiles - bfar[None, :, None, None]
    attn = _dsa(stats, q_p, qi_p, wt_p, k_all, vt_all, ki_all, bias_tiles, bsz, seq_q, past, n_keys, tq)
    attn = attn[:, :seq]

    moe_tile = _pick_tile(t, MOE_TILE)
    tm2 = _pick_tile(moe_tile, 512)
    x1, h2, gt, rt, cnt = _merge(x.reshape(t, d), ya, attn.reshape(t, attn.shape[-1]), sga, sgb, mw, tm2, moe_tile)
    cnt_i = cnt[:, :, 0].astype(I32).reshape(-1)
    y = _moe(h2, x1, gt, rt, cnt_i, ew, moe_tile)

    k_new = kc.reshape(bsz, seq, N_KV_HEADS, HEAD_DIM)
    v_new = vc.reshape(bsz, seq, N_KV_HEADS, HEAD_DIM)
    ik_new = kic.reshape(bsz, seq, IDX_DIM)
    return y.reshape(bsz, seq, d), k_new, v_new, ik_new, s_re, s_im


def kernel(x_prompt, x_sample, cache_k, cache_v, cache_idx_k, state_ssm_re, state_ssm_im, rel_bias, norm1_g, w_in, ssm_lambda_re, ssm_lambda_im, ssm_log_dt, ssm_b_re, ssm_b_im, ssm_c_re, ssm_c_im, ssm_d, ssm_w_glu_a, ssm_w_glu_b, q_norm_g, k_norm_g, idx_k_norm_g, idx_k_norm_b, w_attn_up, w_out, norm2_g, moe_w_router, moe_b_router, moe_w_gate, moe_b_gate, moe_w_up, moe_b_up, moe_w_down, moe_b_down):
    depth = w_in.shape[0]
    d_model = x_prompt.shape[-1]
    bias_tiles = _bias_tiles(rel_bias)
    xp, xs = x_prompt, x_sample
    st_p, st_s = [], []
    for l in range(depth):
        pw = _prep_proj(norm1_g[l], w_in[l], q_norm_g[l], k_norm_g[l], idx_k_norm_g[l], idx_k_norm_b[l], d_model)
        sw = _prep_s5(ssm_lambda_re[l], ssm_lambda_im[l], ssm_log_dt[l], ssm_b_re[l], ssm_b_im[l], ssm_c_re[l],
                      ssm_c_im[l], ssm_d[l], ssm_w_glu_a[l], ssm_w_glu_b[l])
        mw = _prep_merge(w_attn_up[l], w_out[l], norm2_g[l], moe_w_router[l], moe_b_router[l])
        ew = _prep_moe(moe_w_gate[l], moe_b_gate[l], moe_w_up[l], moe_b_up[l], moe_w_down[l], moe_b_down[l])
        xp, *sp = _trunk_layer(xp, None, None, None, None, None, pw, sw, mw, ew, bias_tiles)
        xs, *ss = _trunk_layer(xs, cache_k[l], cache_v[l], cache_idx_k[l], state_ssm_re[l], state_ssm_im[l],
                               pw, sw, mw, ew, bias_tiles)
        st_p.append(sp)
        st_s.append(ss)
    outs_p = [jnp.stack([s[i] for s in st_p]) for i in range(5)]
    outs_s = [jnp.stack([s[i] for s in st_s]) for i in range(5)]
    return (xp, xs, *outs_p, *outs_s)
```

```python
import functools
import math

import numpy as np
import jax
import jax.numpy as jnp
from jax import lax
from jax.experimental import pallas as pl
from jax.experimental.pallas import tpu as pltpu

F32 = jnp.float32
BF16 = jnp.bfloat16
I32 = jnp.int32

LANES = 128
SUBLANES = 8
VMEM_LIMIT = 56 * 1024 * 1024

CHUNK = 64
SSM_GROUP_CH = 16
SSM_STATE = 64
N_HEADS = 8
HEAD_DIM = 64
N_KV_HEADS = 2
KV_REP = N_HEADS // N_KV_HEADS
IDX_HEADS = 8
IDX_DIM = 64
TOPK_MAX = 256
REL_BUCKETS = 32
REL_MAX_DIST = 1024
N_EXPERTS = 32
TOP_K = 4
SWIGLU_LIMIT = 7.0
SWIGLU_ALPHA = 1.702
EPS = 1e-6

KEY_TILE = 128
KEY_BLOCK = 256
SCORE_CHUNK = 4
ATTEND_CHUNK = 4
LOG2E = math.log2(math.e)
NEG_BIG = -1e30
SHIFT_LIMIT = 30.0
S5_ROWS = 1024
S5_DIAG = 2
SEARCH_FIRST = 8
SEARCH_GROUP = 4
MOE_TILE = 1024
MOE_ROWS = 144
MOE_GROUP = 8
MOE_PAIR = 2


def _cp(sem):
    return pltpu.CompilerParams(dimension_semantics=sem, vmem_limit_bytes=VMEM_LIMIT)


def _dot(a, b):
    return jnp.dot(a, b, preferred_element_type=F32)


def _dot_nt(a, b):
    return lax.dot_general(a, b, (((1,), (1,)), ((), ())), preferred_element_type=F32)


def _dot_tn(a, b):
    return lax.dot_general(a, b, (((0,), (0,)), ((), ())), preferred_element_type=F32)


def _split(a):
    hi = a.astype(BF16)
    lo = (a - hi.astype(F32)).astype(BF16)
    return hi, lo


def _dot_split(a, g):
    hi, lo = _split(a)
    return _dot(hi, g) + _dot(lo, g)


def _proj_kernel(x_ref, g1_ref, wu_ref, wq_ref, wk_ref, wv_ref, wqi_ref, wki_ref, wwit_ref, wga_ref, wgb_ref,
                 wkc_ref, wvc_ref, gq_ref, gk_ref, gkc_ref, gi_ref, bi_ref, ones_h_ref, ones_c_ref,
                 u_ref, q_ref, kp_ref, vp_ref, qi_ref, kip_ref, wt_ref, sga_ref, sgb_ref,
                 kc_ref, vc_ref, kic_ref):
    x = x_ref[...]
    ms = jnp.mean(x * x, axis=-1, keepdims=True)
    hn = (x * lax.rsqrt(ms + EPS) * g1_ref[...]).astype(BF16)
    ones_h = ones_h_ref[...]
    lane = lax.broadcasted_iota(I32, (x.shape[0], LANES), 1)

    u_ref[...] = _dot(hn, wu_ref[...]).astype(BF16)

    q = _dot(hn, wq_ref[...])
    scale = HEAD_DIM ** -0.5 * LOG2E
    for h in range(N_HEADS):
        qh = q[:, h * LANES:(h + 1) * LANES]
        msq = _dot_split(qh * qh, ones_h)
        q_ref[h] = (qh * lax.rsqrt(msq + EPS) * (gq_ref[...] * scale)).astype(BF16)

    k = _dot(hn, wk_ref[...])
    for g in range(N_KV_HEADS):
        kg = k[:, g * LANES:(g + 1) * LANES]
        msk = _dot_split(kg * kg, ones_h)
        kn = kg * lax.rsqrt(msk + EPS) * gk_ref[...]
        kp_ref[g] = jnp.where(lane == HEAD_DIM, 1.0, kn).astype(BF16)

    v = _dot(hn, wv_ref[...])
    for g in range(N_KV_HEADS):
        vg = v[:, g * LANES:(g + 1) * LANES]
        vp_ref[g] = jnp.where(lane == HEAD_DIM, 1.0, vg).astype(BF16)

    qi = _dot(hn, wqi_ref[...])
    for h in range(IDX_HEADS):
        qi_ref[h] = qi[:, h * LANES:(h + 1) * LANES].astype(BF16)

    ki = _dot(hn, wki_ref[...])
    mu = _dot_split(ki, ones_h)
    xc = jnp.where(lane < IDX_DIM, ki - mu, 0.0)
    var = _dot_split(xc * xc, ones_h)
    kin = xc * lax.rsqrt(var + EPS) * gi_ref[...] + bi_ref[...]
    kip_ref[...] = kin.astype(BF16)
    kic_ref[...] = kin[:, :IDX_DIM]

    wt = _dot_nt(wwit_ref[...], hn)
    wt_ref[...] = wt[0:IDX_HEADS, :] * (IDX_HEADS ** -0.5 * IDX_DIM ** -0.5)

    sga_ref[...] = jax.nn.sigmoid(_dot(hn, wga_ref[...])).astype(BF16)
    sgb_ref[...] = jax.nn.sigmoid(_dot(hn, wgb_ref[...])).astype(BF16)

    kc = _dot(hn, wkc_ref[...])
    mskc = _dot_split(kc * kc, ones_c_ref[...])
    kc_ref[...] = kc * lax.rsqrt(mskc + EPS) * gkc_ref[...]
    vc_ref[...] = _dot(hn, wvc_ref[...])


def _proj(x, pw, bsz, seq, tm):
    d = x.shape[-1]
    nt = seq // tm
    t = bsz * seq
    x2 = x.reshape(t, d)

    def tok(b, i):
        return (b * nt + i, 0)

    def cst(b, i):
        return (0, 0)

    def wspec(a):
        return pl.BlockSpec(a.shape, cst)

    weights = [pw['g1'], pw['wu'], pw['wq'], pw['wk'], pw['wv'], pw['wqi'], pw['wki'], pw['wwit'], pw['wga'],
               pw['wgb'], pw['wkc'], pw['wvc'], pw['gq'], pw['gk'], pw['gkc'], pw['gi'], pw['bi'],
               pw['ones_h'], pw['ones_c']]
    ssm_w = pw['wu'].shape[1]
    out_shape = (
        jax.ShapeDtypeStruct((t, ssm_w), BF16),
        jax.ShapeDtypeStruct((bsz, N_HEADS, seq, LANES), BF16),
        jax.ShapeDtypeStruct((bsz, N_KV_HEADS, seq, LANES), BF16),
        jax.ShapeDtypeStruct((bsz, N_KV_HEADS, seq, LANES), BF16),
        jax.ShapeDtypeStruct((bsz, IDX_HEADS, seq, LANES), BF16),
        jax.ShapeDtypeStruct((t, LANES), BF16),
        jax.ShapeDtypeStruct((bsz, IDX_HEADS, seq), F32),
        jax.ShapeDtypeStruct((t, d), BF16),
        jax.ShapeDtypeStruct((t, d), BF16),
        jax.ShapeDtypeStruct((t, N_KV_HEADS * HEAD_DIM), F32),
        jax.ShapeDtypeStruct((t, N_KV_HEADS * HEAD_DIM), F32),
        jax.ShapeDtypeStruct((t, IDX_DIM), F32),
    )

    def hm(nh):
        return pl.BlockSpec((None, nh, tm, LANES), lambda b, i: (b, 0, i, 0))

    out_specs = (
        pl.BlockSpec((tm, ssm_w), tok),
        hm(N_HEADS), hm(N_KV_HEADS), hm(N_KV_HEADS), hm(IDX_HEADS),
        pl.BlockSpec((tm, LANES), tok),
        pl.BlockSpec((None, IDX_HEADS, tm), lambda b, i: (b, 0, i)),
        pl.BlockSpec((tm, d), tok), pl.BlockSpec((tm, d), tok),
        pl.BlockSpec((tm, N_KV_HEADS * HEAD_DIM), tok), pl.BlockSpec((tm, N_KV_HEADS * HEAD_DIM), tok),
        pl.BlockSpec((tm, IDX_DIM), tok),
    )
    return pl.pallas_call(
        _proj_kernel,
        grid=(bsz, nt),
        in_specs=[pl.BlockSpec((tm, d), tok)] + [wspec(a) for a in weights],
        out_specs=out_specs,
        out_shape=out_shape,
        compiler_params=_cp(("arbitrary", "arbitrary")),
        name="proj",
    )(x2, *weights)


def _gelu_tanh(x):
    return 0.5 * x * (1.0 + jnp.tanh(math.sqrt(2.0 / math.pi) * (x + 0.044715 * (x * x * x))))


def _s5_kernel(u_ref, h0_ref, bre_ref, bim_ref, are_ref, aim_ref, cre_ref, cim_ref, dvec_ref, wa_ref, wb_ref,
               ya_ref, hout_ref, state_ref, bu_ref, yf_ref, ug_ref, *, bsz, tc, strip):
    s = pl.program_id(0)
    half = are_ref.shape[1]

    @pl.when(s == 0)
    def _():
        state_ref[...] = h0_ref[...]

    ssm_w = dvec_ref.shape[1]
    for b in range(bsz):
        for c in range(ssm_w // LANES):
            ug_ref[c, pl.ds(b, tc, stride=bsz), :] = u_ref[b, :, c * LANES:(c + 1) * LANES].astype(F32)
    u = jnp.concatenate([ug_ref[c] for c in range(ssm_w // LANES)], axis=1).astype(BF16)
    cw = u.shape[1] // S5_DIAG
    sw = half // S5_DIAG
    for j in range(S5_DIAG):
        uj = u[:, j * cw:(j + 1) * cw]
        bu_ref[:, j * sw:(j + 1) * sw] = _dot(uj, bre_ref[j])
        bu_ref[:, half + j * sw:half + (j + 1) * sw] = _dot(uj, bim_ref[j])

    for c0 in range(0, half, strip):
        ar = jnp.broadcast_to(are_ref[:, c0:c0 + strip], (bsz, strip))
        ai = jnp.broadcast_to(aim_ref[:, c0:c0 + strip], (bsz, strip))
        hr0 = state_ref[:, c0:c0 + strip]
        hi0 = state_ref[:, half + c0:half + c0 + strip]

        def step(t, carry):
            hr, hi = carry
            r0 = pl.multiple_of(t * bsz, bsz)
            br = bu_ref[pl.ds(r0, bsz), c0:c0 + strip]
            bi = bu_ref[pl.ds(r0, bsz), half + c0:half + c0 + strip]
            nr = ar * hr - ai * hi + br
            ni = ar * hi + ai * hr + bi
            bu_ref[pl.ds(r0, bsz), c0:c0 + strip] = nr
            bu_ref[pl.ds(r0, bsz), half + c0:half + c0 + strip] = ni
            return nr, ni

        hr, hi = lax.fori_loop(0, tc, step, (hr0, hi0))
        state_ref[:, c0:c0 + strip] = hr
        state_ref[:, half + c0:half + c0 + strip] = hi

    ys = []
    for j in range(S5_DIAG):
        s_re = bu_ref[:, j * sw:(j + 1) * sw].astype(BF16)
        s_im = bu_ref[:, half + j * sw:half + (j + 1) * sw].astype(BF16)
        ys.append(_dot(s_re, cre_ref[j]) + _dot(s_im, cim_ref[j]))
    y = jnp.concatenate(ys, axis=1) + dvec_ref[...] * u.astype(F32)
    g = _gelu_tanh(y).astype(BF16)
    ya = _dot(g, wa_ref[...]) * jax.nn.sigmoid(_dot(g, wb_ref[...]))
    n_chunk = ya.shape[1] // LANES
    for c in range(n_chunk):
        yf_ref[c] = ya[:, c * LANES:(c + 1) * LANES]
    for b in range(bsz):
        ya_ref[b] = jnp.concatenate([yf_ref[c, pl.ds(b, tc, stride=bsz), :] for c in range(n_chunk)],
                                    axis=1).astype(BF16)

    @pl.when(s == pl.num_programs(0) - 1)
    def _():
        hout_ref[...] = state_ref[...]


def _s5(u, h0, sw, bsz, seq, tc):
    rows = tc * bsz
    ssm_w = sw['d'].shape[1]
    half = sw['a_re'].shape[1]
    two_half = 2 * half
    d = sw['wa'].shape[1]
    u3 = u.reshape(bsz, seq, ssm_w)
    strip = min(512, half)

    consts = [h0, sw['b_re'], sw['b_im'], sw['a_re'], sw['a_im'], sw['c_re'], sw['c_im'], sw['d'], sw['wa'], sw['wb']]

    def cst(s):
        return (0, 0)

    def cspec(a):
        return pl.BlockSpec(a.shape, lambda s: (0,) * a.ndim)

    ya, hout = pl.pallas_call(
        functools.partial(_s5_kernel, bsz=bsz, tc=tc, strip=strip),
        grid=(seq // tc,),
        in_specs=[pl.BlockSpec((bsz, tc, ssm_w), lambda s: (0, s, 0))] + [cspec(a) for a in consts],
        out_specs=(pl.BlockSpec((bsz, tc, d), lambda s: (0, s, 0)), pl.BlockSpec((bsz, two_half), cst)),
        out_shape=(jax.ShapeDtypeStruct((bsz, seq, d), BF16), jax.ShapeDtypeStruct((bsz, two_half), F32)),
        scratch_shapes=[pltpu.VMEM((bsz, two_half), F32), pltpu.VMEM((rows, two_half), F32),
                        pltpu.VMEM((d // LANES, rows, LANES), F32), pltpu.VMEM((ssm_w // LANES, rows, LANES), F32)],
        compiler_params=_cp(("arbitrary",)),
        name="s5",
    )(u3, *consts)
    return ya.reshape(bsz * seq, d), hout


def _f2key(x):
    b = lax.bitcast_convert_type(x, I32)
    return b ^ ((b >> 31) & 0x7FFFFFFF)


def _key2f(k):
    return lax.bitcast_convert_type(k ^ ((k >> 31) & 0x7FFFFFFF), F32)


def _dsa_kernel(st_ref, q_ref, qi_ref, wt_ref, k_ref, vt_ref, ki_ref, bias_ref, o_ref,
                s_ref, lo_ref, hi_ref, clo_ref, glo_ref, ghi_ref, side_ref, q2_ref, mrow_ref, acc_ref,
                *, bsz, tq, past, n_keys, topk, nkt, nd, idx_bits, near_max):
    b_id = pl.program_id(0)
    i = pl.program_id(1)
    kb = KEY_BLOCK
    sl = SUBLANES
    q0 = past + i * tq
    last_chunk = (q0 + tq - 1) // CHUNK
    n_kt = jnp.minimum(nkt, ((last_chunk + 1) * CHUNK + kb - 1) // kb)
    d0 = q0 // KEY_TILE

    krow = lax.broadcasted_iota(I32, (kb, tq), 0)
    q_chunk = (q0 + lax.broadcasted_iota(I32, (kb, tq), 1)) // CHUNK
    qc8 = (q0 + lax.broadcasted_iota(I32, (sl, tq), 1)) // CHUNK
    n_adm = jnp.minimum((qc8 + 1) * CHUNK, n_keys)
    n_admf = n_adm.astype(F32)
    is_pad = q0 + lax.broadcasted_iota(I32, (sl, tq), 1) >= n_keys
    needf = jnp.where(is_pad, n_adm, jnp.minimum(topk, n_adm)).astype(F32)

    def bcast(x):
        return jnp.broadcast_to(x[0:1, :], (kb, tq))

    def rep(x):
        return jnp.broadcast_to(x, (sl, tq))

    qi = qi_ref[...].reshape(IDX_HEADS * tq, LANES)

    def score_blocks(kt, nb, masked):
        k0 = pl.multiple_of(kt * kb, kb)
        s = _dot_nt(ki_ref[pl.ds(k0, nb * kb), :], qi)
        for j in range(nb):
            sc = jnp.zeros((kb, tq), F32)
            for h in range(IDX_HEADS):
                sc = sc + wt_ref[h:h + 1, :] * jnp.maximum(s[j * kb:(j + 1) * kb, h * tq:(h + 1) * tq], 0.0)
            if masked:
                kpos = k0 + j * kb + krow
                adm = ((kpos // CHUNK) <= q_chunk) & (kpos < n_keys)
                sc = jnp.where(adm, sc, -jnp.inf)
            s_ref[kt + j] = sc

    n_open = n_kt - 1

    def score_chunk(j, c):
        score_blocks(SCORE_CHUNK * j, SCORE_CHUNK, False)
        return c

    lax.fori_loop(0, n_open // SCORE_CHUNK, score_chunk, 0)

    def score_single(j, c):
        score_blocks(j, 1, False)
        return c

    lax.fori_loop(n_open - n_open % SCORE_CHUNK, n_open, score_single, 0)
    score_blocks(n_kt - 1, 1, True)
    s_ref[n_kt] = jnp.full((kb, tq), -jnp.inf, F32)
    n_pair = (n_kt + 1) // 2

    part = 4 * sl

    def fold(x, op):
        x = x.reshape(kb // part, part, tq)
        acc = x[0]
        for j in range(1, kb // part):
            acc = op(acc, x[j])
        return acc

    def count(pred):
        def one(kt):
            return fold(jnp.where(pred(s_ref[kt], kt), 1.0, 0.0), jnp.add)
        c = lax.fori_loop(0, n_pair, lambda j, c: c + (one(2 * j) + one(2 * j + 1)), jnp.zeros((part, tq), F32))
        return rep(jnp.sum(c, axis=0, keepdims=True))

    def minmax(j, c):
        mx, mn = c
        for kt in (2 * j, 2 * j + 1):
            s = s_ref[kt]
            mx = jnp.maximum(mx, fold(s, jnp.maximum))
            mn = jnp.minimum(mn, fold(jnp.where(s == -jnp.inf, jnp.inf, s), jnp.minimum))
        return mx, mn

    mx, mn = lax.fori_loop(0, n_pair, minmax,
                           (jnp.full((part, tq), -jnp.inf, F32), jnp.full((part, tq), jnp.inf, F32)))
    lo_ref[...] = rep(jnp.min(mn, axis=0, keepdims=True))
    hi_ref[...] = _key2f(_f2key(rep(jnp.max(mx, axis=0, keepdims=True))) + 1)
    def odds(cnt):
        c = jnp.clip(cnt, 0.5, n_admf - 0.5)
        return jnp.log((n_admf - c) / c)

    target = odds(needf - 0.5)
    clo_ref[...] = n_admf
    glo_ref[...] = target - odds(n_admf)
    ghi_ref[...] = target - odds(jnp.zeros((sl, tq), F32))
    side_ref[...] = jnp.zeros((sl, tq), F32)

    def searching(lo, hi, clo):
        return (_f2key(hi) > _f2key(lo) + 1) & (clo > needf)

    def refine(it, c):
        lo, hi, clo = lo_ref[...], hi_ref[...], clo_ref[...]
        glo, ghi, side = glo_ref[...], ghi_ref[...], side_ref[...]
        k_t = _f2key(lo + (hi - lo) * (glo / (glo - ghi)))
        t = _key2f(jnp.minimum(jnp.maximum(k_t, _f2key(lo) + 1), _f2key(hi) - 1))
        tb = bcast(t)
        cnt = count(lambda s, kt: s >= tb)
        g = target - odds(cnt)
        open_ = searching(lo, hi, clo)
        up = open_ & (cnt >= needf)
        dn = open_ & (cnt < needf)
        lo_ref[...] = jnp.where(up, t, lo)
        clo_ref[...] = jnp.where(up, cnt, clo)
        hi_ref[...] = jnp.where(dn, t, hi)
        glo_ref[...] = jnp.where(up, g, jnp.where(dn & (side < 0.0), glo * 0.5, glo))
        ghi_ref[...] = jnp.where(dn, g, jnp.where(up & (side > 0.0), ghi * 0.5, ghi))
        side_ref[...] = jnp.where(up, 1.0, jnp.where(dn, -1.0, side))
        return c

    def snap():
        lo, hi, clo = lo_ref[...], hi_ref[...], clo_ref[...]
        lo_b, hi_b = bcast(lo), bcast(hi)

        def body(j, c):
            a, b = c
            for kt in (2 * j, 2 * j + 1):
                s = s_ref[kt]
                a = jnp.minimum(a, fold(jnp.where(s >= lo_b, s, jnp.inf), jnp.minimum))
                b = jnp.maximum(b, fold(jnp.where(s < hi_b, s, -jnp.inf), jnp.maximum))
            return a, b

        a, b = lax.fori_loop(0, n_pair, body,
                             (jnp.full((part, tq), jnp.inf, F32), jnp.full((part, tq), -jnp.inf, F32)))
        open_ = searching(lo, hi, clo)
        lo_ref[...] = jnp.where(open_, rep(jnp.min(a, axis=0, keepdims=True)), lo)
        hi_ref[...] = jnp.where(open_, _key2f(_f2key(rep(jnp.max(b, axis=0, keepdims=True))) + 1), hi)

    def n_searching():
        return jnp.max(jnp.where(searching(lo_ref[...], hi_ref[...], clo_ref[...]), 1.0, 0.0))

    def group(c):
        grp, _ = c
        lax.fori_loop(0, jnp.where(grp == 0, SEARCH_FIRST, SEARCH_GROUP), refine, 0)
        snap()
        return grp + 1, n_searching()

    lax.while_loop(lambda c: c[1] > 0.0, group, (jnp.int32(0), n_searching()))
    thr = lo_ref[...]
    thr_b = bcast(thr)

    n_tied = jnp.max(jnp.where(clo_ref[...] > needf, 1.0, 0.0))

    @pl.when(n_tied > 0.0)
    def _():
        rem_b = bcast(needf - count(lambda s, kt: s > thr_b))
        tri = jnp.where(lax.broadcasted_iota(I32, (kb, kb), 0) >= lax.broadcasted_iota(I32, (kb, kb), 1),
                        1.0, 0.0).astype(BF16)

        def drop(kt, seen):
            s = s_ref[kt]
            tie = s == thr_b
            rank = _dot(tri, jnp.where(tie, 1.0, 0.0).astype(BF16)) + bcast(seen)
            s_ref[kt] = jnp.where(tie & (rank > rem_b), -jnp.inf, s)
            return rep(rank[kb - 1:kb, :])

        lax.fori_loop(0, n_kt, drop, jnp.zeros((sl, tq), F32))

    rows_g = KV_REP * tq
    qf = q_ref[...].reshape(N_HEADS * tq, LANES).astype(F32)
    qn = jnp.sqrt(jnp.sum(qf * qf, axis=1, keepdims=True))
    lane = lax.broadcasted_iota(I32, (tq, LANES), 1)
    worst = jnp.float32(0.0)
    for h in range(N_HEADS):
        kmax = st_ref[b_id * N_KV_HEADS + h // KV_REP]
        bmax = st_ref[bsz * N_KV_HEADS + h]
        bfar = st_ref[bsz * N_KV_HEADS + N_HEADS + h]
        bound = qn[h * tq:(h + 1) * tq, :] * (kmax * 1.01) + (bmax + 0.1)
        worst = jnp.maximum(worst, jnp.max(bound))
        q2_ref[h * tq:(h + 1) * tq, :] = jnp.where(lane == HEAD_DIM, bfar - bound,
                                                   qf[h * tq:(h + 1) * tq, :]).astype(BF16)
    n_far = jnp.clip((d0 - nd) // 2 + 1, 0, n_kt)

    def select_masks(kt0, nb):
        return [jnp.where(s_ref[kt0 + j] >= thr_b, 0.0, NEG_BIG) for j in range(nb)]

    def logits(kt0, nb, g, near, exact, masks):
        k0 = pl.multiple_of(kt0 * kb, kb)
        s = _dot_nt(k_ref[g, pl.ds(k0, nb * kb), :], q2_ref[g * rows_g:(g + 1) * rows_g, :])
        out = []
        for j in range(nb):
            kt = kt0 + j
            maskadd = masks[j]
            if near:
                da = jnp.clip(d0 - 2 * kt, 0, nd - 1)
                db = jnp.clip(d0 - 2 * kt - 1, 0, nd - 1)
            parts = []
            for r in range(KV_REP):
                h = g * KV_REP + r
                add = maskadd - mrow_ref[0:1, h * tq:(h + 1) * tq] if exact else maskadd
                if near:
                    add = jnp.concatenate([bias_ref[da, h], bias_ref[db, h]], axis=0) + add
                parts.append(s[j * kb:(j + 1) * kb, r * tq:(r + 1) * tq] + add)
            out.append(jnp.concatenate(parts, axis=1))
        return out

    def over_blocks(fn):
        def far_chunk(j, c):
            fn(ATTEND_CHUNK * j, ATTEND_CHUNK, False)
            return c
        lax.fori_loop(0, n_far // ATTEND_CHUNK, far_chunk, 0)

        def far_single(kt, c):
            fn(kt, 1, False)
            return c
        lax.fori_loop(n_far - n_far % ATTEND_CHUNK, n_far, far_single, 0)

        n_near = n_kt - n_far
        for nb in range(1, near_max + 1):
            @pl.when(n_near == nb)
            def _(nb=nb):
                fn(n_far, nb, True)

    def attend(exact):
        acc_ref[...] = jnp.zeros(acc_ref.shape, F32)

        def blocks(kt0, nb, near):
            masks = select_masks(kt0, nb)
            for g in range(N_KV_HEADS):
                p = jnp.concatenate([jnp.exp2(lg).astype(BF16) for lg in logits(kt0, nb, g, near, exact, masks)],
                                    axis=0)
                vt = jnp.concatenate([vt_ref[g, kt0 + j] for j in range(nb)], axis=1)
                acc_ref[g] += _dot(vt, p)
        over_blocks(blocks)

    @pl.when(worst <= SHIFT_LIMIT)
    def _():
        attend(False)

    @pl.when(worst > SHIFT_LIMIT)
    def _():
        mrow_ref[...] = jnp.full(mrow_ref.shape, NEG_BIG, F32)

        def blocks(kt0, nb, near):
            masks = select_masks(kt0, nb)
            for g in range(N_KV_HEADS):
                for lg in logits(kt0, nb, g, near, False, masks):
                    mx = jnp.max(lg, axis=0, keepdims=True)
                    cur = mrow_ref[:, g * rows_g:(g + 1) * rows_g]
                    mrow_ref[:, g * rows_g:(g + 1) * rows_g] = jnp.maximum(cur, jnp.broadcast_to(mx, (sl, rows_g)))
        over_blocks(blocks)
        attend(True)

    for g in range(N_KV_HEADS):
        acc = acc_ref[g]
        og = acc / acc[HEAD_DIM:HEAD_DIM + 1, :]
        for r in range(KV_REP):
            h = g * KV_REP + r
            o_ref[:, h * LANES:(h + 1) * LANES] = og[:, r * tq:(r + 1) * tq].T.astype(BF16)


def _dsa(stats, q, qi, wt, k_all, vt_all, ki_all, bias_tiles, bsz, seq, past, n_keys, tq):
    lk = k_all.shape[2]
    nkt = lk // KEY_BLOCK
    topk = min(TOPK_MAX, n_keys // 4)
    nd = bias_tiles.shape[0]
    nq = seq // tq
    assert past % KEY_TILE == 0 and tq == KEY_TILE
    idx_bits = int(math.ceil(math.log2(lk))) + 1
    near_max = 0
    for i in range(nq):
        q0 = past + i * tq
        n_kt = min(nkt, (((q0 + tq - 1) // CHUNK + 1) * CHUNK + KEY_BLOCK - 1) // KEY_BLOCK)
        n_far = min(max((q0 // KEY_TILE - nd) // 2 + 1, 0), n_kt)
        near_max = max(near_max, n_kt - n_far)
    kern = functools.partial(_dsa_kernel, bsz=bsz, tq=tq, past=past, n_keys=n_keys, topk=topk, nkt=nkt, nd=nd,
                             idx_bits=idx_bits, near_max=near_max)
    row_state = pltpu.VMEM((SUBLANES, tq), F32)
    grid_spec = pltpu.PrefetchScalarGridSpec(
        num_scalar_prefetch=1,
        grid=(bsz, nq),
        in_specs=[
            pl.BlockSpec((None, N_HEADS, tq, LANES), lambda b, i, s: (b, 0, i, 0)),
            pl.BlockSpec((None, IDX_HEADS, tq, LANES), lambda b, i, s: (b, 0, i, 0)),
            pl.BlockSpec((None, IDX_HEADS, tq), lambda b, i, s: (b, 0, i)),
            pl.BlockSpec((None, N_KV_HEADS, lk, LANES), lambda b, i, s: (b, 0, 0, 0)),
            pl.BlockSpec((None, N_KV_HEADS, nkt, LANES, KEY_BLOCK), lambda b, i, s: (b, 0, 0, 0, 0)),
            pl.BlockSpec((None, lk, LANES), lambda b, i, s: (b, 0, 0)),
            pl.BlockSpec(bias_tiles.shape, lambda b, i, s: (0, 0, 0, 0)),
        ],
        out_specs=pl.BlockSpec((None, tq, N_HEADS * LANES), lambda b, i, s: (b, i, 0)),
        scratch_shapes=[
            pltpu.VMEM((nkt + 1, KEY_BLOCK, tq), F32),
            row_state, row_state, row_state, row_state, row_state, row_state,
            pltpu.VMEM((N_HEADS * tq, LANES), BF16),
            pltpu.VMEM((SUBLANES, N_HEADS * tq), F32),
            pltpu.VMEM((N_KV_HEADS, LANES, KV_REP * tq), F32),
        ],
    )
    return pl.pallas_call(
        kern,
        grid_spec=grid_spec,
        out_shape=jax.ShapeDtypeStruct((bsz, seq, N_HEADS * LANES), BF16),
        compiler_params=_cp(("arbitrary", "arbitrary")),
        name="dsa",
    )(stats, q, qi, wt, k_all, vt_all, ki_all, bias_tiles)


def _merge_kernel(x_ref, ya_ref, at_ref, sga_ref, sgb_ref, wup_ref, wout_ref, g2_ref, wr_hi_ref, wr_lo_ref, br_ref,
                  x1_ref, h2_ref, gt_ref, rt_ref, cnt_ref, run_ref, *, tm, sub):
    step = pl.program_id(0)

    @pl.when(step % sub == 0)
    def _():
        run_ref[...] = jnp.zeros(run_ref.shape, F32)

    yb = _dot(at_ref[...], wup_ref[...])
    merged = sga_ref[...].astype(F32) * ya_ref[...].astype(F32) + sgb_ref[...].astype(F32) * yb
    x1 = x_ref[...] + _dot(merged.astype(BF16), wout_ref[...])
    x1_ref[...] = x1
    ms = jnp.mean(x1 * x1, axis=-1, keepdims=True)
    h2 = x1 * lax.rsqrt(ms + EPS) * g2_ref[...]
    h2_hi, h2_lo = _split(h2)
    h2_ref[...] = h2_hi

    wr_hi = wr_hi_ref[...]
    logit = (_dot_nt(wr_hi, h2_hi) + _dot_nt(wr_hi, h2_lo) + _dot_nt(wr_lo_ref[...], h2_hi)) + br_ref[:, 0:1]
    ne = logit.shape[0]
    eid = lax.broadcasted_iota(I32, (ne, tm), 0).astype(F32)
    selb = jnp.zeros((ne, tm), F32)
    tops = []
    picks = []
    for _ in range(TOP_K):
        mx = jnp.max(logit, axis=0, keepdims=True)
        pick = jnp.min(jnp.where(logit == mx, eid, float(ne)), axis=0, keepdims=True)
        hit = eid == pick
        selb = jnp.where(hit, 1.0, selb)
        logit = jnp.where(hit, -jnp.inf, logit)
        tops.append(mx)
        picks.append(hit)
    ex = [jnp.exp(t - tops[0]) for t in tops]
    den = ex[0] + ex[1] + ex[2] + ex[3]
    gate = jnp.zeros((ne, tm), F32)
    for hit, e in zip(picks, ex):
        gate = jnp.where(hit, e / den, gate)
    gt_ref[...] = gate

    sel = selb > 0.5
    selb = selb.astype(BF16)
    r_i = lax.broadcasted_iota(I32, (tm, tm), 0)
    c_i = lax.broadcasted_iota(I32, (tm, tm), 1)
    tri = jnp.where(r_i < c_i, 1.0, 0.0).astype(BF16)
    run = run_ref[...]
    rank = _dot(selb, tri) + jnp.broadcast_to(run[:, 0:1], (ne, tm))
    rt_ref[...] = jnp.where(sel, rank, -1.0)
    run = run + _dot(selb, jnp.ones((tm, LANES), BF16))
    run_ref[...] = run
    cnt_ref[...] = run


def _merge(x2, ya, attn, sga, sgb, mw, tm, moe_tile):
    t, d = x2.shape
    sub = moe_tile // tm
    ne = mw['wr_hi'].shape[0]

    def tok(i):
        return (i, 0)

    def cst(i):
        return (0, 0)

    consts = [mw['wup'], mw['wout'], mw['g2'], mw['wr_hi'], mw['wr_lo'], mw['br']]
    return pl.pallas_call(
        functools.partial(_merge_kernel, tm=tm, sub=sub),
        grid=(t // tm,),
        in_specs=[
            pl.BlockSpec((tm, d), tok),
            pl.BlockSpec((tm, d), tok),
            pl.BlockSpec((tm, attn.shape[-1]), tok),
            pl.BlockSpec((tm, d), tok),
            pl.BlockSpec((tm, d), tok),
        ] + [pl.BlockSpec(a.shape, cst) for a in consts],
        out_specs=(
            pl.BlockSpec((tm, d), tok),
            pl.BlockSpec((tm, d), tok),
            pl.BlockSpec((ne, tm), lambda i: (0, i)),
            pl.BlockSpec((ne, tm), lambda i: (0, i)),
            pl.BlockSpec((None, ne, LANES), lambda i: (i // sub, 0, 0)),
        ),
        out_shape=(
            jax.ShapeDtypeStruct((t, d), F32),
            jax.ShapeDtypeStruct((t, d), BF16),
            jax.ShapeDtypeStruct((ne, t), F32),
            jax.ShapeDtypeStruct((ne, t), F32),
            jax.ShapeDtypeStruct((t // moe_tile, ne, LANES), F32),
        ),
        scratch_shapes=[pltpu.VMEM((ne, LANES), F32)],
        compiler_params=_cp(("arbitrary",)),
        name="merge",
    )(x2, ya, attn, sga, sgb, *consts)


def _moe_kernel(cnt_ref, h2_ref, x1_hbm, gt_ref, rt_ref, wg_ref, wu_ref, wd_ref, bg_ref, bu_ref, bd_ref, y_ref,
                pg_ref, og_ref, sem, *, tt, pair):
    j = pl.program_id(0)
    e = pl.program_id(1)
    ne = pl.num_programs(1)
    rb = MOE_ROWS
    slot = e % MOE_GROUP

    @pl.when(e == 0)
    def _():
        cp = pltpu.make_async_copy(x1_hbm.at[pl.ds(pl.multiple_of(j * pair * tt, tt), pair * tt), :], y_ref, sem)
        cp.start()
        cp.wait()

    mine = lax.broadcasted_iota(I32, (SUBLANES, pair * tt), 0) == e % SUBLANES
    g_all = jnp.sum(jnp.where(mine, gt_ref[...], 0.0), axis=0, keepdims=True)
    r_all = jnp.sum(jnp.where(mine, rt_ref[...], 0.0), axis=0, keepdims=True)
    rid = lax.broadcasted_iota(I32, (rb, tt), 0).astype(F32)

    def one_hot(s, blk):
        return jnp.broadcast_to(r_all[:, s * tt:(s + 1) * tt], (rb, tt)) == (rid + (blk * rb).astype(F32))

    def gather(s, hit):
        p = jnp.where(hit, 1.0, 0.0).astype(BF16)
        return p, _dot(p, h2_ref[s * tt:(s + 1) * tt, :]).astype(BF16)

    def expert(xg):
        a = jnp.minimum(_dot(xg, wg_ref[0]) + bg_ref[0], SWIGLU_LIMIT)
        b = jnp.clip(_dot(xg, wu_ref[0]) + bu_ref[0], -SWIGLU_LIMIT, SWIGLU_LIMIT)
        hid = a * jax.nn.sigmoid(SWIGLU_ALPHA * a) * (b + 1.0)
        return _dot(hid.astype(BF16), wd_ref[0]) + bd_ref[0]

    def gated(s, hit, o):
        g_row = jnp.broadcast_to(g_all[:, s * tt:(s + 1) * tt], (rb, tt))
        return (o * jnp.sum(jnp.where(hit, g_row, 0.0), axis=1, keepdims=True)).astype(BF16)

    hits = [one_hot(s, jnp.int32(0)) for s in range(pair)]
    gathered = [gather(s, hits[s]) for s in range(pair)]
    o = expert(jnp.concatenate([xg for _, xg in gathered], axis=0))
    r0 = pl.multiple_of(slot * rb, rb)
    for s in range(pair):
        pg_ref[s, pl.ds(r0, rb), :] = gathered[s][0]
        og_ref[s, pl.ds(r0, rb), :] = gated(s, hits[s], o[s * rb:(s + 1) * rb])

    @pl.when(slot == MOE_GROUP - 1)
    def _():
        for s in range(pair):
            y_ref[s * tt:(s + 1) * tt, :] += _dot_tn(pg_ref[s], og_ref[s])

    for s in range(pair):
        def overflow(blk, c, s=s):
            hit = one_hot(s, blk)
            p, xg = gather(s, hit)
            y_ref[s * tt:(s + 1) * tt, :] += _dot_tn(p, gated(s, hit, expert(xg)))
            return c

        n_rows = cnt_ref[(j * pair + s) * ne + e]
        lax.fori_loop(1, (n_rows + rb - 1) // rb, overflow, 0)


def _moe(h2, x1, gt, rt, cnt, ew, tt):
    t, d = h2.shape
    ne = gt.shape[0]
    nt = t // tt
    f = ew['wg'].shape[-1]
    pair = MOE_PAIR if nt % MOE_PAIR == 0 else 1
    grid_spec = pltpu.PrefetchScalarGridSpec(
        num_scalar_prefetch=1,
        grid=(nt // pair, ne),
        in_specs=[
            pl.BlockSpec((pair * tt, d), lambda j, e, c: (j, 0)),
            pl.BlockSpec(memory_space=pl.ANY),
            pl.BlockSpec((SUBLANES, pair * tt), lambda j, e, c: (e // SUBLANES, j)),
            pl.BlockSpec((SUBLANES, pair * tt), lambda j, e, c: (e // SUBLANES, j)),
            pl.BlockSpec((1, d, f), lambda j, e, c: (e, 0, 0)),
            pl.BlockSpec((1, d, f), lambda j, e, c: (e, 0, 0)),
            pl.BlockSpec((1, f, d), lambda j, e, c: (e, 0, 0)),
            pl.BlockSpec((1, 1, f), lambda j, e, c: (e, 0, 0)),
            pl.BlockSpec((1, 1, f), lambda j, e, c: (e, 0, 0)),
            pl.BlockSpec((1, 1, d), lambda j, e, c: (e, 0, 0)),
        ],
        out_specs=pl.BlockSpec((pair * tt, d), lambda j, e, c: (j, 0)),
        scratch_shapes=[pltpu.VMEM((pair, MOE_GROUP * MOE_ROWS, tt), BF16),
                        pltpu.VMEM((pair, MOE_GROUP * MOE_ROWS, d), BF16),
                        pltpu.SemaphoreType.DMA(())],
    )
    assert ne % MOE_GROUP == 0
    return pl.pallas_call(
        functools.partial(_moe_kernel, tt=tt, pair=pair),
        grid_spec=grid_spec,
        out_shape=jax.ShapeDtypeStruct((t, d), F32),
        compiler_params=_cp(("arbitrary", "arbitrary")),
        name="moe",
    )(cnt, h2, x1, gt, rt, ew['wg'], ew['wu'], ew['wd'], ew['bg'], ew['bu'], ew['bd'])


def _pad_heads(wmat, n_heads, width):
    d = wmat.shape[0]
    w3 = wmat.reshape(d, n_heads, width)
    return jnp.pad(w3, ((0, 0), (0, 0), (0, LANES - width))).reshape(d, n_heads * LANES)


def _pad_lanes(v, width=LANES):
    v = v.reshape(1, -1)
    return jnp.pad(v, ((0, 0), (0, width - v.shape[1])))


def _rel_bucket(rel):
    half = REL_BUCKETS // 2
    max_exact = half // 2
    n = jnp.abs(rel)
    large = max_exact + (jnp.log(jnp.maximum(n, 1).astype(jnp.float32) / max_exact)
                         / math.log(REL_MAX_DIST / max_exact) * (half - max_exact)).astype(jnp.int32)
    large = jnp.minimum(large, half - 1)
    return jnp.where(rel > 0, half, 0) + jnp.where(n < max_exact, n, large)


def _bias_tiles(rel_bias):
    tk = KEY_TILE
    half = REL_BUCKETS // 2
    max_exact = half // 2
    n_sat = int(math.ceil(max_exact * (REL_MAX_DIST / max_exact) ** ((half - 1 - max_exact) / (half - max_exact)))) + 2
    nd = (n_sat + 2 * tk - 2) // tk + 1
    dd = jnp.arange(nd, dtype=I32)[:, None, None]
    c = jnp.arange(tk, dtype=I32)[None, :, None]
    r = jnp.arange(tk, dtype=I32)[None, None, :]
    bucket = _rel_bucket(c - r - dd * tk)
    onehot = (bucket[..., None] == jnp.arange(REL_BUCKETS, dtype=I32)).astype(F32)
    tiles = jnp.einsum('dcrb,bh->dhcr', onehot, rel_bias.astype(F32) * LOG2E,
                       precision=lax.Precision.HIGHEST)
    return tiles


def _prep_proj(norm1_g, w_in, q_norm_g, k_norm_g, idx_k_norm_g, idx_k_norm_b, d_model):
    ssm_w = d_model // 2
    attn_w = N_HEADS * HEAD_DIM
    kv = N_KV_HEADS * HEAD_DIM
    sizes = [ssm_w, attn_w, kv, kv, IDX_HEADS * IDX_DIM, IDX_DIM, IDX_HEADS, d_model, d_model]
    pts = np.cumsum(sizes)[:-1].tolist()
    wu, wq, wk, wv, wqi, wki, wwi, wga, wgb = jnp.split(w_in, pts, axis=1)
    bf = lambda a: a.astype(BF16)
    blk = np.kron(np.eye(N_KV_HEADS), np.ones((HEAD_DIM, HEAD_DIM))) / HEAD_DIM
    wwit = jnp.pad(wwi.T, ((0, 2 * SUBLANES - IDX_HEADS), (0, 0)))
    return dict(
        g1=norm1_g.reshape(1, -1).astype(F32),
        wu=bf(wu), wq=bf(_pad_heads(wq, N_HEADS, HEAD_DIM)), wk=bf(_pad_heads(wk, N_KV_HEADS, HEAD_DIM)),
        wv=bf(_pad_heads(wv, N_KV_HEADS, HEAD_DIM)), wqi=bf(_pad_heads(wqi, IDX_HEADS, IDX_DIM)),
        wki=bf(_pad_heads(wki, 1, IDX_DIM)), wwit=bf(wwit),
        wga=bf(wga), wgb=bf(wgb), wkc=bf(wk), wvc=bf(wv),
        gq=_pad_lanes(q_norm_g.astype(F32)), gk=_pad_lanes(k_norm_g.astype(F32)),
        gkc=jnp.tile(k_norm_g.astype(F32), N_KV_HEADS).reshape(1, -1),
        gi=_pad_lanes(idx_k_norm_g.astype(F32)), bi=_pad_lanes(idx_k_norm_b.astype(F32)),
        ones_h=jnp.full((LANES, LANES), 1.0 / HEAD_DIM, BF16),
        ones_c=jnp.asarray(blk, BF16),
    )


def _prep_s5(lre, lim, log_dt, b_re, b_im, c_re, c_im, dvec, wa, wb):
    g, p = lre.shape
    ch = b_re.shape[-1]
    lam = lax.complex(lre.astype(F32), lim.astype(F32))
    dt = jnp.exp(log_dt.astype(F32))[:, None]
    a_bar = jnp.exp(lam * dt)
    b_bar = ((a_bar - 1.0) / lam)[:, :, None] * lax.complex(b_re.astype(F32), b_im.astype(F32))
    gs = g // S5_DIAG
    eye = jnp.eye(gs, dtype=F32)

    def blocks_in(m):
        return jnp.einsum('jgpc,gh->jgchp', m.reshape(S5_DIAG, gs, p, ch), eye).reshape(S5_DIAG, gs * ch, gs * p)

    def blocks_out(m):
        return jnp.einsum('jgcp,gh->jgphc', m.reshape(S5_DIAG, gs, ch, p), eye).reshape(S5_DIAG, gs * p, gs * ch)

    return dict(
        b_re=blocks_in(jnp.real(b_bar)).astype(BF16), b_im=blocks_in(jnp.imag(b_bar)).astype(BF16),
        c_re=blocks_out(c_re.astype(F32)).astype(BF16), c_im=blocks_out(-c_im.astype(F32)).astype(BF16),
        a_re=jnp.real(a_bar).reshape(1, g * p), a_im=jnp.imag(a_bar).reshape(1, g * p),
        d=dvec.reshape(1, -1).astype(F32), wa=wa.astype(BF16), wb=wb.astype(BF16),
    )


def _prep_merge(w_attn_up, w_out, norm2_g, w_router, b_router):
    d = w_attn_up.shape[1]
    wup = jnp.pad(w_attn_up.reshape(N_HEADS, HEAD_DIM, d), ((0, 0), (0, LANES - HEAD_DIM), (0, 0)))
    wr_t = w_router.astype(F32).T
    wr_hi = wr_t.astype(BF16)
    wr_lo = (wr_t - wr_hi.astype(F32)).astype(BF16)
    return dict(
        wup=wup.reshape(N_HEADS * LANES, d).astype(BF16), wout=w_out.astype(BF16),
        g2=norm2_g.reshape(1, -1).astype(F32), wr_hi=wr_hi, wr_lo=wr_lo,
        br=jnp.broadcast_to(b_router.astype(F32)[:, None], (b_router.shape[0], LANES)),
    )


def _prep_moe(wg, bg, wu, bu, wd, bd):
    return dict(wg=wg.astype(BF16), wu=wu.astype(BF16), wd=wd.astype(BF16),
                bg=bg.astype(F32)[:, None, :], bu=bu.astype(F32)[:, None, :], bd=bd.astype(F32)[:, None, :])


def _pick_tile(n, pref):
    t = min(n, pref)
    while n % t:
        t //= 2
    return t


def _pad_axis(a, axis, size):
    pad = [(0, 0)] * a.ndim
    pad[axis] = (0, size - a.shape[axis])
    return jnp.pad(a, pad)


def _trunk_layer(x, past_k, past_v, past_ik, h0_re, h0_im, pw, sw, mw, ew, bias_tiles):
    bsz, seq, d = x.shape
    t = bsz * seq
    tm = _pick_tile(seq, 512)
    u_tb, q, kp, vp, qi, kip, wt, sga, sgb, kc, vc, kic = _proj(x, pw, bsz, seq, tm)

    half = sw['a_re'].shape[1]
    if h0_re is None:
        h0 = jnp.zeros((bsz, 2 * half), F32)
    else:
        h0 = jnp.concatenate([h0_re.reshape(bsz, half), h0_im.reshape(bsz, half)], axis=1).astype(F32)
    tc = _pick_tile(seq, max(1, S5_ROWS // bsz))
    ya, hout = _s5(u_tb, h0, sw, bsz, seq, tc)
    groups = half // SSM_STATE
    s_re = hout[:, :half].reshape(bsz, groups, SSM_STATE)
    s_im = hout[:, half:].reshape(bsz, groups, SSM_STATE)

    past = 0 if past_k is None else past_k.shape[1]
    n_keys = past + seq
    lk = -(-n_keys // KEY_BLOCK) * KEY_BLOCK
    kip3 = kip.reshape(bsz, seq, LANES)
    if past:
        lane = jnp.arange(LANES)
        pk = jnp.pad(past_k.astype(F32), ((0, 0), (0, 0), (0, 0), (0, LANES - HEAD_DIM)))
        pk = jnp.where(lane == HEAD_DIM, 1.0, pk).astype(BF16)
        pv = jnp.pad(past_v.astype(F32), ((0, 0), (0, 0), (0, 0), (0, LANES - HEAD_DIM)))
        pv = jnp.where(lane == HEAD_DIM, 1.0, pv).astype(BF16)
        pik = jnp.pad(past_ik.astype(F32), ((0, 0), (0, 0), (0, LANES - IDX_DIM))).astype(BF16)
        k_all = jnp.concatenate([pk.transpose(0, 2, 1, 3), kp], axis=2)
        v_all = jnp.concatenate([pv.transpose(0, 2, 1, 3), vp], axis=2)
        ki_all = jnp.concatenate([pik, kip3], axis=1)
    else:
        k_all, v_all, ki_all = kp, vp, kip3
    k_all = _pad_axis(k_all, 2, lk)
    v_all = _pad_axis(v_all, 2, lk)
    ki_all = _pad_axis(ki_all, 1, lk)
    vt_all = v_all.reshape(bsz, N_KV_HEADS, lk // KEY_BLOCK, KEY_BLOCK, LANES).transpose(0, 1, 2, 4, 3)
    tq = KEY_TILE
    seq_q = -(-seq // tq) * tq
    q_p, qi_p, wt_p = _pad_axis(q, 2, seq_q), _pad_axis(qi, 2, seq_q), _pad_axis(wt, 2, seq_q)
    kf = k_all[..., :HEAD_DIM].astype(F32)
    kmax = jnp.sqrt(jnp.max(jnp.sum(kf * kf, axis=-1), axis=-1)).reshape(-1)
    bfar = bias_tiles[-1, :, 0, 0]
    bmax = jnp.max(jnp.abs(bias_tiles), axis=(0, 2, 3))
    stats = jnp.concatenate([kmax, bmax, bfar]).astype(F32)
    bias_tiles = bias_tiles - bfar[None, :, None, None]
    attn = _dsa(stats, q_p, qi_p, wt_p, k_all, vt_all, ki_all, bias_tiles, bsz, seq_q, past, n_keys, tq)
    attn = attn[:, :seq]

    moe_tile = _pick_tile(t, MOE_TILE)
    tm2 = _pick_tile(moe_tile, 512)
    x1, h2, gt, rt, cnt = _merge(x.reshape(t, d), ya, attn.reshape(t, attn.shape[-1]), sga, sgb, mw, tm2, moe_tile)
    cnt_i = cnt[:, :, 0].astype(I32).reshape(-1)
    y = _moe(h2, x1, gt, rt, cnt_i, ew, moe_tile)

    k_new = kc.reshape(bsz, seq, N_KV_HEADS, HEAD_DIM)
    v_new = vc.reshape(bsz, seq, N_KV_HEADS, HEAD_DIM)
    ik_new = kic.reshape(bsz, seq, IDX_DIM)
    return y.reshape(bsz, seq, d), k_new, v_new, ik_new, s_re, s_im


def kernel(x_prompt, x_sample, cache_k, cache_v, cache_idx_k, state_ssm_re, state_ssm_im, rel_bias, norm1_g, w_in, ssm_lambda_re, ssm_lambda_im, ssm_log_dt, ssm_b_re, ssm_b_im, ssm_c_re, ssm_c_im, ssm_d, ssm_w_glu_a, ssm_w_glu_b, q_norm_g, k_norm_g, idx_k_norm_g, idx_k_norm_b, w_attn_up, w_out, norm2_g, moe_w_router, moe_b_router, moe_w_gate, moe_b_gate, moe_w_up, moe_b_up, moe_w_down, moe_b_down):
    depth = w_in.shape[0]
    d_model = x_prompt.shape[-1]
    bias_tiles = _bias_tiles(rel_bias)
    xp, xs = x_prompt, x_sample
    st_p, st_s = [], []
    for l in range(depth):
        pw = _prep_proj(norm1_g[l], w_in[l], q_norm_g[l], k_norm_g[l], idx_k_norm_g[l], idx_k_norm_b[l], d_model)
        sw = _prep_s5(ssm_lambda_re[l], ssm_lambda_im[l], ssm_log_dt[l], ssm_b_re[l], ssm_b_im[l], ssm_c_re[l],
                      ssm_c_im[l], ssm_d[l], ssm_w_glu_a[l], ssm_w_glu_b[l])
        mw = _prep_merge(w_attn_up[l], w_out[l], norm2_g[l], moe_w_router[l], moe_b_router[l])
        ew = _prep_moe(moe_w_gate[l], moe_b_gate[l], moe_w_up[l], moe_b_up[l], moe_w_down[l], moe_b_down[l])
        xp, *sp = _trunk_layer(xp, None, None, None, None, None, pw, sw, mw, ew, bias_tiles)
        xs, *ss = _trunk_layer(xs, cache_k[l], cache_v[l], cache_idx_k[l], state_ssm_re[l], state_ssm_im[l],
                               pw, sw, mw, ew, bias_tiles)
        st_p.append(sp)
        st_s.append(ss)
    outs_p = [jnp.stack([s[i] for s in st_p]) for i in range(5)]
    outs_s = [jnp.stack([s[i] for s in st_s]) for i in range(5)]
    return (xp, xs, *outs_p, *outs_s)
```

```python
import functools
import math

import numpy as np
import jax
import jax.numpy as jnp
from jax import lax
from jax.experimental import pallas as pl
from jax.experimental.pallas import tpu as pltpu

F32 = jnp.float32
BF16 = jnp.bfloat16
I32 = jnp.int32

LANES = 128
SUBLANES = 8
VMEM_LIMIT = 56 * 1024 * 1024

CHUNK = 64
SSM_GROUP_CH = 16
SSM_STATE = 64
N_HEADS = 8
HEAD_DIM = 64
N_KV_HEADS = 2
KV_REP = N_HEADS // N_KV_HEADS
IDX_HEADS = 8
IDX_DIM = 64
TOPK_MAX = 256
REL_BUCKETS = 32
REL_MAX_DIST = 1024
N_EXPERTS = 32
TOP_K = 4
SWIGLU_LIMIT = 7.0
SWIGLU_ALPHA = 1.702
EPS = 1e-6

KEY_TILE = 128
KEY_BLOCK = 256
SCORE_CHUNK = 4
ATTEND_CHUNK = 4
LOG2E = math.log2(math.e)
NEG_BIG = -1e30
SHIFT_LIMIT = 30.0
S5_ROWS = 1024
S5_DIAG = 2
SEARCH_FIRST = 8
SEARCH_GROUP = 4
MOE_TILE = 1024
MOE_ROWS = 160
MOE_GROUP = 8
MOE_PAIR = 2


def _cp(sem):
    return pltpu.CompilerParams(dimension_semantics=sem, vmem_limit_bytes=VMEM_LIMIT)


def _dot(a, b):
    return jnp.dot(a, b, preferred_element_type=F32)


def _dot_nt(a, b):
    return lax.dot_general(a, b, (((1,), (1,)), ((), ())), preferred_element_type=F32)


def _dot_tn(a, b):
    return lax.dot_general(a, b, (((0,), (0,)), ((), ())), preferred_element_type=F32)


def _split(a):
    hi = a.astype(BF16)
    lo = (a - hi.astype(F32)).astype(BF16)
    return hi, lo


def _dot_split(a, g):
    hi, lo = _split(a)
    return _dot(hi, g) + _dot(lo, g)


def _proj_kernel(x_ref, g1_ref, wu_ref, wq_ref, wk_ref, wv_ref, wqi_ref, wki_ref, wwit_ref, wga_ref, wgb_ref,
                 wkc_ref, wvc_ref, gq_ref, gk_ref, gkc_ref, gi_ref, bi_ref, ones_h_ref, ones_c_ref,
                 u_ref, q_ref, kp_ref, vp_ref, qi_ref, kip_ref, wt_ref, sga_ref, sgb_ref,
                 kc_ref, vc_ref, kic_ref):
    x = x_ref[...]
    ms = jnp.mean(x * x, axis=-1, keepdims=True)
    hn = (x * lax.rsqrt(ms + EPS) * g1_ref[...]).astype(BF16)
    ones_h = ones_h_ref[...]
    lane = lax.broadcasted_iota(I32, (x.shape[0], LANES), 1)

    u_ref[...] = _dot(hn, wu_ref[...]).astype(BF16)

    q = _dot(hn, wq_ref[...])
    scale = HEAD_DIM ** -0.5 * LOG2E
    for h in range(N_HEADS):
        qh = q[:, h * LANES:(h + 1) * LANES]
        msq = _dot_split(qh * qh, ones_h)
        q_ref[h] = (qh * lax.rsqrt(msq + EPS) * (gq_ref[...] * scale)).astype(BF16)

    k = _dot(hn, wk_ref[...])
    for g in range(N_KV_HEADS):
        kg = k[:, g * LANES:(g + 1) * LANES]
        msk = _dot_split(kg * kg, ones_h)
        kn = kg * lax.rsqrt(msk + EPS) * gk_ref[...]
        kp_ref[g] = jnp.where(lane == HEAD_DIM, 1.0, kn).astype(BF16)

    v = _dot(hn, wv_ref[...])
    for g in range(N_KV_HEADS):
        vg = v[:, g * LANES:(g + 1) * LANES]
        vp_ref[g] = jnp.where(lane == HEAD_DIM, 1.0, vg).astype(BF16)

    qi = _dot(hn, wqi_ref[...])
    for h in range(IDX_HEADS):
        qi_ref[h] = qi[:, h * LANES:(h + 1) * LANES].astype(BF16)

    ki = _dot(hn, wki_ref[...])
    mu = _dot_split(ki, ones_h)
    xc = jnp.where(lane < IDX_DIM, ki - mu, 0.0)
    var = _dot_split(xc * xc, ones_h)
    kin = xc * lax.rsqrt(var + EPS) * gi_ref[...] + bi_ref[...]
    kip_ref[...] = kin.astype(BF16)
    kic_ref[...] = kin[:, :IDX_DIM]

    wt = _dot_nt(wwit_ref[...], hn)
    wt_ref[...] = wt[0:IDX_HEADS, :] * (IDX_HEADS ** -0.5 * IDX_DIM ** -0.5)

    sga_ref[...] = jax.nn.sigmoid(_dot(hn, wga_ref[...])).astype(BF16)
    sgb_ref[...] = jax.nn.sigmoid(_dot(hn, wgb_ref[...])).astype(BF16)

    kc = _dot(hn, wkc_ref[...])
    mskc = _dot_split(kc * kc, ones_c_ref[...])
    kcn = kc * lax.rsqrt(mskc + EPS) * gkc_ref[...]
    vcn = _dot(hn, wvc_ref[...])
    for g in range(N_KV_HEADS):
        kc_ref[:, g, :] = kcn[:, g * HEAD_DIM:(g + 1) * HEAD_DIM]
        vc_ref[:, g, :] = vcn[:, g * HEAD_DIM:(g + 1) * HEAD_DIM]


def _proj(x, pw, bsz, seq, tm):
    d = x.shape[-1]
    nt = seq // tm
    t = bsz * seq
    x2 = x.reshape(t, d)

    def tok(b, i):
        return (b * nt + i, 0)

    def cst(b, i):
        return (0, 0)

    def wspec(a):
        return pl.BlockSpec(a.shape, cst)

    weights = [pw['g1'], pw['wu'], pw['wq'], pw['wk'], pw['wv'], pw['wqi'], pw['wki'], pw['wwit'], pw['wga'],
               pw['wgb'], pw['wkc'], pw['wvc'], pw['gq'], pw['gk'], pw['gkc'], pw['gi'], pw['bi'],
               pw['ones_h'], pw['ones_c']]
    ssm_w = pw['wu'].shape[1]
    out_shape = (
        jax.ShapeDtypeStruct((t, ssm_w), BF16),
        jax.ShapeDtypeStruct((bsz, N_HEADS, seq, LANES), BF16),
        jax.ShapeDtypeStruct((bsz, N_KV_HEADS, seq, LANES), BF16),
        jax.ShapeDtypeStruct((bsz, N_KV_HEADS, seq, LANES), BF16),
        jax.ShapeDtypeStruct((bsz, IDX_HEADS, seq, LANES), BF16),
        jax.ShapeDtypeStruct((t, LANES), BF16),
        jax.ShapeDtypeStruct((bsz, IDX_HEADS, seq), F32),
        jax.ShapeDtypeStruct((t, d), BF16),
        jax.ShapeDtypeStruct((t, d), BF16),
        jax.ShapeDtypeStruct((t, N_KV_HEADS, HEAD_DIM), F32),
        jax.ShapeDtypeStruct((t, N_KV_HEADS, HEAD_DIM), F32),
        jax.ShapeDtypeStruct((t, IDX_DIM), F32),
    )

    def hm(nh):
        return pl.BlockSpec((None, nh, tm, LANES), lambda b, i: (b, 0, i, 0))

    out_specs = (
        pl.BlockSpec((tm, ssm_w), tok),
        hm(N_HEADS), hm(N_KV_HEADS), hm(N_KV_HEADS), hm(IDX_HEADS),
        pl.BlockSpec((tm, LANES), tok),
        pl.BlockSpec((None, IDX_HEADS, tm), lambda b, i: (b, 0, i)),
        pl.BlockSpec((tm, d), tok), pl.BlockSpec((tm, d), tok),
        pl.BlockSpec((tm, N_KV_HEADS, HEAD_DIM), lambda b, i: (b * nt + i, 0, 0)),
        pl.BlockSpec((tm, N_KV_HEADS, HEAD_DIM), lambda b, i: (b * nt + i, 0, 0)),
        pl.BlockSpec((tm, IDX_DIM), tok),
    )
    return pl.pallas_call(
        _proj_kernel,
        grid=(bsz, nt),
        in_specs=[pl.BlockSpec((tm, d), tok)] + [wspec(a) for a in weights],
        out_specs=out_specs,
        out_shape=out_shape,
        compiler_params=_cp(("arbitrary", "arbitrary")),
        name="proj",
    )(x2, *weights)


def _gelu_tanh(x):
    return 0.5 * x * (1.0 + jnp.tanh(math.sqrt(2.0 / math.pi) * (x + 0.044715 * (x * x * x))))


def _s5_kernel(u_ref, h0_ref, bre_ref, bim_ref, are_ref, aim_ref, cre_ref, cim_ref, dvec_ref, wa_ref, wb_ref,
               ya_ref, hout_ref, state_ref, bu_ref, yf_ref, ug_ref, *, bsz, tc, strip):
    s = pl.program_id(0)
    half = are_ref.shape[1]

    @pl.when(s == 0)
    def _():
        state_ref[...] = h0_ref[...]

    ssm_w = dvec_ref.shape[1]
    for b in range(bsz):
        for c in range(ssm_w // LANES):
            ug_ref[c, pl.ds(b, tc, stride=bsz), :] = u_ref[b, :, c * LANES:(c + 1) * LANES].astype(F32)
    u = jnp.concatenate([ug_ref[c] for c in range(ssm_w // LANES)], axis=1).astype(BF16)
    cw = u.shape[1] // S5_DIAG
    sw = half // S5_DIAG
    for j in range(S5_DIAG):
        uj = u[:, j * cw:(j + 1) * cw]
        bu_ref[:, j * sw:(j + 1) * sw] = _dot(uj, bre_ref[j])
        bu_ref[:, half + j * sw:half + (j + 1) * sw] = _dot(uj, bim_ref[j])

    for c0 in range(0, half, strip):
        ar = jnp.broadcast_to(are_ref[:, c0:c0 + strip], (bsz, strip))
        ai = jnp.broadcast_to(aim_ref[:, c0:c0 + strip], (bsz, strip))
        hr0 = state_ref[:, c0:c0 + strip]
        hi0 = state_ref[:, half + c0:half + c0 + strip]

        def step(t, carry):
            hr, hi = carry
            r0 = pl.multiple_of(t * bsz, bsz)
            br = bu_ref[pl.ds(r0, bsz), c0:c0 + strip]
            bi = bu_ref[pl.ds(r0, bsz), half + c0:half + c0 + strip]
            nr = ar * hr - ai * hi + br
            ni = ar * hi + ai * hr + bi
            bu_ref[pl.ds(r0, bsz), c0:c0 + strip] = nr
            bu_ref[pl.ds(r0, bsz), half + c0:half + c0 + strip] = ni
            return nr, ni

        hr, hi = lax.fori_loop(0, tc, step, (hr0, hi0))
        state_ref[:, c0:c0 + strip] = hr
        state_ref[:, half + c0:half + c0 + strip] = hi

    ys = []
    for j in range(S5_DIAG):
        s_re = bu_ref[:, j * sw:(j + 1) * sw].astype(BF16)
        s_im = bu_ref[:, half + j * sw:half + (j + 1) * sw].astype(BF16)
        ys.append(_dot(s_re, cre_ref[j]) + _dot(s_im, cim_ref[j]))
    y = jnp.concatenate(ys, axis=1) + dvec_ref[...] * u.astype(F32)
    g = _gelu_tanh(y).astype(BF16)
    ya = _dot(g, wa_ref[...]) * jax.nn.sigmoid(_dot(g, wb_ref[...]))
    n_chunk = ya.shape[1] // LANES
    for c in range(n_chunk):
        yf_ref[c] = ya[:, c * LANES:(c + 1) * LANES]
    for b in range(bsz):
        ya_ref[b] = jnp.concatenate([yf_ref[c, pl.ds(b, tc, stride=bsz), :] for c in range(n_chunk)],
                                    axis=1).astype(BF16)

    @pl.when(s == pl.num_programs(0) - 1)
    def _():
        hout_ref[...] = state_ref[...]


def _s5(u, h0, sw, bsz, seq, tc):
    rows = tc * bsz
    ssm_w = sw['d'].shape[1]
    half = sw['a_re'].shape[1]
    two_half = 2 * half
    d = sw['wa'].shape[1]
    u3 = u.reshape(bsz, seq, ssm_w)
    strip = min(512, half)

    consts = [h0, sw['b_re'], sw['b_im'], sw['a_re'], sw['a_im'], sw['c_re'], sw['c_im'], sw['d'], sw['wa'], sw['wb']]

    def cst(s):
        return (0, 0)

    def cspec(a):
        return pl.BlockSpec(a.shape, lambda s: (0,) * a.ndim)

    ya, hout = pl.pallas_call(
        functools.partial(_s5_kernel, bsz=bsz, tc=tc, strip=strip),
        grid=(seq // tc,),
        in_specs=[pl.BlockSpec((bsz, tc, ssm_w), lambda s: (0, s, 0))] + [cspec(a) for a in consts],
        out_specs=(pl.BlockSpec((bsz, tc, d), lambda s: (0, s, 0)), pl.BlockSpec((bsz, two_half), cst)),
        out_shape=(jax.ShapeDtypeStruct((bsz, seq, d), BF16), jax.ShapeDtypeStruct((bsz, two_half), F32)),
        scratch_shapes=[pltpu.VMEM((bsz, two_half), F32), pltpu.VMEM((rows, two_half), F32),
                        pltpu.VMEM((d // LANES, rows, LANES), F32), pltpu.VMEM((ssm_w // LANES, rows, LANES), F32)],
        compiler_params=_cp(("arbitrary",)),
        name="s5",
    )(u3, *consts)
    return ya.reshape(bsz * seq, d), hout


def _f2key(x):
    b = lax.bitcast_convert_type(x, I32)
    return b ^ ((b >> 31) & 0x7FFFFFFF)


def _key2f(k):
    return lax.bitcast_convert_type(k ^ ((k >> 31) & 0x7FFFFFFF), F32)


def _dsa_kernel(st_ref, q_ref, qi_ref, wt_ref, k_ref, vt_ref, ki_ref, bias_ref, o_ref,
                s_ref, lo_ref, hi_ref, clo_ref, glo_ref, ghi_ref, side_ref, q2_ref, mrow_ref, acc_ref,
                *, bsz, tq, past, n_keys, topk, nkt, nd, idx_bits, near_max):
    b_id = pl.program_id(0)
    i = pl.program_id(1)
    kb = KEY_BLOCK
    sl = SUBLANES
    q0 = past + i * tq
    last_chunk = (q0 + tq - 1) // CHUNK
    n_kt = jnp.minimum(nkt, ((last_chunk + 1) * CHUNK + kb - 1) // kb)
    d0 = q0 // KEY_TILE

    krow = lax.broadcasted_iota(I32, (kb, tq), 0)
    q_chunk = (q0 + lax.broadcasted_iota(I32, (kb, tq), 1)) // CHUNK
    qc8 = (q0 + lax.broadcasted_iota(I32, (sl, tq), 1)) // CHUNK
    n_adm = jnp.minimum((qc8 + 1) * CHUNK, n_keys)
    n_admf = n_adm.astype(F32)
    is_pad = q0 + lax.broadcasted_iota(I32, (sl, tq), 1) >= n_keys
    needf = jnp.where(is_pad, n_adm, jnp.minimum(topk, n_adm)).astype(F32)

    def bcast(x):
        return jnp.broadcast_to(x[0:1, :], (kb, tq))

    def rep(x):
        return jnp.broadcast_to(x, (sl, tq))

    qi = qi_ref[...].reshape(IDX_HEADS * tq, LANES)

    def score_blocks(kt, nb, masked):
        k0 = pl.multiple_of(kt * kb, kb)
        s = _dot_nt(ki_ref[pl.ds(k0, nb * kb), :], qi)
        for j in range(nb):
            sc = jnp.zeros((kb, tq), F32)
            for h in range(IDX_HEADS):
                sc = sc + wt_ref[h:h + 1, :] * jnp.maximum(s[j * kb:(j + 1) * kb, h * tq:(h + 1) * tq], 0.0)
            if masked:
                kpos = k0 + j * kb + krow
                adm = ((kpos // CHUNK) <= q_chunk) & (kpos < n_keys)
                sc = jnp.where(adm, sc, -jnp.inf)
            s_ref[kt + j] = sc

    n_open = n_kt - 1

    def score_chunk(j, c):
        score_blocks(SCORE_CHUNK * j, SCORE_CHUNK, False)
        return c

    lax.fori_loop(0, n_open // SCORE_CHUNK, score_chunk, 0)

    def score_single(j, c):
        score_blocks(j, 1, False)
        return c

    lax.fori_loop(n_open - n_open % SCORE_CHUNK, n_open, score_single, 0)
    score_blocks(n_kt - 1, 1, True)
    s_ref[n_kt] = jnp.full((kb, tq), -jnp.inf, F32)
    n_pair = (n_kt + 1) // 2

    part = 4 * sl

    def fold(x, op):
        x = x.reshape(kb // part, part, tq)
        acc = x[0]
        for j in range(1, kb // part):
            acc = op(acc, x[j])
        return acc

    def count(pred):
        def one(kt):
            return fold(jnp.where(pred(s_ref[kt], kt), 1.0, 0.0), jnp.add)
        c = lax.fori_loop(0, n_pair, lambda j, c: c + (one(2 * j) + one(2 * j + 1)), jnp.zeros((part, tq), F32))
        return rep(jnp.sum(c, axis=0, keepdims=True))

    def minmax(j, c):
        mx, mn = c
        for kt in (2 * j, 2 * j + 1):
            s = s_ref[kt]
            mx = jnp.maximum(mx, fold(s, jnp.maximum))
            mn = jnp.minimum(mn, fold(jnp.where(s == -jnp.inf, jnp.inf, s), jnp.minimum))
        return mx, mn

    mx, mn = lax.fori_loop(0, n_pair, minmax,
                           (jnp.full((part, tq), -jnp.inf, F32), jnp.full((part, tq), jnp.inf, F32)))
    lo_ref[...] = rep(jnp.min(mn, axis=0, keepdims=True))
    hi_ref[...] = _key2f(_f2key(rep(jnp.max(mx, axis=0, keepdims=True))) + 1)
    def odds(cnt):
        c = jnp.clip(cnt, 0.5, n_admf - 0.5)
        return jnp.log((n_admf - c) / c)

    target = odds(needf - 0.5)
    clo_ref[...] = n_admf
    glo_ref[...] = target - odds(n_admf)
    ghi_ref[...] = target - odds(jnp.zeros((sl, tq), F32))
    side_ref[...] = jnp.zeros((sl, tq), F32)

    def searching(lo, hi, clo):
        return (_f2key(hi) > _f2key(lo) + 1) & (clo > needf)

    def refine(it, c):
        lo, hi, clo = lo_ref[...], hi_ref[...], clo_ref[...]
        glo, ghi, side = glo_ref[...], ghi_ref[...], side_ref[...]
        k_t = _f2key(lo + (hi - lo) * (glo / (glo - ghi)))
        t = _key2f(jnp.minimum(jnp.maximum(k_t, _f2key(lo) + 1), _f2key(hi) - 1))
        tb = bcast(t)
        cnt = count(lambda s, kt: s >= tb)
        g = target - odds(cnt)
        open_ = searching(lo, hi, clo)
        up = open_ & (cnt >= needf)
        dn = open_ & (cnt < needf)
        lo_ref[...] = jnp.where(up, t, lo)
        clo_ref[...] = jnp.where(up, cnt, clo)
        hi_ref[...] = jnp.where(dn, t, hi)
        glo_ref[...] = jnp.where(up, g, jnp.where(dn & (side < 0.0), glo * 0.5, glo))
        ghi_ref[...] = jnp.where(dn, g, jnp.where(up & (side > 0.0), ghi * 0.5, ghi))
        side_ref[...] = jnp.where(up, 1.0, jnp.where(dn, -1.0, side))
        return c

    def snap():
        lo, hi, clo = lo_ref[...], hi_ref[...], clo_ref[...]
        lo_b, hi_b = bcast(lo), bcast(hi)

        def body(j, c):
            a, b = c
            for kt in (2 * j, 2 * j + 1):
                s = s_ref[kt]
                a = jnp.minimum(a, fold(jnp.where(s >= lo_b, s, jnp.inf), jnp.minimum))
                b = jnp.maximum(b, fold(jnp.where(s < hi_b, s, -jnp.inf), jnp.maximum))
            return a, b

        a, b = lax.fori_loop(0, n_pair, body,
                             (jnp.full((part, tq), jnp.inf, F32), jnp.full((part, tq), -jnp.inf, F32)))
        open_ = searching(lo, hi, clo)
        lo_ref[...] = jnp.where(open_, rep(jnp.min(a, axis=0, keepdims=True)), lo)
        hi_ref[...] = jnp.where(open_, _key2f(_f2key(rep(jnp.max(b, axis=0, keepdims=True))) + 1), hi)

    def n_searching():
        return jnp.max(jnp.where(searching(lo_ref[...], hi_ref[...], clo_ref[...]), 1.0, 0.0))

    def group(c):
        grp, _ = c
        lax.fori_loop(0, jnp.where(grp == 0, SEARCH_FIRST, SEARCH_GROUP), refine, 0)
        snap()
        return grp + 1, n_searching()

    lax.while_loop(lambda c: c[1] > 0.0, group, (jnp.int32(0), n_searching()))
    thr = lo_ref[...]
    thr_b = bcast(thr)

    n_tied = jnp.max(jnp.where(clo_ref[...] > needf, 1.0, 0.0))

    @pl.when(n_tied > 0.0)
    def _():
        rem_b = bcast(needf - count(lambda s, kt: s > thr_b))
        tri = jnp.where(lax.broadcasted_iota(I32, (kb, kb), 0) >= lax.broadcasted_iota(I32, (kb, kb), 1),
                        1.0, 0.0).astype(BF16)

        def drop(kt, seen):
            s = s_ref[kt]
            tie = s == thr_b
            rank = _dot(tri, jnp.where(tie, 1.0, 0.0).astype(BF16)) + bcast(seen)
            s_ref[kt] = jnp.where(tie & (rank > rem_b), -jnp.inf, s)
            return rep(rank[kb - 1:kb, :])

        lax.fori_loop(0, n_kt, drop, jnp.zeros((sl, tq), F32))

    rows_g = KV_REP * tq
    qf = q_ref[...].reshape(N_HEADS * tq, LANES).astype(F32)
    qn = jnp.sqrt(jnp.sum(qf * qf, axis=1, keepdims=True))
    lane = lax.broadcasted_iota(I32, (tq, LANES), 1)
    worst = jnp.float32(0.0)
    for h in range(N_HEADS):
        kmax = st_ref[b_id * N_KV_HEADS + h // KV_REP]
        bmax = st_ref[bsz * N_KV_HEADS + h]
        bfar = st_ref[bsz * N_KV_HEADS + N_HEADS + h]
        bound = qn[h * tq:(h + 1) * tq, :] * (kmax * 1.01) + (bmax + 0.1)
        worst = jnp.maximum(worst, jnp.max(bound))
        q2_ref[h * tq:(h + 1) * tq, :] = jnp.where(lane == HEAD_DIM, bfar - bound,
                                                   qf[h * tq:(h + 1) * tq, :]).astype(BF16)
    n_far = jnp.clip((d0 - nd) // 2 + 1, 0, n_kt)

    def select_masks(kt0, nb):
        return [jnp.where(s_ref[kt0 + j] >= thr_b, 0.0, NEG_BIG) for j in range(nb)]

    def logits(kt0, nb, g, near, exact, masks):
        k0 = pl.multiple_of(kt0 * kb, kb)
        s = _dot_nt(k_ref[g, pl.ds(k0, nb * kb), :], q2_ref[g * rows_g:(g + 1) * rows_g, :])
        out = []
        for j in range(nb):
            kt = kt0 + j
            maskadd = masks[j]
            if near:
                da = jnp.clip(d0 - 2 * kt, 0, nd - 1)
                db = jnp.clip(d0 - 2 * kt - 1, 0, nd - 1)
            parts = []
            for r in range(KV_REP):
                h = g * KV_REP + r
                add = maskadd - mrow_ref[0:1, h * tq:(h + 1) * tq] if exact else maskadd
                if near:
                    add = jnp.concatenate([bias_ref[da, h], bias_ref[db, h]], axis=0) + add
                parts.append(s[j * kb:(j + 1) * kb, r * tq:(r + 1) * tq] + add)
            out.append(jnp.concatenate(parts, axis=1))
        return out

    def over_blocks(fn):
        def far_chunk(j, c):
            fn(ATTEND_CHUNK * j, ATTEND_CHUNK, False)
            return c
        lax.fori_loop(0, n_far // ATTEND_CHUNK, far_chunk, 0)

        def far_single(kt, c):
            fn(kt, 1, False)
            return c
        lax.fori_loop(n_far - n_far % ATTEND_CHUNK, n_far, far_single, 0)

        n_near = n_kt - n_far
        for nb in range(1, near_max + 1):
            @pl.when(n_near == nb)
            def _(nb=nb):
                fn(n_far, nb, True)

    def attend(exact):
        acc_ref[...] = jnp.zeros(acc_ref.shape, F32)

        def blocks(kt0, nb, near):
            masks = select_masks(kt0, nb)
            for g in range(N_KV_HEADS):
                p = jnp.concatenate([jnp.exp2(lg).astype(BF16) for lg in logits(kt0, nb, g, near, exact, masks)],
                                    axis=0)
                vt = jnp.concatenate([vt_ref[g, kt0 + j] for j in range(nb)], axis=1)
                acc_ref[g] += _dot(vt, p)
        over_blocks(blocks)

    @pl.when(worst <= SHIFT_LIMIT)
    def _():
        attend(False)

    @pl.when(worst > SHIFT_LIMIT)
    def _():
        mrow_ref[...] = jnp.full(mrow_ref.shape, NEG_BIG, F32)

        def blocks(kt0, nb, near):
            masks = select_masks(kt0, nb)
            for g in range(N_KV_HEADS):
                for lg in logits(kt0, nb, g, near, False, masks):
                    mx = jnp.max(lg, axis=0, keepdims=True)
                    cur = mrow_ref[:, g * rows_g:(g + 1) * rows_g]
                    mrow_ref[:, g * rows_g:(g + 1) * rows_g] = jnp.maximum(cur, jnp.broadcast_to(mx, (sl, rows_g)))
        over_blocks(blocks)
        attend(True)

    for g in range(N_KV_HEADS):
        acc = acc_ref[g]
        og = acc / acc[HEAD_DIM:HEAD_DIM + 1, :]
        for r in range(KV_REP):
            h = g * KV_REP + r
            o_ref[:, h * LANES:(h + 1) * LANES] = og[:, r * tq:(r + 1) * tq].T.astype(BF16)


def _dsa(stats, q, qi, wt, k_all, vt_all, ki_all, bias_tiles, bsz, seq, past, n_keys, tq):
    lk = k_all.shape[2]
    nkt = lk // KEY_BLOCK
    topk = min(TOPK_MAX, n_keys // 4)
    nd = bias_tiles.shape[0]
    nq = seq // tq
    assert past % KEY_TILE == 0 and tq == KEY_TILE
    idx_bits = int(math.ceil(math.log2(lk))) + 1
    near_max = 0
    for i in range(nq):
        q0 = past + i * tq
        n_kt = min(nkt, (((q0 + tq - 1) // CHUNK + 1) * CHUNK + KEY_BLOCK - 1) // KEY_BLOCK)
        n_far = min(max((q0 // KEY_TILE - nd) // 2 + 1, 0), n_kt)
        near_max = max(near_max, n_kt - n_far)
    kern = functools.partial(_dsa_kernel, bsz=bsz, tq=tq, past=past, n_keys=n_keys, topk=topk, nkt=nkt, nd=nd,
                             idx_bits=idx_bits, near_max=near_max)
    row_state = pltpu.VMEM((SUBLANES, tq), F32)
    grid_spec = pltpu.PrefetchScalarGridSpec(
        num_scalar_prefetch=1,
        grid=(bsz, nq),
        in_specs=[
            pl.BlockSpec((None, N_HEADS, tq, LANES), lambda b, i, s: (b, 0, i, 0)),
            pl.BlockSpec((None, IDX_HEADS, tq, LANES), lambda b, i, s: (b, 0, i, 0)),
            pl.BlockSpec((None, IDX_HEADS, tq), lambda b, i, s: (b, 0, i)),
            pl.BlockSpec((None, N_KV_HEADS, lk, LANES), lambda b, i, s: (b, 0, 0, 0)),
            pl.BlockSpec((None, N_KV_HEADS, nkt, LANES, KEY_BLOCK), lambda b, i, s: (b, 0, 0, 0, 0)),
            pl.BlockSpec((None, lk, LANES), lambda b, i, s: (b, 0, 0)),
            pl.BlockSpec(bias_tiles.shape, lambda b, i, s: (0, 0, 0, 0)),
        ],
        out_specs=pl.BlockSpec((None, tq, N_HEADS * LANES), lambda b, i, s: (b, i, 0)),
        scratch_shapes=[
            pltpu.VMEM((nkt + 1, KEY_BLOCK, tq), F32),
            row_state, row_state, row_state, row_state, row_state, row_state,
            pltpu.VMEM((N_HEADS * tq, LANES), BF16),
            pltpu.VMEM((SUBLANES, N_HEADS * tq), F32),
            pltpu.VMEM((N_KV_HEADS, LANES, KV_REP * tq), F32),
        ],
    )
    return pl.pallas_call(
        kern,
        grid_spec=grid_spec,
        out_shape=jax.ShapeDtypeStruct((bsz, seq, N_HEADS * LANES), BF16),
        compiler_params=_cp(("arbitrary", "arbitrary")),
        name="dsa",
    )(stats, q, qi, wt, k_all, vt_all, ki_all, bias_tiles)


def _merge_kernel(x_ref, ya_ref, at_ref, sga_ref, sgb_ref, wup_ref, wout_ref, g2_ref, wr_hi_ref, wr_lo_ref, br_ref,
                  x1_ref, h2_ref, gt_ref, rt_ref, cnt_ref, run_ref, *, tm, sub):
    step = pl.program_id(0)

    @pl.when(step % sub == 0)
    def _():
        run_ref[...] = jnp.zeros(run_ref.shape, F32)

    yb = _dot(at_ref[...], wup_ref[...])
    merged = sga_ref[...].astype(F32) * ya_ref[...].astype(F32) + sgb_ref[...].astype(F32) * yb
    x1 = x_ref[...] + _dot(merged.astype(BF16), wout_ref[...])
    x1_ref[...] = x1
    ms = jnp.mean(x1 * x1, axis=-1, keepdims=True)
    h2 = x1 * lax.rsqrt(ms + EPS) * g2_ref[...]
    h2_hi, h2_lo = _split(h2)
    h2_ref[...] = h2_hi

    wr_hi = wr_hi_ref[...]
    logit = (_dot_nt(wr_hi, h2_hi) + _dot_nt(wr_hi, h2_lo) + _dot_nt(wr_lo_ref[...], h2_hi)) + br_ref[:, 0:1]
    ne = logit.shape[0]
    eid = lax.broadcasted_iota(I32, (ne, tm), 0).astype(F32)
    selb = jnp.zeros((ne, tm), F32)
    tops = []
    picks = []
    for _ in range(TOP_K):
        mx = jnp.max(logit, axis=0, keepdims=True)
        pick = jnp.min(jnp.where(logit == mx, eid, float(ne)), axis=0, keepdims=True)
        hit = eid == pick
        selb = jnp.where(hit, 1.0, selb)
        logit = jnp.where(hit, -jnp.inf, logit)
        tops.append(mx)
        picks.append(hit)
    ex = [jnp.exp(t - tops[0]) for t in tops]
    den = ex[0] + ex[1] + ex[2] + ex[3]
    gate = jnp.zeros((ne, tm), F32)
    for hit, e in zip(picks, ex):
        gate = jnp.where(hit, e / den, gate)
    gt_ref[...] = gate

    sel = selb > 0.5
    selb = selb.astype(BF16)
    r_i = lax.broadcasted_iota(I32, (tm, tm), 0)
    c_i = lax.broadcasted_iota(I32, (tm, tm), 1)
    tri = jnp.where(r_i < c_i, 1.0, 0.0).astype(BF16)
    run = run_ref[...]
    rank = _dot(selb, tri) + jnp.broadcast_to(run[:, 0:1], (ne, tm))
    rt_ref[...] = jnp.where(sel, rank, -1.0)
    run = run + _dot(selb, jnp.ones((tm, LANES), BF16))
    run_ref[...] = run
    cnt_ref[...] = run


def _merge(x2, ya, attn, sga, sgb, mw, tm, moe_tile):
    t, d = x2.shape
    sub = moe_tile // tm
    ne = mw['wr_hi'].shape[0]

    def tok(i):
        return (i, 0)

    def cst(i):
        return (0, 0)

    consts = [mw['wup'], mw['wout'], mw['g2'], mw['wr_hi'], mw['wr_lo'], mw['br']]
    return pl.pallas_call(
        functools.partial(_merge_kernel, tm=tm, sub=sub),
        grid=(t // tm,),
        in_specs=[
            pl.BlockSpec((tm, d), tok),
            pl.BlockSpec((tm, d), tok),
            pl.BlockSpec((tm, attn.shape[-1]), tok),
            pl.BlockSpec((tm, d), tok),
            pl.BlockSpec((tm, d), tok),
        ] + [pl.BlockSpec(a.shape, cst) for a in consts],
        out_specs=(
            pl.BlockSpec((tm, d), tok),
            pl.BlockSpec((tm, d), tok),
            pl.BlockSpec((ne, tm), lambda i: (0, i)),
            pl.BlockSpec((ne, tm), lambda i: (0, i)),
            pl.BlockSpec((None, ne, LANES), lambda i: (i // sub, 0, 0)),
        ),
        out_shape=(
            jax.ShapeDtypeStruct((t, d), F32),
            jax.ShapeDtypeStruct((t, d), BF16),
            jax.ShapeDtypeStruct((ne, t), F32),
            jax.ShapeDtypeStruct((ne, t), F32),
            jax.ShapeDtypeStruct((t // moe_tile, ne, LANES), F32),
        ),
        scratch_shapes=[pltpu.VMEM((ne, LANES), F32)],
        compiler_params=_cp(("arbitrary",)),
        name="merge",
    )(x2, ya, attn, sga, sgb, *consts)


def _moe_kernel(cnt_ref, h2_ref, x1_hbm, gt_ref, rt_ref, wg_ref, wu_ref, wd_ref, bg_ref, bu_ref, bd_ref, y_ref,
                pg_ref, og_ref, sem, *, tt, pair):
    j = pl.program_id(0)
    e = pl.program_id(1)
    ne = pl.num_programs(1)
    rb = MOE_ROWS
    slot = e % MOE_GROUP

    @pl.when(e == 0)
    def _():
        cp = pltpu.make_async_copy(x1_hbm.at[pl.ds(pl.multiple_of(j * pair * tt, tt), pair * tt), :], y_ref, sem)
        cp.start()
        cp.wait()

    mine = lax.broadcasted_iota(I32, (SUBLANES, pair * tt), 0) == e % SUBLANES
    g_all = jnp.sum(jnp.where(mine, gt_ref[...], 0.0), axis=0, keepdims=True)
    r_all = jnp.sum(jnp.where(mine, rt_ref[...], 0.0), axis=0, keepdims=True)
    rid = lax.broadcasted_iota(I32, (rb, tt), 0).astype(F32)

    def one_hot(s, blk):
        return jnp.broadcast_to(r_all[:, s * tt:(s + 1) * tt], (rb, tt)) == (rid + (blk * rb).astype(F32))

    def gather(s, hit):
        p = jnp.where(hit, 1.0, 0.0).astype(BF16)
        return p, _dot(p, h2_ref[s * tt:(s + 1) * tt, :]).astype(BF16)

    def expert(xg):
        a = jnp.minimum(_dot(xg, wg_ref[0]) + bg_ref[0], SWIGLU_LIMIT)
        b = jnp.clip(_dot(xg, wu_ref[0]) + bu_ref[0], -SWIGLU_LIMIT, SWIGLU_LIMIT)
        hid = a * jax.nn.sigmoid(SWIGLU_ALPHA * a) * (b + 1.0)
        return _dot(hid.astype(BF16), wd_ref[0]) + bd_ref[0]

    def gated(s, hit, o):
        g_row = jnp.broadcast_to(g_all[:, s * tt:(s + 1) * tt], (rb, tt))
        return (o * jnp.sum(jnp.where(hit, g_row, 0.0), axis=1, keepdims=True)).astype(BF16)

    hits = [one_hot(s, jnp.int32(0)) for s in range(pair)]
    gathered = [gather(s, hits[s]) for s in range(pair)]
    o = expert(jnp.concatenate([xg for _, xg in gathered], axis=0))
    r0 = pl.multiple_of(slot * rb, rb)
    for s in range(pair):
        pg_ref[s, pl.ds(r0, rb), :] = gathered[s][0]
        og_ref[s, pl.ds(r0, rb), :] = gated(s, hits[s], o[s * rb:(s + 1) * rb])

    @pl.when(slot == MOE_GROUP - 1)
    def _():
        for s in range(pair):
            y_ref[s * tt:(s + 1) * tt, :] += _dot_tn(pg_ref[s], og_ref[s])

    for s in range(pair):
        def overflow(blk, c, s=s):
            hit = one_hot(s, blk)
            p, xg = gather(s, hit)
            y_ref[s * tt:(s + 1) * tt, :] += _dot_tn(p, gated(s, hit, expert(xg)))
            return c

        n_rows = cnt_ref[(j * pair + s) * ne + e]
        lax.fori_loop(1, (n_rows + rb - 1) // rb, overflow, 0)


def _moe(h2, x1, gt, rt, cnt, ew, tt):
    t, d = h2.shape
    ne = gt.shape[0]
    nt = t // tt
    f = ew['wg'].shape[-1]
    pair = MOE_PAIR if nt % MOE_PAIR == 0 else 1
    grid_spec = pltpu.PrefetchScalarGridSpec(
        num_scalar_prefetch=1,
        grid=(nt // pair, ne),
        in_specs=[
            pl.BlockSpec((pair * tt, d), lambda j, e, c: (j, 0)),
            pl.BlockSpec(memory_space=pl.ANY),
            pl.BlockSpec((SUBLANES, pair * tt), lambda j, e, c: (e // SUBLANES, j)),
            pl.BlockSpec((SUBLANES, pair * tt), lambda j, e, c: (e // SUBLANES, j)),
            pl.BlockSpec((1, d, f), lambda j, e, c: (e, 0, 0)),
            pl.BlockSpec((1, d, f), lambda j, e, c: (e, 0, 0)),
            pl.BlockSpec((1, f, d), lambda j, e, c: (e, 0, 0)),
            pl.BlockSpec((1, 1, f), lambda j, e, c: (e, 0, 0)),
            pl.BlockSpec((1, 1, f), lambda j, e, c: (e, 0, 0)),
            pl.BlockSpec((1, 1, d), lambda j, e, c: (e, 0, 0)),
        ],
        out_specs=pl.BlockSpec((pair * tt, d), lambda j, e, c: (j, 0)),
        scratch_shapes=[pltpu.VMEM((pair, MOE_GROUP * MOE_ROWS, tt), BF16),
                        pltpu.VMEM((pair, MOE_GROUP * MOE_ROWS, d), BF16),
                        pltpu.SemaphoreType.DMA(())],
    )
    assert ne % MOE_GROUP == 0
    return pl.pallas_call(
        functools.partial(_moe_kernel, tt=tt, pair=pair),
        grid_spec=grid_spec,
        out_shape=jax.ShapeDtypeStruct((t, d), F32),
        compiler_params=_cp(("arbitrary", "arbitrary")),
        name="moe",
    )(cnt, h2, x1, gt, rt, ew['wg'], ew['wu'], ew['wd'], ew['bg'], ew['bu'], ew['bd'])


def _pad_heads(wmat, n_heads, width):
    d = wmat.shape[0]
    w3 = wmat.reshape(d, n_heads, width)
    return jnp.pad(w3, ((0, 0), (0, 0), (0, LANES - width))).reshape(d, n_heads * LANES)


def _pad_lanes(v, width=LANES):
    v = v.reshape(1, -1)
    return jnp.pad(v, ((0, 0), (0, width - v.shape[1])))


def _rel_bucket(rel):
    half = REL_BUCKETS // 2
    max_exact = half // 2
    n = jnp.abs(rel)
    large = max_exact + (jnp.log(jnp.maximum(n, 1).astype(jnp.float32) / max_exact)
                         / math.log(REL_MAX_DIST / max_exact) * (half - max_exact)).astype(jnp.int32)
    large = jnp.minimum(large, half - 1)
    return jnp.where(rel > 0, half, 0) + jnp.where(n < max_exact, n, large)


def _bias_tiles(rel_bias):
    tk = KEY_TILE
    half = REL_BUCKETS // 2
    max_exact = half // 2
    n_sat = int(math.ceil(max_exact * (REL_MAX_DIST / max_exact) ** ((half - 1 - max_exact) / (half - max_exact)))) + 2
    nd = (n_sat + 2 * tk - 2) // tk + 1
    dd = jnp.arange(nd, dtype=I32)[:, None, None]
    c = jnp.arange(tk, dtype=I32)[None, :, None]
    r = jnp.arange(tk, dtype=I32)[None, None, :]
    bucket = _rel_bucket(c - r - dd * tk)
    onehot = (bucket[..., None] == jnp.arange(REL_BUCKETS, dtype=I32)).astype(F32)
    tiles = jnp.einsum('dcrb,bh->dhcr', onehot, rel_bias.astype(F32) * LOG2E,
                       precision=lax.Precision.HIGHEST)
    return tiles


def _prep_proj(norm1_g, w_in, q_norm_g, k_norm_g, idx_k_norm_g, idx_k_norm_b, d_model):
    ssm_w = d_model // 2
    attn_w = N_HEADS * HEAD_DIM
    kv = N_KV_HEADS * HEAD_DIM
    sizes = [ssm_w, attn_w, kv, kv, IDX_HEADS * IDX_DIM, IDX_DIM, IDX_HEADS, d_model, d_model]
    pts = np.cumsum(sizes)[:-1].tolist()
    wu, wq, wk, wv, wqi, wki, wwi, wga, wgb = jnp.split(w_in, pts, axis=1)
    bf = lambda a: a.astype(BF16)
    blk = np.kron(np.eye(N_KV_HEADS), np.ones((HEAD_DIM, HEAD_DIM))) / HEAD_DIM
    wwit = jnp.pad(wwi.T, ((0, 2 * SUBLANES - IDX_HEADS), (0, 0)))
    return dict(
        g1=norm1_g.reshape(1, -1).astype(F32),
        wu=bf(wu), wq=bf(_pad_heads(wq, N_HEADS, HEAD_DIM)), wk=bf(_pad_heads(wk, N_KV_HEADS, HEAD_DIM)),
        wv=bf(_pad_heads(wv, N_KV_HEADS, HEAD_DIM)), wqi=bf(_pad_heads(wqi, IDX_HEADS, IDX_DIM)),
        wki=bf(_pad_heads(wki, 1, IDX_DIM)), wwit=bf(wwit),
        wga=bf(wga), wgb=bf(wgb), wkc=bf(wk), wvc=bf(wv),
        gq=_pad_lanes(q_norm_g.astype(F32)), gk=_pad_lanes(k_norm_g.astype(F32)),
        gkc=jnp.tile(k_norm_g.astype(F32), N_KV_HEADS).reshape(1, -1),
        gi=_pad_lanes(idx_k_norm_g.astype(F32)), bi=_pad_lanes(idx_k_norm_b.astype(F32)),
        ones_h=jnp.full((LANES, LANES), 1.0 / HEAD_DIM, BF16),
        ones_c=jnp.asarray(blk, BF16),
    )


def _prep_s5(lre, lim, log_dt, b_re, b_im, c_re, c_im, dvec, wa, wb):
    g, p = lre.shape
    ch = b_re.shape[-1]
    lam = lax.complex(lre.astype(F32), lim.astype(F32))
    dt = jnp.exp(log_dt.astype(F32))[:, None]
    a_bar = jnp.exp(lam * dt)
    b_bar = ((a_bar - 1.0) / lam)[:, :, None] * lax.complex(b_re.astype(F32), b_im.astype(F32))
    gs = g // S5_DIAG
    eye = jnp.eye(gs, dtype=F32)

    def blocks_in(m):
        return jnp.einsum('jgpc,gh->jgchp', m.reshape(S5_DIAG, gs, p, ch), eye).reshape(S5_DIAG, gs * ch, gs * p)

    def blocks_out(m):
        return jnp.einsum('jgcp,gh->jgphc', m.reshape(S5_DIAG, gs, ch, p), eye).reshape(S5_DIAG, gs * p, gs * ch)

    return dict(
        b_re=blocks_in(jnp.real(b_bar)).astype(BF16), b_im=blocks_in(jnp.imag(b_bar)).astype(BF16),
        c_re=blocks_out(c_re.astype(F32)).astype(BF16), c_im=blocks_out(-c_im.astype(F32)).astype(BF16),
        a_re=jnp.real(a_bar).reshape(1, g * p), a_im=jnp.imag(a_bar).reshape(1, g * p),
        d=dvec.reshape(1, -1).astype(F32), wa=wa.astype(BF16), wb=wb.astype(BF16),
    )


def _prep_merge(w_attn_up, w_out, norm2_g, w_router, b_router):
    d = w_attn_up.shape[1]
    wup = jnp.pad(w_attn_up.reshape(N_HEADS, HEAD_DIM, d), ((0, 0), (0, LANES - HEAD_DIM), (0, 0)))
    wr_t = w_router.astype(F32).T
    wr_hi = wr_t.astype(BF16)
    wr_lo = (wr_t - wr_hi.astype(F32)).astype(BF16)
    return dict(
        wup=wup.reshape(N_HEADS * LANES, d).astype(BF16), wout=w_out.astype(BF16),
        g2=norm2_g.reshape(1, -1).astype(F32), wr_hi=wr_hi, wr_lo=wr_lo,
        br=jnp.broadcast_to(b_router.astype(F32)[:, None], (b_router.shape[0], LANES)),
    )


def _prep_moe(wg, bg, wu, bu, wd, bd):
    return dict(wg=wg.astype(BF16), wu=wu.astype(BF16), wd=wd.astype(BF16),
                bg=bg.astype(F32)[:, None, :], bu=bu.astype(F32)[:, None, :], bd=bd.astype(F32)[:, None, :])


def _pick_tile(n, pref):
    t = min(n, pref)
    while n % t:
        t //= 2
    return t


def _pad_axis(a, axis, size):
    pad = [(0, 0)] * a.ndim
    pad[axis] = (0, size - a.shape[axis])
    return jnp.pad(a, pad)


def _trunk_layer(x, past_k, past_v, past_ik, h0_re, h0_im, pw, sw, mw, ew, bias_tiles):
    bsz, seq, d = x.shape
    t = bsz * seq
    tm = _pick_tile(seq, 512)
    u_tb, q, kp, vp, qi, kip, wt, sga, sgb, kc, vc, kic = _proj(x, pw, bsz, seq, tm)

    half = sw['a_re'].shape[1]
    if h0_re is None:
        h0 = jnp.zeros((bsz, 2 * half), F32)
    else:
        h0 = jnp.concatenate([h0_re.reshape(bsz, half), h0_im.reshape(bsz, half)], axis=1).astype(F32)
    tc = _pick_tile(seq, max(1, S5_ROWS // bsz))
    ya, hout = _s5(u_tb, h0, sw, bsz, seq, tc)
    groups = half // SSM_STATE
    s_re = hout[:, :half].reshape(bsz, groups, SSM_STATE)
    s_im = hout[:, half:].reshape(bsz, groups, SSM_STATE)

    past = 0 if past_k is None else past_k.shape[1]
    n_keys = past + seq
    lk = -(-n_keys // KEY_BLOCK) * KEY_BLOCK
    kip3 = kip.reshape(bsz, seq, LANES)
    if past:
        lane = jnp.arange(LANES)
        pk = jnp.pad(past_k.astype(F32), ((0, 0), (0, 0), (0, 0), (0, LANES - HEAD_DIM)))
        pk = jnp.where(lane == HEAD_DIM, 1.0, pk).astype(BF16)
        pv = jnp.pad(past_v.astype(F32), ((0, 0), (0, 0), (0, 0), (0, LANES - HEAD_DIM)))
        pv = jnp.where(lane == HEAD_DIM, 1.0, pv).astype(BF16)
        pik = jnp.pad(past_ik.astype(F32), ((0, 0), (0, 0), (0, LANES - IDX_DIM))).astype(BF16)
        k_all = jnp.concatenate([pk.transpose(0, 2, 1, 3), kp], axis=2)
        v_all = jnp.concatenate([pv.transpose(0, 2, 1, 3), vp], axis=2)
        ki_all = jnp.concatenate([pik, kip3], axis=1)
    else:
        k_all, v_all, ki_all = kp, vp, kip3
    k_all = _pad_axis(k_all, 2, lk)
    v_all = _pad_axis(v_all, 2, lk)
    ki_all = _pad_axis(ki_all, 1, lk)
    vt_all = v_all.reshape(bsz, N_KV_HEADS, lk // KEY_BLOCK, KEY_BLOCK, LANES).transpose(0, 1, 2, 4, 3)
    tq = KEY_TILE
    seq_q = -(-seq // tq) * tq
    q_p, qi_p, wt_p = _pad_axis(q, 2, seq_q), _pad_axis(qi, 2, seq_q), _pad_axis(wt, 2, seq_q)
    kf = k_all[..., :HEAD_DIM].astype(F32)
    kmax = jnp.sqrt(jnp.max(jnp.sum(kf * kf, axis=-1), axis=-1)).reshape(-1)
    bfar = bias_tiles[-1, :, 0, 0]
    bmax = jnp.max(jnp.abs(bias_tiles), axis=(0, 2, 3))
    stats = jnp.concatenate([kmax, bmax, bfar]).astype(F32)
    bias_tiles = bias_tiles - bfar[None, :, None, None]
    attn = _dsa(stats, q_p, qi_p, wt_p, k_all, vt_all, ki_all, bias_tiles, bsz, seq_q, past, n_keys, tq)
    attn = attn[:, :seq]

    moe_tile = _pick_tile(t, MOE_TILE)
    tm2 = _pick_tile(moe_tile, 512)
    x1, h2, gt, rt, cnt = _merge(x.reshape(t, d), ya, attn.reshape(t, attn.shape[-1]), sga, sgb, mw, tm2, moe_tile)
    cnt_i = cnt[:, :, 0].astype(I32).reshape(-1)
    y = _moe(h2, x1, gt, rt, cnt_i, ew, moe_tile)

    k_new = kc.reshape(bsz, seq, N_KV_HEADS, HEAD_DIM)
    v_new = vc.reshape(bsz, seq, N_KV_HEADS, HEAD_DIM)
    ik_new = kic.reshape(bsz, seq, IDX_DIM)
    return y.reshape(bsz, seq, d), k_new, v_new, ik_new, s_re, s_im


def kernel(x_prompt, x_sample, cache_k, cache_v, cache_idx_k, state_ssm_re, state_ssm_im, rel_bias, norm1_g, w_in, ssm_lambda_re, ssm_lambda_im, ssm_log_dt, ssm_b_re, ssm_b_im, ssm_c_re, ssm_c_im, ssm_d, ssm_w_glu_a, ssm_w_glu_b, q_norm_g, k_norm_g, idx_k_norm_g, idx_k_norm_b, w_attn_up, w_out, norm2_g, moe_w_router, moe_b_router, moe_w_gate, moe_b_gate, moe_w_up, moe_b_up, moe_w_down, moe_b_down):
    depth = w_in.shape[0]
    d_model = x_prompt.shape[-1]
    bias_tiles = _bias_tiles(rel_bias)
    xp, xs = x_prompt, x_sample
    st_p, st_s = [], []
    for l in range(depth):
        pw = _prep_proj(norm1_g[l], w_in[l], q_norm_g[l], k_norm_g[l], idx_k_norm_g[l], idx_k_norm_b[l], d_model)
        sw = _prep_s5(ssm_lambda_re[l], ssm_lambda_im[l], ssm_log_dt[l], ssm_b_re[l], ssm_b_im[l], ssm_c_re[l],
                      ssm_c_im[l], ssm_d[l], ssm_w_glu_a[l], ssm_w_glu_b[l])
        mw = _prep_merge(w_attn_up[l], w_out[l], norm2_g[l], moe_w_router[l], moe_b_router[l])
        ew = _prep_moe(moe_w_gate[l], moe_b_gate[l], moe_w_up[l], moe_b_up[l], moe_w_down[l], moe_b_down[l])
        xp, *sp = _trunk_layer(xp, None, None, None, None, None, pw, sw, mw, ew, bias_tiles)
        xs, *ss = _trunk_layer(xs, cache_k[l], cache_v[l], cache_idx_k[l], state_ssm_re[l], state_ssm_im[l],
                               pw, sw, mw, ew, bias_tiles)
        st_p.append(sp)
        st_s.append(ss)
    outs_p = [jnp.stack([s[i] for s in st_p]) for i in range(5)]
    outs_s = [jnp.stack([s[i] for s in st_s]) for i in range(5)]
    return (xp, xs, *outs_p, *outs_s)
```

```python
import functools
import math

import numpy as np
import jax
import jax.numpy as jnp
from jax import lax
from jax.experimental import pallas as pl
from jax.experimental.pallas import tpu as pltpu

F32 = jnp.float32
BF16 = jnp.bfloat16
I32 = jnp.int32

LANES = 128
SUBLANES = 8
VMEM_LIMIT = 56 * 1024 * 1024

CHUNK = 64
SSM_GROUP_CH = 16
SSM_STATE = 64
N_HEADS = 8
HEAD_DIM = 64
N_KV_HEADS = 2
KV_REP = N_HEADS // N_KV_HEADS
IDX_HEADS = 8
IDX_DIM = 64
TOPK_MAX = 256
REL_BUCKETS = 32
REL_MAX_DIST = 1024
N_EXPERTS = 32
TOP_K = 4
SWIGLU_LIMIT = 7.0
SWIGLU_ALPHA = 1.702
EPS = 1e-6

KEY_TILE = 128
KEY_BLOCK = 256
SCORE_CHUNK = 4
ATTEND_CHUNK = 4
LOG2E = math.log2(math.e)
NEG_BIG = -1e30
SHIFT_LIMIT = 30.0
S5_ROWS = 1024
S5_DIAG = 2
SEARCH_FIRST = 8
SEARCH_GROUP = 4
MOE_TILE = 1024
MOE_ROWS = 160
MOE_GROUP = 8
MOE_PAIR = 2


def _cp(sem):
    return pltpu.CompilerParams(dimension_semantics=sem, vmem_limit_bytes=VMEM_LIMIT)


def _dot(a, b):
    return jnp.dot(a, b, preferred_element_type=F32)


def _dot_nt(a, b):
    return lax.dot_general(a, b, (((1,), (1,)), ((), ())), preferred_element_type=F32)


def _dot_tn(a, b):
    return lax.dot_general(a, b, (((0,), (0,)), ((), ())), preferred_element_type=F32)


def _split(a):
    hi = a.astype(BF16)
    lo = (a - hi.astype(F32)).astype(BF16)
    return hi, lo


def _dot_split(a, g):
    hi, lo = _split(a)
    return _dot(hi, g) + _dot(lo, g)


def _proj_kernel(x_ref, g1_ref, wu_ref, wq_ref, wk_ref, wv_ref, wqi_ref, wki_ref, wwit_ref, wga_ref, wgb_ref,
                 gq_ref, gk_ref, gi_ref, bi_ref, ones_h_ref,
                 u_ref, q_ref, kp_ref, vp_ref, qi_ref, kip_ref, wt_ref, sga_ref, sgb_ref,
                 kc_ref, vc_ref, kic_ref):
    x = x_ref[...]
    ms = jnp.mean(x * x, axis=-1, keepdims=True)
    hn = (x * lax.rsqrt(ms + EPS) * g1_ref[...]).astype(BF16)
    ones_h = ones_h_ref[...]
    lane = lax.broadcasted_iota(I32, (x.shape[0], LANES), 1)

    u_ref[...] = _dot(hn, wu_ref[...]).astype(BF16)

    q = _dot(hn, wq_ref[...])
    scale = HEAD_DIM ** -0.5 * LOG2E
    for h in range(N_HEADS):
        qh = q[:, h * LANES:(h + 1) * LANES]
        msq = _dot_split(qh * qh, ones_h)
        q_ref[h] = (qh * lax.rsqrt(msq + EPS) * (gq_ref[...] * scale)).astype(BF16)

    k = _dot(hn, wk_ref[...])
    for g in range(N_KV_HEADS):
        kg = k[:, g * LANES:(g + 1) * LANES]
        msk = _dot_split(kg * kg, ones_h)
        kn = kg * lax.rsqrt(msk + EPS) * gk_ref[...]
        kp_ref[g] = jnp.where(lane == HEAD_DIM, 1.0, kn).astype(BF16)
        kc_ref[:, g, :] = kn[:, :HEAD_DIM]

    v = _dot(hn, wv_ref[...])
    for g in range(N_KV_HEADS):
        vg = v[:, g * LANES:(g + 1) * LANES]
        vp_ref[g] = jnp.where(lane == HEAD_DIM, 1.0, vg).astype(BF16)
        vc_ref[:, g, :] = vg[:, :HEAD_DIM]

    qi = _dot(hn, wqi_ref[...])
    for h in range(IDX_HEADS):
        qi_ref[h] = qi[:, h * LANES:(h + 1) * LANES].astype(BF16)

    ki = _dot(hn, wki_ref[...])
    mu = _dot_split(ki, ones_h)
    xc = jnp.where(lane < IDX_DIM, ki - mu, 0.0)
    var = _dot_split(xc * xc, ones_h)
    kin = xc * lax.rsqrt(var + EPS) * gi_ref[...] + bi_ref[...]
    kip_ref[...] = kin.astype(BF16)
    kic_ref[...] = kin[:, :IDX_DIM]

    wt = _dot_nt(wwit_ref[...], hn)
    wt_ref[...] = wt[0:IDX_HEADS, :] * (IDX_HEADS ** -0.5 * IDX_DIM ** -0.5)

    sga_ref[...] = jax.nn.sigmoid(_dot(hn, wga_ref[...])).astype(BF16)
    sgb_ref[...] = jax.nn.sigmoid(_dot(hn, wgb_ref[...])).astype(BF16)


def _proj(x, pw, bsz, seq, tm):
    d = x.shape[-1]
    nt = seq // tm
    t = bsz * seq
    x2 = x.reshape(t, d)

    def tok(b, i):
        return (b * nt + i, 0)

    def cst(b, i):
        return (0, 0)

    def wspec(a):
        return pl.BlockSpec(a.shape, cst)

    weights = [pw['g1'], pw['wu'], pw['wq'], pw['wk'], pw['wv'], pw['wqi'], pw['wki'], pw['wwit'], pw['wga'],
               pw['wgb'], pw['gq'], pw['gk'], pw['gi'], pw['bi'], pw['ones_h']]
    ssm_w = pw['wu'].shape[1]
    out_shape = (
        jax.ShapeDtypeStruct((t, ssm_w), BF16),
        jax.ShapeDtypeStruct((bsz, N_HEADS, seq, LANES), BF16),
        jax.ShapeDtypeStruct((bsz, N_KV_HEADS, seq, LANES), BF16),
        jax.ShapeDtypeStruct((bsz, N_KV_HEADS, seq, LANES), BF16),
        jax.ShapeDtypeStruct((bsz, IDX_HEADS, seq, LANES), BF16),
        jax.ShapeDtypeStruct((t, LANES), BF16),
        jax.ShapeDtypeStruct((bsz, IDX_HEADS, seq), F32),
        jax.ShapeDtypeStruct((t, d), BF16),
        jax.ShapeDtypeStruct((t, d), BF16),
        jax.ShapeDtypeStruct((t, N_KV_HEADS, HEAD_DIM), F32),
        jax.ShapeDtypeStruct((t, N_KV_HEADS, HEAD_DIM), F32),
        jax.ShapeDtypeStruct((t, IDX_DIM), F32),
    )

    def hm(nh):
        return pl.BlockSpec((None, nh, tm, LANES), lambda b, i: (b, 0, i, 0))

    out_specs = (
        pl.BlockSpec((tm, ssm_w), tok),
        hm(N_HEADS), hm(N_KV_HEADS), hm(N_KV_HEADS), hm(IDX_HEADS),
        pl.BlockSpec((tm, LANES), tok),
        pl.BlockSpec((None, IDX_HEADS, tm), lambda b, i: (b, 0, i)),
        pl.BlockSpec((tm, d), tok), pl.BlockSpec((tm, d), tok),
        pl.BlockSpec((tm, N_KV_HEADS, HEAD_DIM), lambda b, i: (b * nt + i, 0, 0)),
        pl.BlockSpec((tm, N_KV_HEADS, HEAD_DIM), lambda b, i: (b * nt + i, 0, 0)),
        pl.BlockSpec((tm, IDX_DIM), tok),
    )
    return pl.pallas_call(
        _proj_kernel,
        grid=(bsz, nt),
        in_specs=[pl.BlockSpec((tm, d), tok)] + [wspec(a) for a in weights],
        out_specs=out_specs,
        out_shape=out_shape,
        compiler_params=_cp(("arbitrary", "arbitrary")),
        name="proj",
    )(x2, *weights)


def _gelu_tanh(x):
    return 0.5 * x * (1.0 + jnp.tanh(math.sqrt(2.0 / math.pi) * (x + 0.044715 * (x * x * x))))


def _s5_kernel(u_ref, h0_ref, bre_ref, bim_ref, are_ref, aim_ref, cre_ref, cim_ref, dvec_ref, wa_ref, wb_ref,
               ya_ref, hout_ref, state_ref, bu_ref, yf_ref, ug_ref, *, bsz, tc, strip):
    s = pl.program_id(0)
    half = are_ref.shape[1]

    @pl.when(s == 0)
    def _():
        state_ref[...] = h0_ref[...]

    ssm_w = dvec_ref.shape[1]
    for b in range(bsz):
        for c in range(ssm_w // LANES):
            ug_ref[c, pl.ds(b, tc, stride=bsz), :] = u_ref[b, :, c * LANES:(c + 1) * LANES].astype(F32)
    u = jnp.concatenate([ug_ref[c] for c in range(ssm_w // LANES)], axis=1).astype(BF16)
    cw = u.shape[1] // S5_DIAG
    sw = half // S5_DIAG
    for j in range(S5_DIAG):
        uj = u[:, j * cw:(j + 1) * cw]
        bu_ref[:, j * sw:(j + 1) * sw] = _dot(uj, bre_ref[j])
        bu_ref[:, half + j * sw:half + (j + 1) * sw] = _dot(uj, bim_ref[j])

    for c0 in range(0, half, strip):
        ar = jnp.broadcast_to(are_ref[:, c0:c0 + strip], (bsz, strip))
        ai = jnp.broadcast_to(aim_ref[:, c0:c0 + strip], (bsz, strip))
        hr0 = state_ref[:, c0:c0 + strip]
        hi0 = state_ref[:, half + c0:half + c0 + strip]

        def step(t, carry):
            hr, hi = carry
            r0 = pl.multiple_of(t * bsz, bsz)
            br = bu_ref[pl.ds(r0, bsz), c0:c0 + strip]
            bi = bu_ref[pl.ds(r0, bsz), half + c0:half + c0 + strip]
            nr = ar * hr - ai * hi + br
            ni = ar * hi + ai * hr + bi
            bu_ref[pl.ds(r0, bsz), c0:c0 + strip] = nr
            bu_ref[pl.ds(r0, bsz), half + c0:half + c0 + strip] = ni
            return nr, ni

        hr, hi = lax.fori_loop(0, tc, step, (hr0, hi0))
        state_ref[:, c0:c0 + strip] = hr
        state_ref[:, half + c0:half + c0 + strip] = hi

    ys = []
    for j in range(S5_DIAG):
        s_re = bu_ref[:, j * sw:(j + 1) * sw].astype(BF16)
        s_im = bu_ref[:, half + j * sw:half + (j + 1) * sw].astype(BF16)
        ys.append(_dot(s_re, cre_ref[j]) + _dot(s_im, cim_ref[j]))
    y = jnp.concatenate(ys, axis=1) + dvec_ref[...] * u.astype(F32)
    g = _gelu_tanh(y).astype(BF16)
    ya = _dot(g, wa_ref[...]) * jax.nn.sigmoid(_dot(g, wb_ref[...]))
    n_chunk = ya.shape[1] // LANES
    for c in range(n_chunk):
        yf_ref[c] = ya[:, c * LANES:(c + 1) * LANES]
    for b in range(bsz):
        ya_ref[b] = jnp.concatenate([yf_ref[c, pl.ds(b, tc, stride=bsz), :] for c in range(n_chunk)],
                                    axis=1).astype(BF16)

    @pl.when(s == pl.num_programs(0) - 1)
    def _():
        hout_ref[...] = state_ref[...]


def _s5(u, h0, sw, bsz, seq, tc):
    rows = tc * bsz
    ssm_w = sw['d'].shape[1]
    half = sw['a_re'].shape[1]
    two_half = 2 * half
    d = sw['wa'].shape[1]
    u3 = u.reshape(bsz, seq, ssm_w)
    strip = min(512, half)

    consts = [h0, sw['b_re'], sw['b_im'], sw['a_re'], sw['a_im'], sw['c_re'], sw['c_im'], sw['d'], sw['wa'], sw['wb']]

    def cst(s):
        return (0, 0)

    def cspec(a):
        return pl.BlockSpec(a.shape, lambda s: (0,) * a.ndim)

    ya, hout = pl.pallas_call(
        functools.partial(_s5_kernel, bsz=bsz, tc=tc, strip=strip),
        grid=(seq // tc,),
        in_specs=[pl.BlockSpec((bsz, tc, ssm_w), lambda s: (0, s, 0))] + [cspec(a) for a in consts],
        out_specs=(pl.BlockSpec((bsz, tc, d), lambda s: (0, s, 0)), pl.BlockSpec((bsz, two_half), cst)),
        out_shape=(jax.ShapeDtypeStruct((bsz, seq, d), BF16), jax.ShapeDtypeStruct((bsz, two_half), F32)),
        scratch_shapes=[pltpu.VMEM((bsz, two_half), F32), pltpu.VMEM((rows, two_half), F32),
                        pltpu.VMEM((d // LANES, rows, LANES), F32), pltpu.VMEM((ssm_w // LANES, rows, LANES), F32)],
        compiler_params=_cp(("arbitrary",)),
        name="s5",
    )(u3, *consts)
    return ya.reshape(bsz * seq, d), hout


def _f2key(x):
    b = lax.bitcast_convert_type(x, I32)
    return b ^ ((b >> 31) & 0x7FFFFFFF)


def _key2f(k):
    return lax.bitcast_convert_type(k ^ ((k >> 31) & 0x7FFFFFFF), F32)


def _dsa_kernel(st_ref, q_ref, qi_ref, wt_ref, k_ref, vt_ref, ki_ref, bias_ref, o_ref,
                s_ref, lo_ref, hi_ref, clo_ref, glo_ref, ghi_ref, side_ref, q2_ref, mrow_ref, acc_ref,
                *, bsz, tq, past, n_keys, topk, nkt, nd, idx_bits, near_max):
    b_id = pl.program_id(0)
    i = pl.program_id(1)
    kb = KEY_BLOCK
    sl = SUBLANES
    q0 = past + i * tq
    last_chunk = (q0 + tq - 1) // CHUNK
    n_kt = jnp.minimum(nkt, ((last_chunk + 1) * CHUNK + kb - 1) // kb)
    d0 = q0 // KEY_TILE

    krow = lax.broadcasted_iota(I32, (kb, tq), 0)
    q_chunk = (q0 + lax.broadcasted_iota(I32, (kb, tq), 1)) // CHUNK
    qc8 = (q0 + lax.broadcasted_iota(I32, (sl, tq), 1)) // CHUNK
    n_adm = jnp.minimum((qc8 + 1) * CHUNK, n_keys)
    n_admf = n_adm.astype(F32)
    is_pad = q0 + lax.broadcasted_iota(I32, (sl, tq), 1) >= n_keys
    needf = jnp.where(is_pad, n_adm, jnp.minimum(topk, n_adm)).astype(F32)

    def bcast(x):
        return jnp.broadcast_to(x[0:1, :], (kb, tq))

    def rep(x):
        return jnp.broadcast_to(x, (sl, tq))

    qi = qi_ref[...].reshape(IDX_HEADS * tq, LANES)

    def score_blocks(kt, nb, masked):
        k0 = pl.multiple_of(kt * kb, kb)
        s = _dot_nt(ki_ref[pl.ds(k0, nb * kb), :], qi)
        for j in range(nb):
            sc = jnp.zeros((kb, tq), F32)
            for h in range(IDX_HEADS):
                sc = sc + wt_ref[h:h + 1, :] * jnp.maximum(s[j * kb:(j + 1) * kb, h * tq:(h + 1) * tq], 0.0)
            if masked:
                kpos = k0 + j * kb + krow
                adm = ((kpos // CHUNK) <= q_chunk) & (kpos < n_keys)
                sc = jnp.where(adm, sc, -jnp.inf)
            s_ref[kt + j] = sc

    n_open = n_kt - 1

    def score_chunk(j, c):
        score_blocks(SCORE_CHUNK * j, SCORE_CHUNK, False)
        return c

    lax.fori_loop(0, n_open // SCORE_CHUNK, score_chunk, 0)

    def score_single(j, c):
        score_blocks(j, 1, False)
        return c

    lax.fori_loop(n_open - n_open % SCORE_CHUNK, n_open, score_single, 0)
    score_blocks(n_kt - 1, 1, True)
    s_ref[n_kt] = jnp.full((kb, tq), -jnp.inf, F32)
    n_pair = (n_kt + 1) // 2

    part = 4 * sl

    def fold(x, op):
        x = x.reshape(kb // part, part, tq)
        acc = x[0]
        for j in range(1, kb // part):
            acc = op(acc, x[j])
        return acc

    def count(pred):
        def one(kt):
            return fold(jnp.where(pred(s_ref[kt], kt), 1.0, 0.0), jnp.add)
        c = lax.fori_loop(0, n_pair, lambda j, c: c + (one(2 * j) + one(2 * j + 1)), jnp.zeros((part, tq), F32))
        return rep(jnp.sum(c, axis=0, keepdims=True))

    def minmax(j, c):
        mx, mn = c
        for kt in (2 * j, 2 * j + 1):
            s = s_ref[kt]
            mx = jnp.maximum(mx, fold(s, jnp.maximum))
            mn = jnp.minimum(mn, fold(jnp.where(s == -jnp.inf, jnp.inf, s), jnp.minimum))
        return mx, mn

    mx, mn = lax.fori_loop(0, n_pair, minmax,
                           (jnp.full((part, tq), -jnp.inf, F32), jnp.full((part, tq), jnp.inf, F32)))
    lo_ref[...] = rep(jnp.min(mn, axis=0, keepdims=True))
    hi_ref[...] = _key2f(_f2key(rep(jnp.max(mx, axis=0, keepdims=True))) + 1)
    def odds(cnt):
        c = jnp.clip(cnt, 0.5, n_admf - 0.5)
        return jnp.log((n_admf - c) / c)

    target = odds(needf - 0.5)
    clo_ref[...] = n_admf
    glo_ref[...] = target - odds(n_admf)
    ghi_ref[...] = target - odds(jnp.zeros((sl, tq), F32))
    side_ref[...] = jnp.zeros((sl, tq), F32)

    def searching(lo, hi, clo):
        return (_f2key(hi) > _f2key(lo) + 1) & (clo > needf)

    def refine(it, c):
        lo, hi, clo = lo_ref[...], hi_ref[...], clo_ref[...]
        glo, ghi, side = glo_ref[...], ghi_ref[...], side_ref[...]
        k_t = _f2key(lo + (hi - lo) * (glo / (glo - ghi)))
        t = _key2f(jnp.minimum(jnp.maximum(k_t, _f2key(lo) + 1), _f2key(hi) - 1))
        tb = bcast(t)
        cnt = count(lambda s, kt: s >= tb)
        g = target - odds(cnt)
        open_ = searching(lo, hi, clo)
        up = open_ & (cnt >= needf)
        dn = open_ & (cnt < needf)
        lo_ref[...] = jnp.where(up, t, lo)
        clo_ref[...] = jnp.where(up, cnt, clo)
        hi_ref[...] = jnp.where(dn, t, hi)
        glo_ref[...] = jnp.where(up, g, jnp.where(dn & (side < 0.0), glo * 0.5, glo))
        ghi_ref[...] = jnp.where(dn, g, jnp.where(up & (side > 0.0), ghi * 0.5, ghi))
        side_ref[...] = jnp.where(up, 1.0, jnp.where(dn, -1.0, side))
        return c

    def snap():
        lo, hi, clo = lo_ref[...], hi_ref[...], clo_ref[...]
        lo_b, hi_b = bcast(lo), bcast(hi)

        def body(j, c):
            a, b = c
            for kt in (2 * j, 2 * j + 1):
                s = s_ref[kt]
                a = jnp.minimum(a, fold(jnp.where(s >= lo_b, s, jnp.inf), jnp.minimum))
                b = jnp.maximum(b, fold(jnp.where(s < hi_b, s, -jnp.inf), jnp.maximum))
            return a, b

        a, b = lax.fori_loop(0, n_pair, body,
                             (jnp.full((part, tq), jnp.inf, F32), jnp.full((part, tq), -jnp.inf, F32)))
        open_ = searching(lo, hi, clo)
        lo_ref[...] = jnp.where(open_, rep(jnp.min(a, axis=0, keepdims=True)), lo)
        hi_ref[...] = jnp.where(open_, _key2f(_f2key(rep(jnp.max(b, axis=0, keepdims=True))) + 1), hi)

    def n_searching():
        return jnp.max(jnp.where(searching(lo_ref[...], hi_ref[...], clo_ref[...]), 1.0, 0.0))

    def group(c):
        grp, _ = c
        lax.fori_loop(0, jnp.where(grp == 0, SEARCH_FIRST, SEARCH_GROUP), refine, 0)
        snap()
        return grp + 1, n_searching()

    lax.while_loop(lambda c: c[1] > 0.0, group, (jnp.int32(0), n_searching()))
    thr = lo_ref[...]
    thr_b = bcast(thr)

    n_tied = jnp.max(jnp.where(clo_ref[...] > needf, 1.0, 0.0))

    @pl.when(n_tied > 0.0)
    def _():
        rem_b = bcast(needf - count(lambda s, kt: s > thr_b))
        tri = jnp.where(lax.broadcasted_iota(I32, (kb, kb), 0) >= lax.broadcasted_iota(I32, (kb, kb), 1),
                        1.0, 0.0).astype(BF16)

        def drop(kt, seen):
            s = s_ref[kt]
            tie = s == thr_b
            rank = _dot(tri, jnp.where(tie, 1.0, 0.0).astype(BF16)) + bcast(seen)
            s_ref[kt] = jnp.where(tie & (rank > rem_b), -jnp.inf, s)
            return rep(rank[kb - 1:kb, :])

        lax.fori_loop(0, n_kt, drop, jnp.zeros((sl, tq), F32))

    rows_g = KV_REP * tq
    qf = q_ref[...].reshape(N_HEADS * tq, LANES).astype(F32)
    qn = jnp.sqrt(jnp.sum(qf * qf, axis=1, keepdims=True))
    lane = lax.broadcasted_iota(I32, (tq, LANES), 1)
    worst = jnp.float32(0.0)
    for h in range(N_HEADS):
        kmax = st_ref[b_id * N_KV_HEADS + h // KV_REP]
        bmax = st_ref[bsz * N_KV_HEADS + h]
        bfar = st_ref[bsz * N_KV_HEADS + N_HEADS + h]
        bound = qn[h * tq:(h + 1) * tq, :] * (kmax * 1.01) + (bmax + 0.1)
        worst = jnp.maximum(worst, jnp.max(bound))
        q2_ref[h * tq:(h + 1) * tq, :] = jnp.where(lane == HEAD_DIM, bfar - bound,
                                                   qf[h * tq:(h + 1) * tq, :]).astype(BF16)
    n_far = jnp.clip((d0 - nd) // 2 + 1, 0, n_kt)

    def select_masks(kt0, nb):
        return [jnp.where(s_ref[kt0 + j] >= thr_b, 0.0, NEG_BIG) for j in range(nb)]

    def logits(kt0, nb, g, near, exact, masks):
        k0 = pl.multiple_of(kt0 * kb, kb)
        s = _dot_nt(k_ref[g, pl.ds(k0, nb * kb), :], q2_ref[g * rows_g:(g + 1) * rows_g, :])
        out = []
        for j in range(nb):
            kt = kt0 + j
            maskadd = masks[j]
            if near:
                da = jnp.clip(d0 - 2 * kt, 0, nd - 1)
                db = jnp.clip(d0 - 2 * kt - 1, 0, nd - 1)
            parts = []
            for r in range(KV_REP):
                h = g * KV_REP + r
                add = maskadd - mrow_ref[0:1, h * tq:(h + 1) * tq] if exact else maskadd
                if near:
                    add = jnp.concatenate([bias_ref[da, h], bias_ref[db, h]], axis=0) + add
                parts.append(s[j * kb:(j + 1) * kb, r * tq:(r + 1) * tq] + add)
            out.append(jnp.concatenate(parts, axis=1))
        return out

    def over_blocks(fn):
        def far_chunk(j, c):
            fn(ATTEND_CHUNK * j, ATTEND_CHUNK, False)
            return c
        lax.fori_loop(0, n_far // ATTEND_CHUNK, far_chunk, 0)

        def far_single(kt, c):
            fn(kt, 1, False)
            return c
        lax.fori_loop(n_far - n_far % ATTEND_CHUNK, n_far, far_single, 0)

        n_near = n_kt - n_far
        for nb in range(1, near_max + 1):
            @pl.when(n_near == nb)
            def _(nb=nb):
                fn(n_far, nb, True)

    def attend(exact):
        acc_ref[...] = jnp.zeros(acc_ref.shape, F32)

        def blocks(kt0, nb, near):
            masks = select_masks(kt0, nb)
            for g in range(N_KV_HEADS):
                p = jnp.concatenate([jnp.exp2(lg).astype(BF16) for lg in logits(kt0, nb, g, near, exact, masks)],
                                    axis=0)
                vt = jnp.concatenate([vt_ref[g, kt0 + j] for j in range(nb)], axis=1)
                acc_ref[g] += _dot(vt, p)
        over_blocks(blocks)

    @pl.when(worst <= SHIFT_LIMIT)
    def _():
        attend(False)

    @pl.when(worst > SHIFT_LIMIT)
    def _():
        mrow_ref[...] = jnp.full(mrow_ref.shape, NEG_BIG, F32)

        def blocks(kt0, nb, near):
            masks = select_masks(kt0, nb)
            for g in range(N_KV_HEADS):
                for lg in logits(kt0, nb, g, near, False, masks):
                    mx = jnp.max(lg, axis=0, keepdims=True)
                    cur = mrow_ref[:, g * rows_g:(g + 1) * rows_g]
                    mrow_ref[:, g * rows_g:(g + 1) * rows_g] = jnp.maximum(cur, jnp.broadcast_to(mx, (sl, rows_g)))
        over_blocks(blocks)
        attend(True)

    for g in range(N_KV_HEADS):
        acc = acc_ref[g]
        og = acc / acc[HEAD_DIM:HEAD_DIM + 1, :]
        for r in range(KV_REP):
            h = g * KV_REP + r
            o_ref[:, h * LANES:(h + 1) * LANES] = og[:, r * tq:(r + 1) * tq].T.astype(BF16)


def _dsa(stats, q, qi, wt, k_all, vt_all, ki_all, bias_tiles, bsz, seq, past, n_keys, tq):
    lk = k_all.shape[2]
    nkt = lk // KEY_BLOCK
    topk = min(TOPK_MAX, n_keys // 4)
    nd = bias_tiles.shape[0]
    nq = seq // tq
    assert past % KEY_TILE == 0 and tq == KEY_TILE
    idx_bits = int(math.ceil(math.log2(lk))) + 1
    near_max = 0
    for i in range(nq):
        q0 = past + i * tq
        n_kt = min(nkt, (((q0 + tq - 1) // CHUNK + 1) * CHUNK + KEY_BLOCK - 1) // KEY_BLOCK)
        n_far = min(max((q0 // KEY_TILE - nd) // 2 + 1, 0), n_kt)
        near_max = max(near_max, n_kt - n_far)
    kern = functools.partial(_dsa_kernel, bsz=bsz, tq=tq, past=past, n_keys=n_keys, topk=topk, nkt=nkt, nd=nd,
                             idx_bits=idx_bits, near_max=near_max)
    row_state = pltpu.VMEM((SUBLANES, tq), F32)
    grid_spec = pltpu.PrefetchScalarGridSpec(
        num_scalar_prefetch=1,
        grid=(bsz, nq),
        in_specs=[
            pl.BlockSpec((None, N_HEADS, tq, LANES), lambda b, i, s: (b, 0, i, 0)),
            pl.BlockSpec((None, IDX_HEADS, tq, LANES), lambda b, i, s: (b, 0, i, 0)),
            pl.BlockSpec((None, IDX_HEADS, tq), lambda b, i, s: (b, 0, i)),
            pl.BlockSpec((None, N_KV_HEADS, lk, LANES), lambda b, i, s: (b, 0, 0, 0)),
            pl.BlockSpec((None, N_KV_HEADS, nkt, LANES, KEY_BLOCK), lambda b, i, s: (b, 0, 0, 0, 0)),
            pl.BlockSpec((None, lk, LANES), lambda b, i, s: (b, 0, 0)),
            pl.BlockSpec(bias_tiles.shape, lambda b, i, s: (0, 0, 0, 0)),
        ],
        out_specs=pl.BlockSpec((None, tq, N_HEADS * LANES), lambda b, i, s: (b, i, 0)),
        scratch_shapes=[
            pltpu.VMEM((nkt + 1, KEY_BLOCK, tq), F32),
            row_state, row_state, row_state, row_state, row_state, row_state,
            pltpu.VMEM((N_HEADS * tq, LANES), BF16),
            pltpu.VMEM((SUBLANES, N_HEADS * tq), F32),
            pltpu.VMEM((N_KV_HEADS, LANES, KV_REP * tq), F32),
        ],
    )
    return pl.pallas_call(
        kern,
        grid_spec=grid_spec,
        out_shape=jax.ShapeDtypeStruct((bsz, seq, N_HEADS * LANES), BF16),
        compiler_params=_cp(("arbitrary", "arbitrary")),
        name="dsa",
    )(stats, q, qi, wt, k_all, vt_all, ki_all, bias_tiles)


def _merge_kernel(x_ref, ya_ref, at_ref, sga_ref, sgb_ref, wup_ref, wout_ref, g2_ref, wr_hi_ref, wr_lo_ref, br_ref,
                  x1_ref, h2_ref, gt_ref, rt_ref, cnt_ref, run_ref, *, tm, sub):
    step = pl.program_id(0)

    @pl.when(step % sub == 0)
    def _():
        run_ref[...] = jnp.zeros(run_ref.shape, F32)

    yb = _dot(at_ref[...], wup_ref[...])
    merged = sga_ref[...].astype(F32) * ya_ref[...].astype(F32) + sgb_ref[...].astype(F32) * yb
    x1 = x_ref[...] + _dot(merged.astype(BF16), wout_ref[...])
    x1_ref[...] = x1
    ms = jnp.mean(x1 * x1, axis=-1, keepdims=True)
    h2 = x1 * lax.rsqrt(ms + EPS) * g2_ref[...]
    h2_hi, h2_lo = _split(h2)
    h2_ref[...] = h2_hi

    wr_hi = wr_hi_ref[...]
    logit = (_dot_nt(wr_hi, h2_hi) + _dot_nt(wr_hi, h2_lo) + _dot_nt(wr_lo_ref[...], h2_hi)) + br_ref[:, 0:1]
    ne = logit.shape[0]
    eid = lax.broadcasted_iota(I32, (ne, tm), 0).astype(F32)
    selb = jnp.zeros((ne, tm), F32)
    tops = []
    picks = []
    for _ in range(TOP_K):
        mx = jnp.max(logit, axis=0, keepdims=True)
        pick = jnp.min(jnp.where(logit == mx, eid, float(ne)), axis=0, keepdims=True)
        hit = eid == pick
        selb = jnp.where(hit, 1.0, selb)
        logit = jnp.where(hit, -jnp.inf, logit)
        tops.append(mx)
        picks.append(hit)
    ex = [jnp.exp(t - tops[0]) for t in tops]
    den = ex[0] + ex[1] + ex[2] + ex[3]
    gate = jnp.zeros((ne, tm), F32)
    for hit, e in zip(picks, ex):
        gate = jnp.where(hit, e / den, gate)
    gt_ref[...] = gate

    sel = selb > 0.5
    selb = selb.astype(BF16)
    r_i = lax.broadcasted_iota(I32, (tm, tm), 0)
    c_i = lax.broadcasted_iota(I32, (tm, tm), 1)
    tri = jnp.where(r_i < c_i, 1.0, 0.0).astype(BF16)
    run = run_ref[...]
    rank = _dot(selb, tri) + jnp.broadcast_to(run[:, 0:1], (ne, tm))
    rt_ref[...] = jnp.where(sel, rank, -1.0)
    run = run + _dot(selb, jnp.ones((tm, LANES), BF16))
    run_ref[...] = run
    cnt_ref[...] = run


def _merge(x2, ya, attn, sga, sgb, mw, tm, moe_tile):
    t, d = x2.shape
    sub = moe_tile // tm
    ne = mw['wr_hi'].shape[0]

    def tok(i):
        return (i, 0)

    def cst(i):
        return (0, 0)

    consts = [mw['wup'], mw['wout'], mw['g2'], mw['wr_hi'], mw['wr_lo'], mw['br']]
    return pl.pallas_call(
        functools.partial(_merge_kernel, tm=tm, sub=sub),
        grid=(t // tm,),
        in_specs=[
            pl.BlockSpec((tm, d), tok),
            pl.BlockSpec((tm, d), tok),
            pl.BlockSpec((tm, attn.shape[-1]), tok),
            pl.BlockSpec((tm, d), tok),
            pl.BlockSpec((tm, d), tok),
        ] + [pl.BlockSpec(a.shape, cst) for a in consts],
        out_specs=(
            pl.BlockSpec((tm, d), tok),
            pl.BlockSpec((tm, d), tok),
            pl.BlockSpec((ne, tm), lambda i: (0, i)),
            pl.BlockSpec((ne, tm), lambda i: (0, i)),
            pl.BlockSpec((None, ne, LANES), lambda i: (i // sub, 0, 0)),
        ),
        out_shape=(
            jax.ShapeDtypeStruct((t, d), F32),
            jax.ShapeDtypeStruct((t, d), BF16),
            jax.ShapeDtypeStruct((ne, t), F32),
            jax.ShapeDtypeStruct((ne, t), F32),
            jax.ShapeDtypeStruct((t // moe_tile, ne, LANES), F32),
        ),
        scratch_shapes=[pltpu.VMEM((ne, LANES), F32)],
        compiler_params=_cp(("arbitrary",)),
        name="merge",
    )(x2, ya, attn, sga, sgb, *consts)


def _moe_kernel(cnt_ref, h2_ref, x1_hbm, gt_ref, rt_ref, wg_ref, wu_ref, wd_ref, bg_ref, bu_ref, bd_ref, y_ref,
                pg_ref, og_ref, sem, *, tt, pair):
    j = pl.program_id(0)
    e = pl.program_id(1)
    ne = pl.num_programs(1)
    rb = MOE_ROWS
    slot = e % MOE_GROUP

    @pl.when(e == 0)
    def _():
        cp = pltpu.make_async_copy(x1_hbm.at[pl.ds(pl.multiple_of(j * pair * tt, tt), pair * tt), :], y_ref, sem)
        cp.start()
        cp.wait()

    mine = lax.broadcasted_iota(I32, (SUBLANES, pair * tt), 0) == e % SUBLANES
    g_all = jnp.sum(jnp.where(mine, gt_ref[...], 0.0), axis=0, keepdims=True)
    r_all = jnp.sum(jnp.where(mine, rt_ref[...], 0.0), axis=0, keepdims=True)
    rid = lax.broadcasted_iota(I32, (rb, tt), 0).astype(F32)

    def one_hot(s, blk):
        return jnp.broadcast_to(r_all[:, s * tt:(s + 1) * tt], (rb, tt)) == (rid + (blk * rb).astype(F32))

    def gather(s, hit):
        p = jnp.where(hit, 1.0, 0.0).astype(BF16)
        return p, _dot(p, h2_ref[s * tt:(s + 1) * tt, :]).astype(BF16)

    def expert(xg):
        a = jnp.minimum(_dot(xg, wg_ref[0]) + bg_ref[0], SWIGLU_LIMIT)
        b = jnp.clip(_dot(xg, wu_ref[0]) + bu_ref[0], -SWIGLU_LIMIT, SWIGLU_LIMIT)
        hid = a * jax.nn.sigmoid(SWIGLU_ALPHA * a) * (b + 1.0)
        return _dot(hid.astype(BF16), wd_ref[0]) + bd_ref[0]

    def gated(s, hit, o):
        g_row = jnp.broadcast_to(g_all[:, s * tt:(s + 1) * tt], (rb, tt))
        return (o * jnp.sum(jnp.where(hit, g_row, 0.0), axis=1, keepdims=True)).astype(BF16)

    hits = [one_hot(s, jnp.int32(0)) for s in range(pair)]
    gathered = [gather(s, hits[s]) for s in range(pair)]
    o = expert(jnp.concatenate([xg for _, xg in gathered], axis=0))
    r0 = pl.multiple_of(slot * rb, rb)
    for s in range(pair):
        pg_ref[s, pl.ds(r0, rb), :] = gathered[s][0]
        og_ref[s, pl.ds(r0, rb), :] = gated(s, hits[s], o[s * rb:(s + 1) * rb])

    @pl.when(slot == MOE_GROUP - 1)
    def _():
        for s in range(pair):
            y_ref[s * tt:(s + 1) * tt, :] += _dot_tn(pg_ref[s], og_ref[s])

    for s in range(pair):
        def overflow(blk, c, s=s):
            hit = one_hot(s, blk)
            p, xg = gather(s, hit)
            y_ref[s * tt:(s + 1) * tt, :] += _dot_tn(p, gated(s, hit, expert(xg)))
            return c

        n_rows = cnt_ref[(j * pair + s) * ne + e]
        lax.fori_loop(1, (n_rows + rb - 1) // rb, overflow, 0)


def _moe(h2, x1, gt, rt, cnt, ew, tt):
    t, d = h2.shape
    ne = gt.shape[0]
    nt = t // tt
    f = ew['wg'].shape[-1]
    pair = MOE_PAIR if nt % MOE_PAIR == 0 else 1
    grid_spec = pltpu.PrefetchScalarGridSpec(
        num_scalar_prefetch=1,
        grid=(nt // pair, ne),
        in_specs=[
            pl.BlockSpec((pair * tt, d), lambda j, e, c: (j, 0)),
            pl.BlockSpec(memory_space=pl.ANY),
            pl.BlockSpec((SUBLANES, pair * tt), lambda j, e, c: (e // SUBLANES, j)),
            pl.BlockSpec((SUBLANES, pair * tt), lambda j, e, c: (e // SUBLANES, j)),
            pl.BlockSpec((1, d, f), lambda j, e, c: (e, 0, 0)),
            pl.BlockSpec((1, d, f), lambda j, e, c: (e, 0, 0)),
            pl.BlockSpec((1, f, d), lambda j, e, c: (e, 0, 0)),
            pl.BlockSpec((1, 1, f), lambda j, e, c: (e, 0, 0)),
            pl.BlockSpec((1, 1, f), lambda j, e, c: (e, 0, 0)),
            pl.BlockSpec((1, 1, d), lambda j, e, c: (e, 0, 0)),
        ],
        out_specs=pl.BlockSpec((pair * tt, d), lambda j, e, c: (j, 0)),
        scratch_shapes=[pltpu.VMEM((pair, MOE_GROUP * MOE_ROWS, tt), BF16),
                        pltpu.VMEM((pair, MOE_GROUP * MOE_ROWS, d), BF16),
                        pltpu.SemaphoreType.DMA(())],
    )
    assert ne % MOE_GROUP == 0
    return pl.pallas_call(
        functools.partial(_moe_kernel, tt=tt, pair=pair),
        grid_spec=grid_spec,
        out_shape=jax.ShapeDtypeStruct((t, d), F32),
        compiler_params=_cp(("arbitrary", "arbitrary")),
        name="moe",
    )(cnt, h2, x1, gt, rt, ew['wg'], ew['wu'], ew['wd'], ew['bg'], ew['bu'], ew['bd'])


def _pad_heads(wmat, n_heads, width):
    d = wmat.shape[0]
    w3 = wmat.reshape(d, n_heads, width)
    return jnp.pad(w3, ((0, 0), (0, 0), (0, LANES - width))).reshape(d, n_heads * LANES)


def _pad_lanes(v, width=LANES):
    v = v.reshape(1, -1)
    return jnp.pad(v, ((0, 0), (0, width - v.shape[1])))


def _rel_bucket(rel):
    half = REL_BUCKETS // 2
    max_exact = half // 2
    n = jnp.abs(rel)
    large = max_exact + (jnp.log(jnp.maximum(n, 1).astype(jnp.float32) / max_exact)
                         / math.log(REL_MAX_DIST / max_exact) * (half - max_exact)).astype(jnp.int32)
    large = jnp.minimum(large, half - 1)
    return jnp.where(rel > 0, half, 0) + jnp.where(n < max_exact, n, large)


def _bias_tiles(rel_bias):
    tk = KEY_TILE
    half = REL_BUCKETS // 2
    max_exact = half // 2
    n_sat = int(math.ceil(max_exact * (REL_MAX_DIST / max_exact) ** ((half - 1 - max_exact) / (half - max_exact)))) + 2
    nd = (n_sat + 2 * tk - 2) // tk + 1
    dd = jnp.arange(nd, dtype=I32)[:, None, None]
    c = jnp.arange(tk, dtype=I32)[None, :, None]
    r = jnp.arange(tk, dtype=I32)[None, None, :]
    bucket = _rel_bucket(c - r - dd * tk)
    onehot = (bucket[..., None] == jnp.arange(REL_BUCKETS, dtype=I32)).astype(F32)
    tiles = jnp.einsum('dcrb,bh->dhcr', onehot, rel_bias.astype(F32) * LOG2E,
                       precision=lax.Precision.HIGHEST)
    return tiles


def _prep_proj(norm1_g, w_in, q_norm_g, k_norm_g, idx_k_norm_g, idx_k_norm_b, d_model):
    ssm_w = d_model // 2
    attn_w = N_HEADS * HEAD_DIM
    kv = N_KV_HEADS * HEAD_DIM
    sizes = [ssm_w, attn_w, kv, kv, IDX_HEADS * IDX_DIM, IDX_DIM, IDX_HEADS, d_model, d_model]
    pts = np.cumsum(sizes)[:-1].tolist()
    wu, wq, wk, wv, wqi, wki, wwi, wga, wgb = jnp.split(w_in, pts, axis=1)
    bf = lambda a: a.astype(BF16)
    wwit = jnp.pad(wwi.T, ((0, 2 * SUBLANES - IDX_HEADS), (0, 0)))
    return dict(
        g1=norm1_g.reshape(1, -1).astype(F32),
        wu=bf(wu), wq=bf(_pad_heads(wq, N_HEADS, HEAD_DIM)), wk=bf(_pad_heads(wk, N_KV_HEADS, HEAD_DIM)),
        wv=bf(_pad_heads(wv, N_KV_HEADS, HEAD_DIM)), wqi=bf(_pad_heads(wqi, IDX_HEADS, IDX_DIM)),
        wki=bf(_pad_heads(wki, 1, IDX_DIM)), wwit=bf(wwit),
        wga=bf(wga), wgb=bf(wgb),
        gq=_pad_lanes(q_norm_g.astype(F32)), gk=_pad_lanes(k_norm_g.astype(F32)),
        gi=_pad_lanes(idx_k_norm_g.astype(F32)), bi=_pad_lanes(idx_k_norm_b.astype(F32)),
        ones_h=jnp.full((LANES, LANES), 1.0 / HEAD_DIM, BF16),
    )


def _prep_s5(lre, lim, log_dt, b_re, b_im, c_re, c_im, dvec, wa, wb):
    g, p = lre.shape
    ch = b_re.shape[-1]
    lam = lax.complex(lre.astype(F32), lim.astype(F32))
    dt = jnp.exp(log_dt.astype(F32))[:, None]
    a_bar = jnp.exp(lam * dt)
    b_bar = ((a_bar - 1.0) / lam)[:, :, None] * lax.complex(b_re.astype(F32), b_im.astype(F32))
    gs = g // S5_DIAG
    eye = jnp.eye(gs, dtype=F32)

    def blocks_in(m):
        return jnp.einsum('jgpc,gh->jgchp', m.reshape(S5_DIAG, gs, p, ch), eye).reshape(S5_DIAG, gs * ch, gs * p)

    def blocks_out(m):
        return jnp.einsum('jgcp,gh->jgphc', m.reshape(S5_DIAG, gs, ch, p), eye).reshape(S5_DIAG, gs * p, gs * ch)

    return dict(
        b_re=blocks_in(jnp.real(b_bar)).astype(BF16), b_im=blocks_in(jnp.imag(b_bar)).astype(BF16),
        c_re=blocks_out(c_re.astype(F32)).astype(BF16), c_im=blocks_out(-c_im.astype(F32)).astype(BF16),
        a_re=jnp.real(a_bar).reshape(1, g * p), a_im=jnp.imag(a_bar).reshape(1, g * p),
        d=dvec.reshape(1, -1).astype(F32), wa=wa.astype(BF16), wb=wb.astype(BF16),
    )


def _prep_merge(w_attn_up, w_out, norm2_g, w_router, b_router):
    d = w_attn_up.shape[1]
    wup = jnp.pad(w_attn_up.reshape(N_HEADS, HEAD_DIM, d), ((0, 0), (0, LANES - HEAD_DIM), (0, 0)))
    wr_t = w_router.astype(F32).T
    wr_hi = wr_t.astype(BF16)
    wr_lo = (wr_t - wr_hi.astype(F32)).astype(BF16)
    return dict(
        wup=wup.reshape(N_HEADS * LANES, d).astype(BF16), wout=w_out.astype(BF16),
        g2=norm2_g.reshape(1, -1).astype(F32), wr_hi=wr_hi, wr_lo=wr_lo,
        br=jnp.broadcast_to(b_router.astype(F32)[:, None], (b_router.shape[0], LANES)),
    )


def _prep_moe(wg, bg, wu, bu, wd, bd):
    return dict(wg=wg.astype(BF16), wu=wu.astype(BF16), wd=wd.astype(BF16),
                bg=bg.astype(F32)[:, None, :], bu=bu.astype(F32)[:, None, :], bd=bd.astype(F32)[:, None, :])


def _pick_tile(n, pref):
    t = min(n, pref)
    while n % t:
        t //= 2
    return t


def _pad_axis(a, axis, size):
    pad = [(0, 0)] * a.ndim
    pad[axis] = (0, size - a.shape[axis])
    return jnp.pad(a, pad)


def _trunk_layer(x, past_k, past_v, past_ik, h0_re, h0_im, pw, sw, mw, ew, bias_tiles):
    bsz, seq, d = x.shape
    t = bsz * seq
    tm = _pick_tile(seq, 512)
    u_tb, q, kp, vp, qi, kip, wt, sga, sgb, kc, vc, kic = _proj(x, pw, bsz, seq, tm)

    half = sw['a_re'].shape[1]
    if h0_re is None:
        h0 = jnp.zeros((bsz, 2 * half), F32)
    else:
        h0 = jnp.concatenate([h0_re.reshape(bsz, half), h0_im.reshape(bsz, half)], axis=1).astype(F32)
    tc = _pick_tile(seq, max(1, S5_ROWS // bsz))
    ya, hout = _s5(u_tb, h0, sw, bsz, seq, tc)
    groups = half // SSM_STATE
    s_re = hout[:, :half].reshape(bsz, groups, SSM_STATE)
    s_im = hout[:, half:].reshape(bsz, groups, SSM_STATE)

    past = 0 if past_k is None else past_k.shape[1]
    n_keys = past + seq
    lk = -(-n_keys // KEY_BLOCK) * KEY_BLOCK
    kip3 = kip.reshape(bsz, seq, LANES)
    if past:
        lane = jnp.arange(LANES)
        pk = jnp.pad(past_k.astype(F32), ((0, 0), (0, 0), (0, 0), (0, LANES - HEAD_DIM)))
        pk = jnp.where(lane == HEAD_DIM, 1.0, pk).astype(BF16)
        pv = jnp.pad(past_v.astype(F32), ((0, 0), (0, 0), (0, 0), (0, LANES - HEAD_DIM)))
        pv = jnp.where(lane == HEAD_DIM, 1.0, pv).astype(BF16)
        pik = jnp.pad(past_ik.astype(F32), ((0, 0), (0, 0), (0, LANES - IDX_DIM))).astype(BF16)
        k_all = jnp.concatenate([pk.transpose(0, 2, 1, 3), kp], axis=2)
        v_all = jnp.concatenate([pv.transpose(0, 2, 1, 3), vp], axis=2)
        ki_all = jnp.concatenate([pik, kip3], axis=1)
    else:
        k_all, v_all, ki_all = kp, vp, kip3
    k_all = _pad_axis(k_all, 2, lk)
    v_all = _pad_axis(v_all, 2, lk)
    ki_all = _pad_axis(ki_all, 1, lk)
    vt_all = v_all.reshape(bsz, N_KV_HEADS, lk // KEY_BLOCK, KEY_BLOCK, LANES).transpose(0, 1, 2, 4, 3)
    tq = KEY_TILE
    seq_q = -(-seq // tq) * tq
    q_p, qi_p, wt_p = _pad_axis(q, 2, seq_q), _pad_axis(qi, 2, seq_q), _pad_axis(wt, 2, seq_q)
    kf = k_all[..., :HEAD_DIM].astype(F32)
    kmax = jnp.sqrt(jnp.max(jnp.sum(kf * kf, axis=-1), axis=-1)).reshape(-1)
    bfar = bias_tiles[-1, :, 0, 0]
    bmax = jnp.max(jnp.abs(bias_tiles), axis=(0, 2, 3))
    stats = jnp.concatenate([kmax, bmax, bfar]).astype(F32)
    bias_tiles = bias_tiles - bfar[None, :, None, None]
    attn = _dsa(stats, q_p, qi_p, wt_p, k_all, vt_all, ki_all, bias_tiles, bsz, seq_q, past, n_keys, tq)
    attn = attn[:, :seq]

    moe_tile = _pick_tile(t, MOE_TILE)
    tm2 = _pick_tile(moe_tile, 512)
    x1, h2, gt, rt, cnt = _merge(x.reshape(t, d), ya, attn.reshape(t, attn.shape[-1]), sga, sgb, mw, tm2, moe_tile)
    cnt_i = cnt[:, :, 0].astype(I32).reshape(-1)
    y = _moe(h2, x1, gt, rt, cnt_i, ew, moe_tile)

    k_new = kc.reshape(bsz, seq, N_KV_HEADS, HEAD_DIM)
    v_new = vc.reshape(bsz, seq, N_KV_HEADS, HEAD_DIM)
    ik_new = kic.reshape(bsz, seq, IDX_DIM)
    return y.reshape(bsz, seq, d), k_new, v_new, ik_new, s_re, s_im


def kernel(x_prompt, x_sample, cache_k, cache_v, cache_idx_k, state_ssm_re, state_ssm_im, rel_bias, norm1_g, w_in, ssm_lambda_re, ssm_lambda_im, ssm_log_dt, ssm_b_re, ssm_b_im, ssm_c_re, ssm_c_im, ssm_d, ssm_w_glu_a, ssm_w_glu_b, q_norm_g, k_norm_g, idx_k_norm_g, idx_k_norm_b, w_attn_up, w_out, norm2_g, moe_w_router, moe_b_router, moe_w_gate, moe_b_gate, moe_w_up, moe_b_up, moe_w_down, moe_b_down):
    depth = w_in.shape[0]
    d_model = x_prompt.shape[-1]
    bias_tiles = _bias_tiles(rel_bias)
    xp, xs = x_prompt, x_sample
    st_p, st_s = [], []
    for l in range(depth):
        pw = _prep_proj(norm1_g[l], w_in[l], q_norm_g[l], k_norm_g[l], idx_k_norm_g[l], idx_k_norm_b[l], d_model)
        sw = _prep_s5(ssm_lambda_re[l], ssm_lambda_im[l], ssm_log_dt[l], ssm_b_re[l], ssm_b_im[l], ssm_c_re[l],
                      ssm_c_im[l], ssm_d[l], ssm_w_glu_a[l], ssm_w_glu_b[l])
        mw = _prep_merge(w_attn_up[l], w_out[l], norm2_g[l], moe_w_router[l], moe_b_router[l])
        ew = _prep_moe(moe_w_gate[l], moe_b_gate[l], moe_w_up[l], moe_b_up[l], moe_w_down[l], moe_b_down[l])
        xp, *sp = _trunk_layer(xp, None, None, None, None, None, pw, sw, mw, ew, bias_tiles)
        xs, *ss = _trunk_layer(xs, cache_k[l], cache_v[l], cache_idx_k[l], state_ssm_re[l], state_ssm_im[l],
                               pw, sw, mw, ew, bias_tiles)
        st_p.append(sp)
        st_s.append(ss)
    outs_p = [jnp.stack([s[i] for s in st_p]) for i in range(5)]
    outs_s = [jnp.stack([s[i] for s in st_s]) for i in range(5)]
    return (xp, xs, *outs_p, *outs_s)
```

```python
import functools
import math

import numpy as np
import jax
import jax.numpy as jnp
from jax import lax
from jax.experimental import pallas as pl
from jax.experimental.pallas import tpu as pltpu

F32 = jnp.float32
BF16 = jnp.bfloat16
I32 = jnp.int32

LANES = 128
SUBLANES = 8
VMEM_LIMIT = 56 * 1024 * 1024

CHUNK = 64
SSM_GROUP_CH = 16
SSM_STATE = 64
N_HEADS = 8
HEAD_DIM = 64
N_KV_HEADS = 2
KV_REP = N_HEADS // N_KV_HEADS
IDX_HEADS = 8
IDX_DIM = 64
TOPK_MAX = 256
REL_BUCKETS = 32
REL_MAX_DIST = 1024
N_EXPERTS = 32
TOP_K = 4
SWIGLU_LIMIT = 7.0
SWIGLU_ALPHA = 1.702
EPS = 1e-6

KEY_TILE = 128
KEY_BLOCK = 256
SCORE_CHUNK = 4
ATTEND_CHUNK = 4
LOG2E = math.log2(math.e)
NEG_BIG = -1e30
SHIFT_LIMIT = 30.0
S5_ROWS = 1024
S5_DIAG = 2
SEARCH_FIRST = 8
SEARCH_GROUP = 4
MOE_TILE = 1024
MOE_ROWS = 160
MOE_GROUP = 8
MOE_PAIR = 2


def _cp(sem):
    return pltpu.CompilerParams(dimension_semantics=sem, vmem_limit_bytes=VMEM_LIMIT)


def _dot(a, b):
    return jnp.dot(a, b, preferred_element_type=F32)


def _dot_nt(a, b):
    return lax.dot_general(a, b, (((1,), (1,)), ((), ())), preferred_element_type=F32)


def _dot_tn(a, b):
    return lax.dot_general(a, b, (((0,), (0,)), ((), ())), preferred_element_type=F32)


def _split(a):
    hi = a.astype(BF16)
    lo = (a - hi.astype(F32)).astype(BF16)
    return hi, lo


def _dot_split(a, g):
    hi, lo = _split(a)
    return _dot(hi, g) + _dot(lo, g)


def _proj_kernel(x_ref, g1_ref, wu_ref, wq_ref, wk_ref, wv_ref, wqi_ref, wki_ref, wwit_ref, wga_ref, wgb_ref,
                 gq_ref, gk_ref, gi_ref, bi_ref, ones_h_ref,
                 u_ref, q_ref, kp_ref, vp_ref, qi_ref, kip_ref, wt_ref, sga_ref, sgb_ref,
                 kc_ref, vc_ref, kic_ref):
    x = x_ref[...]
    ms = jnp.mean(x * x, axis=-1, keepdims=True)
    hn = (x * lax.rsqrt(ms + EPS) * g1_ref[...]).astype(BF16)
    ones_h = ones_h_ref[...]
    lane = lax.broadcasted_iota(I32, (x.shape[0], LANES), 1)

    u_ref[...] = _dot(hn, wu_ref[...]).astype(BF16)

    q = _dot(hn, wq_ref[...])
    scale = HEAD_DIM ** -0.5 * LOG2E
    for h in range(N_HEADS):
        qh = q[:, h * LANES:(h + 1) * LANES]
        msq = _dot_split(qh * qh, ones_h)
        q_ref[h] = (qh * lax.rsqrt(msq + EPS) * (gq_ref[...] * scale)).astype(BF16)

    k = _dot(hn, wk_ref[...])
    for g in range(N_KV_HEADS):
        kg = k[:, g * LANES:(g + 1) * LANES]
        msk = _dot_split(kg * kg, ones_h)
        kn = kg * lax.rsqrt(msk + EPS) * gk_ref[...]
        kp_ref[g] = jnp.where(lane == HEAD_DIM, 1.0, kn).astype(BF16)
        kc_ref[:, g, :] = kn[:, :HEAD_DIM]

    v = _dot(hn, wv_ref[...])
    for g in range(N_KV_HEADS):
        vg = v[:, g * LANES:(g + 1) * LANES]
        vp_ref[g] = jnp.where(lane == HEAD_DIM, 1.0, vg).astype(BF16)
        vc_ref[:, g, :] = vg[:, :HEAD_DIM]

    qi = _dot(hn, wqi_ref[...])
    for h in range(IDX_HEADS):
        qi_ref[h] = qi[:, h * LANES:(h + 1) * LANES].astype(BF16)

    ki = _dot(hn, wki_ref[...])
    mu = _dot_split(ki, ones_h)
    xc = jnp.where(lane < IDX_DIM, ki - mu, 0.0)
    var = _dot_split(xc * xc, ones_h)
    kin = xc * lax.rsqrt(var + EPS) * gi_ref[...] + bi_ref[...]
    kip_ref[...] = kin.astype(BF16)
    kic_ref[...] = kin[:, :IDX_DIM]

    wt = _dot_nt(wwit_ref[...], hn)
    wt_ref[...] = wt[0:IDX_HEADS, :] * (IDX_HEADS ** -0.5 * IDX_DIM ** -0.5)

    sga_ref[...] = jax.nn.sigmoid(_dot(hn, wga_ref[...])).astype(BF16)
    sgb_ref[...] = jax.nn.sigmoid(_dot(hn, wgb_ref[...])).astype(BF16)


def _proj(x, pw, bsz, seq, tm):
    d = x.shape[-1]
    nt = seq // tm
    t = bsz * seq
    x2 = x.reshape(t, d)

    def tok(b, i):
        return (b * nt + i, 0)

    def cst(b, i):
        return (0, 0)

    def wspec(a):
        return pl.BlockSpec(a.shape, cst)

    weights = [pw['g1'], pw['wu'], pw['wq'], pw['wk'], pw['wv'], pw['wqi'], pw['wki'], pw['wwit'], pw['wga'],
               pw['wgb'], pw['gq'], pw['gk'], pw['gi'], pw['bi'], pw['ones_h']]
    ssm_w = pw['wu'].shape[1]
    out_shape = (
        jax.ShapeDtypeStruct((t, ssm_w), BF16),
        jax.ShapeDtypeStruct((bsz, N_HEADS, seq, LANES), BF16),
        jax.ShapeDtypeStruct((bsz, N_KV_HEADS, seq, LANES), BF16),
        jax.ShapeDtypeStruct((bsz, N_KV_HEADS, seq, LANES), BF16),
        jax.ShapeDtypeStruct((bsz, IDX_HEADS, seq, LANES), BF16),
        jax.ShapeDtypeStruct((t, LANES), BF16),
        jax.ShapeDtypeStruct((bsz, IDX_HEADS, seq), F32),
        jax.ShapeDtypeStruct((t, d), BF16),
        jax.ShapeDtypeStruct((t, d), BF16),
        jax.ShapeDtypeStruct((t, N_KV_HEADS, HEAD_DIM), F32),
        jax.ShapeDtypeStruct((t, N_KV_HEADS, HEAD_DIM), F32),
        jax.ShapeDtypeStruct((t, IDX_DIM), F32),
    )

    def hm(nh):
        return pl.BlockSpec((None, nh, tm, LANES), lambda b, i: (b, 0, i, 0))

    out_specs = (
        pl.BlockSpec((tm, ssm_w), tok),
        hm(N_HEADS), hm(N_KV_HEADS), hm(N_KV_HEADS), hm(IDX_HEADS),
        pl.BlockSpec((tm, LANES), tok),
        pl.BlockSpec((None, IDX_HEADS, tm), lambda b, i: (b, 0, i)),
        pl.BlockSpec((tm, d), tok), pl.BlockSpec((tm, d), tok),
        pl.BlockSpec((tm, N_KV_HEADS, HEAD_DIM), lambda b, i: (b * nt + i, 0, 0)),
        pl.BlockSpec((tm, N_KV_HEADS, HEAD_DIM), lambda b, i: (b * nt + i, 0, 0)),
        pl.BlockSpec((tm, IDX_DIM), tok),
    )
    return pl.pallas_call(
        _proj_kernel,
        grid=(bsz, nt),
        in_specs=[pl.BlockSpec((tm, d), tok)] + [wspec(a) for a in weights],
        out_specs=out_specs,
        out_shape=out_shape,
        compiler_params=_cp(("arbitrary", "arbitrary")),
        name="proj",
    )(x2, *weights)


def _gelu_tanh(x):
    return 0.5 * x * (1.0 + jnp.tanh(math.sqrt(2.0 / math.pi) * (x + 0.044715 * (x * x * x))))


def _s5_kernel(u_ref, h0_ref, bre_ref, bim_ref, are_ref, aim_ref, cre_ref, cim_ref, dvec_ref, wa_ref, wb_ref,
               ya_ref, hout_ref, state_ref, bu_ref, yf_ref, ug_ref, *, bsz, tc, strip):
    s = pl.program_id(0)
    half = are_ref.shape[1]

    @pl.when(s == 0)
    def _():
        state_ref[...] = h0_ref[...]

    ssm_w = dvec_ref.shape[1]
    for b in range(bsz):
        for c in range(ssm_w // LANES):
            ug_ref[c, pl.ds(b, tc, stride=bsz), :] = u_ref[b, :, c * LANES:(c + 1) * LANES].astype(F32)
    u = jnp.concatenate([ug_ref[c] for c in range(ssm_w // LANES)], axis=1).astype(BF16)
    cw = u.shape[1] // S5_DIAG
    sw = half // S5_DIAG
    for j in range(S5_DIAG):
        uj = u[:, j * cw:(j + 1) * cw]
        bu_ref[:, j * sw:(j + 1) * sw] = _dot(uj, bre_ref[j])
        bu_ref[:, half + j * sw:half + (j + 1) * sw] = _dot(uj, bim_ref[j])

    for c0 in range(0, half, strip):
        ar = jnp.broadcast_to(are_ref[:, c0:c0 + strip], (bsz, strip))
        ai = jnp.broadcast_to(aim_ref[:, c0:c0 + strip], (bsz, strip))
        hr0 = state_ref[:, c0:c0 + strip]
        hi0 = state_ref[:, half + c0:half + c0 + strip]

        def step(t, carry):
            hr, hi = carry
            r0 = pl.multiple_of(t * bsz, bsz)
            br = bu_ref[pl.ds(r0, bsz), c0:c0 + strip]
            bi = bu_ref[pl.ds(r0, bsz), half + c0:half + c0 + strip]
            nr = ar * hr - ai * hi + br
            ni = ar * hi + ai * hr + bi
            bu_ref[pl.ds(r0, bsz), c0:c0 + strip] = nr
            bu_ref[pl.ds(r0, bsz), half + c0:half + c0 + strip] = ni
            return nr, ni

        hr, hi = lax.fori_loop(0, tc, step, (hr0, hi0))
        state_ref[:, c0:c0 + strip] = hr
        state_ref[:, half + c0:half + c0 + strip] = hi

    ys = []
    for j in range(S5_DIAG):
        s_re = bu_ref[:, j * sw:(j + 1) * sw].astype(BF16)
        s_im = bu_ref[:, half + j * sw:half + (j + 1) * sw].astype(BF16)
        ys.append(_dot(s_re, cre_ref[j]) + _dot(s_im, cim_ref[j]))
    y = jnp.concatenate(ys, axis=1) + dvec_ref[...] * u.astype(F32)
    g = _gelu_tanh(y).astype(BF16)
    ya = _dot(g, wa_ref[...]) * jax.nn.sigmoid(_dot(g, wb_ref[...]))
    n_chunk = ya.shape[1] // LANES
    for c in range(n_chunk):
        yf_ref[c] = ya[:, c * LANES:(c + 1) * LANES]
    for b in range(bsz):
        ya_ref[b] = jnp.concatenate([yf_ref[c, pl.ds(b, tc, stride=bsz), :] for c in range(n_chunk)],
                                    axis=1).astype(BF16)

    @pl.when(s == pl.num_programs(0) - 1)
    def _():
        hout_ref[...] = state_ref[...]


def _s5(u, h0, sw, bsz, seq, tc):
    rows = tc * bsz
    ssm_w = sw['d'].shape[1]
    half = sw['a_re'].shape[1]
    two_half = 2 * half
    d = sw['wa'].shape[1]
    u3 = u.reshape(bsz, seq, ssm_w)
    strip = min(512, half)

    consts = [h0, sw['b_re'], sw['b_im'], sw['a_re'], sw['a_im'], sw['c_re'], sw['c_im'], sw['d'], sw['wa'], sw['wb']]

    def cst(s):
        return (0, 0)

    def cspec(a):
        return pl.BlockSpec(a.shape, lambda s: (0,) * a.ndim)

    ya, hout = pl.pallas_call(
        functools.partial(_s5_kernel, bsz=bsz, tc=tc, strip=strip),
        grid=(seq // tc,),
        in_specs=[pl.BlockSpec((bsz, tc, ssm_w), lambda s: (0, s, 0))] + [cspec(a) for a in consts],
        out_specs=(pl.BlockSpec((bsz, tc, d), lambda s: (0, s, 0)), pl.BlockSpec((bsz, two_half), cst)),
        out_shape=(jax.ShapeDtypeStruct((bsz, seq, d), BF16), jax.ShapeDtypeStruct((bsz, two_half), F32)),
        scratch_shapes=[pltpu.VMEM((bsz, two_half), F32), pltpu.VMEM((rows, two_half), F32),
                        pltpu.VMEM((d // LANES, rows, LANES), F32), pltpu.VMEM((ssm_w // LANES, rows, LANES), F32)],
        compiler_params=_cp(("arbitrary",)),
        name="s5",
    )(u3, *consts)
    return ya.reshape(bsz * seq, d), hout


def _f2key(x):
    b = lax.bitcast_convert_type(x, I32)
    return b ^ ((b >> 31) & 0x7FFFFFFF)


def _key2f(k):
    return lax.bitcast_convert_type(k ^ ((k >> 31) & 0x7FFFFFFF), F32)


def _dsa_kernel(st_ref, q_ref, qi_ref, wt_ref, k_ref, vt_ref, ki_ref, bias_ref, o_ref,
                s_ref, lo_ref, hi_ref, clo_ref, glo_ref, ghi_ref, side_ref, q2_ref, mrow_ref, acc_ref,
                *, bsz, tq, past, n_keys, topk, nkt, nd, near_max):
    b_id = pl.program_id(0)
    i = pl.program_id(1)
    kb = KEY_BLOCK
    sl = SUBLANES
    q0 = past + i * tq
    last_chunk = (q0 + tq - 1) // CHUNK
    n_kt = jnp.minimum(nkt, ((last_chunk + 1) * CHUNK + kb - 1) // kb)
    d0 = q0 // KEY_TILE

    krow = lax.broadcasted_iota(I32, (kb, tq), 0)
    q_chunk = (q0 + lax.broadcasted_iota(I32, (kb, tq), 1)) // CHUNK
    qc8 = (q0 + lax.broadcasted_iota(I32, (sl, tq), 1)) // CHUNK
    n_adm = jnp.minimum((qc8 + 1) * CHUNK, n_keys)
    n_admf = n_adm.astype(F32)
    is_pad = q0 + lax.broadcasted_iota(I32, (sl, tq), 1) >= n_keys
    needf = jnp.where(is_pad, n_adm, jnp.minimum(topk, n_adm)).astype(F32)

    def bcast(x):
        return jnp.broadcast_to(x[0:1, :], (kb, tq))

    def rep(x):
        return jnp.broadcast_to(x, (sl, tq))

    qi = qi_ref[...].reshape(IDX_HEADS * tq, LANES)

    def score_blocks(kt, nb, masked):
        k0 = pl.multiple_of(kt * kb, kb)
        s = _dot_nt(ki_ref[pl.ds(k0, nb * kb), :], qi)
        for j in range(nb):
            sc = jnp.zeros((kb, tq), F32)
            for h in range(IDX_HEADS):
                sc = sc + wt_ref[h:h + 1, :] * jnp.maximum(s[j * kb:(j + 1) * kb, h * tq:(h + 1) * tq], 0.0)
            if masked:
                kpos = k0 + j * kb + krow
                adm = ((kpos // CHUNK) <= q_chunk) & (kpos < n_keys)
                sc = jnp.where(adm, sc, -jnp.inf)
            s_ref[kt + j] = sc

    n_open = n_kt - 1

    def score_chunk(j, c):
        score_blocks(SCORE_CHUNK * j, SCORE_CHUNK, False)
        return c

    lax.fori_loop(0, n_open // SCORE_CHUNK, score_chunk, 0)

    def score_single(j, c):
        score_blocks(j, 1, False)
        return c

    lax.fori_loop(n_open - n_open % SCORE_CHUNK, n_open, score_single, 0)
    score_blocks(n_kt - 1, 1, True)
    s_ref[n_kt] = jnp.full((kb, tq), -jnp.inf, F32)
    n_pair = (n_kt + 1) // 2

    part = 4 * sl

    def fold(x, op):
        x = x.reshape(kb // part, part, tq)
        acc = x[0]
        for j in range(1, kb // part):
            acc = op(acc, x[j])
        return acc

    def count(pred):
        def one(kt):
            return fold(jnp.where(pred(s_ref[kt], kt), 1.0, 0.0), jnp.add)
        c = lax.fori_loop(0, n_pair, lambda j, c: c + (one(2 * j) + one(2 * j + 1)), jnp.zeros((part, tq), F32))
        return rep(jnp.sum(c, axis=0, keepdims=True))

    def minmax(j, c):
        mx, mn = c
        for kt in (2 * j, 2 * j + 1):
            s = s_ref[kt]
            mx = jnp.maximum(mx, fold(s, jnp.maximum))
            mn = jnp.minimum(mn, fold(jnp.where(s == -jnp.inf, jnp.inf, s), jnp.minimum))
        return mx, mn

    mx, mn = lax.fori_loop(0, n_pair, minmax,
                           (jnp.full((part, tq), -jnp.inf, F32), jnp.full((part, tq), jnp.inf, F32)))
    lo_ref[...] = rep(jnp.min(mn, axis=0, keepdims=True))
    hi_ref[...] = _key2f(_f2key(rep(jnp.max(mx, axis=0, keepdims=True))) + 1)
    def odds(cnt):
        c = jnp.clip(cnt, 0.5, n_admf - 0.5)
        return jnp.log((n_admf - c) / c)

    target = odds(needf - 0.5)
    clo_ref[...] = n_admf
    glo_ref[...] = target - odds(n_admf)
    ghi_ref[...] = target - odds(jnp.zeros((sl, tq), F32))
    side_ref[...] = jnp.zeros((sl, tq), F32)

    def searching(lo, hi, clo):
        return (_f2key(hi) > _f2key(lo) + 1) & (clo > needf)

    def refine(it, c):
        lo, hi, clo = lo_ref[...], hi_ref[...], clo_ref[...]
        glo, ghi, side = glo_ref[...], ghi_ref[...], side_ref[...]
        k_t = _f2key(lo + (hi - lo) * (glo / (glo - ghi)))
        t = _key2f(jnp.minimum(jnp.maximum(k_t, _f2key(lo) + 1), _f2key(hi) - 1))
        tb = bcast(t)
        cnt = count(lambda s, kt: s >= tb)
        g = target - odds(cnt)
        open_ = searching(lo, hi, clo)
        up = open_ & (cnt >= needf)
        dn = open_ & (cnt < needf)
        lo_ref[...] = jnp.where(up, t, lo)
        clo_ref[...] = jnp.where(up, cnt, clo)
        hi_ref[...] = jnp.where(dn, t, hi)
        glo_ref[...] = jnp.where(up, g, jnp.where(dn & (side < 0.0), glo * 0.5, glo))
        ghi_ref[...] = jnp.where(dn, g, jnp.where(up & (side > 0.0), ghi * 0.5, ghi))
        side_ref[...] = jnp.where(up, 1.0, jnp.where(dn, -1.0, side))
        return c

    def snap():
        lo, hi, clo = lo_ref[...], hi_ref[...], clo_ref[...]
        lo_b, hi_b = bcast(lo), bcast(hi)

        def body(j, c):
            a, b = c
            for kt in (2 * j, 2 * j + 1):
                s = s_ref[kt]
                a = jnp.minimum(a, fold(jnp.where(s >= lo_b, s, jnp.inf), jnp.minimum))
                b = jnp.maximum(b, fold(jnp.where(s < hi_b, s, -jnp.inf), jnp.maximum))
            return a, b

        a, b = lax.fori_loop(0, n_pair, body,
                             (jnp.full((part, tq), jnp.inf, F32), jnp.full((part, tq), -jnp.inf, F32)))
        open_ = searching(lo, hi, clo)
        lo_ref[...] = jnp.where(open_, rep(jnp.min(a, axis=0, keepdims=True)), lo)
        hi_ref[...] = jnp.where(open_, _key2f(_f2key(rep(jnp.max(b, axis=0, keepdims=True))) + 1), hi)

    def n_searching():
        return jnp.max(jnp.where(searching(lo_ref[...], hi_ref[...], clo_ref[...]), 1.0, 0.0))

    lax.fori_loop(0, SEARCH_FIRST, refine, 0)
    snap()
    lax.fori_loop(0, SEARCH_GROUP, refine, 0)
    snap()

    def group(c):
        lax.fori_loop(0, SEARCH_GROUP, refine, 0)
        snap()
        return n_searching()

    lax.while_loop(lambda c: c > 0.0, group, n_searching())
    thr = lo_ref[...]
    thr_b = bcast(thr)

    n_tied = jnp.max(jnp.where(clo_ref[...] > needf, 1.0, 0.0))

    @pl.when(n_tied > 0.0)
    def _():
        rem_b = bcast(needf - count(lambda s, kt: s > thr_b))
        tri = jnp.where(lax.broadcasted_iota(I32, (kb, kb), 0) >= lax.broadcasted_iota(I32, (kb, kb), 1),
                        1.0, 0.0).astype(BF16)

        def drop(kt, seen):
            s = s_ref[kt]
            tie = s == thr_b
            rank = _dot(tri, jnp.where(tie, 1.0, 0.0).astype(BF16)) + bcast(seen)
            s_ref[kt] = jnp.where(tie & (rank > rem_b), -jnp.inf, s)
            return rep(rank[kb - 1:kb, :])

        lax.fori_loop(0, n_kt, drop, jnp.zeros((sl, tq), F32))

    rows_g = KV_REP * tq
    qf = q_ref[...].reshape(N_HEADS * tq, LANES).astype(F32)
    qn = jnp.sqrt(jnp.sum(qf * qf, axis=1, keepdims=True))
    lane = lax.broadcasted_iota(I32, (tq, LANES), 1)
    worst = jnp.float32(0.0)
    for h in range(N_HEADS):
        kmax = st_ref[b_id * N_KV_HEADS + h // KV_REP]
        bmax = st_ref[bsz * N_KV_HEADS + h]
        bfar = st_ref[bsz * N_KV_HEADS + N_HEADS + h]
        bound = qn[h * tq:(h + 1) * tq, :] * (kmax * 1.01) + (bmax + 0.1)
        worst = jnp.maximum(worst, jnp.max(bound))
        q2_ref[h * tq:(h + 1) * tq, :] = jnp.where(lane == HEAD_DIM, bfar - bound,
                                                   qf[h * tq:(h + 1) * tq, :]).astype(BF16)
    n_far = jnp.clip((d0 - nd) // 2 + 1, 0, n_kt)

    def select_masks(kt0, nb):
        return [jnp.where(s_ref[kt0 + j] >= thr_b, 0.0, NEG_BIG) for j in range(nb)]

    def logits(kt0, nb, g, near, exact, masks):
        k0 = pl.multiple_of(kt0 * kb, kb)
        s = _dot_nt(k_ref[g, pl.ds(k0, nb * kb), :], q2_ref[g * rows_g:(g + 1) * rows_g, :])
        out = []
        for j in range(nb):
            kt = kt0 + j
            maskadd = masks[j]
            if near:
                da = jnp.clip(d0 - 2 * kt, 0, nd - 1)
                db = jnp.clip(d0 - 2 * kt - 1, 0, nd - 1)
            parts = []
            for r in range(KV_REP):
                h = g * KV_REP + r
                add = maskadd - mrow_ref[0:1, h * tq:(h + 1) * tq] if exact else maskadd
                if near:
                    add = jnp.concatenate([bias_ref[da, h], bias_ref[db, h]], axis=0) + add
                parts.append(s[j * kb:(j + 1) * kb, r * tq:(r + 1) * tq] + add)
            out.append(jnp.concatenate(parts, axis=1))
        return out

    def over_blocks(fn):
        def far_chunk(j, c):
            fn(ATTEND_CHUNK * j, ATTEND_CHUNK, False)
            return c
        lax.fori_loop(0, n_far // ATTEND_CHUNK, far_chunk, 0)

        def far_single(kt, c):
            fn(kt, 1, False)
            return c
        lax.fori_loop(n_far - n_far % ATTEND_CHUNK, n_far, far_single, 0)

        n_near = n_kt - n_far
        for nb in range(1, near_max + 1):
            @pl.when(n_near == nb)
            def _(nb=nb):
                fn(n_far, nb, True)

    def attend(exact):
        acc_ref[...] = jnp.zeros(acc_ref.shape, F32)

        def blocks(kt0, nb, near):
            masks = select_masks(kt0, nb)
            for g in range(N_KV_HEADS):
                p = jnp.concatenate([jnp.exp2(lg).astype(BF16) for lg in logits(kt0, nb, g, near, exact, masks)],
                                    axis=0)
                vt = jnp.concatenate([vt_ref[g, kt0 + j] for j in range(nb)], axis=1)
                acc_ref[g] += _dot(vt, p)
        over_blocks(blocks)

    @pl.when(worst <= SHIFT_LIMIT)
    def _():
        attend(False)

    @pl.when(worst > SHIFT_LIMIT)
    def _():
        mrow_ref[...] = jnp.full(mrow_ref.shape, NEG_BIG, F32)

        def blocks(kt0, nb, near):
            masks = select_masks(kt0, nb)
            for g in range(N_KV_HEADS):
                for lg in logits(kt0, nb, g, near, False, masks):
                    mx = jnp.max(lg, axis=0, keepdims=True)
                    cur = mrow_ref[:, g * rows_g:(g + 1) * rows_g]
                    mrow_ref[:, g * rows_g:(g + 1) * rows_g] = jnp.maximum(cur, jnp.broadcast_to(mx, (sl, rows_g)))
        over_blocks(blocks)
        attend(True)

    for g in range(N_KV_HEADS):
        acc = acc_ref[g]
        og = acc / acc[HEAD_DIM:HEAD_DIM + 1, :]
        for r in range(KV_REP):
            h = g * KV_REP + r
            o_ref[:, h * LANES:(h + 1) * LANES] = og[:, r * tq:(r + 1) * tq].T.astype(BF16)


def _dsa(stats, q, qi, wt, k_all, vt_all, ki_all, bias_tiles, bsz, seq, past, n_keys, tq):
    lk = k_all.shape[2]
    nkt = lk // KEY_BLOCK
    topk = min(TOPK_MAX, n_keys // 4)
    nd = bias_tiles.shape[0]
    nq = seq // tq
    assert past % KEY_TILE == 0 and tq == KEY_TILE
    near_max = 0
    for i in range(nq):
        q0 = past + i * tq
        n_kt = min(nkt, (((q0 + tq - 1) // CHUNK + 1) * CHUNK + KEY_BLOCK - 1) // KEY_BLOCK)
        n_far = min(max((q0 // KEY_TILE - nd) // 2 + 1, 0), n_kt)
        near_max = max(near_max, n_kt - n_far)
    kern = functools.partial(_dsa_kernel, bsz=bsz, tq=tq, past=past, n_keys=n_keys, topk=topk, nkt=nkt, nd=nd,
                             near_max=near_max)
    row_state = pltpu.VMEM((SUBLANES, tq), F32)
    grid_spec = pltpu.PrefetchScalarGridSpec(
        num_scalar_prefetch=1,
        grid=(bsz, nq),
        in_specs=[
            pl.BlockSpec((None, N_HEADS, tq, LANES), lambda b, i, s: (b, 0, i, 0)),
            pl.BlockSpec((None, IDX_HEADS, tq, LANES), lambda b, i, s: (b, 0, i, 0)),
            pl.BlockSpec((None, IDX_HEADS, tq), lambda b, i, s: (b, 0, i)),
            pl.BlockSpec((None, N_KV_HEADS, lk, LANES), lambda b, i, s: (b, 0, 0, 0)),
            pl.BlockSpec((None, N_KV_HEADS, nkt, LANES, KEY_BLOCK), lambda b, i, s: (b, 0, 0, 0, 0)),
            pl.BlockSpec((None, lk, LANES), lambda b, i, s: (b, 0, 0)),
            pl.BlockSpec(bias_tiles.shape, lambda b, i, s: (0, 0, 0, 0)),
        ],
        out_specs=pl.BlockSpec((None, tq, N_HEADS * LANES), lambda b, i, s: (b, i, 0)),
        scratch_shapes=[
            pltpu.VMEM((nkt + 1, KEY_BLOCK, tq), F32),
            row_state, row_state, row_state, row_state, row_state, row_state,
            pltpu.VMEM((N_HEADS * tq, LANES), BF16),
            pltpu.VMEM((SUBLANES, N_HEADS * tq), F32),
            pltpu.VMEM((N_KV_HEADS, LANES, KV_REP * tq), F32),
        ],
    )
    return pl.pallas_call(
        kern,
        grid_spec=grid_spec,
        out_shape=jax.ShapeDtypeStruct((bsz, seq, N_HEADS * LANES), BF16),
        compiler_params=_cp(("arbitrary", "arbitrary")),
        name="dsa",
    )(stats, q, qi, wt, k_all, vt_all, ki_all, bias_tiles)


def _merge_kernel(x_ref, ya_ref, at_ref, sga_ref, sgb_ref, wup_ref, wout_ref, g2_ref, wr_hi_ref, wr_lo_ref, br_ref,
                  x1_ref, h2_ref, gt_ref, rt_ref, cnt_ref, run_ref, *, tm, sub):
    step = pl.program_id(0)

    @pl.when(step % sub == 0)
    def _():
        run_ref[...] = jnp.zeros(run_ref.shape, F32)

    yb = _dot(at_ref[...], wup_ref[...])
    merged = sga_ref[...].astype(F32) * ya_ref[...].astype(F32) + sgb_ref[...].astype(F32) * yb
    x1 = x_ref[...] + _dot(merged.astype(BF16), wout_ref[...])
    x1_ref[...] = x1
    ms = jnp.mean(x1 * x1, axis=-1, keepdims=True)
    h2 = x1 * lax.rsqrt(ms + EPS) * g2_ref[...]
    h2_hi, h2_lo = _split(h2)
    h2_ref[...] = h2_hi

    wr_hi = wr_hi_ref[...]
    logit = (_dot_nt(wr_hi, h2_hi) + _dot_nt(wr_hi, h2_lo) + _dot_nt(wr_lo_ref[...], h2_hi)) + br_ref[:, 0:1]
    ne = logit.shape[0]
    eid = lax.broadcasted_iota(I32, (ne, tm), 0).astype(F32)
    selb = jnp.zeros((ne, tm), F32)
    tops = []
    picks = []
    for _ in range(TOP_K):
        mx = jnp.max(logit, axis=0, keepdims=True)
        pick = jnp.min(jnp.where(logit == mx, eid, float(ne)), axis=0, keepdims=True)
        hit = eid == pick
        selb = jnp.where(hit, 1.0, selb)
        logit = jnp.where(hit, -jnp.inf, logit)
        tops.append(mx)
        picks.append(hit)
    ex = [jnp.exp(t - tops[0]) for t in tops]
    den = ex[0] + ex[1] + ex[2] + ex[3]
    gate = jnp.zeros((ne, tm), F32)
    for hit, e in zip(picks, ex):
        gate = jnp.where(hit, e / den, gate)
    gt_ref[...] = gate

    sel = selb > 0.5
    selb = selb.astype(BF16)
    r_i = lax.broadcasted_iota(I32, (tm, tm), 0)
    c_i = lax.broadcasted_iota(I32, (tm, tm), 1)
    tri = jnp.where(r_i < c_i, 1.0, 0.0).astype(BF16)
    run = run_ref[...]
    rank = _dot(selb, tri) + jnp.broadcast_to(run[:, 0:1], (ne, tm))
    rt_ref[...] = jnp.where(sel, rank, -1.0)
    run = run + _dot(selb, jnp.ones((tm, LANES), BF16))
    run_ref[...] = run
    cnt_ref[...] = run


def _merge(x2, ya, attn, sga, sgb, mw, tm, moe_tile):
    t, d = x2.shape
    sub = moe_tile // tm
    ne = mw['wr_hi'].shape[0]

    def tok(i):
        return (i, 0)

    def cst(i):
        return (0, 0)

    consts = [mw['wup'], mw['wout'], mw['g2'], mw['wr_hi'], mw['wr_lo'], mw['br']]
    return pl.pallas_call(
        functools.partial(_merge_kernel, tm=tm, sub=sub),
        grid=(t // tm,),
        in_specs=[
            pl.BlockSpec((tm, d), tok),
            pl.BlockSpec((tm, d), tok),
            pl.BlockSpec((tm, attn.shape[-1]), tok),
            pl.BlockSpec((tm, d), tok),
            pl.BlockSpec((tm, d), tok),
        ] + [pl.BlockSpec(a.shape, cst) for a in consts],
        out_specs=(
            pl.BlockSpec((tm, d), tok),
            pl.BlockSpec((tm, d), tok),
            pl.BlockSpec((ne, tm), lambda i: (0, i)),
            pl.BlockSpec((ne, tm), lambda i: (0, i)),
            pl.BlockSpec((None, ne, LANES), lambda i: (i // sub, 0, 0)),
        ),
        out_shape=(
            jax.ShapeDtypeStruct((t, d), F32),
            jax.ShapeDtypeStruct((t, d), BF16),
            jax.ShapeDtypeStruct((ne, t), F32),
            jax.ShapeDtypeStruct((ne, t), F32),
            jax.ShapeDtypeStruct((t // moe_tile, ne, LANES), F32),
        ),
        scratch_shapes=[pltpu.VMEM((ne, LANES), F32)],
        compiler_params=_cp(("arbitrary",)),
        name="merge",
    )(x2, ya, attn, sga, sgb, *consts)


def _moe_kernel(cnt_ref, h2_ref, x1_hbm, gt_ref, rt_ref, wg_ref, wu_ref, wd_ref, bg_ref, bu_ref, bd_ref, y_ref,
                pg_ref, og_ref, sem, *, tt, pair):
    j = pl.program_id(0)
    e = pl.program_id(1)
    ne = pl.num_programs(1)
    rb = MOE_ROWS
    slot = e % MOE_GROUP

    @pl.when(e == 0)
    def _():
        cp = pltpu.make_async_copy(x1_hbm.at[pl.ds(pl.multiple_of(j * pair * tt, tt), pair * tt), :], y_ref, sem)
        cp.start()
        cp.wait()

    mine = lax.broadcasted_iota(I32, (SUBLANES, pair * tt), 0) == e % SUBLANES
    g_all = jnp.sum(jnp.where(mine, gt_ref[...], 0.0), axis=0, keepdims=True)
    r_all = jnp.sum(jnp.where(mine, rt_ref[...], 0.0), axis=0, keepdims=True)
    rid = lax.broadcasted_iota(I32, (rb, tt), 0).astype(F32)

    def one_hot(s, blk):
        return jnp.broadcast_to(r_all[:, s * tt:(s + 1) * tt], (rb, tt)) == (rid + (blk * rb).astype(F32))

    def gather(s, hit):
        p = jnp.where(hit, 1.0, 0.0).astype(BF16)
        return p, _dot(p, h2_ref[s * tt:(s + 1) * tt, :]).astype(BF16)

    def expert(xg):
        a = jnp.minimum(_dot(xg, wg_ref[0]) + bg_ref[0], SWIGLU_LIMIT)
        b = jnp.clip(_dot(xg, wu_ref[0]) + bu_ref[0], -SWIGLU_LIMIT, SWIGLU_LIMIT)
        hid = a * jax.nn.sigmoid(SWIGLU_ALPHA * a) * (b + 1.0)
        return _dot(hid.astype(BF16), wd_ref[0]) + bd_ref[0]

    def gated(s, hit, o):
        g_row = jnp.broadcast_to(g_all[:, s * tt:(s + 1) * tt], (rb, tt))
        return (o * jnp.sum(jnp.where(hit, g_row, 0.0), axis=1, keepdims=True)).astype(BF16)

    hits = [one_hot(s, jnp.int32(0)) for s in range(pair)]
    gathered = [gather(s, hits[s]) for s in range(pair)]
    o = expert(jnp.concatenate([xg for _, xg in gathered], axis=0))
    r0 = pl.multiple_of(slot * rb, rb)
    for s in range(pair):
        pg_ref[s, pl.ds(r0, rb), :] = gathered[s][0]
        og_ref[s, pl.ds(r0, rb), :] = gated(s, hits[s], o[s * rb:(s + 1) * rb])

    @pl.when(slot == MOE_GROUP - 1)
    def _():
        for s in range(pair):
            y_ref[s * tt:(s + 1) * tt, :] += _dot_tn(pg_ref[s], og_ref[s])

    for s in range(pair):
        def overflow(blk, c, s=s):
            hit = one_hot(s, blk)
            p, xg = gather(s, hit)
            y_ref[s * tt:(s + 1) * tt, :] += _dot_tn(p, gated(s, hit, expert(xg)))
            return c

        n_rows = cnt_ref[(j * pair + s) * ne + e]
        lax.fori_loop(1, (n_rows + rb - 1) // rb, overflow, 0)


def _moe(h2, x1, gt, rt, cnt, ew, tt):
    t, d = h2.shape
    ne = gt.shape[0]
    nt = t // tt
    f = ew['wg'].shape[-1]
    pair = MOE_PAIR if nt % MOE_PAIR == 0 else 1
    grid_spec = pltpu.PrefetchScalarGridSpec(
        num_scalar_prefetch=1,
        grid=(nt // pair, ne),
        in_specs=[
            pl.BlockSpec((pair * tt, d), lambda j, e, c: (j, 0)),
            pl.BlockSpec(memory_space=pl.ANY),
            pl.BlockSpec((SUBLANES, pair * tt), lambda j, e, c: (e // SUBLANES, j)),
            pl.BlockSpec((SUBLANES, pair * tt), lambda j, e, c: (e // SUBLANES, j)),
            pl.BlockSpec((1, d, f), lambda j, e, c: (e, 0, 0)),
            pl.BlockSpec((1, d, f), lambda j, e, c: (e, 0, 0)),
            pl.BlockSpec((1, f, d), lambda j, e, c: (e, 0, 0)),
            pl.BlockSpec((1, 1, f), lambda j, e, c: (e, 0, 0)),
            pl.BlockSpec((1, 1, f), lambda j, e, c: (e, 0, 0)),
            pl.BlockSpec((1, 1, d), lambda j, e, c: (e, 0, 0)),
        ],
        out_specs=pl.BlockSpec((pair * tt, d), lambda j, e, c: (j, 0)),
        scratch_shapes=[pltpu.VMEM((pair, MOE_GROUP * MOE_ROWS, tt), BF16),
                        pltpu.VMEM((pair, MOE_GROUP * MOE_ROWS, d), BF16),
                        pltpu.SemaphoreType.DMA(())],
    )
    assert ne % MOE_GROUP == 0
    return pl.pallas_call(
        functools.partial(_moe_kernel, tt=tt, pair=pair),
        grid_spec=grid_spec,
        out_shape=jax.ShapeDtypeStruct((t, d), F32),
        compiler_params=_cp(("arbitrary", "arbitrary")),
        name="moe",
    )(cnt, h2, x1, gt, rt, ew['wg'], ew['wu'], ew['wd'], ew['bg'], ew['bu'], ew['bd'])


def _pad_heads(wmat, n_heads, width):
    d = wmat.shape[0]
    w3 = wmat.reshape(d, n_heads, width)
    return jnp.pad(w3, ((0, 0), (0, 0), (0, LANES - width))).reshape(d, n_heads * LANES)


def _pad_lanes(v, width=LANES):
    v = v.reshape(1, -1)
    return jnp.pad(v, ((0, 0), (0, width - v.shape[1])))


def _rel_bucket(rel):
    half = REL_BUCKETS // 2
    max_exact = half // 2
    n = jnp.abs(rel)
    large = max_exact + (jnp.log(jnp.maximum(n, 1).astype(jnp.float32) / max_exact)
                         / math.log(REL_MAX_DIST / max_exact) * (half - max_exact)).astype(jnp.int32)
    large = jnp.minimum(large, half - 1)
    return jnp.where(rel > 0, half, 0) + jnp.where(n < max_exact, n, large)


def _bias_tiles(rel_bias):
    tk = KEY_TILE
    half = REL_BUCKETS // 2
    max_exact = half // 2
    n_sat = int(math.ceil(max_exact * (REL_MAX_DIST / max_exact) ** ((half - 1 - max_exact) / (half - max_exact)))) + 2
    nd = (n_sat + 2 * tk - 2) // tk + 1
    dd = jnp.arange(nd, dtype=I32)[:, None, None]
    c = jnp.arange(tk, dtype=I32)[None, :, None]
    r = jnp.arange(tk, dtype=I32)[None, None, :]
    bucket = _rel_bucket(c - r - dd * tk)
    onehot = (bucket[..., None] == jnp.arange(REL_BUCKETS, dtype=I32)).astype(F32)
    tiles = jnp.einsum('dcrb,bh->dhcr', onehot, rel_bias.astype(F32) * LOG2E,
                       precision=lax.Precision.HIGHEST)
    return tiles


def _prep_proj(norm1_g, w_in, q_norm_g, k_norm_g, idx_k_norm_g, idx_k_norm_b, d_model):
    ssm_w = d_model // 2
    attn_w = N_HEADS * HEAD_DIM
    kv = N_KV_HEADS * HEAD_DIM
    sizes = [ssm_w, attn_w, kv, kv, IDX_HEADS * IDX_DIM, IDX_DIM, IDX_HEADS, d_model, d_model]
    pts = np.cumsum(sizes)[:-1].tolist()
    wu, wq, wk, wv, wqi, wki, wwi, wga, wgb = jnp.split(w_in, pts, axis=1)
    bf = lambda a: a.astype(BF16)
    wwit = jnp.pad(wwi.T, ((0, 2 * SUBLANES - IDX_HEADS), (0, 0)))
    return dict(
        g1=norm1_g.reshape(1, -1).astype(F32),
        wu=bf(wu), wq=bf(_pad_heads(wq, N_HEADS, HEAD_DIM)), wk=bf(_pad_heads(wk, N_KV_HEADS, HEAD_DIM)),
        wv=bf(_pad_heads(wv, N_KV_HEADS, HEAD_DIM)), wqi=bf(_pad_heads(wqi, IDX_HEADS, IDX_DIM)),
        wki=bf(_pad_heads(wki, 1, IDX_DIM)), wwit=bf(wwit),
        wga=bf(wga), wgb=bf(wgb),
        gq=_pad_lanes(q_norm_g.astype(F32)), gk=_pad_lanes(k_norm_g.astype(F32)),
        gi=_pad_lanes(idx_k_norm_g.astype(F32)), bi=_pad_lanes(idx_k_norm_b.astype(F32)),
        ones_h=jnp.full((LANES, LANES), 1.0 / HEAD_DIM, BF16),
    )


def _prep_s5(lre, lim, log_dt, b_re, b_im, c_re, c_im, dvec, wa, wb):
    g, p = lre.shape
    ch = b_re.shape[-1]
    lam = lax.complex(lre.astype(F32), lim.astype(F32))
    dt = jnp.exp(log_dt.astype(F32))[:, None]
    a_bar = jnp.exp(lam * dt)
    b_bar = ((a_bar - 1.0) / lam)[:, :, None] * lax.complex(b_re.astype(F32), b_im.astype(F32))
    gs = g // S5_DIAG
    eye = jnp.eye(gs, dtype=F32)

    def blocks_in(m):
        return jnp.einsum('jgpc,gh->jgchp', m.reshape(S5_DIAG, gs, p, ch), eye).reshape(S5_DIAG, gs * ch, gs * p)

    def blocks_out(m):
        return jnp.einsum('jgcp,gh->jgphc', m.reshape(S5_DIAG, gs, ch, p), eye).reshape(S5_DIAG, gs * p, gs * ch)

    return dict(
        b_re=blocks_in(jnp.real(b_bar)).astype(BF16), b_im=blocks_in(jnp.imag(b_bar)).astype(BF16),
        c_re=blocks_out(c_re.astype(F32)).astype(BF16), c_im=blocks_out(-c_im.astype(F32)).astype(BF16),
        a_re=jnp.real(a_bar).reshape(1, g * p), a_im=jnp.imag(a_bar).reshape(1, g * p),
        d=dvec.reshape(1, -1).astype(F32), wa=wa.astype(BF16), wb=wb.astype(BF16),
    )


def _prep_merge(w_attn_up, w_out, norm2_g, w_router, b_router):
    d = w_attn_up.shape[1]
    wup = jnp.pad(w_attn_up.reshape(N_HEADS, HEAD_DIM, d), ((0, 0), (0, LANES - HEAD_DIM), (0, 0)))
    wr_t = w_router.astype(F32).T
    wr_hi = wr_t.astype(BF16)
    wr_lo = (wr_t - wr_hi.astype(F32)).astype(BF16)
    return dict(
        wup=wup.reshape(N_HEADS * LANES, d).astype(BF16), wout=w_out.astype(BF16),
        g2=norm2_g.reshape(1, -1).astype(F32), wr_hi=wr_hi, wr_lo=wr_lo,
        br=jnp.broadcast_to(b_router.astype(F32)[:, None], (b_router.shape[0], LANES)),
    )


def _prep_moe(wg, bg, wu, bu, wd, bd):
    return dict(wg=wg.astype(BF16), wu=wu.astype(BF16), wd=wd.astype(BF16),
                bg=bg.astype(F32)[:, None, :], bu=bu.astype(F32)[:, None, :], bd=bd.astype(F32)[:, None, :])


def _pick_tile(n, pref):
    t = min(n, pref)
    while n % t:
        t //= 2
    return t


def _pad_axis(a, axis, size):
    pad = [(0, 0)] * a.ndim
    pad[axis] = (0, size - a.shape[axis])
    return jnp.pad(a, pad)


def _trunk_layer(x, past_k, past_v, past_ik, h0_re, h0_im, pw, sw, mw, ew, bias_tiles):
    bsz, seq, d = x.shape
    t = bsz * seq
    tm = _pick_tile(seq, 512)
    u_tb, q, kp, vp, qi, kip, wt, sga, sgb, kc, vc, kic = _proj(x, pw, bsz, seq, tm)

    half = sw['a_re'].shape[1]
    if h0_re is None:
        h0 = jnp.zeros((bsz, 2 * half), F32)
    else:
        h0 = jnp.concatenate([h0_re.reshape(bsz, half), h0_im.reshape(bsz, half)], axis=1).astype(F32)
    tc = _pick_tile(seq, max(1, S5_ROWS // bsz))
    ya, hout = _s5(u_tb, h0, sw, bsz, seq, tc)
    groups = half // SSM_STATE
    s_re = hout[:, :half].reshape(bsz, groups, SSM_STATE)
    s_im = hout[:, half:].reshape(bsz, groups, SSM_STATE)

    past = 0 if past_k is None else past_k.shape[1]
    n_keys = past + seq
    lk = -(-n_keys // KEY_BLOCK) * KEY_BLOCK
    kip3 = kip.reshape(bsz, seq, LANES)
    if past:
        lane = jnp.arange(LANES)
        pk = jnp.pad(past_k.astype(F32), ((0, 0), (0, 0), (0, 0), (0, LANES - HEAD_DIM)))
        pk = jnp.where(lane == HEAD_DIM, 1.0, pk).astype(BF16)
        pv = jnp.pad(past_v.astype(F32), ((0, 0), (0, 0), (0, 0), (0, LANES - HEAD_DIM)))
        pv = jnp.where(lane == HEAD_DIM, 1.0, pv).astype(BF16)
        pik = jnp.pad(past_ik.astype(F32), ((0, 0), (0, 0), (0, LANES - IDX_DIM))).astype(BF16)
        k_all = jnp.concatenate([pk.transpose(0, 2, 1, 3), kp], axis=2)
        v_all = jnp.concatenate([pv.transpose(0, 2, 1, 3), vp], axis=2)
        ki_all = jnp.concatenate([pik, kip3], axis=1)
    else:
        k_all, v_all, ki_all = kp, vp, kip3
    k_all = _pad_axis(k_all, 2, lk)
    v_all = _pad_axis(v_all, 2, lk)
    ki_all = _pad_axis(ki_all, 1, lk)
    vt_all = v_all.reshape(bsz, N_KV_HEADS, lk // KEY_BLOCK, KEY_BLOCK, LANES).transpose(0, 1, 2, 4, 3)
    tq = KEY_TILE
    seq_q = -(-seq // tq) * tq
    q_p, qi_p, wt_p = _pad_axis(q, 2, seq_q), _pad_axis(qi, 2, seq_q), _pad_axis(wt, 2, seq_q)
    kf = k_all[..., :HEAD_DIM].astype(F32)
    kmax = jnp.sqrt(jnp.max(jnp.sum(kf * kf, axis=-1), axis=-1)).reshape(-1)
    bfar = bias_tiles[-1, :, 0, 0]
    bmax = jnp.max(jnp.abs(bias_tiles), axis=(0, 2, 3))
    stats = jnp.concatenate([kmax, bmax, bfar]).astype(F32)
    bias_tiles = bias_tiles - bfar[None, :, None, None]
    attn = _dsa(stats, q_p, qi_p, wt_p, k_all, vt_all, ki_all, bias_tiles, bsz, seq_q, past, n_keys, tq)
    attn = attn[:, :seq]

    moe_tile = _pick_tile(t, MOE_TILE)
    tm2 = _pick_tile(moe_tile, 512)
    x1, h2, gt, rt, cnt = _merge(x.reshape(t, d), ya, attn.reshape(t, attn.shape[-1]), sga, sgb, mw, tm2, moe_tile)
    cnt_i = cnt[:, :, 0].astype(I32).reshape(-1)
    y = _moe(h2, x1, gt, rt, cnt_i, ew, moe_tile)

    k_new = kc.reshape(bsz, seq, N_KV_HEADS, HEAD_DIM)
    v_new = vc.reshape(bsz, seq, N_KV_HEADS, HEAD_DIM)
    ik_new = kic.reshape(bsz, seq, IDX_DIM)
    return y.reshape(bsz, seq, d), k_new, v_new, ik_new, s_re, s_im


def kernel(x_prompt, x_sample, cache_k, cache_v, cache_idx_k, state_ssm_re, state_ssm_im, rel_bias, norm1_g, w_in, ssm_lambda_re, ssm_lambda_im, ssm_log_dt, ssm_b_re, ssm_b_im, ssm_c_re, ssm_c_im, ssm_d, ssm_w_glu_a, ssm_w_glu_b, q_norm_g, k_norm_g, idx_k_norm_g, idx_k_norm_b, w_attn_up, w_out, norm2_g, moe_w_router, moe_b_router, moe_w_gate, moe_b_gate, moe_w_up, moe_b_up, moe_w_down, moe_b_down):
    depth = w_in.shape[0]
    d_model = x_prompt.shape[-1]
    bias_tiles = _bias_tiles(rel_bias)
    xp, xs = x_prompt, x_sample
    st_p, st_s = [], []
    for l in range(depth):
        pw = _prep_proj(norm1_g[l], w_in[l], q_norm_g[l], k_norm_g[l], idx_k_norm_g[l], idx_k_norm_b[l], d_model)
        sw = _prep_s5(ssm_lambda_re[l], ssm_lambda_im[l], ssm_log_dt[l], ssm_b_re[l], ssm_b_im[l], ssm_c_re[l],
                      ssm_c_im[l], ssm_d[l], ssm_w_glu_a[l], ssm_w_glu_b[l])
        mw = _prep_merge(w_attn_up[l], w_out[l], norm2_g[l], moe_w_router[l], moe_b_router[l])
        ew = _prep_moe(moe_w_gate[l], moe_b_gate[l], moe_w_up[l], moe_b_up[l], moe_w_down[l], moe_b_down[l])
        xp, *sp = _trunk_layer(xp, None, None, None, None, None, pw, sw, mw, ew, bias_tiles)
        xs, *ss = _trunk_layer(xs, cache_k[l], cache_v[l], cache_idx_k[l], state_ssm_re[l], state_ssm_im[l],
                               pw, sw, mw, ew, bias_tiles)
        st_p.append(sp)
        st_s.append(ss)
    outs_p = [jnp.stack([s[i] for s in st_p]) for i in range(5)]
    outs_s = [jnp.stack([s[i] for s in st_s]) for i in range(5)]
    return (xp, xs, *outs_p, *outs_s)
```

```python
import functools
import math

import numpy as np
import jax
import jax.numpy as jnp
from jax import lax
from jax.experimental import pallas as pl
from jax.experimental.pallas import tpu as pltpu

F32 = jnp.float32
BF16 = jnp.bfloat16
I32 = jnp.int32

LANES = 128
SUBLANES = 8
VMEM_LIMIT = 56 * 1024 * 1024

CHUNK = 64
SSM_GROUP_CH = 16
SSM_STATE = 64
N_HEADS = 8
HEAD_DIM = 64
N_KV_HEADS = 2
KV_REP = N_HEADS // N_KV_HEADS
IDX_HEADS = 8
IDX_DIM = 64
TOPK_MAX = 256
REL_BUCKETS = 32
REL_MAX_DIST = 1024
N_EXPERTS = 32
TOP_K = 4
SWIGLU_LIMIT = 7.0
SWIGLU_ALPHA = 1.702
EPS = 1e-6

KEY_TILE = 128
KEY_BLOCK = 256
SCORE_CHUNK = 4
ATTEND_CHUNK = 4
LOG2E = math.log2(math.e)
NEG_BIG = -1e30
SHIFT_LIMIT = 30.0
S5_ROWS = 1024
S5_DIAG = 2
SEARCH_FIRST = 8
SEARCH_GROUP = 4
assert SCORE_CHUNK == 4 and ATTEND_CHUNK == 4
MOE_TILE = 1024
MOE_ROWS = 160
MOE_GROUP = 8
MOE_PAIR = 2


def _cp(sem):
    return pltpu.CompilerParams(dimension_semantics=sem, vmem_limit_bytes=VMEM_LIMIT)


def _dot(a, b):
    return jnp.dot(a, b, preferred_element_type=F32)


def _dot_nt(a, b):
    return lax.dot_general(a, b, (((1,), (1,)), ((), ())), preferred_element_type=F32)


def _dot_tn(a, b):
    return lax.dot_general(a, b, (((0,), (0,)), ((), ())), preferred_element_type=F32)


def _split(a):
    hi = a.astype(BF16)
    lo = (a - hi.astype(F32)).astype(BF16)
    return hi, lo


def _dot_split(a, g):
    hi, lo = _split(a)
    return _dot(hi, g) + _dot(lo, g)


def _proj_kernel(x_ref, g1_ref, wu_ref, wq_ref, wk_ref, wv_ref, wqi_ref, wki_ref, wwit_ref, wga_ref, wgb_ref,
                 gq_ref, gk_ref, gi_ref, bi_ref, ones_h_ref,
                 u_ref, q_ref, kp_ref, vp_ref, qi_ref, kip_ref, wt_ref, sga_ref, sgb_ref,
                 kc_ref, vc_ref, kic_ref):
    x = x_ref[...]
    ms = jnp.mean(x * x, axis=-1, keepdims=True)
    hn = (x * lax.rsqrt(ms + EPS) * g1_ref[...]).astype(BF16)
    ones_h = ones_h_ref[...]
    lane = lax.broadcasted_iota(I32, (x.shape[0], LANES), 1)

    u_ref[...] = _dot(hn, wu_ref[...]).astype(BF16)

    q = _dot(hn, wq_ref[...])
    scale = HEAD_DIM ** -0.5 * LOG2E
    for h in range(N_HEADS):
        qh = q[:, h * LANES:(h + 1) * LANES]
        msq = _dot_split(qh * qh, ones_h)
        q_ref[h] = (qh * lax.rsqrt(msq + EPS) * (gq_ref[...] * scale)).astype(BF16)

    k = _dot(hn, wk_ref[...])
    for g in range(N_KV_HEADS):
        kg = k[:, g * LANES:(g + 1) * LANES]
        msk = _dot_split(kg * kg, ones_h)
        kn = kg * lax.rsqrt(msk + EPS) * gk_ref[...]
        kp_ref[g] = jnp.where(lane == HEAD_DIM, 1.0, kn).astype(BF16)
        kc_ref[:, g, :] = kn[:, :HEAD_DIM]

    v = _dot(hn, wv_ref[...])
    for g in range(N_KV_HEADS):
        vg = v[:, g * LANES:(g + 1) * LANES]
        vp_ref[g] = jnp.where(lane == HEAD_DIM, 1.0, vg).astype(BF16)
        vc_ref[:, g, :] = vg[:, :HEAD_DIM]

    qi = _dot(hn, wqi_ref[...])
    for h in range(IDX_HEADS):
        qi_ref[h] = qi[:, h * LANES:(h + 1) * LANES].astype(BF16)

    ki = _dot(hn, wki_ref[...])
    mu = _dot_split(ki, ones_h)
    xc = jnp.where(lane < IDX_DIM, ki - mu, 0.0)
    var = _dot_split(xc * xc, ones_h)
    kin = xc * lax.rsqrt(var + EPS) * gi_ref[...] + bi_ref[...]
    kip_ref[...] = kin.astype(BF16)
    kic_ref[...] = kin[:, :IDX_DIM]

    wt = _dot_nt(wwit_ref[...], hn)
    wt_ref[...] = wt[0:IDX_HEADS, :] * (IDX_HEADS ** -0.5 * IDX_DIM ** -0.5)

    sga_ref[...] = jax.nn.sigmoid(_dot(hn, wga_ref[...])).astype(BF16)
    sgb_ref[...] = jax.nn.sigmoid(_dot(hn, wgb_ref[...])).astype(BF16)


def _proj(x, pw, bsz, seq, tm):
    d = x.shape[-1]
    nt = seq // tm
    t = bsz * seq
    x2 = x.reshape(t, d)

    def tok(b, i):
        return (b * nt + i, 0)

    def cst(b, i):
        return (0, 0)

    def wspec(a):
        return pl.BlockSpec(a.shape, cst)

    weights = [pw['g1'], pw['wu'], pw['wq'], pw['wk'], pw['wv'], pw['wqi'], pw['wki'], pw['wwit'], pw['wga'],
               pw['wgb'], pw['gq'], pw['gk'], pw['gi'], pw['bi'], pw['ones_h']]
    ssm_w = pw['wu'].shape[1]
    out_shape = (
        jax.ShapeDtypeStruct((t, ssm_w), BF16),
        jax.ShapeDtypeStruct((bsz, N_HEADS, seq, LANES), BF16),
        jax.ShapeDtypeStruct((bsz, N_KV_HEADS, seq, LANES), BF16),
        jax.ShapeDtypeStruct((bsz, N_KV_HEADS, seq, LANES), BF16),
        jax.ShapeDtypeStruct((bsz, IDX_HEADS, seq, LANES), BF16),
        jax.ShapeDtypeStruct((t, LANES), BF16),
        jax.ShapeDtypeStruct((bsz, IDX_HEADS, seq), F32),
        jax.ShapeDtypeStruct((t, d), BF16),
        jax.ShapeDtypeStruct((t, d), BF16),
        jax.ShapeDtypeStruct((t, N_KV_HEADS, HEAD_DIM), F32),
        jax.ShapeDtypeStruct((t, N_KV_HEADS, HEAD_DIM), F32),
        jax.ShapeDtypeStruct((t, IDX_DIM), F32),
    )

    def hm(nh):
        return pl.BlockSpec((None, nh, tm, LANES), lambda b, i: (b, 0, i, 0))

    out_specs = (
        pl.BlockSpec((tm, ssm_w), tok),
        hm(N_HEADS), hm(N_KV_HEADS), hm(N_KV_HEADS), hm(IDX_HEADS),
        pl.BlockSpec((tm, LANES), tok),
        pl.BlockSpec((None, IDX_HEADS, tm), lambda b, i: (b, 0, i)),
        pl.BlockSpec((tm, d), tok), pl.BlockSpec((tm, d), tok),
        pl.BlockSpec((tm, N_KV_HEADS, HEAD_DIM), lambda b, i: (b * nt + i, 0, 0)),
        pl.BlockSpec((tm, N_KV_HEADS, HEAD_DIM), lambda b, i: (b * nt + i, 0, 0)),
        pl.BlockSpec((tm, IDX_DIM), tok),
    )
    return pl.pallas_call(
        _proj_kernel,
        grid=(bsz, nt),
        in_specs=[pl.BlockSpec((tm, d), tok)] + [wspec(a) for a in weights],
        out_specs=out_specs,
        out_shape=out_shape,
        compiler_params=_cp(("arbitrary", "arbitrary")),
        name="proj",
    )(x2, *weights)


def _gelu_tanh(x):
    return 0.5 * x * (1.0 + jnp.tanh(math.sqrt(2.0 / math.pi) * (x + 0.044715 * (x * x * x))))


def _s5_kernel(u_ref, h0_ref, bre_ref, bim_ref, are_ref, aim_ref, cre_ref, cim_ref, dvec_ref, wa_ref, wb_ref,
               ya_ref, hout_ref, state_ref, bu_ref, yf_ref, ug_ref, *, bsz, tc, strip):
    s = pl.program_id(0)
    half = are_ref.shape[1]

    @pl.when(s == 0)
    def _():
        state_ref[...] = h0_ref[...]

    ssm_w = dvec_ref.shape[1]
    for b in range(bsz):
        for c in range(ssm_w // LANES):
            ug_ref[c, pl.ds(b, tc, stride=bsz), :] = u_ref[b, :, c * LANES:(c + 1) * LANES].astype(F32)
    u = jnp.concatenate([ug_ref[c] for c in range(ssm_w // LANES)], axis=1).astype(BF16)
    cw = u.shape[1] // S5_DIAG
    sw = half // S5_DIAG
    for j in range(S5_DIAG):
        uj = u[:, j * cw:(j + 1) * cw]
        bu_ref[:, j * sw:(j + 1) * sw] = _dot(uj, bre_ref[j])
        bu_ref[:, half + j * sw:half + (j + 1) * sw] = _dot(uj, bim_ref[j])

    for c0 in range(0, half, strip):
        ar = jnp.broadcast_to(are_ref[:, c0:c0 + strip], (bsz, strip))
        ai = jnp.broadcast_to(aim_ref[:, c0:c0 + strip], (bsz, strip))
        hr0 = state_ref[:, c0:c0 + strip]
        hi0 = state_ref[:, half + c0:half + c0 + strip]

        def step(t, carry):
            hr, hi = carry
            r0 = pl.multiple_of(t * bsz, bsz)
            br = bu_ref[pl.ds(r0, bsz), c0:c0 + strip]
            bi = bu_ref[pl.ds(r0, bsz), half + c0:half + c0 + strip]
            nr = ar * hr - ai * hi + br
            ni = ar * hi + ai * hr + bi
            bu_ref[pl.ds(r0, bsz), c0:c0 + strip] = nr
            bu_ref[pl.ds(r0, bsz), half + c0:half + c0 + strip] = ni
            return nr, ni

        hr, hi = lax.fori_loop(0, tc, step, (hr0, hi0))
        state_ref[:, c0:c0 + strip] = hr
        state_ref[:, half + c0:half + c0 + strip] = hi

    ys = []
    for j in range(S5_DIAG):
        s_re = bu_ref[:, j * sw:(j + 1) * sw].astype(BF16)
        s_im = bu_ref[:, half + j * sw:half + (j + 1) * sw].astype(BF16)
        ys.append(_dot(s_re, cre_ref[j]) + _dot(s_im, cim_ref[j]))
    y = jnp.concatenate(ys, axis=1) + dvec_ref[...] * u.astype(F32)
    g = _gelu_tanh(y).astype(BF16)
    ya = _dot(g, wa_ref[...]) * jax.nn.sigmoid(_dot(g, wb_ref[...]))
    n_chunk = ya.shape[1] // LANES
    for c in range(n_chunk):
        yf_ref[c] = ya[:, c * LANES:(c + 1) * LANES]
    for b in range(bsz):
        ya_ref[b] = jnp.concatenate([yf_ref[c, pl.ds(b, tc, stride=bsz), :] for c in range(n_chunk)],
                                    axis=1).astype(BF16)

    @pl.when(s == pl.num_programs(0) - 1)
    def _():
        hout_ref[...] = state_ref[...]


def _s5(u, h0, sw, bsz, seq, tc):
    rows = tc * bsz
    ssm_w = sw['d'].shape[1]
    half = sw['a_re'].shape[1]
    two_half = 2 * half
    d = sw['wa'].shape[1]
    u3 = u.reshape(bsz, seq, ssm_w)
    strip = min(512, half)

    consts = [h0, sw['b_re'], sw['b_im'], sw['a_re'], sw['a_im'], sw['c_re'], sw['c_im'], sw['d'], sw['wa'], sw['wb']]

    def cst(s):
        return (0, 0)

    def cspec(a):
        return pl.BlockSpec(a.shape, lambda s: (0,) * a.ndim)

    ya, hout = pl.pallas_call(
        functools.partial(_s5_kernel, bsz=bsz, tc=tc, strip=strip),
        grid=(seq // tc,),
        in_specs=[pl.BlockSpec((bsz, tc, ssm_w), lambda s: (0, s, 0))] + [cspec(a) for a in consts],
        out_specs=(pl.BlockSpec((bsz, tc, d), lambda s: (0, s, 0)), pl.BlockSpec((bsz, two_half), cst)),
        out_shape=(jax.ShapeDtypeStruct((bsz, seq, d), BF16), jax.ShapeDtypeStruct((bsz, two_half), F32)),
        scratch_shapes=[pltpu.VMEM((bsz, two_half), F32), pltpu.VMEM((rows, two_half), F32),
                        pltpu.VMEM((d // LANES, rows, LANES), F32), pltpu.VMEM((ssm_w // LANES, rows, LANES), F32)],
        compiler_params=_cp(("arbitrary",)),
        name="s5",
    )(u3, *consts)
    return ya.reshape(bsz * seq, d), hout


def _f2key(x):
    b = lax.bitcast_convert_type(x, I32)
    return b ^ ((b >> 31) & 0x7FFFFFFF)


def _key2f(k):
    return lax.bitcast_convert_type(k ^ ((k >> 31) & 0x7FFFFFFF), F32)


def _dsa_kernel(st_ref, q_ref, qi_ref, wt_ref, k_ref, vt_ref, ki_ref, bias_ref, o_ref,
                s_ref, lo_ref, hi_ref, clo_ref, glo_ref, ghi_ref, side_ref, q2_ref, mrow_ref, acc_ref,
                *, bsz, tq, past, n_keys, topk, nkt, nd, near_max):
    b_id = pl.program_id(0)
    i = pl.program_id(1)
    kb = KEY_BLOCK
    sl = SUBLANES
    q0 = past + i * tq
    last_chunk = (q0 + tq - 1) // CHUNK
    n_kt = jnp.minimum(nkt, ((last_chunk + 1) * CHUNK + kb - 1) // kb)
    d0 = q0 // KEY_TILE

    krow = lax.broadcasted_iota(I32, (kb, tq), 0)
    q_chunk = (q0 + lax.broadcasted_iota(I32, (kb, tq), 1)) // CHUNK
    qc8 = (q0 + lax.broadcasted_iota(I32, (sl, tq), 1)) // CHUNK
    n_adm = jnp.minimum((qc8 + 1) * CHUNK, n_keys)
    n_admf = n_adm.astype(F32)
    is_pad = q0 + lax.broadcasted_iota(I32, (sl, tq), 1) >= n_keys
    needf = jnp.where(is_pad, n_adm, jnp.minimum(topk, n_adm)).astype(F32)

    def bcast(x):
        return jnp.broadcast_to(x[0:1, :], (kb, tq))

    def rep(x):
        return jnp.broadcast_to(x, (sl, tq))

    qi = qi_ref[...].reshape(IDX_HEADS * tq, LANES)

    def score_blocks(kt, nb, masked):
        k0 = pl.multiple_of(kt * kb, kb)
        s = _dot_nt(ki_ref[pl.ds(k0, nb * kb), :], qi)
        for j in range(nb):
            sc = jnp.zeros((kb, tq), F32)
            for h in range(IDX_HEADS):
                sc = sc + wt_ref[h:h + 1, :] * jnp.maximum(s[j * kb:(j + 1) * kb, h * tq:(h + 1) * tq], 0.0)
            if masked:
                kpos = k0 + j * kb + krow
                adm = ((kpos // CHUNK) <= q_chunk) & (kpos < n_keys)
                sc = jnp.where(adm, sc, -jnp.inf)
            s_ref[kt + j] = sc

    n_open = n_kt - 1

    def score_chunk(j, c):
        score_blocks(SCORE_CHUNK * j, SCORE_CHUNK, False)
        return c

    lax.fori_loop(0, n_open // SCORE_CHUNK, score_chunk, 0)
    rest = n_open % SCORE_CHUNK

    @pl.when(rest >= 2)
    def _():
        score_blocks(n_open - rest, 2, False)

    @pl.when(rest % 2 == 1)
    def _():
        score_blocks(n_open - 1, 1, False)
    score_blocks(n_kt - 1, 1, True)
    s_ref[n_kt] = jnp.full((kb, tq), -jnp.inf, F32)
    n_pair = (n_kt + 1) // 2

    part = 4 * sl

    def fold(x, op):
        x = x.reshape(kb // part, part, tq)
        acc = x[0]
        for j in range(1, kb // part):
            acc = op(acc, x[j])
        return acc

    def count(pred):
        def one(kt):
            return fold(jnp.where(pred(s_ref[kt], kt), 1.0, 0.0), jnp.add)
        c = lax.fori_loop(0, n_pair, lambda j, c: c + (one(2 * j) + one(2 * j + 1)), jnp.zeros((part, tq), F32))
        return rep(jnp.sum(c, axis=0, keepdims=True))

    def minmax(j, c):
        mx, mn = c
        for kt in (2 * j, 2 * j + 1):
            s = s_ref[kt]
            mx = jnp.maximum(mx, fold(s, jnp.maximum))
            mn = jnp.minimum(mn, fold(jnp.where(s == -jnp.inf, jnp.inf, s), jnp.minimum))
        return mx, mn

    mx, mn = lax.fori_loop(0, n_pair, minmax,
                           (jnp.full((part, tq), -jnp.inf, F32), jnp.full((part, tq), jnp.inf, F32)))
    lo_ref[...] = rep(jnp.min(mn, axis=0, keepdims=True))
    hi_ref[...] = _key2f(_f2key(rep(jnp.max(mx, axis=0, keepdims=True))) + 1)
    def odds(cnt):
        c = jnp.clip(cnt, 0.5, n_admf - 0.5)
        return jnp.log((n_admf - c) / c)

    target = odds(needf - 0.5)
    clo_ref[...] = n_admf
    glo_ref[...] = target - odds(n_admf)
    ghi_ref[...] = target - odds(jnp.zeros((sl, tq), F32))
    side_ref[...] = jnp.zeros((sl, tq), F32)

    def searching(lo, hi, clo):
        return (_f2key(hi) > _f2key(lo) + 1) & (clo > needf)

    def refine(it, c):
        lo, hi, clo = lo_ref[...], hi_ref[...], clo_ref[...]
        glo, ghi, side = glo_ref[...], ghi_ref[...], side_ref[...]
        k_t = _f2key(lo + (hi - lo) * (glo / (glo - ghi)))
        t = _key2f(jnp.minimum(jnp.maximum(k_t, _f2key(lo) + 1), _f2key(hi) - 1))
        tb = bcast(t)
        cnt = count(lambda s, kt: s >= tb)
        g = target - odds(cnt)
        open_ = searching(lo, hi, clo)
        up = open_ & (cnt >= needf)
        dn = open_ & (cnt < needf)
        lo_ref[...] = jnp.where(up, t, lo)
        clo_ref[...] = jnp.where(up, cnt, clo)
        hi_ref[...] = jnp.where(dn, t, hi)
        glo_ref[...] = jnp.where(up, g, jnp.where(dn & (side < 0.0), glo * 0.5, glo))
        ghi_ref[...] = jnp.where(dn, g, jnp.where(up & (side > 0.0), ghi * 0.5, ghi))
        side_ref[...] = jnp.where(up, 1.0, jnp.where(dn, -1.0, side))
        return c

    def snap():
        lo, hi, clo = lo_ref[...], hi_ref[...], clo_ref[...]
        lo_b, hi_b = bcast(lo), bcast(hi)

        def body(j, c):
            a, b = c
            for kt in (2 * j, 2 * j + 1):
                s = s_ref[kt]
                a = jnp.minimum(a, fold(jnp.where(s >= lo_b, s, jnp.inf), jnp.minimum))
                b = jnp.maximum(b, fold(jnp.where(s < hi_b, s, -jnp.inf), jnp.maximum))
            return a, b

        a, b = lax.fori_loop(0, n_pair, body,
                             (jnp.full((part, tq), jnp.inf, F32), jnp.full((part, tq), -jnp.inf, F32)))
        open_ = searching(lo, hi, clo)
        lo_ref[...] = jnp.where(open_, rep(jnp.min(a, axis=0, keepdims=True)), lo)
        hi_ref[...] = jnp.where(open_, _key2f(_f2key(rep(jnp.max(b, axis=0, keepdims=True))) + 1), hi)

    def n_searching():
        return jnp.max(jnp.where(searching(lo_ref[...], hi_ref[...], clo_ref[...]), 1.0, 0.0))

    lax.fori_loop(0, SEARCH_FIRST, refine, 0)
    snap()
    lax.fori_loop(0, SEARCH_GROUP, refine, 0)
    snap()

    def group(c):
        lax.fori_loop(0, SEARCH_GROUP, refine, 0)
        snap()
        return n_searching()

    lax.while_loop(lambda c: c > 0.0, group, n_searching())
    thr = lo_ref[...]
    thr_b = bcast(thr)

    n_tied = jnp.max(jnp.where(clo_ref[...] > needf, 1.0, 0.0))

    @pl.when(n_tied > 0.0)
    def _():
        rem_b = bcast(needf - count(lambda s, kt: s > thr_b))
        tri = jnp.where(lax.broadcasted_iota(I32, (kb, kb), 0) >= lax.broadcasted_iota(I32, (kb, kb), 1),
                        1.0, 0.0).astype(BF16)

        def drop(kt, seen):
            s = s_ref[kt]
            tie = s == thr_b
            rank = _dot(tri, jnp.where(tie, 1.0, 0.0).astype(BF16)) + bcast(seen)
            s_ref[kt] = jnp.where(tie & (rank > rem_b), -jnp.inf, s)
            return rep(rank[kb - 1:kb, :])

        lax.fori_loop(0, n_kt, drop, jnp.zeros((sl, tq), F32))

    rows_g = KV_REP * tq
    qf = q_ref[...].reshape(N_HEADS * tq, LANES).astype(F32)
    qn = jnp.sqrt(jnp.sum(qf * qf, axis=1, keepdims=True))
    lane = lax.broadcasted_iota(I32, (tq, LANES), 1)
    worst = jnp.float32(0.0)
    for h in range(N_HEADS):
        kmax = st_ref[b_id * N_KV_HEADS + h // KV_REP]
        bmax = st_ref[bsz * N_KV_HEADS + h]
        bfar = st_ref[bsz * N_KV_HEADS + N_HEADS + h]
        bound = qn[h * tq:(h + 1) * tq, :] * (kmax * 1.01) + (bmax + 0.1)
        worst = jnp.maximum(worst, jnp.max(bound))
        q2_ref[h * tq:(h + 1) * tq, :] = jnp.where(lane == HEAD_DIM, bfar - bound,
                                                   qf[h * tq:(h + 1) * tq, :]).astype(BF16)
    n_far = jnp.clip((d0 - nd) // 2 + 1, 0, n_kt)

    def select_masks(kt0, nb):
        return [jnp.where(s_ref[kt0 + j] >= thr_b, 0.0, NEG_BIG) for j in range(nb)]

    def logits(kt0, nb, g, near, exact, masks):
        k0 = pl.multiple_of(kt0 * kb, kb)
        s = _dot_nt(k_ref[g, pl.ds(k0, nb * kb), :], q2_ref[g * rows_g:(g + 1) * rows_g, :])
        out = []
        for j in range(nb):
            kt = kt0 + j
            maskadd = masks[j]
            if near:
                da = jnp.clip(d0 - 2 * kt, 0, nd - 1)
                db = jnp.clip(d0 - 2 * kt - 1, 0, nd - 1)
            parts = []
            for r in range(KV_REP):
                h = g * KV_REP + r
                add = maskadd - mrow_ref[0:1, h * tq:(h + 1) * tq] if exact else maskadd
                if near:
                    add = jnp.concatenate([bias_ref[da, h], bias_ref[db, h]], axis=0) + add
                parts.append(s[j * kb:(j + 1) * kb, r * tq:(r + 1) * tq] + add)
            out.append(jnp.concatenate(parts, axis=1))
        return out

    def over_blocks(fn):
        def far_chunk(j, c):
            fn(ATTEND_CHUNK * j, ATTEND_CHUNK, False)
            return c
        lax.fori_loop(0, n_far // ATTEND_CHUNK, far_chunk, 0)
        rest = n_far % ATTEND_CHUNK

        @pl.when(rest >= 2)
        def _():
            fn(n_far - rest, 2, False)

        @pl.when(rest % 2 == 1)
        def _():
            fn(n_far - 1, 1, False)

        n_near = n_kt - n_far
        for nb in range(1, near_max + 1):
            @pl.when(n_near == nb)
            def _(nb=nb):
                fn(n_far, nb, True)

    def attend(exact):
        acc_ref[...] = jnp.zeros(acc_ref.shape, F32)

        def blocks(kt0, nb, near):
            masks = select_masks(kt0, nb)
            for g in range(N_KV_HEADS):
                p = jnp.concatenate([jnp.exp2(lg).astype(BF16) for lg in logits(kt0, nb, g, near, exact, masks)],
                                    axis=0)
                vt = jnp.concatenate([vt_ref[g, kt0 + j] for j in range(nb)], axis=1)
                acc_ref[g] += _dot(vt, p)
        over_blocks(blocks)

    @pl.when(worst <= SHIFT_LIMIT)
    def _():
        attend(False)

    @pl.when(worst > SHIFT_LIMIT)
    def _():
        mrow_ref[...] = jnp.full(mrow_ref.shape, NEG_BIG, F32)

        def blocks(kt0, nb, near):
            masks = select_masks(kt0, nb)
            for g in range(N_KV_HEADS):
                for lg in logits(kt0, nb, g, near, False, masks):
                    mx = jnp.max(lg, axis=0, keepdims=True)
                    cur = mrow_ref[:, g * rows_g:(g + 1) * rows_g]
                    mrow_ref[:, g * rows_g:(g + 1) * rows_g] = jnp.maximum(cur, jnp.broadcast_to(mx, (sl, rows_g)))
        over_blocks(blocks)
        attend(True)

    for g in range(N_KV_HEADS):
        acc = acc_ref[g]
        og = acc / acc[HEAD_DIM:HEAD_DIM + 1, :]
        for r in range(KV_REP):
            h = g * KV_REP + r
            o_ref[:, h * LANES:(h + 1) * LANES] = og[:, r * tq:(r + 1) * tq].T.astype(BF16)


def _dsa(stats, q, qi, wt, k_all, vt_all, ki_all, bias_tiles, bsz, seq, past, n_keys, tq):
    lk = k_all.shape[2]
    nkt = lk // KEY_BLOCK
    topk = min(TOPK_MAX, n_keys // 4)
    nd = bias_tiles.shape[0]
    nq = seq // tq
    assert past % KEY_TILE == 0 and tq == KEY_TILE
    near_max = 0
    for i in range(nq):
        q0 = past + i * tq
        n_kt = min(nkt, (((q0 + tq - 1) // CHUNK + 1) * CHUNK + KEY_BLOCK - 1) // KEY_BLOCK)
        n_far = min(max((q0 // KEY_TILE - nd) // 2 + 1, 0), n_kt)
        near_max = max(near_max, n_kt - n_far)
    kern = functools.partial(_dsa_kernel, bsz=bsz, tq=tq, past=past, n_keys=n_keys, topk=topk, nkt=nkt, nd=nd,
                             near_max=near_max)
    row_state = pltpu.VMEM((SUBLANES, tq), F32)
    grid_spec = pltpu.PrefetchScalarGridSpec(
        num_scalar_prefetch=1,
        grid=(bsz, nq),
        in_specs=[
            pl.BlockSpec((None, N_HEADS, tq, LANES), lambda b, i, s: (b, 0, i, 0)),
            pl.BlockSpec((None, IDX_HEADS, tq, LANES), lambda b, i, s: (b, 0, i, 0)),
            pl.BlockSpec((None, IDX_HEADS, tq), lambda b, i, s: (b, 0, i)),
            pl.BlockSpec((None, N_KV_HEADS, lk, LANES), lambda b, i, s: (b, 0, 0, 0)),
            pl.BlockSpec((None, N_KV_HEADS, nkt, LANES, KEY_BLOCK), lambda b, i, s: (b, 0, 0, 0, 0)),
            pl.BlockSpec((None, lk, LANES), lambda b, i, s: (b, 0, 0)),
            pl.BlockSpec(bias_tiles.shape, lambda b, i, s: (0, 0, 0, 0)),
        ],
        out_specs=pl.BlockSpec((None, tq, N_HEADS * LANES), lambda b, i, s: (b, i, 0)),
        scratch_shapes=[
            pltpu.VMEM((nkt + 1, KEY_BLOCK, tq), F32),
            row_state, row_state, row_state, row_state, row_state, row_state,
            pltpu.VMEM((N_HEADS * tq, LANES), BF16),
            pltpu.VMEM((SUBLANES, N_HEADS * tq), F32),
            pltpu.VMEM((N_KV_HEADS, LANES, KV_REP * tq), F32),
        ],
    )
    return pl.pallas_call(
        kern,
        grid_spec=grid_spec,
        out_shape=jax.ShapeDtypeStruct((bsz, seq, N_HEADS * LANES), BF16),
        compiler_params=_cp(("arbitrary", "arbitrary")),
        name="dsa",
    )(stats, q, qi, wt, k_all, vt_all, ki_all, bias_tiles)


def _merge_kernel(x_ref, ya_ref, at_ref, sga_ref, sgb_ref, wup_ref, wout_ref, g2_ref, wr_hi_ref, wr_lo_ref, br_ref,
                  x1_ref, h2_ref, gt_ref, rt_ref, cnt_ref, run_ref, *, tm, sub):
    step = pl.program_id(0)

    @pl.when(step % sub == 0)
    def _():
        run_ref[...] = jnp.zeros(run_ref.shape, F32)

    yb = _dot(at_ref[...], wup_ref[...])
    merged = sga_ref[...].astype(F32) * ya_ref[...].astype(F32) + sgb_ref[...].astype(F32) * yb
    x1 = x_ref[...] + _dot(merged.astype(BF16), wout_ref[...])
    x1_ref[...] = x1
    ms = jnp.mean(x1 * x1, axis=-1, keepdims=True)
    h2 = x1 * lax.rsqrt(ms + EPS) * g2_ref[...]
    h2_hi, h2_lo = _split(h2)
    h2_ref[...] = h2_hi

    wr_hi = wr_hi_ref[...]
    logit = (_dot_nt(wr_hi, h2_hi) + _dot_nt(wr_hi, h2_lo) + _dot_nt(wr_lo_ref[...], h2_hi)) + br_ref[:, 0:1]
    ne = logit.shape[0]
    eid = lax.broadcasted_iota(I32, (ne, tm), 0).astype(F32)
    selb = jnp.zeros((ne, tm), F32)
    tops = []
    picks = []
    for _ in range(TOP_K):
        mx = jnp.max(logit, axis=0, keepdims=True)
        pick = jnp.min(jnp.where(logit == mx, eid, float(ne)), axis=0, keepdims=True)
        hit = eid == pick
        selb = jnp.where(hit, 1.0, selb)
        logit = jnp.where(hit, -jnp.inf, logit)
        tops.append(mx)
        picks.append(hit)
    ex = [jnp.exp(t - tops[0]) for t in tops]
    den = ex[0] + ex[1] + ex[2] + ex[3]
    gate = jnp.zeros((ne, tm), F32)
    for hit, e in zip(picks, ex):
        gate = jnp.where(hit, e / den, gate)
    gt_ref[...] = gate

    sel = selb > 0.5
    selb = selb.astype(BF16)
    r_i = lax.broadcasted_iota(I32, (tm, tm), 0)
    c_i = lax.broadcasted_iota(I32, (tm, tm), 1)
    tri = jnp.where(r_i < c_i, 1.0, 0.0).astype(BF16)
    run = run_ref[...]
    rank = _dot(selb, tri) + jnp.broadcast_to(run[:, 0:1], (ne, tm))
    rt_ref[...] = jnp.where(sel, rank, -1.0)
    run = run + _dot(selb, jnp.ones((tm, LANES), BF16))
    run_ref[...] = run
    cnt_ref[...] = run


def _merge(x2, ya, attn, sga, sgb, mw, tm, moe_tile):
    t, d = x2.shape
    sub = moe_tile // tm
    ne = mw['wr_hi'].shape[0]

    def tok(i):
        return (i, 0)

    def cst(i):
        return (0, 0)

    consts = [mw['wup'], mw['wout'], mw['g2'], mw['wr_hi'], mw['wr_lo'], mw['br']]
    return pl.pallas_call(
        functools.partial(_merge_kernel, tm=tm, sub=sub),
        grid=(t // tm,),
        in_specs=[
            pl.BlockSpec((tm, d), tok),
            pl.BlockSpec((tm, d), tok),
            pl.BlockSpec((tm, attn.shape[-1]), tok),
            pl.BlockSpec((tm, d), tok),
            pl.BlockSpec((tm, d), tok),
        ] + [pl.BlockSpec(a.shape, cst) for a in consts],
        out_specs=(
            pl.BlockSpec((tm, d), tok),
            pl.BlockSpec((tm, d), tok),
            pl.BlockSpec((ne, tm), lambda i: (0, i)),
            pl.BlockSpec((ne, tm), lambda i: (0, i)),
            pl.BlockSpec((None, ne, LANES), lambda i: (i // sub, 0, 0)),
        ),
        out_shape=(
            jax.ShapeDtypeStruct((t, d), F32),
            jax.ShapeDtypeStruct((t, d), BF16),
            jax.ShapeDtypeStruct((ne, t), F32),
            jax.ShapeDtypeStruct((ne, t), F32),
            jax.ShapeDtypeStruct((t // moe_tile, ne, LANES), F32),
        ),
        scratch_shapes=[pltpu.VMEM((ne, LANES), F32)],
        compiler_params=_cp(("arbitrary",)),
        name="merge",
    )(x2, ya, attn, sga, sgb, *consts)


def _moe_kernel(cnt_ref, h2_ref, x1_hbm, gt_ref, rt_ref, wg_ref, wu_ref, wd_ref, bg_ref, bu_ref, bd_ref, y_ref,
                pg_ref, og_ref, sem, *, tt, pair):
    j = pl.program_id(0)
    e = pl.program_id(1)
    ne = pl.num_programs(1)
    rb = MOE_ROWS
    slot = e % MOE_GROUP

    @pl.when(e == 0)
    def _():
        cp = pltpu.make_async_copy(x1_hbm.at[pl.ds(pl.multiple_of(j * pair * tt, tt), pair * tt), :], y_ref, sem)
        cp.start()
        cp.wait()

    mine = lax.broadcasted_iota(I32, (SUBLANES, pair * tt), 0) == e % SUBLANES
    g_all = jnp.sum(jnp.where(mine, gt_ref[...], 0.0), axis=0, keepdims=True)
    r_all = jnp.sum(jnp.where(mine, rt_ref[...], 0.0), axis=0, keepdims=True)
    rid = lax.broadcasted_iota(I32, (rb, tt), 0).astype(F32)

    def one_hot(s, blk):
        return jnp.broadcast_to(r_all[:, s * tt:(s + 1) * tt], (rb, tt)) == (rid + (blk * rb).astype(F32))

    def gather(s, hit):
        p = jnp.where(hit, 1.0, 0.0).astype(BF16)
        return p, _dot(p, h2_ref[s * tt:(s + 1) * tt, :]).astype(BF16)

    def expert(xg):
        a = jnp.minimum(_dot(xg, wg_ref[0]) + bg_ref[0], SWIGLU_LIMIT)
        b = jnp.clip(_dot(xg, wu_ref[0]) + bu_ref[0], -SWIGLU_LIMIT, SWIGLU_LIMIT)
        hid = a * jax.nn.sigmoid(SWIGLU_ALPHA * a) * (b + 1.0)
        return _dot(hid.astype(BF16), wd_ref[0]) + bd_ref[0]

    def gated(s, hit, o):
        g_row = jnp.broadcast_to(g_all[:, s * tt:(s + 1) * tt], (rb, tt))
        return (o * jnp.sum(jnp.where(hit, g_row, 0.0), axis=1, keepdims=True)).astype(BF16)

    hits = [one_hot(s, jnp.int32(0)) for s in range(pair)]
    gathered = [gather(s, hits[s]) for s in range(pair)]
    o = expert(jnp.concatenate([xg for _, xg in gathered], axis=0))
    r0 = pl.multiple_of(slot * rb, rb)
    for s in range(pair):
        pg_ref[s, pl.ds(r0, rb), :] = gathered[s][0]
        og_ref[s, pl.ds(r0, rb), :] = gated(s, hits[s], o[s * rb:(s + 1) * rb])

    @pl.when(slot == MOE_GROUP - 1)
    def _():
        for s in range(pair):
            y_ref[s * tt:(s + 1) * tt, :] += _dot_tn(pg_ref[s], og_ref[s])

    for s in range(pair):
        def overflow(blk, c, s=s):
            hit = one_hot(s, blk)
            p, xg = gather(s, hit)
            y_ref[s * tt:(s + 1) * tt, :] += _dot_tn(p, gated(s, hit, expert(xg)))
            return c

        n_rows = cnt_ref[(j * pair + s) * ne + e]
        lax.fori_loop(1, (n_rows + rb - 1) // rb, overflow, 0)


def _moe(h2, x1, gt, rt, cnt, ew, tt):
    t, d = h2.shape
    ne = gt.shape[0]
    nt = t // tt
    f = ew['wg'].shape[-1]
    pair = MOE_PAIR if nt % MOE_PAIR == 0 else 1
    grid_spec = pltpu.PrefetchScalarGridSpec(
        num_scalar_prefetch=1,
        grid=(nt // pair, ne),
        in_specs=[
            pl.BlockSpec((pair * tt, d), lambda j, e, c: (j, 0)),
            pl.BlockSpec(memory_space=pl.ANY),
            pl.BlockSpec((SUBLANES, pair * tt), lambda j, e, c: (e // SUBLANES, j)),
            pl.BlockSpec((SUBLANES, pair * tt), lambda j, e, c: (e // SUBLANES, j)),
            pl.BlockSpec((1, d, f), lambda j, e, c: (e, 0, 0)),
            pl.BlockSpec((1, d, f), lambda j, e, c: (e, 0, 0)),
            pl.BlockSpec((1, f, d), lambda j, e, c: (e, 0, 0)),
            pl.BlockSpec((1, 1, f), lambda j, e, c: (e, 0, 0)),
            pl.BlockSpec((1, 1, f), lambda j, e, c: (e, 0, 0)),
            pl.BlockSpec((1, 1, d), lambda j, e, c: (e, 0, 0)),
        ],
        out_specs=pl.BlockSpec((pair * tt, d), lambda j, e, c: (j, 0)),
        scratch_shapes=[pltpu.VMEM((pair, MOE_GROUP * MOE_ROWS, tt), BF16),
                        pltpu.VMEM((pair, MOE_GROUP * MOE_ROWS, d), BF16),
                        pltpu.SemaphoreType.DMA(())],
    )
    assert ne % MOE_GROUP == 0
    return pl.pallas_call(
        functools.partial(_moe_kernel, tt=tt, pair=pair),
        grid_spec=grid_spec,
        out_shape=jax.ShapeDtypeStruct((t, d), F32),
        compiler_params=_cp(("arbitrary", "arbitrary")),
        name="moe",
    )(cnt, h2, x1, gt, rt, ew['wg'], ew['wu'], ew['wd'], ew['bg'], ew['bu'], ew['bd'])


def _pad_heads(wmat, n_heads, width):
    d = wmat.shape[0]
    w3 = wmat.reshape(d, n_heads, width)
    return jnp.pad(w3, ((0, 0), (0, 0), (0, LANES - width))).reshape(d, n_heads * LANES)


def _pad_lanes(v, width=LANES):
    v = v.reshape(1, -1)
    return jnp.pad(v, ((0, 0), (0, width - v.shape[1])))


def _rel_bucket(rel):
    half = REL_BUCKETS // 2
    max_exact = half // 2
    n = jnp.abs(rel)
    large = max_exact + (jnp.log(jnp.maximum(n, 1).astype(jnp.float32) / max_exact)
                         / math.log(REL_MAX_DIST / max_exact) * (half - max_exact)).astype(jnp.int32)
    large = jnp.minimum(large, half - 1)
    return jnp.where(rel > 0, half, 0) + jnp.where(n < max_exact, n, large)


def _bias_tiles(rel_bias):
    tk = KEY_TILE
    half = REL_BUCKETS // 2
    max_exact = half // 2
    n_sat = int(math.ceil(max_exact * (REL_MAX_DIST / max_exact) ** ((half - 1 - max_exact) / (half - max_exact)))) + 2
    nd = (n_sat + 2 * tk - 2) // tk + 1
    dd = jnp.arange(nd, dtype=I32)[:, None, None]
    c = jnp.arange(tk, dtype=I32)[None, :, None]
    r = jnp.arange(tk, dtype=I32)[None, None, :]
    bucket = _rel_bucket(c - r - dd * tk)
    onehot = (bucket[..., None] == jnp.arange(REL_BUCKETS, dtype=I32)).astype(F32)
    tiles = jnp.einsum('dcrb,bh->dhcr', onehot, rel_bias.astype(F32) * LOG2E,
                       precision=lax.Precision.HIGHEST)
    return tiles


def _prep_proj(norm1_g, w_in, q_norm_g, k_norm_g, idx_k_norm_g, idx_k_norm_b, d_model):
    ssm_w = d_model // 2
    attn_w = N_HEADS * HEAD_DIM
    kv = N_KV_HEADS * HEAD_DIM
    sizes = [ssm_w, attn_w, kv, kv, IDX_HEADS * IDX_DIM, IDX_DIM, IDX_HEADS, d_model, d_model]
    pts = np.cumsum(sizes)[:-1].tolist()
    wu, wq, wk, wv, wqi, wki, wwi, wga, wgb = jnp.split(w_in, pts, axis=1)
    bf = lambda a: a.astype(BF16)
    wwit = jnp.pad(wwi.T, ((0, 2 * SUBLANES - IDX_HEADS), (0, 0)))
    return dict(
        g1=norm1_g.reshape(1, -1).astype(F32),
        wu=bf(wu), wq=bf(_pad_heads(wq, N_HEADS, HEAD_DIM)), wk=bf(_pad_heads(wk, N_KV_HEADS, HEAD_DIM)),
        wv=bf(_pad_heads(wv, N_KV_HEADS, HEAD_DIM)), wqi=bf(_pad_heads(wqi, IDX_HEADS, IDX_DIM)),
        wki=bf(_pad_heads(wki, 1, IDX_DIM)), wwit=bf(wwit),
        wga=bf(wga), wgb=bf(wgb),
        gq=_pad_lanes(q_norm_g.astype(F32)), gk=_pad_lanes(k_norm_g.astype(F32)),
        gi=_pad_lanes(idx_k_norm_g.astype(F32)), bi=_pad_lanes(idx_k_norm_b.astype(F32)),
        ones_h=jnp.full((LANES, LANES), 1.0 / HEAD_DIM, BF16),
    )


def _prep_s5(lre, lim, log_dt, b_re, b_im, c_re, c_im, dvec, wa, wb):
    g, p = lre.shape
    ch = b_re.shape[-1]
    lam = lax.complex(lre.astype(F32), lim.astype(F32))
    dt = jnp.exp(log_dt.astype(F32))[:, None]
    a_bar = jnp.exp(lam * dt)
    b_bar = ((a_bar - 1.0) / lam)[:, :, None] * lax.complex(b_re.astype(F32), b_im.astype(F32))
    gs = g // S5_DIAG
    eye = jnp.eye(gs, dtype=F32)

    def blocks_in(m):
        return jnp.einsum('jgpc,gh->jgchp', m.reshape(S5_DIAG, gs, p, ch), eye).reshape(S5_DIAG, gs * ch, gs * p)

    def blocks_out(m):
        return jnp.einsum('jgcp,gh->jgphc', m.reshape(S5_DIAG, gs, ch, p), eye).reshape(S5_DIAG, gs * p, gs * ch)

    return dict(
        b_re=blocks_in(jnp.real(b_bar)).astype(BF16), b_im=blocks_in(jnp.imag(b_bar)).astype(BF16),
        c_re=blocks_out(c_re.astype(F32)).astype(BF16), c_im=blocks_out(-c_im.astype(F32)).astype(BF16),
        a_re=jnp.real(a_bar).reshape(1, g * p), a_im=jnp.imag(a_bar).reshape(1, g * p),
        d=dvec.reshape(1, -1).astype(F32), wa=wa.astype(BF16), wb=wb.astype(BF16),
    )


def _prep_merge(w_attn_up, w_out, norm2_g, w_router, b_router):
    d = w_attn_up.shape[1]
    wup = jnp.pad(w_attn_up.reshape(N_HEADS, HEAD_DIM, d), ((0, 0), (0, LANES - HEAD_DIM), (0, 0)))
    wr_t = w_router.astype(F32).T
    wr_hi = wr_t.astype(BF16)
    wr_lo = (wr_t - wr_hi.astype(F32)).astype(BF16)
    return dict(
        wup=wup.reshape(N_HEADS * LANES, d).astype(BF16), wout=w_out.astype(BF16),
        g2=norm2_g.reshape(1, -1).astype(F32), wr_hi=wr_hi, wr_lo=wr_lo,
        br=jnp.broadcast_to(b_router.astype(F32)[:, None], (b_router.shape[0], LANES)),
    )


def _prep_moe(wg, bg, wu, bu, wd, bd):
    return dict(wg=wg.astype(BF16), wu=wu.astype(BF16), wd=wd.astype(BF16),
                bg=bg.astype(F32)[:, None, :], bu=bu.astype(F32)[:, None, :], bd=bd.astype(F32)[:, None, :])


def _pick_tile(n, pref):
    t = min(n, pref)
    while n % t:
        t //= 2
    return t


def _pad_axis(a, axis, size):
    pad = [(0, 0)] * a.ndim
    pad[axis] = (0, size - a.shape[axis])
    return jnp.pad(a, pad)


def _trunk_layer(x, past_k, past_v, past_ik, h0_re, h0_im, pw, sw, mw, ew, bias_tiles):
    bsz, seq, d = x.shape
    t = bsz * seq
    tm = _pick_tile(seq, 512)
    u_tb, q, kp, vp, qi, kip, wt, sga, sgb, kc, vc, kic = _proj(x, pw, bsz, seq, tm)

    half = sw['a_re'].shape[1]
    if h0_re is None:
        h0 = jnp.zeros((bsz, 2 * half), F32)
    else:
        h0 = jnp.concatenate([h0_re.reshape(bsz, half), h0_im.reshape(bsz, half)], axis=1).astype(F32)
    tc = _pick_tile(seq, max(1, S5_ROWS // bsz))
    ya, hout = _s5(u_tb, h0, sw, bsz, seq, tc)
    groups = half // SSM_STATE
    s_re = hout[:, :half].reshape(bsz, groups, SSM_STATE)
    s_im = hout[:, half:].reshape(bsz, groups, SSM_STATE)

    past = 0 if past_k is None else past_k.shape[1]
    n_keys = past + seq
    lk = -(-n_keys // KEY_BLOCK) * KEY_BLOCK
    kip3 = kip.reshape(bsz, seq, LANES)
    if past:
        lane = jnp.arange(LANES)
        pk = jnp.pad(past_k.astype(F32), ((0, 0), (0, 0), (0, 0), (0, LANES - HEAD_DIM)))
        pk = jnp.where(lane == HEAD_DIM, 1.0, pk).astype(BF16)
        pv = jnp.pad(past_v.astype(F32), ((0, 0), (0, 0), (0, 0), (0, LANES - HEAD_DIM)))
        pv = jnp.where(lane == HEAD_DIM, 1.0, pv).astype(BF16)
        pik = jnp.pad(past_ik.astype(F32), ((0, 0), (0, 0), (0, LANES - IDX_DIM))).astype(BF16)
        k_all = jnp.concatenate([pk.transpose(0, 2, 1, 3), kp], axis=2)
        v_all = jnp.concatenate([pv.transpose(0, 2, 1, 3), vp], axis=2)
        ki_all = jnp.concatenate([pik, kip3], axis=1)
    else:
        k_all, v_all, ki_all = kp, vp, kip3
    k_all = _pad_axis(k_all, 2, lk)
    v_all = _pad_axis(v_all, 2, lk)
    ki_all = _pad_axis(ki_all, 1, lk)
    vt_all = v_all.reshape(bsz, N_KV_HEADS, lk // KEY_BLOCK, KEY_BLOCK, LANES).transpose(0, 1, 2, 4, 3)
    tq = KEY_TILE
    seq_q = -(-seq // tq) * tq
    q_p, qi_p, wt_p = _pad_axis(q, 2, seq_q), _pad_axis(qi, 2, seq_q), _pad_axis(wt, 2, seq_q)
    kf = k_all[..., :HEAD_DIM].astype(F32)
    kmax = jnp.sqrt(jnp.max(jnp.sum(kf * kf, axis=-1), axis=-1)).reshape(-1)
    bfar = bias_tiles[-1, :, 0, 0]
    bmax = jnp.max(jnp.abs(bias_tiles), axis=(0, 2, 3))
    stats = jnp.concatenate([kmax, bmax, bfar]).astype(F32)
    bias_tiles = bias_tiles - bfar[None, :, None, None]
    attn = _dsa(stats, q_p, qi_p, wt_p, k_all, vt_all, ki_all, bias_tiles, bsz, seq_q, past, n_keys, tq)
    attn = attn[:, :seq]

    moe_tile = _pick_tile(t, MOE_TILE)
    tm2 = _pick_tile(moe_tile, 512)
    x1, h2, gt, rt, cnt = _merge(x.reshape(t, d), ya, attn.reshape(t, attn.shape[-1]), sga, sgb, mw, tm2, moe_tile)
    cnt_i = cnt[:, :, 0].astype(I32).reshape(-1)
    y = _moe(h2, x1, gt, rt, cnt_i, ew, moe_tile)

    k_new = kc.reshape(bsz, seq, N_KV_HEADS, HEAD_DIM)
    v_new = vc.reshape(bsz, seq, N_KV_HEADS, HEAD_DIM)
    ik_new = kic.reshape(bsz, seq, IDX_DIM)
    return y.reshape(bsz, seq, d), k_new, v_new, ik_new, s_re, s_im


def kernel(x_prompt, x_sample, cache_k, cache_v, cache_idx_k, state_ssm_re, state_ssm_im, rel_bias, norm1_g, w_in, ssm_lambda_re, ssm_lambda_im, ssm_log_dt, ssm_b_re, ssm_b_im, ssm_c_re, ssm_c_im, ssm_d, ssm_w_glu_a, ssm_w_glu_b, q_norm_g, k_norm_g, idx_k_norm_g, idx_k_norm_b, w_attn_up, w_out, norm2_g, moe_w_router, moe_b_router, moe_w_gate, moe_b_gate, moe_w_up, moe_b_up, moe_w_down, moe_b_down):
    depth = w_in.shape[0]
    d_model = x_prompt.shape[-1]
    bias_tiles = _bias_tiles(rel_bias)
    xp, xs = x_prompt, x_sample
    st_p, st_s = [], []
    for l in range(depth):
        pw = _prep_proj(norm1_g[l], w_in[l], q_norm_g[l], k_norm_g[l], idx_k_norm_g[l], idx_k_norm_b[l], d_model)
        sw = _prep_s5(ssm_lambda_re[l], ssm_lambda_im[l], ssm_log_dt[l], ssm_b_re[l], ssm_b_im[l], ssm_c_re[l],
                      ssm_c_im[l], ssm_d[l], ssm_w_glu_a[l], ssm_w_glu_b[l])
        mw = _prep_merge(w_attn_up[l], w_out[l], norm2_g[l], moe_w_router[l], moe_b_router[l])
        ew = _prep_moe(moe_w_gate[l], moe_b_gate[l], moe_w_up[l], moe_b_up[l], moe_w_down[l], moe_b_down[l])
        xp, *sp = _trunk_layer(xp, None, None, None, None, None, pw, sw, mw, ew, bias_tiles)
        xs, *ss = _trunk_layer(xs, cache_k[l], cache_v[l], cache_idx_k[l], state_ssm_re[l], state_ssm_im[l],
                               pw, sw, mw, ew, bias_tiles)
        st_p.append(sp)
        st_s.append(ss)
    outs_p = [jnp.stack([s[i] for s in st_p]) for i in range(5)]
    outs_s = [jnp.stack([s[i] for s in st_s]) for i in range(5)]
    return (xp, xs, *outs_p, *outs_s)
```

```python
import functools
import math

import numpy as np
import jax
import jax.numpy as jnp
from jax import lax
from jax.experimental import pallas as pl
from jax.experimental.pallas import tpu as pltpu

F32 = jnp.float32
BF16 = jnp.bfloat16
I32 = jnp.int32

LANES = 128
SUBLANES = 8
VMEM_LIMIT = 56 * 1024 * 1024

CHUNK = 64
SSM_GROUP_CH = 16
SSM_STATE = 64
N_HEADS = 8
HEAD_DIM = 64
N_KV_HEADS = 2
KV_REP = N_HEADS // N_KV_HEADS
IDX_HEADS = 8
IDX_DIM = 64
TOPK_MAX = 256
REL_BUCKETS = 32
REL_MAX_DIST = 1024
N_EXPERTS = 32
TOP_K = 4
SWIGLU_LIMIT = 7.0
SWIGLU_ALPHA = 1.702
EPS = 1e-6

KEY_TILE = 128
KEY_BLOCK = 256
SCORE_CHUNK = 4
ATTEND_CHUNK = 4
LOG2E = math.log2(math.e)
NEG_BIG = -1e30
SHIFT_LIMIT = 30.0
S5_ROWS = 1024
S5_DIAG = 2
SEARCH_FIRST = 8
SEARCH_GROUP = 4
assert SCORE_CHUNK == 4 and ATTEND_CHUNK == 4
MOE_TILE = 1024
MOE_ROWS = 160
MOE_GROUP = 8
MOE_PAIR = 2


def _cp(sem):
    return pltpu.CompilerParams(dimension_semantics=sem, vmem_limit_bytes=VMEM_LIMIT)


def _dot(a, b):
    return jnp.dot(a, b, preferred_element_type=F32)


def _dot_nt(a, b):
    return lax.dot_general(a, b, (((1,), (1,)), ((), ())), preferred_element_type=F32)


def _dot_tn(a, b):
    return lax.dot_general(a, b, (((0,), (0,)), ((), ())), preferred_element_type=F32)


def _split(a):
    hi = a.astype(BF16)
    lo = (a - hi.astype(F32)).astype(BF16)
    return hi, lo


def _proj_kernel(x_ref, g1_ref, wu_ref, wq_ref, wk_ref, wv_ref, wqi_ref, wki_ref, wwit_ref, wga_ref, wgb_ref,
                 gq_ref, gk_ref, gi_ref, bi_ref,
                 u_ref, q_ref, kp_ref, vp_ref, qi_ref, kip_ref, wt_ref, sga_ref, sgb_ref,
                 kc_ref, vc_ref, kic_ref):
    x = x_ref[...]
    ms = jnp.mean(x * x, axis=-1, keepdims=True)
    hn = (x * lax.rsqrt(ms + EPS) * g1_ref[...]).astype(BF16)
    lane = lax.broadcasted_iota(I32, (x.shape[0], LANES), 1)

    def head_mean(a):
        return jnp.sum(a, axis=-1, keepdims=True) * (1.0 / HEAD_DIM)

    u_ref[...] = _dot(hn, wu_ref[...]).astype(BF16)

    q = _dot(hn, wq_ref[...])
    scale = HEAD_DIM ** -0.5 * LOG2E
    for h in range(N_HEADS):
        qh = q[:, h * LANES:(h + 1) * LANES]
        msq = head_mean(qh * qh)
        q_ref[h] = (qh * lax.rsqrt(msq + EPS) * (gq_ref[...] * scale)).astype(BF16)

    k = _dot(hn, wk_ref[...])
    for g in range(N_KV_HEADS):
        kg = k[:, g * LANES:(g + 1) * LANES]
        msk = head_mean(kg * kg)
        kn = kg * lax.rsqrt(msk + EPS) * gk_ref[...]
        kp_ref[g] = jnp.where(lane == HEAD_DIM, 1.0, kn).astype(BF16)
        kc_ref[:, g, :] = kn[:, :HEAD_DIM]

    v = _dot(hn, wv_ref[...])
    for g in range(N_KV_HEADS):
        vg = v[:, g * LANES:(g + 1) * LANES]
        vp_ref[g] = jnp.where(lane == HEAD_DIM, 1.0, vg).astype(BF16)
        vc_ref[:, g, :] = vg[:, :HEAD_DIM]

    qi = _dot(hn, wqi_ref[...])
    for h in range(IDX_HEADS):
        qi_ref[h] = qi[:, h * LANES:(h + 1) * LANES].astype(BF16)

    ki = _dot(hn, wki_ref[...])
    mu = head_mean(ki)
    xc = jnp.where(lane < IDX_DIM, ki - mu, 0.0)
    var = head_mean(xc * xc)
    kin = xc * lax.rsqrt(var + EPS) * gi_ref[...] + bi_ref[...]
    kip_ref[...] = kin.astype(BF16)
    kic_ref[...] = kin[:, :IDX_DIM]

    wt = _dot_nt(wwit_ref[...], hn)
    wt_ref[...] = wt[0:IDX_HEADS, :] * (IDX_HEADS ** -0.5 * IDX_DIM ** -0.5)

    sga_ref[...] = jax.nn.sigmoid(_dot(hn, wga_ref[...])).astype(BF16)
    sgb_ref[...] = jax.nn.sigmoid(_dot(hn, wgb_ref[...])).astype(BF16)


def _proj(x, pw, bsz, seq, tm):
    d = x.shape[-1]
    nt = seq // tm
    t = bsz * seq
    x2 = x.reshape(t, d)

    def tok(b, i):
        return (b * nt + i, 0)

    def cst(b, i):
        return (0, 0)

    def wspec(a):
        return pl.BlockSpec(a.shape, cst)

    weights = [pw['g1'], pw['wu'], pw['wq'], pw['wk'], pw['wv'], pw['wqi'], pw['wki'], pw['wwit'], pw['wga'],
               pw['wgb'], pw['gq'], pw['gk'], pw['gi'], pw['bi']]
    ssm_w = pw['wu'].shape[1]
    out_shape = (
        jax.ShapeDtypeStruct((t, ssm_w), BF16),
        jax.ShapeDtypeStruct((bsz, N_HEADS, seq, LANES), BF16),
        jax.ShapeDtypeStruct((bsz, N_KV_HEADS, seq, LANES), BF16),
        jax.ShapeDtypeStruct((bsz, N_KV_HEADS, seq, LANES), BF16),
        jax.ShapeDtypeStruct((bsz, IDX_HEADS, seq, LANES), BF16),
        jax.ShapeDtypeStruct((t, LANES), BF16),
        jax.ShapeDtypeStruct((bsz, IDX_HEADS, seq), F32),
        jax.ShapeDtypeStruct((t, d), BF16),
        jax.ShapeDtypeStruct((t, d), BF16),
        jax.ShapeDtypeStruct((t, N_KV_HEADS, HEAD_DIM), F32),
        jax.ShapeDtypeStruct((t, N_KV_HEADS, HEAD_DIM), F32),
        jax.ShapeDtypeStruct((t, IDX_DIM), F32),
    )

    def hm(nh):
        return pl.BlockSpec((None, nh, tm, LANES), lambda b, i: (b, 0, i, 0))

    out_specs = (
        pl.BlockSpec((tm, ssm_w), tok),
        hm(N_HEADS), hm(N_KV_HEADS), hm(N_KV_HEADS), hm(IDX_HEADS),
        pl.BlockSpec((tm, LANES), tok),
        pl.BlockSpec((None, IDX_HEADS, tm), lambda b, i: (b, 0, i)),
        pl.BlockSpec((tm, d), tok), pl.BlockSpec((tm, d), tok),
        pl.BlockSpec((tm, N_KV_HEADS, HEAD_DIM), lambda b, i: (b * nt + i, 0, 0)),
        pl.BlockSpec((tm, N_KV_HEADS, HEAD_DIM), lambda b, i: (b * nt + i, 0, 0)),
        pl.BlockSpec((tm, IDX_DIM), tok),
    )
    return pl.pallas_call(
        _proj_kernel,
        grid=(bsz, nt),
        in_specs=[pl.BlockSpec((tm, d), tok)] + [wspec(a) for a in weights],
        out_specs=out_specs,
        out_shape=out_shape,
        compiler_params=_cp(("arbitrary", "arbitrary")),
        name="proj",
    )(x2, *weights)


def _gelu_tanh(x):
    return 0.5 * x * (1.0 + jnp.tanh(math.sqrt(2.0 / math.pi) * (x + 0.044715 * (x * x * x))))


def _s5_kernel(u_ref, h0_ref, bre_ref, bim_ref, are_ref, aim_ref, cre_ref, cim_ref, dvec_ref, wa_ref, wb_ref,
               ya_ref, hout_ref, state_ref, bu_ref, yf_ref, ug_ref, *, bsz, tc, strip):
    s = pl.program_id(0)
    half = are_ref.shape[1]

    @pl.when(s == 0)
    def _():
        state_ref[...] = h0_ref[...]

    ssm_w = dvec_ref.shape[1]
    for b in range(bsz):
        for c in range(ssm_w // LANES):
            ug_ref[c, pl.ds(b, tc, stride=bsz), :] = u_ref[b, :, c * LANES:(c + 1) * LANES].astype(F32)
    u = jnp.concatenate([ug_ref[c] for c in range(ssm_w // LANES)], axis=1).astype(BF16)
    cw = u.shape[1] // S5_DIAG
    sw = half // S5_DIAG
    for j in range(S5_DIAG):
        uj = u[:, j * cw:(j + 1) * cw]
        bu_ref[:, j * sw:(j + 1) * sw] = _dot(uj, bre_ref[j])
        bu_ref[:, half + j * sw:half + (j + 1) * sw] = _dot(uj, bim_ref[j])

    for c0 in range(0, half, strip):
        ar = jnp.broadcast_to(are_ref[:, c0:c0 + strip], (bsz, strip))
        ai = jnp.broadcast_to(aim_ref[:, c0:c0 + strip], (bsz, strip))
        hr0 = state_ref[:, c0:c0 + strip]
        hi0 = state_ref[:, half + c0:half + c0 + strip]

        def step(t, carry):
            hr, hi = carry
            r0 = pl.multiple_of(t * bsz, bsz)
            br = bu_ref[pl.ds(r0, bsz), c0:c0 + strip]
            bi = bu_ref[pl.ds(r0, bsz), half + c0:half + c0 + strip]
            nr = ar * hr - ai * hi + br
            ni = ar * hi + ai * hr + bi
            bu_ref[pl.ds(r0, bsz), c0:c0 + strip] = nr
            bu_ref[pl.ds(r0, bsz), half + c0:half + c0 + strip] = ni
            return nr, ni

        hr, hi = lax.fori_loop(0, tc, step, (hr0, hi0))
        state_ref[:, c0:c0 + strip] = hr
        state_ref[:, half + c0:half + c0 + strip] = hi

    ys = []
    for j in range(S5_DIAG):
        s_re = bu_ref[:, j * sw:(j + 1) * sw].astype(BF16)
        s_im = bu_ref[:, half + j * sw:half + (j + 1) * sw].astype(BF16)
        ys.append(_dot(s_re, cre_ref[j]) + _dot(s_im, cim_ref[j]))
    y = jnp.concatenate(ys, axis=1) + dvec_ref[...] * u.astype(F32)
    g = _gelu_tanh(y).astype(BF16)
    ya = _dot(g, wa_ref[...]) * jax.nn.sigmoid(_dot(g, wb_ref[...]))
    n_chunk = ya.shape[1] // LANES
    for c in range(n_chunk):
        yf_ref[c] = ya[:, c * LANES:(c + 1) * LANES]
    for b in range(bsz):
        ya_ref[b] = jnp.concatenate([yf_ref[c, pl.ds(b, tc, stride=bsz), :] for c in range(n_chunk)],
                                    axis=1).astype(BF16)

    @pl.when(s == pl.num_programs(0) - 1)
    def _():
        hout_ref[...] = state_ref[...]


def _s5(u, h0, sw, bsz, seq, tc):
    rows = tc * bsz
    ssm_w = sw['d'].shape[1]
    half = sw['a_re'].shape[1]
    two_half = 2 * half
    d = sw['wa'].shape[1]
    u3 = u.reshape(bsz, seq, ssm_w)
    strip = min(512, half)

    consts = [h0, sw['b_re'], sw['b_im'], sw['a_re'], sw['a_im'], sw['c_re'], sw['c_im'], sw['d'], sw['wa'], sw['wb']]

    def cst(s):
        return (0, 0)

    def cspec(a):
        return pl.BlockSpec(a.shape, lambda s: (0,) * a.ndim)

    ya, hout = pl.pallas_call(
        functools.partial(_s5_kernel, bsz=bsz, tc=tc, strip=strip),
        grid=(seq // tc,),
        in_specs=[pl.BlockSpec((bsz, tc, ssm_w), lambda s: (0, s, 0))] + [cspec(a) for a in consts],
        out_specs=(pl.BlockSpec((bsz, tc, d), lambda s: (0, s, 0)), pl.BlockSpec((bsz, two_half), cst)),
        out_shape=(jax.ShapeDtypeStruct((bsz, seq, d), BF16), jax.ShapeDtypeStruct((bsz, two_half), F32)),
        scratch_shapes=[pltpu.VMEM((bsz, two_half), F32), pltpu.VMEM((rows, two_half), F32),
                        pltpu.VMEM((d // LANES, rows, LANES), F32), pltpu.VMEM((ssm_w // LANES, rows, LANES), F32)],
        compiler_params=_cp(("arbitrary",)),
        name="s5",
    )(u3, *consts)
    return ya.reshape(bsz * seq, d), hout


def _f2key(x):
    b = lax.bitcast_convert_type(x, I32)
    return b ^ ((b >> 31) & 0x7FFFFFFF)


def _key2f(k):
    return lax.bitcast_convert_type(k ^ ((k >> 31) & 0x7FFFFFFF), F32)


def _dsa_kernel(st_ref, q_ref, qi_ref, wt_ref, k_ref, vt_ref, ki_ref, bias_ref, o_ref,
                s_ref, lo_ref, hi_ref, clo_ref, glo_ref, ghi_ref, side_ref, q2_ref, mrow_ref, acc_ref,
                *, bsz, tq, past, n_keys, topk, nkt, nd, near_max):
    b_id = pl.program_id(0)
    i = pl.program_id(1)
    kb = KEY_BLOCK
    sl = SUBLANES
    q0 = past + i * tq
    last_chunk = (q0 + tq - 1) // CHUNK
    n_kt = jnp.minimum(nkt, ((last_chunk + 1) * CHUNK + kb - 1) // kb)
    d0 = q0 // KEY_TILE

    krow = lax.broadcasted_iota(I32, (kb, tq), 0)
    q_chunk = (q0 + lax.broadcasted_iota(I32, (kb, tq), 1)) // CHUNK
    qc8 = (q0 + lax.broadcasted_iota(I32, (sl, tq), 1)) // CHUNK
    n_adm = jnp.minimum((qc8 + 1) * CHUNK, n_keys)
    n_admf = n_adm.astype(F32)
    is_pad = q0 + lax.broadcasted_iota(I32, (sl, tq), 1) >= n_keys
    needf = jnp.where(is_pad, n_adm, jnp.minimum(topk, n_adm)).astype(F32)

    def bcast(x):
        return jnp.broadcast_to(x[0:1, :], (kb, tq))

    def rep(x):
        return jnp.broadcast_to(x, (sl, tq))

    qi = qi_ref[...].reshape(IDX_HEADS * tq, LANES)

    def score_blocks(kt, nb, masked):
        k0 = pl.multiple_of(kt * kb, kb)
        s = _dot_nt(ki_ref[pl.ds(k0, nb * kb), :], qi)
        for j in range(nb):
            sc = jnp.zeros((kb, tq), F32)
            for h in range(IDX_HEADS):
                sc = sc + wt_ref[h:h + 1, :] * jnp.maximum(s[j * kb:(j + 1) * kb, h * tq:(h + 1) * tq], 0.0)
            if masked:
                kpos = k0 + j * kb + krow
                adm = ((kpos // CHUNK) <= q_chunk) & (kpos < n_keys)
                sc = jnp.where(adm, sc, -jnp.inf)
            s_ref[kt + j] = sc

    n_open = n_kt - 1

    def score_chunk(j, c):
        score_blocks(SCORE_CHUNK * j, SCORE_CHUNK, False)
        return c

    lax.fori_loop(0, n_open // SCORE_CHUNK, score_chunk, 0)
    rest = n_open % SCORE_CHUNK

    @pl.when(rest >= 2)
    def _():
        score_blocks(n_open - rest, 2, False)

    @pl.when(rest % 2 == 1)
    def _():
        score_blocks(n_open - 1, 1, False)
    score_blocks(n_kt - 1, 1, True)
    s_ref[n_kt] = jnp.full((kb, tq), -jnp.inf, F32)
    n_pair = (n_kt + 1) // 2

    part = 4 * sl

    def fold(x, op):
        x = x.reshape(kb // part, part, tq)
        acc = x[0]
        for j in range(1, kb // part):
            acc = op(acc, x[j])
        return acc

    def count(pred):
        def one(kt):
            return fold(jnp.where(pred(s_ref[kt], kt), 1.0, 0.0), jnp.add)
        c = lax.fori_loop(0, n_pair, lambda j, c: c + (one(2 * j) + one(2 * j + 1)), jnp.zeros((part, tq), F32))
        return rep(jnp.sum(c, axis=0, keepdims=True))

    def minmax(j, c):
        mx, mn = c
        for kt in (2 * j, 2 * j + 1):
            s = s_ref[kt]
            mx = jnp.maximum(mx, fold(s, jnp.maximum))
            mn = jnp.minimum(mn, fold(jnp.where(s == -jnp.inf, jnp.inf, s), jnp.minimum))
        return mx, mn

    mx, mn = lax.fori_loop(0, n_pair, minmax,
                           (jnp.full((part, tq), -jnp.inf, F32), jnp.full((part, tq), jnp.inf, F32)))
    lo_ref[...] = rep(jnp.min(mn, axis=0, keepdims=True))
    hi_ref[...] = _key2f(_f2key(rep(jnp.max(mx, axis=0, keepdims=True))) + 1)
    def odds(cnt):
        c = jnp.clip(cnt, 0.5, n_admf - 0.5)
        return jnp.log((n_admf - c) / c)

    target = odds(needf - 0.5)
    clo_ref[...] = n_admf
    glo_ref[...] = target - odds(n_admf)
    ghi_ref[...] = target - odds(jnp.zeros((sl, tq), F32))
    side_ref[...] = jnp.zeros((sl, tq), F32)

    def searching(lo, hi, clo):
        return (_f2key(hi) > _f2key(lo) + 1) & (clo > needf)

    def refine(it, c):
        lo, hi, clo = lo_ref[...], hi_ref[...], clo_ref[...]
        glo, ghi, side = glo_ref[...], ghi_ref[...], side_ref[...]
        k_t = _f2key(lo + (hi - lo) * (glo / (glo - ghi)))
        t = _key2f(jnp.minimum(jnp.maximum(k_t, _f2key(lo) + 1), _f2key(hi) - 1))
        tb = bcast(t)
        cnt = count(lambda s, kt: s >= tb)
        g = target - odds(cnt)
        open_ = searching(lo, hi, clo)
        up = open_ & (cnt >= needf)
        dn = open_ & (cnt < needf)
        lo_ref[...] = jnp.where(up, t, lo)
        clo_ref[...] = jnp.where(up, cnt, clo)
        hi_ref[...] = jnp.where(dn, t, hi)
        glo_ref[...] = jnp.where(up, g, jnp.where(dn & (side < 0.0), glo * 0.5, glo))
        ghi_ref[...] = jnp.where(dn, g, jnp.where(up & (side > 0.0), ghi * 0.5, ghi))
        side_ref[...] = jnp.where(up, 1.0, jnp.where(dn, -1.0, side))
        return c

    def snap():
        lo, hi, clo = lo_ref[...], hi_ref[...], clo_ref[...]
        lo_b, hi_b = bcast(lo), bcast(hi)

        def body(j, c):
            a, b = c
            for kt in (2 * j, 2 * j + 1):
                s = s_ref[kt]
                a = jnp.minimum(a, fold(jnp.where(s >= lo_b, s, jnp.inf), jnp.minimum))
                b = jnp.maximum(b, fold(jnp.where(s < hi_b, s, -jnp.inf), jnp.maximum))
            return a, b

        a, b = lax.fori_loop(0, n_pair, body,
                             (jnp.full((part, tq), jnp.inf, F32), jnp.full((part, tq), -jnp.inf, F32)))
        open_ = searching(lo, hi, clo)
        lo_ref[...] = jnp.where(open_, rep(jnp.min(a, axis=0, keepdims=True)), lo)
        hi_ref[...] = jnp.where(open_, _key2f(_f2key(rep(jnp.max(b, axis=0, keepdims=True))) + 1), hi)

    def n_searching():
        return jnp.max(jnp.where(searching(lo_ref[...], hi_ref[...], clo_ref[...]), 1.0, 0.0))

    lax.fori_loop(0, SEARCH_FIRST, refine, 0)
    snap()
    lax.fori_loop(0, SEARCH_GROUP, refine, 0)
    snap()

    def group(c):
        lax.fori_loop(0, SEARCH_GROUP, refine, 0)
        snap()
        return n_searching()

    lax.while_loop(lambda c: c > 0.0, group, n_searching())
    thr = lo_ref[...]
    thr_b = bcast(thr)

    n_tied = jnp.max(jnp.where(clo_ref[...] > needf, 1.0, 0.0))

    @pl.when(n_tied > 0.0)
    def _():
        rem_b = bcast(needf - count(lambda s, kt: s > thr_b))
        tri = jnp.where(lax.broadcasted_iota(I32, (kb, kb), 0) >= lax.broadcasted_iota(I32, (kb, kb), 1),
                        1.0, 0.0).astype(BF16)

        def drop(kt, seen):
            s = s_ref[kt]
            tie = s == thr_b
            rank = _dot(tri, jnp.where(tie, 1.0, 0.0).astype(BF16)) + bcast(seen)
            s_ref[kt] = jnp.where(tie & (rank > rem_b), -jnp.inf, s)
            return rep(rank[kb - 1:kb, :])

        lax.fori_loop(0, n_kt, drop, jnp.zeros((sl, tq), F32))

    rows_g = KV_REP * tq
    qf = q_ref[...].reshape(N_HEADS * tq, LANES).astype(F32)
    qn = jnp.sqrt(jnp.sum(qf * qf, axis=1, keepdims=True))
    lane = lax.broadcasted_iota(I32, (tq, LANES), 1)
    worst = jnp.float32(0.0)
    for h in range(N_HEADS):
        kmax = st_ref[b_id * N_KV_HEADS + h // KV_REP]
        bmax = st_ref[bsz * N_KV_HEADS + h]
        bfar = st_ref[bsz * N_KV_HEADS + N_HEADS + h]
        bound = qn[h * tq:(h + 1) * tq, :] * (kmax * 1.01) + (bmax + 0.1)
        worst = jnp.maximum(worst, jnp.max(bound))
        q2_ref[h * tq:(h + 1) * tq, :] = jnp.where(lane == HEAD_DIM, bfar - bound,
                                                   qf[h * tq:(h + 1) * tq, :]).astype(BF16)
    n_far = jnp.clip((d0 - nd) // 2 + 1, 0, n_kt)

    def select_masks(kt0, nb):
        return [jnp.where(s_ref[kt0 + j] >= thr_b, 0.0, NEG_BIG) for j in range(nb)]

    def logits(kt0, nb, g, near, exact, masks):
        k0 = pl.multiple_of(kt0 * kb, kb)
        s = _dot_nt(k_ref[g, pl.ds(k0, nb * kb), :], q2_ref[g * rows_g:(g + 1) * rows_g, :])
        out = []
        for j in range(nb):
            kt = kt0 + j
            maskadd = masks[j]
            if near:
                da = jnp.clip(d0 - 2 * kt, 0, nd - 1)
                db = jnp.clip(d0 - 2 * kt - 1, 0, nd - 1)
            parts = []
            for r in range(KV_REP):
                h = g * KV_REP + r
                add = maskadd - mrow_ref[0:1, h * tq:(h + 1) * tq] if exact else maskadd
                if near:
                    add = jnp.concatenate([bias_ref[da, h], bias_ref[db, h]], axis=0) + add
                parts.append(s[j * kb:(j + 1) * kb, r * tq:(r + 1) * tq] + add)
            out.append(jnp.concatenate(parts, axis=1))
        return out

    def over_blocks(fn):
        def far_chunk(j, c):
            fn(ATTEND_CHUNK * j, ATTEND_CHUNK, False)
            return c
        lax.fori_loop(0, n_far // ATTEND_CHUNK, far_chunk, 0)
        rest = n_far % ATTEND_CHUNK

        @pl.when(rest >= 2)
        def _():
            fn(n_far - rest, 2, False)

        @pl.when(rest % 2 == 1)
        def _():
            fn(n_far - 1, 1, False)

        n_near = n_kt - n_far
        for nb in range(1, near_max + 1):
            @pl.when(n_near == nb)
            def _(nb=nb):
                fn(n_far, nb, True)

    def attend(exact):
        acc_ref[...] = jnp.zeros(acc_ref.shape, F32)

        def blocks(kt0, nb, near):
            masks = select_masks(kt0, nb)
            for g in range(N_KV_HEADS):
                p = jnp.concatenate([jnp.exp2(lg).astype(BF16) for lg in logits(kt0, nb, g, near, exact, masks)],
                                    axis=0)
                vt = jnp.concatenate([vt_ref[g, kt0 + j] for j in range(nb)], axis=1)
                acc_ref[g] += _dot(vt, p)
        over_blocks(blocks)

    @pl.when(worst <= SHIFT_LIMIT)
    def _():
        attend(False)

    @pl.when(worst > SHIFT_LIMIT)
    def _():
        mrow_ref[...] = jnp.full(mrow_ref.shape, NEG_BIG, F32)

        def blocks(kt0, nb, near):
            masks = select_masks(kt0, nb)
            for g in range(N_KV_HEADS):
                for lg in logits(kt0, nb, g, near, False, masks):
                    mx = jnp.max(lg, axis=0, keepdims=True)
                    cur = mrow_ref[:, g * rows_g:(g + 1) * rows_g]
                    mrow_ref[:, g * rows_g:(g + 1) * rows_g] = jnp.maximum(cur, jnp.broadcast_to(mx, (sl, rows_g)))
        over_blocks(blocks)
        attend(True)

    for g in range(N_KV_HEADS):
        acc = acc_ref[g]
        og = acc / acc[HEAD_DIM:HEAD_DIM + 1, :]
        for r in range(KV_REP):
            h = g * KV_REP + r
            o_ref[:, h * LANES:(h + 1) * LANES] = og[:, r * tq:(r + 1) * tq].T.astype(BF16)


def _dsa(stats, q, qi, wt, k_all, vt_all, ki_all, bias_tiles, bsz, seq, past, n_keys, tq):
    lk = k_all.shape[2]
    nkt = lk // KEY_BLOCK
    topk = min(TOPK_MAX, n_keys // 4)
    nd = bias_tiles.shape[0]
    nq = seq // tq
    assert past % KEY_TILE == 0 and tq == KEY_TILE
    near_max = 0
    for i in range(nq):
        q0 = past + i * tq
        n_kt = min(nkt, (((q0 + tq - 1) // CHUNK + 1) * CHUNK + KEY_BLOCK - 1) // KEY_BLOCK)
        n_far = min(max((q0 // KEY_TILE - nd) // 2 + 1, 0), n_kt)
        near_max = max(near_max, n_kt - n_far)
    kern = functools.partial(_dsa_kernel, bsz=bsz, tq=tq, past=past, n_keys=n_keys, topk=topk, nkt=nkt, nd=nd,
                             near_max=near_max)
    row_state = pltpu.VMEM((SUBLANES, tq), F32)
    grid_spec = pltpu.PrefetchScalarGridSpec(
        num_scalar_prefetch=1,
        grid=(bsz, nq),
        in_specs=[
            pl.BlockSpec((None, N_HEADS, tq, LANES), lambda b, i, s: (b, 0, i, 0)),
            pl.BlockSpec((None, IDX_HEADS, tq, LANES), lambda b, i, s: (b, 0, i, 0)),
            pl.BlockSpec((None, IDX_HEADS, tq), lambda b, i, s: (b, 0, i)),
            pl.BlockSpec((None, N_KV_HEADS, lk, LANES), lambda b, i, s: (b, 0, 0, 0)),
            pl.BlockSpec((None, N_KV_HEADS, nkt, LANES, KEY_BLOCK), lambda b, i, s: (b, 0, 0, 0, 0)),
            pl.BlockSpec((None, lk, LANES), lambda b, i, s: (b, 0, 0)),
            pl.BlockSpec(bias_tiles.shape, lambda b, i, s: (0, 0, 0, 0)),
        ],
        out_specs=pl.BlockSpec((None, tq, N_HEADS * LANES), lambda b, i, s: (b, i, 0)),
        scratch_shapes=[
            pltpu.VMEM((nkt + 1, KEY_BLOCK, tq), F32),
            row_state, row_state, row_state, row_state, row_state, row_state,
            pltpu.VMEM((N_HEADS * tq, LANES), BF16),
            pltpu.VMEM((SUBLANES, N_HEADS * tq), F32),
            pltpu.VMEM((N_KV_HEADS, LANES, KV_REP * tq), F32),
        ],
    )
    return pl.pallas_call(
        kern,
        grid_spec=grid_spec,
        out_shape=jax.ShapeDtypeStruct((bsz, seq, N_HEADS * LANES), BF16),
        compiler_params=_cp(("arbitrary", "arbitrary")),
        name="dsa",
    )(stats, q, qi, wt, k_all, vt_all, ki_all, bias_tiles)


def _merge_kernel(x_ref, ya_ref, at_ref, sga_ref, sgb_ref, wup_ref, wout_ref, g2_ref, wr_hi_ref, wr_lo_ref, br_ref,
                  x1_ref, h2_ref, gt_ref, rt_ref, cnt_ref, run_ref, *, tm, sub):
    step = pl.program_id(0)

    @pl.when(step % sub == 0)
    def _():
        run_ref[...] = jnp.zeros(run_ref.shape, F32)

    yb = _dot(at_ref[...], wup_ref[...])
    merged = sga_ref[...].astype(F32) * ya_ref[...].astype(F32) + sgb_ref[...].astype(F32) * yb
    x1 = x_ref[...] + _dot(merged.astype(BF16), wout_ref[...])
    x1_ref[...] = x1
    ms = jnp.mean(x1 * x1, axis=-1, keepdims=True)
    h2 = x1 * lax.rsqrt(ms + EPS) * g2_ref[...]
    h2_hi, h2_lo = _split(h2)
    h2_ref[...] = h2_hi

    wr_hi = wr_hi_ref[...]
    logit = (_dot_nt(wr_hi, h2_hi) + _dot_nt(wr_hi, h2_lo) + _dot_nt(wr_lo_ref[...], h2_hi)) + br_ref[:, 0:1]
    ne = logit.shape[0]
    eid = lax.broadcasted_iota(I32, (ne, tm), 0).astype(F32)
    selb = jnp.zeros((ne, tm), F32)
    tops = []
    picks = []
    for _ in range(TOP_K):
        mx = jnp.max(logit, axis=0, keepdims=True)
        pick = jnp.min(jnp.where(logit == mx, eid, float(ne)), axis=0, keepdims=True)
        hit = eid == pick
        selb = jnp.where(hit, 1.0, selb)
        logit = jnp.where(hit, -jnp.inf, logit)
        tops.append(mx)
        picks.append(hit)
    ex = [jnp.exp(t - tops[0]) for t in tops]
    den = ex[0] + ex[1] + ex[2] + ex[3]
    gate = jnp.zeros((ne, tm), F32)
    for hit, e in zip(picks, ex):
        gate = jnp.where(hit, e / den, gate)
    gt_ref[...] = gate

    sel = selb > 0.5
    selb = selb.astype(BF16)
    r_i = lax.broadcasted_iota(I32, (tm, tm), 0)
    c_i = lax.broadcasted_iota(I32, (tm, tm), 1)
    tri = jnp.where(r_i < c_i, 1.0, 0.0).astype(BF16)
    run = run_ref[...]
    rank = _dot(selb, tri) + jnp.broadcast_to(run[:, 0:1], (ne, tm))
    rt_ref[...] = jnp.where(sel, rank, -1.0)
    run = run + _dot(selb, jnp.ones((tm, LANES), BF16))
    run_ref[...] = run
    cnt_ref[...] = run


def _merge(x2, ya, attn, sga, sgb, mw, tm, moe_tile):
    t, d = x2.shape
    sub = moe_tile // tm
    ne = mw['wr_hi'].shape[0]

    def tok(i):
        return (i, 0)

    def cst(i):
        return (0, 0)

    consts = [mw['wup'], mw['wout'], mw['g2'], mw['wr_hi'], mw['wr_lo'], mw['br']]
    return pl.pallas_call(
        functools.partial(_merge_kernel, tm=tm, sub=sub),
        grid=(t // tm,),
        in_specs=[
            pl.BlockSpec((tm, d), tok),
            pl.BlockSpec((tm, d), tok),
            pl.BlockSpec((tm, attn.shape[-1]), tok),
            pl.BlockSpec((tm, d), tok),
            pl.BlockSpec((tm, d), tok),
        ] + [pl.BlockSpec(a.shape, cst) for a in consts],
        out_specs=(
            pl.BlockSpec((tm, d), tok),
            pl.BlockSpec((tm, d), tok),
            pl.BlockSpec((ne, tm), lambda i: (0, i)),
            pl.BlockSpec((ne, tm), lambda i: (0, i)),
            pl.BlockSpec((None, ne, LANES), lambda i: (i // sub, 0, 0)),
        ),
        out_shape=(
            jax.ShapeDtypeStruct((t, d), F32),
            jax.ShapeDtypeStruct((t, d), BF16),
            jax.ShapeDtypeStruct((ne, t), F32),
            jax.ShapeDtypeStruct((ne, t), F32),
            jax.ShapeDtypeStruct((t // moe_tile, ne, LANES), F32),
        ),
        scratch_shapes=[pltpu.VMEM((ne, LANES), F32)],
        compiler_params=_cp(("arbitrary",)),
        name="merge",
    )(x2, ya, attn, sga, sgb, *consts)


def _moe_kernel(cnt_ref, h2_ref, x1_hbm, gt_ref, rt_ref, wg_ref, wu_ref, wd_ref, bg_ref, bu_ref, bd_ref, y_ref,
                pg_ref, og_ref, sem, *, tt, pair):
    j = pl.program_id(0)
    e = pl.program_id(1)
    ne = pl.num_programs(1)
    rb = MOE_ROWS
    slot = e % MOE_GROUP

    @pl.when(e == 0)
    def _():
        cp = pltpu.make_async_copy(x1_hbm.at[pl.ds(pl.multiple_of(j * pair * tt, tt), pair * tt), :], y_ref, sem)
        cp.start()
        cp.wait()

    mine = lax.broadcasted_iota(I32, (SUBLANES, pair * tt), 0) == e % SUBLANES
    g_all = jnp.sum(jnp.where(mine, gt_ref[...], 0.0), axis=0, keepdims=True)
    r_all = jnp.sum(jnp.where(mine, rt_ref[...], 0.0), axis=0, keepdims=True)
    rid = lax.broadcasted_iota(I32, (rb, tt), 0).astype(F32)

    def one_hot(s, blk):
        return jnp.broadcast_to(r_all[:, s * tt:(s + 1) * tt], (rb, tt)) == (rid + (blk * rb).astype(F32))

    def gather(s, hit):
        p = jnp.where(hit, 1.0, 0.0).astype(BF16)
        return p, _dot(p, h2_ref[s * tt:(s + 1) * tt, :]).astype(BF16)

    def expert(xg):
        a = jnp.minimum(_dot(xg, wg_ref[0]) + bg_ref[0], SWIGLU_LIMIT)
        b = jnp.clip(_dot(xg, wu_ref[0]) + bu_ref[0], -SWIGLU_LIMIT, SWIGLU_LIMIT)
        hid = a * jax.nn.sigmoid(SWIGLU_ALPHA * a) * (b + 1.0)
        return _dot(hid.astype(BF16), wd_ref[0]) + bd_ref[0]

    def gated(s, hit, o):
        g_row = jnp.broadcast_to(g_all[:, s * tt:(s + 1) * tt], (rb, tt))
        return (o * jnp.sum(jnp.where(hit, g_row, 0.0), axis=1, keepdims=True)).astype(BF16)

    hits = [one_hot(s, jnp.int32(0)) for s in range(pair)]
    gathered = [gather(s, hits[s]) for s in range(pair)]
    o = expert(jnp.concatenate([xg for _, xg in gathered], axis=0))
    r0 = pl.multiple_of(slot * rb, rb)
    for s in range(pair):
        pg_ref[s, pl.ds(r0, rb), :] = gathered[s][0]
        og_ref[s, pl.ds(r0, rb), :] = gated(s, hits[s], o[s * rb:(s + 1) * rb])

    @pl.when(slot == MOE_GROUP - 1)
    def _():
        for s in range(pair):
            y_ref[s * tt:(s + 1) * tt, :] += _dot_tn(pg_ref[s], og_ref[s])

    for s in range(pair):
        def overflow(blk, c, s=s):
            hit = one_hot(s, blk)
            p, xg = gather(s, hit)
            y_ref[s * tt:(s + 1) * tt, :] += _dot_tn(p, gated(s, hit, expert(xg)))
            return c

        n_rows = cnt_ref[(j * pair + s) * ne + e]
        lax.fori_loop(1, (n_rows + rb - 1) // rb, overflow, 0)


def _moe(h2, x1, gt, rt, cnt, ew, tt):
    t, d = h2.shape
    ne = gt.shape[0]
    nt = t // tt
    f = ew['wg'].shape[-1]
    pair = MOE_PAIR if nt % MOE_PAIR == 0 else 1
    grid_spec = pltpu.PrefetchScalarGridSpec(
        num_scalar_prefetch=1,
        grid=(nt // pair, ne),
        in_specs=[
            pl.BlockSpec((pair * tt, d), lambda j, e, c: (j, 0)),
            pl.BlockSpec(memory_space=pl.ANY),
            pl.BlockSpec((SUBLANES, pair * tt), lambda j, e, c: (e // SUBLANES, j)),
            pl.BlockSpec((SUBLANES, pair * tt), lambda j, e, c: (e // SUBLANES, j)),
            pl.BlockSpec((1, d, f), lambda j, e, c: (e, 0, 0)),
            pl.BlockSpec((1, d, f), lambda j, e, c: (e, 0, 0)),
            pl.BlockSpec((1, f, d), lambda j, e, c: (e, 0, 0)),
            pl.BlockSpec((1, 1, f), lambda j, e, c: (e, 0, 0)),
            pl.BlockSpec((1, 1, f), lambda j, e, c: (e, 0, 0)),
            pl.BlockSpec((1, 1, d), lambda j, e, c: (e, 0, 0)),
        ],
        out_specs=pl.BlockSpec((pair * tt, d), lambda j, e, c: (j, 0)),
        scratch_shapes=[pltpu.VMEM((pair, MOE_GROUP * MOE_ROWS, tt), BF16),
                        pltpu.VMEM((pair, MOE_GROUP * MOE_ROWS, d), BF16),
                        pltpu.SemaphoreType.DMA(())],
    )
    assert ne % MOE_GROUP == 0
    return pl.pallas_call(
        functools.partial(_moe_kernel, tt=tt, pair=pair),
        grid_spec=grid_spec,
        out_shape=jax.ShapeDtypeStruct((t, d), F32),
        compiler_params=_cp(("arbitrary", "arbitrary")),
        name="moe",
    )(cnt, h2, x1, gt, rt, ew['wg'], ew['wu'], ew['wd'], ew['bg'], ew['bu'], ew['bd'])


def _pad_heads(wmat, n_heads, width):
    d = wmat.shape[0]
    w3 = wmat.reshape(d, n_heads, width)
    return jnp.pad(w3, ((0, 0), (0, 0), (0, LANES - width))).reshape(d, n_heads * LANES)


def _pad_lanes(v, width=LANES):
    v = v.reshape(1, -1)
    return jnp.pad(v, ((0, 0), (0, width - v.shape[1])))


def _rel_bucket(rel):
    half = REL_BUCKETS // 2
    max_exact = half // 2
    n = jnp.abs(rel)
    large = max_exact + (jnp.log(jnp.maximum(n, 1).astype(jnp.float32) / max_exact)
                         / math.log(REL_MAX_DIST / max_exact) * (half - max_exact)).astype(jnp.int32)
    large = jnp.minimum(large, half - 1)
    return jnp.where(rel > 0, half, 0) + jnp.where(n < max_exact, n, large)


def _bias_tiles(rel_bias):
    tk = KEY_TILE
    half = REL_BUCKETS // 2
    max_exact = half // 2
    n_sat = int(math.ceil(max_exact * (REL_MAX_DIST / max_exact) ** ((half - 1 - max_exact) / (half - max_exact)))) + 2
    nd = (n_sat + 2 * tk - 2) // tk + 1
    dd = jnp.arange(nd, dtype=I32)[:, None, None]
    c = jnp.arange(tk, dtype=I32)[None, :, None]
    r = jnp.arange(tk, dtype=I32)[None, None, :]
    bucket = _rel_bucket(c - r - dd * tk)
    onehot = (bucket[..., None] == jnp.arange(REL_BUCKETS, dtype=I32)).astype(F32)
    tiles = jnp.einsum('dcrb,bh->dhcr', onehot, rel_bias.astype(F32) * LOG2E,
                       precision=lax.Precision.HIGHEST)
    return tiles


def _prep_proj(norm1_g, w_in, q_norm_g, k_norm_g, idx_k_norm_g, idx_k_norm_b, d_model):
    ssm_w = d_model // 2
    attn_w = N_HEADS * HEAD_DIM
    kv = N_KV_HEADS * HEAD_DIM
    sizes = [ssm_w, attn_w, kv, kv, IDX_HEADS * IDX_DIM, IDX_DIM, IDX_HEADS, d_model, d_model]
    pts = np.cumsum(sizes)[:-1].tolist()
    wu, wq, wk, wv, wqi, wki, wwi, wga, wgb = jnp.split(w_in, pts, axis=1)
    bf = lambda a: a.astype(BF16)
    wwit = jnp.pad(wwi.T, ((0, 2 * SUBLANES - IDX_HEADS), (0, 0)))
    return dict(
        g1=norm1_g.reshape(1, -1).astype(F32),
        wu=bf(wu), wq=bf(_pad_heads(wq, N_HEADS, HEAD_DIM)), wk=bf(_pad_heads(wk, N_KV_HEADS, HEAD_DIM)),
        wv=bf(_pad_heads(wv, N_KV_HEADS, HEAD_DIM)), wqi=bf(_pad_heads(wqi, IDX_HEADS, IDX_DIM)),
        wki=bf(_pad_heads(wki, 1, IDX_DIM)), wwit=bf(wwit),
        wga=bf(wga), wgb=bf(wgb),
        gq=_pad_lanes(q_norm_g.astype(F32)), gk=_pad_lanes(k_norm_g.astype(F32)),
        gi=_pad_lanes(idx_k_norm_g.astype(F32)), bi=_pad_lanes(idx_k_norm_b.astype(F32)),
    )


def _prep_s5(lre, lim, log_dt, b_re, b_im, c_re, c_im, dvec, wa, wb):
    g, p = lre.shape
    ch = b_re.shape[-1]
    lam = lax.complex(lre.astype(F32), lim.astype(F32))
    dt = jnp.exp(log_dt.astype(F32))[:, None]
    a_bar = jnp.exp(lam * dt)
    b_bar = ((a_bar - 1.0) / lam)[:, :, None] * lax.complex(b_re.astype(F32), b_im.astype(F32))
    gs = g // S5_DIAG
    eye = jnp.eye(gs, dtype=F32)

    def blocks_in(m):
        return jnp.einsum('jgpc,gh->jgchp', m.reshape(S5_DIAG, gs, p, ch), eye).reshape(S5_DIAG, gs * ch, gs * p)

    def blocks_out(m):
        return jnp.einsum('jgcp,gh->jgphc', m.reshape(S5_DIAG, gs, ch, p), eye).reshape(S5_DIAG, gs * p, gs * ch)

    return dict(
        b_re=blocks_in(jnp.real(b_bar)).astype(BF16), b_im=blocks_in(jnp.imag(b_bar)).astype(BF16),
        c_re=blocks_out(c_re.astype(F32)).astype(BF16), c_im=blocks_out(-c_im.astype(F32)).astype(BF16),
        a_re=jnp.real(a_bar).reshape(1, g * p), a_im=jnp.imag(a_bar).reshape(1, g * p),
        d=dvec.reshape(1, -1).astype(F32), wa=wa.astype(BF16), wb=wb.astype(BF16),
    )


def _prep_merge(w_attn_up, w_out, norm2_g, w_router, b_router):
    d = w_attn_up.shape[1]
    wup = jnp.pad(w_attn_up.reshape(N_HEADS, HEAD_DIM, d), ((0, 0), (0, LANES - HEAD_DIM), (0, 0)))
    wr_t = w_router.astype(F32).T
    wr_hi = wr_t.astype(BF16)
    wr_lo = (wr_t - wr_hi.astype(F32)).astype(BF16)
    return dict(
        wup=wup.reshape(N_HEADS * LANES, d).astype(BF16), wout=w_out.astype(BF16),
        g2=norm2_g.reshape(1, -1).astype(F32), wr_hi=wr_hi, wr_lo=wr_lo,
        br=jnp.broadcast_to(b_router.astype(F32)[:, None], (b_router.shape[0], LANES)),
    )


def _prep_moe(wg, bg, wu, bu, wd, bd):
    return dict(wg=wg.astype(BF16), wu=wu.astype(BF16), wd=wd.astype(BF16),
                bg=bg.astype(F32)[:, None, :], bu=bu.astype(F32)[:, None, :], bd=bd.astype(F32)[:, None, :])


def _pick_tile(n, pref):
    t = min(n, pref)
    while n % t:
        t //= 2
    return t


def _pad_axis(a, axis, size):
    pad = [(0, 0)] * a.ndim
    pad[axis] = (0, size - a.shape[axis])
    return jnp.pad(a, pad)


def _trunk_layer(x, past_k, past_v, past_ik, h0_re, h0_im, pw, sw, mw, ew, bias_tiles):
    bsz, seq, d = x.shape
    t = bsz * seq
    tm = _pick_tile(seq, 512)
    u_tb, q, kp, vp, qi, kip, wt, sga, sgb, kc, vc, kic = _proj(x, pw, bsz, seq, tm)

    half = sw['a_re'].shape[1]
    if h0_re is None:
        h0 = jnp.zeros((bsz, 2 * half), F32)
    else:
        h0 = jnp.concatenate([h0_re.reshape(bsz, half), h0_im.reshape(bsz, half)], axis=1).astype(F32)
    tc = _pick_tile(seq, max(1, S5_ROWS // bsz))
    ya, hout = _s5(u_tb, h0, sw, bsz, seq, tc)
    groups = half // SSM_STATE
    s_re = hout[:, :half].reshape(bsz, groups, SSM_STATE)
    s_im = hout[:, half:].reshape(bsz, groups, SSM_STATE)

    past = 0 if past_k is None else past_k.shape[1]
    n_keys = past + seq
    lk = -(-n_keys // KEY_BLOCK) * KEY_BLOCK
    kip3 = kip.reshape(bsz, seq, LANES)
    if past:
        lane = jnp.arange(LANES)
        pk = jnp.pad(past_k.astype(F32), ((0, 0), (0, 0), (0, 0), (0, LANES - HEAD_DIM)))
        pk = jnp.where(lane == HEAD_DIM, 1.0, pk).astype(BF16)
        pv = jnp.pad(past_v.astype(F32), ((0, 0), (0, 0), (0, 0), (0, LANES - HEAD_DIM)))
        pv = jnp.where(lane == HEAD_DIM, 1.0, pv).astype(BF16)
        pik = jnp.pad(past_ik.astype(F32), ((0, 0), (0, 0), (0, LANES - IDX_DIM))).astype(BF16)
        k_all = jnp.concatenate([pk.transpose(0, 2, 1, 3), kp], axis=2)
        v_all = jnp.concatenate([pv.transpose(0, 2, 1, 3), vp], axis=2)
        ki_all = jnp.concatenate([pik, kip3], axis=1)
    else:
        k_all, v_all, ki_all = kp, vp, kip3
    k_all = _pad_axis(k_all, 2, lk)
    v_all = _pad_axis(v_all, 2, lk)
    ki_all = _pad_axis(ki_all, 1, lk)
    vt_all = v_all.reshape(bsz, N_KV_HEADS, lk // KEY_BLOCK, KEY_BLOCK, LANES).transpose(0, 1, 2, 4, 3)
    tq = KEY_TILE
    seq_q = -(-seq // tq) * tq
    q_p, qi_p, wt_p = _pad_axis(q, 2, seq_q), _pad_axis(qi, 2, seq_q), _pad_axis(wt, 2, seq_q)
    kf = k_all[..., :HEAD_DIM].astype(F32)
    kmax = jnp.sqrt(jnp.max(jnp.sum(kf * kf, axis=-1), axis=-1)).reshape(-1)
    bfar = bias_tiles[-1, :, 0, 0]
    bmax = jnp.max(jnp.abs(bias_tiles), axis=(0, 2, 3))
    stats = jnp.concatenate([kmax, bmax, bfar]).astype(F32)
    bias_tiles = bias_tiles - bfar[None, :, None, None]
    attn = _dsa(stats, q_p, qi_p, wt_p, k_all, vt_all, ki_all, bias_tiles, bsz, seq_q, past, n_keys, tq)
    attn = attn[:, :seq]

    moe_tile = _pick_tile(t, MOE_TILE)
    tm2 = _pick_tile(moe_tile, 512)
    x1, h2, gt, rt, cnt = _merge(x.reshape(t, d), ya, attn.reshape(t, attn.shape[-1]), sga, sgb, mw, tm2, moe_tile)
    cnt_i = cnt[:, :, 0].astype(I32).reshape(-1)
    y = _moe(h2, x1, gt, rt, cnt_i, ew, moe_tile)

    k_new = kc.reshape(bsz, seq, N_KV_HEADS, HEAD_DIM)
    v_new = vc.reshape(bsz, seq, N_KV_HEADS, HEAD_DIM)
    ik_new = kic.reshape(bsz, seq, IDX_DIM)
    return y.reshape(bsz, seq, d), k_new, v_new, ik_new, s_re, s_im


def kernel(x_prompt, x_sample, cache_k, cache_v, cache_idx_k, state_ssm_re, state_ssm_im, rel_bias, norm1_g, w_in, ssm_lambda_re, ssm_lambda_im, ssm_log_dt, ssm_b_re, ssm_b_im, ssm_c_re, ssm_c_im, ssm_d, ssm_w_glu_a, ssm_w_glu_b, q_norm_g, k_norm_g, idx_k_norm_g, idx_k_norm_b, w_attn_up, w_out, norm2_g, moe_w_router, moe_b_router, moe_w_gate, moe_b_gate, moe_w_up, moe_b_up, moe_w_down, moe_b_down):
    depth = w_in.shape[0]
    d_model = x_prompt.shape[-1]
    bias_tiles = _bias_tiles(rel_bias)
    xp, xs = x_prompt, x_sample
    st_p, st_s = [], []
    for l in range(depth):
        pw = _prep_proj(norm1_g[l], w_in[l], q_norm_g[l], k_norm_g[l], idx_k_norm_g[l], idx_k_norm_b[l], d_model)
        sw = _prep_s5(ssm_lambda_re[l], ssm_lambda_im[l], ssm_log_dt[l], ssm_b_re[l], ssm_b_im[l], ssm_c_re[l],
                      ssm_c_im[l], ssm_d[l], ssm_w_glu_a[l], ssm_w_glu_b[l])
        mw = _prep_merge(w_attn_up[l], w_out[l], norm2_g[l], moe_w_router[l], moe_b_router[l])
        ew = _prep_moe(moe_w_gate[l], moe_b_gate[l], moe_w_up[l], moe_b_up[l], moe_w_down[l], moe_b_down[l])
        xp, *sp = _trunk_layer(xp, None, None, None, None, None, pw, sw, mw, ew, bias_tiles)
        xs, *ss = _trunk_layer(xs, cache_k[l], cache_v[l], cache_idx_k[l], state_ssm_re[l], state_ssm_im[l],
                               pw, sw, mw, ew, bias_tiles)
        st_p.append(sp)
        st_s.append(ss)
    outs_p = [jnp.stack([s[i] for s in st_p]) for i in range(5)]
    outs_s = [jnp.stack([s[i] for s in st_s]) for i in range(5)]
    return (xp, xs, *outs_p, *outs_s)
```

```python
import functools
import math

import numpy as np
import jax
import jax.numpy as jnp
from jax import lax
from jax.experimental import pallas as pl
from jax.experimental.pallas import tpu as pltpu

F32 = jnp.float32
BF16 = jnp.bfloat16
I32 = jnp.int32

LANES = 128
SUBLANES = 8
VMEM_LIMIT = 56 * 1024 * 1024

CHUNK = 64
SSM_GROUP_CH = 16
SSM_STATE = 64
N_HEADS = 8
HEAD_DIM = 64
N_KV_HEADS = 2
KV_REP = N_HEADS // N_KV_HEADS
IDX_HEADS = 8
IDX_DIM = 64
TOPK_MAX = 256
REL_BUCKETS = 32
REL_MAX_DIST = 1024
N_EXPERTS = 32
TOP_K = 4
SWIGLU_LIMIT = 7.0
SWIGLU_ALPHA = 1.702
EPS = 1e-6

KEY_TILE = 128
KEY_BLOCK = 256
SCORE_CHUNK = 4
ATTEND_CHUNK = 4
LOG2E = math.log2(math.e)
NEG_BIG = -1e30
SHIFT_LIMIT = 30.0
S5_ROWS = 1024
S5_DIAG = 2
SEARCH_FIRST = 8
SEARCH_GROUP = 4
assert SCORE_CHUNK == 4 and ATTEND_CHUNK == 4
MOE_TILE = 1024
MOE_ROWS = 160
MOE_GROUP = 8
MOE_PAIR = 2


def _cp(sem):
    return pltpu.CompilerParams(dimension_semantics=sem, vmem_limit_bytes=VMEM_LIMIT)


def _dot(a, b):
    return jnp.dot(a, b, preferred_element_type=F32)


def _dot_nt(a, b):
    return lax.dot_general(a, b, (((1,), (1,)), ((), ())), preferred_element_type=F32)


def _dot_tn(a, b):
    return lax.dot_general(a, b, (((0,), (0,)), ((), ())), preferred_element_type=F32)


def _split(a):
    hi = a.astype(BF16)
    lo = (a - hi.astype(F32)).astype(BF16)
    return hi, lo


def _proj_kernel(x_ref, g1_ref, wu_ref, wq_ref, wk_ref, wv_ref, wqi_ref, wki_ref, wwit_ref, wga_ref, wgb_ref,
                 gq_ref, gk_ref, gi_ref, bi_ref,
                 u_ref, q_ref, kp_ref, vp_ref, qi_ref, kip_ref, wt_ref, sga_ref, sgb_ref,
                 kc_ref, vc_ref, kic_ref):
    x = x_ref[...]
    ms = jnp.mean(x * x, axis=-1, keepdims=True)
    hn = (x * lax.rsqrt(ms + EPS) * g1_ref[...]).astype(BF16)
    lane = lax.broadcasted_iota(I32, (x.shape[0], LANES), 1)

    def head_mean(a):
        return jnp.sum(a, axis=-1, keepdims=True) * (1.0 / HEAD_DIM)

    u_ref[...] = _dot(hn, wu_ref[...]).astype(BF16)

    q = _dot(hn, wq_ref[...])
    scale = HEAD_DIM ** -0.5 * LOG2E
    for h in range(N_HEADS):
        qh = q[:, h * LANES:(h + 1) * LANES]
        msq = head_mean(qh * qh)
        q_ref[h] = (qh * lax.rsqrt(msq + EPS) * (gq_ref[...] * scale)).astype(BF16)

    k = _dot(hn, wk_ref[...])
    for g in range(N_KV_HEADS):
        kg = k[:, g * LANES:(g + 1) * LANES]
        msk = head_mean(kg * kg)
        kn = kg * lax.rsqrt(msk + EPS) * gk_ref[...]
        kp_ref[g] = jnp.where(lane == HEAD_DIM, 1.0, kn).astype(BF16)
        kc_ref[:, g, :] = kn[:, :HEAD_DIM]

    v = _dot(hn, wv_ref[...])
    for g in range(N_KV_HEADS):
        vg = v[:, g * LANES:(g + 1) * LANES]
        vp_ref[g] = jnp.where(lane == HEAD_DIM, 1.0, vg).astype(BF16)
        vc_ref[:, g, :] = vg[:, :HEAD_DIM]

    qi = _dot(hn, wqi_ref[...])
    for h in range(IDX_HEADS):
        qi_ref[h] = qi[:, h * LANES:(h + 1) * LANES].astype(BF16)

    ki = _dot(hn, wki_ref[...])
    mu = head_mean(ki)
    xc = jnp.where(lane < IDX_DIM, ki - mu, 0.0)
    var = head_mean(xc * xc)
    kin = xc * lax.rsqrt(var + EPS) * gi_ref[...] + bi_ref[...]
    kip_ref[...] = kin.astype(BF16)
    kic_ref[...] = kin[:, :IDX_DIM]

    wt = _dot_nt(wwit_ref[...], hn)
    wt_ref[...] = wt[0:IDX_HEADS, :] * (IDX_HEADS ** -0.5 * IDX_DIM ** -0.5)

    sga_ref[...] = jax.nn.sigmoid(_dot(hn, wga_ref[...])).astype(BF16)
    sgb_ref[...] = jax.nn.sigmoid(_dot(hn, wgb_ref[...])).astype(BF16)


def _proj(x, pw, bsz, seq, tm):
    d = x.shape[-1]
    nt = seq // tm
    t = bsz * seq
    x2 = x.reshape(t, d)

    def tok(b, i):
        return (b * nt + i, 0)

    def cst(b, i):
        return (0, 0)

    def wspec(a):
        return pl.BlockSpec(a.shape, cst)

    weights = [pw['g1'], pw['wu'], pw['wq'], pw['wk'], pw['wv'], pw['wqi'], pw['wki'], pw['wwit'], pw['wga'],
               pw['wgb'], pw['gq'], pw['gk'], pw['gi'], pw['bi']]
    ssm_w = pw['wu'].shape[1]
    out_shape = (
        jax.ShapeDtypeStruct((t, ssm_w), BF16),
        jax.ShapeDtypeStruct((bsz, N_HEADS, seq, LANES), BF16),
        jax.ShapeDtypeStruct((bsz, N_KV_HEADS, seq, LANES), BF16),
        jax.ShapeDtypeStruct((bsz, N_KV_HEADS, seq, LANES), BF16),
        jax.ShapeDtypeStruct((bsz, IDX_HEADS, seq, LANES), BF16),
        jax.ShapeDtypeStruct((t, LANES), BF16),
        jax.ShapeDtypeStruct((bsz, IDX_HEADS, seq), F32),
        jax.ShapeDtypeStruct((t, d), BF16),
        jax.ShapeDtypeStruct((t, d), BF16),
        jax.ShapeDtypeStruct((t, N_KV_HEADS, HEAD_DIM), F32),
        jax.ShapeDtypeStruct((t, N_KV_HEADS, HEAD_DIM), F32),
        jax.ShapeDtypeStruct((t, IDX_DIM), F32),
    )

    def hm(nh):
        return pl.BlockSpec((None, nh, tm, LANES), lambda b, i: (b, 0, i, 0))

    out_specs = (
        pl.BlockSpec((tm, ssm_w), tok),
        hm(N_HEADS), hm(N_KV_HEADS), hm(N_KV_HEADS), hm(IDX_HEADS),
        pl.BlockSpec((tm, LANES), tok),
        pl.BlockSpec((None, IDX_HEADS, tm), lambda b, i: (b, 0, i)),
        pl.BlockSpec((tm, d), tok), pl.BlockSpec((tm, d), tok),
        pl.BlockSpec((tm, N_KV_HEADS, HEAD_DIM), lambda b, i: (b * nt + i, 0, 0)),
        pl.BlockSpec((tm, N_KV_HEADS, HEAD_DIM), lambda b, i: (b * nt + i, 0, 0)),
        pl.BlockSpec((tm, IDX_DIM), tok),
    )
    return pl.pallas_call(
        _proj_kernel,
        grid=(bsz, nt),
        in_specs=[pl.BlockSpec((tm, d), tok)] + [wspec(a) for a in weights],
        out_specs=out_specs,
        out_shape=out_shape,
        compiler_params=_cp(("arbitrary", "arbitrary")),
        name="proj",
    )(x2, *weights)


def _gelu_tanh(x):
    return 0.5 * x * (1.0 + jnp.tanh(math.sqrt(2.0 / math.pi) * (x + 0.044715 * (x * x * x))))


def _s5_kernel(u_ref, h0_ref, bre_ref, bim_ref, are_ref, aim_ref, cre_ref, cim_ref, dvec_ref, wa_ref, wb_ref,
               ya_ref, hout_ref, state_ref, bu_ref, yf_ref, ug_ref, *, bsz, tc, strip):
    s = pl.program_id(0)
    half = are_ref.shape[1]

    @pl.when(s == 0)
    def _():
        state_ref[...] = h0_ref[...]

    ssm_w = dvec_ref.shape[1]
    for b in range(bsz):
        for c in range(ssm_w // LANES):
            ug_ref[c, pl.ds(b, tc, stride=bsz), :] = u_ref[b, :, c * LANES:(c + 1) * LANES].astype(F32)
    u = jnp.concatenate([ug_ref[c] for c in range(ssm_w // LANES)], axis=1).astype(BF16)
    cw = u.shape[1] // S5_DIAG
    sw = half // S5_DIAG
    for j in range(S5_DIAG):
        uj = u[:, j * cw:(j + 1) * cw]
        bu_ref[:, j * sw:(j + 1) * sw] = _dot(uj, bre_ref[j])
        bu_ref[:, half + j * sw:half + (j + 1) * sw] = _dot(uj, bim_ref[j])

    for c0 in range(0, half, strip):
        ar = jnp.broadcast_to(are_ref[:, c0:c0 + strip], (bsz, strip))
        ai = jnp.broadcast_to(aim_ref[:, c0:c0 + strip], (bsz, strip))
        hr0 = state_ref[:, c0:c0 + strip]
        hi0 = state_ref[:, half + c0:half + c0 + strip]

        def step(t, carry):
            hr, hi = carry
            r0 = pl.multiple_of(t * bsz, bsz)
            br = bu_ref[pl.ds(r0, bsz), c0:c0 + strip]
            bi = bu_ref[pl.ds(r0, bsz), half + c0:half + c0 + strip]
            nr = ar * hr - ai * hi + br
            ni = ar * hi + ai * hr + bi
            bu_ref[pl.ds(r0, bsz), c0:c0 + strip] = nr
            bu_ref[pl.ds(r0, bsz), half + c0:half + c0 + strip] = ni
            return nr, ni

        hr, hi = lax.fori_loop(0, tc, step, (hr0, hi0))
        state_ref[:, c0:c0 + strip] = hr
        state_ref[:, half + c0:half + c0 + strip] = hi

    ys = []
    for j in range(S5_DIAG):
        s_re = bu_ref[:, j * sw:(j + 1) * sw].astype(BF16)
        s_im = bu_ref[:, half + j * sw:half + (j + 1) * sw].astype(BF16)
        ys.append(_dot(s_re, cre_ref[j]) + _dot(s_im, cim_ref[j]))
    y = jnp.concatenate(ys, axis=1) + dvec_ref[...] * u.astype(F32)
    g = _gelu_tanh(y).astype(BF16)
    ya = _dot(g, wa_ref[...]) * jax.nn.sigmoid(_dot(g, wb_ref[...]))
    n_chunk = ya.shape[1] // LANES
    for c in range(n_chunk):
        yf_ref[c] = ya[:, c * LANES:(c + 1) * LANES]
    for b in range(bsz):
        ya_ref[b] = jnp.concatenate([yf_ref[c, pl.ds(b, tc, stride=bsz), :] for c in range(n_chunk)],
                                    axis=1).astype(BF16)

    @pl.when(s == pl.num_programs(0) - 1)
    def _():
        hout_ref[...] = state_ref[...]


def _s5(u, h0, sw, bsz, seq, tc):
    rows = tc * bsz
    ssm_w = sw['d'].shape[1]
    half = sw['a_re'].shape[1]
    two_half = 2 * half
    d = sw['wa'].shape[1]
    u3 = u.reshape(bsz, seq, ssm_w)
    strip = min(512, half)

    consts = [h0, sw['b_re'], sw['b_im'], sw['a_re'], sw['a_im'], sw['c_re'], sw['c_im'], sw['d'], sw['wa'], sw['wb']]

    def cst(s):
        return (0, 0)

    def cspec(a):
        return pl.BlockSpec(a.shape, lambda s: (0,) * a.ndim)

    ya, hout = pl.pallas_call(
        functools.partial(_s5_kernel, bsz=bsz, tc=tc, strip=strip),
        grid=(seq // tc,),
        in_specs=[pl.BlockSpec((bsz, tc, ssm_w), lambda s: (0, s, 0))] + [cspec(a) for a in consts],
        out_specs=(pl.BlockSpec((bsz, tc, d), lambda s: (0, s, 0)), pl.BlockSpec((bsz, two_half), cst)),
        out_shape=(jax.ShapeDtypeStruct((bsz, seq, d), BF16), jax.ShapeDtypeStruct((bsz, two_half), F32)),
        scratch_shapes=[pltpu.VMEM((bsz, two_half), F32), pltpu.VMEM((rows, two_half), F32),
                        pltpu.VMEM((d // LANES, rows, LANES), F32), pltpu.VMEM((ssm_w // LANES, rows, LANES), F32)],
        compiler_params=_cp(("arbitrary",)),
        name="s5",
    )(u3, *consts)
    return ya.reshape(bsz * seq, d), hout


def _f2key(x):
    b = lax.bitcast_convert_type(x, I32)
    return b ^ ((b >> 31) & 0x7FFFFFFF)


def _key2f(k):
    return lax.bitcast_convert_type(k ^ ((k >> 31) & 0x7FFFFFFF), F32)


def _dsa_kernel(st_ref, q_ref, qi_ref, wt_ref, k_ref, vt_ref, ki_ref, bias_ref, o_ref,
                s_ref, lo_ref, hi_ref, clo_ref, glo_ref, ghi_ref, side_ref, q2_ref, mrow_ref, acc_ref,
                *, bsz, tq, past, n_keys, topk, nkt, nd, near_max):
    b_id = pl.program_id(0)
    i = pl.program_id(1)
    kb = KEY_BLOCK
    sl = SUBLANES
    q0 = past + i * tq
    last_chunk = (q0 + tq - 1) // CHUNK
    n_kt = jnp.minimum(nkt, ((last_chunk + 1) * CHUNK + kb - 1) // kb)
    d0 = q0 // KEY_TILE

    krow = lax.broadcasted_iota(I32, (kb, tq), 0)
    q_chunk = (q0 + lax.broadcasted_iota(I32, (kb, tq), 1)) // CHUNK
    qc8 = (q0 + lax.broadcasted_iota(I32, (sl, tq), 1)) // CHUNK
    n_adm = jnp.minimum((qc8 + 1) * CHUNK, n_keys)
    n_admf = n_adm.astype(F32)
    is_pad = q0 + lax.broadcasted_iota(I32, (sl, tq), 1) >= n_keys
    needf = jnp.where(is_pad, n_adm, jnp.minimum(topk, n_adm)).astype(F32)

    def bcast(x):
        return jnp.broadcast_to(x[0:1, :], (kb, tq))

    def rep(x):
        return jnp.broadcast_to(x, (sl, tq))

    qi = qi_ref[...].reshape(IDX_HEADS * tq, LANES)

    def score_blocks(kt, nb, masked):
        k0 = pl.multiple_of(kt * kb, kb)
        s = _dot_nt(ki_ref[pl.ds(k0, nb * kb), :], qi)
        for j in range(nb):
            sc = jnp.zeros((kb, tq), F32)
            for h in range(IDX_HEADS):
                sc = sc + wt_ref[h:h + 1, :] * jnp.maximum(s[j * kb:(j + 1) * kb, h * tq:(h + 1) * tq], 0.0)
            if masked:
                kpos = k0 + j * kb + krow
                adm = ((kpos // CHUNK) <= q_chunk) & (kpos < n_keys)
                sc = jnp.where(adm, sc, -jnp.inf)
            s_ref[kt + j] = sc

    n_open = n_kt - 1

    def score_chunk(j, c):
        score_blocks(SCORE_CHUNK * j, SCORE_CHUNK, False)
        return c

    lax.fori_loop(0, n_open // SCORE_CHUNK, score_chunk, 0)
    rest = n_open % SCORE_CHUNK

    @pl.when(rest >= 2)
    def _():
        score_blocks(n_open - rest, 2, False)

    @pl.when(rest % 2 == 1)
    def _():
        score_blocks(n_open - 1, 1, False)
    score_blocks(n_kt - 1, 1, True)
    s_ref[n_kt] = jnp.full((kb, tq), -jnp.inf, F32)
    n_pair = (n_kt + 1) // 2

    part = 4 * sl

    def fold(x, op):
        x = x.reshape(kb // part, part, tq)
        acc = x[0]
        for j in range(1, kb // part):
            acc = op(acc, x[j])
        return acc

    def count(pred):
        def one(kt):
            return fold(jnp.where(pred(s_ref[kt], kt), 1.0, 0.0), jnp.add)
        c = lax.fori_loop(0, n_pair, lambda j, c: c + (one(2 * j) + one(2 * j + 1)), jnp.zeros((part, tq), F32))
        return rep(jnp.sum(c, axis=0, keepdims=True))

    def minmax(j, c):
        mx, mn = c
        for kt in (2 * j, 2 * j + 1):
            s = s_ref[kt]
            mx = jnp.maximum(mx, fold(s, jnp.maximum))
            mn = jnp.minimum(mn, fold(jnp.where(s == -jnp.inf, jnp.inf, s), jnp.minimum))
        return mx, mn

    mx, mn = lax.fori_loop(0, n_pair, minmax,
                           (jnp.full((part, tq), -jnp.inf, F32), jnp.full((part, tq), jnp.inf, F32)))
    lo_ref[...] = rep(jnp.min(mn, axis=0, keepdims=True))
    hi_ref[...] = _key2f(_f2key(rep(jnp.max(mx, axis=0, keepdims=True))) + 1)
    def odds(cnt):
        c = jnp.clip(cnt, 0.5, n_admf - 0.5)
        return jnp.log((n_admf - c) / c)

    target = odds(needf - 0.5)
    clo_ref[...] = n_admf
    glo_ref[...] = target - odds(n_admf)
    ghi_ref[...] = target - odds(jnp.zeros((sl, tq), F32))
    side_ref[...] = jnp.zeros((sl, tq), F32)

    def searching(lo, hi, clo):
        return (_f2key(hi) > _f2key(lo) + 1) & (clo > needf)

    def refine(it, c):
        lo, hi, clo = lo_ref[...], hi_ref[...], clo_ref[...]
        glo, ghi, side = glo_ref[...], ghi_ref[...], side_ref[...]
        k_t = _f2key(lo + (hi - lo) * (glo / (glo - ghi)))
        t = _key2f(jnp.minimum(jnp.maximum(k_t, _f2key(lo) + 1), _f2key(hi) - 1))
        tb = bcast(t)
        cnt = count(lambda s, kt: s >= tb)
        g = target - odds(cnt)
        open_ = searching(lo, hi, clo)
        up = open_ & (cnt >= needf)
        dn = open_ & (cnt < needf)
        lo_ref[...] = jnp.where(up, t, lo)
        clo_ref[...] = jnp.where(up, cnt, clo)
        hi_ref[...] = jnp.where(dn, t, hi)
        glo_ref[...] = jnp.where(up, g, jnp.where(dn & (side < 0.0), glo * 0.5, glo))
        ghi_ref[...] = jnp.where(dn, g, jnp.where(up & (side > 0.0), ghi * 0.5, ghi))
        side_ref[...] = jnp.where(up, 1.0, jnp.where(dn, -1.0, side))
        return c

    def snap():
        lo, hi, clo = lo_ref[...], hi_ref[...], clo_ref[...]
        lo_b, hi_b = bcast(lo), bcast(hi)

        def body(j, c):
            a, b = c
            for kt in (2 * j, 2 * j + 1):
                s = s_ref[kt]
                a = jnp.minimum(a, fold(jnp.where(s >= lo_b, s, jnp.inf), jnp.minimum))
                b = jnp.maximum(b, fold(jnp.where(s < hi_b, s, -jnp.inf), jnp.maximum))
            return a, b

        a, b = lax.fori_loop(0, n_pair, body,
                             (jnp.full((part, tq), jnp.inf, F32), jnp.full((part, tq), -jnp.inf, F32)))
        open_ = searching(lo, hi, clo)
        lo_ref[...] = jnp.where(open_, rep(jnp.min(a, axis=0, keepdims=True)), lo)
        hi_ref[...] = jnp.where(open_, _key2f(_f2key(rep(jnp.max(b, axis=0, keepdims=True))) + 1), hi)

    def n_searching():
        return jnp.max(jnp.where(searching(lo_ref[...], hi_ref[...], clo_ref[...]), 1.0, 0.0))

    lax.fori_loop(0, SEARCH_FIRST, refine, 0)
    snap()
    lax.fori_loop(0, SEARCH_GROUP, refine, 0)
    snap()

    def group(c):
        lax.fori_loop(0, SEARCH_GROUP, refine, 0)
        snap()
        return n_searching()

    lax.while_loop(lambda c: c > 0.0, group, n_searching())
    thr = lo_ref[...]
    thr_b = bcast(thr)

    n_tied = jnp.max(jnp.where(clo_ref[...] > needf, 1.0, 0.0))

    @pl.when(n_tied > 0.0)
    def _():
        rem_b = bcast(needf - count(lambda s, kt: s > thr_b))
        tri = jnp.where(lax.broadcasted_iota(I32, (kb, kb), 0) >= lax.broadcasted_iota(I32, (kb, kb), 1),
                        1.0, 0.0).astype(BF16)

        def drop(kt, seen):
            s = s_ref[kt]
            tie = s == thr_b
            rank = _dot(tri, jnp.where(tie, 1.0, 0.0).astype(BF16)) + bcast(seen)
            s_ref[kt] = jnp.where(tie & (rank > rem_b), -jnp.inf, s)
            return rep(rank[kb - 1:kb, :])

        lax.fori_loop(0, n_kt, drop, jnp.zeros((sl, tq), F32))

    rows_g = KV_REP * tq
    qf = q_ref[...].reshape(N_HEADS * tq, LANES).astype(F32)
    qn = jnp.sqrt(jnp.sum(qf * qf, axis=1, keepdims=True))
    lane = lax.broadcasted_iota(I32, (tq, LANES), 1)
    worst_rows = jnp.zeros((tq, 1), F32)
    for h in range(N_HEADS):
        kmax = st_ref[b_id * N_KV_HEADS + h // KV_REP]
        bmax = st_ref[bsz * N_KV_HEADS + h]
        bfar = st_ref[bsz * N_KV_HEADS + N_HEADS + h]
        bound = qn[h * tq:(h + 1) * tq, :] * (kmax * 1.01) + (bmax + 0.1)
        worst_rows = jnp.maximum(worst_rows, bound)
        q2_ref[h * tq:(h + 1) * tq, :] = jnp.where(lane == HEAD_DIM, bfar - bound,
                                                   qf[h * tq:(h + 1) * tq, :]).astype(BF16)
    worst = jnp.max(worst_rows)
    n_far = jnp.clip((d0 - nd) // 2 + 1, 0, n_kt)

    def select_masks(kt0, nb):
        return [jnp.where(s_ref[kt0 + j] >= thr_b, 0.0, NEG_BIG) for j in range(nb)]

    def logits(kt0, nb, g, near, exact, masks):
        k0 = pl.multiple_of(kt0 * kb, kb)
        s = _dot_nt(k_ref[g, pl.ds(k0, nb * kb), :], q2_ref[g * rows_g:(g + 1) * rows_g, :])
        out = []
        for j in range(nb):
            kt = kt0 + j
            maskadd = masks[j]
            if near:
                da = jnp.clip(d0 - 2 * kt, 0, nd - 1)
                db = jnp.clip(d0 - 2 * kt - 1, 0, nd - 1)
            parts = []
            for r in range(KV_REP):
                h = g * KV_REP + r
                add = maskadd - mrow_ref[0:1, h * tq:(h + 1) * tq] if exact else maskadd
                if near:
                    add = jnp.concatenate([bias_ref[da, h], bias_ref[db, h]], axis=0) + add
                parts.append(s[j * kb:(j + 1) * kb, r * tq:(r + 1) * tq] + add)
            out.append(jnp.concatenate(parts, axis=1))
        return out

    def over_blocks(fn):
        def far_chunk(j, c):
            fn(ATTEND_CHUNK * j, ATTEND_CHUNK, False)
            return c
        lax.fori_loop(0, n_far // ATTEND_CHUNK, far_chunk, 0)
        rest = n_far % ATTEND_CHUNK

        @pl.when(rest >= 2)
        def _():
            fn(n_far - rest, 2, False)

        @pl.when(rest % 2 == 1)
        def _():
            fn(n_far - 1, 1, False)

        n_near = n_kt - n_far
        for nb in range(1, near_max + 1):
            @pl.when(n_near == nb)
            def _(nb=nb):
                fn(n_far, nb, True)

    def attend(exact):
        acc_ref[...] = jnp.zeros(acc_ref.shape, F32)

        def blocks(kt0, nb, near):
            masks = select_masks(kt0, nb)
            for g in range(N_KV_HEADS):
                p = jnp.concatenate([jnp.exp2(lg).astype(BF16) for lg in logits(kt0, nb, g, near, exact, masks)],
                                    axis=0)
                vt = jnp.concatenate([vt_ref[g, kt0 + j] for j in range(nb)], axis=1)
                acc_ref[g] += _dot(vt, p)
        over_blocks(blocks)

    @pl.when(worst <= SHIFT_LIMIT)
    def _():
        attend(False)

    @pl.when(worst > SHIFT_LIMIT)
    def _():
        mrow_ref[...] = jnp.full(mrow_ref.shape, NEG_BIG, F32)

        def blocks(kt0, nb, near):
            masks = select_masks(kt0, nb)
            for g in range(N_KV_HEADS):
                for lg in logits(kt0, nb, g, near, False, masks):
                    mx = jnp.max(lg, axis=0, keepdims=True)
                    cur = mrow_ref[:, g * rows_g:(g + 1) * rows_g]
                    mrow_ref[:, g * rows_g:(g + 1) * rows_g] = jnp.maximum(cur, jnp.broadcast_to(mx, (sl, rows_g)))
        over_blocks(blocks)
        attend(True)

    for g in range(N_KV_HEADS):
        acc = acc_ref[g]
        og = acc / acc[HEAD_DIM:HEAD_DIM + 1, :]
        for r in range(KV_REP):
            h = g * KV_REP + r
            o_ref[:, h * LANES:(h + 1) * LANES] = og[:, r * tq:(r + 1) * tq].T.astype(BF16)


def _dsa(stats, q, qi, wt, k_all, vt_all, ki_all, bias_tiles, bsz, seq, past, n_keys, tq):
    lk = k_all.shape[2]
    nkt = lk // KEY_BLOCK
    topk = min(TOPK_MAX, n_keys // 4)
    nd = bias_tiles.shape[0]
    nq = seq // tq
    assert past % KEY_TILE == 0 and tq == KEY_TILE
    near_max = 0
    for i in range(nq):
        q0 = past + i * tq
        n_kt = min(nkt, (((q0 + tq - 1) // CHUNK + 1) * CHUNK + KEY_BLOCK - 1) // KEY_BLOCK)
        n_far = min(max((q0 // KEY_TILE - nd) // 2 + 1, 0), n_kt)
        near_max = max(near_max, n_kt - n_far)
    kern = functools.partial(_dsa_kernel, bsz=bsz, tq=tq, past=past, n_keys=n_keys, topk=topk, nkt=nkt, nd=nd,
                             near_max=near_max)
    row_state = pltpu.VMEM((SUBLANES, tq), F32)
    grid_spec = pltpu.PrefetchScalarGridSpec(
        num_scalar_prefetch=1,
        grid=(bsz, nq),
        in_specs=[
            pl.BlockSpec((None, N_HEADS, tq, LANES), lambda b, i, s: (b, 0, i, 0)),
            pl.BlockSpec((None, IDX_HEADS, tq, LANES), lambda b, i, s: (b, 0, i, 0)),
            pl.BlockSpec((None, IDX_HEADS, tq), lambda b, i, s: (b, 0, i)),
            pl.BlockSpec((None, N_KV_HEADS, lk, LANES), lambda b, i, s: (b, 0, 0, 0)),
            pl.BlockSpec((None, N_KV_HEADS, nkt, LANES, KEY_BLOCK), lambda b, i, s: (b, 0, 0, 0, 0)),
            pl.BlockSpec((None, lk, LANES), lambda b, i, s: (b, 0, 0)),
            pl.BlockSpec(bias_tiles.shape, lambda b, i, s: (0, 0, 0, 0)),
        ],
        out_specs=pl.BlockSpec((None, tq, N_HEADS * LANES), lambda b, i, s: (b, i, 0)),
        scratch_shapes=[
            pltpu.VMEM((nkt + 1, KEY_BLOCK, tq), F32),
            row_state, row_state, row_state, row_state, row_state, row_state,
            pltpu.VMEM((N_HEADS * tq, LANES), BF16),
            pltpu.VMEM((SUBLANES, N_HEADS * tq), F32),
            pltpu.VMEM((N_KV_HEADS, LANES, KV_REP * tq), F32),
        ],
    )
    return pl.pallas_call(
        kern,
        grid_spec=grid_spec,
        out_shape=jax.ShapeDtypeStruct((bsz, seq, N_HEADS * LANES), BF16),
        compiler_params=_cp(("arbitrary", "arbitrary")),
        name="dsa",
    )(stats, q, qi, wt, k_all, vt_all, ki_all, bias_tiles)


def _merge_kernel(x_ref, ya_ref, at_ref, sga_ref, sgb_ref, wup_ref, wout_ref, g2_ref, wr_hi_ref, wr_lo_ref, br_ref,
                  x1_ref, h2_ref, gt_ref, rt_ref, cnt_ref, run_ref, *, tm, sub):
    step = pl.program_id(0)

    @pl.when(step % sub == 0)
    def _():
        run_ref[...] = jnp.zeros(run_ref.shape, F32)

    yb = _dot(at_ref[...], wup_ref[...])
    merged = sga_ref[...].astype(F32) * ya_ref[...].astype(F32) + sgb_ref[...].astype(F32) * yb
    x1 = x_ref[...] + _dot(merged.astype(BF16), wout_ref[...])
    x1_ref[...] = x1
    ms = jnp.mean(x1 * x1, axis=-1, keepdims=True)
    h2 = x1 * lax.rsqrt(ms + EPS) * g2_ref[...]
    h2_hi, h2_lo = _split(h2)
    h2_ref[...] = h2_hi

    wr_hi = wr_hi_ref[...]
    logit = (_dot_nt(wr_hi, h2_hi) + _dot_nt(wr_hi, h2_lo) + _dot_nt(wr_lo_ref[...], h2_hi)) + br_ref[:, 0:1]
    ne = logit.shape[0]
    eid = lax.broadcasted_iota(I32, (ne, tm), 0).astype(F32)
    selb = jnp.zeros((ne, tm), F32)
    tops = []
    picks = []
    for _ in range(TOP_K):
        mx = jnp.max(logit, axis=0, keepdims=True)
        pick = jnp.min(jnp.where(logit == mx, eid, float(ne)), axis=0, keepdims=True)
        hit = eid == pick
        selb = jnp.where(hit, 1.0, selb)
        logit = jnp.where(hit, -jnp.inf, logit)
        tops.append(mx)
        picks.append(hit)
    ex = [jnp.exp(t - tops[0]) for t in tops]
    den = ex[0] + ex[1] + ex[2] + ex[3]
    gate = jnp.zeros((ne, tm), F32)
    for hit, e in zip(picks, ex):
        gate = jnp.where(hit, e / den, gate)
    gt_ref[...] = gate

    sel = selb > 0.5
    selb = selb.astype(BF16)
    r_i = lax.broadcasted_iota(I32, (tm, tm), 0)
    c_i = lax.broadcasted_iota(I32, (tm, tm), 1)
    tri = jnp.where(r_i < c_i, 1.0, 0.0).astype(BF16)
    run = run_ref[...]
    rank = _dot(selb, tri) + jnp.broadcast_to(run[:, 0:1], (ne, tm))
    rt_ref[...] = jnp.where(sel, rank, -1.0)
    run = run + _dot(selb, jnp.ones((tm, LANES), BF16))
    run_ref[...] = run
    cnt_ref[...] = run


def _merge(x2, ya, attn, sga, sgb, mw, tm, moe_tile):
    t, d = x2.shape
    sub = moe_tile // tm
    ne = mw['wr_hi'].shape[0]

    def tok(i):
        return (i, 0)

    def cst(i):
        return (0, 0)

    consts = [mw['wup'], mw['wout'], mw['g2'], mw['wr_hi'], mw['wr_lo'], mw['br']]
    return pl.pallas_call(
        functools.partial(_merge_kernel, tm=tm, sub=sub),
        grid=(t // tm,),
        in_specs=[
            pl.BlockSpec((tm, d), tok),
            pl.BlockSpec((tm, d), tok),
            pl.BlockSpec((tm, attn.shape[-1]), tok),
            pl.BlockSpec((tm, d), tok),
            pl.BlockSpec((tm, d), tok),
        ] + [pl.BlockSpec(a.shape, cst) for a in consts],
        out_specs=(
            pl.BlockSpec((tm, d), tok),
            pl.BlockSpec((tm, d), tok),
            pl.BlockSpec((ne, tm), lambda i: (0, i)),
            pl.BlockSpec((ne, tm), lambda i: (0, i)),
            pl.BlockSpec((None, ne, LANES), lambda i: (i // sub, 0, 0)),
        ),
        out_shape=(
            jax.ShapeDtypeStruct((t, d), F32),
            jax.ShapeDtypeStruct((t, d), BF16),
            jax.ShapeDtypeStruct((ne, t), F32),
            jax.ShapeDtypeStruct((ne, t), F32),
            jax.ShapeDtypeStruct((t // moe_tile, ne, LANES), F32),
        ),
        scratch_shapes=[pltpu.VMEM((ne, LANES), F32)],
        compiler_params=_cp(("arbitrary",)),
        name="merge",
    )(x2, ya, attn, sga, sgb, *consts)


def _moe_kernel(cnt_ref, h2_ref, x1_hbm, gt_ref, rt_ref, wg_ref, wu_ref, wd_ref, bg_ref, bu_ref, bd_ref, y_ref,
                pg_ref, og_ref, sem, *, tt, pair):
    j = pl.program_id(0)
    e = pl.program_id(1)
    ne = pl.num_programs(1)
    rb = MOE_ROWS
    slot = e % MOE_GROUP

    @pl.when(e == 0)
    def _():
        cp = pltpu.make_async_copy(x1_hbm.at[pl.ds(pl.multiple_of(j * pair * tt, tt), pair * tt), :], y_ref, sem)
        cp.start()
        cp.wait()

    mine = lax.broadcasted_iota(I32, (SUBLANES, pair * tt), 0) == e % SUBLANES
    g_all = jnp.sum(jnp.where(mine, gt_ref[...], 0.0), axis=0, keepdims=True)
    r_all = jnp.sum(jnp.where(mine, rt_ref[...], 0.0), axis=0, keepdims=True)
    rid = lax.broadcasted_iota(I32, (rb, tt), 0).astype(F32)

    def one_hot(s, blk):
        return jnp.broadcast_to(r_all[:, s * tt:(s + 1) * tt], (rb, tt)) == (rid + (blk * rb).astype(F32))

    def gather(s, hit):
        p = jnp.where(hit, 1.0, 0.0).astype(BF16)
        return p, _dot(p, h2_ref[s * tt:(s + 1) * tt, :]).astype(BF16)

    def expert(xg):
        a = jnp.minimum(_dot(xg, wg_ref[0]) + bg_ref[0], SWIGLU_LIMIT)
        b = jnp.clip(_dot(xg, wu_ref[0]) + bu_ref[0], -SWIGLU_LIMIT, SWIGLU_LIMIT)
        hid = a * jax.nn.sigmoid(SWIGLU_ALPHA * a) * (b + 1.0)
        return _dot(hid.astype(BF16), wd_ref[0]) + bd_ref[0]

    def gated(s, hit, o):
        g_row = jnp.broadcast_to(g_all[:, s * tt:(s + 1) * tt], (rb, tt))
        return (o * jnp.sum(jnp.where(hit, g_row, 0.0), axis=1, keepdims=True)).astype(BF16)

    hits = [one_hot(s, jnp.int32(0)) for s in range(pair)]
    gathered = [gather(s, hits[s]) for s in range(pair)]
    o = expert(jnp.concatenate([xg for _, xg in gathered], axis=0))
    r0 = pl.multiple_of(slot * rb, rb)
    for s in range(pair):
        pg_ref[s, pl.ds(r0, rb), :] = gathered[s][0]
        og_ref[s, pl.ds(r0, rb), :] = gated(s, hits[s], o[s * rb:(s + 1) * rb])

    @pl.when(slot == MOE_GROUP - 1)
    def _():
        for s in range(pair):
            y_ref[s * tt:(s + 1) * tt, :] += _dot_tn(pg_ref[s], og_ref[s])

    for s in range(pair):
        def overflow(blk, c, s=s):
            hit = one_hot(s, blk)
            p, xg = gather(s, hit)
            y_ref[s * tt:(s + 1) * tt, :] += _dot_tn(p, gated(s, hit, expert(xg)))
            return c

        n_rows = cnt_ref[(j * pair + s) * ne + e]
        lax.fori_loop(1, (n_rows + rb - 1) // rb, overflow, 0)


def _moe(h2, x1, gt, rt, cnt, ew, tt):
    t, d = h2.shape
    ne = gt.shape[0]
    nt = t // tt
    f = ew['wg'].shape[-1]
    pair = MOE_PAIR if nt % MOE_PAIR == 0 else 1
    grid_spec = pltpu.PrefetchScalarGridSpec(
        num_scalar_prefetch=1,
        grid=(nt // pair, ne),
        in_specs=[
            pl.BlockSpec((pair * tt, d), lambda j, e, c: (j, 0)),
            pl.BlockSpec(memory_space=pl.ANY),
            pl.BlockSpec((SUBLANES, pair * tt), lambda j, e, c: (e // SUBLANES, j)),
            pl.BlockSpec((SUBLANES, pair * tt), lambda j, e, c: (e // SUBLANES, j)),
            pl.BlockSpec((1, d, f), lambda j, e, c: (e, 0, 0)),
            pl.BlockSpec((1, d, f), lambda j, e, c: (e, 0, 0)),
            pl.BlockSpec((1, f, d), lambda j, e, c: (e, 0, 0)),
            pl.BlockSpec((1, 1, f), lambda j, e, c: (e, 0, 0)),
            pl.BlockSpec((1, 1, f), lambda j, e, c: (e, 0, 0)),
            pl.BlockSpec((1, 1, d), lambda j, e, c: (e, 0, 0)),
        ],
        out_specs=pl.BlockSpec((pair * tt, d), lambda j, e, c: (j, 0)),
        scratch_shapes=[pltpu.VMEM((pair, MOE_GROUP * MOE_ROWS, tt), BF16),
                        pltpu.VMEM((pair, MOE_GROUP * MOE_ROWS, d), BF16),
                        pltpu.SemaphoreType.DMA(())],
    )
    assert ne % MOE_GROUP == 0
    return pl.pallas_call(
        functools.partial(_moe_kernel, tt=tt, pair=pair),
        grid_spec=grid_spec,
        out_shape=jax.ShapeDtypeStruct((t, d), F32),
        compiler_params=_cp(("arbitrary", "arbitrary")),
        name="moe",
    )(cnt, h2, x1, gt, rt, ew['wg'], ew['wu'], ew['wd'], ew['bg'], ew['bu'], ew['bd'])


def _pad_heads(wmat, n_heads, width):
    d = wmat.shape[0]
    w3 = wmat.reshape(d, n_heads, width)
    return jnp.pad(w3, ((0, 0), (0, 0), (0, LANES - width))).reshape(d, n_heads * LANES)


def _pad_lanes(v, width=LANES):
    v = v.reshape(1, -1)
    return jnp.pad(v, ((0, 0), (0, width - v.shape[1])))


def _rel_bucket(rel):
    half = REL_BUCKETS // 2
    max_exact = half // 2
    n = jnp.abs(rel)
    large = max_exact + (jnp.log(jnp.maximum(n, 1).astype(jnp.float32) / max_exact)
                         / math.log(REL_MAX_DIST / max_exact) * (half - max_exact)).astype(jnp.int32)
    large = jnp.minimum(large, half - 1)
    return jnp.where(rel > 0, half, 0) + jnp.where(n < max_exact, n, large)


def _bias_tiles(rel_bias):
    tk = KEY_TILE
    half = REL_BUCKETS // 2
    max_exact = half // 2
    n_sat = int(math.ceil(max_exact * (REL_MAX_DIST / max_exact) ** ((half - 1 - max_exact) / (half - max_exact)))) + 2
    nd = (n_sat + 2 * tk - 2) // tk + 1
    dd = jnp.arange(nd, dtype=I32)[:, None, None]
    c = jnp.arange(tk, dtype=I32)[None, :, None]
    r = jnp.arange(tk, dtype=I32)[None, None, :]
    bucket = _rel_bucket(c - r - dd * tk)
    onehot = (bucket[..., None] == jnp.arange(REL_BUCKETS, dtype=I32)).astype(F32)
    tiles = jnp.einsum('dcrb,bh->dhcr', onehot, rel_bias.astype(F32) * LOG2E,
                       precision=lax.Precision.HIGHEST)
    return tiles


def _prep_proj(norm1_g, w_in, q_norm_g, k_norm_g, idx_k_norm_g, idx_k_norm_b, d_model):
    ssm_w = d_model // 2
    attn_w = N_HEADS * HEAD_DIM
    kv = N_KV_HEADS * HEAD_DIM
    sizes = [ssm_w, attn_w, kv, kv, IDX_HEADS * IDX_DIM, IDX_DIM, IDX_HEADS, d_model, d_model]
    pts = np.cumsum(sizes)[:-1].tolist()
    wu, wq, wk, wv, wqi, wki, wwi, wga, wgb = jnp.split(w_in, pts, axis=1)
    bf = lambda a: a.astype(BF16)
    wwit = jnp.pad(wwi.T, ((0, 2 * SUBLANES - IDX_HEADS), (0, 0)))
    return dict(
        g1=norm1_g.reshape(1, -1).astype(F32),
        wu=bf(wu), wq=bf(_pad_heads(wq, N_HEADS, HEAD_DIM)), wk=bf(_pad_heads(wk, N_KV_HEADS, HEAD_DIM)),
        wv=bf(_pad_heads(wv, N_KV_HEADS, HEAD_DIM)), wqi=bf(_pad_heads(wqi, IDX_HEADS, IDX_DIM)),
        wki=bf(_pad_heads(wki, 1, IDX_DIM)), wwit=bf(wwit),
        wga=bf(wga), wgb=bf(wgb),
        gq=_pad_lanes(q_norm_g.astype(F32)), gk=_pad_lanes(k_norm_g.astype(F32)),
        gi=_pad_lanes(idx_k_norm_g.astype(F32)), bi=_pad_lanes(idx_k_norm_b.astype(F32)),
    )


def _prep_s5(lre, lim, log_dt, b_re, b_im, c_re, c_im, dvec, wa, wb):
    g, p = lre.shape
    ch = b_re.shape[-1]
    lam = lax.complex(lre.astype(F32), lim.astype(F32))
    dt = jnp.exp(log_dt.astype(F32))[:, None]
    a_bar = jnp.exp(lam * dt)
    b_bar = ((a_bar - 1.0) / lam)[:, :, None] * lax.complex(b_re.astype(F32), b_im.astype(F32))
    gs = g // S5_DIAG
    eye = jnp.eye(gs, dtype=F32)

    def blocks_in(m):
        return jnp.einsum('jgpc,gh->jgchp', m.reshape(S5_DIAG, gs, p, ch), eye).reshape(S5_DIAG, gs * ch, gs * p)

    def blocks_out(m):
        return jnp.einsum('jgcp,gh->jgphc', m.reshape(S5_DIAG, gs, ch, p), eye).reshape(S5_DIAG, gs * p, gs * ch)

    return dict(
        b_re=blocks_in(jnp.real(b_bar)).astype(BF16), b_im=blocks_in(jnp.imag(b_bar)).astype(BF16),
        c_re=blocks_out(c_re.astype(F32)).astype(BF16), c_im=blocks_out(-c_im.astype(F32)).astype(BF16),
        a_re=jnp.real(a_bar).reshape(1, g * p), a_im=jnp.imag(a_bar).reshape(1, g * p),
        d=dvec.reshape(1, -1).astype(F32), wa=wa.astype(BF16), wb=wb.astype(BF16),
    )


def _prep_merge(w_attn_up, w_out, norm2_g, w_router, b_router):
    d = w_attn_up.shape[1]
    wup = jnp.pad(w_attn_up.reshape(N_HEADS, HEAD_DIM, d), ((0, 0), (0, LANES - HEAD_DIM), (0, 0)))
    wr_t = w_router.astype(F32).T
    wr_hi = wr_t.astype(BF16)
    wr_lo = (wr_t - wr_hi.astype(F32)).astype(BF16)
    return dict(
        wup=wup.reshape(N_HEADS * LANES, d).astype(BF16), wout=w_out.astype(BF16),
        g2=norm2_g.reshape(1, -1).astype(F32), wr_hi=wr_hi, wr_lo=wr_lo,
        br=jnp.broadcast_to(b_router.astype(F32)[:, None], (b_router.shape[0], LANES)),
    )


def _prep_moe(wg, bg, wu, bu, wd, bd):
    return dict(wg=wg.astype(BF16), wu=wu.astype(BF16), wd=wd.astype(BF16),
                bg=bg.astype(F32)[:, None, :], bu=bu.astype(F32)[:, None, :], bd=bd.astype(F32)[:, None, :])


def _pick_tile(n, pref):
    t = min(n, pref)
    while n % t:
        t //= 2
    return t


def _pad_axis(a, axis, size):
    pad = [(0, 0)] * a.ndim
    pad[axis] = (0, size - a.shape[axis])
    return jnp.pad(a, pad)


def _trunk_layer(x, past_k, past_v, past_ik, h0_re, h0_im, pw, sw, mw, ew, bias_tiles):
    bsz, seq, d = x.shape
    t = bsz * seq
    tm = _pick_tile(seq, 512)
    u_tb, q, kp, vp, qi, kip, wt, sga, sgb, kc, vc, kic = _proj(x, pw, bsz, seq, tm)

    half = sw['a_re'].shape[1]
    if h0_re is None:
        h0 = jnp.zeros((bsz, 2 * half), F32)
    else:
        h0 = jnp.concatenate([h0_re.reshape(bsz, half), h0_im.reshape(bsz, half)], axis=1).astype(F32)
    tc = _pick_tile(seq, max(1, S5_ROWS // bsz))
    ya, hout = _s5(u_tb, h0, sw, bsz, seq, tc)
    groups = half // SSM_STATE
    s_re = hout[:, :half].reshape(bsz, groups, SSM_STATE)
    s_im = hout[:, half:].reshape(bsz, groups, SSM_STATE)

    past = 0 if past_k is None else past_k.shape[1]
    n_keys = past + seq
    lk = -(-n_keys // KEY_BLOCK) * KEY_BLOCK
    kip3 = kip.reshape(bsz, seq, LANES)
    if past:
        lane = jnp.arange(LANES)
        pk = jnp.pad(past_k.astype(F32), ((0, 0), (0, 0), (0, 0), (0, LANES - HEAD_DIM)))
        pk = jnp.where(lane == HEAD_DIM, 1.0, pk).astype(BF16)
        pv = jnp.pad(past_v.astype(F32), ((0, 0), (0, 0), (0, 0), (0, LANES - HEAD_DIM)))
        pv = jnp.where(lane == HEAD_DIM, 1.0, pv).astype(BF16)
        pik = jnp.pad(past_ik.astype(F32), ((0, 0), (0, 0), (0, LANES - IDX_DIM))).astype(BF16)
        k_all = jnp.concatenate([pk.transpose(0, 2, 1, 3), kp], axis=2)
        v_all = jnp.concatenate([pv.transpose(0, 2, 1, 3), vp], axis=2)
        ki_all = jnp.concatenate([pik, kip3], axis=1)
    else:
        k_all, v_all, ki_all = kp, vp, kip3
    k_all = _pad_axis(k_all, 2, lk)
    v_all = _pad_axis(v_all, 2, lk)
    ki_all = _pad_axis(ki_all, 1, lk)
    vt_all = v_all.reshape(bsz, N_KV_HEADS, lk // KEY_BLOCK, KEY_BLOCK, LANES).transpose(0, 1, 2, 4, 3)
    tq = KEY_TILE
    seq_q = -(-seq // tq) * tq
    q_p, qi_p, wt_p = _pad_axis(q, 2, seq_q), _pad_axis(qi, 2, seq_q), _pad_axis(wt, 2, seq_q)
    kf = k_all[..., :HEAD_DIM].astype(F32)
    kmax = jnp.sqrt(jnp.max(jnp.sum(kf * kf, axis=-1), axis=-1)).reshape(-1)
    bfar = bias_tiles[-1, :, 0, 0]
    bmax = jnp.max(jnp.abs(bias_tiles), axis=(0, 2, 3))
    stats = jnp.concatenate([kmax, bmax, bfar]).astype(F32)
    bias_tiles = bias_tiles - bfar[None, :, None, None]
    attn = _dsa(stats, q_p, qi_p, wt_p, k_all, vt_all, ki_all, bias_tiles, bsz, seq_q, past, n_keys, tq)
    attn = attn[:, :seq]

    moe_tile = _pick_tile(t, MOE_TILE)
    tm2 = _pick_tile(moe_tile, 512)
    x1, h2, gt, rt, cnt = _merge(x.reshape(t, d), ya, attn.reshape(t, attn.shape[-1]), sga, sgb, mw, tm2, moe_tile)
    cnt_i = cnt[:, :, 0].astype(I32).reshape(-1)
    y = _moe(h2, x1, gt, rt, cnt_i, ew, moe_tile)

    k_new = kc.reshape(bsz, seq, N_KV_HEADS, HEAD_DIM)
    v_new = vc.reshape(bsz, seq, N_KV_HEADS, HEAD_DIM)
    ik_new = kic.reshape(bsz, seq, IDX_DIM)
    return y.reshape(bsz, seq, d), k_new, v_new, ik_new, s_re, s_im


def kernel(x_prompt, x_sample, cache_k, cache_v, cache_idx_k, state_ssm_re, state_ssm_im, rel_bias, norm1_g, w_in, ssm_lambda_re, ssm_lambda_im, ssm_log_dt, ssm_b_re, ssm_b_im, ssm_c_re, ssm_c_im, ssm_d, ssm_w_glu_a, ssm_w_glu_b, q_norm_g, k_norm_g, idx_k_norm_g, idx_k_norm_b, w_attn_up, w_out, norm2_g, moe_w_router, moe_b_router, moe_w_gate, moe_b_gate, moe_w_up, moe_b_up, moe_w_down, moe_b_down):
    depth = w_in.shape[0]
    d_model = x_prompt.shape[-1]
    bias_tiles = _bias_tiles(rel_bias)
    xp, xs = x_prompt, x_sample
    st_p, st_s = [], []
    for l in range(depth):
        pw = _prep_proj(norm1_g[l], w_in[l], q_norm_g[l], k_norm_g[l], idx_k_norm_g[l], idx_k_norm_b[l], d_model)
        sw = _prep_s5(ssm_lambda_re[l], ssm_lambda_im[l], ssm_log_dt[l], ssm_b_re[l], ssm_b_im[l], ssm_c_re[l],
                      ssm_c_im[l], ssm_d[l], ssm_w_glu_a[l], ssm_w_glu_b[l])
        mw = _prep_merge(w_attn_up[l], w_out[l], norm2_g[l], moe_w_router[l], moe_b_router[l])
        ew = _prep_moe(moe_w_gate[l], moe_b_gate[l], moe_w_up[l], moe_b_up[l], moe_w_down[l], moe_b_down[l])
        xp, *sp = _trunk_layer(xp, None, None, None, None, None, pw, sw, mw, ew, bias_tiles)
        xs, *ss = _trunk_layer(xs, cache_k[l], cache_v[l], cache_idx_k[l], state_ssm_re[l], state_ssm_im[l],
                               pw, sw, mw, ew, bias_tiles)
        st_p.append(sp)
        st_s.append(ss)
    outs_p = [jnp.stack([s[i] for s in st_p]) for i in range(5)]
    outs_s = [jnp.stack([s[i] for s in st_s]) for i in range(5)]
    return (xp, xs, *outs_p, *outs_s)
```

```python
import functools
import math

import numpy as np
import jax
import jax.numpy as jnp
from jax import lax
from jax.experimental import pallas as pl
from jax.experimental.pallas import tpu as pltpu

F32 = jnp.float32
BF16 = jnp.bfloat16
I32 = jnp.int32

LANES = 128
SUBLANES = 8
VMEM_LIMIT = 56 * 1024 * 1024

CHUNK = 64
SSM_GROUP_CH = 16
SSM_STATE = 64
N_HEADS = 8
HEAD_DIM = 64
N_KV_HEADS = 2
KV_REP = N_HEADS // N_KV_HEADS
IDX_HEADS = 8
IDX_DIM = 64
TOPK_MAX = 256
REL_BUCKETS = 32
REL_MAX_DIST = 1024
N_EXPERTS = 32
TOP_K = 4
SWIGLU_LIMIT = 7.0
SWIGLU_ALPHA = 1.702
EPS = 1e-6

KEY_TILE = 128
KEY_BLOCK = 256
SCORE_CHUNK = 4
ATTEND_CHUNK = 4
LOG2E = math.log2(math.e)
NEG_BIG = -1e30
SHIFT_LIMIT = 30.0
S5_ROWS = 1024
S5_DIAG = 2
SEARCH_FIRST = 8
SEARCH_GROUP = 4
assert SCORE_CHUNK == 4 and ATTEND_CHUNK == 4
MOE_TILE = 1024
MOE_ROWS = 160
MOE_GROUP = 8
MOE_PAIR = 2


def _cp(sem):
    return pltpu.CompilerParams(dimension_semantics=sem, vmem_limit_bytes=VMEM_LIMIT)


def _dot(a, b):
    return jnp.dot(a, b, preferred_element_type=F32)


def _dot_nt(a, b):
    return lax.dot_general(a, b, (((1,), (1,)), ((), ())), preferred_element_type=F32)


def _dot_tn(a, b):
    return lax.dot_general(a, b, (((0,), (0,)), ((), ())), preferred_element_type=F32)


def _split(a):
    hi = a.astype(BF16)
    lo = (a - hi.astype(F32)).astype(BF16)
    return hi, lo


def _proj_kernel(x_ref, g1_ref, wu_ref, wq_ref, wk_ref, wv_ref, wqi_ref, wki_ref, wwit_ref, wga_ref, wgb_ref,
                 gq_ref, gk_ref, gi_ref, bi_ref,
                 u_ref, q_ref, kp_ref, vp_ref, qi_ref, kip_ref, wt_ref, sga_ref, sgb_ref,
                 kc_ref, vc_ref, kic_ref):
    x = x_ref[...]
    ms = jnp.mean(x * x, axis=-1, keepdims=True)
    hn = (x * lax.rsqrt(ms + EPS) * g1_ref[...]).astype(BF16)
    lane = lax.broadcasted_iota(I32, (x.shape[0], LANES), 1)

    def head_mean(a):
        return jnp.sum(a, axis=-1, keepdims=True) * (1.0 / HEAD_DIM)

    u_ref[...] = _dot(hn, wu_ref[...]).astype(BF16)

    q = _dot(hn, wq_ref[...])
    scale = HEAD_DIM ** -0.5 * LOG2E
    for h in range(N_HEADS):
        qh = q[:, h * LANES:(h + 1) * LANES]
        msq = head_mean(qh * qh)
        q_ref[h] = (qh * lax.rsqrt(msq + EPS) * (gq_ref[...] * scale)).astype(BF16)

    k = _dot(hn, wk_ref[...])
    for g in range(N_KV_HEADS):
        kg = k[:, g * LANES:(g + 1) * LANES]
        msk = head_mean(kg * kg)
        kn = kg * lax.rsqrt(msk + EPS) * gk_ref[...]
        kp_ref[g] = jnp.where(lane == HEAD_DIM, 1.0, kn).astype(BF16)
        kc_ref[:, g, :] = kn[:, :HEAD_DIM]

    v = _dot(hn, wv_ref[...])
    for g in range(N_KV_HEADS):
        vg = v[:, g * LANES:(g + 1) * LANES]
        vp_ref[g] = jnp.where(lane == HEAD_DIM, 1.0, vg).astype(BF16)
        vc_ref[:, g, :] = vg[:, :HEAD_DIM]

    qi = _dot(hn, wqi_ref[...])
    for h in range(IDX_HEADS):
        qi_ref[h] = qi[:, h * LANES:(h + 1) * LANES].astype(BF16)

    ki = _dot(hn, wki_ref[...])
    mu = head_mean(ki)
    xc = jnp.where(lane < IDX_DIM, ki - mu, 0.0)
    var = head_mean(xc * xc)
    kin = xc * lax.rsqrt(var + EPS) * gi_ref[...] + bi_ref[...]
    kip_ref[...] = kin.astype(BF16)
    kic_ref[...] = kin[:, :IDX_DIM]

    wt = _dot_nt(wwit_ref[...], hn)
    wt_ref[...] = wt[0:IDX_HEADS, :] * (IDX_HEADS ** -0.5 * IDX_DIM ** -0.5)

    sga_ref[...] = jax.nn.sigmoid(_dot(hn, wga_ref[...])).astype(BF16)
    sgb_ref[...] = jax.nn.sigmoid(_dot(hn, wgb_ref[...])).astype(BF16)


def _proj(x, pw, bsz, seq, tm):
    d = x.shape[-1]
    nt = seq // tm
    t = bsz * seq
    x2 = x.reshape(t, d)

    def tok(b, i):
        return (b * nt + i, 0)

    def cst(b, i):
        return (0, 0)

    def wspec(a):
        return pl.BlockSpec(a.shape, cst)

    weights = [pw['g1'], pw['wu'], pw['wq'], pw['wk'], pw['wv'], pw['wqi'], pw['wki'], pw['wwit'], pw['wga'],
               pw['wgb'], pw['gq'], pw['gk'], pw['gi'], pw['bi']]
    ssm_w = pw['wu'].shape[1]
    out_shape = (
        jax.ShapeDtypeStruct((t, ssm_w), BF16),
        jax.ShapeDtypeStruct((bsz, N_HEADS, seq, LANES), BF16),
        jax.ShapeDtypeStruct((bsz, N_KV_HEADS, seq, LANES), BF16),
        jax.ShapeDtypeStruct((bsz, N_KV_HEADS, seq, LANES), BF16),
        jax.ShapeDtypeStruct((bsz, IDX_HEADS, seq, LANES), BF16),
        jax.ShapeDtypeStruct((t, LANES), BF16),
        jax.ShapeDtypeStruct((bsz, IDX_HEADS, seq), F32),
        jax.ShapeDtypeStruct((t, d), BF16),
        jax.ShapeDtypeStruct((t, d), BF16),
        jax.ShapeDtypeStruct((t, N_KV_HEADS, HEAD_DIM), F32),
        jax.ShapeDtypeStruct((t, N_KV_HEADS, HEAD_DIM), F32),
        jax.ShapeDtypeStruct((t, IDX_DIM), F32),
    )

    def hm(nh):
        return pl.BlockSpec((None, nh, tm, LANES), lambda b, i: (b, 0, i, 0))

    out_specs = (
        pl.BlockSpec((tm, ssm_w), tok),
        hm(N_HEADS), hm(N_KV_HEADS), hm(N_KV_HEADS), hm(IDX_HEADS),
        pl.BlockSpec((tm, LANES), tok),
        pl.BlockSpec((None, IDX_HEADS, tm), lambda b, i: (b, 0, i)),
        pl.BlockSpec((tm, d), tok), pl.BlockSpec((tm, d), tok),
        pl.BlockSpec((tm, N_KV_HEADS, HEAD_DIM), lambda b, i: (b * nt + i, 0, 0)),
        pl.BlockSpec((tm, N_KV_HEADS, HEAD_DIM), lambda b, i: (b * nt + i, 0, 0)),
        pl.BlockSpec((tm, IDX_DIM), tok),
    )
    return pl.pallas_call(
        _proj_kernel,
        grid=(bsz, nt),
        in_specs=[pl.BlockSpec((tm, d), tok)] + [wspec(a) for a in weights],
        out_specs=out_specs,
        out_shape=out_shape,
        compiler_params=_cp(("arbitrary", "arbitrary")),
        name="proj",
    )(x2, *weights)


def _gelu_tanh(x):
    return 0.5 * x * (1.0 + jnp.tanh(math.sqrt(2.0 / math.pi) * (x + 0.044715 * (x * x * x))))


def _s5_kernel(u_ref, h0_ref, bre_ref, bim_ref, are_ref, aim_ref, cre_ref, cim_ref, dvec_ref, wa_ref, wb_ref,
               ya_ref, hout_ref, state_ref, bu_ref, yf_ref, ug_ref, *, bsz, tc, strip):
    s = pl.program_id(0)
    half = are_ref.shape[1]

    @pl.when(s == 0)
    def _():
        state_ref[...] = h0_ref[...]

    ssm_w = dvec_ref.shape[1]
    for b in range(bsz):
        for c in range(ssm_w // LANES):
            ug_ref[c, pl.ds(b, tc, stride=bsz), :] = u_ref[b, :, c * LANES:(c + 1) * LANES].astype(F32)
    u = jnp.concatenate([ug_ref[c] for c in range(ssm_w // LANES)], axis=1).astype(BF16)
    cw = u.shape[1] // S5_DIAG
    sw = half // S5_DIAG
    for j in range(S5_DIAG):
        uj = u[:, j * cw:(j + 1) * cw]
        bu_ref[:, j * sw:(j + 1) * sw] = _dot(uj, bre_ref[j])
        bu_ref[:, half + j * sw:half + (j + 1) * sw] = _dot(uj, bim_ref[j])

    for c0 in range(0, half, strip):
        ar = jnp.broadcast_to(are_ref[:, c0:c0 + strip], (bsz, strip))
        ai = jnp.broadcast_to(aim_ref[:, c0:c0 + strip], (bsz, strip))
        hr0 = state_ref[:, c0:c0 + strip]
        hi0 = state_ref[:, half + c0:half + c0 + strip]

        def step(t, carry):
            hr, hi = carry
            r0 = pl.multiple_of(t * bsz, bsz)
            br = bu_ref[pl.ds(r0, bsz), c0:c0 + strip]
            bi = bu_ref[pl.ds(r0, bsz), half + c0:half + c0 + strip]
            nr = ar * hr - ai * hi + br
            ni = ar * hi + ai * hr + bi
            bu_ref[pl.ds(r0, bsz), c0:c0 + strip] = nr
            bu_ref[pl.ds(r0, bsz), half + c0:half + c0 + strip] = ni
            return nr, ni

        hr, hi = lax.fori_loop(0, tc, step, (hr0, hi0))
        state_ref[:, c0:c0 + strip] = hr
        state_ref[:, half + c0:half + c0 + strip] = hi

    ys = []
    for j in range(S5_DIAG):
        s_re = bu_ref[:, j * sw:(j + 1) * sw].astype(BF16)
        s_im = bu_ref[:, half + j * sw:half + (j + 1) * sw].astype(BF16)
        ys.append(_dot(s_re, cre_ref[j]) + _dot(s_im, cim_ref[j]))
    y = jnp.concatenate(ys, axis=1) + dvec_ref[...] * u.astype(F32)
    g = _gelu_tanh(y).astype(BF16)
    ya = _dot(g, wa_ref[...]) * jax.nn.sigmoid(_dot(g, wb_ref[...]))
    n_chunk = ya.shape[1] // LANES
    for c in range(n_chunk):
        yf_ref[c] = ya[:, c * LANES:(c + 1) * LANES]
    for b in range(bsz):
        ya_ref[b] = jnp.concatenate([yf_ref[c, pl.ds(b, tc, stride=bsz), :] for c in range(n_chunk)],
                                    axis=1).astype(BF16)

    @pl.when(s == pl.num_programs(0) - 1)
    def _():
        hout_ref[...] = state_ref[...]


def _s5(u, h0, sw, bsz, seq, tc):
    rows = tc * bsz
    ssm_w = sw['d'].shape[1]
    half = sw['a_re'].shape[1]
    two_half = 2 * half
    d = sw['wa'].shape[1]
    u3 = u.reshape(bsz, seq, ssm_w)
    strip = min(512, half)

    consts = [h0, sw['b_re'], sw['b_im'], sw['a_re'], sw['a_im'], sw['c_re'], sw['c_im'], sw['d'], sw['wa'], sw['wb']]

    def cst(s):
        return (0, 0)

    def cspec(a):
        return pl.BlockSpec(a.shape, lambda s: (0,) * a.ndim)

    ya, hout = pl.pallas_call(
        functools.partial(_s5_kernel, bsz=bsz, tc=tc, strip=strip),
        grid=(seq // tc,),
        in_specs=[pl.BlockSpec((bsz, tc, ssm_w), lambda s: (0, s, 0))] + [cspec(a) for a in consts],
        out_specs=(pl.BlockSpec((bsz, tc, d), lambda s: (0, s, 0)), pl.BlockSpec((bsz, two_half), cst)),
        out_shape=(jax.ShapeDtypeStruct((bsz, seq, d), BF16), jax.ShapeDtypeStruct((bsz, two_half), F32)),
        scratch_shapes=[pltpu.VMEM((bsz, two_half), F32), pltpu.VMEM((rows, two_half), F32),
                        pltpu.VMEM((d // LANES, rows, LANES), F32), pltpu.VMEM((ssm_w // LANES, rows, LANES), F32)],
        compiler_params=_cp(("arbitrary",)),
        name="s5",
    )(u3, *consts)
    return ya.reshape(bsz * seq, d), hout


def _f2key(x):
    b = lax.bitcast_convert_type(x, I32)
    return b ^ ((b >> 31) & 0x7FFFFFFF)


def _key2f(k):
    return lax.bitcast_convert_type(k ^ ((k >> 31) & 0x7FFFFFFF), F32)


def _dsa_kernel(st_ref, q_ref, qi_ref, wt_ref, k_ref, vt_ref, ki_ref, bias_ref, o_ref,
                s_ref, lo_ref, hi_ref, clo_ref, glo_ref, ghi_ref, side_ref, q2_ref, mrow_ref, acc_ref,
                *, bsz, tq, past, n_keys, topk, nkt, nd, near_max):
    b_id = pl.program_id(0)
    i = pl.program_id(1)
    kb = KEY_BLOCK
    sl = SUBLANES
    q0 = past + i * tq
    last_chunk = (q0 + tq - 1) // CHUNK
    n_kt = jnp.minimum(nkt, ((last_chunk + 1) * CHUNK + kb - 1) // kb)
    d0 = q0 // KEY_TILE

    krow = lax.broadcasted_iota(I32, (kb, tq), 0)
    q_chunk = (q0 + lax.broadcasted_iota(I32, (kb, tq), 1)) // CHUNK
    qc8 = (q0 + lax.broadcasted_iota(I32, (sl, tq), 1)) // CHUNK
    n_adm = jnp.minimum((qc8 + 1) * CHUNK, n_keys)
    n_admf = n_adm.astype(F32)
    is_pad = q0 + lax.broadcasted_iota(I32, (sl, tq), 1) >= n_keys
    needf = jnp.where(is_pad, n_adm, jnp.minimum(topk, n_adm)).astype(F32)

    def bcast(x):
        return jnp.broadcast_to(x[0:1, :], (kb, tq))

    def rep(x):
        return jnp.broadcast_to(x, (sl, tq))

    qi = qi_ref[...].reshape(IDX_HEADS * tq, LANES)

    def score_blocks(kt, nb, masked):
        k0 = pl.multiple_of(kt * kb, kb)
        s = _dot_nt(ki_ref[pl.ds(k0, nb * kb), :], qi)
        for j in range(nb):
            sc = jnp.zeros((kb, tq), F32)
            for h in range(IDX_HEADS):
                sc = sc + wt_ref[h:h + 1, :] * jnp.maximum(s[j * kb:(j + 1) * kb, h * tq:(h + 1) * tq], 0.0)
            if masked:
                kpos = k0 + j * kb + krow
                adm = ((kpos // CHUNK) <= q_chunk) & (kpos < n_keys)
                sc = jnp.where(adm, sc, -jnp.inf)
            s_ref[kt + j] = sc

    n_open = n_kt - 1

    def score_chunk(j, c):
        score_blocks(SCORE_CHUNK * j, SCORE_CHUNK, False)
        return c

    lax.fori_loop(0, n_open // SCORE_CHUNK, score_chunk, 0)
    rest = n_open % SCORE_CHUNK

    @pl.when(rest >= 2)
    def _():
        score_blocks(n_open - rest, 2, False)

    @pl.when(rest % 2 == 1)
    def _():
        score_blocks(n_open - 1, 1, False)
    score_blocks(n_kt - 1, 1, True)
    s_ref[n_kt] = jnp.full((kb, tq), -jnp.inf, F32)
    n_pair = (n_kt + 1) // 2

    part = 4 * sl

    def fold(x, op):
        x = x.reshape(kb // part, part, tq)
        acc = x[0]
        for j in range(1, kb // part):
            acc = op(acc, x[j])
        return acc

    def count(pred):
        def one(kt):
            return fold(jnp.where(pred(s_ref[kt], kt), 1.0, 0.0), jnp.add)
        c = lax.fori_loop(0, n_pair, lambda j, c: c + (one(2 * j) + one(2 * j + 1)), jnp.zeros((part, tq), F32))
        return rep(jnp.sum(c, axis=0, keepdims=True))

    def minmax(j, c):
        mx, mn = c
        for kt in (2 * j, 2 * j + 1):
            s = s_ref[kt]
            mx = jnp.maximum(mx, fold(s, jnp.maximum))
            mn = jnp.minimum(mn, fold(jnp.where(s == -jnp.inf, jnp.inf, s), jnp.minimum))
        return mx, mn

    mx, mn = lax.fori_loop(0, n_pair, minmax,
                           (jnp.full((part, tq), -jnp.inf, F32), jnp.full((part, tq), jnp.inf, F32)))
    lo_ref[...] = rep(jnp.min(mn, axis=0, keepdims=True))
    hi_ref[...] = _key2f(_f2key(rep(jnp.max(mx, axis=0, keepdims=True))) + 1)
    def odds(cnt):
        c = jnp.clip(cnt, 0.5, n_admf - 0.5)
        return jnp.log((n_admf - c) / c)

    target = odds(needf - 0.5)
    clo_ref[...] = n_admf
    glo_ref[...] = target - odds(n_admf)
    ghi_ref[...] = target - odds(jnp.zeros((sl, tq), F32))
    side_ref[...] = jnp.zeros((sl, tq), F32)

    def searching(lo, hi, clo):
        return (_f2key(hi) > _f2key(lo) + 1) & (clo > needf)

    def refine(it, c):
        lo, hi, clo = lo_ref[...], hi_ref[...], clo_ref[...]
        glo, ghi, side = glo_ref[...], ghi_ref[...], side_ref[...]
        k_t = _f2key(lo + (hi - lo) * (glo / (glo - ghi)))
        t = _key2f(jnp.minimum(jnp.maximum(k_t, _f2key(lo) + 1), _f2key(hi) - 1))
        tb = bcast(t)
        cnt = count(lambda s, kt: s >= tb)
        g = target - odds(cnt)
        open_ = searching(lo, hi, clo)
        up = open_ & (cnt >= needf)
        dn = open_ & (cnt < needf)
        lo_ref[...] = jnp.where(up, t, lo)
        clo_ref[...] = jnp.where(up, cnt, clo)
        hi_ref[...] = jnp.where(dn, t, hi)
        glo_ref[...] = jnp.where(up, g, jnp.where(dn & (side < 0.0), glo * 0.5, glo))
        ghi_ref[...] = jnp.where(dn, g, jnp.where(up & (side > 0.0), ghi * 0.5, ghi))
        side_ref[...] = jnp.where(up, 1.0, jnp.where(dn, -1.0, side))
        return c

    def snap():
        lo, hi, clo = lo_ref[...], hi_ref[...], clo_ref[...]
        lo_b, hi_b = bcast(lo), bcast(hi)

        def body(j, c):
            a, b = c
            for kt in (2 * j, 2 * j + 1):
                s = s_ref[kt]
                a = jnp.minimum(a, fold(jnp.where(s >= lo_b, s, jnp.inf), jnp.minimum))
                b = jnp.maximum(b, fold(jnp.where(s < hi_b, s, -jnp.inf), jnp.maximum))
            return a, b

        a, b = lax.fori_loop(0, n_pair, body,
                             (jnp.full((part, tq), jnp.inf, F32), jnp.full((part, tq), -jnp.inf, F32)))
        open_ = searching(lo, hi, clo)
        lo_ref[...] = jnp.where(open_, rep(jnp.min(a, axis=0, keepdims=True)), lo)
        hi_ref[...] = jnp.where(open_, _key2f(_f2key(rep(jnp.max(b, axis=0, keepdims=True))) + 1), hi)

    def n_searching():
        return jnp.max(jnp.where(searching(lo_ref[...], hi_ref[...], clo_ref[...]), 1.0, 0.0))

    lax.fori_loop(0, SEARCH_FIRST, refine, 0)
    snap()
    lax.fori_loop(0, SEARCH_GROUP, refine, 0)
    snap()

    def group(c):
        lax.fori_loop(0, SEARCH_GROUP, refine, 0)
        snap()
        return n_searching()

    lax.while_loop(lambda c: c > 0.0, group, n_searching())
    thr = lo_ref[...]
    thr_b = bcast(thr)

    n_tied = jnp.max(jnp.where(clo_ref[...] > needf, 1.0, 0.0))

    @pl.when(n_tied > 0.0)
    def _():
        rem_b = bcast(needf - count(lambda s, kt: s > thr_b))
        tri = jnp.where(lax.broadcasted_iota(I32, (kb, kb), 0) >= lax.broadcasted_iota(I32, (kb, kb), 1),
                        1.0, 0.0).astype(BF16)

        def drop(kt, seen):
            s = s_ref[kt]
            tie = s == thr_b
            rank = _dot(tri, jnp.where(tie, 1.0, 0.0).astype(BF16)) + bcast(seen)
            s_ref[kt] = jnp.where(tie & (rank > rem_b), -jnp.inf, s)
            return rep(rank[kb - 1:kb, :])

        lax.fori_loop(0, n_kt, drop, jnp.zeros((sl, tq), F32))

    rows_g = KV_REP * tq
    qf = q_ref[...].reshape(N_HEADS * tq, LANES).astype(F32)
    qn = jnp.sqrt(jnp.sum(qf * qf, axis=1, keepdims=True))
    lane = lax.broadcasted_iota(I32, (tq, LANES), 1)
    worst_rows = jnp.zeros((tq, 1), F32)
    for h in range(N_HEADS):
        kmax = st_ref[b_id * N_KV_HEADS + h // KV_REP]
        bmax = st_ref[bsz * N_KV_HEADS + h]
        bfar = st_ref[bsz * N_KV_HEADS + N_HEADS + h]
        bound = qn[h * tq:(h + 1) * tq, :] * (kmax * 1.01) + (bmax + 0.1)
        worst_rows = jnp.maximum(worst_rows, bound)
        q2_ref[h * tq:(h + 1) * tq, :] = jnp.where(lane == HEAD_DIM, bfar - bound,
                                                   qf[h * tq:(h + 1) * tq, :]).astype(BF16)
    worst = jnp.max(worst_rows)
    n_far = jnp.clip((d0 - nd) // 2 + 1, 0, n_kt)

    def select_masks(kt0, nb):
        return [jnp.where(s_ref[kt0 + j] >= thr_b, 0.0, NEG_BIG) for j in range(nb)]

    def logits(kt0, nb, g, near, exact, masks):
        k0 = pl.multiple_of(kt0 * kb, kb)
        s = _dot_nt(k_ref[g, pl.ds(k0, nb * kb), :], q2_ref[g * rows_g:(g + 1) * rows_g, :])
        out = []
        for j in range(nb):
            kt = kt0 + j
            maskadd = masks[j]
            if near:
                da = jnp.clip(d0 - 2 * kt, 0, nd - 1)
                db = jnp.clip(d0 - 2 * kt - 1, 0, nd - 1)
            parts = []
            for r in range(KV_REP):
                h = g * KV_REP + r
                add = maskadd - mrow_ref[0:1, h * tq:(h + 1) * tq] if exact else maskadd
                if near:
                    add = jnp.concatenate([bias_ref[da, h], bias_ref[db, h]], axis=0) + add
                parts.append(s[j * kb:(j + 1) * kb, r * tq:(r + 1) * tq] + add)
            out.append(jnp.concatenate(parts, axis=1))
        return out

    def over_blocks(fn):
        def far_chunk(j, c):
            fn(ATTEND_CHUNK * j, ATTEND_CHUNK, False)
            return c
        lax.fori_loop(0, n_far // ATTEND_CHUNK, far_chunk, 0)
        rest = n_far % ATTEND_CHUNK

        @pl.when(rest >= 2)
        def _():
            fn(n_far - rest, 2, False)

        @pl.when(rest % 2 == 1)
        def _():
            fn(n_far - 1, 1, False)

        n_near = n_kt - n_far
        for nb in range(1, near_max + 1):
            @pl.when(n_near == nb)
            def _(nb=nb):
                fn(n_far, nb, True)

    def attend(exact):
        acc_ref[...] = jnp.zeros(acc_ref.shape, F32)

        def blocks(kt0, nb, near):
            masks = select_masks(kt0, nb)
            for g in range(N_KV_HEADS):
                p = jnp.concatenate([jnp.exp2(lg).astype(BF16) for lg in logits(kt0, nb, g, near, exact, masks)],
                                    axis=0)
                vt = jnp.concatenate([vt_ref[g, kt0 + j] for j in range(nb)], axis=1)
                acc_ref[g] += _dot(vt, p)
        over_blocks(blocks)

    @pl.when(worst <= SHIFT_LIMIT)
    def _():
        attend(False)

    @pl.when(worst > SHIFT_LIMIT)
    def _():
        mrow_ref[...] = jnp.full(mrow_ref.shape, NEG_BIG, F32)

        def blocks(kt0, nb, near):
            masks = select_masks(kt0, nb)
            for g in range(N_KV_HEADS):
                for lg in logits(kt0, nb, g, near, False, masks):
                    mx = jnp.max(lg, axis=0, keepdims=True)
                    cur = mrow_ref[:, g * rows_g:(g + 1) * rows_g]
                    mrow_ref[:, g * rows_g:(g + 1) * rows_g] = jnp.maximum(cur, jnp.broadcast_to(mx, (sl, rows_g)))
        over_blocks(blocks)
        attend(True)

    for g in range(N_KV_HEADS):
        acc = acc_ref[g]
        og = acc / acc[HEAD_DIM:HEAD_DIM + 1, :]
        for r in range(KV_REP):
            h = g * KV_REP + r
            o_ref[:, h * LANES:(h + 1) * LANES] = og[:, r * tq:(r + 1) * tq].T.astype(BF16)


def _dsa(stats, q, qi, wt, k_all, vt_all, ki_all, bias_tiles, bsz, seq, past, n_keys, tq):
    lk = k_all.shape[2]
    nkt = lk // KEY_BLOCK
    topk = min(TOPK_MAX, n_keys // 4)
    nd = bias_tiles.shape[0]
    nq = seq // tq
    assert past % KEY_TILE == 0 and tq == KEY_TILE
    near_max = 0
    for i in range(nq):
        q0 = past + i * tq
        n_kt = min(nkt, (((q0 + tq - 1) // CHUNK + 1) * CHUNK + KEY_BLOCK - 1) // KEY_BLOCK)
        n_far = min(max((q0 // KEY_TILE - nd) // 2 + 1, 0), n_kt)
        near_max = max(near_max, n_kt - n_far)
    kern = functools.partial(_dsa_kernel, bsz=bsz, tq=tq, past=past, n_keys=n_keys, topk=topk, nkt=nkt, nd=nd,
                             near_max=near_max)
    row_state = pltpu.VMEM((SUBLANES, tq), F32)
    grid_spec = pltpu.PrefetchScalarGridSpec(
        num_scalar_prefetch=1,
        grid=(bsz, nq),
        in_specs=[
            pl.BlockSpec((None, N_HEADS, tq, LANES), lambda b, i, s: (b, 0, i, 0)),
            pl.BlockSpec((None, IDX_HEADS, tq, LANES), lambda b, i, s: (b, 0, i, 0)),
            pl.BlockSpec((None, IDX_HEADS, tq), lambda b, i, s: (b, 0, i)),
            pl.BlockSpec((None, N_KV_HEADS, lk, LANES), lambda b, i, s: (b, 0, 0, 0)),
            pl.BlockSpec((None, N_KV_HEADS, nkt, LANES, KEY_BLOCK), lambda b, i, s: (b, 0, 0, 0, 0)),
            pl.BlockSpec((None, lk, LANES), lambda b, i, s: (b, 0, 0)),
            pl.BlockSpec(bias_tiles.shape, lambda b, i, s: (0, 0, 0, 0)),
        ],
        out_specs=pl.BlockSpec((None, tq, N_HEADS * LANES), lambda b, i, s: (b, i, 0)),
        scratch_shapes=[
            pltpu.VMEM((nkt + 1, KEY_BLOCK, tq), F32),
            row_state, row_state, row_state, row_state, row_state, row_state,
            pltpu.VMEM((N_HEADS * tq, LANES), BF16),
            pltpu.VMEM((SUBLANES, N_HEADS * tq), F32),
            pltpu.VMEM((N_KV_HEADS, LANES, KV_REP * tq), F32),
        ],
    )
    return pl.pallas_call(
        kern,
        grid_spec=grid_spec,
        out_shape=jax.ShapeDtypeStruct((bsz, seq, N_HEADS * LANES), BF16),
        compiler_params=_cp(("arbitrary", "arbitrary")),
        name="dsa",
    )(stats, q, qi, wt, k_all, vt_all, ki_all, bias_tiles)


def _merge_kernel(x_ref, ya_ref, at_ref, sga_ref, sgb_ref, wup_ref, wout_ref, g2_ref, wr_hi_ref, wr_lo_ref, br_ref,
                  x1_ref, h2_ref, gt_ref, rt_ref, cnt_ref, run_ref, *, tm, sub):
    step = pl.program_id(0)

    @pl.when(step % sub == 0)
    def _():
        run_ref[...] = jnp.zeros(run_ref.shape, F32)

    yb = _dot(at_ref[...], wup_ref[...])
    merged = sga_ref[...].astype(F32) * ya_ref[...].astype(F32) + sgb_ref[...].astype(F32) * yb
    x1 = x_ref[...] + _dot(merged.astype(BF16), wout_ref[...])
    x1_ref[...] = x1
    ms = jnp.mean(x1 * x1, axis=-1, keepdims=True)
    h2 = x1 * lax.rsqrt(ms + EPS) * g2_ref[...]
    h2_hi, h2_lo = _split(h2)
    h2_ref[...] = h2_hi

    wr_hi = wr_hi_ref[...]
    logit = (_dot_nt(wr_hi, h2_hi) + _dot_nt(wr_hi, h2_lo) + _dot_nt(wr_lo_ref[...], h2_hi)) + br_ref[:, 0:1]
    ne = logit.shape[0]
    eid = lax.broadcasted_iota(I32, (ne, tm), 0).astype(F32)
    selb = jnp.zeros((ne, tm), F32)
    tops = []
    picks = []
    for _ in range(TOP_K):
        mx = jnp.max(logit, axis=0, keepdims=True)
        pick = jnp.min(jnp.where(logit == mx, eid, float(ne)), axis=0, keepdims=True)
        hit = eid == pick
        selb = jnp.where(hit, 1.0, selb)
        logit = jnp.where(hit, -jnp.inf, logit)
        tops.append(mx)
        picks.append(hit)
    ex = [jnp.exp(t - tops[0]) for t in tops]
    den = ex[0] + ex[1] + ex[2] + ex[3]
    gate = jnp.zeros((ne, tm), F32)
    for hit, e in zip(picks, ex):
        gate = jnp.where(hit, e / den, gate)
    gt_ref[...] = gate

    sel = selb > 0.5
    selb = selb.astype(BF16)
    r_i = lax.broadcasted_iota(I32, (tm, tm), 0)
    c_i = lax.broadcasted_iota(I32, (tm, tm), 1)
    tri = jnp.where(r_i < c_i, 1.0, 0.0).astype(BF16)
    run = run_ref[...]
    rank = _dot(selb, tri) + jnp.broadcast_to(run[:, 0:1], (ne, tm))
    rt_ref[...] = jnp.where(sel, rank, -1.0)
    run = run + _dot(selb, jnp.ones((tm, LANES), BF16))
    run_ref[...] = run
    cnt_ref[...] = run


def _merge(x2, ya, attn, sga, sgb, mw, tm, moe_tile):
    t, d = x2.shape
    sub = moe_tile // tm
    ne = mw['wr_hi'].shape[0]

    def tok(i):
        return (i, 0)

    def cst(i):
        return (0, 0)

    consts = [mw['wup'], mw['wout'], mw['g2'], mw['wr_hi'], mw['wr_lo'], mw['br']]
    return pl.pallas_call(
        functools.partial(_merge_kernel, tm=tm, sub=sub),
        grid=(t // tm,),
        in_specs=[
            pl.BlockSpec((tm, d), tok),
            pl.BlockSpec((tm, d), tok),
            pl.BlockSpec((tm, attn.shape[-1]), tok),
            pl.BlockSpec((tm, d), tok),
            pl.BlockSpec((tm, d), tok),
        ] + [pl.BlockSpec(a.shape, cst) for a in consts],
        out_specs=(
            pl.BlockSpec((tm, d), tok),
            pl.BlockSpec((tm, d), tok),
            pl.BlockSpec((ne, tm), lambda i: (0, i)),
            pl.BlockSpec((ne, tm), lambda i: (0, i)),
            pl.BlockSpec((None, ne, LANES), lambda i: (i // sub, 0, 0)),
        ),
        out_shape=(
            jax.ShapeDtypeStruct((t, d), F32),
            jax.ShapeDtypeStruct((t, d), BF16),
            jax.ShapeDtypeStruct((ne, t), F32),
            jax.ShapeDtypeStruct((ne, t), F32),
            jax.ShapeDtypeStruct((t // moe_tile, ne, LANES), F32),
        ),
        scratch_shapes=[pltpu.VMEM((ne, LANES), F32)],
        compiler_params=_cp(("arbitrary",)),
        name="merge",
    )(x2, ya, attn, sga, sgb, *consts)


def _moe_kernel(cnt_ref, h2_ref, x1_hbm, gt_ref, rt_ref, wg_ref, wu_ref, wd_ref, bg_ref, bu_ref, bd_ref, y_ref,
                pg_ref, og_ref, sem, *, tt, pair):
    j = pl.program_id(0)
    e = pl.program_id(1)
    ne = pl.num_programs(1)
    rb = MOE_ROWS
    slot = e % MOE_GROUP

    x1_copy = pltpu.make_async_copy(x1_hbm.at[pl.ds(pl.multiple_of(j * pair * tt, tt), pair * tt), :], y_ref, sem)

    @pl.when(e == 0)
    def _():
        x1_copy.start()

    mine = lax.broadcasted_iota(I32, (SUBLANES, pair * tt), 0) == e % SUBLANES
    g_all = jnp.sum(jnp.where(mine, gt_ref[...], 0.0), axis=0, keepdims=True)
    r_all = jnp.sum(jnp.where(mine, rt_ref[...], 0.0), axis=0, keepdims=True)
    rid = lax.broadcasted_iota(I32, (rb, tt), 0).astype(F32)

    def one_hot(s, blk):
        return jnp.broadcast_to(r_all[:, s * tt:(s + 1) * tt], (rb, tt)) == (rid + (blk * rb).astype(F32))

    def gather(s, hit):
        p = jnp.where(hit, 1.0, 0.0).astype(BF16)
        return p, _dot(p, h2_ref[s * tt:(s + 1) * tt, :]).astype(BF16)

    def expert(xg):
        a = jnp.minimum(_dot(xg, wg_ref[0]) + bg_ref[0], SWIGLU_LIMIT)
        b = jnp.clip(_dot(xg, wu_ref[0]) + bu_ref[0], -SWIGLU_LIMIT, SWIGLU_LIMIT)
        hid = a * jax.nn.sigmoid(SWIGLU_ALPHA * a) * (b + 1.0)
        return _dot(hid.astype(BF16), wd_ref[0]) + bd_ref[0]

    def gated(s, hit, o):
        g_row = jnp.broadcast_to(g_all[:, s * tt:(s + 1) * tt], (rb, tt))
        return (o * jnp.sum(jnp.where(hit, g_row, 0.0), axis=1, keepdims=True)).astype(BF16)

    hits = [one_hot(s, jnp.int32(0)) for s in range(pair)]
    gathered = [gather(s, hits[s]) for s in range(pair)]
    o = expert(jnp.concatenate([xg for _, xg in gathered], axis=0))
    r0 = pl.multiple_of(slot * rb, rb)
    for s in range(pair):
        pg_ref[s, pl.ds(r0, rb), :] = gathered[s][0]
        og_ref[s, pl.ds(r0, rb), :] = gated(s, hits[s], o[s * rb:(s + 1) * rb])

    @pl.when(e == 0)
    def _():
        x1_copy.wait()

    @pl.when(slot == MOE_GROUP - 1)
    def _():
        for s in range(pair):
            y_ref[s * tt:(s + 1) * tt, :] += _dot_tn(pg_ref[s], og_ref[s])

    for s in range(pair):
        def overflow(blk, c, s=s):
            hit = one_hot(s, blk)
            p, xg = gather(s, hit)
            y_ref[s * tt:(s + 1) * tt, :] += _dot_tn(p, gated(s, hit, expert(xg)))
            return c

        n_rows = cnt_ref[(j * pair + s) * ne + e]
        lax.fori_loop(1, (n_rows + rb - 1) // rb, overflow, 0)


def _moe(h2, x1, gt, rt, cnt, ew, tt):
    t, d = h2.shape
    ne = gt.shape[0]
    nt = t // tt
    f = ew['wg'].shape[-1]
    pair = MOE_PAIR if nt % MOE_PAIR == 0 else 1
    grid_spec = pltpu.PrefetchScalarGridSpec(
        num_scalar_prefetch=1,
        grid=(nt // pair, ne),
        in_specs=[
            pl.BlockSpec((pair * tt, d), lambda j, e, c: (j, 0)),
            pl.BlockSpec(memory_space=pl.ANY),
            pl.BlockSpec((SUBLANES, pair * tt), lambda j, e, c: (e // SUBLANES, j)),
            pl.BlockSpec((SUBLANES, pair * tt), lambda j, e, c: (e // SUBLANES, j)),
            pl.BlockSpec((1, d, f), lambda j, e, c: (e, 0, 0)),
            pl.BlockSpec((1, d, f), lambda j, e, c: (e, 0, 0)),
            pl.BlockSpec((1, f, d), lambda j, e, c: (e, 0, 0)),
            pl.BlockSpec((1, 1, f), lambda j, e, c: (e, 0, 0)),
            pl.BlockSpec((1, 1, f), lambda j, e, c: (e, 0, 0)),
            pl.BlockSpec((1, 1, d), lambda j, e, c: (e, 0, 0)),
        ],
        out_specs=pl.BlockSpec((pair * tt, d), lambda j, e, c: (j, 0)),
        scratch_shapes=[pltpu.VMEM((pair, MOE_GROUP * MOE_ROWS, tt), BF16),
                        pltpu.VMEM((pair, MOE_GROUP * MOE_ROWS, d), BF16),
                        pltpu.SemaphoreType.DMA(())],
    )
    assert ne % MOE_GROUP == 0
    return pl.pallas_call(
        functools.partial(_moe_kernel, tt=tt, pair=pair),
        grid_spec=grid_spec,
        out_shape=jax.ShapeDtypeStruct((t, d), F32),
        compiler_params=_cp(("arbitrary", "arbitrary")),
        name="moe",
    )(cnt, h2, x1, gt, rt, ew['wg'], ew['wu'], ew['wd'], ew['bg'], ew['bu'], ew['bd'])


def _pad_heads(wmat, n_heads, width):
    d = wmat.shape[0]
    w3 = wmat.reshape(d, n_heads, width)
    return jnp.pad(w3, ((0, 0), (0, 0), (0, LANES - width))).reshape(d, n_heads * LANES)


def _pad_lanes(v, width=LANES):
    v = v.reshape(1, -1)
    return jnp.pad(v, ((0, 0), (0, width - v.shape[1])))


def _rel_bucket(rel):
    half = REL_BUCKETS // 2
    max_exact = half // 2
    n = jnp.abs(rel)
    large = max_exact + (jnp.log(jnp.maximum(n, 1).astype(jnp.float32) / max_exact)
                         / math.log(REL_MAX_DIST / max_exact) * (half - max_exact)).astype(jnp.int32)
    large = jnp.minimum(large, half - 1)
    return jnp.where(rel > 0, half, 0) + jnp.where(n < max_exact, n, large)


def _bias_tiles(rel_bias):
    tk = KEY_TILE
    half = REL_BUCKETS // 2
    max_exact = half // 2
    n_sat = int(math.ceil(max_exact * (REL_MAX_DIST / max_exact) ** ((half - 1 - max_exact) / (half - max_exact)))) + 2
    nd = (n_sat + 2 * tk - 2) // tk + 1
    dd = jnp.arange(nd, dtype=I32)[:, None, None]
    c = jnp.arange(tk, dtype=I32)[None, :, None]
    r = jnp.arange(tk, dtype=I32)[None, None, :]
    bucket = _rel_bucket(c - r - dd * tk)
    onehot = (bucket[..., None] == jnp.arange(REL_BUCKETS, dtype=I32)).astype(F32)
    tiles = jnp.einsum('dcrb,bh->dhcr', onehot, rel_bias.astype(F32) * LOG2E,
                       precision=lax.Precision.HIGHEST)
    return tiles


def _prep_proj(norm1_g, w_in, q_norm_g, k_norm_g, idx_k_norm_g, idx_k_norm_b, d_model):
    ssm_w = d_model // 2
    attn_w = N_HEADS * HEAD_DIM
    kv = N_KV_HEADS * HEAD_DIM
    sizes = [ssm_w, attn_w, kv, kv, IDX_HEADS * IDX_DIM, IDX_DIM, IDX_HEADS, d_model, d_model]
    pts = np.cumsum(sizes)[:-1].tolist()
    wu, wq, wk, wv, wqi, wki, wwi, wga, wgb = jnp.split(w_in, pts, axis=1)
    bf = lambda a: a.astype(BF16)
    wwit = jnp.pad(wwi.T, ((0, 2 * SUBLANES - IDX_HEADS), (0, 0)))
    return dict(
        g1=norm1_g.reshape(1, -1).astype(F32),
        wu=bf(wu), wq=bf(_pad_heads(wq, N_HEADS, HEAD_DIM)), wk=bf(_pad_heads(wk, N_KV_HEADS, HEAD_DIM)),
        wv=bf(_pad_heads(wv, N_KV_HEADS, HEAD_DIM)), wqi=bf(_pad_heads(wqi, IDX_HEADS, IDX_DIM)),
        wki=bf(_pad_heads(wki, 1, IDX_DIM)), wwit=bf(wwit),
        wga=bf(wga), wgb=bf(wgb),
        gq=_pad_lanes(q_norm_g.astype(F32)), gk=_pad_lanes(k_norm_g.astype(F32)),
        gi=_pad_lanes(idx_k_norm_g.astype(F32)), bi=_pad_lanes(idx_k_norm_b.astype(F32)),
    )


def _prep_s5(lre, lim, log_dt, b_re, b_im, c_re, c_im, dvec, wa, wb):
    g, p = lre.shape
    ch = b_re.shape[-1]
    lam = lax.complex(lre.astype(F32), lim.astype(F32))
    dt = jnp.exp(log_dt.astype(F32))[:, None]
    a_bar = jnp.exp(lam * dt)
    b_bar = ((a_bar - 1.0) / lam)[:, :, None] * lax.complex(b_re.astype(F32), b_im.astype(F32))
    gs = g // S5_DIAG
    eye = jnp.eye(gs, dtype=F32)

    def blocks_in(m):
        return jnp.einsum('jgpc,gh->jgchp', m.reshape(S5_DIAG, gs, p, ch), eye).reshape(S5_DIAG, gs * ch, gs * p)

    def blocks_out(m):
        return jnp.einsum('jgcp,gh->jgphc', m.reshape(S5_DIAG, gs, ch, p), eye).reshape(S5_DIAG, gs * p, gs * ch)

    return dict(
        b_re=blocks_in(jnp.real(b_bar)).astype(BF16), b_im=blocks_in(jnp.imag(b_bar)).astype(BF16),
        c_re=blocks_out(c_re.astype(F32)).astype(BF16), c_im=blocks_out(-c_im.astype(F32)).astype(BF16),
        a_re=jnp.real(a_bar).reshape(1, g * p), a_im=jnp.imag(a_bar).reshape(1, g * p),
        d=dvec.reshape(1, -1).astype(F32), wa=wa.astype(BF16), wb=wb.astype(BF16),
    )


def _prep_merge(w_attn_up, w_out, norm2_g, w_router, b_router):
    d = w_attn_up.shape[1]
    wup = jnp.pad(w_attn_up.reshape(N_HEADS, HEAD_DIM, d), ((0, 0), (0, LANES - HEAD_DIM), (0, 0)))
    wr_t = w_router.astype(F32).T
    wr_hi = wr_t.astype(BF16)
    wr_lo = (wr_t - wr_hi.astype(F32)).astype(BF16)
    return dict(
        wup=wup.reshape(N_HEADS * LANES, d).astype(BF16), wout=w_out.astype(BF16),
        g2=norm2_g.reshape(1, -1).astype(F32), wr_hi=wr_hi, wr_lo=wr_lo,
        br=jnp.broadcast_to(b_router.astype(F32)[:, None], (b_router.shape[0], LANES)),
    )


def _prep_moe(wg, bg, wu, bu, wd, bd):
    return dict(wg=wg.astype(BF16), wu=wu.astype(BF16), wd=wd.astype(BF16),
                bg=bg.astype(F32)[:, None, :], bu=bu.astype(F32)[:, None, :], bd=bd.astype(F32)[:, None, :])


def _pick_tile(n, pref):
    t = min(n, pref)
    while n % t:
        t //= 2
    return t


def _pad_axis(a, axis, size):
    pad = [(0, 0)] * a.ndim
    pad[axis] = (0, size - a.shape[axis])
    return jnp.pad(a, pad)


def _trunk_layer(x, past_k, past_v, past_ik, h0_re, h0_im, pw, sw, mw, ew, bias_tiles):
    bsz, seq, d = x.shape
    t = bsz * seq
    tm = _pick_tile(seq, 512)
    u_tb, q, kp, vp, qi, kip, wt, sga, sgb, kc, vc, kic = _proj(x, pw, bsz, seq, tm)

    half = sw['a_re'].shape[1]
    if h0_re is None:
        h0 = jnp.zeros((bsz, 2 * half), F32)
    else:
        h0 = jnp.concatenate([h0_re.reshape(bsz, half), h0_im.reshape(bsz, half)], axis=1).astype(F32)
    tc = _pick_tile(seq, max(1, S5_ROWS // bsz))
    ya, hout = _s5(u_tb, h0, sw, bsz, seq, tc)
    groups = half // SSM_STATE
    s_re = hout[:, :half].reshape(bsz, groups, SSM_STATE)
    s_im = hout[:, half:].reshape(bsz, groups, SSM_STATE)

    past = 0 if past_k is None else past_k.shape[1]
    n_keys = past + seq
    lk = -(-n_keys // KEY_BLOCK) * KEY_BLOCK
    kip3 = kip.reshape(bsz, seq, LANES)
    if past:
        lane = jnp.arange(LANES)
        pk = jnp.pad(past_k.astype(F32), ((0, 0), (0, 0), (0, 0), (0, LANES - HEAD_DIM)))
        pk = jnp.where(lane == HEAD_DIM, 1.0, pk).astype(BF16)
        pv = jnp.pad(past_v.astype(F32), ((0, 0), (0, 0), (0, 0), (0, LANES - HEAD_DIM)))
        pv = jnp.where(lane == HEAD_DIM, 1.0, pv).astype(BF16)
        pik = jnp.pad(past_ik.astype(F32), ((0, 0), (0, 0), (0, LANES - IDX_DIM))).astype(BF16)
        k_all = jnp.concatenate([pk.transpose(0, 2, 1, 3), kp], axis=2)
        v_all = jnp.concatenate([pv.transpose(0, 2, 1, 3), vp], axis=2)
        ki_all = jnp.concatenate([pik, kip3], axis=1)
    else:
        k_all, v_all, ki_all = kp, vp, kip3
    k_all = _pad_axis(k_all, 2, lk)
    v_all = _pad_axis(v_all, 2, lk)
    ki_all = _pad_axis(ki_all, 1, lk)
    vt_all = v_all.reshape(bsz, N_KV_HEADS, lk // KEY_BLOCK, KEY_BLOCK, LANES).transpose(0, 1, 2, 4, 3)
    tq = KEY_TILE
    seq_q = -(-seq // tq) * tq
    q_p, qi_p, wt_p = _pad_axis(q, 2, seq_q), _pad_axis(qi, 2, seq_q), _pad_axis(wt, 2, seq_q)
    kf = k_all[..., :HEAD_DIM].astype(F32)
    kmax = jnp.sqrt(jnp.max(jnp.sum(kf * kf, axis=-1), axis=-1)).reshape(-1)
    bfar = bias_tiles[-1, :, 0, 0]
    bmax = jnp.max(jnp.abs(bias_tiles), axis=(0, 2, 3))
    stats = jnp.concatenate([kmax, bmax, bfar]).astype(F32)
    bias_tiles = bias_tiles - bfar[None, :, None, None]
    attn = _dsa(stats, q_p, qi_p, wt_p, k_all, vt_all, ki_all, bias_tiles, bsz, seq_q, past, n_keys, tq)
    attn = attn[:, :seq]

    moe_tile = _pick_tile(t, MOE_TILE)
    tm2 = _pick_tile(moe_tile, 512)
    x1, h2, gt, rt, cnt = _merge(x.reshape(t, d), ya, attn.reshape(t, attn.shape[-1]), sga, sgb, mw, tm2, moe_tile)
    cnt_i = cnt[:, :, 0].astype(I32).reshape(-1)
    y = _moe(h2, x1, gt, rt, cnt_i, ew, moe_tile)

    k_new = kc.reshape(bsz, seq, N_KV_HEADS, HEAD_DIM)
    v_new = vc.reshape(bsz, seq, N_KV_HEADS, HEAD_DIM)
    ik_new = kic.reshape(bsz, seq, IDX_DIM)
    return y.reshape(bsz, seq, d), k_new, v_new, ik_new, s_re, s_im


def kernel(x_prompt, x_sample, cache_k, cache_v, cache_idx_k, state_ssm_re, state_ssm_im, rel_bias, norm1_g, w_in, ssm_lambda_re, ssm_lambda_im, ssm_log_dt, ssm_b_re, ssm_b_im, ssm_c_re, ssm_c_im, ssm_d, ssm_w_glu_a, ssm_w_glu_b, q_norm_g, k_norm_g, idx_k_norm_g, idx_k_norm_b, w_attn_up, w_out, norm2_g, moe_w_router, moe_b_router, moe_w_gate, moe_b_gate, moe_w_up, moe_b_up, moe_w_down, moe_b_down):
    depth = w_in.shape[0]
    d_model = x_prompt.shape[-1]
    bias_tiles = _bias_tiles(rel_bias)
    xp, xs = x_prompt, x_sample
    st_p, st_s = [], []
    for l in range(depth):
        pw = _prep_proj(norm1_g[l], w_in[l], q_norm_g[l], k_norm_g[l], idx_k_norm_g[l], idx_k_norm_b[l], d_model)
        sw = _prep_s5(ssm_lambda_re[l], ssm_lambda_im[l], ssm_log_dt[l], ssm_b_re[l], ssm_b_im[l], ssm_c_re[l],
                      ssm_c_im[l], ssm_d[l], ssm_w_glu_a[l], ssm_w_glu_b[l])
        mw = _prep_merge(w_attn_up[l], w_out[l], norm2_g[l], moe_w_router[l], moe_b_router[l])
        ew = _prep_moe(moe_w_gate[l], moe_b_gate[l], moe_w_up[l], moe_b_up[l], moe_w_down[l], moe_b_down[l])
        xp, *sp = _trunk_layer(xp, None, None, None, None, None, pw, sw, mw, ew, bias_tiles)
        xs, *ss = _trunk_layer(xs, cache_k[l], cache_v[l], cache_idx_k[l], state_ssm_re[l], state_ssm_im[l],
                               pw, sw, mw, ew, bias_tiles)
        st_p.append(sp)
        st_s.append(ss)
    outs_p = [jnp.stack([s[i] for s in st_p]) for i in range(5)]
    outs_s = [jnp.stack([s[i] for s in st_s]) for i in range(5)]
    return (xp, xs, *outs_p, *outs_s)
```
